```python
import jax, jax.numpy as jnp
from jax import lax
import numpy as np

D_MODEL = 1024
BATCH = 32
SEQ = 2048
DEPTH = 1

CHUNK = 64
Q_BLOCK = 128
N_HEADS = 8
QK_NOPE_DIM = 64
QK_ROPE_DIM = 32
QK_HEAD_DIM = QK_NOPE_DIM + QK_ROPE_DIM
V_HEAD_DIM = 64
Q_LORA_RANK = 256
KV_LORA_RANK = 128
MLA_WIDTH = N_HEADS * V_HEAD_DIM
CONV_CH = 512
CONV_WIDTH = 31
D_FF = 4 * D_MODEL
N_BRANCH = 2
ADA_CHUNKS = 6
ROPE_THETA = 10000.0
EPS = 1e-6

OFF_Q = Q_LORA_RANK
OFF_KV = OFF_Q + KV_LORA_RANK
OFF_KR = OFF_KV + QK_ROPE_DIM
OFF_GLU = OFF_KR + 2 * CONV_CH
D_IN = OFF_GLU + N_BRANCH * D_MODEL

kernel_name = "hybrid_mla_conformer_conv_adaln_block"


def rms_norm(x, g):
    xf = x.astype(jnp.float32)
    y = xf * lax.rsqrt(jnp.mean(jnp.square(xf), axis=-1, keepdims=True) + EPS)
    return (y * g.astype(jnp.float32)).astype(x.dtype)


def layer_norm(x, g, b):
    xf = x.astype(jnp.float32)
    mu = jnp.mean(xf, axis=-1, keepdims=True)
    var = jnp.mean(jnp.square(xf - mu), axis=-1, keepdims=True)
    y = (xf - mu) * lax.rsqrt(var + EPS)
    return (y * g.astype(jnp.float32) + b.astype(jnp.float32)).astype(x.dtype)


def rope_tables(seq, dtype):
    inv_freq = ROPE_THETA ** (-jnp.arange(0, QK_ROPE_DIM, 2, dtype=jnp.float32) / QK_ROPE_DIM)
    ang = jnp.arange(seq, dtype=jnp.float32)[:, None] * inv_freq[None, :]
    return jnp.cos(ang)[:, None, :].astype(dtype), jnp.sin(ang)[:, None, :].astype(dtype)


def apply_rope(x, cos, sin):
    half = x.shape[-1] // 2
    x1, x2 = x[..., :half], x[..., half:]
    return jnp.concatenate([x1 * cos - x2 * sin, x2 * cos + x1 * sin], axis=-1)


def chunk_causal_attention(q, k, v):
    seq = q.shape[1]
    scale = QK_HEAD_DIM ** -0.5
    chunk_id = jnp.arange(seq) // CHUNK
    outs = []
    for q0 in range(0, seq, Q_BLOCK):
        kv_end = q0 + Q_BLOCK
        qb = q[:, q0:kv_end]
        kb = k[:, :kv_end]
        vb = v[:, :kv_end]
        s = jnp.einsum('bqhd,bkhd->bhqk', qb, kb).astype(jnp.float32) * scale
        mask = chunk_id[q0:kv_end][:, None] >= chunk_id[:kv_end][None, :]
        s = jnp.where(mask[None, None], s, jnp.finfo(jnp.float32).min)
        p = jax.nn.softmax(s, axis=-1).astype(v.dtype)
        outs.append(jnp.einsum('bhqk,bkhd->bqhd', p, vb))
    return jnp.concatenate(outs, axis=1)


def causal_depthwise_conv(u, w, b):
    out = lax.conv_general_dilated(
        u, w[:, None, :].astype(u.dtype), window_strides=(1,),
        padding=[(CONV_WIDTH - 1, 0)],
        dimension_numbers=('NWC', 'WIO', 'NWC'),
        feature_group_count=u.shape[-1])
    return out + b


def _fwd_setup_inputs(seed: int = 0) -> dict:
    key = jax.random.key(seed)
    ks = jax.random.split(key, 24)
    f32 = jnp.float32
    L = DEPTH

    def nrm(k, shape, fan_in):
        return jax.random.normal(k, shape, f32) * (fan_in ** -0.5)

    def gain(k, shape):
        return 1.0 + 0.02 * jax.random.normal(k, shape, f32)

    return {
        "x": jax.random.normal(ks[0], (BATCH, SEQ, D_MODEL), f32),
        "c": jax.random.normal(ks[1], (BATCH, D_MODEL), f32),
        "w_ada": nrm(ks[2], (L, D_MODEL, ADA_CHUNKS * D_MODEL), D_MODEL),
        "b_ada": 0.02 * jax.random.normal(ks[3], (L, ADA_CHUNKS * D_MODEL), f32),
        "norm1_g": gain(ks[4], (L, D_MODEL)),
        "w_in": nrm(ks[5], (L, D_MODEL, D_IN), D_MODEL),
        "q_latent_g": gain(ks[6], (L, Q_LORA_RANK)),
        "w_uq": nrm(ks[7], (L, Q_LORA_RANK, N_HEADS * QK_HEAD_DIM), Q_LORA_RANK),
        "kv_latent_g": gain(ks[8], (L, KV_LORA_RANK)),
        "w_ukv": nrm(ks[9], (L, KV_LORA_RANK, N_HEADS * (QK_NOPE_DIM + V_HEAD_DIM)), KV_LORA_RANK),
        "qk_norm_q_g": gain(ks[10], (L, QK_HEAD_DIM)),
        "qk_norm_k_g": gain(ks[11], (L, QK_HEAD_DIM)),
        "w_o_mla": nrm(ks[12], (L, MLA_WIDTH, D_MODEL), MLA_WIDTH),
        "conv_w": nrm(ks[13], (L, CONV_WIDTH, CONV_CH), CONV_WIDTH),
        "conv_b": 0.02 * jax.random.normal(ks[14], (L, CONV_CH), f32),
        "conv_ln_g": gain(ks[15], (L, CONV_CH)),
        "conv_ln_b": 0.02 * jax.random.normal(ks[16], (L, CONV_CH), f32),
        "w_pw_out": nrm(ks[17], (L, CONV_CH, D_MODEL), CONV_CH),
        "w_out": nrm(ks[18], (L, D_MODEL, D_MODEL), D_MODEL),
        "norm2_g": gain(ks[19], (L, D_MODEL)),
        "w_ff1": nrm(ks[20], (L, D_MODEL, D_FF), D_MODEL),
        "w_ff2": nrm(ks[21], (L, D_FF, D_MODEL), D_FF),
    }


def _fwd_reference(x, c, w_ada, b_ada, norm1_g, w_in, q_latent_g, w_uq, kv_latent_g, w_ukv,
              qk_norm_q_g, qk_norm_k_g, w_o_mla, conv_w, conv_b, conv_ln_g, conv_ln_b,
              w_pw_out, w_out, norm2_g, w_ff1, w_ff2):
    B, S, D = x.shape
    cos, sin = rope_tables(S, x.dtype)
    c_act = jax.nn.silu(c)
    for l in range(DEPTH):
        mod = c_act @ w_ada[l] + b_ada[l]
        shift1, scale1, gate1, shift2, scale2, gate2 = jnp.split(mod[:, None, :], ADA_CHUNKS, axis=-1)

        h = rms_norm(x, norm1_g[l]) * (1.0 + scale1) + shift1
        z = h @ w_in[l]
        z_q = z[..., :OFF_Q]
        z_kv = z[..., OFF_Q:OFF_KV]
        z_kr = z[..., OFF_KV:OFF_KR]
        z_glu = z[..., OFF_KR:OFF_GLU]
        z_gate = z[..., OFF_GLU:]

        q = (rms_norm(z_q, q_latent_g[l]) @ w_uq[l]).reshape(B, S, N_HEADS, QK_HEAD_DIM)
        kv = (rms_norm(z_kv, kv_latent_g[l]) @ w_ukv[l]).reshape(B, S, N_HEADS, QK_NOPE_DIM + V_HEAD_DIM)
        k_nope, v = kv[..., :QK_NOPE_DIM], kv[..., QK_NOPE_DIM:]
        k_rope = jnp.broadcast_to(z_kr[:, :, None, :], (B, S, N_HEADS, QK_ROPE_DIM))
        k = jnp.concatenate([k_nope, k_rope], axis=-1)
        q = rms_norm(q, qk_norm_q_g[l])
        k = rms_norm(k, qk_norm_k_g[l])
        q = jnp.concatenate([q[..., :QK_NOPE_DIM], apply_rope(q[..., QK_NOPE_DIM:], cos, sin)], axis=-1)
        k = jnp.concatenate([k[..., :QK_NOPE_DIM], apply_rope(k[..., QK_NOPE_DIM:], cos, sin)], axis=-1)
        attn = chunk_causal_attention(q, k, v).reshape(B, S, MLA_WIDTH)
        y_a = attn @ w_o_mla[l]

        glu_a, glu_b = jnp.split(z_glu, 2, axis=-1)
        u = glu_a * jax.nn.sigmoid(glu_b)
        u = causal_depthwise_conv(u, conv_w[l], conv_b[l])
        u = jax.nn.silu(layer_norm(u, conv_ln_g[l], conv_ln_b[l]))
        y_b = u @ w_pw_out[l]

        g_a, g_b = jnp.split(jax.nn.sigmoid(z_gate), N_BRANCH, axis=-1)
        mixed = (g_a * y_a + g_b * y_b) @ w_out[l]
        x = x + gate1 * mixed

        h2 = rms_norm(x, norm2_g[l]) * (1.0 + scale2) + shift2
        f = jnp.square(jax.nn.relu(h2 @ w_ff1[l])) @ w_ff2[l]
        x = x + gate2 * f
    return x


import jax as _jax
import jax.numpy as _jnp

TWIN_FORMAT = 'train_step'
FWD_PARAMS = ['x', 'c', 'w_ada', 'b_ada', 'norm1_g', 'w_in', 'q_latent_g', 'w_uq', 'kv_latent_g', 'w_ukv', 'qk_norm_q_g', 'qk_norm_k_g', 'w_o_mla', 'conv_w', 'conv_b', 'conv_ln_g', 'conv_ln_b', 'w_pw_out', 'w_out', 'norm2_g', 'w_ff1', 'w_ff2']
TWIN_WEIGHTS = ['w_ada', 'b_ada', 'norm1_g', 'w_in', 'q_latent_g', 'w_uq', 'kv_latent_g', 'w_ukv', 'qk_norm_q_g', 'qk_norm_k_g', 'w_o_mla', 'conv_w', 'conv_b', 'conv_ln_g', 'conv_ln_b', 'w_pw_out', 'w_out', 'norm2_g', 'w_ff1', 'w_ff2']
TWIN_DIFF_INPUT = 'x'
TWIN_INPUTS = ['x', 'c', 'w_ada', 'b_ada', 'norm1_g', 'w_in', 'q_latent_g', 'w_uq', 'kv_latent_g', 'w_ukv', 'qk_norm_q_g', 'qk_norm_k_g', 'w_o_mla', 'conv_w', 'conv_b', 'conv_ln_g', 'conv_ln_b', 'w_pw_out', 'w_out', 'norm2_g', 'w_ff1', 'w_ff2', 'loss_target', 'm_w_ada', 'm_b_ada', 'm_norm1_g', 'm_w_in', 'm_q_latent_g', 'm_w_uq', 'm_kv_latent_g', 'm_w_ukv', 'm_qk_norm_q_g', 'm_qk_norm_k_g', 'm_w_o_mla', 'm_conv_w', 'm_conv_b', 'm_conv_ln_g', 'm_conv_ln_b', 'm_w_pw_out', 'm_w_out', 'm_norm2_g', 'm_w_ff1', 'm_w_ff2', 'v_w_ada', 'v_b_ada', 'v_norm1_g', 'v_w_in', 'v_q_latent_g', 'v_w_uq', 'v_kv_latent_g', 'v_w_ukv', 'v_qk_norm_q_g', 'v_qk_norm_k_g', 'v_w_o_mla', 'v_conv_w', 'v_conv_b', 'v_conv_ln_g', 'v_conv_ln_b', 'v_w_pw_out', 'v_w_out', 'v_norm2_g', 'v_w_ff1', 'v_w_ff2']
TWIN_OUTPUTS = ['loss', 'grad_x', 'grad_w_ada', 'grad_b_ada', 'grad_norm1_g', 'grad_w_in', 'grad_q_latent_g', 'grad_w_uq', 'grad_kv_latent_g', 'grad_w_ukv', 'grad_qk_norm_q_g', 'grad_qk_norm_k_g', 'grad_w_o_mla', 'grad_conv_w', 'grad_conv_b', 'grad_conv_ln_g', 'grad_conv_ln_b', 'grad_w_pw_out', 'grad_w_out', 'grad_norm2_g', 'grad_w_ff1', 'grad_w_ff2', 'delta_w_ada', 'delta_b_ada', 'delta_norm1_g', 'delta_w_in', 'delta_q_latent_g', 'delta_w_uq', 'delta_kv_latent_g', 'delta_w_ukv', 'delta_qk_norm_q_g', 'delta_qk_norm_k_g', 'delta_w_o_mla', 'delta_conv_w', 'delta_conv_b', 'delta_conv_ln_g', 'delta_conv_ln_b', 'delta_w_pw_out', 'delta_w_out', 'delta_norm2_g', 'delta_w_ff1', 'delta_w_ff2', 'new_m_w_ada', 'new_m_b_ada', 'new_m_norm1_g', 'new_m_w_in', 'new_m_q_latent_g', 'new_m_w_uq', 'new_m_kv_latent_g', 'new_m_w_ukv', 'new_m_qk_norm_q_g', 'new_m_qk_norm_k_g', 'new_m_w_o_mla', 'new_m_conv_w', 'new_m_conv_b', 'new_m_conv_ln_g', 'new_m_conv_ln_b', 'new_m_w_pw_out', 'new_m_w_out', 'new_m_norm2_g', 'new_m_w_ff1', 'new_m_w_ff2', 'new_v_w_ada', 'new_v_b_ada', 'new_v_norm1_g', 'new_v_w_in', 'new_v_q_latent_g', 'new_v_w_uq', 'new_v_kv_latent_g', 'new_v_w_ukv', 'new_v_qk_norm_q_g', 'new_v_qk_norm_k_g', 'new_v_w_o_mla', 'new_v_conv_w', 'new_v_conv_b', 'new_v_conv_ln_g', 'new_v_conv_ln_b', 'new_v_w_pw_out', 'new_v_w_out', 'new_v_norm2_g', 'new_v_w_ff1', 'new_v_w_ff2']
TWIN_LEAF_KINDS = {'loss': 'loss', 'grad_x': 'grad_x', 'grad_w_ada': 'grad_w', 'grad_b_ada': 'grad_w', 'grad_norm1_g': 'grad_w', 'grad_w_in': 'grad_w', 'grad_q_latent_g': 'grad_w', 'grad_w_uq': 'grad_w', 'grad_kv_latent_g': 'grad_w', 'grad_w_ukv': 'grad_w', 'grad_qk_norm_q_g': 'grad_w', 'grad_qk_norm_k_g': 'grad_w', 'grad_w_o_mla': 'grad_w', 'grad_conv_w': 'grad_w', 'grad_conv_b': 'grad_w', 'grad_conv_ln_g': 'grad_w', 'grad_conv_ln_b': 'grad_w', 'grad_w_pw_out': 'grad_w', 'grad_w_out': 'grad_w', 'grad_norm2_g': 'grad_w', 'grad_w_ff1': 'grad_w', 'grad_w_ff2': 'grad_w', 'delta_w_ada': 'delta_w', 'delta_b_ada': 'delta_w', 'delta_norm1_g': 'delta_w', 'delta_w_in': 'delta_w', 'delta_q_latent_g': 'delta_w', 'delta_w_uq': 'delta_w', 'delta_kv_latent_g': 'delta_w', 'delta_w_ukv': 'delta_w', 'delta_qk_norm_q_g': 'delta_w', 'delta_qk_norm_k_g': 'delta_w', 'delta_w_o_mla': 'delta_w', 'delta_conv_w': 'delta_w', 'delta_conv_b': 'delta_w', 'delta_conv_ln_g': 'delta_w', 'delta_conv_ln_b': 'delta_w', 'delta_w_pw_out': 'delta_w', 'delta_w_out': 'delta_w', 'delta_norm2_g': 'delta_w', 'delta_w_ff1': 'delta_w', 'delta_w_ff2': 'delta_w', 'new_m_w_ada': 'new_m', 'new_m_b_ada': 'new_m', 'new_m_norm1_g': 'new_m', 'new_m_w_in': 'new_m', 'new_m_q_latent_g': 'new_m', 'new_m_w_uq': 'new_m', 'new_m_kv_latent_g': 'new_m', 'new_m_w_ukv': 'new_m', 'new_m_qk_norm_q_g': 'new_m', 'new_m_qk_norm_k_g': 'new_m', 'new_m_w_o_mla': 'new_m', 'new_m_conv_w': 'new_m', 'new_m_conv_b': 'new_m', 'new_m_conv_ln_g': 'new_m', 'new_m_conv_ln_b': 'new_m', 'new_m_w_pw_out': 'new_m', 'new_m_w_out': 'new_m', 'new_m_norm2_g': 'new_m', 'new_m_w_ff1': 'new_m', 'new_m_w_ff2': 'new_m', 'new_v_w_ada': 'new_v', 'new_v_b_ada': 'new_v', 'new_v_norm1_g': 'new_v', 'new_v_w_in': 'new_v', 'new_v_q_latent_g': 'new_v', 'new_v_w_uq': 'new_v', 'new_v_kv_latent_g': 'new_v', 'new_v_w_ukv': 'new_v', 'new_v_qk_norm_q_g': 'new_v', 'new_v_qk_norm_k_g': 'new_v', 'new_v_w_o_mla': 'new_v', 'new_v_conv_w': 'new_v', 'new_v_conv_b': 'new_v', 'new_v_conv_ln_g': 'new_v', 'new_v_conv_ln_b': 'new_v', 'new_v_w_pw_out': 'new_v', 'new_v_w_out': 'new_v', 'new_v_norm2_g': 'new_v', 'new_v_w_ff1': 'new_v', 'new_v_w_ff2': 'new_v'}


def _forward(args):
    return _fwd_reference(*[args[k] for k in FWD_PARAMS])


def _output_shape():
    out = _jax.eval_shape(lambda: _forward(_fwd_setup_inputs(0)))
    return out.shape, out.dtype

N_MICROBATCH = 1
ADAM_LR = 0.001
ADAM_B1 = 0.9
ADAM_B2 = 0.999
ADAM_EPS = 1e-08
ADAM_WD = 0.01
ADAM_STEP = 10
PER_EXAMPLE_BATCH_AXIS = {'x': 0, 'c': 0, 'loss_target': 0}
SHARED_INPUTS = []
_WEIGHT_DTYPES = {'w_ada': _jnp.float32, 'b_ada': _jnp.float32, 'norm1_g': _jnp.float32, 'w_in': _jnp.float32, 'q_latent_g': _jnp.float32, 'w_uq': _jnp.float32, 'kv_latent_g': _jnp.float32, 'w_ukv': _jnp.float32, 'qk_norm_q_g': _jnp.float32, 'qk_norm_k_g': _jnp.float32, 'w_o_mla': _jnp.float32, 'conv_w': _jnp.float32, 'conv_b': _jnp.float32, 'conv_ln_g': _jnp.float32, 'conv_ln_b': _jnp.float32, 'w_pw_out': _jnp.float32, 'w_out': _jnp.float32, 'norm2_g': _jnp.float32, 'w_ff1': _jnp.float32, 'w_ff2': _jnp.float32}
MOMENT_SCALE = {'w_ada': 3.682327e+01, 'b_ada': 7.917789e+01, 'norm1_g': 8.105602e-01, 'w_in': 6.995120e+00, 'q_latent_g': 5.314571e-01, 'w_uq': 4.676237e-01, 'kv_latent_g': 3.690075e+01, 'w_ukv': 1.126309e+01, 'qk_norm_q_g': 1.305517e+00, 'qk_norm_k_g': 1.221964e+00, 'w_o_mla': 1.103542e+01, 'conv_w': 6.647315e+00, 'conv_b': 2.269694e+01, 'conv_ln_g': 1.509261e+01, 'conv_ln_b': 1.913063e+01, 'w_pw_out': 7.134951e+00, 'w_out': 1.282600e+01, 'norm2_g': 1.983342e+02, 'w_ff1': 1.481307e+01, 'w_ff2': 3.027657e+01}


def _to_microbatches(a, axis):
    t = _jnp.moveaxis(a, axis, 0)
    t = t.reshape((N_MICROBATCH, t.shape[0] // N_MICROBATCH) + t.shape[1:])
    return _jnp.moveaxis(t, 1, axis + 1)


def setup_inputs(seed: int = 0) -> dict:
    inp = _fwd_setup_inputs(seed)
    key = _jax.random.fold_in(_jax.random.key(seed), 7919)
    shape, _ = _output_shape()
    out = dict(inp)
    out["loss_target"] = _jax.random.normal(_jax.random.fold_in(key, 0), shape, _jnp.float32)
    for i, name in enumerate(TWIN_WEIGHTS):
        w = inp[name].astype(_jnp.float32)
        if MOMENT_SCALE is None:
            s = _jnp.sqrt(_jnp.mean(_jnp.square(w)) + 1e-30)
        else:
            s = MOMENT_SCALE[name]
        km, kv = _jax.random.split(_jax.random.fold_in(key, i + 1))
        out[name] = w
        out["m_" + name] = s * _jax.random.normal(km, w.shape, _jnp.float32)
        out["v_" + name] = (s * s) * _jax.random.uniform(kv, w.shape, _jnp.float32, 0.5, 1.5)
    if N_MICROBATCH > 1:
        for name, axis in PER_EXAMPLE_BATCH_AXIS.items():
            out[name] = _to_microbatches(out[name], axis)
    return {'x': out['x'], 'c': out['c'], 'w_ada': out['w_ada'], 'b_ada': out['b_ada'], 'norm1_g': out['norm1_g'], 'w_in': out['w_in'], 'q_latent_g': out['q_latent_g'], 'w_uq': out['w_uq'], 'kv_latent_g': out['kv_latent_g'], 'w_ukv': out['w_ukv'], 'qk_norm_q_g': out['qk_norm_q_g'], 'qk_norm_k_g': out['qk_norm_k_g'], 'w_o_mla': out['w_o_mla'], 'conv_w': out['conv_w'], 'conv_b': out['conv_b'], 'conv_ln_g': out['conv_ln_g'], 'conv_ln_b': out['conv_ln_b'], 'w_pw_out': out['w_pw_out'], 'w_out': out['w_out'], 'norm2_g': out['norm2_g'], 'w_ff1': out['w_ff1'], 'w_ff2': out['w_ff2'], 'loss_target': out['loss_target'], 'm_w_ada': out['m_w_ada'], 'm_b_ada': out['m_b_ada'], 'm_norm1_g': out['m_norm1_g'], 'm_w_in': out['m_w_in'], 'm_q_latent_g': out['m_q_latent_g'], 'm_w_uq': out['m_w_uq'], 'm_kv_latent_g': out['m_kv_latent_g'], 'm_w_ukv': out['m_w_ukv'], 'm_qk_norm_q_g': out['m_qk_norm_q_g'], 'm_qk_norm_k_g': out['m_qk_norm_k_g'], 'm_w_o_mla': out['m_w_o_mla'], 'm_conv_w': out['m_conv_w'], 'm_conv_b': out['m_conv_b'], 'm_conv_ln_g': out['m_conv_ln_g'], 'm_conv_ln_b': out['m_conv_ln_b'], 'm_w_pw_out': out['m_w_pw_out'], 'm_w_out': out['m_w_out'], 'm_norm2_g': out['m_norm2_g'], 'm_w_ff1': out['m_w_ff1'], 'm_w_ff2': out['m_w_ff2'], 'v_w_ada': out['v_w_ada'], 'v_b_ada': out['v_b_ada'], 'v_norm1_g': out['v_norm1_g'], 'v_w_in': out['v_w_in'], 'v_q_latent_g': out['v_q_latent_g'], 'v_w_uq': out['v_w_uq'], 'v_kv_latent_g': out['v_kv_latent_g'], 'v_w_ukv': out['v_w_ukv'], 'v_qk_norm_q_g': out['v_qk_norm_q_g'], 'v_qk_norm_k_g': out['v_qk_norm_k_g'], 'v_w_o_mla': out['v_w_o_mla'], 'v_conv_w': out['v_conv_w'], 'v_conv_b': out['v_conv_b'], 'v_conv_ln_g': out['v_conv_ln_g'], 'v_conv_ln_b': out['v_conv_ln_b'], 'v_w_pw_out': out['v_w_pw_out'], 'v_w_out': out['v_w_out'], 'v_norm2_g': out['v_norm2_g'], 'v_w_ff1': out['v_w_ff1'], 'v_w_ff2': out['v_w_ff2']}


def _loss(weights, diff, rest, loss_target):
    with _jax.named_scope("forward"):
        args = {**rest, TWIN_DIFF_INPUT: diff, **{k: w.astype(_WEIGHT_DTYPES[k]) for k, w in weights.items()}}
        y = _forward(args)
    with _jax.named_scope("loss_head"):
        err = _jnp.square(y.astype(_jnp.float32) - loss_target)
        return 0.5 * _jnp.sum(_jnp.mean(err, axis=-1)) if err.ndim else 0.5 * err


def _adamw(w, g, m, v):
    m = ADAM_B1 * m + (1.0 - ADAM_B1) * g
    v = ADAM_B2 * v + (1.0 - ADAM_B2) * _jnp.square(g)
    m_hat = m / (1.0 - ADAM_B1 ** ADAM_STEP)
    v_hat = v / (1.0 - ADAM_B2 ** ADAM_STEP)
    delta = -ADAM_LR * (m_hat / (_jnp.sqrt(v_hat) + ADAM_EPS) + ADAM_WD * w)
    return delta, m, v


def reference(x, c, w_ada, b_ada, norm1_g, w_in, q_latent_g, w_uq, kv_latent_g, w_ukv, qk_norm_q_g, qk_norm_k_g, w_o_mla, conv_w, conv_b, conv_ln_g, conv_ln_b, w_pw_out, w_out, norm2_g, w_ff1, w_ff2, loss_target, m_w_ada, m_b_ada, m_norm1_g, m_w_in, m_q_latent_g, m_w_uq, m_kv_latent_g, m_w_ukv, m_qk_norm_q_g, m_qk_norm_k_g, m_w_o_mla, m_conv_w, m_conv_b, m_conv_ln_g, m_conv_ln_b, m_w_pw_out, m_w_out, m_norm2_g, m_w_ff1, m_w_ff2, v_w_ada, v_b_ada, v_norm1_g, v_w_in, v_q_latent_g, v_w_uq, v_kv_latent_g, v_w_ukv, v_qk_norm_q_g, v_qk_norm_k_g, v_w_o_mla, v_conv_w, v_conv_b, v_conv_ln_g, v_conv_ln_b, v_w_pw_out, v_w_out, v_norm2_g, v_w_ff1, v_w_ff2):
    given = dict(x=x, c=c, w_ada=w_ada, b_ada=b_ada, norm1_g=norm1_g, w_in=w_in, q_latent_g=q_latent_g, w_uq=w_uq, kv_latent_g=kv_latent_g, w_ukv=w_ukv, qk_norm_q_g=qk_norm_q_g, qk_norm_k_g=qk_norm_k_g, w_o_mla=w_o_mla, conv_w=conv_w, conv_b=conv_b, conv_ln_g=conv_ln_g, conv_ln_b=conv_ln_b, w_pw_out=w_pw_out, w_out=w_out, norm2_g=norm2_g, w_ff1=w_ff1, w_ff2=w_ff2, loss_target=loss_target, m_w_ada=m_w_ada, m_b_ada=m_b_ada, m_norm1_g=m_norm1_g, m_w_in=m_w_in, m_q_latent_g=m_q_latent_g, m_w_uq=m_w_uq, m_kv_latent_g=m_kv_latent_g, m_w_ukv=m_w_ukv, m_qk_norm_q_g=m_qk_norm_q_g, m_qk_norm_k_g=m_qk_norm_k_g, m_w_o_mla=m_w_o_mla, m_conv_w=m_conv_w, m_conv_b=m_conv_b, m_conv_ln_g=m_conv_ln_g, m_conv_ln_b=m_conv_ln_b, m_w_pw_out=m_w_pw_out, m_w_out=m_w_out, m_norm2_g=m_norm2_g, m_w_ff1=m_w_ff1, m_w_ff2=m_w_ff2, v_w_ada=v_w_ada, v_b_ada=v_b_ada, v_norm1_g=v_norm1_g, v_w_in=v_w_in, v_q_latent_g=v_q_latent_g, v_w_uq=v_w_uq, v_kv_latent_g=v_kv_latent_g, v_w_ukv=v_w_ukv, v_qk_norm_q_g=v_qk_norm_q_g, v_qk_norm_k_g=v_qk_norm_k_g, v_w_o_mla=v_w_o_mla, v_conv_w=v_conv_w, v_conv_b=v_conv_b, v_conv_ln_g=v_conv_ln_g, v_conv_ln_b=v_conv_ln_b, v_w_pw_out=v_w_pw_out, v_w_out=v_w_out, v_norm2_g=v_norm2_g, v_w_ff1=v_w_ff1, v_w_ff2=v_w_ff2)
    weights = {n: given[n] for n in TWIN_WEIGHTS}
    shared = {n: given[n] for n in SHARED_INPUTS}
    per_example = {n: given[n] for n in ['x', 'c']}
    grad_fn = _jax.value_and_grad(_loss, argnums=(0, 1))

    def one_microbatch(ex, loss_target):
        ex = dict(ex)
        diff = ex.pop(TWIN_DIFF_INPUT)
        return grad_fn(weights, diff, {**shared, **ex}, loss_target)

    if N_MICROBATCH == 1:
        loss, (grad_w, grad_x) = one_microbatch(per_example, given["loss_target"])
    else:
        def body(carry, xs):
            loss_sum, grad_sum = carry
            l_k, (gw_k, gx_k) = one_microbatch(xs[0], xs[1])
            with _jax.named_scope("update"):
                return (loss_sum + l_k, _jax.tree.map(_jnp.add, grad_sum, gw_k)), gx_k

        init = (_jnp.zeros((), _jnp.float32), _jax.tree.map(_jnp.zeros_like, weights))
        (loss, grad_w), grad_x = _jax.lax.scan(body, init, (per_example, given["loss_target"]))
    with _jax.named_scope("update"):
        delta_w, new_m, new_v = {}, {}, {}
        for n in TWIN_WEIGHTS:
            delta_w[n], new_m[n], new_v[n] = _adamw(weights[n], grad_w[n], given["m_" + n], given["v_" + n])
    return (loss, grad_x, *[grad_w[n] for n in TWIN_WEIGHTS], *[delta_w[n] for n in TWIN_WEIGHTS],
            *[new_m[n] for n in TWIN_WEIGHTS], *[new_v[n] for n in TWIN_WEIGHTS])
```

```python
import functools

import jax
import jax.numpy as jnp
from jax import lax
from jax.experimental import pallas as pl
from jax.experimental.pallas import tpu as pltpu

F32 = jnp.float32
BF16 = jnp.bfloat16
MESH = pl.DeviceIdType.MESH
ANY = pl.BlockSpec(memory_space=pl.ANY)

CHUNK = 64
CHUNK_SHIFT = 6
N_HEADS = 8
QK_NOPE = 64
QK_ROPE = 32
QK_HEAD = QK_NOPE + QK_ROPE
V_HEAD = 64
Q_RANK = 256
KV_RANK = 128
MLA_WIDTH = N_HEADS * V_HEAD
CONV_CH = 512
CONV_W = 31
ROPE_THETA = 10000.0
EPS = 1e-6
LANES = 128
HW = N_HEADS * LANES
OFF_Q = Q_RANK
OFF_KV = OFF_Q + KV_RANK
OFF_KR = OFF_KV + QK_ROPE
OFF_GLU = OFF_KR + 2 * CONV_CH
P_KV = Q_RANK
P_KR = P_KV + KV_RANK
P_GLU = P_KR + LANES
P_GATE = P_GLU + 2 * CONV_CH
KR_LANE = QK_NOPE
HALO = 32

ADAM_LR = 0.001
ADAM_B1 = 0.9
ADAM_B2 = 0.999
ADAM_EPS = 1e-08
ADAM_WD = 0.01
ADAM_STEP = 10

VMEM_LIMIT = 56 * 1024 * 1024
BQ = 256


def _params(*sem):
    return pltpu.CompilerParams(dimension_semantics=sem, vmem_limit_bytes=VMEM_LIMIT)


def _dot(a, b):
    return jnp.dot(a, b, preferred_element_type=F32)


def _dot_tn(a, b):
    return lax.dot_general(a, b, (((0,), (0,)), ((), ())), preferred_element_type=F32)


def _dot_nt(a, b):
    return lax.dot_general(a, b, (((1,), (1,)), ((), ())), preferred_element_type=F32)


def _acc(ref, val, first):
    @pl.when(first)
    def _():
        ref[...] = val

    @pl.when(jnp.logical_not(first))
    def _():
        ref[...] += val


def _rms(x):
    r = lax.rsqrt(jnp.mean(x * x, axis=-1, keepdims=True) + EPS)
    return x * r, r


def _rms_bwd(n, r, dn):
    return r * (dn - n * jnp.mean(dn * n, axis=-1, keepdims=True))


def _head_rms(sl):
    r = lax.rsqrt(jnp.sum(sl * sl, axis=-1, keepdims=True) * (1.0 / QK_HEAD) + EPS)
    return sl * r, r


def _head_rms_bwd(n, r, dn):
    return r * (dn - n * (jnp.sum(dn * n, axis=-1, keepdims=True) * (1.0 / QK_HEAD)))


def _rope(x, c, s1, s2):
    return x * c + pltpu.roll(x, QK_ROPE // 2, 1) * s1 + pltpu.roll(x, LANES - QK_ROPE // 2, 1) * s2


def _rope_t(dy, c, s1, s2):
    return dy * c + pltpu.roll(dy * s1, LANES - QK_ROPE // 2, 1) + pltpu.roll(dy * s2, QK_ROPE // 2, 1)


def _rope_tables(seq):
    half = QK_ROPE // 2
    inv_freq = ROPE_THETA ** (-jnp.arange(0, QK_ROPE, 2, dtype=F32) / QK_ROPE)
    ang = jnp.arange(seq, dtype=F32)[:, None] * inv_freq[None, :]
    cos, sin = jnp.cos(ang), jnp.sin(ang)
    z = lambda n: jnp.zeros((seq, n), F32)
    tail = LANES - QK_HEAD
    c = jnp.concatenate([jnp.ones((seq, QK_NOPE), F32), cos, cos, jnp.ones((seq, tail), F32)], axis=1)
    s1 = jnp.concatenate([z(QK_NOPE + half), sin, z(tail)], axis=1)
    s2 = jnp.concatenate([z(QK_NOPE), -sin, z(half + tail)], axis=1)
    return c, s1, s2


def _row(tm, w):
    return pl.BlockSpec((tm, w), lambda i: (i, 0))


def _seqv(w, tps):
    return pl.BlockSpec((None, 1, w), lambda i: (i // tps, 0, 0))


def _full(shape):
    return pl.BlockSpec(shape, lambda i: tuple(0 for _ in shape))


def _sds(shape, dtype):
    return jax.ShapeDtypeStruct(shape, dtype)


def _load_resident(i, pairs):
    @pl.when(i == 0)
    def _():
        for src, dst in pairs:
            pltpu.sync_copy(src, dst)


def _all_gather8(block, name):
    m, n = block.shape

    def body(x_ref, out_ref, send_sems, recv_sems, local_sem):
        x, y, c = lax.axis_index("x"), lax.axis_index("y"), lax.axis_index("c")
        me, sibling = (x, y, c), (x, y, 1 - c)
        chips = [(1 - x, y), (x, 1 - y), (1 - x, 1 - y)]

        def rows(px, py, pc):
            return out_ref.at[4 * px + 2 * py + pc]

        def copy(k, blk, to, src=None):
            return pltpu.make_async_remote_copy(
                src_ref=rows(*blk) if src is None else src, dst_ref=rows(*blk),
                send_sem=send_sems.at[k], recv_sem=recv_sems.at[k], device_id=to, device_id_type=MESH)

        mine = pltpu.make_async_copy(x_ref, rows(*me), local_sem)
        mine.start()
        first = [copy(0, me, sibling, src=x_ref)]
        first += [copy(1 + j, me, (*chip, c), src=x_ref) for j, chip in enumerate(chips)]
        for cp in first:
            cp.start()
        passed = [copy(4 + j, (*chip, c), sibling) for j, chip in enumerate(chips)]
        for j, chip in enumerate(chips):
            copy(1 + j, (*chip, c), me).wait_recv()
            passed[j].start()
        copy(0, sibling, me).wait_recv()
        for j, chip in enumerate(chips):
            copy(4 + j, (*chip, 1 - c), me).wait_recv()
        for cp in first + passed:
            cp.wait_send()
        mine.wait()

    return pl.pallas_call(
        body, name=name, out_shape=_sds((8, m, n), block.dtype), in_specs=[ANY], out_specs=ANY,
        scratch_shapes=[pltpu.SemaphoreType.DMA((7,)), pltpu.SemaphoreType.DMA((7,)), pltpu.SemaphoreType.DMA],
    )(block)


def _sibling_swap_other_half(g2, name):
    shape = g2.shape[1:]

    def body(g_ref, land_ref, send_sem, recv_sem):
        x, y, c = lax.axis_index("x"), lax.axis_index("y"), lax.axis_index("c")
        cp = pltpu.make_async_remote_copy(src_ref=g_ref.at[1 - c], dst_ref=land_ref, send_sem=send_sem,
                                          recv_sem=recv_sem, device_id=(x, y, 1 - c), device_id_type=MESH)
        cp.start()
        cp.wait()

    return pl.pallas_call(
        body, name=name, out_shape=_sds(shape, g2.dtype), in_specs=[ANY], out_specs=ANY,
        scratch_shapes=[pltpu.SemaphoreType.DMA, pltpu.SemaphoreType.DMA],
    )(g2)


def _chip_scatter(h, name):
    _, r, n = h.shape

    def body(h_ref, land_ref, send_sems, recv_sems):
        x, y, c = lax.axis_index("x"), lax.axis_index("y"), lax.axis_index("c")
        chips = [(1 - x, y), (x, 1 - y), (1 - x, 1 - y)]
        cps = [pltpu.make_async_remote_copy(src_ref=h_ref.at[2 * tx + ty], dst_ref=land_ref.at[j],
                                            send_sem=send_sems.at[j], recv_sem=recv_sems.at[j],
                                            device_id=(tx, ty, c), device_id_type=MESH)
               for j, (tx, ty) in enumerate(chips)]
        for cp in cps:
            cp.start()
        for cp in cps:
            cp.wait()

    return pl.pallas_call(
        body, name=name, out_shape=_sds((3, r, n), h.dtype), in_specs=[ANY], out_specs=ANY,
        scratch_shapes=[pltpu.SemaphoreType.DMA((3,)), pltpu.SemaphoreType.DMA((3,))],
    )(h)


def _sibling_all_gather(f, name):
    r, n = f.shape

    def body(f_ref, out_ref, send_sem, recv_sem, local_sem):
        x, y, c = lax.axis_index("x"), lax.axis_index("y"), lax.axis_index("c")
        mine = pltpu.make_async_copy(f_ref, out_ref.at[c], local_sem)
        mine.start()
        cp = pltpu.make_async_remote_copy(src_ref=f_ref, dst_ref=out_ref.at[c], send_sem=send_sem, recv_sem=recv_sem,
                                          device_id=(x, y, 1 - c), device_id_type=MESH)
        cp.start()
        recv = pltpu.make_async_remote_copy(src_ref=f_ref, dst_ref=out_ref.at[1 - c], send_sem=send_sem,
                                            recv_sem=recv_sem, device_id=(x, y, 1 - c), device_id_type=MESH)
        recv.wait_recv()
        cp.wait_send()
        mine.wait()

    return pl.pallas_call(
        body, name=name, out_shape=_sds((2, r, n), f.dtype), in_specs=[ANY], out_specs=ANY,
        scratch_shapes=[pltpu.SemaphoreType.DMA, pltpu.SemaphoreType.DMA, pltpu.SemaphoreType.DMA],
    )(f)


def _add_pair(g2, land, cidx, tr):
    _, ns, r, n = g2.shape

    def body(c_ref, a_ref, b_ref, o_ref):
        o_ref[...] = a_ref[...] + b_ref[...]

    return pl.pallas_call(
        body, name="grad_pair_sum", out_shape=_sds((ns, r, n), F32),
        grid_spec=pltpu.PrefetchScalarGridSpec(
            num_scalar_prefetch=1, grid=(ns, r // tr),
            in_specs=[pl.BlockSpec((None, None, tr, n), lambda s, i, cr: (cr[0], s, i, 0)),
                      pl.BlockSpec((None, tr, n), lambda s, i, cr: (s, i, 0))],
            out_specs=pl.BlockSpec((None, tr, n), lambda s, i, cr: (s, i, 0))),
        compiler_params=_params("arbitrary", "arbitrary"),
    )(cidx, g2, land)


def _add_chips(h, land, own, tr):
    _, r, n = h.shape

    def body(o_idx, h_ref, l_ref, o_ref):
        o_ref[...] = ((h_ref[...] + l_ref[0]) + l_ref[1]) + l_ref[2]

    return pl.pallas_call(
        body, name="grad_chip_sum", out_shape=_sds((r, n), F32),
        grid_spec=pltpu.PrefetchScalarGridSpec(
            num_scalar_prefetch=1, grid=(r // tr,),
            in_specs=[pl.BlockSpec((None, tr, n), lambda i, o: (o[0], i, 0)),
                      pl.BlockSpec((3, tr, n), lambda i, o: (0, i, 0))],
            out_specs=pl.BlockSpec((tr, n), lambda i, o: (i, 0))),
        compiler_params=_params("arbitrary"),
    )(own, h, land)


def _adamw(w, g, m, v, tr, name):
    r, n = w.shape

    def body(w_ref, g_ref, m_ref, v_ref, d_ref, nm_ref, nv_ref):
        gg = g_ref[...]
        nm = ADAM_B1 * m_ref[...] + (1.0 - ADAM_B1) * gg
        nv = ADAM_B2 * v_ref[...] + (1.0 - ADAM_B2) * (gg * gg)
        m_hat = nm / (1.0 - ADAM_B1 ** ADAM_STEP)
        v_hat = nv / (1.0 - ADAM_B2 ** ADAM_STEP)
        d_ref[...] = -ADAM_LR * (m_hat / (jnp.sqrt(v_hat) + ADAM_EPS) + ADAM_WD * w_ref[...])
        nm_ref[...] = nm
        nv_ref[...] = nv

    spec = pl.BlockSpec((tr, n), lambda i: (i, 0))
    return pl.pallas_call(
        body, name=name, out_shape=(_sds((r, n), F32),) * 3, grid=(r // tr,),
        in_specs=[spec] * 4, out_specs=(spec,) * 3, compiler_params=_params("arbitrary"),
    )(w, g, m, v)


def _ada_mod(c_all, w_sh, b_sh):
    b, _ = c_all.shape
    n = w_sh.shape[1]

    def body(c_ref, w_ref, b_ref, o_ref):
        cc = c_ref[...]
        ca = (cc * jax.nn.sigmoid(cc)).astype(BF16)
        o_ref[...] = _dot(ca, w_ref[...].astype(BF16)) + b_ref[...]

    return pl.pallas_call(body, name="ada_mod", out_shape=_sds((b, n), F32),
                          compiler_params=pltpu.CompilerParams(vmem_limit_bytes=VMEM_LIMIT))(c_all, w_sh, b_sh)


def _ada_bwd(c_all, dmod_all, dmod_sh, small_parts):
    b, d = c_all.shape
    n6 = dmod_all.shape[1]
    n = dmod_sh.shape[1]
    nd, rs, _ = small_parts.shape

    def body(c_ref, da_ref, ds_ref, sp_ref, dw_ref, db_ref, sm_ref):
        cc = c_ref[...]
        ca = (cc * jax.nn.sigmoid(cc)).astype(BF16)
        dw_ref[...] = _dot_tn(ca, ds_ref[...].astype(BF16))
        db_ref[...] = jnp.sum(da_ref[...], axis=0, keepdims=True)
        tot = sp_ref[0]
        for k in range(1, nd):
            tot = tot + sp_ref[k]
        sm_ref[...] = tot

    return pl.pallas_call(
        body, name="ada_bwd", out_shape=(_sds((d, n), F32), _sds((1, n6), F32), _sds((rs, LANES), F32)),
        compiler_params=pltpu.CompilerParams(vmem_limit_bytes=VMEM_LIMIT),
    )(c_all, dmod_all, dmod_sh, small_parts)


def _fwd_in(x, g1, scale1, shift1, win_p, tm, tps):
    t, d = x.shape
    npad = win_p.shape[1]
    ngate = npad - P_GATE

    def body(x_ref, g_ref, sc_ref, sh_ref, w_hbm, h_ref, zq_ref, zkv_ref, zkr_ref, zglu_ref, zgate_ref, u0_ref, w_ref):
        _load_resident(pl.program_id(0), [(w_hbm, w_ref)])
        n, _ = _rms(x_ref[...])
        h = ((n * g_ref[...]) * (1.0 + sc_ref[...]) + sh_ref[...]).astype(BF16)
        h_ref[...] = h
        z = _dot(h, w_ref[...])
        zq_ref[...] = z[:, :P_KV]
        zkv_ref[...] = z[:, P_KV:P_KR]
        zkr_ref[...] = z[:, P_KR:P_GLU]
        zglu = z[:, P_GLU:P_GATE]
        zglu_ref[...] = zglu
        zgate_ref[...] = z[:, P_GATE:]
        u0_ref[...] = zglu[:, :CONV_CH] * jax.nn.sigmoid(zglu[:, CONV_CH:])

    return pl.pallas_call(
        body, name="fwd_in", grid=(t // tm,),
        out_shape=(_sds((t, d), BF16), _sds((t, Q_RANK), F32), _sds((t, KV_RANK), F32), _sds((t, LANES), F32),
                   _sds((t, 2 * CONV_CH), F32), _sds((t, ngate), F32), _sds((t, CONV_CH), F32)),
        in_specs=[_row(tm, d), _full((1, d)), _seqv(d, tps), _seqv(d, tps), ANY],
        out_specs=(_row(tm, d), _row(tm, Q_RANK), _row(tm, KV_RANK), _row(tm, LANES), _row(tm, 2 * CONV_CH),
                   _row(tm, ngate), _row(tm, CONV_CH)),
        scratch_shapes=[pltpu.VMEM(win_p.shape, BF16)],
        compiler_params=_params("arbitrary"),
    )(x, g1, scale1, shift1, win_p)


def _mla_prep(zq, zkv, zkr, gql, gkvl, gq, gk, tabs, wuq_p, wk_p, wv_p, tm, tps):
    t = zq.shape[0]
    c_t, s1_t, s2_t = tabs
    tab = pl.BlockSpec((tm, LANES), lambda i: (i % tps, 0))

    def body(zq_ref, zkv_ref, zkr_ref, gql_ref, gkvl_ref, gq_ref, gk_ref, c_ref, s1_ref, s2_ref, wuq_ref, wk_ref, wv_ref,
             q_ref, k_ref, v_ref, qln_ref, kvn_ref):
        c, s1, s2 = c_ref[...], s1_ref[...], s2_ref[...]
        nq, _ = _rms(zq_ref[...])
        qln = (nq * gql_ref[...]).astype(BF16)
        qln_ref[...] = qln
        qpre = _dot(qln, wuq_ref[...])
        nkv, _ = _rms(zkv_ref[...])
        kvn = (nkv * gkvl_ref[...]).astype(BF16)
        kvn_ref[...] = kvn
        knope = _dot(kvn, wk_ref[...])
        v_ref[...] = _dot(kvn, wv_ref[...]).astype(BF16)
        zkr_v = zkr_ref[...]
        for hd in range(N_HEADS):
            sl = slice(hd * LANES, (hd + 1) * LANES)
            n, _ = _head_rms(qpre[:, sl])
            q_ref[:, sl] = _rope(n * gq_ref[...], c, s1, s2).astype(BF16)
            n, _ = _head_rms(knope[:, sl] + zkr_v)
            k_ref[:, sl] = _rope(n * gk_ref[...], c, s1, s2).astype(BF16)

    return pl.pallas_call(
        body, name="mla_prep", grid=(t // tm,),
        out_shape=(_sds((t, HW), BF16),) * 3 + (_sds((t, Q_RANK), BF16), _sds((t, KV_RANK), BF16)),
        in_specs=[_row(tm, Q_RANK), _row(tm, KV_RANK), _row(tm, LANES), _full((1, Q_RANK)), _full((1, KV_RANK)),
                  _full((1, LANES)), _full((1, LANES)), tab, tab, tab,
                  _full(wuq_p.shape), _full(wk_p.shape), _full(wv_p.shape)],
        out_specs=(_row(tm, HW),) * 3 + (_row(tm, Q_RANK), _row(tm, KV_RANK)),
        compiler_params=_params("arbitrary"),
    )(zq, zkv, zkr, gql, gkvl, gq, gk, c_t, s1_t, s2_t, wuq_p, wk_p, wv_p)


def _scores(q_i, k_e, i, e):
    s = _dot_nt(q_i, k_e) * (QK_HEAD ** -0.5)
    rc = jnp.right_shift(lax.broadcasted_iota(jnp.int32, (BQ, 1), 0) + i * BQ, CHUNK_SHIFT)
    cc = jnp.right_shift(lax.broadcasted_iota(jnp.int32, (1, e), 1), CHUNK_SHIFT)
    s = jnp.where(rc >= cc, s, jnp.finfo(F32).min)
    p = jnp.exp(s - jnp.max(s, axis=-1, keepdims=True))
    return p / jnp.sum(p, axis=-1, keepdims=True)


def _attn_fwd(q, k, v, nseq, seq):
    t = q.shape[0]
    blk = pl.BlockSpec((seq, LANES), lambda b, h: (b, h))

    def body(q_ref, k_ref, v_ref, o_ref):
        for i in range(seq // BQ):
            e = (i + 1) * BQ
            p = _scores(q_ref[i * BQ:e, :], k_ref[:e, :], i, e)
            o_ref[i * BQ:e, :] = _dot(p.astype(BF16), v_ref[:e, :]).astype(BF16)

    return pl.pallas_call(
        body, name="attn_fwd", grid=(nseq, N_HEADS), out_shape=_sds((t, HW), BF16),
        in_specs=[blk, blk, blk], out_specs=blk, compiler_params=_params("arbitrary", "arbitrary"),
    )(q, k, v)


def _fwd_mix(attn, u0, zgate, x, gate1, wo_p, cw, cb, lng, lnb, wpw, wout, tm, tps):
    t, d = x.shape
    hpt = tm // HALO

    def body(a_ref, u_ref, uh_ref, zg_ref, x_ref, g1_ref, wo_ref, cw_ref, cb_ref, lng_ref, lnb_ref, wpw_ref, wout_ref,
             x1_ref, mixed_ref, mpre_ref, ya_ref, yb_ref, u1_ref, u3_ref, ext_ref):
        i = pl.program_id(0)
        ya = _dot(a_ref[...], wo_ref[...])
        ya_ref[...] = ya
        first = (i % tps) == 0
        ext_ref[:HALO, :] = jnp.where(first, 0.0, uh_ref[...])
        ext_ref[HALO:, :] = u_ref[...]
        acc = jnp.zeros((tm, CONV_CH), F32) + cb_ref[...]
        for kk in range(CONV_W):
            o = HALO - (CONV_W - 1) + kk
            acc = acc + cw_ref[kk:kk + 1, :] * ext_ref[o:o + tm, :]
        u1_ref[...] = acc
        mu = jnp.mean(acc, axis=-1, keepdims=True)
        xc = acc - mu
        rstd = lax.rsqrt(jnp.mean(xc * xc, axis=-1, keepdims=True) + EPS)
        l = (xc * rstd) * lng_ref[...] + lnb_ref[...]
        u3 = (l * jax.nn.sigmoid(l)).astype(BF16)
        u3_ref[...] = u3
        yb = _dot(u3, wpw_ref[...])
        yb_ref[...] = yb
        zg = zg_ref[...]
        mpre = (jax.nn.sigmoid(zg[:, :d]) * ya + jax.nn.sigmoid(zg[:, d:]) * yb).astype(BF16)
        mpre_ref[...] = mpre
        mixed = _dot(mpre, wout_ref[...])
        mixed_ref[...] = mixed
        x1_ref[...] = x_ref[...] + g1_ref[...] * mixed

    halo = pl.BlockSpec((HALO, CONV_CH), lambda i: (jnp.maximum(i * hpt - 1, 0), 0))
    return pl.pallas_call(
        body, name="fwd_mix", grid=(t // tm,),
        out_shape=(_sds((t, d), F32), _sds((t, d), F32), _sds((t, d), BF16), _sds((t, d), F32), _sds((t, d), F32),
                   _sds((t, CONV_CH), F32), _sds((t, CONV_CH), BF16)),
        in_specs=[_row(tm, HW), _row(tm, CONV_CH), halo, _row(tm, 2 * d), _row(tm, d), _seqv(d, tps),
                  _full(wo_p.shape), _full(cw.shape), _full((1, CONV_CH)), _full((1, CONV_CH)), _full((1, CONV_CH)),
                  _full(wpw.shape), _full(wout.shape)],
        out_specs=(_row(tm, d), _row(tm, d), _row(tm, d), _row(tm, d), _row(tm, d), _row(tm, CONV_CH),
                   _row(tm, CONV_CH)),
        scratch_shapes=[pltpu.VMEM((tm + HALO, CONV_CH), F32)],
        compiler_params=_params("arbitrary"),
    )(attn, u0, u0, zgate, x, gate1, wo_p, cw, cb, lng, lnb, wpw, wout)


def _fwd_ffn(x1, target, g2, scale2, shift2, gate2, w1, w2, tm, tps):
    t, d = x1.shape
    dff = w1.shape[1]

    def body(x1_ref, tg_ref, g_ref, sc_ref, sh_ref, gt_ref, w1_hbm, w2_hbm,
             h2_ref, a_ref, r_ref, dy_ref, df_ref, dgate_ref, loss_ref, w1_ref, w2_ref):
        i = pl.program_id(0)
        _load_resident(i, [(w1_hbm, w1_ref), (w2_hbm, w2_ref)])
        x1v = x1_ref[...]
        n, _ = _rms(x1v)
        h2 = ((n * g_ref[...]) * (1.0 + sc_ref[...]) + sh_ref[...]).astype(BF16)
        h2_ref[...] = h2
        a = _dot(h2, w1_ref[...])
        a_ref[...] = a
        r = jnp.square(jnp.maximum(a, 0.0)).astype(BF16)
        r_ref[...] = r
        f = _dot(r, w2_ref[...])
        e = (x1v + gt_ref[...] * f) - tg_ref[...]
        part = 0.5 * jnp.sum(jnp.mean(e * e, axis=-1, keepdims=True), axis=0, keepdims=True)
        _acc(loss_ref, jnp.broadcast_to(part, loss_ref.shape), i == 0)
        dy = e * (1.0 / d)
        dy_ref[...] = dy
        df_ref[...] = (dy * gt_ref[...]).astype(BF16)
        _acc(dgate_ref, jnp.sum(dy * f, axis=0, keepdims=True), (i % tps) == 0)

    nseq = t // (tm * tps)
    return pl.pallas_call(
        body, name="fwd_ffn", grid=(t // tm,),
        out_shape=(_sds((t, d), BF16), _sds((t, dff), F32), _sds((t, dff), BF16), _sds((t, d), F32), _sds((t, d), BF16),
                   _sds((nseq, 1, d), F32), _sds((8, LANES), F32)),
        in_specs=[_row(tm, d), _row(tm, d), _full((1, d)), _seqv(d, tps), _seqv(d, tps), _seqv(d, tps), ANY, ANY],
        out_specs=(_row(tm, d), _row(tm, dff), _row(tm, dff), _row(tm, d), _row(tm, d), _seqv(d, tps),
                   _full((8, LANES))),
        scratch_shapes=[pltpu.VMEM(w1.shape, BF16), pltpu.VMEM(w2.shape, BF16)],
        compiler_params=_params("arbitrary"),
    )(x1, target, g2, scale2, shift2, gate2, w1, w2)


def _bwd_ffn(df, a, x1, dy, mixed, g2, scale2, gate1, w2t, w1t, tm, tps):
    t, d = x1.shape
    dff = a.shape[1]

    def body(df_ref, a_ref, x1_ref, dy_ref, mx_ref, g_ref, sc_ref, g1_ref, w2t_hbm, w1t_hbm,
             da_ref, dx1_ref, dmixed_ref, dshift_ref, dscale_ref, dgate1_ref, dg2_ref, w2t_ref, w1t_ref):
        i = pl.program_id(0)
        _load_resident(i, [(w2t_hbm, w2t_ref), (w1t_hbm, w1t_ref)])
        first_seq = (i % tps) == 0
        dr = _dot(df_ref[...], w2t_ref[...])
        da = (dr * (2.0 * jnp.maximum(a_ref[...], 0.0))).astype(BF16)
        da_ref[...] = da
        dh2 = _dot(da, w1t_ref[...])
        n, r = _rms(x1_ref[...])
        g = g_ref[...]
        sc1 = 1.0 + sc_ref[...]
        _acc(dshift_ref, jnp.sum(dh2, axis=0, keepdims=True), first_seq)
        _acc(dscale_ref, jnp.sum(dh2 * (n * g), axis=0, keepdims=True), first_seq)
        _acc(dg2_ref, jnp.sum((dh2 * sc1) * n, axis=0, keepdims=True), i == 0)
        dx1 = dy_ref[...] + _rms_bwd(n, r, (dh2 * sc1) * g)
        dx1_ref[...] = dx1
        _acc(dgate1_ref, jnp.sum(dx1 * mx_ref[...], axis=0, keepdims=True), first_seq)
        dmixed_ref[...] = (dx1 * g1_ref[...]).astype(BF16)

    nseq = t // (tm * tps)
    sv = _sds((nseq, 1, d), F32)
    return pl.pallas_call(
        body, name="bwd_ffn", grid=(t // tm,),
        out_shape=(_sds((t, dff), BF16), _sds((t, d), F32), _sds((t, d), BF16), sv, sv, sv, _sds((1, d), F32)),
        in_specs=[_row(tm, d), _row(tm, dff), _row(tm, d), _row(tm, d), _row(tm, d), _full((1, d)), _seqv(d, tps),
                  _seqv(d, tps), ANY, ANY],
        out_specs=(_row(tm, dff), _row(tm, d), _row(tm, d), _seqv(d, tps), _seqv(d, tps), _seqv(d, tps),
                   _full((1, d))),
        scratch_shapes=[pltpu.VMEM(w2t.shape, BF16), pltpu.VMEM(w1t.shape, BF16)],
        compiler_params=_params("arbitrary"),
    )(df, a, x1, dy, mixed, g2, scale2, gate1, w2t, w1t)


def _bwd_mix(dmixed, zgate, ya, yb, u1, lng, lnb, woutt, wot_p, wpwt, tm):
    t, d = ya.shape

    def body(dm_ref, zg_ref, ya_ref, yb_ref, u1_ref, lng_ref, lnb_ref, woutt_ref, wot_ref, wpwt_ref,
             dya_ref, dyb_ref, dzg_ref, do_ref, du1_ref, dlng_ref, dlnb_ref, dcb_ref):
        i = pl.program_id(0)
        dmpre = _dot(dm_ref[...], woutt_ref[...])
        zg = zg_ref[...]
        ga = jax.nn.sigmoid(zg[:, :d])
        gb = jax.nn.sigmoid(zg[:, d:])
        dya = (dmpre * ga).astype(BF16)
        dyb = (dmpre * gb).astype(BF16)
        dya_ref[...] = dya
        dyb_ref[...] = dyb
        dzg_ref[:, :d] = ((dmpre * ya_ref[...]) * (ga * (1.0 - ga))).astype(BF16)
        dzg_ref[:, d:] = ((dmpre * yb_ref[...]) * (gb * (1.0 - gb))).astype(BF16)
        do_ref[...] = _dot(dya, wot_ref[...]).astype(BF16)
        du3 = _dot(dyb, wpwt_ref[...])
        u1 = u1_ref[...]
        mu = jnp.mean(u1, axis=-1, keepdims=True)
        xc = u1 - mu
        rstd = lax.rsqrt(jnp.mean(xc * xc, axis=-1, keepdims=True) + EPS)
        nh = xc * rstd
        l = nh * lng_ref[...] + lnb_ref[...]
        sg = jax.nn.sigmoid(l)
        dl = du3 * (sg * (1.0 + l * (1.0 - sg)))
        _acc(dlng_ref, jnp.sum(dl * nh, axis=0, keepdims=True), i == 0)
        _acc(dlnb_ref, jnp.sum(dl, axis=0, keepdims=True), i == 0)
        dnh = dl * lng_ref[...]
        du1 = rstd * (dnh - jnp.mean(dnh, axis=-1, keepdims=True) - nh * jnp.mean(dnh * nh, axis=-1, keepdims=True))
        du1_ref[...] = du1
        _acc(dcb_ref, jnp.sum(du1, axis=0, keepdims=True), i == 0)

    cv = _sds((1, CONV_CH), F32)
    return pl.pallas_call(
        body, name="bwd_mix", grid=(t // tm,),
        out_shape=(_sds((t, d), BF16), _sds((t, d), BF16), _sds((t, 2 * d), BF16), _sds((t, HW), BF16),
                   _sds((t, CONV_CH), F32), cv, cv, cv),
        in_specs=[_row(tm, d), _row(tm, 2 * d), _row(tm, d), _row(tm, d), _row(tm, CONV_CH), _full((1, CONV_CH)),
                  _full((1, CONV_CH)), _full(woutt.shape), _full(wot_p.shape), _full(wpwt.shape)],
        out_specs=(_row(tm, d), _row(tm, d), _row(tm, 2 * d), _row(tm, HW), _row(tm, CONV_CH),
                   _full((1, CONV_CH)), _full((1, CONV_CH)), _full((1, CONV_CH))),
        compiler_params=_params("arbitrary"),
    )(dmixed, zgate, ya, yb, u1, lng, lnb, woutt, wot_p, wpwt)


def _bwd_conv(du1, u0, zglu, cw, tm, tps):
    t = du1.shape[0]
    hpt = tm // HALO
    last_blk = t // HALO - 1

    def body(du_ref, dun_ref, u_ref, uh_ref, zl_ref, cw_ref, dzl_ref, dcw_ref, ext_ref, dext_ref):
        i = pl.program_id(0)
        first = (i % tps) == 0
        last = (i % tps) == (tps - 1)
        ext_ref[:HALO, :] = jnp.where(first, 0.0, uh_ref[...])
        ext_ref[HALO:, :] = u_ref[...]
        du = du_ref[...]
        dext_ref[:tm, :] = du
        dext_ref[tm:, :] = jnp.where(last, 0.0, dun_ref[...])

        @pl.when(i == 0)
        def _():
            dcw_ref[...] = jnp.zeros_like(dcw_ref)

        du0 = jnp.zeros((tm, CONV_CH), F32)
        for kk in range(CONV_W):
            o = HALO - (CONV_W - 1) + kk
            dcw_ref[kk:kk + 1, :] += jnp.sum(du * ext_ref[o:o + tm, :], axis=0, keepdims=True)
            o2 = CONV_W - 1 - kk
            du0 = du0 + cw_ref[kk:kk + 1, :] * dext_ref[o2:o2 + tm, :]
        zl = zl_ref[...]
        ga = zl[:, :CONV_CH]
        sb = jax.nn.sigmoid(zl[:, CONV_CH:])
        dzl_ref[:, :CONV_CH] = (du0 * sb).astype(BF16)
        dzl_ref[:, CONV_CH:] = ((du0 * ga) * (sb * (1.0 - sb))).astype(BF16)

    prev = pl.BlockSpec((HALO, CONV_CH), lambda i: (jnp.maximum(i * hpt - 1, 0), 0))
    nxt = pl.BlockSpec((HALO, CONV_CH), lambda i: (jnp.minimum((i + 1) * hpt, last_blk), 0))
    return pl.pallas_call(
        body, name="bwd_conv", grid=(t // tm,),
        out_shape=(_sds((t, 2 * CONV_CH), BF16), _sds(cw.shape, F32)),
        in_specs=[_row(tm, CONV_CH), nxt, _row(tm, CONV_CH), prev, _row(tm, 2 * CONV_CH), _full(cw.shape)],
        out_specs=(_row(tm, 2 * CONV_CH), _full(cw.shape)),
        scratch_shapes=[pltpu.VMEM((tm + HALO, CONV_CH), F32), pltpu.VMEM((tm + HALO, CONV_CH), F32)],
        compiler_params=_params("arbitrary"),
    )(du1, du1, u0, u0, zglu, cw)


def _attn_bwd(q, k, v, do, nseq, seq):
    t = q.shape[0]
    blk = pl.BlockSpec((seq, LANES), lambda b, h: (b, h))

    def body(q_ref, k_ref, v_ref, do_ref, dq_ref, dk_ref, dv_ref, dka_ref, dva_ref):
        dka_ref[...] = jnp.zeros_like(dka_ref)
        dva_ref[...] = jnp.zeros_like(dva_ref)
        for i in range(seq // BQ):
            e = (i + 1) * BQ
            q_i = q_ref[i * BQ:e, :]
            do_i = do_ref[i * BQ:e, :]
            k_e = k_ref[:e, :]
            p = _scores(q_i, k_e, i, e)
            dp = _dot_nt(do_i, v_ref[:e, :])
            ds = (p * (dp - jnp.sum(p * dp, axis=-1, keepdims=True)) * (QK_HEAD ** -0.5)).astype(BF16)
            dq_ref[i * BQ:e, :] = _dot(ds, k_e)
            dka_ref[:e, :] += _dot_tn(ds, q_i)
            dva_ref[:e, :] += _dot_tn(p.astype(BF16), do_i)
        dk_ref[...] = dka_ref[...]
        dv_ref[...] = dva_ref[...].astype(BF16)

    return pl.pallas_call(
        body, name="attn_bwd", grid=(nseq, N_HEADS),
        out_shape=(_sds((t, HW), F32), _sds((t, HW), F32), _sds((t, HW), BF16)),
        in_specs=[blk] * 4, out_specs=(blk,) * 3,
        scratch_shapes=[pltpu.VMEM((seq, LANES), F32), pltpu.VMEM((seq, LANES), F32)],
        compiler_params=_params("arbitrary", "arbitrary"),
    )(q, k, v, do)


def _mla_bwd(dq, dk, dv, zq, zkv, zkr, gql, gkvl, gq, gk, tabs, wuq_p, wk_p, wuqt_p, wkt_p, wvt_p, tm, tps):
    t = zq.shape[0]
    c_t, s1_t, s2_t = tabs
    tab = pl.BlockSpec((tm, LANES), lambda i: (i % tps, 0))

    def body(dq_ref, dk_ref, dv_ref, zq_ref, zkv_ref, zkr_ref, gql_ref, gkvl_ref, gq_ref, gk_ref, c_ref, s1_ref, s2_ref,
             wuq_ref, wk_ref, wuqt_ref, wkt_ref, wvt_ref,
             dqpre_ref, dkh_ref, dzq_ref, dzkv_ref, dzkr_ref, dgq_ref, dgk_ref, dgql_ref, dgkvl_ref):
        i = pl.program_id(0)
        c, s1, s2 = c_ref[...], s1_ref[...], s2_ref[...]
        nq, rq = _rms(zq_ref[...])
        qpre = _dot((nq * gql_ref[...]).astype(BF16), wuq_ref[...])
        nkv, rkv = _rms(zkv_ref[...])
        knope = _dot((nkv * gkvl_ref[...]).astype(BF16), wk_ref[...])
        zkr_v = zkr_ref[...]
        dgq = jnp.zeros((1, LANES), F32)
        dgk = jnp.zeros((1, LANES), F32)
        dzkr = jnp.zeros((tm, LANES), F32)
        for hd in range(N_HEADS):
            sl = slice(hd * LANES, (hd + 1) * LANES)
            n, r = _head_rms(qpre[:, sl])
            dyr = _rope_t(dq_ref[:, sl], c, s1, s2)
            dgq = dgq + jnp.sum(dyr * n, axis=0, keepdims=True)
            dqpre_ref[:, sl] = _head_rms_bwd(n, r, dyr * gq_ref[...]).astype(BF16)
            n, r = _head_rms(knope[:, sl] + zkr_v)
            dyr = _rope_t(dk_ref[:, sl], c, s1, s2)
            dgk = dgk + jnp.sum(dyr * n, axis=0, keepdims=True)
            dkh = _head_rms_bwd(n, r, dyr * gk_ref[...])
            dzkr = dzkr + dkh
            dkh_ref[:, sl] = dkh.astype(BF16)
        _acc(dgq_ref, dgq, i == 0)
        _acc(dgk_ref, dgk, i == 0)
        dzkr_ref[...] = dzkr.astype(BF16)
        dqln = _dot(dqpre_ref[...], wuqt_ref[...])
        _acc(dgql_ref, jnp.sum(dqln * nq, axis=0, keepdims=True), i == 0)
        dzq_ref[...] = _rms_bwd(nq, rq, dqln * gql_ref[...]).astype(BF16)
        dkvn = _dot(dkh_ref[...], wkt_ref[...]) + _dot(dv_ref[...], wvt_ref[...])
        _acc(dgkvl_ref, jnp.sum(dkvn * nkv, axis=0, keepdims=True), i == 0)
        dzkv_ref[...] = _rms_bwd(nkv, rkv, dkvn * gkvl_ref[...]).astype(BF16)

    return pl.pallas_call(
        body, name="mla_bwd", grid=(t // tm,),
        out_shape=(_sds((t, HW), BF16), _sds((t, HW), BF16), _sds((t, Q_RANK), BF16), _sds((t, KV_RANK), BF16),
                   _sds((t, LANES), BF16), _sds((1, LANES), F32), _sds((1, LANES), F32), _sds((1, Q_RANK), F32),
                   _sds((1, KV_RANK), F32)),
        in_specs=[_row(tm, HW), _row(tm, HW), _row(tm, HW), _row(tm, Q_RANK), _row(tm, KV_RANK), _row(tm, LANES),
                  _full((1, Q_RANK)), _full((1, KV_RANK)), _full((1, LANES)), _full((1, LANES)), tab, tab, tab,
                  _full(wuq_p.shape), _full(wk_p.shape), _full(wuqt_p.shape), _full(wkt_p.shape), _full(wvt_p.shape)],
        out_specs=(_row(tm, HW), _row(tm, HW), _row(tm, Q_RANK), _row(tm, KV_RANK), _row(tm, LANES),
                   _full((1, LANES)), _full((1, LANES)), _full((1, Q_RANK)), _full((1, KV_RANK))),
        compiler_params=_params("arbitrary"),
    )(dq, dk, dv, zq, zkv, zkr, gql, gkvl, gq, gk, c_t, s1_t, s2_t, wuq_p, wk_p, wuqt_p, wkt_p, wvt_p)


def _bwd_in(dzq, dzkv, dzkr, dzglu, dzgate, x, dx1, g1, scale1, wint_p, tm, tps):
    t, d = x.shape
    ngate = dzgate.shape[1]

    def body(dzq_ref, dzkv_ref, dzkr_ref, dzglu_ref, dzgate_ref, x_ref, dx1_ref, g_ref, sc_ref, wt_hbm,
             gx_ref, dshift_ref, dscale_ref, dg1_ref, wt_ref):
        i = pl.program_id(0)
        _load_resident(i, [(wt_hbm, wt_ref)])
        first_seq = (i % tps) == 0
        dh = _dot(dzq_ref[...], wt_ref[:P_KV, :])
        dh = dh + _dot(dzkv_ref[...], wt_ref[P_KV:P_KR, :])
        dh = dh + _dot(dzkr_ref[...], wt_ref[P_KR:P_GLU, :])
        dh = dh + _dot(dzglu_ref[...], wt_ref[P_GLU:P_GATE, :])
        dh = dh + _dot(dzgate_ref[...], wt_ref[P_GATE:, :])
        n, r = _rms(x_ref[...])
        g = g_ref[...]
        sc1 = 1.0 + sc_ref[...]
        _acc(dshift_ref, jnp.sum(dh, axis=0, keepdims=True), first_seq)
        _acc(dscale_ref, jnp.sum(dh * (n * g), axis=0, keepdims=True), first_seq)
        _acc(dg1_ref, jnp.sum((dh * sc1) * n, axis=0, keepdims=True), i == 0)
        gx_ref[...] = dx1_ref[...] + _rms_bwd(n, r, (dh * sc1) * g)

    nseq = t // (tm * tps)
    sv = _sds((nseq, 1, d), F32)
    return pl.pallas_call(
        body, name="bwd_in", grid=(t // tm,),
        out_shape=(_sds((t, d), F32), sv, sv, _sds((1, d), F32)),
        in_specs=[_row(tm, Q_RANK), _row(tm, KV_RANK), _row(tm, LANES), _row(tm, 2 * CONV_CH), _row(tm, ngate),
                  _row(tm, d), _row(tm, d), _full((1, d)), _seqv(d, tps), ANY],
        out_specs=(_row(tm, d), _seqv(d, tps), _seqv(d, tps), _full((1, d))),
        scratch_shapes=[pltpu.VMEM(wint_p.shape, BF16)],
        compiler_params=_params("arbitrary"),
    )(dzq, dzkv, dzkr, dzglu, dzgate, x, dx1, g1, scale1, wint_p)


def _tn_matmul(a, b, name):
    t, k = a.shape
    n = b.shape[1]
    tk, tn, tt = min(k, 1024), min(n, 1024), min(t, 1024)

    def body(a_ref, b_ref, o_ref):
        _acc(o_ref, _dot_tn(a_ref[...], b_ref[...]), pl.program_id(2) == 0)

    return pl.pallas_call(
        body, name=name, grid=(k // tk, n // tn, t // tt), out_shape=_sds((k, n), F32),
        in_specs=[pl.BlockSpec((tt, tk), lambda i, j, s: (s, i)), pl.BlockSpec((tt, tn), lambda i, j, s: (s, j))],
        out_specs=pl.BlockSpec((tk, tn), lambda i, j, s: (i, j)),
        compiler_params=_params("arbitrary", "arbitrary", "arbitrary"),
    )(a, b)


def _pad_heads(w, width):
    k = w.shape[0]
    w3 = w.reshape(k, N_HEADS, width)
    return jnp.pad(w3, ((0, 0), (0, 0), (0, LANES - width))).reshape(k, HW)


def _unpad_heads(g, width):
    k = g.shape[0]
    return g.reshape(k, N_HEADS, LANES)[:, :, :width].reshape(k, N_HEADS * width)


def _pad_win(w):
    d = w.shape[0]
    z = lambda n: jnp.zeros((d, n), w.dtype)
    return jnp.concatenate([w[:, :OFF_KV], z(KR_LANE), w[:, OFF_KV:OFF_KR], z(LANES - KR_LANE - QK_ROPE), w[:, OFF_KR:]],
                           axis=1)


def _unpad_win(g):
    return jnp.concatenate([g[:, :P_KR], g[:, P_KR + KR_LANE:P_KR + KR_LANE + QK_ROPE], g[:, P_GLU:]], axis=1)


def _local_step(x, target, mod, sp, w, tm=256):
    nseq, seq, d = x.shape
    t = nseq * seq
    tps = seq // tm
    xf = x.reshape(t, d)
    tg = target.reshape(t, d)
    m3 = mod.reshape(nseq, 6, 1, d)
    shift1, scale1, gate1, shift2, scale2, gate2 = (m3[:, j] for j in range(6))

    win_p = _pad_win(w["w_in"])
    wint_p = win_p.T
    wuq_p = _pad_heads(w["w_uq"], QK_HEAD)
    wkv3 = w["w_ukv"].reshape(KV_RANK, N_HEADS, QK_NOPE + V_HEAD)
    wk_p = _pad_heads(wkv3[:, :, :QK_NOPE].reshape(KV_RANK, -1), QK_NOPE)
    wv_p = _pad_heads(wkv3[:, :, QK_NOPE:].reshape(KV_RANK, -1), V_HEAD)
    wo_p = jnp.pad(w["w_o_mla"].reshape(N_HEADS, V_HEAD, d), ((0, 0), (0, LANES - V_HEAD), (0, 0))).reshape(HW, d)
    cw = jnp.pad(w["conv_w"], ((0, HALO - CONV_W), (0, 0)))
    pad_g = lambda g: jnp.pad(g, ((0, 0), (0, LANES - QK_HEAD)))
    gq, gk = pad_g(sp["qk_norm_q_g"]), pad_g(sp["qk_norm_k_g"])
    tabs = _rope_tables(seq)

    h, zq, zkv, zkr, zglu, zgate, u0 = _fwd_in(xf, sp["norm1_g"], scale1, shift1, win_p, tm, tps)
    q, k, v, qln, kvn = _mla_prep(zq, zkv, zkr, sp["q_latent_g"], sp["kv_latent_g"], gq, gk, tabs, wuq_p, wk_p, wv_p,
                                  tm, tps)
    attn = _attn_fwd(q, k, v, nseq, seq)
    x1, mixed, mpre, ya, yb, u1, u3 = _fwd_mix(attn, u0, zgate, xf, gate1, wo_p, cw, sp["conv_b"], sp["conv_ln_g"],
                                               sp["conv_ln_b"], w["w_pw_out"], w["w_out"], tm, tps)
    h2, a, r, dy, df, dgate2, loss_acc = _fwd_ffn(x1, tg, sp["norm2_g"], scale2, shift2, gate2, w["w_ff1"], w["w_ff2"],
                                                   tm, tps)
    da, dx1, dmixed, dshift2, dscale2, dgate1, dg2 = _bwd_ffn(df, a, x1, dy, mixed, sp["norm2_g"], scale2, gate1,
                                                              w["w_ff2"].T, w["w_ff1"].T, tm, tps)
    dya, dyb, dzgate, do, du1, dlng, dlnb, dcb = _bwd_mix(dmixed, zgate, ya, yb, u1, sp["conv_ln_g"], sp["conv_ln_b"],
                                                          w["w_out"].T, wo_p.T, w["w_pw_out"].T, tm)
    dzglu, dcw = _bwd_conv(du1, u0, zglu, cw, tm, tps)
    dq, dk, dv = _attn_bwd(q, k, v, do, nseq, seq)
    dqpre, dkh, dzq, dzkv, dzkr, dgq, dgk, dgql, dgkvl = _mla_bwd(
        dq, dk, dv, zq, zkv, zkr, sp["q_latent_g"], sp["kv_latent_g"], gq, gk, tabs, wuq_p, wk_p, wuq_p.T, wk_p.T,
        wv_p.T, tm, tps)
    gx, dshift1, dscale1, dg1 = _bwd_in(dzq, dzkv, dzkr, dzglu, dzgate, xf, dx1, sp["norm1_g"], scale1, wint_p, tm, tps)

    dwin_p = jnp.concatenate([_tn_matmul(h, dz, "dw_in_%d" % j)
                              for j, dz in enumerate((dzq, dzkv, dzkr, dzglu, dzgate))], axis=1)
    dwk_p = _tn_matmul(kvn, dkh, "dw_uk")
    dwv_p = _tn_matmul(kvn, dv, "dw_uv")
    dwkv = jnp.concatenate([dwk_p.reshape(KV_RANK, N_HEADS, LANES)[:, :, :QK_NOPE],
                            dwv_p.reshape(KV_RANK, N_HEADS, LANES)[:, :, :V_HEAD]], axis=2).reshape(KV_RANK, -1)
    dwo_p = _tn_matmul(attn, dya, "dw_o")
    gw = {
        "w_in": _unpad_win(dwin_p),
        "w_uq": _unpad_heads(_tn_matmul(qln, dqpre, "dw_uq"), QK_HEAD),
        "w_ukv": dwkv,
        "w_o_mla": dwo_p.reshape(N_HEADS, LANES, d)[:, :V_HEAD].reshape(MLA_WIDTH, d),
        "conv_w": dcw[:CONV_W],
        "w_pw_out": _tn_matmul(u3, dyb, "dw_pw"),
        "w_out": _tn_matmul(mpre, dmixed, "dw_out"),
        "w_ff1": _tn_matmul(h2, da, "dw_ff1"),
        "w_ff2": _tn_matmul(r, df, "dw_ff2"),
    }
    gs = {
        "norm1_g": dg1, "q_latent_g": dgql, "kv_latent_g": dgkvl, "qk_norm_q_g": dgq[:, :QK_HEAD],
        "qk_norm_k_g": dgk[:, :QK_HEAD], "conv_b": dcb, "conv_ln_g": dlng, "conv_ln_b": dlnb, "norm2_g": dg2,
    }
    dmod = jnp.concatenate([dshift1, dscale1, dgate1, dshift2, dscale2, dgate2], axis=2).reshape(nseq, 6 * d)
    return loss_acc[0, 0], gx.reshape(nseq, seq, d), dmod, gw, gs


SHARDED = ("w_in", "w_uq", "w_ukv", "w_o_mla", "conv_w", "w_pw_out", "w_out", "w_ff1", "w_ff2")
ROW_SHARDED = ("w_out", "w_ff2")
SMALL = ("norm1_g", "q_latent_g", "kv_latent_g", "qk_norm_q_g", "qk_norm_k_g", "conv_b", "conv_ln_g", "conv_ln_b",
         "norm2_g")
WEIGHTS = ("w_ada", "b_ada", "norm1_g", "w_in", "q_latent_g", "w_uq", "kv_latent_g", "w_ukv", "qk_norm_q_g",
           "qk_norm_k_g", "w_o_mla", "conv_w", "conv_b", "conv_ln_g", "conv_ln_b", "w_pw_out", "w_out", "norm2_g",
           "w_ff1", "w_ff2")
N_SHARD = 4
PACK_W = 1024


def _ceil_to(n, m):
    return -(-n // m) * m


def _pack(arrs, rows, width, dtype):
    flat = jnp.concatenate([a.reshape(-1).astype(dtype) for a in arrs])
    return jnp.pad(flat, (0, rows * width - flat.shape[0])).reshape(rows, width)


def _unpack(buf, shapes):
    flat = buf.reshape(-1)
    out, o = [], 0
    for s in shapes:
        n = 1
        for dim in s:
            n *= dim
        out.append(flat[o:o + n].reshape(s))
        o += n
    return out


def _f32_bits_as_bf16(a):
    return lax.bitcast_convert_type(a, BF16).reshape(-1)


def _bf16_bits_as_f32(flat, shape):
    return lax.bitcast_convert_type(flat.reshape(shape + (2,)), F32)


def _shard_of(full, name, s):
    if name in ROW_SHARDED:
        n = full.shape[0] // N_SHARD
        return full[s * n:(s + 1) * n]
    n = full.shape[1] // N_SHARD
    return full[:, s * n:(s + 1) * n]


def kernel(x, c, w_ada, b_ada, norm1_g, w_in, q_latent_g, w_uq, kv_latent_g, w_ukv, qk_norm_q_g, qk_norm_k_g, w_o_mla, conv_w, conv_b, conv_ln_g, conv_ln_b, w_pw_out, w_out, norm2_g, w_ff1, w_ff2, loss_target, m_w_ada, m_b_ada, m_norm1_g, m_w_in, m_q_latent_g, m_w_uq, m_kv_latent_g, m_w_ukv, m_qk_norm_q_g, m_qk_norm_k_g, m_w_o_mla, m_conv_w, m_conv_b, m_conv_ln_g, m_conv_ln_b, m_w_pw_out, m_w_out, m_norm2_g, m_w_ff1, m_w_ff2, v_w_ada, v_b_ada, v_norm1_g, v_w_in, v_q_latent_g, v_w_uq, v_kv_latent_g, v_w_ukv, v_qk_norm_q_g, v_qk_norm_k_g, v_w_o_mla, v_conv_w, v_conv_b, v_conv_ln_g, v_conv_ln_b, v_w_pw_out, v_w_out, v_norm2_g, v_w_ff1, v_w_ff2):
    given = dict(locals())
    wts = {n: given[n][0] for n in WEIGHTS}
    mom = {n: given["m_" + n][0] for n in WEIGHTS}
    var = {n: given["v_" + n][0] for n in WEIGHTS}
    nseq, seq, d = x.shape
    ix, iy, ic = lax.axis_index("x"), lax.axis_index("y"), lax.axis_index("c")
    shard = 2 * ix + iy
    dev = 4 * ix + 2 * iy + ic

    shard_shapes = [wts[n].shape for n in SHARDED]
    n_el = sum(a * b for a, b in shard_shapes) + wts["conv_w"].size
    rows = _ceil_to(-(-n_el // PACK_W), 32)
    half = rows // 2
    pieces = [_f32_bits_as_bf16(wts[n]) if n == "conv_w" else wts[n] for n in SHARDED]
    wpack = _pack(pieces, rows, PACK_W, BF16)
    mine = lax.dynamic_slice_in_dim(wpack, ic * half, half, axis=0)
    c_rows = _ceil_to(nseq * d * 2 // PACK_W, 16)
    cbits = _pack([_f32_bits_as_bf16(c)], c_rows, PACK_W, BF16)
    gathered = _all_gather8(jnp.concatenate([mine, cbits], axis=0), "gather_weights")
    c_all = _bf16_bits_as_f32(gathered[:, half:half + nseq * d * 2 // PACK_W].reshape(-1), (8 * nseq, d))
    full = {}
    per_shard = []
    for s in range(N_SHARD):
        buf = gathered[2 * s:2 * s + 2, :half].reshape(-1)
        parts, o = [], 0
        for n, shp in zip(SHARDED, shard_shapes):
            cnt = shp[0] * shp[1] * (2 if n == "conv_w" else 1)
            seg = buf[o:o + cnt]
            parts.append(_bf16_bits_as_f32(seg, shp) if n == "conv_w" else seg.reshape(shp))
            o += cnt
        per_shard.append(parts)
    for j, n in enumerate(SHARDED):
        full[n] = jnp.concatenate([per_shard[s][j] for s in range(N_SHARD)], axis=0 if n in ROW_SHARDED else 1)

    n_ada = wts["w_ada"].shape[1]
    b_sh = lax.dynamic_slice_in_dim(wts["b_ada"].reshape(1, -1), shard * n_ada, n_ada, axis=1)
    mod_sh = _ada_mod(c_all, wts["w_ada"], b_sh)
    hb = 4 * nseq
    mod_blk = lax.dynamic_slice_in_dim(mod_sh, ic * hb, hb, axis=0)
    mod_all = _all_gather8(mod_blk, "gather_mod")
    mod_mine = lax.dynamic_slice_in_dim(mod_all, (2 * iy + ic) * nseq, nseq, axis=1)
    mod = jnp.concatenate([lax.dynamic_index_in_dim(mod_mine, 2 * s + ix, axis=0, keepdims=False)
                           for s in range(N_SHARD)], axis=1)

    sp = {n: wts[n].reshape(1, -1) for n in SMALL}
    loss_part, grad_x, dmod, gw, gs = _local_step(x, loss_target, mod, sp, full)
    loss = lax.psum(loss_part, ("x", "y", "c"))

    small_rows = [jnp.pad(gs[n].reshape(-1), (0, _ceil_to(gs[n].size, LANES) - gs[n].size)) for n in SMALL]
    small_flat = jnp.concatenate([dmod.reshape(-1)] + small_rows)
    n_dm = dmod.size // LANES
    r_small = _ceil_to(small_flat.shape[0] // LANES, 8)
    blk = jnp.pad(small_flat, (0, r_small * LANES - small_flat.shape[0])).reshape(r_small, LANES)
    parts = _all_gather8(blk, "gather_small")
    dmod_all = parts[:, :n_dm].reshape(8 * nseq, 6 * d)
    dmod_sh = lax.dynamic_slice_in_dim(dmod_all, shard * n_ada, n_ada, axis=1)
    g_w_ada, g_b_ada, small_sum = _ada_bwd(c_all, dmod_all, dmod_sh, parts[:, n_dm:])
    grads = {"w_ada": g_w_ada, "b_ada": g_b_ada.reshape(-1)}
    flat = small_sum.reshape(-1)
    o = 0
    for n in SMALL:
        grads[n] = flat[o:o + wts[n].size]
        o += _ceil_to(wts[n].size, LANES)

    g_rows = _ceil_to(-(-sum(a * b for a, b in shard_shapes) // PACK_W), 16)
    g_half = g_rows // 2
    gpack = jnp.stack([_pack([_shard_of(gw[n], n, s) for n in SHARDED], g_rows, PACK_W, F32)
                       for s in range(N_SHARD)])
    g2 = gpack.reshape(N_SHARD, 2, g_half, PACK_W).transpose(1, 0, 2, 3)
    from_sibling = _sibling_swap_other_half(g2, "grad_pair_swap")
    tr = g_half // 10 if g_half % 80 == 0 else 8
    pair = _add_pair(g2, from_sibling, ic.reshape(1).astype(jnp.int32), tr)
    from_chips = _chip_scatter(pair, "grad_chip_scatter")
    mine_sum = _add_chips(pair, from_chips, shard.reshape(1).astype(jnp.int32), tr)
    gsum = _sibling_all_gather(mine_sum, "grad_pair_gather").reshape(g_rows, PACK_W)
    for n, g in zip(SHARDED, _unpack(gsum, shard_shapes)):
        grads[n] = g

    delta, new_m, new_v = {}, {}, {}
    tr_a = g_rows // 20 if g_rows % 160 == 0 else 8
    res = _adamw(*[_pack([t[n] for n in SHARDED], g_rows, PACK_W, F32) for t in (wts, grads, mom, var)], tr_a,
                 "adamw_sharded")
    for dst, buf in zip((delta, new_m, new_v), res):
        for n, a in zip(SHARDED, _unpack(buf, shard_shapes)):
            dst[n] = a
    res = _adamw(wts["w_ada"], grads["w_ada"], mom["w_ada"], var["w_ada"], 128, "adamw_ada")
    for dst, buf in zip((delta, new_m, new_v), res):
        dst["w_ada"] = buf
    rep = ("b_ada",) + SMALL
    rep_shapes = [(wts[n].size,) for n in rep]
    r_rep = _ceil_to(-(-sum(s[0] for s in rep_shapes) // LANES), 8)
    res = _adamw(*[_pack([t[n] for n in rep], r_rep, LANES, F32) for t in (wts, grads, mom, var)], r_rep,
                 "adamw_replicated")
    for dst, buf in zip((delta, new_m, new_v), res):
        for n, a in zip(rep, _unpack(buf, rep_shapes)):
            dst[n] = a

    outs = [loss, grad_x]
    for group in (grads, delta, new_m, new_v):
        outs += [group[n].reshape(given[n].shape) for n in WEIGHTS]
    return tuple(outs)
```

```python
import jax
import jax.numpy as jnp
from jax import lax
from jax.experimental import pallas as pl
from jax.experimental.pallas import tpu as pltpu

F32 = jnp.float32
BF16 = jnp.bfloat16
MESH = pl.DeviceIdType.MESH
ANY = pl.BlockSpec(memory_space=pl.ANY)

CHUNK = 64
CHUNK_SHIFT = 6
N_HEADS = 8
QK_NOPE = 64
QK_ROPE = 32
QK_HEAD = QK_NOPE + QK_ROPE
V_HEAD = 64
Q_RANK = 256
KV_RANK = 128
MLA_WIDTH = N_HEADS * V_HEAD
CONV_CH = 512
CONV_W = 31
ROPE_THETA = 10000.0
EPS = 1e-6
LANES = 128
HW = N_HEADS * LANES
OFF_KV = Q_RANK + KV_RANK
OFF_KR = OFF_KV + QK_ROPE
OFF_GLU = OFF_KR + 2 * CONV_CH
KR_LANE = QK_NOPE
MLA_IN = Q_RANK + KV_RANK + LANES
HALO = 32
N_MOD = 6

ADAM_LR = 0.001
ADAM_B1 = 0.9
ADAM_B2 = 0.999
ADAM_EPS = 1e-08
ADAM_WD = 0.01
ADAM_STEP = 10

VMEM_LIMIT = 56 * 1024 * 1024
BQ = 256


def _layout(d):
    p_glu = 2 * d
    p_q = p_glu + 2 * CONV_CH
    return p_glu, p_q, p_q + MLA_IN


def _params(*sem):
    return pltpu.CompilerParams(dimension_semantics=sem, vmem_limit_bytes=VMEM_LIMIT)


def _dot(a, b):
    return jnp.dot(a, b, preferred_element_type=F32)


def _dot_tn(a, b):
    return lax.dot_general(a, b, (((0,), (0,)), ((), ())), preferred_element_type=F32)


def _dot_nt(a, b):
    return lax.dot_general(a, b, (((1,), (1,)), ((), ())), preferred_element_type=F32)


def _acc(ref, val, first):
    @pl.when(first)
    def _():
        ref[...] = val

    @pl.when(jnp.logical_not(first))
    def _():
        ref[...] += val


def _rms(x):
    r = lax.rsqrt(jnp.mean(x * x, axis=-1, keepdims=True) + EPS)
    return x * r, r


def _rms_bwd(n, r, dn):
    return r * (dn - n * jnp.mean(dn * n, axis=-1, keepdims=True))


def _head_rms(sl):
    r = lax.rsqrt(jnp.sum(sl * sl, axis=-1, keepdims=True) * (1.0 / QK_HEAD) + EPS)
    return sl * r, r


def _head_rms_bwd(n, r, dn):
    return r * (dn - n * (jnp.sum(dn * n, axis=-1, keepdims=True) * (1.0 / QK_HEAD)))


def _rope(x, c, s1, s2):
    return x * c + pltpu.roll(x, QK_ROPE // 2, 1) * s1 + pltpu.roll(x, LANES - QK_ROPE // 2, 1) * s2


def _rope_t(dy, c, s1, s2):
    return dy * c + pltpu.roll(dy * s1, LANES - QK_ROPE // 2, 1) + pltpu.roll(dy * s2, QK_ROPE // 2, 1)


def _rope_tables(seq):
    half = QK_ROPE // 2
    inv_freq = ROPE_THETA ** (-jnp.arange(0, QK_ROPE, 2, dtype=F32) / QK_ROPE)
    ang = jnp.arange(seq, dtype=F32)[:, None] * inv_freq[None, :]
    cos, sin = jnp.cos(ang), jnp.sin(ang)
    z = lambda n: jnp.zeros((seq, n), F32)
    tail = LANES - QK_HEAD
    c = jnp.concatenate([jnp.ones((seq, QK_NOPE), F32), cos, cos, jnp.ones((seq, tail), F32)], axis=1)
    s1 = jnp.concatenate([z(QK_NOPE + half), sin, z(tail)], axis=1)
    s2 = jnp.concatenate([z(QK_NOPE), -sin, z(half + tail)], axis=1)
    return c, s1, s2


def _row(tm, w):
    return pl.BlockSpec((tm, w), lambda i: (i, 0))


def _modspec(d, tps):
    return pl.BlockSpec((None, N_MOD, d), lambda i: (i // tps, 0, 0))


def _seqv(w, tps):
    return pl.BlockSpec((None, 1, w), lambda i: (i // tps, 0, 0))


def _full(shape):
    return pl.BlockSpec(shape, lambda i: tuple(0 for _ in shape))


def _sds(shape, dtype):
    return jax.ShapeDtypeStruct(shape, dtype)


def _load_resident(i, pairs):
    @pl.when(i == 0)
    def _():
        for src, dst in pairs:
            pltpu.sync_copy(src, dst)


def _place():
    return lax.axis_index("x"), lax.axis_index("y"), lax.axis_index("c")


def _all_gather8(blocks, name):
    na = len(blocks)

    def body(*refs):
        x_refs, out_refs = refs[:na], refs[na:2 * na]
        send_sems, recv_sems, local_sems = refs[2 * na:]
        x, y, c = _place()
        me, sibling = (x, y, c), (x, y, 1 - c)
        chips = [(1 - x, y), (x, 1 - y), (1 - x, 1 - y)]

        def copy(a, k, blk, to, from_input=False):
            dst = out_refs[a].at[4 * blk[0] + 2 * blk[1] + blk[2]]
            return pltpu.make_async_remote_copy(
                src_ref=x_refs[a] if from_input else dst, dst_ref=dst,
                send_sem=send_sems.at[7 * a + k], recv_sem=recv_sems.at[7 * a + k], device_id=to, device_id_type=MESH)

        mine = [pltpu.make_async_copy(x_refs[a], out_refs[a].at[4 * x + 2 * y + c], local_sems.at[a]) for a in range(na)]
        for cp in mine:
            cp.start()
        first = []
        for a in range(na):
            first.append(copy(a, 0, me, sibling, True))
            first += [copy(a, 1 + j, me, (*chip, c), True) for j, chip in enumerate(chips)]
        for cp in first:
            cp.start()
        passed = []
        for j, chip in enumerate(chips):
            for a in range(na):
                copy(a, 1 + j, (*chip, c), me).wait_recv()
                fwd = copy(a, 4 + j, (*chip, c), sibling)
                fwd.start()
                passed.append(fwd)
        for a in range(na):
            copy(a, 0, sibling, me).wait_recv()
            for j, chip in enumerate(chips):
                copy(a, 4 + j, (*chip, 1 - c), me).wait_recv()
        for cp in first + passed:
            cp.wait_send()
        for cp in mine:
            cp.wait()

    return pl.pallas_call(
        body, name=name, out_shape=tuple(_sds((8,) + b.shape, b.dtype) for b in blocks),
        in_specs=[ANY] * na, out_specs=(ANY,) * na,
        scratch_shapes=[pltpu.SemaphoreType.DMA((7 * na,)), pltpu.SemaphoreType.DMA((7 * na,)),
                        pltpu.SemaphoreType.DMA((na,))],
    )(*blocks)


def _pair_swap(gs, name):
    na = len(gs)
    ns = gs[0].shape[0]

    def body(*refs):
        g_refs, land_refs = refs[:na], refs[na:2 * na]
        send_sems, recv_sems = refs[2 * na:]
        x, y, c = _place()
        cps = [pltpu.make_async_remote_copy(
            src_ref=g_refs[a].at[s, 1 - c], dst_ref=land_refs[a].at[s], send_sem=send_sems.at[ns * a + s],
            recv_sem=recv_sems.at[ns * a + s], device_id=(x, y, 1 - c), device_id_type=MESH)
            for a in range(na) for s in range(ns)]
        for cp in cps:
            cp.start()
        for cp in cps:
            cp.wait()

    return pl.pallas_call(
        body, name=name, out_shape=tuple(_sds((ns,) + g.shape[2:], g.dtype) for g in gs),
        in_specs=[ANY] * na, out_specs=(ANY,) * na,
        scratch_shapes=[pltpu.SemaphoreType.DMA((ns * na,)), pltpu.SemaphoreType.DMA((ns * na,))],
    )(*gs)


def _chip_scatter(hs, name):
    na = len(hs)

    def body(*refs):
        h_refs, land_refs = refs[:na], refs[na:2 * na]
        send_sems, recv_sems = refs[2 * na:]
        x, y, c = _place()
        chips = [(1 - x, y), (x, 1 - y), (1 - x, 1 - y)]
        cps = [pltpu.make_async_remote_copy(
            src_ref=h_refs[a].at[2 * tx + ty], dst_ref=land_refs[a].at[j], send_sem=send_sems.at[3 * a + j],
            recv_sem=recv_sems.at[3 * a + j], device_id=(tx, ty, c), device_id_type=MESH)
            for a in range(na) for j, (tx, ty) in enumerate(chips)]
        for cp in cps:
            cp.start()
        for cp in cps:
            cp.wait()

    return pl.pallas_call(
        body, name=name, out_shape=tuple(_sds((3,) + h.shape[1:], h.dtype) for h in hs),
        in_specs=[ANY] * na, out_specs=(ANY,) * na,
        scratch_shapes=[pltpu.SemaphoreType.DMA((3 * na,)), pltpu.SemaphoreType.DMA((3 * na,))],
    )(*hs)


def _pair_gather(fs, name):
    na = len(fs)

    def body(*refs):
        f_refs, out_refs = refs[:na], refs[na:2 * na]
        send_sems, recv_sems, local_sems = refs[2 * na:]
        x, y, c = _place()
        mine = [pltpu.make_async_copy(f_refs[a], out_refs[a].at[c], local_sems.at[a]) for a in range(na)]
        sends = [pltpu.make_async_remote_copy(
            src_ref=f_refs[a], dst_ref=out_refs[a].at[c], send_sem=send_sems.at[a], recv_sem=recv_sems.at[a],
            device_id=(x, y, 1 - c), device_id_type=MESH) for a in range(na)]
        recvs = [pltpu.make_async_remote_copy(
            src_ref=f_refs[a], dst_ref=out_refs[a].at[1 - c], send_sem=send_sems.at[a], recv_sem=recv_sems.at[a],
            device_id=(x, y, 1 - c), device_id_type=MESH) for a in range(na)]
        for cp in mine + sends:
            cp.start()
        for cp in recvs:
            cp.wait_recv()
        for cp in sends:
            cp.wait_send()
        for cp in mine:
            cp.wait()

    return pl.pallas_call(
        body, name=name, out_shape=tuple(_sds((2,) + f.shape, f.dtype) for f in fs),
        in_specs=[ANY] * na, out_specs=(ANY,) * na,
        scratch_shapes=[pltpu.SemaphoreType.DMA((na,)), pltpu.SemaphoreType.DMA((na,)), pltpu.SemaphoreType.DMA((na,))],
    )(*fs)


def _row_tile(r, n, itemsize=4, budget=1 << 20):
    if r * n * itemsize <= budget:
        return r
    best = None
    for tr in range(16, r, 16):
        if r % tr == 0 and tr * n * itemsize <= budget:
            best = tr
    assert best is not None, (r, n)
    return best


def _add_pair(g, land, cidx, name):
    ns, _, r, n = g.shape
    tr = _row_tile(r, n)

    def body(c_ref, a_ref, b_ref, o_ref, ob_ref):
        s = a_ref[...] + b_ref[...]
        o_ref[...] = s
        ob_ref[...] = s.astype(BF16)

    out = pl.BlockSpec((None, tr, n), lambda s, i, cr: (s, i, 0))
    return pl.pallas_call(
        body, name=name, out_shape=(_sds((ns, r, n), F32), _sds((ns, r, n), BF16)),
        grid_spec=pltpu.PrefetchScalarGridSpec(
            num_scalar_prefetch=1, grid=(ns, r // tr),
            in_specs=[pl.BlockSpec((None, None, tr, n), lambda s, i, cr: (s, cr[0], i, 0)), out],
            out_specs=(out, out)),
        compiler_params=_params("arbitrary", "arbitrary"),
    )(cidx, g, land)


def _add_chips(h, land, own, name):
    _, r, n = h.shape
    tr = _row_tile(r, n)

    def body(o_idx, h_ref, l_ref, o_ref):
        o_ref[...] = ((h_ref[...] + l_ref[0].astype(F32)) + l_ref[1].astype(F32)) + l_ref[2].astype(F32)

    return pl.pallas_call(
        body, name=name, out_shape=_sds((r, n), F32),
        grid_spec=pltpu.PrefetchScalarGridSpec(
            num_scalar_prefetch=1, grid=(r // tr,),
            in_specs=[pl.BlockSpec((None, tr, n), lambda i, o: (o[0], i, 0)),
                      pl.BlockSpec((3, tr, n), lambda i, o: (0, i, 0))],
            out_specs=pl.BlockSpec((tr, n), lambda i, o: (i, 0))),
        compiler_params=_params("arbitrary"),
    )(own, h, land)


def _adam_math(w, g, m, v):
    nm = ADAM_B1 * m + (1.0 - ADAM_B1) * g
    nv = ADAM_B2 * v + (1.0 - ADAM_B2) * (g * g)
    m_hat = nm / (1.0 - ADAM_B1 ** ADAM_STEP)
    v_hat = nv / (1.0 - ADAM_B2 ** ADAM_STEP)
    return -ADAM_LR * (m_hat / (jnp.sqrt(v_hat) + ADAM_EPS) + ADAM_WD * w), nm, nv


def _adamw(w, g, m, v, name):
    r, n = w.shape
    tr = _row_tile(r, n, budget=1 << 19)

    def body(w_ref, g_ref, m_ref, v_ref, d_ref, nm_ref, nv_ref):
        d_ref[...], nm_ref[...], nv_ref[...] = _adam_math(w_ref[...], g_ref[...], m_ref[...], v_ref[...])

    spec = pl.BlockSpec((tr, n), lambda i: (i, 0))
    return pl.pallas_call(
        body, name=name, out_shape=(_sds((r, n), F32),) * 3, grid=(r // tr,),
        in_specs=[spec] * 4, out_specs=(spec,) * 3, compiler_params=_params("arbitrary"),
    )(w, g, m, v)


def _adamw_small(ws, gs, ms, vs):
    k = len(ws)

    def body(*refs):
        ins, outs = refs[:4 * k], refs[4 * k:]
        for j in range(k):
            d, nm, nv = _adam_math(ins[j][...], ins[k + j][...], ins[2 * k + j][...], ins[3 * k + j][...])
            outs[j][...] = d
            outs[k + j][...] = nm
            outs[2 * k + j][...] = nv

    shapes = tuple(_sds(w.shape, F32) for w in ws)
    res = pl.pallas_call(body, name="adamw_small", out_shape=shapes * 3,
                         compiler_params=pltpu.CompilerParams(vmem_limit_bytes=VMEM_LIMIT))(*ws, *gs, *ms, *vs)
    return res[:k], res[k:2 * k], res[2 * k:]


def _ada_mod(c_all, w_sh, b_sh):
    b, _ = c_all.shape
    n = w_sh.shape[1]

    def body(c_ref, w_ref, b_ref, o_ref):
        cc = c_ref[...]
        ca = (cc * jax.nn.sigmoid(cc)).astype(BF16)
        o_ref[...] = _dot(ca, w_ref[...].astype(BF16)) + b_ref[...]

    return pl.pallas_call(body, name="ada_mod", out_shape=_sds((b, n), F32),
                          compiler_params=pltpu.CompilerParams(vmem_limit_bytes=VMEM_LIMIT))(c_all, w_sh, b_sh)


def _ada_bwd(c_all, dmod_all, dmod_sh, parts):
    b, d = c_all.shape
    n6 = dmod_all.shape[1]
    n = dmod_sh.shape[1]
    k = len(parts)

    def body(*refs):
        c_ref, da_ref, ds_ref = refs[:3]
        p_refs = refs[3:3 + k]
        dw_ref, db_ref = refs[3 + k:5 + k]
        s_refs = refs[5 + k:]
        cc = c_ref[...]
        ca = (cc * jax.nn.sigmoid(cc)).astype(BF16)
        dw_ref[...] = _dot_tn(ca, ds_ref[...].astype(BF16))
        db_ref[...] = jnp.sum(da_ref[...], axis=0, keepdims=True)
        for p_ref, s_ref in zip(p_refs, s_refs):
            tot = p_ref[0]
            for j in range(1, p_ref.shape[0]):
                tot = tot + p_ref[j]
            s_ref[...] = tot

    return pl.pallas_call(
        body, name="ada_bwd",
        out_shape=(_sds((d, n), F32), _sds((1, n6), F32)) + tuple(_sds(p.shape[1:], F32) for p in parts),
        compiler_params=pltpu.CompilerParams(vmem_limit_bytes=VMEM_LIMIT),
    )(c_all, dmod_all, dmod_sh, *parts)


def _fwd_in(x, g1, mod3, win_p, tm, tps):
    t, d = x.shape
    p_glu, p_q, npad = _layout(d)

    def body(x_ref, g_ref, mod_ref, w_hbm, h_ref, zm_ref, zglu_ref, zgate_ref, u0_ref, w_ref):
        _load_resident(pl.program_id(0), [(w_hbm, w_ref)])
        n, _ = _rms(x_ref[...])
        h = ((n * g_ref[...]) * (1.0 + mod_ref[1:2, :]) + mod_ref[0:1, :]).astype(BF16)
        h_ref[...] = h
        z = _dot(h, w_ref[...])
        zgate_ref[...] = z[:, :p_glu]
        zglu = z[:, p_glu:p_q]
        zglu_ref[...] = zglu
        zm_ref[...] = z[:, p_q:]
        u0_ref[...] = zglu[:, :CONV_CH] * jax.nn.sigmoid(zglu[:, CONV_CH:])

    return pl.pallas_call(
        body, name="fwd_in", grid=(t // tm,),
        out_shape=(_sds((t, d), BF16), _sds((t, MLA_IN), F32), _sds((t, 2 * CONV_CH), F32), _sds((t, 2 * d), F32),
                   _sds((t, CONV_CH), F32)),
        in_specs=[_row(tm, d), _full((1, d)), _modspec(d, tps), ANY],
        out_specs=(_row(tm, d), _row(tm, MLA_IN), _row(tm, 2 * CONV_CH), _row(tm, 2 * d), _row(tm, CONV_CH)),
        scratch_shapes=[pltpu.VMEM(win_p.shape, BF16)],
        compiler_params=_params("arbitrary"),
    )(x, g1, mod3, win_p)


def _mla_prep(zm, gql, gkvl, gq, gk, tabs, wuq_p, wk_p, wv_p, tm, tps):
    t = zm.shape[0]
    c_t, s1_t, s2_t = tabs
    tab = pl.BlockSpec((tm, LANES), lambda i: (i % tps, 0))

    def body(zm_ref, gql_ref, gkvl_ref, gq_ref, gk_ref, c_ref, s1_ref, s2_ref, wuq_ref, wk_ref, wv_ref,
             q_ref, k_ref, v_ref, qln_ref, kvn_ref):
        c, s1, s2 = c_ref[...], s1_ref[...], s2_ref[...]
        nq, _ = _rms(zm_ref[:, :Q_RANK])
        qln = (nq * gql_ref[...]).astype(BF16)
        qln_ref[...] = qln
        qpre = _dot(qln, wuq_ref[...])
        nkv, _ = _rms(zm_ref[:, Q_RANK:OFF_KV])
        kvn = (nkv * gkvl_ref[...]).astype(BF16)
        kvn_ref[...] = kvn
        knope = _dot(kvn, wk_ref[...])
        v_ref[...] = _dot(kvn, wv_ref[...]).astype(BF16)
        zkr_v = zm_ref[:, OFF_KV:]
        for hd in range(N_HEADS):
            sl = slice(hd * LANES, (hd + 1) * LANES)
            n, _ = _head_rms(qpre[:, sl])
            q_ref[:, sl] = _rope(n * gq_ref[...], c, s1, s2).astype(BF16)
            n, _ = _head_rms(knope[:, sl] + zkr_v)
            k_ref[:, sl] = _rope(n * gk_ref[...], c, s1, s2).astype(BF16)

    return pl.pallas_call(
        body, name="mla_prep", grid=(t // tm,),
        out_shape=(_sds((t, HW), BF16),) * 3 + (_sds((t, Q_RANK), BF16), _sds((t, KV_RANK), BF16)),
        in_specs=[_row(tm, MLA_IN), _full((1, Q_RANK)), _full((1, KV_RANK)),
                  _full((1, LANES)), _full((1, LANES)), tab, tab, tab,
                  _full(wuq_p.shape), _full(wk_p.shape), _full(wv_p.shape)],
        out_specs=(_row(tm, HW),) * 3 + (_row(tm, Q_RANK), _row(tm, KV_RANK)),
        compiler_params=_params("arbitrary"),
    )(zm, gql, gkvl, gq, gk, c_t, s1_t, s2_t, wuq_p, wk_p, wv_p)


def _scores(q_i, k_e, i, e):
    s = _dot_nt(q_i, k_e) * (QK_HEAD ** -0.5)
    rc = jnp.right_shift(lax.broadcasted_iota(jnp.int32, (BQ, 1), 0) + i * BQ, CHUNK_SHIFT)
    cc = jnp.right_shift(lax.broadcasted_iota(jnp.int32, (1, e), 1), CHUNK_SHIFT)
    s = jnp.where(rc >= cc, s, jnp.finfo(F32).min)
    p = jnp.exp(s - jnp.max(s, axis=-1, keepdims=True))
    return p / jnp.sum(p, axis=-1, keepdims=True)


def _attn_fwd(q, k, v, nseq, seq):
    t = q.shape[0]
    blk = pl.BlockSpec((seq, LANES), lambda b, h: (b, h))

    def body(q_ref, k_ref, v_ref, o_ref):
        for i in range(seq // BQ):
            e = (i + 1) * BQ
            p = _scores(q_ref[i * BQ:e, :], k_ref[:e, :], i, e)
            o_ref[i * BQ:e, :] = _dot(p.astype(BF16), v_ref[:e, :]).astype(BF16)

    return pl.pallas_call(
        body, name="attn_fwd", grid=(nseq, N_HEADS), out_shape=_sds((t, HW), BF16),
        in_specs=[blk, blk, blk], out_specs=blk, compiler_params=_params("arbitrary", "arbitrary"),
    )(q, k, v)


def _fwd_mix(attn, u0, zgate, x, mod3, wo_p, cw, cb, lng, lnb, wpw, wout, tm, tps):
    t, d = x.shape
    hpt = tm // HALO

    def body(a_ref, u_ref, uh_ref, zg_ref, x_ref, mod_ref, wo_ref, cw_ref, cb_ref, lng_ref, lnb_ref, wpw_ref, wout_ref,
             x1_ref, mixed_ref, mpre_ref, ya_ref, yb_ref, u1_ref, u3_ref, ext_ref):
        i = pl.program_id(0)
        ya = _dot(a_ref[...], wo_ref[...])
        ya_ref[...] = ya
        first = (i % tps) == 0
        ext_ref[:HALO, :] = jnp.where(first, 0.0, uh_ref[...])
        ext_ref[HALO:, :] = u_ref[...]
        acc = jnp.zeros((tm, CONV_CH), F32) + cb_ref[...]
        for kk in range(CONV_W):
            o = HALO - (CONV_W - 1) + kk
            acc = acc + cw_ref[kk:kk + 1, :] * ext_ref[o:o + tm, :]
        u1_ref[...] = acc
        mu = jnp.mean(acc, axis=-1, keepdims=True)
        xc = acc - mu
        rstd = lax.rsqrt(jnp.mean(xc * xc, axis=-1, keepdims=True) + EPS)
        l = (xc * rstd) * lng_ref[...] + lnb_ref[...]
        u3 = (l * jax.nn.sigmoid(l)).astype(BF16)
        u3_ref[...] = u3
        yb = _dot(u3, wpw_ref[...])
        yb_ref[...] = yb
        zg = zg_ref[...]
        mpre = (jax.nn.sigmoid(zg[:, :d]) * ya + jax.nn.sigmoid(zg[:, d:]) * yb).astype(BF16)
        mpre_ref[...] = mpre
        mixed = _dot(mpre, wout_ref[...])
        mixed_ref[...] = mixed
        x1_ref[...] = x_ref[...] + mod_ref[2:3, :] * mixed

    halo = pl.BlockSpec((HALO, CONV_CH), lambda i: (jnp.maximum(i * hpt - 1, 0), 0))
    return pl.pallas_call(
        body, name="fwd_mix", grid=(t // tm,),
        out_shape=(_sds((t, d), F32), _sds((t, d), F32), _sds((t, d), BF16), _sds((t, d), F32), _sds((t, d), F32),
                   _sds((t, CONV_CH), F32), _sds((t, CONV_CH), BF16)),
        in_specs=[_row(tm, HW), _row(tm, CONV_CH), halo, _row(tm, 2 * d), _row(tm, d), _modspec(d, tps),
                  _full(wo_p.shape), _full(cw.shape), _full((1, CONV_CH)), _full((1, CONV_CH)), _full((1, CONV_CH)),
                  _full(wpw.shape), _full(wout.shape)],
        out_specs=(_row(tm, d), _row(tm, d), _row(tm, d), _row(tm, d), _row(tm, d), _row(tm, CONV_CH),
                   _row(tm, CONV_CH)),
        scratch_shapes=[pltpu.VMEM((tm + HALO, CONV_CH), F32)],
        compiler_params=_params("arbitrary"),
    )(attn, u0, u0, zgate, x, mod3, wo_p, cw, cb, lng, lnb, wpw, wout)


def _fwd_ffn(x1, target, g2, mod3, w1, w2, tm, tps):
    t, d = x1.shape
    dff = w1.shape[1]

    def body(x1_ref, tg_ref, g_ref, mod_ref, w1_hbm, w2_hbm,
             h2_ref, a_ref, r_ref, dy_ref, df_ref, dgate_ref, loss_ref, w1_ref, w2_ref):
        i = pl.program_id(0)
        _load_resident(i, [(w1_hbm, w1_ref), (w2_hbm, w2_ref)])
        x1v = x1_ref[...]
        gate2 = mod_ref[5:6, :]
        n, _ = _rms(x1v)
        h2 = ((n * g_ref[...]) * (1.0 + mod_ref[4:5, :]) + mod_ref[3:4, :]).astype(BF16)
        h2_ref[...] = h2
        a = _dot(h2, w1_ref[...])
        a_ref[...] = a
        r = jnp.square(jnp.maximum(a, 0.0)).astype(BF16)
        r_ref[...] = r
        f = _dot(r, w2_ref[...])
        e = (x1v + gate2 * f) - tg_ref[...]
        part = 0.5 * jnp.sum(jnp.mean(e * e, axis=-1, keepdims=True), axis=0, keepdims=True)
        _acc(loss_ref, jnp.broadcast_to(part, loss_ref.shape), i == 0)
        dy = e * (1.0 / d)
        dy_ref[...] = dy
        df_ref[...] = (dy * gate2).astype(BF16)
        _acc(dgate_ref, jnp.sum(dy * f, axis=0, keepdims=True), (i % tps) == 0)

    nseq = t // (tm * tps)
    return pl.pallas_call(
        body, name="fwd_ffn", grid=(t // tm,),
        out_shape=(_sds((t, d), BF16), _sds((t, dff), F32), _sds((t, dff), BF16), _sds((t, d), F32), _sds((t, d), BF16),
                   _sds((nseq, 1, d), F32), _sds((8, LANES), F32)),
        in_specs=[_row(tm, d), _row(tm, d), _full((1, d)), _modspec(d, tps), ANY, ANY],
        out_specs=(_row(tm, d), _row(tm, dff), _row(tm, dff), _row(tm, d), _row(tm, d), _seqv(d, tps),
                   _full((8, LANES))),
        scratch_shapes=[pltpu.VMEM(w1.shape, BF16), pltpu.VMEM(w2.shape, BF16)],
        compiler_params=_params("arbitrary"),
    )(x1, target, g2, mod3, w1, w2)


def _bwd_ffn(df, a, x1, dy, mixed, g2, mod3, w2t, w1t, tm, tps):
    t, d = x1.shape
    dff = a.shape[1]

    def body(df_ref, a_ref, x1_ref, dy_ref, mx_ref, g_ref, mod_ref, w2t_hbm, w1t_hbm,
             da_ref, dx1_ref, dmixed_ref, dshift_ref, dscale_ref, dgate1_ref, dg2_ref, w2t_ref, w1t_ref):
        i = pl.program_id(0)
        _load_resident(i, [(w2t_hbm, w2t_ref), (w1t_hbm, w1t_ref)])
        first_seq = (i % tps) == 0
        dr = _dot(df_ref[...], w2t_ref[...])
        da = (dr * (2.0 * jnp.maximum(a_ref[...], 0.0))).astype(BF16)
        da_ref[...] = da
        dh2 = _dot(da, w1t_ref[...])
        n, r = _rms(x1_ref[...])
        g = g_ref[...]
        sc1 = 1.0 + mod_ref[4:5, :]
        _acc(dshift_ref, jnp.sum(dh2, axis=0, keepdims=True), first_seq)
        _acc(dscale_ref, jnp.sum(dh2 * (n * g), axis=0, keepdims=True), first_seq)
        _acc(dg2_ref, jnp.sum((dh2 * sc1) * n, axis=0, keepdims=True), i == 0)
        dx1 = dy_ref[...] + _rms_bwd(n, r, (dh2 * sc1) * g)
        dx1_ref[...] = dx1
        _acc(dgate1_ref, jnp.sum(dx1 * mx_ref[...], axis=0, keepdims=True), first_seq)
        dmixed_ref[...] = (dx1 * mod_ref[2:3, :]).astype(BF16)

    nseq = t // (tm * tps)
    sv = _sds((nseq, 1, d), F32)
    return pl.pallas_call(
        body, name="bwd_ffn", grid=(t // tm,),
        out_shape=(_sds((t, dff), BF16), _sds((t, d), F32), _sds((t, d), BF16), sv, sv, sv, _sds((1, d), F32)),
        in_specs=[_row(tm, d), _row(tm, dff), _row(tm, d), _row(tm, d), _row(tm, d), _full((1, d)), _modspec(d, tps),
                  ANY, ANY],
        out_specs=(_row(tm, dff), _row(tm, d), _row(tm, d), _seqv(d, tps), _seqv(d, tps), _seqv(d, tps),
                   _full((1, d))),
        scratch_shapes=[pltpu.VMEM(w2t.shape, BF16), pltpu.VMEM(w1t.shape, BF16)],
        compiler_params=_params("arbitrary"),
    )(df, a, x1, dy, mixed, g2, mod3, w2t, w1t)


def _bwd_mix(dmixed, zgate, ya, yb, u1, lng, lnb, woutt, wot_p, wpwt, tm):
    t, d = ya.shape
    _, _, npad = _layout(d)

    def body(dm_ref, zg_ref, ya_ref, yb_ref, u1_ref, lng_ref, lnb_ref, woutt_ref, wot_ref, wpwt_ref,
             dya_ref, dyb_ref, dz_ref, do_ref, du1_ref, dlng_ref, dlnb_ref, dcb_ref):
        i = pl.program_id(0)
        dmpre = _dot(dm_ref[...], woutt_ref[...])
        zg = zg_ref[...]
        ga = jax.nn.sigmoid(zg[:, :d])
        gb = jax.nn.sigmoid(zg[:, d:])
        dya = (dmpre * ga).astype(BF16)
        dyb = (dmpre * gb).astype(BF16)
        dya_ref[...] = dya
        dyb_ref[...] = dyb
        dz_ref[:, :d] = ((dmpre * ya_ref[...]) * (ga * (1.0 - ga))).astype(BF16)
        dz_ref[:, d:] = ((dmpre * yb_ref[...]) * (gb * (1.0 - gb))).astype(BF16)
        do_ref[...] = _dot(dya, wot_ref[...]).astype(BF16)
        du3 = _dot(dyb, wpwt_ref[...])
        u1 = u1_ref[...]
        mu = jnp.mean(u1, axis=-1, keepdims=True)
        xc = u1 - mu
        rstd = lax.rsqrt(jnp.mean(xc * xc, axis=-1, keepdims=True) + EPS)
        nh = xc * rstd
        l = nh * lng_ref[...] + lnb_ref[...]
        sg = jax.nn.sigmoid(l)
        dl = du3 * (sg * (1.0 + l * (1.0 - sg)))
        _acc(dlng_ref, jnp.sum(dl * nh, axis=0, keepdims=True), i == 0)
        _acc(dlnb_ref, jnp.sum(dl, axis=0, keepdims=True), i == 0)
        dnh = dl * lng_ref[...]
        du1 = rstd * (dnh - jnp.mean(dnh, axis=-1, keepdims=True) - nh * jnp.mean(dnh * nh, axis=-1, keepdims=True))
        du1_ref[...] = du1
        _acc(dcb_ref, jnp.sum(du1, axis=0, keepdims=True), i == 0)

    cv = _sds((1, CONV_CH), F32)
    return pl.pallas_call(
        body, name="bwd_mix", grid=(t // tm,),
        out_shape=(_sds((t, d), BF16), _sds((t, d), BF16), _sds((t, npad), BF16), _sds((t, HW), BF16),
                   _sds((t, CONV_CH), F32), cv, cv, cv),
        in_specs=[_row(tm, d), _row(tm, 2 * d), _row(tm, d), _row(tm, d), _row(tm, CONV_CH), _full((1, CONV_CH)),
                  _full((1, CONV_CH)), _full(woutt.shape), _full(wot_p.shape), _full(wpwt.shape)],
        out_specs=(_row(tm, d), _row(tm, d), _row(tm, 2 * d), _row(tm, HW), _row(tm, CONV_CH),
                   _full((1, CONV_CH)), _full((1, CONV_CH)), _full((1, CONV_CH))),
        compiler_params=_params("arbitrary"),
    )(dmixed, zgate, ya, yb, u1, lng, lnb, woutt, wot_p, wpwt)


def _bwd_conv(dz, du1, u0, zglu, cw, tm, tps):
    t = du1.shape[0]
    d = (dz.shape[1] - MLA_IN - 2 * CONV_CH) // 2
    p_glu, _, _ = _layout(d)
    hpt = tm // HALO
    last_blk = t // HALO - 1

    def body(dz_hbm, du_ref, dun_ref, u_ref, uh_ref, zl_ref, cw_ref, dzl_ref, dcw_ref, ext_ref, dext_ref):
        i = pl.program_id(0)
        first = (i % tps) == 0
        last = (i % tps) == (tps - 1)
        ext_ref[:HALO, :] = jnp.where(first, 0.0, uh_ref[...])
        ext_ref[HALO:, :] = u_ref[...]
        du = du_ref[...]
        dext_ref[:tm, :] = du
        dext_ref[tm:, :] = jnp.where(last, 0.0, dun_ref[...])

        @pl.when(i == 0)
        def _():
            dcw_ref[...] = jnp.zeros_like(dcw_ref)

        du0 = jnp.zeros((tm, CONV_CH), F32)
        for kk in range(CONV_W):
            o = HALO - (CONV_W - 1) + kk
            dcw_ref[kk:kk + 1, :] += jnp.sum(du * ext_ref[o:o + tm, :], axis=0, keepdims=True)
            o2 = CONV_W - 1 - kk
            du0 = du0 + cw_ref[kk:kk + 1, :] * dext_ref[o2:o2 + tm, :]
        zl = zl_ref[...]
        ga = zl[:, :CONV_CH]
        sb = jax.nn.sigmoid(zl[:, CONV_CH:])
        dzl_ref[:, :CONV_CH] = (du0 * sb).astype(BF16)
        dzl_ref[:, CONV_CH:] = ((du0 * ga) * (sb * (1.0 - sb))).astype(BF16)

    prev = pl.BlockSpec((HALO, CONV_CH), lambda i: (jnp.maximum(i * hpt - 1, 0), 0))
    nxt = pl.BlockSpec((HALO, CONV_CH), lambda i: (jnp.minimum((i + 1) * hpt, last_blk), 0))
    glu_blk = p_glu // (2 * CONV_CH)
    return pl.pallas_call(
        body, name="bwd_conv", grid=(t // tm,),
        out_shape=(_sds(dz.shape, BF16), _sds(cw.shape, F32)),
        in_specs=[ANY, _row(tm, CONV_CH), nxt, _row(tm, CONV_CH), prev, _row(tm, 2 * CONV_CH), _full(cw.shape)],
        out_specs=(pl.BlockSpec((tm, 2 * CONV_CH), lambda i: (i, glu_blk)), _full(cw.shape)),
        scratch_shapes=[pltpu.VMEM((tm + HALO, CONV_CH), F32), pltpu.VMEM((tm + HALO, CONV_CH), F32)],
        input_output_aliases={0: 0},
        compiler_params=_params("arbitrary"),
    )(dz, du1, du1, u0, u0, zglu, cw)


def _attn_bwd(q, k, v, do, nseq, seq):
    t = q.shape[0]
    blk = pl.BlockSpec((seq, LANES), lambda b, h: (b, h))

    def body(q_ref, k_ref, v_ref, do_ref, dq_ref, dk_ref, dv_ref, dka_ref, dva_ref):
        dka_ref[...] = jnp.zeros_like(dka_ref)
        dva_ref[...] = jnp.zeros_like(dva_ref)
        for i in range(seq // BQ):
            e = (i + 1) * BQ
            q_i = q_ref[i * BQ:e, :]
            do_i = do_ref[i * BQ:e, :]
            k_e = k_ref[:e, :]
            p = _scores(q_i, k_e, i, e)
            dp = _dot_nt(do_i, v_ref[:e, :])
            ds = (p * (dp - jnp.sum(p * dp, axis=-1, keepdims=True)) * (QK_HEAD ** -0.5)).astype(BF16)
            dq_ref[i * BQ:e, :] = _dot(ds, k_e)
            dka_ref[:e, :] += _dot_tn(ds, q_i)
            dva_ref[:e, :] += _dot_tn(p.astype(BF16), do_i)
        dk_ref[...] = dka_ref[...]
        dv_ref[...] = dva_ref[...].astype(BF16)

    return pl.pallas_call(
        body, name="attn_bwd", grid=(nseq, N_HEADS),
        out_shape=(_sds((t, HW), F32), _sds((t, HW), F32), _sds((t, HW), BF16)),
        in_specs=[blk] * 4, out_specs=(blk,) * 3,
        scratch_shapes=[pltpu.VMEM((seq, LANES), F32), pltpu.VMEM((seq, LANES), F32)],
        compiler_params=_params("arbitrary", "arbitrary"),
    )(q, k, v, do)


def _mla_bwd(dz, dq, dk, dv, zm, gql, gkvl, gq, gk, tabs, wuq_p, wk_p, wuqt_p, wkt_p, wvt_p, tm, tps):
    t = zm.shape[0]
    d = (dz.shape[1] - MLA_IN - 2 * CONV_CH) // 2
    _, p_q, _ = _layout(d)
    c_t, s1_t, s2_t = tabs
    tab = pl.BlockSpec((tm, LANES), lambda i: (i % tps, 0))

    def body(dz_hbm, dq_ref, dk_ref, dv_ref, zm_ref, gql_ref, gkvl_ref, gq_ref, gk_ref, c_ref, s1_ref, s2_ref,
             wuq_ref, wk_ref, wuqt_ref, wkt_ref, wvt_ref,
             dzm_ref, dqpre_ref, dkh_ref, dgq_ref, dgk_ref, dgql_ref, dgkvl_ref):
        i = pl.program_id(0)
        c, s1, s2 = c_ref[...], s1_ref[...], s2_ref[...]
        nq, rq = _rms(zm_ref[:, :Q_RANK])
        qpre = _dot((nq * gql_ref[...]).astype(BF16), wuq_ref[...])
        nkv, rkv = _rms(zm_ref[:, Q_RANK:OFF_KV])
        knope = _dot((nkv * gkvl_ref[...]).astype(BF16), wk_ref[...])
        zkr_v = zm_ref[:, OFF_KV:]
        dgq = jnp.zeros((1, LANES), F32)
        dgk = jnp.zeros((1, LANES), F32)
        dzkr = jnp.zeros((tm, LANES), F32)
        for hd in range(N_HEADS):
            sl = slice(hd * LANES, (hd + 1) * LANES)
            n, r = _head_rms(qpre[:, sl])
            dyr = _rope_t(dq_ref[:, sl], c, s1, s2)
            dgq = dgq + jnp.sum(dyr * n, axis=0, keepdims=True)
            dqpre_ref[:, sl] = _head_rms_bwd(n, r, dyr * gq_ref[...]).astype(BF16)
            n, r = _head_rms(knope[:, sl] + zkr_v)
            dyr = _rope_t(dk_ref[:, sl], c, s1, s2)
            dgk = dgk + jnp.sum(dyr * n, axis=0, keepdims=True)
            dkh = _head_rms_bwd(n, r, dyr * gk_ref[...])
            dzkr = dzkr + dkh
            dkh_ref[:, sl] = dkh.astype(BF16)
        _acc(dgq_ref, dgq[:, :QK_HEAD], i == 0)
        _acc(dgk_ref, dgk[:, :QK_HEAD], i == 0)
        dzm_ref[:, OFF_KV:] = dzkr.astype(BF16)
        dqln = _dot(dqpre_ref[...], wuqt_ref[...])
        _acc(dgql_ref, jnp.sum(dqln * nq, axis=0, keepdims=True), i == 0)
        dzm_ref[:, :Q_RANK] = _rms_bwd(nq, rq, dqln * gql_ref[...]).astype(BF16)
        dkvn = _dot(dkh_ref[...], wkt_ref[...]) + _dot(dv_ref[...], wvt_ref[...])
        _acc(dgkvl_ref, jnp.sum(dkvn * nkv, axis=0, keepdims=True), i == 0)
        dzm_ref[:, Q_RANK:OFF_KV] = _rms_bwd(nkv, rkv, dkvn * gkvl_ref[...]).astype(BF16)

    return pl.pallas_call(
        body, name="mla_bwd", grid=(t // tm,),
        out_shape=(_sds(dz.shape, BF16), _sds((t, HW), BF16), _sds((t, HW), BF16), _sds((1, QK_HEAD), F32),
                   _sds((1, QK_HEAD), F32), _sds((1, Q_RANK), F32), _sds((1, KV_RANK), F32)),
        in_specs=[ANY, _row(tm, HW), _row(tm, HW), _row(tm, HW), _row(tm, MLA_IN),
                  _full((1, Q_RANK)), _full((1, KV_RANK)), _full((1, LANES)), _full((1, LANES)), tab, tab, tab,
                  _full(wuq_p.shape), _full(wk_p.shape), _full(wuqt_p.shape), _full(wkt_p.shape), _full(wvt_p.shape)],
        out_specs=(pl.BlockSpec((tm, MLA_IN), lambda i: (i, p_q // MLA_IN)), _row(tm, HW), _row(tm, HW),
                   _full((1, QK_HEAD)), _full((1, QK_HEAD)), _full((1, Q_RANK)), _full((1, KV_RANK))),
        input_output_aliases={0: 0},
        compiler_params=_params("arbitrary"),
    )(dz, dq, dk, dv, zm, gql, gkvl, gq, gk, c_t, s1_t, s2_t, wuq_p, wk_p, wuqt_p, wkt_p, wvt_p)


def _bwd_in(dz, x, dx1, g1, mod3, wint_p, tm, tps):
    t, d = x.shape
    npad = dz.shape[1]

    def body(dz_ref, x_ref, dx1_ref, g_ref, mod_ref, wt_hbm, gx_ref, dshift_ref, dscale_ref, dg1_ref, wt_ref):
        i = pl.program_id(0)
        _load_resident(i, [(wt_hbm, wt_ref)])
        first_seq = (i % tps) == 0
        dh = _dot(dz_ref[...], wt_ref[...])
        n, r = _rms(x_ref[...])
        g = g_ref[...]
        sc1 = 1.0 + mod_ref[1:2, :]
        _acc(dshift_ref, jnp.sum(dh, axis=0, keepdims=True), first_seq)
        _acc(dscale_ref, jnp.sum(dh * (n * g), axis=0, keepdims=True), first_seq)
        _acc(dg1_ref, jnp.sum((dh * sc1) * n, axis=0, keepdims=True), i == 0)
        gx_ref[...] = dx1_ref[...] + _rms_bwd(n, r, (dh * sc1) * g)

    nseq = t // (tm * tps)
    sv = _sds((nseq, 1, d), F32)
    return pl.pallas_call(
        body, name="bwd_in", grid=(t // tm,),
        out_shape=(_sds((t, d), F32), sv, sv, _sds((1, d), F32)),
        in_specs=[_row(tm, npad), _row(tm, d), _row(tm, d), _full((1, d)), _modspec(d, tps), ANY],
        out_specs=(_row(tm, d), _seqv(d, tps), _seqv(d, tps), _full((1, d))),
        scratch_shapes=[pltpu.VMEM(wint_p.shape, BF16)],
        compiler_params=_params("arbitrary"),
    )(dz, x, dx1, g1, mod3, wint_p)


def _tile_of(n, choices):
    for c in choices:
        if n % c == 0:
            return c
    return n


def _tn_matmul(a, b, name, col_shards=0):
    t, k = a.shape
    n = b.shape[1]
    tk = _tile_of(k, (1024, 512, 256, 128))
    tn = n // col_shards if col_shards else _tile_of(n, (1024, 896, 768, 512, 384, 256, 128))
    tt = _tile_of(t, (1024, 512, 256))

    def body(a_ref, b_ref, o_ref):
        _acc(o_ref, _dot_tn(a_ref[...], b_ref[...]), pl.program_id(2) == 0)

    if col_shards:
        out_shape, out_spec = _sds((col_shards, k, tn), F32), pl.BlockSpec((None, tk, tn), lambda i, j, s: (j, i, 0))
    else:
        out_shape, out_spec = _sds((k, n), F32), pl.BlockSpec((tk, tn), lambda i, j, s: (i, j))
    return pl.pallas_call(
        body, name=name, grid=(k // tk, n // tn, t // tt), out_shape=out_shape,
        in_specs=[pl.BlockSpec((tt, tk), lambda i, j, s: (s, i)), pl.BlockSpec((tt, tn), lambda i, j, s: (s, j))],
        out_specs=out_spec, compiler_params=_params("arbitrary", "arbitrary", "arbitrary"),
    )(a, b)


N_SHARD = 4
COL_SHARDED = ("w_in", "w_uq", "w_ukv", "w_o_mla", "w_pw_out", "w_ff1")
ROW_SHARDED = ("w_out", "w_ff2")
BIG = ("w_in", "w_uq", "w_ukv", "w_o_mla", "w_pw_out", "w_out", "w_ff1", "w_ff2")
SMALL = ("norm1_g", "q_latent_g", "kv_latent_g", "qk_norm_q_g", "qk_norm_k_g", "conv_b", "conv_ln_g", "conv_ln_b",
         "norm2_g")
WEIGHTS = ("w_ada", "b_ada", "norm1_g", "w_in", "q_latent_g", "w_uq", "kv_latent_g", "w_ukv", "qk_norm_q_g",
           "qk_norm_k_g", "w_o_mla", "conv_w", "conv_b", "conv_ln_g", "conv_ln_b", "w_pw_out", "w_out", "norm2_g",
           "w_ff1", "w_ff2")


def _pad_heads(w, width):
    k = w.shape[0]
    w3 = w.reshape(k, N_HEADS, width)
    return jnp.pad(w3, ((0, 0), (0, 0), (0, LANES - width))).reshape(k, HW)


def _unpad_heads(g, width):
    k = g.shape[0]
    return g.reshape(k, N_HEADS, LANES)[:, :, :width].reshape(k, N_HEADS * width)


def _pad_win(w):
    d = w.shape[0]
    z = lambda n: jnp.zeros((d, n), w.dtype)
    return jnp.concatenate([w[:, OFF_GLU:], w[:, OFF_KR:OFF_GLU], w[:, :OFF_KV], z(KR_LANE), w[:, OFF_KV:OFF_KR],
                            z(LANES - KR_LANE - QK_ROPE)], axis=1)


def _unpad_win(g):
    d = g.shape[0]
    p_glu, p_q, _ = _layout(d)
    kr = p_q + OFF_KV + KR_LANE
    return jnp.concatenate([g[:, p_q:p_q + OFF_KV], g[:, kr:kr + QK_ROPE], g[:, p_glu:p_q], g[:, :p_glu]], axis=1)


def _col_shards(g):
    k, n = g.shape
    return g.reshape(k, N_SHARD, n // N_SHARD).transpose(1, 0, 2)


def _from_shards(g, name):
    ns, ks, nn = g.shape
    if name in ROW_SHARDED:
        return g.reshape(ns * ks, nn)
    return g.transpose(1, 0, 2).reshape(ks, ns * nn)


def _local_step(x, target, mod, sp, w, tm=256):
    nseq, seq, d = x.shape
    t = nseq * seq
    tps = seq // tm
    xf = x.reshape(t, d)
    tg = target.reshape(t, d)
    mod3 = mod.reshape(nseq, N_MOD, d)

    win_p = _pad_win(w["w_in"])
    wuq_p = _pad_heads(w["w_uq"], QK_HEAD)
    wkv3 = w["w_ukv"].reshape(KV_RANK, N_HEADS, QK_NOPE + V_HEAD)
    wk_p = _pad_heads(wkv3[:, :, :QK_NOPE].reshape(KV_RANK, -1), QK_NOPE)
    wv_p = _pad_heads(wkv3[:, :, QK_NOPE:].reshape(KV_RANK, -1), V_HEAD)
    wo_p = jnp.pad(w["w_o_mla"].reshape(N_HEADS, V_HEAD, d), ((0, 0), (0, LANES - V_HEAD), (0, 0))).reshape(HW, d)
    cw = jnp.pad(w["conv_w"], ((0, HALO - CONV_W), (0, 0)))
    pad_g = lambda g: jnp.pad(g, ((0, 0), (0, LANES - QK_HEAD)))
    gq, gk = pad_g(sp["qk_norm_q_g"]), pad_g(sp["qk_norm_k_g"])
    tabs = _rope_tables(seq)

    h, zm, zglu, zgate, u0 = _fwd_in(xf, sp["norm1_g"], mod3, win_p, tm, tps)
    q, k, v, qln, kvn = _mla_prep(zm, sp["q_latent_g"], sp["kv_latent_g"], gq, gk, tabs, wuq_p, wk_p, wv_p, tm, tps)
    attn = _attn_fwd(q, k, v, nseq, seq)
    x1, mixed, mpre, ya, yb, u1, u3 = _fwd_mix(attn, u0, zgate, xf, mod3, wo_p, cw, sp["conv_b"], sp["conv_ln_g"],
                                               sp["conv_ln_b"], w["w_pw_out"], w["w_out"], tm, tps)
    h2, a, r, dy, df, dgate2, loss_acc = _fwd_ffn(x1, tg, sp["norm2_g"], mod3, w["w_ff1"], w["w_ff2"], tm, tps)
    da, dx1, dmixed, dshift2, dscale2, dgate1, dg2 = _bwd_ffn(df, a, x1, dy, mixed, sp["norm2_g"], mod3,
                                                              w["w_ff2"].T, w["w_ff1"].T, tm, tps)
    dya, dyb, dz, do, du1, dlng, dlnb, dcb = _bwd_mix(dmixed, zgate, ya, yb, u1, sp["conv_ln_g"], sp["conv_ln_b"],
                                                      w["w_out"].T, wo_p.T, w["w_pw_out"].T, tm)
    dz, dcw = _bwd_conv(dz, du1, u0, zglu, cw, tm, tps)
    dq, dk, dv = _attn_bwd(q, k, v, do, nseq, seq)
    dz, dqpre, dkh, dgq, dgk, dgql, dgkvl = _mla_bwd(dz, dq, dk, dv, zm, sp["q_latent_g"], sp["kv_latent_g"], gq, gk,
                                                      tabs, wuq_p, wk_p, wuq_p.T, wk_p.T, wv_p.T, tm, tps)
    gx, dshift1, dscale1, dg1 = _bwd_in(dz, xf, dx1, sp["norm1_g"], mod3, win_p.T, tm, tps)

    dwk_p = _tn_matmul(kvn, dkh, "dw_uk")
    dwv_p = _tn_matmul(kvn, dv, "dw_uv")
    dwkv = jnp.concatenate([dwk_p.reshape(KV_RANK, N_HEADS, LANES)[:, :, :QK_NOPE],
                            dwv_p.reshape(KV_RANK, N_HEADS, LANES)[:, :, :V_HEAD]], axis=2).reshape(KV_RANK, -1)
    dwo = _tn_matmul(attn, dya, "dw_o").reshape(N_HEADS, LANES, d)[:, :V_HEAD].reshape(MLA_WIDTH, d)
    gw = {
        "w_in": _col_shards(_unpad_win(_tn_matmul(h, dz, "dw_in"))),
        "w_uq": _col_shards(_unpad_heads(_tn_matmul(qln, dqpre, "dw_uq"), QK_HEAD)),
        "w_ukv": _col_shards(dwkv),
        "w_o_mla": _col_shards(dwo),
        "conv_w": dcw,
        "w_pw_out": _tn_matmul(u3, dyb, "dw_pw", N_SHARD),
        "w_out": _tn_matmul(mpre, dmixed, "dw_out").reshape(N_SHARD, d // N_SHARD, d),
        "w_ff1": _tn_matmul(h2, da, "dw_ff1", N_SHARD),
        "w_ff2": _tn_matmul(r, df, "dw_ff2").reshape(N_SHARD, -1, d),
    }
    gs = {
        "norm1_g": dg1, "q_latent_g": dgql, "kv_latent_g": dgkvl, "qk_norm_q_g": dgq, "qk_norm_k_g": dgk,
        "conv_b": dcb, "conv_ln_g": dlng, "conv_ln_b": dlnb, "norm2_g": dg2,
    }
    dmod = jnp.concatenate([dshift1, dscale1, dgate1, dshift2, dscale2, dgate2], axis=2).reshape(nseq, N_MOD * d)
    return loss_acc[0, 0], gx.reshape(nseq, seq, d), dmod, gw, gs


def kernel(x, c, w_ada, b_ada, norm1_g, w_in, q_latent_g, w_uq, kv_latent_g, w_ukv, qk_norm_q_g, qk_norm_k_g, w_o_mla, conv_w, conv_b, conv_ln_g, conv_ln_b, w_pw_out, w_out, norm2_g, w_ff1, w_ff2, loss_target, m_w_ada, m_b_ada, m_norm1_g, m_w_in, m_q_latent_g, m_w_uq, m_kv_latent_g, m_w_ukv, m_qk_norm_q_g, m_qk_norm_k_g, m_w_o_mla, m_conv_w, m_conv_b, m_conv_ln_g, m_conv_ln_b, m_w_pw_out, m_w_out, m_norm2_g, m_w_ff1, m_w_ff2, v_w_ada, v_b_ada, v_norm1_g, v_w_in, v_q_latent_g, v_w_uq, v_kv_latent_g, v_w_ukv, v_qk_norm_q_g, v_qk_norm_k_g, v_w_o_mla, v_conv_w, v_conv_b, v_conv_ln_g, v_conv_ln_b, v_w_pw_out, v_w_out, v_norm2_g, v_w_ff1, v_w_ff2):
    given = dict(locals())
    wts = {n: given[n][0] for n in WEIGHTS}
    mom = {n: given["m_" + n][0] for n in WEIGHTS}
    var = {n: given["v_" + n][0] for n in WEIGHTS}
    vec = lambda a: a.reshape(1, -1)
    nseq, seq, d = x.shape
    ix, iy, ic = _place()
    shard = 2 * ix + iy

    halves = [lax.dynamic_slice_in_dim(wts[n].astype(BF16), ic * (wts[n].shape[0] // 2), wts[n].shape[0] // 2, axis=0)
              for n in BIG]
    gathered = _all_gather8(halves + [wts["conv_w"], c], "gather_weights")
    full = {n: _from_shards(g.reshape((N_SHARD, 2 * g.shape[1]) + g.shape[2:]), n) for n, g in zip(BIG, gathered)}
    full["conv_w"] = _from_shards(gathered[-2][0::2], "conv_w")
    c_all = gathered[-1].reshape(8 * nseq, d)

    n_ada = wts["w_ada"].shape[1]
    b_sh = lax.dynamic_slice_in_dim(vec(wts["b_ada"]), shard * n_ada, n_ada, axis=1)
    mod_sh = _ada_mod(c_all, wts["w_ada"], b_sh)
    hb = 4 * nseq
    mod_blk = lax.dynamic_slice_in_dim(mod_sh, ic * hb, hb, axis=0)
    (mod_all,) = _all_gather8([mod_blk], "gather_mod")
    mod_mine = lax.dynamic_slice_in_dim(mod_all, (2 * iy + ic) * nseq, nseq, axis=1)
    mod = jnp.concatenate([lax.dynamic_index_in_dim(mod_mine, 2 * s + ix, axis=0, keepdims=False)
                           for s in range(N_SHARD)], axis=1)

    sp = {n: vec(wts[n]) for n in SMALL}
    loss_part, grad_x, dmod, gw, gs = _local_step(x, loss_target, mod, sp, full)
    loss = lax.psum(loss_part, ("x", "y", "c"))

    parts = _all_gather8([dmod, gw["conv_w"]] + [gs[n] for n in SMALL], "gather_small")
    dmod_all = parts[0].reshape(8 * nseq, N_MOD * d)
    dmod_sh = lax.dynamic_slice_in_dim(dmod_all, shard * n_ada, n_ada, axis=1)
    res = _ada_bwd(c_all, dmod_all, dmod_sh, parts[1:])
    grads = {"w_ada": res[0], "b_ada": res[1]}
    n_cw = wts["conv_w"].shape[1]
    grads["conv_w"] = lax.dynamic_slice_in_dim(res[2], shard * n_cw, n_cw, axis=1)[:CONV_W]
    for n, g in zip(SMALL, res[3:]):
        grads[n] = g

    g2 = [gw[n].reshape(N_SHARD, 2, gw[n].shape[1] // 2, gw[n].shape[2]) for n in BIG]
    from_sibling = _pair_swap(g2, "grad_pair_swap")
    cidx = ic.reshape(1).astype(jnp.int32)
    pair = [_add_pair(g, l, cidx, "pair_sum_" + n) for n, g, l in zip(BIG, g2, from_sibling)]
    from_chips = _chip_scatter([p[1] for p in pair], "grad_chip_scatter")
    own = shard.reshape(1).astype(jnp.int32)
    mine_sum = [_add_chips(p[0], l, own, "chip_sum_" + n) for n, p, l in zip(BIG, pair, from_chips)]
    for n, g in zip(BIG, _pair_gather(mine_sum, "grad_pair_gather")):
        grads[n] = g.reshape(wts[n].shape)

    delta, new_m, new_v = {}, {}, {}
    for n in BIG + ("w_ada",):
        delta[n], new_m[n], new_v[n] = _adamw(wts[n], grads[n], mom[n], var[n], "adamw_" + n)
    rest = ("b_ada", "conv_w") + SMALL
    as2d = lambda a: a if a.ndim == 2 else vec(a)
    res = _adamw_small(*[[as2d(t[n]) for n in rest] for t in (wts, grads, mom, var)])
    for dst, arrs in zip((delta, new_m, new_v), res):
        for n, a in zip(rest, arrs):
            dst[n] = a

    outs = [loss, grad_x]
    for group in (grads, delta, new_m, new_v):
        outs += [group[n].reshape(given[n].shape) for n in WEIGHTS]
    return tuple(outs)
```

```python
import jax
import jax.numpy as jnp
from jax import lax
from jax.experimental import pallas as pl
from jax.experimental.pallas import tpu as pltpu

F32 = jnp.float32
BF16 = jnp.bfloat16
MESH = pl.DeviceIdType.MESH
ANY = pl.BlockSpec(memory_space=pl.ANY)

CHUNK = 64
CHUNK_SHIFT = 6
N_HEADS = 8
QK_NOPE = 64
QK_ROPE = 32
QK_HEAD = QK_NOPE + QK_ROPE
V_HEAD = 64
Q_RANK = 256
KV_RANK = 128
MLA_WIDTH = N_HEADS * V_HEAD
CONV_CH = 512
CONV_W = 31
ROPE_THETA = 10000.0
EPS = 1e-6
LANES = 128
SUBLANES = 8
HW = N_HEADS * LANES
OFF_KV = Q_RANK + KV_RANK
OFF_KR = OFF_KV + QK_ROPE
OFF_GLU = OFF_KR + 2 * CONV_CH
KR_LANE = QK_NOPE
MLA_IN = Q_RANK + KV_RANK + LANES
HALO = 32
N_MOD = 6

ADAM_LR = 0.001
ADAM_B1 = 0.9
ADAM_B2 = 0.999
ADAM_EPS = 1e-08
ADAM_WD = 0.01
ADAM_STEP = 10

VMEM_LIMIT = 56 * 1024 * 1024
BQ = 256


def _layout(d):
    p_glu = 2 * d
    p_q = p_glu + 2 * CONV_CH
    return p_glu, p_q, p_q + MLA_IN


def _params(*sem):
    return pltpu.CompilerParams(dimension_semantics=sem, vmem_limit_bytes=VMEM_LIMIT)


def _dot(a, b):
    return jnp.dot(a, b, preferred_element_type=F32)


def _dot_tn(a, b):
    return lax.dot_general(a, b, (((0,), (0,)), ((), ())), preferred_element_type=F32)


def _dot_nt(a, b):
    return lax.dot_general(a, b, (((1,), (1,)), ((), ())), preferred_element_type=F32)


def _acc(ref, val, first):
    @pl.when(first)
    def _():
        ref[...] = val

    @pl.when(jnp.logical_not(first))
    def _():
        ref[...] += val


def _rms(x):
    r = lax.rsqrt(jnp.mean(x * x, axis=-1, keepdims=True) + EPS)
    return x * r, r


def _rms_bwd(n, r, dn):
    return r * (dn - n * jnp.mean(dn * n, axis=-1, keepdims=True))


def _head_rms(sl):
    r = lax.rsqrt(jnp.sum(sl * sl, axis=-1, keepdims=True) * (1.0 / QK_HEAD) + EPS)
    return sl * r, r


def _head_rms_bwd(n, r, dn):
    return r * (dn - n * (jnp.sum(dn * n, axis=-1, keepdims=True) * (1.0 / QK_HEAD)))


def _rope(x, c, s1, s2):
    return x * c + pltpu.roll(x, QK_ROPE // 2, 1) * s1 + pltpu.roll(x, LANES - QK_ROPE // 2, 1) * s2


def _rope_t(dy, c, s1, s2):
    return dy * c + pltpu.roll(dy * s1, LANES - QK_ROPE // 2, 1) + pltpu.roll(dy * s2, QK_ROPE // 2, 1)


def _rope_tables(seq):
    half = QK_ROPE // 2
    inv_freq = ROPE_THETA ** (-jnp.arange(0, QK_ROPE, 2, dtype=F32) / QK_ROPE)
    ang = jnp.arange(seq, dtype=F32)[:, None] * inv_freq[None, :]
    cos, sin = jnp.cos(ang), jnp.sin(ang)
    z = lambda n: jnp.zeros((seq, n), F32)
    tail = LANES - QK_HEAD
    c = jnp.concatenate([jnp.ones((seq, QK_NOPE), F32), cos, cos, jnp.ones((seq, tail), F32)], axis=1)
    s1 = jnp.concatenate([z(QK_NOPE + half), sin, z(tail)], axis=1)
    s2 = jnp.concatenate([z(QK_NOPE), -sin, z(half + tail)], axis=1)
    return c, s1, s2


def _row(tm, w):
    return pl.BlockSpec((tm, w), lambda i: (i, 0))


def _modspec(d, tps):
    return pl.BlockSpec((None, N_MOD, d), lambda i: (i // tps, 0, 0))


def _seqv(w, tps):
    return pl.BlockSpec((None, 1, w), lambda i: (i // tps, 0, 0))


def _full(shape):
    return pl.BlockSpec(shape, lambda i: tuple(0 for _ in shape))


def _sds(shape, dtype):
    return jax.ShapeDtypeStruct(shape, dtype)


def _fill_shifted(ext_ref, head, body):
    nh = head.shape[0]
    ext_ref[0, :nh, :] = head
    ext_ref[0, nh:, :] = body
    rows = ext_ref[0]
    for b in range(1, SUBLANES):
        ext_ref[b] = pltpu.roll(rows, rows.shape[0] - b, 0)


def _shifted(ext_ref, o, tm):
    a = (o // SUBLANES) * SUBLANES
    return ext_ref[o % SUBLANES, a:a + tm, :]


def _load_resident(i, pairs):
    @pl.when(i == 0)
    def _():
        for src, dst in pairs:
            pltpu.sync_copy(src, dst)


def _place():
    return lax.axis_index("x"), lax.axis_index("y"), lax.axis_index("c")


def _all_gather8(blocks, name):
    na = len(blocks)

    def body(*refs):
        x_refs, out_refs = refs[:na], refs[na:2 * na]
        send_sems, recv_sems = refs[2 * na:]
        x, y, c = _place()
        me, sibling = (x, y, c), (x, y, 1 - c)
        chips = [(1 - x, y), (x, 1 - y), (1 - x, 1 - y)]

        def copy(a, k, blk, to, from_input=False):
            dst = out_refs[a].at[4 * blk[0] + 2 * blk[1] + blk[2]]
            return pltpu.make_async_remote_copy(
                src_ref=x_refs[a] if from_input else dst, dst_ref=dst,
                send_sem=send_sems.at[7 * a + k], recv_sem=recv_sems.at[7 * a + k], device_id=to, device_id_type=MESH)

        first = []
        for a in range(na):
            first.append(copy(a, 0, me, sibling, True))
            first += [copy(a, 1 + j, me, (*chip, c), True) for j, chip in enumerate(chips)]
        for cp in first:
            cp.start()
        passed = []
        for j, chip in enumerate(chips):
            for a in range(na):
                copy(a, 1 + j, (*chip, c), me).wait_recv()
                fwd = copy(a, 4 + j, (*chip, c), sibling)
                fwd.start()
                passed.append(fwd)
        for a in range(na):
            copy(a, 0, sibling, me).wait_recv()
            for j, chip in enumerate(chips):
                copy(a, 4 + j, (*chip, 1 - c), me).wait_recv()
        for cp in first + passed:
            cp.wait_send()

    outs = pl.pallas_call(
        body, name=name, out_shape=tuple(_sds((8,) + b.shape, b.dtype) for b in blocks),
        in_specs=[ANY] * na, out_specs=(ANY,) * na,
        scratch_shapes=[pltpu.SemaphoreType.DMA((7 * na,)), pltpu.SemaphoreType.DMA((7 * na,))],
    )(*blocks)
    ix, iy, ic = _place()
    return tuple(lax.dynamic_update_index_in_dim(o, b, 4 * ix + 2 * iy + ic, 0) for o, b in zip(outs, blocks))


def _pair_swap(gs, name):
    na = len(gs)
    ns = gs[0].shape[0]

    def body(*refs):
        g_refs, land_refs = refs[:na], refs[na:2 * na]
        send_sems, recv_sems = refs[2 * na:]
        x, y, c = _place()
        cps = [pltpu.make_async_remote_copy(
            src_ref=g_refs[a].at[s, 1 - c], dst_ref=land_refs[a].at[s], send_sem=send_sems.at[ns * a + s],
            recv_sem=recv_sems.at[ns * a + s], device_id=(x, y, 1 - c), device_id_type=MESH)
            for a in range(na) for s in range(ns)]
        for cp in cps:
            cp.start()
        for cp in cps:
            cp.wait()

    return pl.pallas_call(
        body, name=name, out_shape=tuple(_sds((ns,) + g.shape[2:], g.dtype) for g in gs),
        in_specs=[ANY] * na, out_specs=(ANY,) * na,
        scratch_shapes=[pltpu.SemaphoreType.DMA((ns * na,)), pltpu.SemaphoreType.DMA((ns * na,))],
    )(*gs)


def _chip_scatter(hs, name):
    na = len(hs)

    def body(*refs):
        h_refs, land_refs = refs[:na], refs[na:2 * na]
        send_sems, recv_sems = refs[2 * na:]
        x, y, c = _place()
        chips = [(1 - x, y), (x, 1 - y), (1 - x, 1 - y)]
        cps = [pltpu.make_async_remote_copy(
            src_ref=h_refs[a].at[2 * tx + ty], dst_ref=land_refs[a].at[j], send_sem=send_sems.at[3 * a + j],
            recv_sem=recv_sems.at[3 * a + j], device_id=(tx, ty, c), device_id_type=MESH)
            for a in range(na) for j, (tx, ty) in enumerate(chips)]
        for cp in cps:
            cp.start()
        for cp in cps:
            cp.wait()

    return pl.pallas_call(
        body, name=name, out_shape=tuple(_sds((3,) + h.shape[1:], h.dtype) for h in hs),
        in_specs=[ANY] * na, out_specs=(ANY,) * na,
        scratch_shapes=[pltpu.SemaphoreType.DMA((3 * na,)), pltpu.SemaphoreType.DMA((3 * na,))],
    )(*hs)


def _pair_gather(fs, name):
    na = len(fs)

    def body(*refs):
        out_refs = refs[na:2 * na]
        send_sems, recv_sems = refs[2 * na:]
        x, y, c = _place()
        sends = [pltpu.make_async_remote_copy(
            src_ref=out_refs[a].at[c], dst_ref=out_refs[a].at[c], send_sem=send_sems.at[a], recv_sem=recv_sems.at[a],
            device_id=(x, y, 1 - c), device_id_type=MESH) for a in range(na)]
        recvs = [pltpu.make_async_remote_copy(
            src_ref=out_refs[a].at[c], dst_ref=out_refs[a].at[1 - c], send_sem=send_sems.at[a],
            recv_sem=recv_sems.at[a], device_id=(x, y, 1 - c), device_id_type=MESH) for a in range(na)]
        for cp in sends:
            cp.start()
        for cp in recvs:
            cp.wait_recv()
        for cp in sends:
            cp.wait_send()

    return pl.pallas_call(
        body, name=name, out_shape=tuple(_sds(f.shape, f.dtype) for f in fs),
        in_specs=[ANY] * na, out_specs=(ANY,) * na, input_output_aliases={a: a for a in range(na)},
        scratch_shapes=[pltpu.SemaphoreType.DMA((na,)), pltpu.SemaphoreType.DMA((na,))],
    )(*fs)


def _row_tile(r, n, itemsize=4, budget=1 << 20):
    if r * n * itemsize <= budget:
        return r
    best = None
    for tr in range(16, r, 16):
        if r % tr == 0 and tr * n * itemsize <= budget:
            best = tr
    assert best is not None, (r, n)
    return best


def _add_pair(g, land, cidx, name):
    ns, _, r, n = g.shape
    tr = _row_tile(r, n)

    def body(c_ref, a_ref, b_ref, o_ref, ob_ref):
        s = a_ref[...] + b_ref[...]
        o_ref[...] = s
        ob_ref[...] = s.astype(BF16)

    out = pl.BlockSpec((None, tr, n), lambda s, i, cr: (s, i, 0))
    return pl.pallas_call(
        body, name=name, out_shape=(_sds((ns, r, n), F32), _sds((ns, r, n), BF16)),
        grid_spec=pltpu.PrefetchScalarGridSpec(
            num_scalar_prefetch=1, grid=(ns, r // tr),
            in_specs=[pl.BlockSpec((None, None, tr, n), lambda s, i, cr: (s, cr[0], i, 0)), out],
            out_specs=(out, out)),
        compiler_params=_params("arbitrary", "arbitrary"),
    )(cidx, g, land)


def _add_chips(h, land, own_c, name):
    _, r, n = h.shape
    tr = _row_tile(r, n)

    def body(o_idx, h_ref, l_ref, o_ref):
        o_ref[...] = ((h_ref[...] + l_ref[0].astype(F32)) + l_ref[1].astype(F32)) + l_ref[2].astype(F32)

    return pl.pallas_call(
        body, name=name, out_shape=_sds((2, r, n), F32),
        grid_spec=pltpu.PrefetchScalarGridSpec(
            num_scalar_prefetch=1, grid=(r // tr,),
            in_specs=[pl.BlockSpec((None, tr, n), lambda i, o: (o[0], i, 0)),
                      pl.BlockSpec((3, tr, n), lambda i, o: (0, i, 0))],
            out_specs=pl.BlockSpec((None, tr, n), lambda i, o: (o[1], i, 0))),
        compiler_params=_params("arbitrary"),
    )(own_c, h, land)


def _adam_math(w, g, m, v):
    nm = ADAM_B1 * m + (1.0 - ADAM_B1) * g
    nv = ADAM_B2 * v + (1.0 - ADAM_B2) * (g * g)
    m_hat = nm / (1.0 - ADAM_B1 ** ADAM_STEP)
    v_hat = nv / (1.0 - ADAM_B2 ** ADAM_STEP)
    return -ADAM_LR * (m_hat / (jnp.sqrt(v_hat) + ADAM_EPS) + ADAM_WD * w), nm, nv


def _adamw(w, g, m, v, name):
    r, n = w.shape
    tr = _row_tile(r, n, budget=1 << 19)

    def body(w_ref, g_ref, m_ref, v_ref, d_ref, nm_ref, nv_ref):
        d_ref[...], nm_ref[...], nv_ref[...] = _adam_math(w_ref[...], g_ref[...], m_ref[...], v_ref[...])

    spec = pl.BlockSpec((tr, n), lambda i: (i, 0))
    return pl.pallas_call(
        body, name=name, out_shape=(_sds((r, n), F32),) * 3, grid=(r // tr,),
        in_specs=[spec] * 4, out_specs=(spec,) * 3, compiler_params=_params("arbitrary"),
    )(w, g, m, v)


def _adamw_small(ws, gs, ms, vs):
    k = len(ws)

    def body(*refs):
        ins, outs = refs[:4 * k], refs[4 * k:]
        for j in range(k):
            d, nm, nv = _adam_math(ins[j][...], ins[k + j][...], ins[2 * k + j][...], ins[3 * k + j][...])
            outs[j][...] = d
            outs[k + j][...] = nm
            outs[2 * k + j][...] = nv

    shapes = tuple(_sds(w.shape, F32) for w in ws)
    res = pl.pallas_call(body, name="adamw_small", out_shape=shapes * 3,
                         compiler_params=pltpu.CompilerParams(vmem_limit_bytes=VMEM_LIMIT))(*ws, *gs, *ms, *vs)
    return res[:k], res[k:2 * k], res[2 * k:]


def _ada_mod(c_all, w_sh, b_sh):
    b, _ = c_all.shape
    n = w_sh.shape[1]

    def body(c_ref, w_ref, b_ref, o_ref):
        cc = c_ref[...]
        ca = (cc * jax.nn.sigmoid(cc)).astype(BF16)
        o_ref[...] = _dot(ca, w_ref[...].astype(BF16)) + b_ref[...]

    return pl.pallas_call(body, name="ada_mod", out_shape=_sds((b, n), F32),
                          compiler_params=pltpu.CompilerParams(vmem_limit_bytes=VMEM_LIMIT))(c_all, w_sh, b_sh)


def _ada_bwd(c_all, dmod_all, dmod_sh, parts):
    b, d = c_all.shape
    n6 = dmod_all.shape[1]
    n = dmod_sh.shape[1]
    k = len(parts)

    def body(*refs):
        c_ref, da_ref, ds_ref = refs[:3]
        p_refs = refs[3:3 + k]
        dw_ref, db_ref = refs[3 + k:5 + k]
        s_refs = refs[5 + k:]
        cc = c_ref[...]
        ca = (cc * jax.nn.sigmoid(cc)).astype(BF16)
        dw_ref[...] = _dot_tn(ca, ds_ref[...].astype(BF16))
        db_ref[...] = jnp.sum(da_ref[...], axis=0, keepdims=True)
        for p_ref, s_ref in zip(p_refs, s_refs):
            tot = p_ref[0]
            for j in range(1, p_ref.shape[0]):
                tot = tot + p_ref[j]
            s_ref[...] = tot

    return pl.pallas_call(
        body, name="ada_bwd",
        out_shape=(_sds((d, n), F32), _sds((1, n6), F32)) + tuple(_sds(p.shape[1:], F32) for p in parts),
        compiler_params=pltpu.CompilerParams(vmem_limit_bytes=VMEM_LIMIT),
    )(c_all, dmod_all, dmod_sh, *parts)


def _fwd_in(x, g1, mod3, win_p, tm, tps):
    t, d = x.shape
    p_glu, p_q, npad = _layout(d)

    def body(x_ref, g_ref, mod_ref, w_hbm, h_ref, zm_ref, zglu_ref, zgate_ref, u0_ref, w_ref):
        _load_resident(pl.program_id(0), [(w_hbm, w_ref)])
        n, _ = _rms(x_ref[...])
        h = ((n * g_ref[...]) * (1.0 + mod_ref[1:2, :]) + mod_ref[0:1, :]).astype(BF16)
        h_ref[...] = h
        z = _dot(h, w_ref[...])
        zgate_ref[...] = z[:, :p_glu]
        zglu = z[:, p_glu:p_q]
        zglu_ref[...] = zglu
        zm_ref[...] = z[:, p_q:]
        u0_ref[...] = zglu[:, :CONV_CH] * jax.nn.sigmoid(zglu[:, CONV_CH:])

    return pl.pallas_call(
        body, name="fwd_in", grid=(t // tm,),
        out_shape=(_sds((t, d), BF16), _sds((t, MLA_IN), F32), _sds((t, 2 * CONV_CH), F32), _sds((t, 2 * d), F32),
                   _sds((t, CONV_CH), F32)),
        in_specs=[_row(tm, d), _full((1, d)), _modspec(d, tps), ANY],
        out_specs=(_row(tm, d), _row(tm, MLA_IN), _row(tm, 2 * CONV_CH), _row(tm, 2 * d), _row(tm, CONV_CH)),
        scratch_shapes=[pltpu.VMEM(win_p.shape, BF16)],
        compiler_params=_params("arbitrary"),
    )(x, g1, mod3, win_p)


def _mla_prep(zm, gql, gkvl, gq, gk, tabs, wuq_p, wk_p, wv_p, tm, tps):
    t = zm.shape[0]
    c_t, s1_t, s2_t = tabs
    tab = pl.BlockSpec((tm, LANES), lambda i: (i % tps, 0))

    def body(zm_ref, gql_ref, gkvl_ref, gq_ref, gk_ref, c_ref, s1_ref, s2_ref, wuq_ref, wk_ref, wv_ref,
             q_ref, k_ref, v_ref, qln_ref, kvn_ref):
        c, s1, s2 = c_ref[...], s1_ref[...], s2_ref[...]
        nq, _ = _rms(zm_ref[:, :Q_RANK])
        qln = (nq * gql_ref[...]).astype(BF16)
        qln_ref[...] = qln
        qpre = _dot(qln, wuq_ref[...])
        nkv, _ = _rms(zm_ref[:, Q_RANK:OFF_KV])
        kvn = (nkv * gkvl_ref[...]).astype(BF16)
        kvn_ref[...] = kvn
        knope = _dot(kvn, wk_ref[...])
        v_ref[...] = _dot(kvn, wv_ref[...]).astype(BF16)
        zkr_v = zm_ref[:, OFF_KV:]
        for hd in range(N_HEADS):
            sl = slice(hd * LANES, (hd + 1) * LANES)
            n, _ = _head_rms(qpre[:, sl])
            q_ref[:, sl] = _rope(n * gq_ref[...], c, s1, s2).astype(BF16)
            n, _ = _head_rms(knope[:, sl] + zkr_v)
            k_ref[:, sl] = _rope(n * gk_ref[...], c, s1, s2).astype(BF16)

    return pl.pallas_call(
        body, name="mla_prep", grid=(t // tm,),
        out_shape=(_sds((t, HW), BF16),) * 3 + (_sds((t, Q_RANK), BF16), _sds((t, KV_RANK), BF16)),
        in_specs=[_row(tm, MLA_IN), _full((1, Q_RANK)), _full((1, KV_RANK)),
                  _full((1, LANES)), _full((1, LANES)), tab, tab, tab,
                  _full(wuq_p.shape), _full(wk_p.shape), _full(wv_p.shape)],
        out_specs=(_row(tm, HW),) * 3 + (_row(tm, Q_RANK), _row(tm, KV_RANK)),
        compiler_params=_params("arbitrary"),
    )(zm, gql, gkvl, gq, gk, c_t, s1_t, s2_t, wuq_p, wk_p, wv_p)


SM_SCALE = QK_HEAD ** -0.5
EXP2_SCALE = SM_SCALE * 1.4426950408889634


def _diag_mask():
    rc = jnp.right_shift(lax.broadcasted_iota(jnp.int32, (BQ, 1), 0), CHUNK_SHIFT)
    cc = jnp.right_shift(lax.broadcasted_iota(jnp.int32, (1, BQ), 1), CHUNK_SHIFT)
    return rc >= cc


def _softmax_parts(q_i, k_ref, lo, e, mask):
    sd = jnp.where(mask, _dot_nt(q_i, k_ref[lo:e, :]), jnp.finfo(F32).min)
    m = jnp.max(sd, axis=-1, keepdims=True)
    if lo:
        sp = _dot_nt(q_i, k_ref[:lo, :])
        m = jnp.maximum(m, jnp.max(sp, axis=-1, keepdims=True))
    pd = jnp.exp2((sd - m) * EXP2_SCALE)
    l = jnp.sum(pd, axis=-1, keepdims=True)
    pp = None
    if lo:
        pp = jnp.exp2((sp - m) * EXP2_SCALE)
        l = l + jnp.sum(pp, axis=-1, keepdims=True)
    return pp, pd, l


def _attn_fwd(q, k, v, nseq, seq):
    t = q.shape[0]
    blk = pl.BlockSpec((seq, LANES), lambda b, h: (b, h))

    def body(q_ref, k_ref, v_ref, o_ref):
        mask = _diag_mask()
        for i in range(seq // BQ):
            lo, e = i * BQ, (i + 1) * BQ
            pp, pd, l = _softmax_parts(q_ref[lo:e, :], k_ref, lo, e, mask)
            o = _dot(pd.astype(BF16), v_ref[lo:e, :])
            if lo:
                o = o + _dot(pp.astype(BF16), v_ref[:lo, :])
            o_ref[lo:e, :] = (o * (1.0 / l)).astype(BF16)

    return pl.pallas_call(
        body, name="attn_fwd", grid=(nseq, N_HEADS), out_shape=_sds((t, HW), BF16),
        in_specs=[blk, blk, blk], out_specs=blk, compiler_params=_params("arbitrary", "arbitrary"),
    )(q, k, v)


def _fwd_mix(attn, u0, zgate, x, mod3, wo_p, cw, cb, lng, lnb, wpw, wout, tm, tps):
    t, d = x.shape
    hpt = tm // HALO

    def body(a_ref, u_ref, uh_ref, zg_ref, x_ref, mod_ref, wo_ref, cw_ref, cb_ref, lng_ref, lnb_ref, wpw_ref, wout_ref,
             x1_ref, mixed_ref, mpre_ref, ya_ref, yb_ref, u1_ref, u3_ref, ext_ref):
        i = pl.program_id(0)
        ya = _dot(a_ref[...], wo_ref[...])
        ya_ref[...] = ya
        first = (i % tps) == 0
        _fill_shifted(ext_ref, jnp.where(first, 0.0, uh_ref[...]), u_ref[...])
        acc = jnp.zeros((tm, CONV_CH), F32) + cb_ref[...]
        for kk in range(CONV_W):
            acc = acc + cw_ref[kk:kk + 1, :] * _shifted(ext_ref, HALO - (CONV_W - 1) + kk, tm)
        u1_ref[...] = acc
        mu = jnp.mean(acc, axis=-1, keepdims=True)
        xc = acc - mu
        rstd = lax.rsqrt(jnp.mean(xc * xc, axis=-1, keepdims=True) + EPS)
        l = (xc * rstd) * lng_ref[...] + lnb_ref[...]
        u3 = (l * jax.nn.sigmoid(l)).astype(BF16)
        u3_ref[...] = u3
        yb = _dot(u3, wpw_ref[...])
        yb_ref[...] = yb
        zg = zg_ref[...]
        mpre = (jax.nn.sigmoid(zg[:, :d]) * ya + jax.nn.sigmoid(zg[:, d:]) * yb).astype(BF16)
        mpre_ref[...] = mpre
        mixed = _dot(mpre, wout_ref[...])
        mixed_ref[...] = mixed
        x1_ref[...] = x_ref[...] + mod_ref[2:3, :] * mixed

    halo = pl.BlockSpec((HALO, CONV_CH), lambda i: (jnp.maximum(i * hpt - 1, 0), 0))
    return pl.pallas_call(
        body, name="fwd_mix", grid=(t // tm,),
        out_shape=(_sds((t, d), F32), _sds((t, d), F32), _sds((t, d), BF16), _sds((t, d), F32), _sds((t, d), F32),
                   _sds((t, CONV_CH), F32), _sds((t, CONV_CH), BF16)),
        in_specs=[_row(tm, HW), _row(tm, CONV_CH), halo, _row(tm, 2 * d), _row(tm, d), _modspec(d, tps),
                  _full(wo_p.shape), _full(cw.shape), _full((1, CONV_CH)), _full((1, CONV_CH)), _full((1, CONV_CH)),
                  _full(wpw.shape), _full(wout.shape)],
        out_specs=(_row(tm, d), _row(tm, d), _row(tm, d), _row(tm, d), _row(tm, d), _row(tm, CONV_CH),
                   _row(tm, CONV_CH)),
        scratch_shapes=[pltpu.VMEM((SUBLANES, tm + HALO, CONV_CH), F32)],
        compiler_params=_params("arbitrary"),
    )(attn, u0, u0, zgate, x, mod3, wo_p, cw, cb, lng, lnb, wpw, wout)


def _fwd_ffn(x1, target, g2, mod3, w1, w2, tm, tps):
    t, d = x1.shape
    dff = w1.shape[1]

    def body(x1_ref, tg_ref, g_ref, mod_ref, w1_hbm, w2_hbm,
             h2_ref, a_ref, r_ref, dy_ref, df_ref, dgate_ref, loss_ref, w1_ref, w2_ref):
        i = pl.program_id(0)
        _load_resident(i, [(w1_hbm, w1_ref), (w2_hbm, w2_ref)])
        x1v = x1_ref[...]
        gate2 = mod_ref[5:6, :]
        n, _ = _rms(x1v)
        h2 = ((n * g_ref[...]) * (1.0 + mod_ref[4:5, :]) + mod_ref[3:4, :]).astype(BF16)
        h2_ref[...] = h2
        a = _dot(h2, w1_ref[...])
        a_ref[...] = a
        r = jnp.square(jnp.maximum(a, 0.0)).astype(BF16)
        r_ref[...] = r
        f = _dot(r, w2_ref[...])
        e = (x1v + gate2 * f) - tg_ref[...]
        part = 0.5 * jnp.sum(jnp.mean(e * e, axis=-1, keepdims=True), axis=0, keepdims=True)
        _acc(loss_ref, jnp.broadcast_to(part, loss_ref.shape), i == 0)
        dy = e * (1.0 / d)
        dy_ref[...] = dy
        df_ref[...] = (dy * gate2).astype(BF16)
        _acc(dgate_ref, jnp.sum(dy * f, axis=0, keepdims=True), (i % tps) == 0)

    nseq = t // (tm * tps)
    return pl.pallas_call(
        body, name="fwd_ffn", grid=(t // tm,),
        out_shape=(_sds((t, d), BF16), _sds((t, dff), F32), _sds((t, dff), BF16), _sds((t, d), F32), _sds((t, d), BF16),
                   _sds((nseq, 1, d), F32), _sds((8, LANES), F32)),
        in_specs=[_row(tm, d), _row(tm, d), _full((1, d)), _modspec(d, tps), ANY, ANY],
        out_specs=(_row(tm, d), _row(tm, dff), _row(tm, dff), _row(tm, d), _row(tm, d), _seqv(d, tps),
                   _full((8, LANES))),
        scratch_shapes=[pltpu.VMEM(w1.shape, BF16), pltpu.VMEM(w2.shape, BF16)],
        compiler_params=_params("arbitrary"),
    )(x1, target, g2, mod3, w1, w2)


def _bwd_ffn(df, a, x1, dy, mixed, g2, mod3, w2t, w1t, tm, tps):
    t, d = x1.shape
    dff = a.shape[1]

    def body(df_ref, a_ref, x1_ref, dy_ref, mx_ref, g_ref, mod_ref, w2t_hbm, w1t_hbm,
             da_ref, dx1_ref, dmixed_ref, dshift_ref, dscale_ref, dgate1_ref, dg2_ref, w2t_ref, w1t_ref):
        i = pl.program_id(0)
        _load_resident(i, [(w2t_hbm, w2t_ref), (w1t_hbm, w1t_ref)])
        first_seq = (i % tps) == 0
        dr = _dot(df_ref[...], w2t_ref[...])
        da = (dr * (2.0 * jnp.maximum(a_ref[...], 0.0))).astype(BF16)
        da_ref[...] = da
        dh2 = _dot(da, w1t_ref[...])
        n, r = _rms(x1_ref[...])
        g = g_ref[...]
        sc1 = 1.0 + mod_ref[4:5, :]
        _acc(dshift_ref, jnp.sum(dh2, axis=0, keepdims=True), first_seq)
        _acc(dscale_ref, jnp.sum(dh2 * (n * g), axis=0, keepdims=True), first_seq)
        _acc(dg2_ref, jnp.sum((dh2 * sc1) * n, axis=0, keepdims=True), i == 0)
        dx1 = dy_ref[...] + _rms_bwd(n, r, (dh2 * sc1) * g)
        dx1_ref[...] = dx1
        _acc(dgate1_ref, jnp.sum(dx1 * mx_ref[...], axis=0, keepdims=True), first_seq)
        dmixed_ref[...] = (dx1 * mod_ref[2:3, :]).astype(BF16)

    nseq = t // (tm * tps)
    sv = _sds((nseq, 1, d), F32)
    return pl.pallas_call(
        body, name="bwd_ffn", grid=(t // tm,),
        out_shape=(_sds((t, dff), BF16), _sds((t, d), F32), _sds((t, d), BF16), sv, sv, sv, _sds((1, d), F32)),
        in_specs=[_row(tm, d), _row(tm, dff), _row(tm, d), _row(tm, d), _row(tm, d), _full((1, d)), _modspec(d, tps),
                  ANY, ANY],
        out_specs=(_row(tm, dff), _row(tm, d), _row(tm, d), _seqv(d, tps), _seqv(d, tps), _seqv(d, tps),
                   _full((1, d))),
        scratch_shapes=[pltpu.VMEM(w2t.shape, BF16), pltpu.VMEM(w1t.shape, BF16)],
        compiler_params=_params("arbitrary"),
    )(df, a, x1, dy, mixed, g2, mod3, w2t, w1t)


def _bwd_mix(dmixed, zgate, ya, yb, u1, lng, lnb, woutt, wot_p, wpwt, tm):
    t, d = ya.shape
    _, _, npad = _layout(d)

    def body(dm_ref, zg_ref, ya_ref, yb_ref, u1_ref, lng_ref, lnb_ref, woutt_ref, wot_ref, wpwt_ref,
             dya_ref, dyb_ref, dz_ref, do_ref, du1_ref, dlng_ref, dlnb_ref, dcb_ref):
        i = pl.program_id(0)
        dmpre = _dot(dm_ref[...], woutt_ref[...])
        zg = zg_ref[...]
        ga = jax.nn.sigmoid(zg[:, :d])
        gb = jax.nn.sigmoid(zg[:, d:])
        dya = (dmpre * ga).astype(BF16)
        dyb = (dmpre * gb).astype(BF16)
        dya_ref[...] = dya
        dyb_ref[...] = dyb
        dz_ref[:, :d] = ((dmpre * ya_ref[...]) * (ga * (1.0 - ga))).astype(BF16)
        dz_ref[:, d:] = ((dmpre * yb_ref[...]) * (gb * (1.0 - gb))).astype(BF16)
        do_ref[...] = _dot(dya, wot_ref[...]).astype(BF16)
        du3 = _dot(dyb, wpwt_ref[...])
        u1 = u1_ref[...]
        mu = jnp.mean(u1, axis=-1, keepdims=True)
        xc = u1 - mu
        rstd = lax.rsqrt(jnp.mean(xc * xc, axis=-1, keepdims=True) + EPS)
        nh = xc * rstd
        l = nh * lng_ref[...] + lnb_ref[...]
        sg = jax.nn.sigmoid(l)
        dl = du3 * (sg * (1.0 + l * (1.0 - sg)))
        _acc(dlng_ref, jnp.sum(dl * nh, axis=0, keepdims=True), i == 0)
        _acc(dlnb_ref, jnp.sum(dl, axis=0, keepdims=True), i == 0)
        dnh = dl * lng_ref[...]
        du1 = rstd * (dnh - jnp.mean(dnh, axis=-1, keepdims=True) - nh * jnp.mean(dnh * nh, axis=-1, keepdims=True))
        du1_ref[...] = du1
        _acc(dcb_ref, jnp.sum(du1, axis=0, keepdims=True), i == 0)

    cv = _sds((1, CONV_CH), F32)
    return pl.pallas_call(
        body, name="bwd_mix", grid=(t // tm,),
        out_shape=(_sds((t, d), BF16), _sds((t, d), BF16), _sds((t, npad), BF16), _sds((t, HW), BF16),
                   _sds((t, CONV_CH), F32), cv, cv, cv),
        in_specs=[_row(tm, d), _row(tm, 2 * d), _row(tm, d), _row(tm, d), _row(tm, CONV_CH), _full((1, CONV_CH)),
                  _full((1, CONV_CH)), _full(woutt.shape), _full(wot_p.shape), _full(wpwt.shape)],
        out_specs=(_row(tm, d), _row(tm, d), _row(tm, 2 * d), _row(tm, HW), _row(tm, CONV_CH),
                   _full((1, CONV_CH)), _full((1, CONV_CH)), _full((1, CONV_CH))),
        compiler_params=_params("arbitrary"),
    )(dmixed, zgate, ya, yb, u1, lng, lnb, woutt, wot_p, wpwt)


def _bwd_conv(dz, du1, u0, zglu, cw, tm, tps):
    t = du1.shape[0]
    d = (dz.shape[1] - MLA_IN - 2 * CONV_CH) // 2
    p_glu, _, _ = _layout(d)
    hpt = tm // HALO
    last_blk = t // HALO - 1

    def body(dz_hbm, du_ref, dun_ref, u_ref, uh_ref, zl_ref, cw_ref, dzl_ref, dcw_ref, ext_ref, dext_ref):
        i = pl.program_id(0)
        first = (i % tps) == 0
        last = (i % tps) == (tps - 1)
        _fill_shifted(ext_ref, jnp.where(first, 0.0, uh_ref[...]), u_ref[...])
        du = du_ref[...]
        _fill_shifted(dext_ref, du, jnp.where(last, 0.0, dun_ref[...]))

        @pl.when(i == 0)
        def _():
            dcw_ref[...] = jnp.zeros_like(dcw_ref)

        du0 = jnp.zeros((tm, CONV_CH), F32)
        for kk in range(CONV_W):
            dcw_ref[kk:kk + 1, :] += jnp.sum(du * _shifted(ext_ref, HALO - (CONV_W - 1) + kk, tm), axis=0, keepdims=True)
            du0 = du0 + cw_ref[kk:kk + 1, :] * _shifted(dext_ref, CONV_W - 1 - kk, tm)
        zl = zl_ref[...]
        ga = zl[:, :CONV_CH]
        sb = jax.nn.sigmoid(zl[:, CONV_CH:])
        dzl_ref[:, :CONV_CH] = (du0 * sb).astype(BF16)
        dzl_ref[:, CONV_CH:] = ((du0 * ga) * (sb * (1.0 - sb))).astype(BF16)

    prev = pl.BlockSpec((HALO, CONV_CH), lambda i: (jnp.maximum(i * hpt - 1, 0), 0))
    nxt = pl.BlockSpec((HALO, CONV_CH), lambda i: (jnp.minimum((i + 1) * hpt, last_blk), 0))
    glu_blk = p_glu // (2 * CONV_CH)
    return pl.pallas_call(
        body, name="bwd_conv", grid=(t // tm,),
        out_shape=(_sds(dz.shape, BF16), _sds(cw.shape, F32)),
        in_specs=[ANY, _row(tm, CONV_CH), nxt, _row(tm, CONV_CH), prev, _row(tm, 2 * CONV_CH), _full(cw.shape)],
        out_specs=(pl.BlockSpec((tm, 2 * CONV_CH), lambda i: (i, glu_blk)), _full(cw.shape)),
        scratch_shapes=[pltpu.VMEM((SUBLANES, tm + HALO, CONV_CH), F32)] * 2,
        input_output_aliases={0: 0},
        compiler_params=_params("arbitrary"),
    )(dz, du1, du1, u0, u0, zglu, cw)


def _attn_bwd(q, k, v, do, nseq, seq):
    t = q.shape[0]
    blk = pl.BlockSpec((seq, LANES), lambda b, h: (b, h))

    def body(q_ref, k_ref, v_ref, do_ref, dq_ref, dk_ref, dv_ref, dka_ref, dva_ref):
        dka_ref[...] = jnp.zeros_like(dka_ref)
        dva_ref[...] = jnp.zeros_like(dva_ref)
        mask = _diag_mask()
        for i in range(seq // BQ):
            lo, e = i * BQ, (i + 1) * BQ
            q_i = q_ref[lo:e, :]
            do_i = do_ref[lo:e, :]
            pp, pd, l = _softmax_parts(q_i, k_ref, lo, e, mask)
            inv = 1.0 / l
            pd = pd * inv
            dpd = _dot_nt(do_i, v_ref[lo:e, :])
            delta = jnp.sum(pd * dpd, axis=-1, keepdims=True)
            if lo:
                pp = pp * inv
                dpp = _dot_nt(do_i, v_ref[:lo, :])
                delta = delta + jnp.sum(pp * dpp, axis=-1, keepdims=True)
            dsd = (pd * (dpd - delta)).astype(BF16)
            dq = _dot(dsd, k_ref[lo:e, :])
            dka_ref[lo:e, :] += _dot_tn(dsd, q_i)
            dva_ref[lo:e, :] += _dot_tn(pd.astype(BF16), do_i)
            if lo:
                dsp = (pp * (dpp - delta)).astype(BF16)
                dq = dq + _dot(dsp, k_ref[:lo, :])
                dka_ref[:lo, :] += _dot_tn(dsp, q_i)
                dva_ref[:lo, :] += _dot_tn(pp.astype(BF16), do_i)
            dq_ref[lo:e, :] = dq * SM_SCALE
        dk_ref[...] = dka_ref[...] * SM_SCALE
        dv_ref[...] = dva_ref[...].astype(BF16)

    return pl.pallas_call(
        body, name="attn_bwd", grid=(nseq, N_HEADS),
        out_shape=(_sds((t, HW), F32), _sds((t, HW), F32), _sds((t, HW), BF16)),
        in_specs=[blk] * 4, out_specs=(blk,) * 3,
        scratch_shapes=[pltpu.VMEM((seq, LANES), F32), pltpu.VMEM((seq, LANES), F32)],
        compiler_params=_params("arbitrary", "arbitrary"),
    )(q, k, v, do)


def _mla_bwd(dz, dq, dk, dv, zm, gql, gkvl, gq, gk, tabs, wuq_p, wk_p, wuqt_p, wkt_p, wvt_p, tm, tps):
    t = zm.shape[0]
    d = (dz.shape[1] - MLA_IN - 2 * CONV_CH) // 2
    _, p_q, _ = _layout(d)
    c_t, s1_t, s2_t = tabs
    tab = pl.BlockSpec((tm, LANES), lambda i: (i % tps, 0))

    def body(dz_hbm, dq_ref, dk_ref, dv_ref, zm_ref, gql_ref, gkvl_ref, gq_ref, gk_ref, c_ref, s1_ref, s2_ref,
             wuq_ref, wk_ref, wuqt_ref, wkt_ref, wvt_ref,
             dzm_ref, dqpre_ref, dkh_ref, dgq_ref, dgk_ref, dgql_ref, dgkvl_ref):
        i = pl.program_id(0)
        c, s1, s2 = c_ref[...], s1_ref[...], s2_ref[...]
        nq, rq = _rms(zm_ref[:, :Q_RANK])
        qpre = _dot((nq * gql_ref[...]).astype(BF16), wuq_ref[...])
        nkv, rkv = _rms(zm_ref[:, Q_RANK:OFF_KV])
        knope = _dot((nkv * gkvl_ref[...]).astype(BF16), wk_ref[...])
        zkr_v = zm_ref[:, OFF_KV:]
        dgq = jnp.zeros((1, LANES), F32)
        dgk = jnp.zeros((1, LANES), F32)
        dzkr = jnp.zeros((tm, LANES), F32)
        for hd in range(N_HEADS):
            sl = slice(hd * LANES, (hd + 1) * LANES)
            n, r = _head_rms(qpre[:, sl])
            dyr = _rope_t(dq_ref[:, sl], c, s1, s2)
            dgq = dgq + jnp.sum(dyr * n, axis=0, keepdims=True)
            dqpre_ref[:, sl] = _head_rms_bwd(n, r, dyr * gq_ref[...]).astype(BF16)
            n, r = _head_rms(knope[:, sl] + zkr_v)
            dyr = _rope_t(dk_ref[:, sl], c, s1, s2)
            dgk = dgk + jnp.sum(dyr * n, axis=0, keepdims=True)
            dkh = _head_rms_bwd(n, r, dyr * gk_ref[...])
            dzkr = dzkr + dkh
            dkh_ref[:, sl] = dkh.astype(BF16)
        _acc(dgq_ref, dgq[:, :QK_HEAD], i == 0)
        _acc(dgk_ref, dgk[:, :QK_HEAD], i == 0)
        dzm_ref[:, OFF_KV:] = dzkr.astype(BF16)
        dqln = _dot(dqpre_ref[...], wuqt_ref[...])
        _acc(dgql_ref, jnp.sum(dqln * nq, axis=0, keepdims=True), i == 0)
        dzm_ref[:, :Q_RANK] = _rms_bwd(nq, rq, dqln * gql_ref[...]).astype(BF16)
        dkvn = _dot(dkh_ref[...], wkt_ref[...]) + _dot(dv_ref[...], wvt_ref[...])
        _acc(dgkvl_ref, jnp.sum(dkvn * nkv, axis=0, keepdims=True), i == 0)
        dzm_ref[:, Q_RANK:OFF_KV] = _rms_bwd(nkv, rkv, dkvn * gkvl_ref[...]).astype(BF16)

    return pl.pallas_call(
        body, name="mla_bwd", grid=(t // tm,),
        out_shape=(_sds(dz.shape, BF16), _sds((t, HW), BF16), _sds((t, HW), BF16), _sds((1, QK_HEAD), F32),
                   _sds((1, QK_HEAD), F32), _sds((1, Q_RANK), F32), _sds((1, KV_RANK), F32)),
        in_specs=[ANY, _row(tm, HW), _row(tm, HW), _row(tm, HW), _row(tm, MLA_IN),
                  _full((1, Q_RANK)), _full((1, KV_RANK)), _full((1, LANES)), _full((1, LANES)), tab, tab, tab,
                  _full(wuq_p.shape), _full(wk_p.shape), _full(wuqt_p.shape), _full(wkt_p.shape), _full(wvt_p.shape)],
        out_specs=(pl.BlockSpec((tm, MLA_IN), lambda i: (i, p_q // MLA_IN)), _row(tm, HW), _row(tm, HW),
                   _full((1, QK_HEAD)), _full((1, QK_HEAD)), _full((1, Q_RANK)), _full((1, KV_RANK))),
        input_output_aliases={0: 0},
        compiler_params=_params("arbitrary"),
    )(dz, dq, dk, dv, zm, gql, gkvl, gq, gk, c_t, s1_t, s2_t, wuq_p, wk_p, wuqt_p, wkt_p, wvt_p)


def _bwd_in(dz, x, dx1, g1, mod3, wint_p, tm, tps):
    t, d = x.shape
    npad = dz.shape[1]

    def body(dz_ref, x_ref, dx1_ref, g_ref, mod_ref, wt_hbm, gx_ref, dshift_ref, dscale_ref, dg1_ref, wt_ref):
        i = pl.program_id(0)
        _load_resident(i, [(wt_hbm, wt_ref)])
        first_seq = (i % tps) == 0
        dh = _dot(dz_ref[...], wt_ref[...])
        n, r = _rms(x_ref[...])
        g = g_ref[...]
        sc1 = 1.0 + mod_ref[1:2, :]
        _acc(dshift_ref, jnp.sum(dh, axis=0, keepdims=True), first_seq)
        _acc(dscale_ref, jnp.sum(dh * (n * g), axis=0, keepdims=True), first_seq)
        _acc(dg1_ref, jnp.sum((dh * sc1) * n, axis=0, keepdims=True), i == 0)
        gx_ref[...] = dx1_ref[...] + _rms_bwd(n, r, (dh * sc1) * g)

    nseq = t // (tm * tps)
    sv = _sds((nseq, 1, d), F32)
    return pl.pallas_call(
        body, name="bwd_in", grid=(t // tm,),
        out_shape=(_sds((t, d), F32), sv, sv, _sds((1, d), F32)),
        in_specs=[_row(tm, npad), _row(tm, d), _row(tm, d), _full((1, d)), _modspec(d, tps), ANY],
        out_specs=(_row(tm, d), _seqv(d, tps), _seqv(d, tps), _full((1, d))),
        scratch_shapes=[pltpu.VMEM(wint_p.shape, BF16)],
        compiler_params=_params("arbitrary"),
    )(dz, x, dx1, g1, mod3, wint_p)


def _tile_of(n, choices):
    for c in choices:
        if n % c == 0:
            return c
    return n


def _tn_matmul(a, b, name, col_shards=0):
    t, k = a.shape
    n = b.shape[1]
    tk = _tile_of(k, (1024, 512, 256, 128))
    tn = n // col_shards if col_shards else _tile_of(n, (1024, 896, 768, 512, 384, 256, 128))
    tt = _tile_of(t, (1024, 512, 256))

    def body(a_ref, b_ref, o_ref):
        _acc(o_ref, _dot_tn(a_ref[...], b_ref[...]), pl.program_id(2) == 0)

    if col_shards:
        out_shape, out_spec = _sds((col_shards, k, tn), F32), pl.BlockSpec((None, tk, tn), lambda i, j, s: (j, i, 0))
    else:
        out_shape, out_spec = _sds((k, n), F32), pl.BlockSpec((tk, tn), lambda i, j, s: (i, j))
    return pl.pallas_call(
        body, name=name, grid=(k // tk, n // tn, t // tt), out_shape=out_shape,
        in_specs=[pl.BlockSpec((tt, tk), lambda i, j, s: (s, i)), pl.BlockSpec((tt, tn), lambda i, j, s: (s, j))],
        out_specs=out_spec, compiler_params=_params("arbitrary", "arbitrary", "arbitrary"),
    )(a, b)


N_SHARD = 4
COL_SHARDED = ("w_in", "w_uq", "w_ukv", "w_o_mla", "w_pw_out", "w_ff1")
ROW_SHARDED = ("w_out", "w_ff2")
BIG = ("w_in", "w_uq", "w_ukv", "w_o_mla", "w_pw_out", "w_out", "w_ff1", "w_ff2")
SMALL = ("norm1_g", "q_latent_g", "kv_latent_g", "qk_norm_q_g", "qk_norm_k_g", "conv_b", "conv_ln_g", "conv_ln_b",
         "norm2_g")
WEIGHTS = ("w_ada", "b_ada", "norm1_g", "w_in", "q_latent_g", "w_uq", "kv_latent_g", "w_ukv", "qk_norm_q_g",
           "qk_norm_k_g", "w_o_mla", "conv_w", "conv_b", "conv_ln_g", "conv_ln_b", "w_pw_out", "w_out", "norm2_g",
           "w_ff1", "w_ff2")


def _pad_heads(w, width):
    k = w.shape[0]
    w3 = w.reshape(k, N_HEADS, width)
    return jnp.pad(w3, ((0, 0), (0, 0), (0, LANES - width))).reshape(k, HW)


def _unpad_heads(g, width):
    k = g.shape[0]
    return g.reshape(k, N_HEADS, LANES)[:, :, :width].reshape(k, N_HEADS * width)


def _pad_win(w):
    d = w.shape[0]
    z = lambda n: jnp.zeros((d, n), w.dtype)
    return jnp.concatenate([w[:, OFF_GLU:], w[:, OFF_KR:OFF_GLU], w[:, :OFF_KV], z(KR_LANE), w[:, OFF_KV:OFF_KR],
                            z(LANES - KR_LANE - QK_ROPE)], axis=1)


def _unpad_win(g):
    d = g.shape[0]
    p_glu, p_q, _ = _layout(d)
    kr = p_q + OFF_KV + KR_LANE
    return jnp.concatenate([g[:, p_q:p_q + OFF_KV], g[:, kr:kr + QK_ROPE], g[:, p_glu:p_q], g[:, :p_glu]], axis=1)


def _col_shards(g):
    k, n = g.shape
    return g.reshape(k, N_SHARD, n // N_SHARD).transpose(1, 0, 2)


def _from_shards(g, name):
    ns, ks, nn = g.shape
    if name in ROW_SHARDED:
        return g.reshape(ns * ks, nn)
    return g.transpose(1, 0, 2).reshape(ks, ns * nn)


def _local_step(x, target, mod, sp, w, tm=256):
    nseq, seq, d = x.shape
    t = nseq * seq
    tps = seq // tm
    xf = x.reshape(t, d)
    tg = target.reshape(t, d)
    mod3 = mod.reshape(nseq, N_MOD, d)

    win_p = _pad_win(w["w_in"])
    wuq_p = _pad_heads(w["w_uq"], QK_HEAD)
    wkv3 = w["w_ukv"].reshape(KV_RANK, N_HEADS, QK_NOPE + V_HEAD)
    wk_p = _pad_heads(wkv3[:, :, :QK_NOPE].reshape(KV_RANK, -1), QK_NOPE)
    wv_p = _pad_heads(wkv3[:, :, QK_NOPE:].reshape(KV_RANK, -1), V_HEAD)
    wo_p = jnp.pad(w["w_o_mla"].reshape(N_HEADS, V_HEAD, d), ((0, 0), (0, LANES - V_HEAD), (0, 0))).reshape(HW, d)
    cw = jnp.pad(w["conv_w"], ((0, HALO - CONV_W), (0, 0)))
    pad_g = lambda g: jnp.pad(g, ((0, 0), (0, LANES - QK_HEAD)))
    gq, gk = pad_g(sp["qk_norm_q_g"]), pad_g(sp["qk_norm_k_g"])
    tabs = _rope_tables(seq)

    h, zm, zglu, zgate, u0 = _fwd_in(xf, sp["norm1_g"], mod3, win_p, tm, tps)
    q, k, v, qln, kvn = _mla_prep(zm, sp["q_latent_g"], sp["kv_latent_g"], gq, gk, tabs, wuq_p, wk_p, wv_p, tm, tps)
    attn = _attn_fwd(q, k, v, nseq, seq)
    x1, mixed, mpre, ya, yb, u1, u3 = _fwd_mix(attn, u0, zgate, xf, mod3, wo_p, cw, sp["conv_b"], sp["conv_ln_g"],
                                               sp["conv_ln_b"], w["w_pw_out"], w["w_out"], tm, tps)
    h2, a, r, dy, df, dgate2, loss_acc = _fwd_ffn(x1, tg, sp["norm2_g"], mod3, w["w_ff1"], w["w_ff2"], tm, tps)
    da, dx1, dmixed, dshift2, dscale2, dgate1, dg2 = _bwd_ffn(df, a, x1, dy, mixed, sp["norm2_g"], mod3,
                                                              w["w_ff2"].T, w["w_ff1"].T, tm, tps)
    dya, dyb, dz, do, du1, dlng, dlnb, dcb = _bwd_mix(dmixed, zgate, ya, yb, u1, sp["conv_ln_g"], sp["conv_ln_b"],
                                                      w["w_out"].T, wo_p.T, w["w_pw_out"].T, tm)
    dz, dcw = _bwd_conv(dz, du1, u0, zglu, cw, tm, tps)
    dq, dk, dv = _attn_bwd(q, k, v, do, nseq, seq)
    dz, dqpre, dkh, dgq, dgk, dgql, dgkvl = _mla_bwd(dz, dq, dk, dv, zm, sp["q_latent_g"], sp["kv_latent_g"], gq, gk,
                                                      tabs, wuq_p, wk_p, wuq_p.T, wk_p.T, wv_p.T, tm, tps)
    gx, dshift1, dscale1, dg1 = _bwd_in(dz, xf, dx1, sp["norm1_g"], mod3, win_p.T, tm, tps)

    dwk_p = _tn_matmul(kvn, dkh, "dw_uk")
    dwv_p = _tn_matmul(kvn, dv, "dw_uv")
    dwkv = jnp.concatenate([dwk_p.reshape(KV_RANK, N_HEADS, LANES)[:, :, :QK_NOPE],
                            dwv_p.reshape(KV_RANK, N_HEADS, LANES)[:, :, :V_HEAD]], axis=2).reshape(KV_RANK, -1)
    dwo = _tn_matmul(attn, dya, "dw_o").reshape(N_HEADS, LANES, d)[:, :V_HEAD].reshape(MLA_WIDTH, d)
    gw = {
        "w_in": _col_shards(_unpad_win(_tn_matmul(h, dz, "dw_in"))),
        "w_uq": _col_shards(_unpad_heads(_tn_matmul(qln, dqpre, "dw_uq"), QK_HEAD)),
        "w_ukv": _col_shards(dwkv),
        "w_o_mla": _col_shards(dwo),
        "conv_w": dcw,
        "w_pw_out": _tn_matmul(u3, dyb, "dw_pw", N_SHARD),
        "w_out": _tn_matmul(mpre, dmixed, "dw_out").reshape(N_SHARD, d // N_SHARD, d),
        "w_ff1": _tn_matmul(h2, da, "dw_ff1", N_SHARD),
        "w_ff2": _tn_matmul(r, df, "dw_ff2").reshape(N_SHARD, -1, d),
    }
    gs = {
        "norm1_g": dg1, "q_latent_g": dgql, "kv_latent_g": dgkvl, "qk_norm_q_g": dgq, "qk_norm_k_g": dgk,
        "conv_b": dcb, "conv_ln_g": dlng, "conv_ln_b": dlnb, "norm2_g": dg2,
    }
    dmod = jnp.concatenate([dshift1, dscale1, dgate1, dshift2, dscale2, dgate2], axis=2).reshape(nseq, N_MOD * d)
    return loss_acc[0, 0], gx.reshape(nseq, seq, d), dmod, gw, gs


def kernel(x, c, w_ada, b_ada, norm1_g, w_in, q_latent_g, w_uq, kv_latent_g, w_ukv, qk_norm_q_g, qk_norm_k_g, w_o_mla, conv_w, conv_b, conv_ln_g, conv_ln_b, w_pw_out, w_out, norm2_g, w_ff1, w_ff2, loss_target, m_w_ada, m_b_ada, m_norm1_g, m_w_in, m_q_latent_g, m_w_uq, m_kv_latent_g, m_w_ukv, m_qk_norm_q_g, m_qk_norm_k_g, m_w_o_mla, m_conv_w, m_conv_b, m_conv_ln_g, m_conv_ln_b, m_w_pw_out, m_w_out, m_norm2_g, m_w_ff1, m_w_ff2, v_w_ada, v_b_ada, v_norm1_g, v_w_in, v_q_latent_g, v_w_uq, v_kv_latent_g, v_w_ukv, v_qk_norm_q_g, v_qk_norm_k_g, v_w_o_mla, v_conv_w, v_conv_b, v_conv_ln_g, v_conv_ln_b, v_w_pw_out, v_w_out, v_norm2_g, v_w_ff1, v_w_ff2):
    given = dict(locals())
    wts = {n: given[n][0] for n in WEIGHTS}
    mom = {n: given["m_" + n][0] for n in WEIGHTS}
    var = {n: given["v_" + n][0] for n in WEIGHTS}
    vec = lambda a: a.reshape(1, -1)
    nseq, seq, d = x.shape
    ix, iy, ic = _place()
    shard = 2 * ix + iy

    halves = [lax.dynamic_slice_in_dim(wts[n].astype(BF16), ic * (wts[n].shape[0] // 2), wts[n].shape[0] // 2, axis=0)
              for n in BIG]
    gathered = _all_gather8(halves + [wts["conv_w"], c], "gather_weights")
    full = {n: _from_shards(g.reshape((N_SHARD, 2 * g.shape[1]) + g.shape[2:]), n) for n, g in zip(BIG, gathered)}
    full["conv_w"] = _from_shards(gathered[-2][0::2], "conv_w")
    c_all = gathered[-1].reshape(8 * nseq, d)

    n_ada = wts["w_ada"].shape[1]
    b_sh = lax.dynamic_slice_in_dim(vec(wts["b_ada"]), shard * n_ada, n_ada, axis=1)
    mod_sh = _ada_mod(c_all, wts["w_ada"], b_sh)
    hb = 4 * nseq
    mod_blk = lax.dynamic_slice_in_dim(mod_sh, ic * hb, hb, axis=0)
    (mod_all,) = _all_gather8([mod_blk], "gather_mod")
    mod_mine = lax.dynamic_slice_in_dim(mod_all, (2 * iy + ic) * nseq, nseq, axis=1)
    mod = jnp.concatenate([lax.dynamic_index_in_dim(mod_mine, 2 * s + ix, axis=0, keepdims=False)
                           for s in range(N_SHARD)], axis=1)

    sp = {n: vec(wts[n]) for n in SMALL}
    loss_part, grad_x, dmod, gw, gs = _local_step(x, loss_target, mod, sp, full)
    loss = lax.psum(loss_part, ("x", "y", "c"))

    parts = _all_gather8([dmod, gw["conv_w"]] + [gs[n] for n in SMALL], "gather_small")
    dmod_all = parts[0].reshape(8 * nseq, N_MOD * d)
    dmod_sh = lax.dynamic_slice_in_dim(dmod_all, shard * n_ada, n_ada, axis=1)
    res = _ada_bwd(c_all, dmod_all, dmod_sh, parts[1:])
    grads = {"w_ada": res[0], "b_ada": res[1]}
    n_cw = wts["conv_w"].shape[1]
    grads["conv_w"] = lax.dynamic_slice_in_dim(res[2], shard * n_cw, n_cw, axis=1)[:CONV_W]
    for n, g in zip(SMALL, res[3:]):
        grads[n] = g

    g2 = [gw[n].reshape(N_SHARD, 2, gw[n].shape[1] // 2, gw[n].shape[2]) for n in BIG]
    from_sibling = _pair_swap(g2, "grad_pair_swap")
    cidx = ic.reshape(1).astype(jnp.int32)
    pair = [_add_pair(g, l, cidx, "pair_sum_" + n) for n, g, l in zip(BIG, g2, from_sibling)]
    from_chips = _chip_scatter([p[1] for p in pair], "grad_chip_scatter")
    own_c = jnp.stack([shard, ic]).astype(jnp.int32)
    mine_sum = [_add_chips(p[0], l, own_c, "chip_sum_" + n) for n, p, l in zip(BIG, pair, from_chips)]
    for n, g in zip(BIG, _pair_gather(mine_sum, "grad_pair_gather")):
        grads[n] = g.reshape(wts[n].shape)

    delta, new_m, new_v = {}, {}, {}
    for n in BIG + ("w_ada",):
        delta[n], new_m[n], new_v[n] = _adamw(wts[n], grads[n], mom[n], var[n], "adamw_" + n)
    rest = ("b_ada", "conv_w") + SMALL
    as2d = lambda a: a if a.ndim == 2 else vec(a)
    res = _adamw_small(*[[as2d(t[n]) for n in rest] for t in (wts, grads, mom, var)])
    for dst, arrs in zip((delta, new_m, new_v), res):
        for n, a in zip(rest, arrs):
            dst[n] = a

    outs = [loss, grad_x]
    for group in (grads, delta, new_m, new_v):
        outs += [group[n].reshape(given[n].shape) for n in WEIGHTS]
    return tuple(outs)
```

```python
import jax
import jax.numpy as jnp
from jax import lax
from jax.experimental import pallas as pl
from jax.experimental.pallas import tpu as pltpu

F32 = jnp.float32
BF16 = jnp.bfloat16
MESH = pl.DeviceIdType.MESH
ANY = pl.BlockSpec(memory_space=pl.ANY)

CHUNK = 64
CHUNK_SHIFT = 6
N_HEADS = 8
QK_NOPE = 64
QK_ROPE = 32
QK_HEAD = QK_NOPE + QK_ROPE
V_HEAD = 64
Q_RANK = 256
KV_RANK = 128
MLA_WIDTH = N_HEADS * V_HEAD
CONV_CH = 512
CONV_W = 31
ROPE_THETA = 10000.0
EPS = 1e-6
LANES = 128
SUBLANES = 8
HW = N_HEADS * LANES
OFF_KV = Q_RANK + KV_RANK
OFF_KR = OFF_KV + QK_ROPE
OFF_GLU = OFF_KR + 2 * CONV_CH
KR_LANE = QK_NOPE
MLA_IN = Q_RANK + KV_RANK + LANES
HALO = 32
N_MOD = 6

ADAM_LR = 0.001
ADAM_B1 = 0.9
ADAM_B2 = 0.999
ADAM_EPS = 1e-08
ADAM_WD = 0.01
ADAM_STEP = 10

VMEM_LIMIT = 56 * 1024 * 1024
BQ = 256


def _layout(d):
    p_glu = 2 * d
    p_q = p_glu + 2 * CONV_CH
    return p_glu, p_q, p_q + MLA_IN


def _params(*sem):
    return pltpu.CompilerParams(dimension_semantics=sem, vmem_limit_bytes=VMEM_LIMIT)


def _dot(a, b):
    return jnp.dot(a, b, preferred_element_type=F32)


def _dot_tn(a, b):
    return lax.dot_general(a, b, (((0,), (0,)), ((), ())), preferred_element_type=F32)


def _dot_nt(a, b):
    return lax.dot_general(a, b, (((1,), (1,)), ((), ())), preferred_element_type=F32)


def _acc(ref, val, first):
    @pl.when(first)
    def _():
        ref[...] = val

    @pl.when(jnp.logical_not(first))
    def _():
        ref[...] += val


def _rms(x):
    r = lax.rsqrt(jnp.mean(x * x, axis=-1, keepdims=True) + EPS)
    return x * r, r


def _rms_bwd(n, r, dn):
    return r * (dn - n * jnp.mean(dn * n, axis=-1, keepdims=True))


def _head_rms(sl):
    r = lax.rsqrt(jnp.sum(sl * sl, axis=-1, keepdims=True) * (1.0 / QK_HEAD) + EPS)
    return sl * r, r


def _head_rms_bwd(n, r, dn):
    return r * (dn - n * (jnp.sum(dn * n, axis=-1, keepdims=True) * (1.0 / QK_HEAD)))


def _rope(x, c, s1, s2):
    return x * c + pltpu.roll(x, QK_ROPE // 2, 1) * s1 + pltpu.roll(x, LANES - QK_ROPE // 2, 1) * s2


def _rope_t(dy, c, s1, s2):
    return dy * c + pltpu.roll(dy * s1, LANES - QK_ROPE // 2, 1) + pltpu.roll(dy * s2, QK_ROPE // 2, 1)


def _rope_tables(seq):
    half = QK_ROPE // 2
    inv_freq = ROPE_THETA ** (-jnp.arange(0, QK_ROPE, 2, dtype=F32) / QK_ROPE)
    ang = jnp.arange(seq, dtype=F32)[:, None] * inv_freq[None, :]
    cos, sin = jnp.cos(ang), jnp.sin(ang)
    z = lambda n: jnp.zeros((seq, n), F32)
    tail = LANES - QK_HEAD
    c = jnp.concatenate([jnp.ones((seq, QK_NOPE), F32), cos, cos, jnp.ones((seq, tail), F32)], axis=1)
    s1 = jnp.concatenate([z(QK_NOPE + half), sin, z(tail)], axis=1)
    s2 = jnp.concatenate([z(QK_NOPE), -sin, z(half + tail)], axis=1)
    return c, s1, s2


def _row(tm, w):
    return pl.BlockSpec((tm, w), lambda i: (i, 0))


def _modspec(d, tps):
    return pl.BlockSpec((None, N_MOD, d), lambda i: (i // tps, 0, 0))


def _seqv(w, tps):
    return pl.BlockSpec((None, 1, w), lambda i: (i // tps, 0, 0))


def _full(shape):
    return pl.BlockSpec(shape, lambda i: tuple(0 for _ in shape))


def _sds(shape, dtype):
    return jax.ShapeDtypeStruct(shape, dtype)


def _fill_shifted(ext_ref, head, body):
    nh = head.shape[0]
    ext_ref[0, :nh, :] = head
    ext_ref[0, nh:, :] = body
    rows = ext_ref[0]
    for b in range(1, SUBLANES):
        ext_ref[b] = pltpu.roll(rows, rows.shape[0] - b, 0)


def _shifted(ext_ref, o, tm):
    a = (o // SUBLANES) * SUBLANES
    return ext_ref[o % SUBLANES, a:a + tm, :]


def _load_resident(i, pairs):
    @pl.when(i == 0)
    def _():
        for src, dst in pairs:
            pltpu.sync_copy(src, dst)


def _place():
    return lax.axis_index("x"), lax.axis_index("y"), lax.axis_index("c")


def _all_gather8(blocks, name):
    na = len(blocks)

    def body(*refs):
        start, forward, finish = _gather8_phases(refs[:na], refs[na:2 * na], *refs[2 * na:])
        start()
        forward()
        finish()

    outs = pl.pallas_call(
        body, name=name, out_shape=_gather8_shapes(blocks), in_specs=[ANY] * na, out_specs=(ANY,) * na,
        scratch_shapes=_gather8_sems(na),
    )(*blocks)
    return _own_block_placed(outs, blocks)


def _gather8_shapes(blocks):
    return tuple(_sds((8,) + b.shape, b.dtype) for b in blocks)


def _gather8_sems(na):
    return [pltpu.SemaphoreType.DMA((7 * na,)), pltpu.SemaphoreType.DMA((7 * na,))]


def _own_block_placed(outs, blocks):
    ix, iy, ic = _place()
    return tuple(lax.dynamic_update_index_in_dim(o, b, 4 * ix + 2 * iy + ic, 0) for o, b in zip(outs, blocks))


def _gather8_phases(x_refs, out_refs, send_sems, recv_sems):
    na = len(x_refs)
    x, y, c = _place()
    me, sibling = (x, y, c), (x, y, 1 - c)
    chips = [(1 - x, y), (x, 1 - y), (1 - x, 1 - y)]

    def copy(a, k, blk, to, from_input=False):
        dst = out_refs[a].at[4 * blk[0] + 2 * blk[1] + blk[2]]
        return pltpu.make_async_remote_copy(
            src_ref=x_refs[a] if from_input else dst, dst_ref=dst,
            send_sem=send_sems.at[7 * a + k], recv_sem=recv_sems.at[7 * a + k], device_id=to, device_id_type=MESH)

    def first(a):
        return [copy(a, 0, me, sibling, True)] + [copy(a, 1 + j, me, (*chip, c), True) for j, chip in enumerate(chips)]

    def start():
        for a in range(na):
            for cp in first(a):
                cp.start()

    def forward():
        for j, chip in enumerate(chips):
            for a in range(na):
                copy(a, 1 + j, (*chip, c), me).wait_recv()
                copy(a, 4 + j, (*chip, c), sibling).start()

    def finish():
        for a in range(na):
            copy(a, 0, sibling, me).wait_recv()
            for j, chip in enumerate(chips):
                copy(a, 4 + j, (*chip, 1 - c), me).wait_recv()
        for a in range(na):
            for cp in first(a) + [copy(a, 4 + j, (*chip, c), sibling) for j, chip in enumerate(chips)]:
                cp.wait_send()

    return start, forward, finish


def _pair_swap(gs, name):
    na = len(gs)
    ns = gs[0].shape[0]

    def body(*refs):
        g_refs, land_refs = refs[:na], refs[na:2 * na]
        send_sems, recv_sems = refs[2 * na:]
        x, y, c = _place()
        cps = [pltpu.make_async_remote_copy(
            src_ref=g_refs[a].at[s, 1 - c], dst_ref=land_refs[a].at[s], send_sem=send_sems.at[ns * a + s],
            recv_sem=recv_sems.at[ns * a + s], device_id=(x, y, 1 - c), device_id_type=MESH)
            for a in range(na) for s in range(ns)]
        for cp in cps:
            cp.start()
        for cp in cps:
            cp.wait()

    return pl.pallas_call(
        body, name=name, out_shape=tuple(_sds((ns,) + g.shape[2:], g.dtype) for g in gs),
        in_specs=[ANY] * na, out_specs=(ANY,) * na,
        scratch_shapes=[pltpu.SemaphoreType.DMA((ns * na,)), pltpu.SemaphoreType.DMA((ns * na,))],
    )(*gs)


def _chip_scatter(hs, name):
    na = len(hs)

    def body(*refs):
        start, finish = _scatter_phases(refs[:na], refs[na:2 * na], *refs[2 * na:])
        start()
        finish()

    return pl.pallas_call(
        body, name=name, out_shape=_scatter_shapes(hs), in_specs=[ANY] * na, out_specs=(ANY,) * na,
        scratch_shapes=_scatter_sems(na),
    )(*hs)


def _scatter_shapes(hs):
    return tuple(_sds((3,) + h.shape[1:], h.dtype) for h in hs)


def _scatter_sems(na):
    return [pltpu.SemaphoreType.DMA((3 * na,)), pltpu.SemaphoreType.DMA((3 * na,))]


def _scatter_phases(h_refs, land_refs, send_sems, recv_sems):
    x, y, c = _place()
    chips = [(1 - x, y), (x, 1 - y), (1 - x, 1 - y)]

    def copies():
        return [pltpu.make_async_remote_copy(
            src_ref=h_refs[a].at[2 * tx + ty], dst_ref=land_refs[a].at[j], send_sem=send_sems.at[3 * a + j],
            recv_sem=recv_sems.at[3 * a + j], device_id=(tx, ty, c), device_id_type=MESH)
            for a in range(len(h_refs)) for j, (tx, ty) in enumerate(chips)]

    def start():
        for cp in copies():
            cp.start()

    def finish():
        for cp in copies():
            cp.wait()

    return start, finish


def _pair_gather(fs, name):
    na = len(fs)

    def body(*refs):
        out_refs = refs[na:2 * na]
        send_sems, recv_sems = refs[2 * na:]
        x, y, c = _place()
        sends = [pltpu.make_async_remote_copy(
            src_ref=out_refs[a].at[c], dst_ref=out_refs[a].at[c], send_sem=send_sems.at[a], recv_sem=recv_sems.at[a],
            device_id=(x, y, 1 - c), device_id_type=MESH) for a in range(na)]
        recvs = [pltpu.make_async_remote_copy(
            src_ref=out_refs[a].at[c], dst_ref=out_refs[a].at[1 - c], send_sem=send_sems.at[a],
            recv_sem=recv_sems.at[a], device_id=(x, y, 1 - c), device_id_type=MESH) for a in range(na)]
        for cp in sends:
            cp.start()
        for cp in recvs:
            cp.wait_recv()
        for cp in sends:
            cp.wait_send()

    return pl.pallas_call(
        body, name=name, out_shape=tuple(_sds(f.shape, f.dtype) for f in fs),
        in_specs=[ANY] * na, out_specs=(ANY,) * na, input_output_aliases={a: a for a in range(na)},
        scratch_shapes=[pltpu.SemaphoreType.DMA((na,)), pltpu.SemaphoreType.DMA((na,))],
    )(*fs)


def _row_tile(r, n, itemsize=4, budget=1 << 20):
    if r * n * itemsize <= budget:
        return r
    best = None
    for tr in range(16, r, 16):
        if r % tr == 0 and tr * n * itemsize <= budget:
            best = tr
    assert best is not None, (r, n)
    return best


def _add_pair(g, land, cidx, name):
    ns, _, r, n = g.shape
    tr = _row_tile(r, n)

    def body(c_ref, a_ref, b_ref, o_ref, ob_ref):
        s = a_ref[...] + b_ref[...]
        o_ref[...] = s
        ob_ref[...] = s.astype(BF16)

    out = pl.BlockSpec((None, tr, n), lambda s, i, cr: (s, i, 0))
    return pl.pallas_call(
        body, name=name, out_shape=(_sds((ns, r, n), F32), _sds((ns, r, n), BF16)),
        grid_spec=pltpu.PrefetchScalarGridSpec(
            num_scalar_prefetch=1, grid=(ns, r // tr),
            in_specs=[pl.BlockSpec((None, None, tr, n), lambda s, i, cr: (s, cr[0], i, 0)), out],
            out_specs=(out, out)),
        compiler_params=_params("arbitrary", "arbitrary"),
    )(cidx, g, land)


def _add_chips(h, land, own_c, name):
    _, r, n = h.shape
    tr = _row_tile(r, n)

    def body(o_idx, h_ref, l_ref, o_ref):
        o_ref[...] = ((h_ref[...] + l_ref[0].astype(F32)) + l_ref[1].astype(F32)) + l_ref[2].astype(F32)

    return pl.pallas_call(
        body, name=name, out_shape=_sds((2, r, n), F32),
        grid_spec=pltpu.PrefetchScalarGridSpec(
            num_scalar_prefetch=1, grid=(r // tr,),
            in_specs=[pl.BlockSpec((None, tr, n), lambda i, o: (o[0], i, 0)),
                      pl.BlockSpec((3, tr, n), lambda i, o: (0, i, 0))],
            out_specs=pl.BlockSpec((None, tr, n), lambda i, o: (o[1], i, 0))),
        compiler_params=_params("arbitrary"),
    )(own_c, h, land)


def _adam_math(w, g, m, v):
    nm = ADAM_B1 * m + (1.0 - ADAM_B1) * g
    nv = ADAM_B2 * v + (1.0 - ADAM_B2) * (g * g)
    m_hat = nm / (1.0 - ADAM_B1 ** ADAM_STEP)
    v_hat = nv / (1.0 - ADAM_B2 ** ADAM_STEP)
    return -ADAM_LR * (m_hat / (jnp.sqrt(v_hat) + ADAM_EPS) + ADAM_WD * w), nm, nv


def _adamw(w, g, m, v, name):
    r, n = w.shape
    tr = _row_tile(r, n, budget=1 << 19)

    def body(w_ref, g_ref, m_ref, v_ref, d_ref, nm_ref, nv_ref):
        d_ref[...], nm_ref[...], nv_ref[...] = _adam_math(w_ref[...], g_ref[...], m_ref[...], v_ref[...])

    spec = pl.BlockSpec((tr, n), lambda i: (i, 0))
    return pl.pallas_call(
        body, name=name, out_shape=(_sds((r, n), F32),) * 3, grid=(r // tr,),
        in_specs=[spec] * 4, out_specs=(spec,) * 3, compiler_params=_params("arbitrary"),
    )(w, g, m, v)


def _adamw_small(ws, gs, ms, vs):
    k = len(ws)

    def body(*refs):
        ins, outs = refs[:4 * k], refs[4 * k:]
        for j in range(k):
            d, nm, nv = _adam_math(ins[j][...], ins[k + j][...], ins[2 * k + j][...], ins[3 * k + j][...])
            outs[j][...] = d
            outs[k + j][...] = nm
            outs[2 * k + j][...] = nv

    shapes = tuple(_sds(w.shape, F32) for w in ws)
    res = pl.pallas_call(body, name="adamw_small", out_shape=shapes * 3,
                         compiler_params=pltpu.CompilerParams(vmem_limit_bytes=VMEM_LIMIT))(*ws, *gs, *ms, *vs)
    return res[:k], res[k:2 * k], res[2 * k:]


def _ada_mod(c_all, w_sh, b_sh):
    b, _ = c_all.shape
    n = w_sh.shape[1]

    def body(c_ref, w_ref, b_ref, o_ref):
        cc = c_ref[...]
        ca = (cc * jax.nn.sigmoid(cc)).astype(BF16)
        o_ref[...] = _dot(ca, w_ref[...].astype(BF16)) + b_ref[...]

    return pl.pallas_call(body, name="ada_mod", out_shape=_sds((b, n), F32),
                          compiler_params=pltpu.CompilerParams(vmem_limit_bytes=VMEM_LIMIT))(c_all, w_sh, b_sh)


def _ada_bwd(c_all, dmod_all, dmod_sh, parts):
    b, d = c_all.shape
    n6 = dmod_all.shape[1]
    n = dmod_sh.shape[1]
    k = len(parts)

    def body(*refs):
        c_ref, da_ref, ds_ref = refs[:3]
        p_refs = refs[3:3 + k]
        dw_ref, db_ref = refs[3 + k:5 + k]
        s_refs = refs[5 + k:]
        cc = c_ref[...]
        ca = (cc * jax.nn.sigmoid(cc)).astype(BF16)
        dw_ref[...] = _dot_tn(ca, ds_ref[...].astype(BF16))
        db_ref[...] = jnp.sum(da_ref[...], axis=0, keepdims=True)
        for p_ref, s_ref in zip(p_refs, s_refs):
            tot = p_ref[0]
            for j in range(1, p_ref.shape[0]):
                tot = tot + p_ref[j]
            s_ref[...] = tot

    return pl.pallas_call(
        body, name="ada_bwd",
        out_shape=(_sds((d, n), F32), _sds((1, n6), F32)) + tuple(_sds(p.shape[1:], F32) for p in parts),
        compiler_params=pltpu.CompilerParams(vmem_limit_bytes=VMEM_LIMIT),
    )(c_all, dmod_all, dmod_sh, *parts)


def _fwd_in(x, g1, mod3, win_p, tm, tps):
    t, d = x.shape
    p_glu, p_q, npad = _layout(d)

    def body(x_ref, g_ref, mod_ref, w_hbm, h_ref, zm_ref, zglu_ref, zgate_ref, u0_ref, w_ref):
        _load_resident(pl.program_id(0), [(w_hbm, w_ref)])
        n, _ = _rms(x_ref[...])
        h = ((n * g_ref[...]) * (1.0 + mod_ref[1:2, :]) + mod_ref[0:1, :]).astype(BF16)
        h_ref[...] = h
        z = _dot(h, w_ref[...])
        zgate_ref[...] = z[:, :p_glu]
        zglu = z[:, p_glu:p_q]
        zglu_ref[...] = zglu
        zm_ref[...] = z[:, p_q:]
        u0_ref[...] = zglu[:, :CONV_CH] * jax.nn.sigmoid(zglu[:, CONV_CH:])

    return pl.pallas_call(
        body, name="fwd_in", grid=(t // tm,),
        out_shape=(_sds((t, d), BF16), _sds((t, MLA_IN), F32), _sds((t, 2 * CONV_CH), F32), _sds((t, 2 * d), F32),
                   _sds((t, CONV_CH), F32)),
        in_specs=[_row(tm, d), _full((1, d)), _modspec(d, tps), ANY],
        out_specs=(_row(tm, d), _row(tm, MLA_IN), _row(tm, 2 * CONV_CH), _row(tm, 2 * d), _row(tm, CONV_CH)),
        scratch_shapes=[pltpu.VMEM(win_p.shape, BF16)],
        compiler_params=_params("arbitrary"),
    )(x, g1, mod3, win_p)


def _mla_prep(zm, gql, gkvl, gq, gk, tabs, wuq_p, wk_p, wv_p, tm, tps):
    t = zm.shape[0]
    c_t, s1_t, s2_t = tabs
    tab = pl.BlockSpec((tm, LANES), lambda i: (i % tps, 0))

    def body(zm_ref, gql_ref, gkvl_ref, gq_ref, gk_ref, c_ref, s1_ref, s2_ref, wuq_ref, wk_ref, wv_ref,
             q_ref, k_ref, v_ref, qln_ref, kvn_ref):
        c, s1, s2 = c_ref[...], s1_ref[...], s2_ref[...]
        nq, _ = _rms(zm_ref[:, :Q_RANK])
        qln = (nq * gql_ref[...]).astype(BF16)
        qln_ref[...] = qln
        qpre = _dot(qln, wuq_ref[...])
        nkv, _ = _rms(zm_ref[:, Q_RANK:OFF_KV])
        kvn = (nkv * gkvl_ref[...]).astype(BF16)
        kvn_ref[...] = kvn
        knope = _dot(kvn, wk_ref[...])
        v_ref[...] = _dot(kvn, wv_ref[...]).astype(BF16)
        zkr_v = zm_ref[:, OFF_KV:]
        for hd in range(N_HEADS):
            sl = slice(hd * LANES, (hd + 1) * LANES)
            n, _ = _head_rms(qpre[:, sl])
            q_ref[:, sl] = _rope(n * gq_ref[...], c, s1, s2).astype(BF16)
            n, _ = _head_rms(knope[:, sl] + zkr_v)
            k_ref[:, sl] = _rope(n * gk_ref[...], c, s1, s2).astype(BF16)

    return pl.pallas_call(
        body, name="mla_prep", grid=(t // tm,),
        out_shape=(_sds((t, HW), BF16),) * 3 + (_sds((t, Q_RANK), BF16), _sds((t, KV_RANK), BF16)),
        in_specs=[_row(tm, MLA_IN), _full((1, Q_RANK)), _full((1, KV_RANK)),
                  _full((1, LANES)), _full((1, LANES)), tab, tab, tab,
                  _full(wuq_p.shape), _full(wk_p.shape), _full(wv_p.shape)],
        out_specs=(_row(tm, HW),) * 3 + (_row(tm, Q_RANK), _row(tm, KV_RANK)),
        compiler_params=_params("arbitrary"),
    )(zm, gql, gkvl, gq, gk, c_t, s1_t, s2_t, wuq_p, wk_p, wv_p)


SM_SCALE = QK_HEAD ** -0.5
EXP2_SCALE = SM_SCALE * 1.4426950408889634


def _diag_mask():
    rc = jnp.right_shift(lax.broadcasted_iota(jnp.int32, (BQ, 1), 0), CHUNK_SHIFT)
    cc = jnp.right_shift(lax.broadcasted_iota(jnp.int32, (1, BQ), 1), CHUNK_SHIFT)
    return rc >= cc


def _softmax_parts(q_i, k_ref, lo, e, mask):
    sd = jnp.where(mask, _dot_nt(q_i, k_ref[lo:e, :]), jnp.finfo(F32).min)
    m = jnp.max(sd, axis=-1, keepdims=True)
    if lo:
        sp = _dot_nt(q_i, k_ref[:lo, :])
        m = jnp.maximum(m, jnp.max(sp, axis=-1, keepdims=True))
    pd = jnp.exp2((sd - m) * EXP2_SCALE)
    l = jnp.sum(pd, axis=-1, keepdims=True)
    pp = None
    if lo:
        pp = jnp.exp2((sp - m) * EXP2_SCALE)
        l = l + jnp.sum(pp, axis=-1, keepdims=True)
    return pp, pd, l


def _attn_fwd(q, k, v, nseq, seq, gather=()):
    t = q.shape[0]
    na = len(gather)
    blk = pl.BlockSpec((seq, LANES), lambda b, h: (b, h))
    n_steps = nseq * N_HEADS

    def body(q_ref, k_ref, v_ref, *rest):
        o_ref = rest[na]
        if na:
            start, forward, finish = _gather8_phases(rest[:na], rest[na + 1:2 * na + 1], *rest[2 * na + 1:])
            step = pl.program_id(0) * N_HEADS + pl.program_id(1)
            pl.when(step == 0)(start)
            pl.when(step == (3 * n_steps) // 4)(forward)
        mask = _diag_mask()
        for i in range(seq // BQ):
            lo, e = i * BQ, (i + 1) * BQ
            pp, pd, l = _softmax_parts(q_ref[lo:e, :], k_ref, lo, e, mask)
            o = _dot(pd.astype(BF16), v_ref[lo:e, :])
            if lo:
                o = o + _dot(pp.astype(BF16), v_ref[:lo, :])
            o_ref[lo:e, :] = (o * (1.0 / l)).astype(BF16)
        if na:
            pl.when(step == n_steps - 1)(finish)

    res = pl.pallas_call(
        body, name="attn_fwd", grid=(nseq, N_HEADS), out_shape=(_sds((t, HW), BF16),) + _gather8_shapes(gather),
        in_specs=[blk, blk, blk] + [ANY] * na, out_specs=(blk,) + (ANY,) * na,
        scratch_shapes=_gather8_sems(na) if na else [],
        compiler_params=_params("arbitrary", "arbitrary"),
    )(q, k, v, *gather)
    return res[0], (_own_block_placed(res[1:], gather) if na else ())


def _fwd_mix(attn, u0, zgate, x, mod3, wo_p, cw, cb, lng, lnb, wpw, wout, tm, tps):
    t, d = x.shape
    hpt = tm // HALO

    def body(a_ref, u_ref, uh_ref, zg_ref, x_ref, mod_ref, wo_ref, cw_ref, cb_ref, lng_ref, lnb_ref, wpw_ref, wout_ref,
             x1_ref, mixed_ref, mpre_ref, ya_ref, yb_ref, u1_ref, u3_ref, ext_ref):
        i = pl.program_id(0)
        ya = _dot(a_ref[...], wo_ref[...])
        ya_ref[...] = ya
        first = (i % tps) == 0
        _fill_shifted(ext_ref, jnp.where(first, 0.0, uh_ref[...]), u_ref[...])
        acc = jnp.zeros((tm, CONV_CH), F32) + cb_ref[...]
        for kk in range(CONV_W):
            acc = acc + cw_ref[kk:kk + 1, :] * _shifted(ext_ref, HALO - (CONV_W - 1) + kk, tm)
        u1_ref[...] = acc
        mu = jnp.mean(acc, axis=-1, keepdims=True)
        xc = acc - mu
        rstd = lax.rsqrt(jnp.mean(xc * xc, axis=-1, keepdims=True) + EPS)
        l = (xc * rstd) * lng_ref[...] + lnb_ref[...]
        u3 = (l * jax.nn.sigmoid(l)).astype(BF16)
        u3_ref[...] = u3
        yb = _dot(u3, wpw_ref[...])
        yb_ref[...] = yb
        zg = zg_ref[...]
        mpre = (jax.nn.sigmoid(zg[:, :d]) * ya + jax.nn.sigmoid(zg[:, d:]) * yb).astype(BF16)
        mpre_ref[...] = mpre
        mixed = _dot(mpre, wout_ref[...])
        mixed_ref[...] = mixed
        x1_ref[...] = x_ref[...] + mod_ref[2:3, :] * mixed

    halo = pl.BlockSpec((HALO, CONV_CH), lambda i: (jnp.maximum(i * hpt - 1, 0), 0))
    return pl.pallas_call(
        body, name="fwd_mix", grid=(t // tm,),
        out_shape=(_sds((t, d), F32), _sds((t, d), F32), _sds((t, d), BF16), _sds((t, d), F32), _sds((t, d), F32),
                   _sds((t, CONV_CH), F32), _sds((t, CONV_CH), BF16)),
        in_specs=[_row(tm, HW), _row(tm, CONV_CH), halo, _row(tm, 2 * d), _row(tm, d), _modspec(d, tps),
                  _full(wo_p.shape), _full(cw.shape), _full((1, CONV_CH)), _full((1, CONV_CH)), _full((1, CONV_CH)),
                  _full(wpw.shape), _full(wout.shape)],
        out_specs=(_row(tm, d), _row(tm, d), _row(tm, d), _row(tm, d), _row(tm, d), _row(tm, CONV_CH),
                   _row(tm, CONV_CH)),
        scratch_shapes=[pltpu.VMEM((SUBLANES, tm + HALO, CONV_CH), F32)],
        compiler_params=_params("arbitrary"),
    )(attn, u0, u0, zgate, x, mod3, wo_p, cw, cb, lng, lnb, wpw, wout)


def _fwd_ffn(x1, target, g2, mod3, w1, w2, tm, tps):
    t, d = x1.shape
    dff = w1.shape[1]

    def body(x1_ref, tg_ref, g_ref, mod_ref, w1_hbm, w2_hbm,
             h2_ref, a_ref, r_ref, dy_ref, df_ref, dgate_ref, loss_ref, w1_ref, w2_ref):
        i = pl.program_id(0)
        _load_resident(i, [(w1_hbm, w1_ref), (w2_hbm, w2_ref)])
        x1v = x1_ref[...]
        gate2 = mod_ref[5:6, :]
        n, _ = _rms(x1v)
        h2 = ((n * g_ref[...]) * (1.0 + mod_ref[4:5, :]) + mod_ref[3:4, :]).astype(BF16)
        h2_ref[...] = h2
        a = _dot(h2, w1_ref[...])
        a_ref[...] = a
        r = jnp.square(jnp.maximum(a, 0.0)).astype(BF16)
        r_ref[...] = r
        f = _dot(r, w2_ref[...])
        e = (x1v + gate2 * f) - tg_ref[...]
        part = 0.5 * jnp.sum(jnp.mean(e * e, axis=-1, keepdims=True), axis=0, keepdims=True)
        _acc(loss_ref, jnp.broadcast_to(part, loss_ref.shape), i == 0)
        dy = e * (1.0 / d)
        dy_ref[...] = dy
        df_ref[...] = (dy * gate2).astype(BF16)
        _acc(dgate_ref, jnp.sum(dy * f, axis=0, keepdims=True), (i % tps) == 0)

    nseq = t // (tm * tps)
    return pl.pallas_call(
        body, name="fwd_ffn", grid=(t // tm,),
        out_shape=(_sds((t, d), BF16), _sds((t, dff), F32), _sds((t, dff), BF16), _sds((t, d), F32), _sds((t, d), BF16),
                   _sds((nseq, 1, d), F32), _sds((8, LANES), F32)),
        in_specs=[_row(tm, d), _row(tm, d), _full((1, d)), _modspec(d, tps), ANY, ANY],
        out_specs=(_row(tm, d), _row(tm, dff), _row(tm, dff), _row(tm, d), _row(tm, d), _seqv(d, tps),
                   _full((8, LANES))),
        scratch_shapes=[pltpu.VMEM(w1.shape, BF16), pltpu.VMEM(w2.shape, BF16)],
        compiler_params=_params("arbitrary"),
    )(x1, target, g2, mod3, w1, w2)


def _bwd_ffn(df, a, x1, dy, mixed, g2, mod3, w2t, w1t, tm, tps):
    t, d = x1.shape
    dff = a.shape[1]

    def body(df_ref, a_ref, x1_ref, dy_ref, mx_ref, g_ref, mod_ref, w2t_hbm, w1t_hbm,
             da_ref, dx1_ref, dmixed_ref, dshift_ref, dscale_ref, dgate1_ref, dg2_ref, w2t_ref, w1t_ref):
        i = pl.program_id(0)
        _load_resident(i, [(w2t_hbm, w2t_ref), (w1t_hbm, w1t_ref)])
        first_seq = (i % tps) == 0
        dr = _dot(df_ref[...], w2t_ref[...])
        da = (dr * (2.0 * jnp.maximum(a_ref[...], 0.0))).astype(BF16)
        da_ref[...] = da
        dh2 = _dot(da, w1t_ref[...])
        n, r = _rms(x1_ref[...])
        g = g_ref[...]
        sc1 = 1.0 + mod_ref[4:5, :]
        _acc(dshift_ref, jnp.sum(dh2, axis=0, keepdims=True), first_seq)
        _acc(dscale_ref, jnp.sum(dh2 * (n * g), axis=0, keepdims=True), first_seq)
        _acc(dg2_ref, jnp.sum((dh2 * sc1) * n, axis=0, keepdims=True), i == 0)
        dx1 = dy_ref[...] + _rms_bwd(n, r, (dh2 * sc1) * g)
        dx1_ref[...] = dx1
        _acc(dgate1_ref, jnp.sum(dx1 * mx_ref[...], axis=0, keepdims=True), first_seq)
        dmixed_ref[...] = (dx1 * mod_ref[2:3, :]).astype(BF16)

    nseq = t // (tm * tps)
    sv = _sds((nseq, 1, d), F32)
    return pl.pallas_call(
        body, name="bwd_ffn", grid=(t // tm,),
        out_shape=(_sds((t, dff), BF16), _sds((t, d), F32), _sds((t, d), BF16), sv, sv, sv, _sds((1, d), F32)),
        in_specs=[_row(tm, d), _row(tm, dff), _row(tm, d), _row(tm, d), _row(tm, d), _full((1, d)), _modspec(d, tps),
                  ANY, ANY],
        out_specs=(_row(tm, dff), _row(tm, d), _row(tm, d), _seqv(d, tps), _seqv(d, tps), _seqv(d, tps),
                   _full((1, d))),
        scratch_shapes=[pltpu.VMEM(w2t.shape, BF16), pltpu.VMEM(w1t.shape, BF16)],
        compiler_params=_params("arbitrary"),
    )(df, a, x1, dy, mixed, g2, mod3, w2t, w1t)


def _bwd_mix(dmixed, zgate, ya, yb, u1, lng, lnb, woutt, wot_p, wpwt, tm):
    t, d = ya.shape
    _, _, npad = _layout(d)

    def body(dm_ref, zg_ref, ya_ref, yb_ref, u1_ref, lng_ref, lnb_ref, woutt_ref, wot_ref, wpwt_ref,
             dya_ref, dyb_ref, dz_ref, do_ref, du1_ref, dlng_ref, dlnb_ref, dcb_ref):
        i = pl.program_id(0)
        dmpre = _dot(dm_ref[...], woutt_ref[...])
        zg = zg_ref[...]
        ga = jax.nn.sigmoid(zg[:, :d])
        gb = jax.nn.sigmoid(zg[:, d:])
        dya = (dmpre * ga).astype(BF16)
        dyb = (dmpre * gb).astype(BF16)
        dya_ref[...] = dya
        dyb_ref[...] = dyb
        dz_ref[:, :d] = ((dmpre * ya_ref[...]) * (ga * (1.0 - ga))).astype(BF16)
        dz_ref[:, d:] = ((dmpre * yb_ref[...]) * (gb * (1.0 - gb))).astype(BF16)
        do_ref[...] = _dot(dya, wot_ref[...]).astype(BF16)
        du3 = _dot(dyb, wpwt_ref[...])
        u1 = u1_ref[...]
        mu = jnp.mean(u1, axis=-1, keepdims=True)
        xc = u1 - mu
        rstd = lax.rsqrt(jnp.mean(xc * xc, axis=-1, keepdims=True) + EPS)
        nh = xc * rstd
        l = nh * lng_ref[...] + lnb_ref[...]
        sg = jax.nn.sigmoid(l)
        dl = du3 * (sg * (1.0 + l * (1.0 - sg)))
        _acc(dlng_ref, jnp.sum(dl * nh, axis=0, keepdims=True), i == 0)
        _acc(dlnb_ref, jnp.sum(dl, axis=0, keepdims=True), i == 0)
        dnh = dl * lng_ref[...]
        du1 = rstd * (dnh - jnp.mean(dnh, axis=-1, keepdims=True) - nh * jnp.mean(dnh * nh, axis=-1, keepdims=True))
        du1_ref[...] = du1
        _acc(dcb_ref, jnp.sum(du1, axis=0, keepdims=True), i == 0)

    cv = _sds((1, CONV_CH), F32)
    return pl.pallas_call(
        body, name="bwd_mix", grid=(t // tm,),
        out_shape=(_sds((t, d), BF16), _sds((t, d), BF16), _sds((t, npad), BF16), _sds((t, HW), BF16),
                   _sds((t, CONV_CH), F32), cv, cv, cv),
        in_specs=[_row(tm, d), _row(tm, 2 * d), _row(tm, d), _row(tm, d), _row(tm, CONV_CH), _full((1, CONV_CH)),
                  _full((1, CONV_CH)), _full(woutt.shape), _full(wot_p.shape), _full(wpwt.shape)],
        out_specs=(_row(tm, d), _row(tm, d), _row(tm, 2 * d), _row(tm, HW), _row(tm, CONV_CH),
                   _full((1, CONV_CH)), _full((1, CONV_CH)), _full((1, CONV_CH))),
        compiler_params=_params("arbitrary"),
    )(dmixed, zgate, ya, yb, u1, lng, lnb, woutt, wot_p, wpwt)


def _bwd_conv(dz, du1, u0, zglu, cw, tm, tps):
    t = du1.shape[0]
    d = (dz.shape[1] - MLA_IN - 2 * CONV_CH) // 2
    p_glu, _, _ = _layout(d)
    hpt = tm // HALO
    last_blk = t // HALO - 1

    def body(dz_hbm, du_ref, dun_ref, u_ref, uh_ref, zl_ref, cw_ref, dzl_ref, dcw_ref, ext_ref, dext_ref):
        i = pl.program_id(0)
        first = (i % tps) == 0
        last = (i % tps) == (tps - 1)
        _fill_shifted(ext_ref, jnp.where(first, 0.0, uh_ref[...]), u_ref[...])
        du = du_ref[...]
        _fill_shifted(dext_ref, du, jnp.where(last, 0.0, dun_ref[...]))

        @pl.when(i == 0)
        def _():
            dcw_ref[...] = jnp.zeros_like(dcw_ref)

        du0 = jnp.zeros((tm, CONV_CH), F32)
        for kk in range(CONV_W):
            dcw_ref[kk:kk + 1, :] += jnp.sum(du * _shifted(ext_ref, HALO - (CONV_W - 1) + kk, tm), axis=0, keepdims=True)
            du0 = du0 + cw_ref[kk:kk + 1, :] * _shifted(dext_ref, CONV_W - 1 - kk, tm)
        zl = zl_ref[...]
        ga = zl[:, :CONV_CH]
        sb = jax.nn.sigmoid(zl[:, CONV_CH:])
        dzl_ref[:, :CONV_CH] = (du0 * sb).astype(BF16)
        dzl_ref[:, CONV_CH:] = ((du0 * ga) * (sb * (1.0 - sb))).astype(BF16)

    prev = pl.BlockSpec((HALO, CONV_CH), lambda i: (jnp.maximum(i * hpt - 1, 0), 0))
    nxt = pl.BlockSpec((HALO, CONV_CH), lambda i: (jnp.minimum((i + 1) * hpt, last_blk), 0))
    glu_blk = p_glu // (2 * CONV_CH)
    return pl.pallas_call(
        body, name="bwd_conv", grid=(t // tm,),
        out_shape=(_sds(dz.shape, BF16), _sds(cw.shape, F32)),
        in_specs=[ANY, _row(tm, CONV_CH), nxt, _row(tm, CONV_CH), prev, _row(tm, 2 * CONV_CH), _full(cw.shape)],
        out_specs=(pl.BlockSpec((tm, 2 * CONV_CH), lambda i: (i, glu_blk)), _full(cw.shape)),
        scratch_shapes=[pltpu.VMEM((SUBLANES, tm + HALO, CONV_CH), F32)] * 2,
        input_output_aliases={0: 0},
        compiler_params=_params("arbitrary"),
    )(dz, du1, du1, u0, u0, zglu, cw)


def _attn_bwd(q, k, v, do, nseq, seq, scatter=()):
    t = q.shape[0]
    ns = len(scatter)
    blk = pl.BlockSpec((seq, LANES), lambda b, h: (b, h))
    n_steps = nseq * N_HEADS

    def body(q_ref, k_ref, v_ref, do_ref, *rest):
        dq_ref, dk_ref, dv_ref = rest[ns:ns + 3]
        dka_ref, dva_ref = rest[2 * ns + 3:2 * ns + 5]
        if ns:
            start, finish = _scatter_phases(rest[:ns], rest[ns + 3:2 * ns + 3], *rest[2 * ns + 5:])
            step = pl.program_id(0) * N_HEADS + pl.program_id(1)
            pl.when(step == 0)(start)
        dka_ref[...] = jnp.zeros_like(dka_ref)
        dva_ref[...] = jnp.zeros_like(dva_ref)
        mask = _diag_mask()
        for i in range(seq // BQ):
            lo, e = i * BQ, (i + 1) * BQ
            q_i = q_ref[lo:e, :]
            do_i = do_ref[lo:e, :]
            pp, pd, l = _softmax_parts(q_i, k_ref, lo, e, mask)
            inv = 1.0 / l
            pd = pd * inv
            dpd = _dot_nt(do_i, v_ref[lo:e, :])
            delta = jnp.sum(pd * dpd, axis=-1, keepdims=True)
            if lo:
                pp = pp * inv
                dpp = _dot_nt(do_i, v_ref[:lo, :])
                delta = delta + jnp.sum(pp * dpp, axis=-1, keepdims=True)
            dsd = (pd * (dpd - delta)).astype(BF16)
            dq = _dot(dsd, k_ref[lo:e, :])
            dka_ref[lo:e, :] += _dot_tn(dsd, q_i)
            dva_ref[lo:e, :] += _dot_tn(pd.astype(BF16), do_i)
            if lo:
                dsp = (pp * (dpp - delta)).astype(BF16)
                dq = dq + _dot(dsp, k_ref[:lo, :])
                dka_ref[:lo, :] += _dot_tn(dsp, q_i)
                dva_ref[:lo, :] += _dot_tn(pp.astype(BF16), do_i)
            dq_ref[lo:e, :] = dq * SM_SCALE
        dk_ref[...] = dka_ref[...] * SM_SCALE
        dv_ref[...] = dva_ref[...].astype(BF16)
        if ns:
            pl.when(step == n_steps - 1)(finish)

    res = pl.pallas_call(
        body, name="attn_bwd", grid=(nseq, N_HEADS),
        out_shape=(_sds((t, HW), F32), _sds((t, HW), F32), _sds((t, HW), BF16)) + _scatter_shapes(scatter),
        in_specs=[blk] * 4 + [ANY] * ns, out_specs=(blk,) * 3 + (ANY,) * ns,
        scratch_shapes=[pltpu.VMEM((seq, LANES), F32), pltpu.VMEM((seq, LANES), F32)]
        + (_scatter_sems(ns) if ns else []),
        compiler_params=_params("arbitrary", "arbitrary"),
    )(q, k, v, do, *scatter)
    return res[0], res[1], res[2], res[3:]


def _mla_bwd(dz, dq, dk, dv, zm, gql, gkvl, gq, gk, tabs, wuq_p, wk_p, wuqt_p, wkt_p, wvt_p, tm, tps):
    t = zm.shape[0]
    d = (dz.shape[1] - MLA_IN - 2 * CONV_CH) // 2
    _, p_q, _ = _layout(d)
    c_t, s1_t, s2_t = tabs
    tab = pl.BlockSpec((tm, LANES), lambda i: (i % tps, 0))

    def body(dz_hbm, dq_ref, dk_ref, dv_ref, zm_ref, gql_ref, gkvl_ref, gq_ref, gk_ref, c_ref, s1_ref, s2_ref,
             wuq_ref, wk_ref, wuqt_ref, wkt_ref, wvt_ref,
             dzm_ref, dqpre_ref, dkh_ref, dgq_ref, dgk_ref, dgql_ref, dgkvl_ref):
        i = pl.program_id(0)
        c, s1, s2 = c_ref[...], s1_ref[...], s2_ref[...]
        nq, rq = _rms(zm_ref[:, :Q_RANK])
        qpre = _dot((nq * gql_ref[...]).astype(BF16), wuq_ref[...])
        nkv, rkv = _rms(zm_ref[:, Q_RANK:OFF_KV])
        knope = _dot((nkv * gkvl_ref[...]).astype(BF16), wk_ref[...])
        zkr_v = zm_ref[:, OFF_KV:]
        dgq = jnp.zeros((1, LANES), F32)
        dgk = jnp.zeros((1, LANES), F32)
        dzkr = jnp.zeros((tm, LANES), F32)
        for hd in range(N_HEADS):
            sl = slice(hd * LANES, (hd + 1) * LANES)
            n, r = _head_rms(qpre[:, sl])
            dyr = _rope_t(dq_ref[:, sl], c, s1, s2)
            dgq = dgq + jnp.sum(dyr * n, axis=0, keepdims=True)
            dqpre_ref[:, sl] = _head_rms_bwd(n, r, dyr * gq_ref[...]).astype(BF16)
            n, r = _head_rms(knope[:, sl] + zkr_v)
            dyr = _rope_t(dk_ref[:, sl], c, s1, s2)
            dgk = dgk + jnp.sum(dyr * n, axis=0, keepdims=True)
            dkh = _head_rms_bwd(n, r, dyr * gk_ref[...])
            dzkr = dzkr + dkh
            dkh_ref[:, sl] = dkh.astype(BF16)
        _acc(dgq_ref, dgq[:, :QK_HEAD], i == 0)
        _acc(dgk_ref, dgk[:, :QK_HEAD], i == 0)
        dzm_ref[:, OFF_KV:] = dzkr.astype(BF16)
        dqln = _dot(dqpre_ref[...], wuqt_ref[...])
        _acc(dgql_ref, jnp.sum(dqln * nq, axis=0, keepdims=True), i == 0)
        dzm_ref[:, :Q_RANK] = _rms_bwd(nq, rq, dqln * gql_ref[...]).astype(BF16)
        dkvn = _dot(dkh_ref[...], wkt_ref[...]) + _dot(dv_ref[...], wvt_ref[...])
        _acc(dgkvl_ref, jnp.sum(dkvn * nkv, axis=0, keepdims=True), i == 0)
        dzm_ref[:, Q_RANK:OFF_KV] = _rms_bwd(nkv, rkv, dkvn * gkvl_ref[...]).astype(BF16)

    return pl.pallas_call(
        body, name="mla_bwd", grid=(t // tm,),
        out_shape=(_sds(dz.shape, BF16), _sds((t, HW), BF16), _sds((t, HW), BF16), _sds((1, QK_HEAD), F32),
                   _sds((1, QK_HEAD), F32), _sds((1, Q_RANK), F32), _sds((1, KV_RANK), F32)),
        in_specs=[ANY, _row(tm, HW), _row(tm, HW), _row(tm, HW), _row(tm, MLA_IN),
                  _full((1, Q_RANK)), _full((1, KV_RANK)), _full((1, LANES)), _full((1, LANES)), tab, tab, tab,
                  _full(wuq_p.shape), _full(wk_p.shape), _full(wuqt_p.shape), _full(wkt_p.shape), _full(wvt_p.shape)],
        out_specs=(pl.BlockSpec((tm, MLA_IN), lambda i: (i, p_q // MLA_IN)), _row(tm, HW), _row(tm, HW),
                   _full((1, QK_HEAD)), _full((1, QK_HEAD)), _full((1, Q_RANK)), _full((1, KV_RANK))),
        input_output_aliases={0: 0},
        compiler_params=_params("arbitrary"),
    )(dz, dq, dk, dv, zm, gql, gkvl, gq, gk, c_t, s1_t, s2_t, wuq_p, wk_p, wuqt_p, wkt_p, wvt_p)


def _bwd_in(dz, x, dx1, g1, mod3, wint_p, tm, tps, scatter=()):
    t, d = x.shape
    npad = dz.shape[1]

    ns = len(scatter)
    n_steps = t // tm

    def body(dz_ref, x_ref, dx1_ref, g_ref, mod_ref, wt_hbm, *rest):
        gx_ref, dshift_ref, dscale_ref, dg1_ref = rest[ns:ns + 4]
        wt_ref = rest[2 * ns + 4]
        i = pl.program_id(0)
        if ns:
            start, finish = _scatter_phases(rest[:ns], rest[ns + 4:2 * ns + 4], *rest[2 * ns + 5:])
            pl.when(i == 0)(start)
        _load_resident(i, [(wt_hbm, wt_ref)])
        first_seq = (i % tps) == 0
        dh = _dot(dz_ref[...], wt_ref[...])
        n, r = _rms(x_ref[...])
        g = g_ref[...]
        sc1 = 1.0 + mod_ref[1:2, :]
        _acc(dshift_ref, jnp.sum(dh, axis=0, keepdims=True), first_seq)
        _acc(dscale_ref, jnp.sum(dh * (n * g), axis=0, keepdims=True), first_seq)
        _acc(dg1_ref, jnp.sum((dh * sc1) * n, axis=0, keepdims=True), i == 0)
        gx_ref[...] = dx1_ref[...] + _rms_bwd(n, r, (dh * sc1) * g)
        if ns:
            pl.when(i == n_steps - 1)(finish)

    nseq = t // (tm * tps)
    sv = _sds((nseq, 1, d), F32)
    res = pl.pallas_call(
        body, name="bwd_in", grid=(n_steps,),
        out_shape=(_sds((t, d), F32), sv, sv, _sds((1, d), F32)) + _scatter_shapes(scatter),
        in_specs=[_row(tm, npad), _row(tm, d), _row(tm, d), _full((1, d)), _modspec(d, tps), ANY] + [ANY] * ns,
        out_specs=(_row(tm, d), _seqv(d, tps), _seqv(d, tps), _full((1, d))) + (ANY,) * ns,
        scratch_shapes=[pltpu.VMEM(wint_p.shape, BF16)] + (_scatter_sems(ns) if ns else []),
        compiler_params=_params("arbitrary"),
    )(dz, x, dx1, g1, mod3, wint_p, *scatter)
    return res[0], res[1], res[2], res[3], res[4:]


def _tile_of(n, choices):
    for c in choices:
        if n % c == 0:
            return c
    return n


def _tn_matmul(a, b, name, col_shards=0):
    t, k = a.shape
    n = b.shape[1]
    tk = _tile_of(k, (1024, 512, 256, 128))
    tn = n // col_shards if col_shards else _tile_of(n, (1024, 896, 768, 512, 384, 256, 128))
    tt = _tile_of(t, (1024, 512, 256))

    def body(a_ref, b_ref, o_ref):
        _acc(o_ref, _dot_tn(a_ref[...], b_ref[...]), pl.program_id(2) == 0)

    if col_shards:
        out_shape, out_spec = _sds((col_shards, k, tn), F32), pl.BlockSpec((None, tk, tn), lambda i, j, s: (j, i, 0))
    else:
        out_shape, out_spec = _sds((k, n), F32), pl.BlockSpec((tk, tn), lambda i, j, s: (i, j))
    return pl.pallas_call(
        body, name=name, grid=(k // tk, n // tn, t // tt), out_shape=out_shape,
        in_specs=[pl.BlockSpec((tt, tk), lambda i, j, s: (s, i)), pl.BlockSpec((tt, tn), lambda i, j, s: (s, j))],
        out_specs=out_spec, compiler_params=_params("arbitrary", "arbitrary", "arbitrary"),
    )(a, b)


N_SHARD = 4
COL_SHARDED = ("w_in", "w_uq", "w_ukv", "w_o_mla", "w_pw_out", "w_ff1")
ROW_SHARDED = ("w_out", "w_ff2")
BIG = ("w_in", "w_uq", "w_ukv", "w_o_mla", "w_pw_out", "w_out", "w_ff1", "w_ff2")
SMALL = ("norm1_g", "q_latent_g", "kv_latent_g", "qk_norm_q_g", "qk_norm_k_g", "conv_b", "conv_ln_g", "conv_ln_b",
         "norm2_g")
WEIGHTS = ("w_ada", "b_ada", "norm1_g", "w_in", "q_latent_g", "w_uq", "kv_latent_g", "w_ukv", "qk_norm_q_g",
           "qk_norm_k_g", "w_o_mla", "conv_w", "conv_b", "conv_ln_g", "conv_ln_b", "w_pw_out", "w_out", "norm2_g",
           "w_ff1", "w_ff2")


def _pad_heads(w, width):
    k = w.shape[0]
    w3 = w.reshape(k, N_HEADS, width)
    return jnp.pad(w3, ((0, 0), (0, 0), (0, LANES - width))).reshape(k, HW)


def _unpad_heads(g, width):
    k = g.shape[0]
    return g.reshape(k, N_HEADS, LANES)[:, :, :width].reshape(k, N_HEADS * width)


def _pad_win(w):
    d = w.shape[0]
    z = lambda n: jnp.zeros((d, n), w.dtype)
    return jnp.concatenate([w[:, OFF_GLU:], w[:, OFF_KR:OFF_GLU], w[:, :OFF_KV], z(KR_LANE), w[:, OFF_KV:OFF_KR],
                            z(LANES - KR_LANE - QK_ROPE)], axis=1)


def _unpad_win(g):
    d = g.shape[0]
    p_glu, p_q, _ = _layout(d)
    kr = p_q + OFF_KV + KR_LANE
    return jnp.concatenate([g[:, p_q:p_q + OFF_KV], g[:, kr:kr + QK_ROPE], g[:, p_glu:p_q], g[:, :p_glu]], axis=1)


def _col_shards(g):
    k, n = g.shape
    return g.reshape(k, N_SHARD, n // N_SHARD).transpose(1, 0, 2)


def _from_shards(g, name):
    ns, ks, nn = g.shape
    if name in ROW_SHARDED:
        return g.reshape(ns * ks, nn)
    return g.transpose(1, 0, 2).reshape(ks, ns * nn)


EARLY = ("w_in", "w_uq", "w_ukv")
LATE = ("w_o_mla", "w_pw_out", "w_out", "w_ff1", "w_ff2")


def _assemble(names, gathered):
    return {n: _from_shards(g.reshape((N_SHARD, 2 * g.shape[1]) + g.shape[2:]), n) for n, g in zip(names, gathered)}


def _pair_reduce(names, gws, tag):
    g2 = [g.reshape(N_SHARD, 2, g.shape[1] // 2, g.shape[2]) for g in gws]
    from_sibling = _pair_swap(g2, "grad_pair_swap_" + tag)
    cidx = lax.axis_index("c").reshape(1).astype(jnp.int32)
    return [_add_pair(g, l, cidx, "pair_sum_" + n) for n, g, l in zip(names, g2, from_sibling)]


def _local_step(x, target, mod, sp, w, late=None, tm=256):
    comm = late is not None
    w = dict(w)
    nseq, seq, d = x.shape
    t = nseq * seq
    tps = seq // tm
    xf = x.reshape(t, d)
    tg = target.reshape(t, d)
    mod3 = mod.reshape(nseq, N_MOD, d)

    win_p = _pad_win(w["w_in"])
    wuq_p = _pad_heads(w["w_uq"], QK_HEAD)
    wkv3 = w["w_ukv"].reshape(KV_RANK, N_HEADS, QK_NOPE + V_HEAD)
    wk_p = _pad_heads(wkv3[:, :, :QK_NOPE].reshape(KV_RANK, -1), QK_NOPE)
    wv_p = _pad_heads(wkv3[:, :, QK_NOPE:].reshape(KV_RANK, -1), V_HEAD)
    cw = jnp.pad(w["conv_w"], ((0, HALO - CONV_W), (0, 0)))
    pad_g = lambda g: jnp.pad(g, ((0, 0), (0, LANES - QK_HEAD)))
    gq, gk = pad_g(sp["qk_norm_q_g"]), pad_g(sp["qk_norm_k_g"])
    tabs = _rope_tables(seq)

    h, zm, zglu, zgate, u0 = _fwd_in(xf, sp["norm1_g"], mod3, win_p, tm, tps)
    q, k, v, qln, kvn = _mla_prep(zm, sp["q_latent_g"], sp["kv_latent_g"], gq, gk, tabs, wuq_p, wk_p, wv_p, tm, tps)
    attn, gathered = _attn_fwd(q, k, v, nseq, seq, tuple(late) if comm else ())
    if comm:
        w.update(_assemble(LATE, gathered))
    wo_p = jnp.pad(w["w_o_mla"].reshape(N_HEADS, V_HEAD, d), ((0, 0), (0, LANES - V_HEAD), (0, 0))).reshape(HW, d)
    x1, mixed, mpre, ya, yb, u1, u3 = _fwd_mix(attn, u0, zgate, xf, mod3, wo_p, cw, sp["conv_b"], sp["conv_ln_g"],
                                               sp["conv_ln_b"], w["w_pw_out"], w["w_out"], tm, tps)
    h2, a, r, dy, df, dgate2, loss_acc = _fwd_ffn(x1, tg, sp["norm2_g"], mod3, w["w_ff1"], w["w_ff2"], tm, tps)
    da, dx1, dmixed, dshift2, dscale2, dgate1, dg2 = _bwd_ffn(df, a, x1, dy, mixed, sp["norm2_g"], mod3,
                                                              w["w_ff2"].T, w["w_ff1"].T, tm, tps)
    dya, dyb, dz, do, du1, dlng, dlnb, dcb = _bwd_mix(dmixed, zgate, ya, yb, u1, sp["conv_ln_g"], sp["conv_ln_b"],
                                                      w["w_out"].T, wo_p.T, w["w_pw_out"].T, tm)
    dz, dcw = _bwd_conv(dz, du1, u0, zglu, cw, tm, tps)
    dwo = _tn_matmul(attn, dya, "dw_o").reshape(N_HEADS, LANES, d)[:, :V_HEAD].reshape(MLA_WIDTH, d)
    gw = {
        "conv_w": dcw,
        "w_o_mla": _col_shards(dwo),
        "w_pw_out": _tn_matmul(u3, dyb, "dw_pw", N_SHARD),
        "w_out": _tn_matmul(mpre, dmixed, "dw_out").reshape(N_SHARD, d // N_SHARD, d),
        "w_ff1": _tn_matmul(h2, da, "dw_ff1", N_SHARD),
        "w_ff2": _tn_matmul(r, df, "dw_ff2").reshape(N_SHARD, -1, d),
    }
    pair_late = _pair_reduce(LATE, [gw[n] for n in LATE], "late") if comm else []
    dq, dk, dv, land_late = _attn_bwd(q, k, v, do, nseq, seq, tuple(p[1] for p in pair_late))
    dz, dqpre, dkh, dgq, dgk, dgql, dgkvl = _mla_bwd(dz, dq, dk, dv, zm, sp["q_latent_g"], sp["kv_latent_g"], gq, gk,
                                                      tabs, wuq_p, wk_p, wuq_p.T, wk_p.T, wv_p.T, tm, tps)
    dwk_p = _tn_matmul(kvn, dkh, "dw_uk")
    dwv_p = _tn_matmul(kvn, dv, "dw_uv")
    dwkv = jnp.concatenate([dwk_p.reshape(KV_RANK, N_HEADS, LANES)[:, :, :QK_NOPE],
                            dwv_p.reshape(KV_RANK, N_HEADS, LANES)[:, :, :V_HEAD]], axis=2).reshape(KV_RANK, -1)
    gw["w_in"] = _col_shards(_unpad_win(_tn_matmul(h, dz, "dw_in")))
    gw["w_uq"] = _col_shards(_unpad_heads(_tn_matmul(qln, dqpre, "dw_uq"), QK_HEAD))
    gw["w_ukv"] = _col_shards(dwkv)
    pair_early = _pair_reduce(EARLY, [gw[n] for n in EARLY], "early") if comm else []
    gx, dshift1, dscale1, dg1, land_early = _bwd_in(dz, xf, dx1, sp["norm1_g"], mod3, win_p.T, tm, tps,
                                                    tuple(p[1] for p in pair_early))
    if comm:
        for n, p, l in zip(LATE + EARLY, pair_late + pair_early, land_late + land_early):
            gw[n] = (p[0], l)
    gs = {
        "norm1_g": dg1, "q_latent_g": dgql, "kv_latent_g": dgkvl, "qk_norm_q_g": dgq, "qk_norm_k_g": dgk,
        "conv_b": dcb, "conv_ln_g": dlng, "conv_ln_b": dlnb, "norm2_g": dg2,
    }
    dmod = jnp.concatenate([dshift1, dscale1, dgate1, dshift2, dscale2, dgate2], axis=2).reshape(nseq, N_MOD * d)
    return loss_acc[0, 0], gx.reshape(nseq, seq, d), dmod, gw, gs


def kernel(x, c, w_ada, b_ada, norm1_g, w_in, q_latent_g, w_uq, kv_latent_g, w_ukv, qk_norm_q_g, qk_norm_k_g, w_o_mla, conv_w, conv_b, conv_ln_g, conv_ln_b, w_pw_out, w_out, norm2_g, w_ff1, w_ff2, loss_target, m_w_ada, m_b_ada, m_norm1_g, m_w_in, m_q_latent_g, m_w_uq, m_kv_latent_g, m_w_ukv, m_qk_norm_q_g, m_qk_norm_k_g, m_w_o_mla, m_conv_w, m_conv_b, m_conv_ln_g, m_conv_ln_b, m_w_pw_out, m_w_out, m_norm2_g, m_w_ff1, m_w_ff2, v_w_ada, v_b_ada, v_norm1_g, v_w_in, v_q_latent_g, v_w_uq, v_kv_latent_g, v_w_ukv, v_qk_norm_q_g, v_qk_norm_k_g, v_w_o_mla, v_conv_w, v_conv_b, v_conv_ln_g, v_conv_ln_b, v_w_pw_out, v_w_out, v_norm2_g, v_w_ff1, v_w_ff2):
    given = dict(locals())
    wts = {n: given[n][0] for n in WEIGHTS}
    mom = {n: given["m_" + n][0] for n in WEIGHTS}
    var = {n: given["v_" + n][0] for n in WEIGHTS}
    vec = lambda a: a.reshape(1, -1)
    nseq, seq, d = x.shape
    ix, iy, ic = _place()
    shard = 2 * ix + iy

    half = lambda n: lax.dynamic_slice_in_dim(wts[n].astype(BF16), ic * (wts[n].shape[0] // 2), wts[n].shape[0] // 2,
                                              axis=0)
    gathered = _all_gather8([half(n) for n in EARLY] + [wts["conv_w"], c], "gather_weights")
    full = _assemble(EARLY, gathered)
    full["conv_w"] = _from_shards(gathered[-2][0::2], "conv_w")
    c_all = gathered[-1].reshape(8 * nseq, d)

    n_ada = wts["w_ada"].shape[1]
    b_sh = lax.dynamic_slice_in_dim(vec(wts["b_ada"]), shard * n_ada, n_ada, axis=1)
    mod_sh = _ada_mod(c_all, wts["w_ada"], b_sh)
    hb = 4 * nseq
    mod_blk = lax.dynamic_slice_in_dim(mod_sh, ic * hb, hb, axis=0)
    (mod_all,) = _all_gather8([mod_blk], "gather_mod")
    mod_mine = lax.dynamic_slice_in_dim(mod_all, (2 * iy + ic) * nseq, nseq, axis=1)
    mod = jnp.concatenate([lax.dynamic_index_in_dim(mod_mine, 2 * s + ix, axis=0, keepdims=False)
                           for s in range(N_SHARD)], axis=1)

    sp = {n: vec(wts[n]) for n in SMALL}
    loss_part, grad_x, dmod, gw, gs = _local_step(x, loss_target, mod, sp, full, [half(n) for n in LATE])
    loss = lax.psum(loss_part, ("x", "y", "c"))

    parts = _all_gather8([dmod, gw["conv_w"]] + [gs[n] for n in SMALL], "gather_small")
    dmod_all = parts[0].reshape(8 * nseq, N_MOD * d)
    dmod_sh = lax.dynamic_slice_in_dim(dmod_all, shard * n_ada, n_ada, axis=1)
    res = _ada_bwd(c_all, dmod_all, dmod_sh, parts[1:])
    grads = {"w_ada": res[0], "b_ada": res[1]}
    n_cw = wts["conv_w"].shape[1]
    grads["conv_w"] = lax.dynamic_slice_in_dim(res[2], shard * n_cw, n_cw, axis=1)[:CONV_W]
    for n, g in zip(SMALL, res[3:]):
        grads[n] = g

    own_c = jnp.stack([shard, ic]).astype(jnp.int32)
    mine_sum = [_add_chips(gw[n][0], gw[n][1], own_c, "chip_sum_" + n) for n in BIG]
    for n, g in zip(BIG, _pair_gather(mine_sum, "grad_pair_gather")):
        grads[n] = g.reshape(wts[n].shape)

    delta, new_m, new_v = {}, {}, {}
    for n in BIG + ("w_ada",):
        delta[n], new_m[n], new_v[n] = _adamw(wts[n], grads[n], mom[n], var[n], "adamw_" + n)
    rest = ("b_ada", "conv_w") + SMALL
    as2d = lambda a: a if a.ndim == 2 else vec(a)
    res = _adamw_small(*[[as2d(t[n]) for n in rest] for t in (wts, grads, mom, var)])
    for dst, arrs in zip((delta, new_m, new_v), res):
        for n, a in zip(rest, arrs):
            dst[n] = a

    outs = [loss, grad_x]
    for group in (grads, delta, new_m, new_v):
        outs += [group[n].reshape(given[n].shape) for n in WEIGHTS]
    return tuple(outs)
```

```python
import jax
import jax.numpy as jnp
from jax import lax
from jax.experimental import pallas as pl
from jax.experimental.pallas import tpu as pltpu

F32 = jnp.float32
BF16 = jnp.bfloat16
MESH = pl.DeviceIdType.MESH
ANY = pl.BlockSpec(memory_space=pl.ANY)

CHUNK = 64
CHUNK_SHIFT = 6
N_HEADS = 8
QK_NOPE = 64
QK_ROPE = 32
QK_HEAD = QK_NOPE + QK_ROPE
V_HEAD = 64
Q_RANK = 256
KV_RANK = 128
MLA_WIDTH = N_HEADS * V_HEAD
CONV_CH = 512
CONV_W = 31
ROPE_THETA = 10000.0
EPS = 1e-6
LANES = 128
SUBLANES = 8
HW = N_HEADS * LANES
OFF_KV = Q_RANK + KV_RANK
OFF_KR = OFF_KV + QK_ROPE
OFF_GLU = OFF_KR + 2 * CONV_CH
KR_LANE = QK_NOPE
MLA_IN = Q_RANK + KV_RANK + LANES
HALO = 32
N_MOD = 6

ADAM_LR = 0.001
ADAM_B1 = 0.9
ADAM_B2 = 0.999
ADAM_EPS = 1e-08
ADAM_WD = 0.01
ADAM_STEP = 10

VMEM_LIMIT = 56 * 1024 * 1024
BQ = 256


def _layout(d):
    p_glu = 2 * d
    p_q = p_glu + 2 * CONV_CH
    return p_glu, p_q, p_q + MLA_IN


def _params(*sem):
    return pltpu.CompilerParams(dimension_semantics=sem, vmem_limit_bytes=VMEM_LIMIT)


def _dot(a, b):
    return jnp.dot(a, b, preferred_element_type=F32)


def _dot_tn(a, b):
    return lax.dot_general(a, b, (((0,), (0,)), ((), ())), preferred_element_type=F32)


def _dot_nt(a, b):
    return lax.dot_general(a, b, (((1,), (1,)), ((), ())), preferred_element_type=F32)


def _acc(ref, val, first):
    @pl.when(first)
    def _():
        ref[...] = val

    @pl.when(jnp.logical_not(first))
    def _():
        ref[...] += val


def _rms(x):
    r = lax.rsqrt(jnp.mean(x * x, axis=-1, keepdims=True) + EPS)
    return x * r, r


def _rms_bwd(n, r, dn):
    return r * (dn - n * jnp.mean(dn * n, axis=-1, keepdims=True))


def _head_rms(sl):
    r = lax.rsqrt(jnp.sum(sl * sl, axis=-1, keepdims=True) * (1.0 / QK_HEAD) + EPS)
    return sl * r, r


def _head_rms_bwd(n, r, dn):
    return r * (dn - n * (jnp.sum(dn * n, axis=-1, keepdims=True) * (1.0 / QK_HEAD)))


def _rope(x, c, s1, s2):
    return x * c + pltpu.roll(x, QK_ROPE // 2, 1) * s1 + pltpu.roll(x, LANES - QK_ROPE // 2, 1) * s2


def _rope_t(dy, c, s1, s2):
    return dy * c + pltpu.roll(dy * s1, LANES - QK_ROPE // 2, 1) + pltpu.roll(dy * s2, QK_ROPE // 2, 1)


def _rope_tables(seq):
    half = QK_ROPE // 2
    inv_freq = ROPE_THETA ** (-jnp.arange(0, QK_ROPE, 2, dtype=F32) / QK_ROPE)
    ang = jnp.arange(seq, dtype=F32)[:, None] * inv_freq[None, :]
    cos, sin = jnp.cos(ang), jnp.sin(ang)
    z = lambda n: jnp.zeros((seq, n), F32)
    tail = LANES - QK_HEAD
    c = jnp.concatenate([jnp.ones((seq, QK_NOPE), F32), cos, cos, jnp.ones((seq, tail), F32)], axis=1)
    s1 = jnp.concatenate([z(QK_NOPE + half), sin, z(tail)], axis=1)
    s2 = jnp.concatenate([z(QK_NOPE), -sin, z(half + tail)], axis=1)
    return c, s1, s2


def _row(tm, w):
    return pl.BlockSpec((tm, w), lambda i: (i, 0))


def _modspec(d, tps):
    return pl.BlockSpec((None, N_MOD, d), lambda i: (i // tps, 0, 0))


def _seqv(w, tps):
    return pl.BlockSpec((None, 1, w), lambda i: (i // tps, 0, 0))


def _full(shape):
    return pl.BlockSpec(shape, lambda i: tuple(0 for _ in shape))


def _sds(shape, dtype):
    return jax.ShapeDtypeStruct(shape, dtype)


CONV_ROWS = 64
CONV_LC = CONV_CH // LANES


def _lane_chunks():
    return [(lc, slice(lc * LANES, (lc + 1) * LANES)) for lc in range(CONV_LC)]


def _fill_shifted(ext_ref, head, body):
    nh = head.shape[0]
    for lc, ls in _lane_chunks():
        ext_ref[0, lc, :nh, :] = head[:, ls]
        ext_ref[0, lc, nh:, :] = body[:, ls]
        rows = ext_ref[0, lc]
        for b in range(1, SUBLANES):
            ext_ref[b, lc] = pltpu.roll(rows, rows.shape[0] - b, 0)


def _shifted_shape(tm):
    return (SUBLANES, CONV_LC, tm + HALO, LANES)


def _conv_chunk(c):
    return c % CONV_LC, pl.multiple_of((c // CONV_LC) * CONV_ROWS, CONV_ROWS)


def _shifted(ext_ref, o, lc, r0):
    a = pl.multiple_of((o // SUBLANES) * SUBLANES + r0, SUBLANES)
    return ext_ref[o % SUBLANES, lc, pl.ds(a, CONV_ROWS), :]


def _by_lane_chunk(a):
    return a.reshape(a.shape[0], CONV_LC, LANES).transpose(1, 0, 2)


def _load_resident(i, pairs):
    @pl.when(i == 0)
    def _():
        for src, dst in pairs:
            pltpu.sync_copy(src, dst)


def _place():
    return lax.axis_index("x"), lax.axis_index("y"), lax.axis_index("c")


def _all_gather8(blocks, name):
    na = len(blocks)

    def body(*refs):
        start, forward, finish = _gather8_phases(refs[:na], refs[na:2 * na], *refs[2 * na:])
        start()
        forward()
        finish()

    outs = pl.pallas_call(
        body, name=name, out_shape=_gather8_shapes(blocks), in_specs=[ANY] * na, out_specs=(ANY,) * na,
        scratch_shapes=_gather8_sems(na),
    )(*blocks)
    return _own_block_placed(outs, blocks)


def _gather8_shapes(blocks):
    return tuple(_sds((8,) + b.shape, b.dtype) for b in blocks)


def _gather8_sems(na):
    return [pltpu.SemaphoreType.DMA((7 * na,)), pltpu.SemaphoreType.DMA((7 * na,))]


def _own_block_placed(outs, blocks):
    ix, iy, ic = _place()
    return tuple(lax.dynamic_update_index_in_dim(o, b, 4 * ix + 2 * iy + ic, 0) for o, b in zip(outs, blocks))


def _gather8_phases(x_refs, out_refs, send_sems, recv_sems):
    na = len(x_refs)
    x, y, c = _place()
    me, sibling = (x, y, c), (x, y, 1 - c)
    chips = [(1 - x, y), (x, 1 - y), (1 - x, 1 - y)]

    def copy(a, k, blk, to, from_input=False):
        dst = out_refs[a].at[4 * blk[0] + 2 * blk[1] + blk[2]]
        return pltpu.make_async_remote_copy(
            src_ref=x_refs[a] if from_input else dst, dst_ref=dst,
            send_sem=send_sems.at[7 * a + k], recv_sem=recv_sems.at[7 * a + k], device_id=to, device_id_type=MESH)

    def first(a):
        return [copy(a, 0, me, sibling, True)] + [copy(a, 1 + j, me, (*chip, c), True) for j, chip in enumerate(chips)]

    def start():
        for a in range(na):
            for cp in first(a):
                cp.start()

    def forward():
        for j, chip in enumerate(chips):
            for a in range(na):
                copy(a, 1 + j, (*chip, c), me).wait_recv()
                copy(a, 4 + j, (*chip, c), sibling).start()

    def finish():
        for a in range(na):
            copy(a, 0, sibling, me).wait_recv()
            for j, chip in enumerate(chips):
                copy(a, 4 + j, (*chip, 1 - c), me).wait_recv()
        for a in range(na):
            for cp in first(a) + [copy(a, 4 + j, (*chip, c), sibling) for j, chip in enumerate(chips)]:
                cp.wait_send()

    return start, forward, finish


def _pair_swap(gs, name):
    na = len(gs)
    ns = gs[0].shape[0]

    def body(*refs):
        g_refs, land_refs = refs[:na], refs[na:2 * na]
        send_sems, recv_sems = refs[2 * na:]
        x, y, c = _place()
        cps = [pltpu.make_async_remote_copy(
            src_ref=g_refs[a].at[s, 1 - c], dst_ref=land_refs[a].at[s], send_sem=send_sems.at[ns * a + s],
            recv_sem=recv_sems.at[ns * a + s], device_id=(x, y, 1 - c), device_id_type=MESH)
            for a in range(na) for s in range(ns)]
        for cp in cps:
            cp.start()
        for cp in cps:
            cp.wait()

    return pl.pallas_call(
        body, name=name, out_shape=tuple(_sds((ns,) + g.shape[2:], g.dtype) for g in gs),
        in_specs=[ANY] * na, out_specs=(ANY,) * na,
        scratch_shapes=[pltpu.SemaphoreType.DMA((ns * na,)), pltpu.SemaphoreType.DMA((ns * na,))],
    )(*gs)


def _scatter_shapes(hs):
    return tuple(_sds((3,) + h.shape[1:], h.dtype) for h in hs)


def _scatter_sems(na):
    return [pltpu.SemaphoreType.DMA((3 * na,)), pltpu.SemaphoreType.DMA((3 * na,))]


def _scatter_phases(h_refs, land_refs, send_sems, recv_sems):
    x, y, c = _place()
    chips = [(1 - x, y), (x, 1 - y), (1 - x, 1 - y)]

    def copies():
        return [pltpu.make_async_remote_copy(
            src_ref=h_refs[a].at[2 * tx + ty], dst_ref=land_refs[a].at[j], send_sem=send_sems.at[3 * a + j],
            recv_sem=recv_sems.at[3 * a + j], device_id=(tx, ty, c), device_id_type=MESH)
            for a in range(len(h_refs)) for j, (tx, ty) in enumerate(chips)]

    def start():
        for cp in copies():
            cp.start()

    def finish():
        for cp in copies():
            cp.wait()

    return start, finish


def _pair_gather(fs, name):
    na = len(fs)

    def body(*refs):
        out_refs = refs[na:2 * na]
        send_sems, recv_sems = refs[2 * na:]
        x, y, c = _place()
        sends = [pltpu.make_async_remote_copy(
            src_ref=out_refs[a].at[c], dst_ref=out_refs[a].at[c], send_sem=send_sems.at[a], recv_sem=recv_sems.at[a],
            device_id=(x, y, 1 - c), device_id_type=MESH) for a in range(na)]
        recvs = [pltpu.make_async_remote_copy(
            src_ref=out_refs[a].at[c], dst_ref=out_refs[a].at[1 - c], send_sem=send_sems.at[a],
            recv_sem=recv_sems.at[a], device_id=(x, y, 1 - c), device_id_type=MESH) for a in range(na)]
        for cp in sends:
            cp.start()
        for cp in recvs:
            cp.wait_recv()
        for cp in sends:
            cp.wait_send()

    return pl.pallas_call(
        body, name=name, out_shape=tuple(_sds(f.shape, f.dtype) for f in fs),
        in_specs=[ANY] * na, out_specs=(ANY,) * na, input_output_aliases={a: a for a in range(na)},
        scratch_shapes=[pltpu.SemaphoreType.DMA((na,)), pltpu.SemaphoreType.DMA((na,))],
    )(*fs)


def _row_tile(r, n, itemsize=4, budget=1 << 20):
    if r * n * itemsize <= budget:
        return r
    best = None
    for tr in range(16, r, 16):
        if r % tr == 0 and tr * n * itemsize <= budget:
            best = tr
    assert best is not None, (r, n)
    return best


def _add_pair(g, land, cidx, name):
    ns, _, r, n = g.shape
    tr = _row_tile(r, n)

    def body(c_ref, a_ref, b_ref, o_ref, ob_ref):
        s = a_ref[...] + b_ref[...]
        o_ref[...] = s
        ob_ref[...] = s.astype(BF16)

    out = pl.BlockSpec((None, tr, n), lambda s, i, cr: (s, i, 0))
    return pl.pallas_call(
        body, name=name, out_shape=(_sds((ns, r, n), F32), _sds((ns, r, n), BF16)),
        grid_spec=pltpu.PrefetchScalarGridSpec(
            num_scalar_prefetch=1, grid=(ns, r // tr),
            in_specs=[pl.BlockSpec((None, None, tr, n), lambda s, i, cr: (s, cr[0], i, 0)), out],
            out_specs=(out, out)),
        compiler_params=_params("arbitrary", "arbitrary"),
    )(cidx, g, land)


def _add_chips(h, land, own_c, name):
    _, r, n = h.shape
    tr = _row_tile(r, n)

    def body(o_idx, h_ref, l_ref, o_ref):
        o_ref[...] = ((h_ref[...] + l_ref[0].astype(F32)) + l_ref[1].astype(F32)) + l_ref[2].astype(F32)

    return pl.pallas_call(
        body, name=name, out_shape=_sds((2, r, n), F32),
        grid_spec=pltpu.PrefetchScalarGridSpec(
            num_scalar_prefetch=1, grid=(r // tr,),
            in_specs=[pl.BlockSpec((None, tr, n), lambda i, o: (o[0], i, 0)),
                      pl.BlockSpec((3, tr, n), lambda i, o: (0, i, 0))],
            out_specs=pl.BlockSpec((None, tr, n), lambda i, o: (o[1], i, 0))),
        compiler_params=_params("arbitrary"),
    )(own_c, h, land)


def _adam_math(w, g, m, v):
    nm = ADAM_B1 * m + (1.0 - ADAM_B1) * g
    nv = ADAM_B2 * v + (1.0 - ADAM_B2) * (g * g)
    m_hat = nm / (1.0 - ADAM_B1 ** ADAM_STEP)
    v_hat = nv / (1.0 - ADAM_B2 ** ADAM_STEP)
    return -ADAM_LR * (m_hat / (jnp.sqrt(v_hat) + ADAM_EPS) + ADAM_WD * w), nm, nv


def _adamw(w, g, m, v, name):
    r, n = w.shape
    tr = _row_tile(r, n, budget=1 << 19)

    def body(w_ref, g_ref, m_ref, v_ref, d_ref, nm_ref, nv_ref):
        d_ref[...], nm_ref[...], nv_ref[...] = _adam_math(w_ref[...], g_ref[...], m_ref[...], v_ref[...])

    spec = pl.BlockSpec((tr, n), lambda i: (i, 0))
    return pl.pallas_call(
        body, name=name, out_shape=(_sds((r, n), F32),) * 3, grid=(r // tr,),
        in_specs=[spec] * 4, out_specs=(spec,) * 3, compiler_params=_params("arbitrary"),
    )(w, g, m, v)


def _adamw_small(ws, gs, ms, vs):
    k = len(ws)

    def body(*refs):
        ins, outs = refs[:4 * k], refs[4 * k:]
        for j in range(k):
            d, nm, nv = _adam_math(ins[j][...], ins[k + j][...], ins[2 * k + j][...], ins[3 * k + j][...])
            outs[j][...] = d
            outs[k + j][...] = nm
            outs[2 * k + j][...] = nv

    shapes = tuple(_sds(w.shape, F32) for w in ws)
    res = pl.pallas_call(body, name="adamw_small", out_shape=shapes * 3,
                         compiler_params=pltpu.CompilerParams(vmem_limit_bytes=VMEM_LIMIT))(*ws, *gs, *ms, *vs)
    return res[:k], res[k:2 * k], res[2 * k:]


def _ada_mod(c_all, w_sh, b_sh):
    b, _ = c_all.shape
    n = w_sh.shape[1]

    def body(c_ref, w_ref, b_ref, o_ref):
        cc = c_ref[...]
        ca = (cc * jax.nn.sigmoid(cc)).astype(BF16)
        o_ref[...] = _dot(ca, w_ref[...].astype(BF16)) + b_ref[...]

    return pl.pallas_call(body, name="ada_mod", out_shape=_sds((b, n), F32),
                          compiler_params=pltpu.CompilerParams(vmem_limit_bytes=VMEM_LIMIT))(c_all, w_sh, b_sh)


def _ada_bwd(c_all, dmod_all, dmod_sh, parts):
    b, d = c_all.shape
    n6 = dmod_all.shape[1]
    n = dmod_sh.shape[1]
    k = len(parts)

    def body(*refs):
        c_ref, da_ref, ds_ref = refs[:3]
        p_refs = refs[3:3 + k]
        dw_ref, db_ref = refs[3 + k:5 + k]
        s_refs = refs[5 + k:]
        cc = c_ref[...]
        ca = (cc * jax.nn.sigmoid(cc)).astype(BF16)
        dw_ref[...] = _dot_tn(ca, ds_ref[...].astype(BF16))
        db_ref[...] = jnp.sum(da_ref[...], axis=0, keepdims=True)
        for p_ref, s_ref in zip(p_refs, s_refs):
            tot = p_ref[0]
            for j in range(1, p_ref.shape[0]):
                tot = tot + p_ref[j]
            s_ref[...] = tot

    return pl.pallas_call(
        body, name="ada_bwd",
        out_shape=(_sds((d, n), F32), _sds((1, n6), F32)) + tuple(_sds(p.shape[1:], F32) for p in parts),
        compiler_params=pltpu.CompilerParams(vmem_limit_bytes=VMEM_LIMIT),
    )(c_all, dmod_all, dmod_sh, *parts)


def _fwd_in(x, g1, mod3, win_p, tm, tps):
    t, d = x.shape
    p_glu, p_q, npad = _layout(d)

    def body(x_ref, g_ref, mod_ref, w_hbm, h_ref, zm_ref, zglu_ref, zgate_ref, u0_ref, w_ref):
        _load_resident(pl.program_id(0), [(w_hbm, w_ref)])
        n, _ = _rms(x_ref[...])
        h = ((n * g_ref[...]) * (1.0 + mod_ref[1:2, :]) + mod_ref[0:1, :]).astype(BF16)
        h_ref[...] = h
        z = _dot(h, w_ref[...])
        zgate_ref[...] = z[:, :p_glu]
        zglu = z[:, p_glu:p_q]
        zglu_ref[...] = zglu
        zm_ref[...] = z[:, p_q:]
        u0_ref[...] = zglu[:, :CONV_CH] * jax.nn.sigmoid(zglu[:, CONV_CH:])

    return pl.pallas_call(
        body, name="fwd_in", grid=(t // tm,),
        out_shape=(_sds((t, d), BF16), _sds((t, MLA_IN), F32), _sds((t, 2 * CONV_CH), F32), _sds((t, 2 * d), F32),
                   _sds((t, CONV_CH), F32)),
        in_specs=[_row(tm, d), _full((1, d)), _modspec(d, tps), ANY],
        out_specs=(_row(tm, d), _row(tm, MLA_IN), _row(tm, 2 * CONV_CH), _row(tm, 2 * d), _row(tm, CONV_CH)),
        scratch_shapes=[pltpu.VMEM(win_p.shape, BF16)],
        compiler_params=_params("arbitrary"),
    )(x, g1, mod3, win_p)


def _mla_prep(zm, gql, gkvl, gq, gk, tabs, wuq_p, wk_p, wv_p, tm, tps):
    t = zm.shape[0]
    c_t, s1_t, s2_t = tabs
    tab = pl.BlockSpec((tm, LANES), lambda i: (i % tps, 0))

    def body(zm_ref, gql_ref, gkvl_ref, gq_ref, gk_ref, c_ref, s1_ref, s2_ref, wuq_ref, wk_ref, wv_ref,
             q_ref, k_ref, v_ref, qln_ref, kvn_ref):
        c, s1, s2 = c_ref[...], s1_ref[...], s2_ref[...]
        nq, _ = _rms(zm_ref[:, :Q_RANK])
        qln = (nq * gql_ref[...]).astype(BF16)
        qln_ref[...] = qln
        qpre = _dot(qln, wuq_ref[...])
        nkv, _ = _rms(zm_ref[:, Q_RANK:OFF_KV])
        kvn = (nkv * gkvl_ref[...]).astype(BF16)
        kvn_ref[...] = kvn
        knope = _dot(kvn, wk_ref[...])
        v_ref[...] = _dot(kvn, wv_ref[...]).astype(BF16)
        zkr_v = zm_ref[:, OFF_KV:]
        kr_roped = _rope(zkr_v * gk_ref[...], c, s1, s2)
        for hd in range(N_HEADS):
            sl = slice(hd * LANES, (hd + 1) * LANES)
            n, _ = _head_rms(qpre[:, sl])
            q_ref[:, sl] = _rope(n * gq_ref[...], c, s1, s2).astype(BF16)
            _, r = _head_rms(knope[:, sl] + zkr_v)
            k_ref[:, sl] = (r * (knope[:, sl] * gk_ref[...] + kr_roped)).astype(BF16)

    return pl.pallas_call(
        body, name="mla_prep", grid=(t // tm,),
        out_shape=(_sds((t, HW), BF16),) * 3 + (_sds((t, Q_RANK), BF16), _sds((t, KV_RANK), BF16)),
        in_specs=[_row(tm, MLA_IN), _full((1, Q_RANK)), _full((1, KV_RANK)),
                  _full((1, LANES)), _full((1, LANES)), tab, tab, tab,
                  _full(wuq_p.shape), _full(wk_p.shape), _full(wv_p.shape)],
        out_specs=(_row(tm, HW),) * 3 + (_row(tm, Q_RANK), _row(tm, KV_RANK)),
        compiler_params=_params("arbitrary"),
    )(zm, gql, gkvl, gq, gk, c_t, s1_t, s2_t, wuq_p, wk_p, wv_p)


SM_SCALE = QK_HEAD ** -0.5
EXP2_SCALE = SM_SCALE * 1.4426950408889634


def _diag_mask():
    rc = jnp.right_shift(lax.broadcasted_iota(jnp.int32, (BQ, 1), 0), CHUNK_SHIFT)
    cc = jnp.right_shift(lax.broadcasted_iota(jnp.int32, (1, BQ), 1), CHUNK_SHIFT)
    return rc >= cc


def _softmax_parts(q_i, k_ref, lo, e, mask):
    sd = jnp.where(mask, _dot_nt(q_i, k_ref[lo:e, :]), jnp.finfo(F32).min)
    m = jnp.max(sd, axis=-1, keepdims=True)
    if lo:
        sp = _dot_nt(q_i, k_ref[:lo, :])
        m = jnp.maximum(m, jnp.max(sp, axis=-1, keepdims=True))
    pd = jnp.exp2((sd - m) * EXP2_SCALE)
    l = jnp.sum(pd, axis=-1, keepdims=True)
    pp = None
    if lo:
        pp = jnp.exp2((sp - m) * EXP2_SCALE)
        l = l + jnp.sum(pp, axis=-1, keepdims=True)
    return pp, pd, l


def _attn_fwd(q, k, v, nseq, seq, gather=()):
    t = q.shape[0]
    na = len(gather)
    blk = pl.BlockSpec((seq, LANES), lambda b, h: (b, h))
    n_steps = nseq * N_HEADS

    def body(q_ref, k_ref, v_ref, *rest):
        o_ref = rest[na]
        if na:
            start, forward, finish = _gather8_phases(rest[:na], rest[na + 1:2 * na + 1], *rest[2 * na + 1:])
            step = pl.program_id(0) * N_HEADS + pl.program_id(1)
            pl.when(step == 0)(start)
            pl.when(step == (3 * n_steps) // 4)(forward)
        mask = _diag_mask()
        for i in range(seq // BQ):
            lo, e = i * BQ, (i + 1) * BQ
            pp, pd, l = _softmax_parts(q_ref[lo:e, :], k_ref, lo, e, mask)
            o = _dot(pd.astype(BF16), v_ref[lo:e, :])
            if lo:
                o = o + _dot(pp.astype(BF16), v_ref[:lo, :])
            o_ref[lo:e, :] = (o * (1.0 / l)).astype(BF16)
        if na:
            pl.when(step == n_steps - 1)(finish)

    res = pl.pallas_call(
        body, name="attn_fwd", grid=(nseq, N_HEADS), out_shape=(_sds((t, HW), BF16),) + _gather8_shapes(gather),
        in_specs=[blk, blk, blk] + [ANY] * na, out_specs=(blk,) + (ANY,) * na,
        scratch_shapes=_gather8_sems(na) if na else [],
        compiler_params=_params("arbitrary", "arbitrary"),
    )(q, k, v, *gather)
    return res[0], (_own_block_placed(res[1:], gather) if na else ())


def _fwd_mix(attn, u0, zgate, x, mod3, wo_p, cw, cb, lng, lnb, wpw, wout, tm, tps):
    t, d = x.shape
    hpt = tm // HALO
    cwc, cbc = _by_lane_chunk(cw), _by_lane_chunk(cb)

    def body(a_ref, u_ref, uh_ref, zg_ref, x_ref, mod_ref, wo_ref, cw_ref, cb_ref, lng_ref, lnb_ref, wpw_ref, wout_ref,
             x1_ref, mixed_ref, mpre_ref, ya_ref, yb_ref, u1_ref, u3_ref, ext_ref):
        i = pl.program_id(0)
        ya = _dot(a_ref[...], wo_ref[...])
        ya_ref[...] = ya
        first = (i % tps) == 0
        _fill_shifted(ext_ref, jnp.where(first, 0.0, uh_ref[...]), u_ref[...])
        for lc, ls in _lane_chunks():
            acc = jnp.broadcast_to(cb_ref[lc], (tm, LANES))
            for kk in range(CONV_W):
                o = HALO - (CONV_W - 1) + kk
                a = (o // SUBLANES) * SUBLANES
                acc = acc + cw_ref[lc, kk:kk + 1, :] * ext_ref[o % SUBLANES, lc, a:a + tm, :]
            u1_ref[:, ls] = acc
        acc = u1_ref[...]
        mu = jnp.mean(acc, axis=-1, keepdims=True)
        xc = acc - mu
        rstd = lax.rsqrt(jnp.mean(xc * xc, axis=-1, keepdims=True) + EPS)
        l = (xc * rstd) * lng_ref[...] + lnb_ref[...]
        u3 = (l * jax.nn.sigmoid(l)).astype(BF16)
        u3_ref[...] = u3
        yb = _dot(u3, wpw_ref[...])
        yb_ref[...] = yb
        zg = zg_ref[...]
        mpre = (jax.nn.sigmoid(zg[:, :d]) * ya + jax.nn.sigmoid(zg[:, d:]) * yb).astype(BF16)
        mpre_ref[...] = mpre
        mixed = _dot(mpre, wout_ref[...])
        mixed_ref[...] = mixed
        x1_ref[...] = x_ref[...] + mod_ref[2:3, :] * mixed

    halo = pl.BlockSpec((HALO, CONV_CH), lambda i: (jnp.maximum(i * hpt - 1, 0), 0))
    return pl.pallas_call(
        body, name="fwd_mix", grid=(t // tm,),
        out_shape=(_sds((t, d), F32), _sds((t, d), F32), _sds((t, d), BF16), _sds((t, d), F32), _sds((t, d), F32),
                   _sds((t, CONV_CH), F32), _sds((t, CONV_CH), BF16)),
        in_specs=[_row(tm, HW), _row(tm, CONV_CH), halo, _row(tm, 2 * d), _row(tm, d), _modspec(d, tps),
                  _full(wo_p.shape), _full(cwc.shape), _full(cbc.shape), _full((1, CONV_CH)), _full((1, CONV_CH)),
                  _full(wpw.shape), _full(wout.shape)],
        out_specs=(_row(tm, d), _row(tm, d), _row(tm, d), _row(tm, d), _row(tm, d), _row(tm, CONV_CH),
                   _row(tm, CONV_CH)),
        scratch_shapes=[pltpu.VMEM(_shifted_shape(tm), F32)],
        compiler_params=_params("arbitrary"),
    )(attn, u0, u0, zgate, x, mod3, wo_p, cwc, cbc, lng, lnb, wpw, wout)


def _fwd_ffn(x1, target, g2, mod3, w1, w2, tm, tps):
    t, d = x1.shape
    dff = w1.shape[1]

    def body(x1_ref, tg_ref, g_ref, mod_ref, w1_hbm, w2_hbm,
             h2_ref, a_ref, r_ref, dy_ref, df_ref, dgate_ref, loss_ref, w1_ref, w2_ref):
        i = pl.program_id(0)
        _load_resident(i, [(w1_hbm, w1_ref), (w2_hbm, w2_ref)])
        x1v = x1_ref[...]
        gate2 = mod_ref[5:6, :]
        n, _ = _rms(x1v)
        h2 = ((n * g_ref[...]) * (1.0 + mod_ref[4:5, :]) + mod_ref[3:4, :]).astype(BF16)
        h2_ref[...] = h2
        a = _dot(h2, w1_ref[...])
        a_ref[...] = a
        r = jnp.square(jnp.maximum(a, 0.0)).astype(BF16)
        r_ref[...] = r
        f = _dot(r, w2_ref[...])
        e = (x1v + gate2 * f) - tg_ref[...]
        part = 0.5 * jnp.sum(jnp.mean(e * e, axis=-1, keepdims=True), axis=0, keepdims=True)
        _acc(loss_ref, jnp.broadcast_to(part, loss_ref.shape), i == 0)
        dy = e * (1.0 / d)
        dy_ref[...] = dy
        df_ref[...] = (dy * gate2).astype(BF16)
        _acc(dgate_ref, jnp.sum(dy * f, axis=0, keepdims=True), (i % tps) == 0)

    nseq = t // (tm * tps)
    return pl.pallas_call(
        body, name="fwd_ffn", grid=(t // tm,),
        out_shape=(_sds((t, d), BF16), _sds((t, dff), F32), _sds((t, dff), BF16), _sds((t, d), F32), _sds((t, d), BF16),
                   _sds((nseq, 1, d), F32), _sds((8, LANES), F32)),
        in_specs=[_row(tm, d), _row(tm, d), _full((1, d)), _modspec(d, tps), ANY, ANY],
        out_specs=(_row(tm, d), _row(tm, dff), _row(tm, dff), _row(tm, d), _row(tm, d), _seqv(d, tps),
                   _full((8, LANES))),
        scratch_shapes=[pltpu.VMEM(w1.shape, BF16), pltpu.VMEM(w2.shape, BF16)],
        compiler_params=_params("arbitrary"),
    )(x1, target, g2, mod3, w1, w2)


def _bwd_ffn(df, a, x1, dy, mixed, g2, mod3, w2, w1, tm, tps):
    t, d = x1.shape
    dff = a.shape[1]

    def body(df_ref, a_ref, x1_ref, dy_ref, mx_ref, g_ref, mod_ref, w2_hbm, w1_hbm,
             da_ref, dx1_ref, dmixed_ref, dshift_ref, dscale_ref, dgate1_ref, dg2_ref, w2_ref, w1_ref):
        i = pl.program_id(0)
        _load_resident(i, [(w2_hbm, w2_ref), (w1_hbm, w1_ref)])
        first_seq = (i % tps) == 0
        dr = _dot_nt(df_ref[...], w2_ref[...])
        da = (dr * (2.0 * jnp.maximum(a_ref[...], 0.0))).astype(BF16)
        da_ref[...] = da
        dh2 = _dot_nt(da, w1_ref[...])
        n, r = _rms(x1_ref[...])
        g = g_ref[...]
        sc1 = 1.0 + mod_ref[4:5, :]
        _acc(dshift_ref, jnp.sum(dh2, axis=0, keepdims=True), first_seq)
        _acc(dscale_ref, jnp.sum(dh2 * (n * g), axis=0, keepdims=True), first_seq)
        _acc(dg2_ref, jnp.sum((dh2 * sc1) * n, axis=0, keepdims=True), i == 0)
        dx1 = dy_ref[...] + _rms_bwd(n, r, (dh2 * sc1) * g)
        dx1_ref[...] = dx1
        _acc(dgate1_ref, jnp.sum(dx1 * mx_ref[...], axis=0, keepdims=True), first_seq)
        dmixed_ref[...] = (dx1 * mod_ref[2:3, :]).astype(BF16)

    nseq = t // (tm * tps)
    sv = _sds((nseq, 1, d), F32)
    return pl.pallas_call(
        body, name="bwd_ffn", grid=(t // tm,),
        out_shape=(_sds((t, dff), BF16), _sds((t, d), F32), _sds((t, d), BF16), sv, sv, sv, _sds((1, d), F32)),
        in_specs=[_row(tm, d), _row(tm, dff), _row(tm, d), _row(tm, d), _row(tm, d), _full((1, d)), _modspec(d, tps),
                  ANY, ANY],
        out_specs=(_row(tm, dff), _row(tm, d), _row(tm, d), _seqv(d, tps), _seqv(d, tps), _seqv(d, tps),
                   _full((1, d))),
        scratch_shapes=[pltpu.VMEM(w2.shape, BF16), pltpu.VMEM(w1.shape, BF16)],
        compiler_params=_params("arbitrary"),
    )(df, a, x1, dy, mixed, g2, mod3, w2, w1)


def _bwd_mix(dmixed, zgate, ya, yb, u1, lng, lnb, wout, wo_p, wpw, tm):
    t, d = ya.shape
    _, _, npad = _layout(d)

    def body(dm_ref, zg_ref, ya_ref, yb_ref, u1_ref, lng_ref, lnb_ref, wout_ref, wo_ref, wpw_ref,
             dya_ref, dyb_ref, dz_ref, do_ref, du1_ref, dlng_ref, dlnb_ref, dcb_ref):
        i = pl.program_id(0)
        dmpre = _dot_nt(dm_ref[...], wout_ref[...])
        zg = zg_ref[...]
        ga = jax.nn.sigmoid(zg[:, :d])
        gb = jax.nn.sigmoid(zg[:, d:])
        dya = (dmpre * ga).astype(BF16)
        dyb = (dmpre * gb).astype(BF16)
        dya_ref[...] = dya
        dyb_ref[...] = dyb
        dz_ref[:, :d] = ((dmpre * ya_ref[...]) * (ga * (1.0 - ga))).astype(BF16)
        dz_ref[:, d:] = ((dmpre * yb_ref[...]) * (gb * (1.0 - gb))).astype(BF16)
        do_ref[...] = _dot_nt(dya, wo_ref[...]).astype(BF16)
        du3 = _dot_nt(dyb, wpw_ref[...])
        u1 = u1_ref[...]
        mu = jnp.mean(u1, axis=-1, keepdims=True)
        xc = u1 - mu
        rstd = lax.rsqrt(jnp.mean(xc * xc, axis=-1, keepdims=True) + EPS)
        nh = xc * rstd
        l = nh * lng_ref[...] + lnb_ref[...]
        sg = jax.nn.sigmoid(l)
        dl = du3 * (sg * (1.0 + l * (1.0 - sg)))
        _acc(dlng_ref, jnp.sum(dl * nh, axis=0, keepdims=True), i == 0)
        _acc(dlnb_ref, jnp.sum(dl, axis=0, keepdims=True), i == 0)
        dnh = dl * lng_ref[...]
        du1 = rstd * (dnh - jnp.mean(dnh, axis=-1, keepdims=True) - nh * jnp.mean(dnh * nh, axis=-1, keepdims=True))
        du1_ref[...] = du1
        _acc(dcb_ref, jnp.sum(du1, axis=0, keepdims=True), i == 0)

    cv = _sds((1, CONV_CH), F32)
    return pl.pallas_call(
        body, name="bwd_mix", grid=(t // tm,),
        out_shape=(_sds((t, d), BF16), _sds((t, d), BF16), _sds((t, npad), BF16), _sds((t, HW), BF16),
                   _sds((t, CONV_CH), F32), cv, cv, cv),
        in_specs=[_row(tm, d), _row(tm, 2 * d), _row(tm, d), _row(tm, d), _row(tm, CONV_CH), _full((1, CONV_CH)),
                  _full((1, CONV_CH)), _full(wout.shape), _full(wo_p.shape), _full(wpw.shape)],
        out_specs=(_row(tm, d), _row(tm, d), _row(tm, 2 * d), _row(tm, HW), _row(tm, CONV_CH),
                   _full((1, CONV_CH)), _full((1, CONV_CH)), _full((1, CONV_CH))),
        compiler_params=_params("arbitrary"),
    )(dmixed, zgate, ya, yb, u1, lng, lnb, wout, wo_p, wpw)


def _bwd_conv(dz, du1, u0, zglu, cw, tm, tps):
    t = du1.shape[0]
    d = (dz.shape[1] - MLA_IN - 2 * CONV_CH) // 2
    p_glu, _, _ = _layout(d)
    hpt = tm // HALO
    last_blk = t // HALO - 1
    cwc = _by_lane_chunk(cw)

    def body(dz_hbm, du_ref, dun_ref, u_ref, uh_ref, zl_ref, cw_ref, dzl_ref, dcw_ref, ext_ref, dext_ref, dcw8_ref,
             du0_ref):
        i = pl.program_id(0)
        first = (i % tps) == 0
        last = (i % tps) == (tps - 1)
        _fill_shifted(ext_ref, jnp.where(first, 0.0, uh_ref[...]), u_ref[...])
        _fill_shifted(dext_ref, du_ref[...], jnp.where(last, 0.0, dun_ref[...]))

        @pl.when(i == 0)
        def _():
            dcw8_ref[...] = jnp.zeros_like(dcw8_ref)

        groups = CONV_ROWS // SUBLANES

        def conv_chunk(c, carry):
            lc, r0 = _conv_chunk(c)
            du = _shifted(dext_ref, 0, lc, r0)
            du0 = jnp.zeros((CONV_ROWS, LANES), F32)
            for kk in range(CONV_W):
                prod = du * _shifted(ext_ref, HALO - (CONV_W - 1) + kk, lc, r0)
                part = prod[:SUBLANES]
                for g in range(1, groups):
                    part = part + prod[g * SUBLANES:(g + 1) * SUBLANES]
                dcw8_ref[lc, kk] += part
                du0 = du0 + cw_ref[lc, kk:kk + 1, :] * _shifted(dext_ref, CONV_W - 1 - kk, lc, r0)
            du0_ref[lc, pl.ds(r0, CONV_ROWS), :] = du0
            return carry

        lax.fori_loop(0, CONV_LC * (tm // CONV_ROWS), conv_chunk, 0)

        @pl.when(i == pl.num_programs(0) - 1)
        def _():
            for lc, ls in _lane_chunks():
                dcw_ref[:, ls] = jnp.sum(dcw8_ref[lc], axis=1)

        for lc, ls in _lane_chunks():
            du0 = du0_ref[lc]
            ga = zl_ref[:, ls]
            sb = jax.nn.sigmoid(zl_ref[:, CONV_CH + lc * LANES:CONV_CH + (lc + 1) * LANES])
            dzl_ref[:, ls] = (du0 * sb).astype(BF16)
            dzl_ref[:, CONV_CH + lc * LANES:CONV_CH + (lc + 1) * LANES] = ((du0 * ga) * (sb * (1.0 - sb))).astype(BF16)

    prev = pl.BlockSpec((HALO, CONV_CH), lambda i: (jnp.maximum(i * hpt - 1, 0), 0))
    nxt = pl.BlockSpec((HALO, CONV_CH), lambda i: (jnp.minimum((i + 1) * hpt, last_blk), 0))
    glu_blk = p_glu // (2 * CONV_CH)
    return pl.pallas_call(
        body, name="bwd_conv", grid=(t // tm,),
        out_shape=(_sds(dz.shape, BF16), _sds(cw.shape, F32)),
        in_specs=[ANY, _row(tm, CONV_CH), nxt, _row(tm, CONV_CH), prev, _row(tm, 2 * CONV_CH), _full(cwc.shape)],
        out_specs=(pl.BlockSpec((tm, 2 * CONV_CH), lambda i: (i, glu_blk)), _full(cw.shape)),
        scratch_shapes=[pltpu.VMEM(_shifted_shape(tm), F32)] * 2
        + [pltpu.VMEM((CONV_LC, HALO, SUBLANES, LANES), F32), pltpu.VMEM((CONV_LC, tm, LANES), F32)],
        input_output_aliases={0: 0},
        compiler_params=_params("arbitrary"),
    )(dz, du1, du1, u0, u0, zglu, cwc)


def _attn_bwd(q, k, v, do, nseq, seq, scatter=()):
    t = q.shape[0]
    ns = len(scatter)
    blk = pl.BlockSpec((seq, LANES), lambda b, h: (b, h))
    n_steps = nseq * N_HEADS

    def body(q_ref, k_ref, v_ref, do_ref, *rest):
        dq_ref, dk_ref, dv_ref = rest[ns:ns + 3]
        dka_ref, dva_ref = rest[2 * ns + 3:2 * ns + 5]
        if ns:
            start, finish = _scatter_phases(rest[:ns], rest[ns + 3:2 * ns + 3], *rest[2 * ns + 5:])
            step = pl.program_id(0) * N_HEADS + pl.program_id(1)
            pl.when(step == 0)(start)
        dka_ref[...] = jnp.zeros_like(dka_ref)
        dva_ref[...] = jnp.zeros_like(dva_ref)
        mask = _diag_mask()
        for i in range(seq // BQ):
            lo, e = i * BQ, (i + 1) * BQ
            q_i = q_ref[lo:e, :]
            do_i = do_ref[lo:e, :]
            pp, pd, l = _softmax_parts(q_i, k_ref, lo, e, mask)
            inv = 1.0 / l
            pd = pd * inv
            dpd = _dot_nt(do_i, v_ref[lo:e, :])
            delta = jnp.sum(pd * dpd, axis=-1, keepdims=True)
            if lo:
                pp = pp * inv
                dpp = _dot_nt(do_i, v_ref[:lo, :])
                delta = delta + jnp.sum(pp * dpp, axis=-1, keepdims=True)
            dsd = (pd * (dpd - delta)).astype(BF16)
            dq = _dot(dsd, k_ref[lo:e, :])
            dka_ref[lo:e, :] += _dot_tn(dsd, q_i)
            dva_ref[lo:e, :] += _dot_tn(pd.astype(BF16), do_i)
            if lo:
                dsp = (pp * (dpp - delta)).astype(BF16)
                dq = dq + _dot(dsp, k_ref[:lo, :])
                dka_ref[:lo, :] += _dot_tn(dsp, q_i)
                dva_ref[:lo, :] += _dot_tn(pp.astype(BF16), do_i)
            dq_ref[lo:e, :] = dq * SM_SCALE
        dk_ref[...] = dka_ref[...] * SM_SCALE
        dv_ref[...] = dva_ref[...].astype(BF16)
        if ns:
            pl.when(step == n_steps - 1)(finish)

    res = pl.pallas_call(
        body, name="attn_bwd", grid=(nseq, N_HEADS),
        out_shape=(_sds((t, HW), F32), _sds((t, HW), F32), _sds((t, HW), BF16)) + _scatter_shapes(scatter),
        in_specs=[blk] * 4 + [ANY] * ns, out_specs=(blk,) * 3 + (ANY,) * ns,
        scratch_shapes=[pltpu.VMEM((seq, LANES), F32), pltpu.VMEM((seq, LANES), F32)]
        + (_scatter_sems(ns) if ns else []),
        compiler_params=_params("arbitrary", "arbitrary"),
    )(q, k, v, do, *scatter)
    return res[0], res[1], res[2], res[3:]


def _mla_bwd(dz, dq, dk, dv, zm, gql, gkvl, gq, gk, tabs, wuq_p, wk_p, wv_p, tm, tps):
    t = zm.shape[0]
    d = (dz.shape[1] - MLA_IN - 2 * CONV_CH) // 2
    _, p_q, _ = _layout(d)
    c_t, s1_t, s2_t = tabs
    tab = pl.BlockSpec((tm, LANES), lambda i: (i % tps, 0))

    def body(dz_hbm, dq_ref, dk_ref, dv_ref, zm_ref, gql_ref, gkvl_ref, gq_ref, gk_ref, c_ref, s1_ref, s2_ref,
             wuq_ref, wk_ref, wv_ref,
             dzm_ref, dqpre_ref, dkh_ref, dgq_ref, dgk_ref, dgql_ref, dgkvl_ref):
        i = pl.program_id(0)
        c, s1, s2 = c_ref[...], s1_ref[...], s2_ref[...]
        nq, rq = _rms(zm_ref[:, :Q_RANK])
        qpre = _dot((nq * gql_ref[...]).astype(BF16), wuq_ref[...])
        nkv, rkv = _rms(zm_ref[:, Q_RANK:OFF_KV])
        knope = _dot((nkv * gkvl_ref[...]).astype(BF16), wk_ref[...])
        zkr_v = zm_ref[:, OFF_KV:]
        dgq = jnp.zeros((1, LANES), F32)
        dgk = jnp.zeros((1, LANES), F32)
        dzkr = jnp.zeros((tm, LANES), F32)
        for hd in range(N_HEADS):
            sl = slice(hd * LANES, (hd + 1) * LANES)
            n, r = _head_rms(qpre[:, sl])
            dyr = _rope_t(dq_ref[:, sl], c, s1, s2)
            dgq = dgq + jnp.sum(dyr * n, axis=0, keepdims=True)
            dqpre_ref[:, sl] = _head_rms_bwd(n, r, dyr * gq_ref[...]).astype(BF16)
            n, r = _head_rms(knope[:, sl] + zkr_v)
            dyr = _rope_t(dk_ref[:, sl], c, s1, s2)
            dgk = dgk + jnp.sum(dyr * n, axis=0, keepdims=True)
            dkh = _head_rms_bwd(n, r, dyr * gk_ref[...])
            dzkr = dzkr + dkh
            dkh_ref[:, sl] = dkh.astype(BF16)
        _acc(dgq_ref, dgq[:, :QK_HEAD], i == 0)
        _acc(dgk_ref, dgk[:, :QK_HEAD], i == 0)
        dzm_ref[:, OFF_KV:] = dzkr.astype(BF16)
        dqln = _dot_nt(dqpre_ref[...], wuq_ref[...])
        _acc(dgql_ref, jnp.sum(dqln * nq, axis=0, keepdims=True), i == 0)
        dzm_ref[:, :Q_RANK] = _rms_bwd(nq, rq, dqln * gql_ref[...]).astype(BF16)
        dkvn = _dot_nt(dkh_ref[...], wk_ref[...]) + _dot_nt(dv_ref[...], wv_ref[...])
        _acc(dgkvl_ref, jnp.sum(dkvn * nkv, axis=0, keepdims=True), i == 0)
        dzm_ref[:, Q_RANK:OFF_KV] = _rms_bwd(nkv, rkv, dkvn * gkvl_ref[...]).astype(BF16)

    return pl.pallas_call(
        body, name="mla_bwd", grid=(t // tm,),
        out_shape=(_sds(dz.shape, BF16), _sds((t, HW), BF16), _sds((t, HW), BF16), _sds((1, QK_HEAD), F32),
                   _sds((1, QK_HEAD), F32), _sds((1, Q_RANK), F32), _sds((1, KV_RANK), F32)),
        in_specs=[ANY, _row(tm, HW), _row(tm, HW), _row(tm, HW), _row(tm, MLA_IN),
                  _full((1, Q_RANK)), _full((1, KV_RANK)), _full((1, LANES)), _full((1, LANES)), tab, tab, tab,
                  _full(wuq_p.shape), _full(wk_p.shape), _full(wv_p.shape)],
        out_specs=(pl.BlockSpec((tm, MLA_IN), lambda i: (i, p_q // MLA_IN)), _row(tm, HW), _row(tm, HW),
                   _full((1, QK_HEAD)), _full((1, QK_HEAD)), _full((1, Q_RANK)), _full((1, KV_RANK))),
        input_output_aliases={0: 0},
        compiler_params=_params("arbitrary"),
    )(dz, dq, dk, dv, zm, gql, gkvl, gq, gk, c_t, s1_t, s2_t, wuq_p, wk_p, wv_p)


def _bwd_in(dz, x, dx1, g1, mod3, win_p, tm, tps, scatter=()):
    t, d = x.shape
    npad = dz.shape[1]

    ns = len(scatter)
    n_steps = t // tm

    def body(dz_ref, x_ref, dx1_ref, g_ref, mod_ref, wt_hbm, *rest):
        gx_ref, dshift_ref, dscale_ref, dg1_ref = rest[ns:ns + 4]
        wt_ref = rest[2 * ns + 4]
        i = pl.program_id(0)
        if ns:
            start, finish = _scatter_phases(rest[:ns], rest[ns + 4:2 * ns + 4], *rest[2 * ns + 5:])
            pl.when(i == 0)(start)
        _load_resident(i, [(wt_hbm, wt_ref)])
        first_seq = (i % tps) == 0
        dh = _dot_nt(dz_ref[...], wt_ref[...])
        n, r = _rms(x_ref[...])
        g = g_ref[...]
        sc1 = 1.0 + mod_ref[1:2, :]
        _acc(dshift_ref, jnp.sum(dh, axis=0, keepdims=True), first_seq)
        _acc(dscale_ref, jnp.sum(dh * (n * g), axis=0, keepdims=True), first_seq)
        _acc(dg1_ref, jnp.sum((dh * sc1) * n, axis=0, keepdims=True), i == 0)
        gx_ref[...] = dx1_ref[...] + _rms_bwd(n, r, (dh * sc1) * g)
        if ns:
            pl.when(i == n_steps - 1)(finish)

    nseq = t // (tm * tps)
    sv = _sds((nseq, 1, d), F32)
    res = pl.pallas_call(
        body, name="bwd_in", grid=(n_steps,),
        out_shape=(_sds((t, d), F32), sv, sv, _sds((1, d), F32)) + _scatter_shapes(scatter),
        in_specs=[_row(tm, npad), _row(tm, d), _row(tm, d), _full((1, d)), _modspec(d, tps), ANY] + [ANY] * ns,
        out_specs=(_row(tm, d), _seqv(d, tps), _seqv(d, tps), _full((1, d))) + (ANY,) * ns,
        scratch_shapes=[pltpu.VMEM(win_p.shape, BF16)] + (_scatter_sems(ns) if ns else []),
        compiler_params=_params("arbitrary"),
    )(dz, x, dx1, g1, mod3, win_p, *scatter)
    return res[0], res[1], res[2], res[3], res[4:]


def _tile_of(n, choices):
    for c in choices:
        if n % c == 0:
            return c
    return n


def _tn_matmul(a, b, name, col_shards=0):
    t, k = a.shape
    n = b.shape[1]
    tk = _tile_of(k, (1024, 512, 256, 128))
    tn = n // col_shards if col_shards else _tile_of(n, (1024, 896, 768, 512, 384, 256, 128))
    tt = _tile_of(t, (1024, 512, 256))

    def body(a_ref, b_ref, o_ref):
        _acc(o_ref, _dot_tn(a_ref[...], b_ref[...]), pl.program_id(2) == 0)

    if col_shards:
        out_shape, out_spec = _sds((col_shards, k, tn), F32), pl.BlockSpec((None, tk, tn), lambda i, j, s: (j, i, 0))
    else:
        out_shape, out_spec = _sds((k, n), F32), pl.BlockSpec((tk, tn), lambda i, j, s: (i, j))
    return pl.pallas_call(
        body, name=name, grid=(k // tk, n // tn, t // tt), out_shape=out_shape,
        in_specs=[pl.BlockSpec((tt, tk), lambda i, j, s: (s, i)), pl.BlockSpec((tt, tn), lambda i, j, s: (s, j))],
        out_specs=out_spec, compiler_params=_params("arbitrary", "arbitrary", "arbitrary"),
    )(a, b)


N_SHARD = 4
COL_SHARDED = ("w_in", "w_uq", "w_ukv", "w_o_mla", "w_pw_out", "w_ff1")
ROW_SHARDED = ("w_out", "w_ff2")
BIG = ("w_in", "w_uq", "w_ukv", "w_o_mla", "w_pw_out", "w_out", "w_ff1", "w_ff2")
SMALL = ("norm1_g", "q_latent_g", "kv_latent_g", "qk_norm_q_g", "qk_norm_k_g", "conv_b", "conv_ln_g", "conv_ln_b",
         "norm2_g")
WEIGHTS = ("w_ada", "b_ada", "norm1_g", "w_in", "q_latent_g", "w_uq", "kv_latent_g", "w_ukv", "qk_norm_q_g",
           "qk_norm_k_g", "w_o_mla", "conv_w", "conv_b", "conv_ln_g", "conv_ln_b", "w_pw_out", "w_out", "norm2_g",
           "w_ff1", "w_ff2")


def _pad_heads(w, width):
    k = w.shape[0]
    w3 = w.reshape(k, N_HEADS, width)
    return jnp.pad(w3, ((0, 0), (0, 0), (0, LANES - width))).reshape(k, HW)


def _unpad_heads(g, width):
    k = g.shape[0]
    return g.reshape(k, N_HEADS, LANES)[:, :, :width].reshape(k, N_HEADS * width)


def _pad_win(w):
    d = w.shape[0]
    z = lambda n: jnp.zeros((d, n), w.dtype)
    return jnp.concatenate([w[:, OFF_GLU:], w[:, OFF_KR:OFF_GLU], w[:, :OFF_KV], z(KR_LANE), w[:, OFF_KV:OFF_KR],
                            z(LANES - KR_LANE - QK_ROPE)], axis=1)


def _unpad_win(g):
    d = g.shape[0]
    p_glu, p_q, _ = _layout(d)
    kr = p_q + OFF_KV + KR_LANE
    return jnp.concatenate([g[:, p_q:p_q + OFF_KV], g[:, kr:kr + QK_ROPE], g[:, p_glu:p_q], g[:, :p_glu]], axis=1)


def _col_shards(g):
    k, n = g.shape
    return g.reshape(k, N_SHARD, n // N_SHARD).transpose(1, 0, 2)


def _from_shards(g, name):
    ns, ks, nn = g.shape
    if name in ROW_SHARDED:
        return g.reshape(ns * ks, nn)
    return g.transpose(1, 0, 2).reshape(ks, ns * nn)


EARLY = ("w_in", "w_uq", "w_ukv")
LATE = ("w_o_mla", "w_pw_out", "w_out", "w_ff1", "w_ff2")


def _assemble(names, gathered):
    return {n: _from_shards(g.reshape((N_SHARD, 2 * g.shape[1]) + g.shape[2:]), n) for n, g in zip(names, gathered)}


def _pair_reduce(names, gws, tag):
    g2 = [g.reshape(N_SHARD, 2, g.shape[1] // 2, g.shape[2]) for g in gws]
    from_sibling = _pair_swap(g2, "grad_pair_swap_" + tag)
    cidx = lax.axis_index("c").reshape(1).astype(jnp.int32)
    return [_add_pair(g, l, cidx, "pair_sum_" + n) for n, g, l in zip(names, g2, from_sibling)]


def _local_step(x, target, mod, sp, w, late=None, tm=256):
    comm = late is not None
    w = dict(w)
    nseq, seq, d = x.shape
    t = nseq * seq
    tps = seq // tm
    xf = x.reshape(t, d)
    tg = target.reshape(t, d)
    mod3 = mod.reshape(nseq, N_MOD, d)

    win_p = _pad_win(w["w_in"])
    wuq_p = _pad_heads(w["w_uq"], QK_HEAD)
    wkv3 = w["w_ukv"].reshape(KV_RANK, N_HEADS, QK_NOPE + V_HEAD)
    wk_p = _pad_heads(wkv3[:, :, :QK_NOPE].reshape(KV_RANK, -1), QK_NOPE)
    wv_p = _pad_heads(wkv3[:, :, QK_NOPE:].reshape(KV_RANK, -1), V_HEAD)
    cw = jnp.pad(w["conv_w"], ((0, HALO - CONV_W), (0, 0)))
    pad_g = lambda g: jnp.pad(g, ((0, 0), (0, LANES - QK_HEAD)))
    gq, gk = pad_g(sp["qk_norm_q_g"]), pad_g(sp["qk_norm_k_g"])
    tabs = _rope_tables(seq)

    h, zm, zglu, zgate, u0 = _fwd_in(xf, sp["norm1_g"], mod3, win_p, tm, tps)
    q, k, v, qln, kvn = _mla_prep(zm, sp["q_latent_g"], sp["kv_latent_g"], gq, gk, tabs, wuq_p, wk_p, wv_p, tm, tps)
    attn, gathered = _attn_fwd(q, k, v, nseq, seq, tuple(late) if comm else ())
    if comm:
        w.update(_assemble(LATE, gathered))
    wo_p = jnp.pad(w["w_o_mla"].reshape(N_HEADS, V_HEAD, d), ((0, 0), (0, LANES - V_HEAD), (0, 0))).reshape(HW, d)
    x1, mixed, mpre, ya, yb, u1, u3 = _fwd_mix(attn, u0, zgate, xf, mod3, wo_p, cw, sp["conv_b"], sp["conv_ln_g"],
                                               sp["conv_ln_b"], w["w_pw_out"], w["w_out"], tm, tps)
    h2, a, r, dy, df, dgate2, loss_acc = _fwd_ffn(x1, tg, sp["norm2_g"], mod3, w["w_ff1"], w["w_ff2"], tm, tps)
    da, dx1, dmixed, dshift2, dscale2, dgate1, dg2 = _bwd_ffn(df, a, x1, dy, mixed, sp["norm2_g"], mod3,
                                                              w["w_ff2"], w["w_ff1"], tm, tps)
    dya, dyb, dz, do, du1, dlng, dlnb, dcb = _bwd_mix(dmixed, zgate, ya, yb, u1, sp["conv_ln_g"], sp["conv_ln_b"],
                                                      w["w_out"], wo_p, w["w_pw_out"], tm)
    dz, dcw = _bwd_conv(dz, du1, u0, zglu, cw, tm, tps)
    dwo = _tn_matmul(attn, dya, "dw_o").reshape(N_HEADS, LANES, d)[:, :V_HEAD].reshape(MLA_WIDTH, d)
    gw = {
        "conv_w": dcw,
        "w_o_mla": _col_shards(dwo),
        "w_pw_out": _tn_matmul(u3, dyb, "dw_pw", N_SHARD),
        "w_out": _tn_matmul(mpre, dmixed, "dw_out").reshape(N_SHARD, d // N_SHARD, d),
        "w_ff1": _tn_matmul(h2, da, "dw_ff1", N_SHARD),
        "w_ff2": _tn_matmul(r, df, "dw_ff2").reshape(N_SHARD, -1, d),
    }
    pair_late = _pair_reduce(LATE, [gw[n] for n in LATE], "late") if comm else []
    dq, dk, dv, land_late = _attn_bwd(q, k, v, do, nseq, seq, tuple(p[1] for p in pair_late))
    dz, dqpre, dkh, dgq, dgk, dgql, dgkvl = _mla_bwd(dz, dq, dk, dv, zm, sp["q_latent_g"], sp["kv_latent_g"], gq, gk,
                                                      tabs, wuq_p, wk_p, wv_p, tm, tps)
    dwk_p = _tn_matmul(kvn, dkh, "dw_uk")
    dwv_p = _tn_matmul(kvn, dv, "dw_uv")
    dwkv = jnp.concatenate([dwk_p.reshape(KV_RANK, N_HEADS, LANES)[:, :, :QK_NOPE],
                            dwv_p.reshape(KV_RANK, N_HEADS, LANES)[:, :, :V_HEAD]], axis=2).reshape(KV_RANK, -1)
    gw["w_in"] = _col_shards(_unpad_win(_tn_matmul(h, dz, "dw_in")))
    gw["w_uq"] = _col_shards(_unpad_heads(_tn_matmul(qln, dqpre, "dw_uq"), QK_HEAD))
    gw["w_ukv"] = _col_shards(dwkv)
    pair_early = _pair_reduce(EARLY, [gw[n] for n in EARLY], "early") if comm else []
    gx, dshift1, dscale1, dg1, land_early = _bwd_in(dz, xf, dx1, sp["norm1_g"], mod3, win_p, tm, tps,
                                                    tuple(p[1] for p in pair_early))
    if comm:
        for n, p, l in zip(LATE + EARLY, pair_late + pair_early, land_late + land_early):
            gw[n] = (p[0], l)
    gs = {
        "norm1_g": dg1, "q_latent_g": dgql, "kv_latent_g": dgkvl, "qk_norm_q_g": dgq, "qk_norm_k_g": dgk,
        "conv_b": dcb, "conv_ln_g": dlng, "conv_ln_b": dlnb, "norm2_g": dg2,
    }
    dmod = jnp.concatenate([dshift1, dscale1, dgate1, dshift2, dscale2, dgate2], axis=2).reshape(nseq, N_MOD * d)
    return loss_acc, gx.reshape(nseq, seq, d), dmod, gw, gs


def kernel(x, c, w_ada, b_ada, norm1_g, w_in, q_latent_g, w_uq, kv_latent_g, w_ukv, qk_norm_q_g, qk_norm_k_g, w_o_mla, conv_w, conv_b, conv_ln_g, conv_ln_b, w_pw_out, w_out, norm2_g, w_ff1, w_ff2, loss_target, m_w_ada, m_b_ada, m_norm1_g, m_w_in, m_q_latent_g, m_w_uq, m_kv_latent_g, m_w_ukv, m_qk_norm_q_g, m_qk_norm_k_g, m_w_o_mla, m_conv_w, m_conv_b, m_conv_ln_g, m_conv_ln_b, m_w_pw_out, m_w_out, m_norm2_g, m_w_ff1, m_w_ff2, v_w_ada, v_b_ada, v_norm1_g, v_w_in, v_q_latent_g, v_w_uq, v_kv_latent_g, v_w_ukv, v_qk_norm_q_g, v_qk_norm_k_g, v_w_o_mla, v_conv_w, v_conv_b, v_conv_ln_g, v_conv_ln_b, v_w_pw_out, v_w_out, v_norm2_g, v_w_ff1, v_w_ff2):
    given = dict(locals())
    wts = {n: given[n][0] for n in WEIGHTS}
    mom = {n: given["m_" + n][0] for n in WEIGHTS}
    var = {n: given["v_" + n][0] for n in WEIGHTS}
    vec = lambda a: a.reshape(1, -1)
    nseq, seq, d = x.shape
    ix, iy, ic = _place()
    shard = 2 * ix + iy

    half = lambda n: lax.dynamic_slice_in_dim(wts[n].astype(BF16), ic * (wts[n].shape[0] // 2), wts[n].shape[0] // 2,
                                              axis=0)
    gathered = _all_gather8([half(n) for n in EARLY] + [wts["conv_w"], c], "gather_weights")
    full = _assemble(EARLY, gathered)
    full["conv_w"] = _from_shards(gathered[-2][0::2], "conv_w")
    c_all = gathered[-1].reshape(8 * nseq, d)

    n_ada = wts["w_ada"].shape[1]
    b_sh = lax.dynamic_slice_in_dim(vec(wts["b_ada"]), shard * n_ada, n_ada, axis=1)
    mod_sh = _ada_mod(c_all, wts["w_ada"], b_sh)
    hb = 4 * nseq
    mod_blk = lax.dynamic_slice_in_dim(mod_sh, ic * hb, hb, axis=0)
    (mod_all,) = _all_gather8([mod_blk], "gather_mod")
    mod_mine = lax.dynamic_slice_in_dim(mod_all, (2 * iy + ic) * nseq, nseq, axis=1)
    mod = jnp.concatenate([lax.dynamic_index_in_dim(mod_mine, 2 * s + ix, axis=0, keepdims=False)
                           for s in range(N_SHARD)], axis=1)

    sp = {n: vec(wts[n]) for n in SMALL}
    loss_part, grad_x, dmod, gw, gs = _local_step(x, loss_target, mod, sp, full, [half(n) for n in LATE])

    parts = _all_gather8([dmod, gw["conv_w"], loss_part] + [gs[n] for n in SMALL], "gather_small")
    dmod_all = parts[0].reshape(8 * nseq, N_MOD * d)
    dmod_sh = lax.dynamic_slice_in_dim(dmod_all, shard * n_ada, n_ada, axis=1)
    res = _ada_bwd(c_all, dmod_all, dmod_sh, parts[1:])
    grads = {"w_ada": res[0], "b_ada": res[1]}
    n_cw = wts["conv_w"].shape[1]
    grads["conv_w"] = lax.dynamic_slice_in_dim(res[2], shard * n_cw, n_cw, axis=1)[:CONV_W]
    loss = res[3][0, 0]
    for n, g in zip(SMALL, res[4:]):
        grads[n] = g

    own_c = jnp.stack([shard, ic]).astype(jnp.int32)
    mine_sum = [_add_chips(gw[n][0], gw[n][1], own_c, "chip_sum_" + n) for n in BIG]
    for n, g in zip(BIG, _pair_gather(mine_sum, "grad_pair_gather")):
        grads[n] = g.reshape(wts[n].shape)

    delta, new_m, new_v = {}, {}, {}
    for n in BIG + ("w_ada",):
        delta[n], new_m[n], new_v[n] = _adamw(wts[n], grads[n], mom[n], var[n], "adamw_" + n)
    rest = ("b_ada", "conv_w") + SMALL
    as2d = lambda a: a if a.ndim == 2 else vec(a)
    res = _adamw_small(*[[as2d(t[n]) for n in rest] for t in (wts, grads, mom, var)])
    for dst, arrs in zip((delta, new_m, new_v), res):
        for n, a in zip(rest, arrs):
            dst[n] = a

    outs = [loss, grad_x]
    for group in (grads, delta, new_m, new_v):
        outs += [group[n].reshape(given[n].shape) for n in WEIGHTS]
    return tuple(outs)
```

```python
import jax
import jax.numpy as jnp
from jax import lax
from jax.experimental import pallas as pl
from jax.experimental.pallas import tpu as pltpu

F32 = jnp.float32
BF16 = jnp.bfloat16
MESH = pl.DeviceIdType.MESH
ANY = pl.BlockSpec(memory_space=pl.ANY)

CHUNK = 64
CHUNK_SHIFT = 6
N_HEADS = 8
QK_NOPE = 64
QK_ROPE = 32
QK_HEAD = QK_NOPE + QK_ROPE
V_HEAD = 64
Q_RANK = 256
KV_RANK = 128
MLA_WIDTH = N_HEADS * V_HEAD
CONV_CH = 512
CONV_W = 31
ROPE_THETA = 10000.0
EPS = 1e-6
LANES = 128
SUBLANES = 8
HW = N_HEADS * LANES
OFF_KV = Q_RANK + KV_RANK
OFF_KR = OFF_KV + QK_ROPE
OFF_GLU = OFF_KR + 2 * CONV_CH
KR_LANE = QK_NOPE
MLA_IN = Q_RANK + KV_RANK + LANES
HALO = 32
N_MOD = 6

ADAM_LR = 0.001
ADAM_B1 = 0.9
ADAM_B2 = 0.999
ADAM_EPS = 1e-08
ADAM_WD = 0.01
ADAM_STEP = 10

VMEM_LIMIT = 56 * 1024 * 1024
BQ = 256


def _layout(d):
    p_glu = 2 * d
    p_q = p_glu + 2 * CONV_CH
    return p_glu, p_q, p_q + MLA_IN


def _params(*sem):
    return pltpu.CompilerParams(dimension_semantics=sem, vmem_limit_bytes=VMEM_LIMIT)


def _dot(a, b):
    return jnp.dot(a, b, preferred_element_type=F32)


def _dot_tn(a, b):
    return lax.dot_general(a, b, (((0,), (0,)), ((), ())), preferred_element_type=F32)


def _dot_nt(a, b):
    return lax.dot_general(a, b, (((1,), (1,)), ((), ())), preferred_element_type=F32)


def _acc(ref, val, first):
    @pl.when(first)
    def _():
        ref[...] = val

    @pl.when(jnp.logical_not(first))
    def _():
        ref[...] += val


def _rms(x):
    r = lax.rsqrt(jnp.mean(x * x, axis=-1, keepdims=True) + EPS)
    return x * r, r


def _rms_bwd(n, r, dn):
    return r * (dn - n * jnp.mean(dn * n, axis=-1, keepdims=True))


def _head_rms(sl):
    r = lax.rsqrt(jnp.sum(sl * sl, axis=-1, keepdims=True) * (1.0 / QK_HEAD) + EPS)
    return sl * r, r


def _head_rms_bwd(n, r, dn):
    return r * (dn - n * (jnp.sum(dn * n, axis=-1, keepdims=True) * (1.0 / QK_HEAD)))


def _rope(x, c, s1, s2):
    return x * c + pltpu.roll(x, QK_ROPE // 2, 1) * s1 + pltpu.roll(x, LANES - QK_ROPE // 2, 1) * s2


def _rope_t(dy, c, s1, s2):
    return dy * c + pltpu.roll(dy * s1, LANES - QK_ROPE // 2, 1) + pltpu.roll(dy * s2, QK_ROPE // 2, 1)


def _rope_tables(seq):
    half = QK_ROPE // 2
    inv_freq = ROPE_THETA ** (-jnp.arange(0, QK_ROPE, 2, dtype=F32) / QK_ROPE)
    ang = jnp.arange(seq, dtype=F32)[:, None] * inv_freq[None, :]
    cos, sin = jnp.cos(ang), jnp.sin(ang)
    z = lambda n: jnp.zeros((seq, n), F32)
    tail = LANES - QK_HEAD
    c = jnp.concatenate([jnp.ones((seq, QK_NOPE), F32), cos, cos, jnp.ones((seq, tail), F32)], axis=1)
    s1 = jnp.concatenate([z(QK_NOPE + half), sin, z(tail)], axis=1)
    s2 = jnp.concatenate([z(QK_NOPE), -sin, z(half + tail)], axis=1)
    return c, s1, s2


def _row(tm, w):
    return pl.BlockSpec((tm, w), lambda i: (i, 0))


def _modspec(d, tps):
    return pl.BlockSpec((None, N_MOD, d), lambda i: (i // tps, 0, 0))


def _seqv(w, tps):
    return pl.BlockSpec((None, 1, w), lambda i: (i // tps, 0, 0))


def _full(shape):
    return pl.BlockSpec(shape, lambda i: tuple(0 for _ in shape))


def _sds(shape, dtype):
    return jax.ShapeDtypeStruct(shape, dtype)


CONV_ROWS = 64
CONV_LC = CONV_CH // LANES


def _lane_chunks():
    return [(lc, slice(lc * LANES, (lc + 1) * LANES)) for lc in range(CONV_LC)]


def _fill_shifted(ext_ref, head, body):
    nh = head.shape[0]
    for lc, ls in _lane_chunks():
        ext_ref[0, lc, :nh, :] = head[:, ls]
        ext_ref[0, lc, nh:, :] = body[:, ls]
        rows = ext_ref[0, lc]
        for b in range(1, SUBLANES):
            ext_ref[b, lc] = pltpu.roll(rows, rows.shape[0] - b, 0)


def _shifted_shape(tm):
    return (SUBLANES, CONV_LC, tm + HALO, LANES)


def _conv_chunk(c):
    return c % CONV_LC, pl.multiple_of((c // CONV_LC) * CONV_ROWS, CONV_ROWS)


def _shifted(ext_ref, o, lc, r0):
    a = pl.multiple_of((o // SUBLANES) * SUBLANES + r0, SUBLANES)
    return ext_ref[o % SUBLANES, lc, pl.ds(a, CONV_ROWS), :]


def _by_lane_chunk(a):
    return a.reshape(a.shape[0], CONV_LC, LANES).transpose(1, 0, 2)


def _load_resident(i, pairs):
    @pl.when(i == 0)
    def _():
        for src, dst in pairs:
            pltpu.sync_copy(src, dst)


def _place():
    return lax.axis_index("x"), lax.axis_index("y"), lax.axis_index("c")


def _all_gather8(blocks, name):
    na = len(blocks)

    def body(*refs):
        start, forward, finish = _gather8_phases(refs[:na], refs[na:2 * na], *refs[2 * na:])
        start()
        forward()
        finish()

    outs = pl.pallas_call(
        body, name=name, out_shape=_gather8_shapes(blocks), in_specs=[ANY] * na, out_specs=(ANY,) * na,
        scratch_shapes=_gather8_sems(na),
    )(*blocks)
    return _own_block_placed(outs, blocks)


def _gather8_shapes(blocks):
    return tuple(_sds((8,) + b.shape, b.dtype) for b in blocks)


def _gather8_sems(na):
    return [pltpu.SemaphoreType.DMA((7 * na,)), pltpu.SemaphoreType.DMA((7 * na,))]


def _own_block_placed(outs, blocks):
    ix, iy, ic = _place()
    return tuple(lax.dynamic_update_index_in_dim(o, b, 4 * ix + 2 * iy + ic, 0) for o, b in zip(outs, blocks))


def _gather8_phases(x_refs, out_refs, send_sems, recv_sems):
    na = len(x_refs)
    x, y, c = _place()
    me, sibling = (x, y, c), (x, y, 1 - c)
    chips = [(1 - x, y), (x, 1 - y), (1 - x, 1 - y)]

    def copy(a, k, blk, to, from_input=False):
        dst = out_refs[a].at[4 * blk[0] + 2 * blk[1] + blk[2]]
        return pltpu.make_async_remote_copy(
            src_ref=x_refs[a] if from_input else dst, dst_ref=dst,
            send_sem=send_sems.at[7 * a + k], recv_sem=recv_sems.at[7 * a + k], device_id=to, device_id_type=MESH)

    def first(a):
        return [copy(a, 0, me, sibling, True)] + [copy(a, 1 + j, me, (*chip, c), True) for j, chip in enumerate(chips)]

    def start():
        for a in range(na):
            for cp in first(a):
                cp.start()

    def forward():
        for j, chip in enumerate(chips):
            for a in range(na):
                copy(a, 1 + j, (*chip, c), me).wait_recv()
                copy(a, 4 + j, (*chip, c), sibling).start()

    def finish():
        for a in range(na):
            copy(a, 0, sibling, me).wait_recv()
            for j, chip in enumerate(chips):
                copy(a, 4 + j, (*chip, 1 - c), me).wait_recv()
        for a in range(na):
            for cp in first(a) + [copy(a, 4 + j, (*chip, c), sibling) for j, chip in enumerate(chips)]:
                cp.wait_send()

    return start, forward, finish


def _pair_swap(gs, name):
    na = len(gs)

    def body(*refs):
        start, finish = _swap_phases(refs[:na], refs[na:2 * na], *refs[2 * na:])
        start()
        finish()

    return pl.pallas_call(
        body, name=name, out_shape=_swap_shapes(gs), in_specs=[ANY] * na, out_specs=(ANY,) * na,
        scratch_shapes=_swap_sems(gs),
    )(*gs)


def _swap_shapes(gs):
    return tuple(_sds(g.shape[:1] + g.shape[2:], g.dtype) for g in gs)


def _swap_sems(gs):
    n = sum(g.shape[0] for g in gs)
    return [pltpu.SemaphoreType.DMA((n,)), pltpu.SemaphoreType.DMA((n,))]


def _swap_phases(g_refs, land_refs, send_sems, recv_sems):
    x, y, c = _place()

    def copies():
        cps, k = [], 0
        for g_ref, land_ref in zip(g_refs, land_refs):
            for s in range(g_ref.shape[0]):
                cps.append(pltpu.make_async_remote_copy(
                    src_ref=g_ref.at[s, 1 - c], dst_ref=land_ref.at[s], send_sem=send_sems.at[k],
                    recv_sem=recv_sems.at[k], device_id=(x, y, 1 - c), device_id_type=MESH))
                k += 1
        return cps

    def start():
        for cp in copies():
            cp.start()

    def finish():
        for cp in copies():
            cp.wait()

    return start, finish


def _scatter_shapes(hs):
    return tuple(_sds((3,) + h.shape[1:], h.dtype) for h in hs)


def _scatter_sems(na):
    return [pltpu.SemaphoreType.DMA((3 * na,)), pltpu.SemaphoreType.DMA((3 * na,))]


def _scatter_phases(h_refs, land_refs, send_sems, recv_sems):
    x, y, c = _place()
    chips = [(1 - x, y), (x, 1 - y), (1 - x, 1 - y)]

    def copies():
        return [pltpu.make_async_remote_copy(
            src_ref=h_refs[a].at[2 * tx + ty], dst_ref=land_refs[a].at[j], send_sem=send_sems.at[3 * a + j],
            recv_sem=recv_sems.at[3 * a + j], device_id=(tx, ty, c), device_id_type=MESH)
            for a in range(len(h_refs)) for j, (tx, ty) in enumerate(chips)]

    def start():
        for cp in copies():
            cp.start()

    def finish():
        for cp in copies():
            cp.wait()

    return start, finish


def _pair_gather(fs, name):
    na = len(fs)

    def body(*refs):
        out_refs = refs[na:2 * na]
        send_sems, recv_sems = refs[2 * na:]
        x, y, c = _place()
        sends = [pltpu.make_async_remote_copy(
            src_ref=out_refs[a].at[c], dst_ref=out_refs[a].at[c], send_sem=send_sems.at[a], recv_sem=recv_sems.at[a],
            device_id=(x, y, 1 - c), device_id_type=MESH) for a in range(na)]
        recvs = [pltpu.make_async_remote_copy(
            src_ref=out_refs[a].at[c], dst_ref=out_refs[a].at[1 - c], send_sem=send_sems.at[a],
            recv_sem=recv_sems.at[a], device_id=(x, y, 1 - c), device_id_type=MESH) for a in range(na)]
        for cp in sends:
            cp.start()
        for cp in recvs:
            cp.wait_recv()
        for cp in sends:
            cp.wait_send()

    return pl.pallas_call(
        body, name=name, out_shape=tuple(_sds(f.shape, f.dtype) for f in fs),
        in_specs=[ANY] * na, out_specs=(ANY,) * na, input_output_aliases={a: a for a in range(na)},
        scratch_shapes=[pltpu.SemaphoreType.DMA((na,)), pltpu.SemaphoreType.DMA((na,))],
    )(*fs)


def _row_tile(r, n, itemsize=4, budget=1 << 20):
    if r * n * itemsize <= budget:
        return r
    best = None
    for tr in range(16, r, 16):
        if r % tr == 0 and tr * n * itemsize <= budget:
            best = tr
    assert best is not None, (r, n)
    return best


def _add_pair(g, land, cidx, name):
    ns, _, r, n = g.shape
    tr = _row_tile(r, n)

    def body(c_ref, a_ref, b_ref, o_ref, ob_ref):
        s = a_ref[...] + b_ref[...]
        o_ref[...] = s
        ob_ref[...] = s.astype(BF16)

    out = pl.BlockSpec((None, tr, n), lambda s, i, cr: (s, i, 0))
    return pl.pallas_call(
        body, name=name, out_shape=(_sds((ns, r, n), F32), _sds((ns, r, n), BF16)),
        grid_spec=pltpu.PrefetchScalarGridSpec(
            num_scalar_prefetch=1, grid=(ns, r // tr),
            in_specs=[pl.BlockSpec((None, None, tr, n), lambda s, i, cr: (s, cr[0], i, 0)), out],
            out_specs=(out, out)),
        compiler_params=_params("arbitrary", "arbitrary"),
    )(cidx, g, land)


def _add_chips(h, land, own_c, name):
    _, r, n = h.shape
    tr = _row_tile(r, n)

    def body(o_idx, h_ref, l_ref, o_ref):
        o_ref[...] = ((h_ref[...] + l_ref[0].astype(F32)) + l_ref[1].astype(F32)) + l_ref[2].astype(F32)

    return pl.pallas_call(
        body, name=name, out_shape=_sds((2, r, n), F32),
        grid_spec=pltpu.PrefetchScalarGridSpec(
            num_scalar_prefetch=1, grid=(r // tr,),
            in_specs=[pl.BlockSpec((None, tr, n), lambda i, o: (o[0], i, 0)),
                      pl.BlockSpec((3, tr, n), lambda i, o: (0, i, 0))],
            out_specs=pl.BlockSpec((None, tr, n), lambda i, o: (o[1], i, 0))),
        compiler_params=_params("arbitrary"),
    )(own_c, h, land)


def _adam_math(w, g, m, v):
    nm = ADAM_B1 * m + (1.0 - ADAM_B1) * g
    nv = ADAM_B2 * v + (1.0 - ADAM_B2) * (g * g)
    m_hat = nm / (1.0 - ADAM_B1 ** ADAM_STEP)
    v_hat = nv / (1.0 - ADAM_B2 ** ADAM_STEP)
    return -ADAM_LR * (m_hat / (jnp.sqrt(v_hat) + ADAM_EPS) + ADAM_WD * w), nm, nv


def _adamw(w, g, m, v, name):
    r, n = w.shape
    tr = _row_tile(r, n, budget=1 << 19)

    def body(w_ref, g_ref, m_ref, v_ref, d_ref, nm_ref, nv_ref):
        d_ref[...], nm_ref[...], nv_ref[...] = _adam_math(w_ref[...], g_ref[...], m_ref[...], v_ref[...])

    spec = pl.BlockSpec((tr, n), lambda i: (i, 0))
    return pl.pallas_call(
        body, name=name, out_shape=(_sds((r, n), F32),) * 3, grid=(r // tr,),
        in_specs=[spec] * 4, out_specs=(spec,) * 3, compiler_params=_params("arbitrary"),
    )(w, g, m, v)


def _adamw_small(ws, gs, ms, vs):
    k = len(ws)

    def body(*refs):
        ins, outs = refs[:4 * k], refs[4 * k:]
        for j in range(k):
            d, nm, nv = _adam_math(ins[j][...], ins[k + j][...], ins[2 * k + j][...], ins[3 * k + j][...])
            outs[j][...] = d
            outs[k + j][...] = nm
            outs[2 * k + j][...] = nv

    shapes = tuple(_sds(w.shape, F32) for w in ws)
    res = pl.pallas_call(body, name="adamw_small", out_shape=shapes * 3,
                         compiler_params=pltpu.CompilerParams(vmem_limit_bytes=VMEM_LIMIT))(*ws, *gs, *ms, *vs)
    return res[:k], res[k:2 * k], res[2 * k:]


def _ada_mod(c_all, w_sh, b_sh):
    b, _ = c_all.shape
    n = w_sh.shape[1]

    def body(c_ref, w_ref, b_ref, o_ref):
        cc = c_ref[...]
        ca = (cc * jax.nn.sigmoid(cc)).astype(BF16)
        o_ref[...] = _dot(ca, w_ref[...].astype(BF16)) + b_ref[...]

    return pl.pallas_call(body, name="ada_mod", out_shape=_sds((b, n), F32),
                          compiler_params=pltpu.CompilerParams(vmem_limit_bytes=VMEM_LIMIT))(c_all, w_sh, b_sh)


def _ada_bwd(c_all, dmod_all, dmod_sh, parts):
    b, d = c_all.shape
    n6 = dmod_all.shape[1]
    n = dmod_sh.shape[1]
    k = len(parts)

    def body(*refs):
        c_ref, da_ref, ds_ref = refs[:3]
        p_refs = refs[3:3 + k]
        dw_ref, db_ref = refs[3 + k:5 + k]
        s_refs = refs[5 + k:]
        cc = c_ref[...]
        ca = (cc * jax.nn.sigmoid(cc)).astype(BF16)
        dw_ref[...] = _dot_tn(ca, ds_ref[...].astype(BF16))
        db_ref[...] = jnp.sum(da_ref[...], axis=0, keepdims=True)
        for p_ref, s_ref in zip(p_refs, s_refs):
            tot = p_ref[0]
            for j in range(1, p_ref.shape[0]):
                tot = tot + p_ref[j]
            s_ref[...] = tot

    return pl.pallas_call(
        body, name="ada_bwd",
        out_shape=(_sds((d, n), F32), _sds((1, n6), F32)) + tuple(_sds(p.shape[1:], F32) for p in parts),
        compiler_params=pltpu.CompilerParams(vmem_limit_bytes=VMEM_LIMIT),
    )(c_all, dmod_all, dmod_sh, *parts)


def _fwd_in(x, g1, mod3, win_p, tm, tps):
    t, d = x.shape
    p_glu, p_q, npad = _layout(d)

    def body(x_ref, g_ref, mod_ref, w_hbm, h_ref, zm_ref, zglu_ref, zgate_ref, u0_ref, w_ref):
        _load_resident(pl.program_id(0), [(w_hbm, w_ref)])
        n, _ = _rms(x_ref[...])
        h = ((n * g_ref[...]) * (1.0 + mod_ref[1:2, :]) + mod_ref[0:1, :]).astype(BF16)
        h_ref[...] = h
        z = _dot(h, w_ref[...])
        zgate_ref[...] = z[:, :p_glu]
        zglu = z[:, p_glu:p_q]
        zglu_ref[...] = zglu
        zm_ref[...] = z[:, p_q:]
        u0_ref[...] = zglu[:, :CONV_CH] * jax.nn.sigmoid(zglu[:, CONV_CH:])

    return pl.pallas_call(
        body, name="fwd_in", grid=(t // tm,),
        out_shape=(_sds((t, d), BF16), _sds((t, MLA_IN), F32), _sds((t, 2 * CONV_CH), F32), _sds((t, 2 * d), F32),
                   _sds((t, CONV_CH), F32)),
        in_specs=[_row(tm, d), _full((1, d)), _modspec(d, tps), ANY],
        out_specs=(_row(tm, d), _row(tm, MLA_IN), _row(tm, 2 * CONV_CH), _row(tm, 2 * d), _row(tm, CONV_CH)),
        scratch_shapes=[pltpu.VMEM(win_p.shape, BF16)],
        compiler_params=_params("arbitrary"),
    )(x, g1, mod3, win_p)


def _mla_prep(zm, gql, gkvl, gq, gk, tabs, wuq_p, wk_p, wv_p, tm, tps):
    t = zm.shape[0]
    c_t, s1_t, s2_t = tabs
    tab = pl.BlockSpec((tm, LANES), lambda i: (i % tps, 0))

    def body(zm_ref, gql_ref, gkvl_ref, gq_ref, gk_ref, c_ref, s1_ref, s2_ref, wuq_ref, wk_ref, wv_ref,
             q_ref, k_ref, v_ref, qln_ref, kvn_ref):
        c, s1, s2 = c_ref[...], s1_ref[...], s2_ref[...]
        nq, _ = _rms(zm_ref[:, :Q_RANK])
        qln = (nq * gql_ref[...]).astype(BF16)
        qln_ref[...] = qln
        qpre = _dot(qln, wuq_ref[...])
        nkv, _ = _rms(zm_ref[:, Q_RANK:OFF_KV])
        kvn = (nkv * gkvl_ref[...]).astype(BF16)
        kvn_ref[...] = kvn
        knope = _dot(kvn, wk_ref[...])
        v_ref[...] = _dot(kvn, wv_ref[...]).astype(BF16)
        zkr_v = zm_ref[:, OFF_KV:]
        kr_roped = _rope(zkr_v * gk_ref[...], c, s1, s2)
        for hd in range(N_HEADS):
            sl = slice(hd * LANES, (hd + 1) * LANES)
            n, _ = _head_rms(qpre[:, sl])
            q_ref[:, sl] = _rope(n * gq_ref[...], c, s1, s2).astype(BF16)
            _, r = _head_rms(knope[:, sl] + zkr_v)
            k_ref[:, sl] = (r * (knope[:, sl] * gk_ref[...] + kr_roped)).astype(BF16)

    return pl.pallas_call(
        body, name="mla_prep", grid=(t // tm,),
        out_shape=(_sds((t, HW), BF16),) * 3 + (_sds((t, Q_RANK), BF16), _sds((t, KV_RANK), BF16)),
        in_specs=[_row(tm, MLA_IN), _full((1, Q_RANK)), _full((1, KV_RANK)),
                  _full((1, LANES)), _full((1, LANES)), tab, tab, tab,
                  _full(wuq_p.shape), _full(wk_p.shape), _full(wv_p.shape)],
        out_specs=(_row(tm, HW),) * 3 + (_row(tm, Q_RANK), _row(tm, KV_RANK)),
        compiler_params=_params("arbitrary"),
    )(zm, gql, gkvl, gq, gk, c_t, s1_t, s2_t, wuq_p, wk_p, wv_p)


SM_SCALE = QK_HEAD ** -0.5
EXP2_SCALE = SM_SCALE * 1.4426950408889634


def _diag_mask():
    rc = jnp.right_shift(lax.broadcasted_iota(jnp.int32, (BQ, 1), 0), CHUNK_SHIFT)
    cc = jnp.right_shift(lax.broadcasted_iota(jnp.int32, (1, BQ), 1), CHUNK_SHIFT)
    return rc >= cc


def _softmax_parts(q_i, k_ref, lo, e, mask):
    sd = jnp.where(mask, _dot_nt(q_i, k_ref[lo:e, :]), jnp.finfo(F32).min)
    m = jnp.max(sd, axis=-1, keepdims=True)
    if lo:
        sp = _dot_nt(q_i, k_ref[:lo, :])
        m = jnp.maximum(m, jnp.max(sp, axis=-1, keepdims=True))
    pd = jnp.exp2((sd - m) * EXP2_SCALE)
    l = jnp.sum(pd, axis=-1, keepdims=True)
    pp = None
    if lo:
        pp = jnp.exp2((sp - m) * EXP2_SCALE)
        l = l + jnp.sum(pp, axis=-1, keepdims=True)
    return pp, pd, l


def _attn_fwd(q, k, v, nseq, seq, gather=()):
    t = q.shape[0]
    na = len(gather)
    blk = pl.BlockSpec((seq, LANES), lambda b, h: (b, h))
    n_steps = nseq * N_HEADS

    def body(q_ref, k_ref, v_ref, *rest):
        o_ref = rest[na]
        if na:
            start, forward, finish = _gather8_phases(rest[:na], rest[na + 1:2 * na + 1], *rest[2 * na + 1:])
            step = pl.program_id(0) * N_HEADS + pl.program_id(1)
            pl.when(step == 0)(start)
            pl.when(step == (3 * n_steps) // 4)(forward)
        mask = _diag_mask()
        for i in range(seq // BQ):
            lo, e = i * BQ, (i + 1) * BQ
            pp, pd, l = _softmax_parts(q_ref[lo:e, :], k_ref, lo, e, mask)
            o = _dot(pd.astype(BF16), v_ref[lo:e, :])
            if lo:
                o = o + _dot(pp.astype(BF16), v_ref[:lo, :])
            o_ref[lo:e, :] = (o * (1.0 / l)).astype(BF16)
        if na:
            pl.when(step == n_steps - 1)(finish)

    res = pl.pallas_call(
        body, name="attn_fwd", grid=(nseq, N_HEADS), out_shape=(_sds((t, HW), BF16),) + _gather8_shapes(gather),
        in_specs=[blk, blk, blk] + [ANY] * na, out_specs=(blk,) + (ANY,) * na,
        scratch_shapes=_gather8_sems(na) if na else [],
        compiler_params=_params("arbitrary", "arbitrary"),
    )(q, k, v, *gather)
    return res[0], (_own_block_placed(res[1:], gather) if na else ())


def _fwd_mix(attn, u0, zgate, x, mod3, wo_p, cw, cb, lng, lnb, wpw, wout, tm, tps):
    t, d = x.shape
    hpt = tm // HALO
    cwc, cbc = _by_lane_chunk(cw), _by_lane_chunk(cb)

    def body(a_ref, u_ref, uh_ref, zg_ref, x_ref, mod_ref, wo_ref, cw_ref, cb_ref, lng_ref, lnb_ref, wpw_ref, wout_ref,
             x1_ref, mixed_ref, mpre_ref, ya_ref, yb_ref, u1_ref, u3_ref, ext_ref):
        i = pl.program_id(0)
        ya = _dot(a_ref[...], wo_ref[...])
        ya_ref[...] = ya
        first = (i % tps) == 0
        _fill_shifted(ext_ref, jnp.where(first, 0.0, uh_ref[...]), u_ref[...])
        for lc, ls in _lane_chunks():
            acc = jnp.broadcast_to(cb_ref[lc], (tm, LANES))
            for kk in range(CONV_W):
                o = HALO - (CONV_W - 1) + kk
                a = (o // SUBLANES) * SUBLANES
                acc = acc + cw_ref[lc, kk:kk + 1, :] * ext_ref[o % SUBLANES, lc, a:a + tm, :]
            u1_ref[:, ls] = acc
        acc = u1_ref[...]
        mu = jnp.mean(acc, axis=-1, keepdims=True)
        xc = acc - mu
        rstd = lax.rsqrt(jnp.mean(xc * xc, axis=-1, keepdims=True) + EPS)
        l = (xc * rstd) * lng_ref[...] + lnb_ref[...]
        u3 = (l * jax.nn.sigmoid(l)).astype(BF16)
        u3_ref[...] = u3
        yb = _dot(u3, wpw_ref[...])
        yb_ref[...] = yb
        zg = zg_ref[...]
        mpre = (jax.nn.sigmoid(zg[:, :d]) * ya + jax.nn.sigmoid(zg[:, d:]) * yb).astype(BF16)
        mpre_ref[...] = mpre
        mixed = _dot(mpre, wout_ref[...])
        mixed_ref[...] = mixed
        x1_ref[...] = x_ref[...] + mod_ref[2:3, :] * mixed

    halo = pl.BlockSpec((HALO, CONV_CH), lambda i: (jnp.maximum(i * hpt - 1, 0), 0))
    return pl.pallas_call(
        body, name="fwd_mix", grid=(t // tm,),
        out_shape=(_sds((t, d), F32), _sds((t, d), F32), _sds((t, d), BF16), _sds((t, d), F32), _sds((t, d), F32),
                   _sds((t, CONV_CH), F32), _sds((t, CONV_CH), BF16)),
        in_specs=[_row(tm, HW), _row(tm, CONV_CH), halo, _row(tm, 2 * d), _row(tm, d), _modspec(d, tps),
                  _full(wo_p.shape), _full(cwc.shape), _full(cbc.shape), _full((1, CONV_CH)), _full((1, CONV_CH)),
                  _full(wpw.shape), _full(wout.shape)],
        out_specs=(_row(tm, d), _row(tm, d), _row(tm, d), _row(tm, d), _row(tm, d), _row(tm, CONV_CH),
                   _row(tm, CONV_CH)),
        scratch_shapes=[pltpu.VMEM(_shifted_shape(tm), F32)],
        compiler_params=_params("arbitrary"),
    )(attn, u0, u0, zgate, x, mod3, wo_p, cwc, cbc, lng, lnb, wpw, wout)


def _fwd_ffn(x1, target, g2, mod3, w1, w2, tm, tps):
    t, d = x1.shape
    dff = w1.shape[1]

    def body(x1_ref, tg_ref, g_ref, mod_ref, w1_hbm, w2_hbm,
             h2_ref, a_ref, r_ref, dy_ref, df_ref, dgate_ref, loss_ref, w1_ref, w2_ref):
        i = pl.program_id(0)
        _load_resident(i, [(w1_hbm, w1_ref), (w2_hbm, w2_ref)])
        x1v = x1_ref[...]
        gate2 = mod_ref[5:6, :]
        n, _ = _rms(x1v)
        h2 = ((n * g_ref[...]) * (1.0 + mod_ref[4:5, :]) + mod_ref[3:4, :]).astype(BF16)
        h2_ref[...] = h2
        a = _dot(h2, w1_ref[...])
        a_ref[...] = a
        r = jnp.square(jnp.maximum(a, 0.0)).astype(BF16)
        r_ref[...] = r
        f = _dot(r, w2_ref[...])
        e = (x1v + gate2 * f) - tg_ref[...]
        part = 0.5 * jnp.sum(jnp.mean(e * e, axis=-1, keepdims=True), axis=0, keepdims=True)
        _acc(loss_ref, jnp.broadcast_to(part, loss_ref.shape), i == 0)
        dy = e * (1.0 / d)
        dy_ref[...] = dy
        df_ref[...] = (dy * gate2).astype(BF16)
        _acc(dgate_ref, jnp.sum(dy * f, axis=0, keepdims=True), (i % tps) == 0)

    nseq = t // (tm * tps)
    return pl.pallas_call(
        body, name="fwd_ffn", grid=(t // tm,),
        out_shape=(_sds((t, d), BF16), _sds((t, dff), F32), _sds((t, dff), BF16), _sds((t, d), F32), _sds((t, d), BF16),
                   _sds((nseq, 1, d), F32), _sds((8, LANES), F32)),
        in_specs=[_row(tm, d), _row(tm, d), _full((1, d)), _modspec(d, tps), ANY, ANY],
        out_specs=(_row(tm, d), _row(tm, dff), _row(tm, dff), _row(tm, d), _row(tm, d), _seqv(d, tps),
                   _full((8, LANES))),
        scratch_shapes=[pltpu.VMEM(w1.shape, BF16), pltpu.VMEM(w2.shape, BF16)],
        compiler_params=_params("arbitrary"),
    )(x1, target, g2, mod3, w1, w2)


def _bwd_ffn(df, a, x1, dy, mixed, g2, mod3, w2, w1, tm, tps):
    t, d = x1.shape
    dff = a.shape[1]

    def body(df_ref, a_ref, x1_ref, dy_ref, mx_ref, g_ref, mod_ref, w2_hbm, w1_hbm,
             da_ref, dx1_ref, dmixed_ref, dshift_ref, dscale_ref, dgate1_ref, dg2_ref, w2_ref, w1_ref):
        i = pl.program_id(0)
        _load_resident(i, [(w2_hbm, w2_ref), (w1_hbm, w1_ref)])
        first_seq = (i % tps) == 0
        dr = _dot_nt(df_ref[...], w2_ref[...])
        da = (dr * (2.0 * jnp.maximum(a_ref[...], 0.0))).astype(BF16)
        da_ref[...] = da
        dh2 = _dot_nt(da, w1_ref[...])
        n, r = _rms(x1_ref[...])
        g = g_ref[...]
        sc1 = 1.0 + mod_ref[4:5, :]
        _acc(dshift_ref, jnp.sum(dh2, axis=0, keepdims=True), first_seq)
        _acc(dscale_ref, jnp.sum(dh2 * (n * g), axis=0, keepdims=True), first_seq)
        _acc(dg2_ref, jnp.sum((dh2 * sc1) * n, axis=0, keepdims=True), i == 0)
        dx1 = dy_ref[...] + _rms_bwd(n, r, (dh2 * sc1) * g)
        dx1_ref[...] = dx1
        _acc(dgate1_ref, jnp.sum(dx1 * mx_ref[...], axis=0, keepdims=True), first_seq)
        dmixed_ref[...] = (dx1 * mod_ref[2:3, :]).astype(BF16)

    nseq = t // (tm * tps)
    sv = _sds((nseq, 1, d), F32)
    return pl.pallas_call(
        body, name="bwd_ffn", grid=(t // tm,),
        out_shape=(_sds((t, dff), BF16), _sds((t, d), F32), _sds((t, d), BF16), sv, sv, sv, _sds((1, d), F32)),
        in_specs=[_row(tm, d), _row(tm, dff), _row(tm, d), _row(tm, d), _row(tm, d), _full((1, d)), _modspec(d, tps),
                  ANY, ANY],
        out_specs=(_row(tm, dff), _row(tm, d), _row(tm, d), _seqv(d, tps), _seqv(d, tps), _seqv(d, tps),
                   _full((1, d))),
        scratch_shapes=[pltpu.VMEM(w2.shape, BF16), pltpu.VMEM(w1.shape, BF16)],
        compiler_params=_params("arbitrary"),
    )(df, a, x1, dy, mixed, g2, mod3, w2, w1)


def _bwd_mix(dmixed, zgate, ya, yb, u1, lng, lnb, wout, wo_p, wpw, tm, swap=()):
    t, d = ya.shape
    _, _, npad = _layout(d)
    nw = len(swap)
    n_steps = t // tm

    def body(dm_ref, zg_ref, ya_ref, yb_ref, u1_ref, lng_ref, lnb_ref, wout_ref, wo_ref, wpw_ref, *rest):
        dya_ref, dyb_ref, dz_ref, do_ref, du1_ref, dlng_ref, dlnb_ref, dcb_ref = rest[nw:nw + 8]
        i = pl.program_id(0)
        if nw:
            start, finish = _swap_phases(rest[:nw], rest[nw + 8:2 * nw + 8], *rest[2 * nw + 8:])
            pl.when(i == 0)(start)
        dmpre = _dot_nt(dm_ref[...], wout_ref[...])
        zg = zg_ref[...]
        ga = jax.nn.sigmoid(zg[:, :d])
        gb = jax.nn.sigmoid(zg[:, d:])
        dya = (dmpre * ga).astype(BF16)
        dyb = (dmpre * gb).astype(BF16)
        dya_ref[...] = dya
        dyb_ref[...] = dyb
        dz_ref[:, :d] = ((dmpre * ya_ref[...]) * (ga * (1.0 - ga))).astype(BF16)
        dz_ref[:, d:] = ((dmpre * yb_ref[...]) * (gb * (1.0 - gb))).astype(BF16)
        do_ref[...] = _dot_nt(dya, wo_ref[...]).astype(BF16)
        du3 = _dot_nt(dyb, wpw_ref[...])
        u1 = u1_ref[...]
        mu = jnp.mean(u1, axis=-1, keepdims=True)
        xc = u1 - mu
        rstd = lax.rsqrt(jnp.mean(xc * xc, axis=-1, keepdims=True) + EPS)
        nh = xc * rstd
        l = nh * lng_ref[...] + lnb_ref[...]
        sg = jax.nn.sigmoid(l)
        dl = du3 * (sg * (1.0 + l * (1.0 - sg)))
        _acc(dlng_ref, jnp.sum(dl * nh, axis=0, keepdims=True), i == 0)
        _acc(dlnb_ref, jnp.sum(dl, axis=0, keepdims=True), i == 0)
        dnh = dl * lng_ref[...]
        du1 = rstd * (dnh - jnp.mean(dnh, axis=-1, keepdims=True) - nh * jnp.mean(dnh * nh, axis=-1, keepdims=True))
        du1_ref[...] = du1
        _acc(dcb_ref, jnp.sum(du1, axis=0, keepdims=True), i == 0)
        if nw:
            pl.when(i == n_steps - 1)(finish)

    cv = _sds((1, CONV_CH), F32)
    res = pl.pallas_call(
        body, name="bwd_mix", grid=(n_steps,),
        out_shape=(_sds((t, d), BF16), _sds((t, d), BF16), _sds((t, npad), BF16), _sds((t, HW), BF16),
                   _sds((t, CONV_CH), F32), cv, cv, cv) + _swap_shapes(swap),
        in_specs=[_row(tm, d), _row(tm, 2 * d), _row(tm, d), _row(tm, d), _row(tm, CONV_CH), _full((1, CONV_CH)),
                  _full((1, CONV_CH)), _full(wout.shape), _full(wo_p.shape), _full(wpw.shape)] + [ANY] * nw,
        out_specs=(_row(tm, d), _row(tm, d), _row(tm, 2 * d), _row(tm, HW), _row(tm, CONV_CH),
                   _full((1, CONV_CH)), _full((1, CONV_CH)), _full((1, CONV_CH))) + (ANY,) * nw,
        scratch_shapes=_swap_sems(swap) if nw else [],
        compiler_params=_params("arbitrary"),
    )(dmixed, zgate, ya, yb, u1, lng, lnb, wout, wo_p, wpw, *swap)
    return res[:8] + (res[8:],)


def _bwd_conv(dz, du1, u0, zglu, cw, tm, tps):
    t = du1.shape[0]
    d = (dz.shape[1] - MLA_IN - 2 * CONV_CH) // 2
    p_glu, _, _ = _layout(d)
    hpt = tm // HALO
    last_blk = t // HALO - 1
    cwc = _by_lane_chunk(cw)

    def body(dz_hbm, du_ref, dun_ref, u_ref, uh_ref, zl_ref, cw_ref, dzl_ref, dcw_ref, ext_ref, dext_ref, dcw8_ref,
             du0_ref):
        i = pl.program_id(0)
        first = (i % tps) == 0
        last = (i % tps) == (tps - 1)
        _fill_shifted(ext_ref, jnp.where(first, 0.0, uh_ref[...]), u_ref[...])
        _fill_shifted(dext_ref, du_ref[...], jnp.where(last, 0.0, dun_ref[...]))

        @pl.when(i == 0)
        def _():
            dcw8_ref[...] = jnp.zeros_like(dcw8_ref)

        groups = CONV_ROWS // SUBLANES

        def conv_chunk(c, carry):
            lc, r0 = _conv_chunk(c)
            du = _shifted(dext_ref, 0, lc, r0)
            du0 = jnp.zeros((CONV_ROWS, LANES), F32)
            for kk in range(CONV_W):
                prod = du * _shifted(ext_ref, HALO - (CONV_W - 1) + kk, lc, r0)
                part = prod[:SUBLANES]
                for g in range(1, groups):
                    part = part + prod[g * SUBLANES:(g + 1) * SUBLANES]
                dcw8_ref[lc, kk] += part
                du0 = du0 + cw_ref[lc, kk:kk + 1, :] * _shifted(dext_ref, CONV_W - 1 - kk, lc, r0)
            du0_ref[lc, pl.ds(r0, CONV_ROWS), :] = du0
            return carry

        lax.fori_loop(0, CONV_LC * (tm // CONV_ROWS), conv_chunk, 0)

        @pl.when(i == pl.num_programs(0) - 1)
        def _():
            for lc, ls in _lane_chunks():
                dcw_ref[:, ls] = jnp.sum(dcw8_ref[lc], axis=1)

        for lc, ls in _lane_chunks():
            du0 = du0_ref[lc]
            ga = zl_ref[:, ls]
            sb = jax.nn.sigmoid(zl_ref[:, CONV_CH + lc * LANES:CONV_CH + (lc + 1) * LANES])
            dzl_ref[:, ls] = (du0 * sb).astype(BF16)
            dzl_ref[:, CONV_CH + lc * LANES:CONV_CH + (lc + 1) * LANES] = ((du0 * ga) * (sb * (1.0 - sb))).astype(BF16)

    prev = pl.BlockSpec((HALO, CONV_CH), lambda i: (jnp.maximum(i * hpt - 1, 0), 0))
    nxt = pl.BlockSpec((HALO, CONV_CH), lambda i: (jnp.minimum((i + 1) * hpt, last_blk), 0))
    glu_blk = p_glu // (2 * CONV_CH)
    return pl.pallas_call(
        body, name="bwd_conv", grid=(t // tm,),
        out_shape=(_sds(dz.shape, BF16), _sds(cw.shape, F32)),
        in_specs=[ANY, _row(tm, CONV_CH), nxt, _row(tm, CONV_CH), prev, _row(tm, 2 * CONV_CH), _full(cwc.shape)],
        out_specs=(pl.BlockSpec((tm, 2 * CONV_CH), lambda i: (i, glu_blk)), _full(cw.shape)),
        scratch_shapes=[pltpu.VMEM(_shifted_shape(tm), F32)] * 2
        + [pltpu.VMEM((CONV_LC, HALO, SUBLANES, LANES), F32), pltpu.VMEM((CONV_LC, tm, LANES), F32)],
        input_output_aliases={0: 0},
        compiler_params=_params("arbitrary"),
    )(dz, du1, du1, u0, u0, zglu, cwc)


def _attn_bwd(q, k, v, do, nseq, seq, scatter=()):
    t = q.shape[0]
    ns = len(scatter)
    blk = pl.BlockSpec((seq, LANES), lambda b, h: (b, h))
    n_steps = nseq * N_HEADS

    def body(q_ref, k_ref, v_ref, do_ref, *rest):
        dq_ref, dk_ref, dv_ref = rest[ns:ns + 3]
        dka_ref, dva_ref = rest[2 * ns + 3:2 * ns + 5]
        if ns:
            start, finish = _scatter_phases(rest[:ns], rest[ns + 3:2 * ns + 3], *rest[2 * ns + 5:])
            step = pl.program_id(0) * N_HEADS + pl.program_id(1)
            pl.when(step == 0)(start)
        dka_ref[...] = jnp.zeros_like(dka_ref)
        dva_ref[...] = jnp.zeros_like(dva_ref)
        mask = _diag_mask()
        for i in range(seq // BQ):
            lo, e = i * BQ, (i + 1) * BQ
            q_i = q_ref[lo:e, :]
            do_i = do_ref[lo:e, :]
            pp, pd, l = _softmax_parts(q_i, k_ref, lo, e, mask)
            inv = 1.0 / l
            pd = pd * inv
            dpd = _dot_nt(do_i, v_ref[lo:e, :])
            delta = jnp.sum(pd * dpd, axis=-1, keepdims=True)
            if lo:
                pp = pp * inv
                dpp = _dot_nt(do_i, v_ref[:lo, :])
                delta = delta + jnp.sum(pp * dpp, axis=-1, keepdims=True)
            dsd = (pd * (dpd - delta)).astype(BF16)
            dq = _dot(dsd, k_ref[lo:e, :])
            dka_ref[lo:e, :] += _dot_tn(dsd, q_i)
            dva_ref[lo:e, :] += _dot_tn(pd.astype(BF16), do_i)
            if lo:
                dsp = (pp * (dpp - delta)).astype(BF16)
                dq = dq + _dot(dsp, k_ref[:lo, :])
                dka_ref[:lo, :] += _dot_tn(dsp, q_i)
                dva_ref[:lo, :] += _dot_tn(pp.astype(BF16), do_i)
            dq_ref[lo:e, :] = dq * SM_SCALE
        dk_ref[...] = dka_ref[...] * SM_SCALE
        dv_ref[...] = dva_ref[...].astype(BF16)
        if ns:
            pl.when(step == n_steps - 1)(finish)

    res = pl.pallas_call(
        body, name="attn_bwd", grid=(nseq, N_HEADS),
        out_shape=(_sds((t, HW), F32), _sds((t, HW), F32), _sds((t, HW), BF16)) + _scatter_shapes(scatter),
        in_specs=[blk] * 4 + [ANY] * ns, out_specs=(blk,) * 3 + (ANY,) * ns,
        scratch_shapes=[pltpu.VMEM((seq, LANES), F32), pltpu.VMEM((seq, LANES), F32)]
        + (_scatter_sems(ns) if ns else []),
        compiler_params=_params("arbitrary", "arbitrary"),
    )(q, k, v, do, *scatter)
    return res[0], res[1], res[2], res[3:]


def _mla_bwd(dz, dq, dk, dv, zm, gql, gkvl, gq, gk, tabs, wuq_p, wk_p, wv_p, tm, tps):
    t = zm.shape[0]
    d = (dz.shape[1] - MLA_IN - 2 * CONV_CH) // 2
    _, p_q, _ = _layout(d)
    c_t, s1_t, s2_t = tabs
    tab = pl.BlockSpec((tm, LANES), lambda i: (i % tps, 0))

    def body(dz_hbm, dq_ref, dk_ref, dv_ref, zm_ref, gql_ref, gkvl_ref, gq_ref, gk_ref, c_ref, s1_ref, s2_ref,
             wuq_ref, wk_ref, wv_ref,
             dzm_ref, dqpre_ref, dkh_ref, dgq_ref, dgk_ref, dgql_ref, dgkvl_ref):
        i = pl.program_id(0)
        c, s1, s2 = c_ref[...], s1_ref[...], s2_ref[...]
        nq, rq = _rms(zm_ref[:, :Q_RANK])
        qpre = _dot((nq * gql_ref[...]).astype(BF16), wuq_ref[...])
        nkv, rkv = _rms(zm_ref[:, Q_RANK:OFF_KV])
        knope = _dot((nkv * gkvl_ref[...]).astype(BF16), wk_ref[...])
        zkr_v = zm_ref[:, OFF_KV:]
        gk = gk_ref[...]
        kr_roped = _rope(zkr_v * gk, c, s1, s2)
        dgq = jnp.zeros((1, LANES), F32)
        dgk = jnp.zeros((1, LANES), F32)
        dzkr = jnp.zeros((tm, LANES), F32)
        dt_sum = jnp.zeros((tm, LANES), F32)
        for hd in range(N_HEADS):
            sl = slice(hd * LANES, (hd + 1) * LANES)
            n, r = _head_rms(qpre[:, sl])
            dyr = _rope_t(dq_ref[:, sl], c, s1, s2)
            dgq = dgq + jnp.sum(dyr * n, axis=0, keepdims=True)
            dqpre_ref[:, sl] = _head_rms_bwd(n, r, dyr * gq_ref[...]).astype(BF16)
            kn = knope[:, sl]
            kh = kn + zkr_v
            _, r = _head_rms(kh)
            dk = dk_ref[:, sl]
            dt = dk * r
            dr = jnp.sum(dk * (kn * gk + kr_roped), axis=-1, keepdims=True)
            via_r = (dr * (r * r * r) * (-1.0 / QK_HEAD)) * kh
            dgk = dgk + jnp.sum(dt * kn, axis=0, keepdims=True)
            dt_sum = dt_sum + dt
            dzkr = dzkr + via_r
            dkh_ref[:, sl] = (dt * gk + via_r).astype(BF16)
        de = _rope_t(dt_sum, c, s1, s2)
        dzkr = dzkr + de * gk
        dgk = dgk + jnp.sum(de * zkr_v, axis=0, keepdims=True)
        _acc(dgq_ref, dgq[:, :QK_HEAD], i == 0)
        _acc(dgk_ref, dgk[:, :QK_HEAD], i == 0)
        dzm_ref[:, OFF_KV:] = dzkr.astype(BF16)
        dqln = _dot_nt(dqpre_ref[...], wuq_ref[...])
        _acc(dgql_ref, jnp.sum(dqln * nq, axis=0, keepdims=True), i == 0)
        dzm_ref[:, :Q_RANK] = _rms_bwd(nq, rq, dqln * gql_ref[...]).astype(BF16)
        dkvn = _dot_nt(dkh_ref[...], wk_ref[...]) + _dot_nt(dv_ref[...], wv_ref[...])
        _acc(dgkvl_ref, jnp.sum(dkvn * nkv, axis=0, keepdims=True), i == 0)
        dzm_ref[:, Q_RANK:OFF_KV] = _rms_bwd(nkv, rkv, dkvn * gkvl_ref[...]).astype(BF16)

    return pl.pallas_call(
        body, name="mla_bwd", grid=(t // tm,),
        out_shape=(_sds(dz.shape, BF16), _sds((t, HW), BF16), _sds((t, HW), BF16), _sds((1, QK_HEAD), F32),
                   _sds((1, QK_HEAD), F32), _sds((1, Q_RANK), F32), _sds((1, KV_RANK), F32)),
        in_specs=[ANY, _row(tm, HW), _row(tm, HW), _row(tm, HW), _row(tm, MLA_IN),
                  _full((1, Q_RANK)), _full((1, KV_RANK)), _full((1, LANES)), _full((1, LANES)), tab, tab, tab,
                  _full(wuq_p.shape), _full(wk_p.shape), _full(wv_p.shape)],
        out_specs=(pl.BlockSpec((tm, MLA_IN), lambda i: (i, p_q // MLA_IN)), _row(tm, HW), _row(tm, HW),
                   _full((1, QK_HEAD)), _full((1, QK_HEAD)), _full((1, Q_RANK)), _full((1, KV_RANK))),
        input_output_aliases={0: 0},
        compiler_params=_params("arbitrary"),
    )(dz, dq, dk, dv, zm, gql, gkvl, gq, gk, c_t, s1_t, s2_t, wuq_p, wk_p, wv_p)


def _bwd_in(dz, x, dx1, g1, mod3, win_p, tm, tps, scatter=()):
    t, d = x.shape
    npad = dz.shape[1]

    ns = len(scatter)
    n_steps = t // tm

    def body(dz_ref, x_ref, dx1_ref, g_ref, mod_ref, wt_hbm, *rest):
        gx_ref, dshift_ref, dscale_ref, dg1_ref = rest[ns:ns + 4]
        wt_ref = rest[2 * ns + 4]
        i = pl.program_id(0)
        if ns:
            start, finish = _scatter_phases(rest[:ns], rest[ns + 4:2 * ns + 4], *rest[2 * ns + 5:])
            pl.when(i == 0)(start)
        _load_resident(i, [(wt_hbm, wt_ref)])
        first_seq = (i % tps) == 0
        dh = _dot_nt(dz_ref[...], wt_ref[...])
        n, r = _rms(x_ref[...])
        g = g_ref[...]
        sc1 = 1.0 + mod_ref[1:2, :]
        _acc(dshift_ref, jnp.sum(dh, axis=0, keepdims=True), first_seq)
        _acc(dscale_ref, jnp.sum(dh * (n * g), axis=0, keepdims=True), first_seq)
        _acc(dg1_ref, jnp.sum((dh * sc1) * n, axis=0, keepdims=True), i == 0)
        gx_ref[...] = dx1_ref[...] + _rms_bwd(n, r, (dh * sc1) * g)
        if ns:
            pl.when(i == n_steps - 1)(finish)

    nseq = t // (tm * tps)
    sv = _sds((nseq, 1, d), F32)
    res = pl.pallas_call(
        body, name="bwd_in", grid=(n_steps,),
        out_shape=(_sds((t, d), F32), sv, sv, _sds((1, d), F32)) + _scatter_shapes(scatter),
        in_specs=[_row(tm, npad), _row(tm, d), _row(tm, d), _full((1, d)), _modspec(d, tps), ANY] + [ANY] * ns,
        out_specs=(_row(tm, d), _seqv(d, tps), _seqv(d, tps), _full((1, d))) + (ANY,) * ns,
        scratch_shapes=[pltpu.VMEM(win_p.shape, BF16)] + (_scatter_sems(ns) if ns else []),
        compiler_params=_params("arbitrary"),
    )(dz, x, dx1, g1, mod3, win_p, *scatter)
    return res[0], res[1], res[2], res[3], res[4:]


def _tile_of(n, choices):
    for c in choices:
        if n % c == 0:
            return c
    return n


def _tn_matmul(a, b, name, col_shards=0):
    t, k = a.shape
    n = b.shape[1]
    tk = _tile_of(k, (1024, 512, 256, 128))
    tn = n // col_shards if col_shards else _tile_of(n, (1024, 896, 768, 512, 384, 256, 128))
    tt = _tile_of(t, (1024, 512, 256))

    def body(a_ref, b_ref, o_ref):
        _acc(o_ref, _dot_tn(a_ref[...], b_ref[...]), pl.program_id(2) == 0)

    if col_shards:
        out_shape, out_spec = _sds((col_shards, k, tn), F32), pl.BlockSpec((None, tk, tn), lambda i, j, s: (j, i, 0))
    else:
        out_shape, out_spec = _sds((k, n), F32), pl.BlockSpec((tk, tn), lambda i, j, s: (i, j))
    return pl.pallas_call(
        body, name=name, grid=(k // tk, n // tn, t // tt), out_shape=out_shape,
        in_specs=[pl.BlockSpec((tt, tk), lambda i, j, s: (s, i)), pl.BlockSpec((tt, tn), lambda i, j, s: (s, j))],
        out_specs=out_spec, compiler_params=_params("arbitrary", "arbitrary", "arbitrary"),
    )(a, b)


N_SHARD = 4
COL_SHARDED = ("w_in", "w_uq", "w_ukv", "w_o_mla", "w_pw_out", "w_ff1")
ROW_SHARDED = ("w_out", "w_ff2")
BIG = ("w_in", "w_uq", "w_ukv", "w_o_mla", "w_pw_out", "w_out", "w_ff1", "w_ff2")
SMALL = ("norm1_g", "q_latent_g", "kv_latent_g", "qk_norm_q_g", "qk_norm_k_g", "conv_b", "conv_ln_g", "conv_ln_b",
         "norm2_g")
WEIGHTS = ("w_ada", "b_ada", "norm1_g", "w_in", "q_latent_g", "w_uq", "kv_latent_g", "w_ukv", "qk_norm_q_g",
           "qk_norm_k_g", "w_o_mla", "conv_w", "conv_b", "conv_ln_g", "conv_ln_b", "w_pw_out", "w_out", "norm2_g",
           "w_ff1", "w_ff2")


def _pad_heads(w, width):
    k = w.shape[0]
    w3 = w.reshape(k, N_HEADS, width)
    return jnp.pad(w3, ((0, 0), (0, 0), (0, LANES - width))).reshape(k, HW)


def _unpad_heads(g, width):
    k = g.shape[0]
    return g.reshape(k, N_HEADS, LANES)[:, :, :width].reshape(k, N_HEADS * width)


def _pad_win(w):
    d = w.shape[0]
    z = lambda n: jnp.zeros((d, n), w.dtype)
    return jnp.concatenate([w[:, OFF_GLU:], w[:, OFF_KR:OFF_GLU], w[:, :OFF_KV], z(KR_LANE), w[:, OFF_KV:OFF_KR],
                            z(LANES - KR_LANE - QK_ROPE)], axis=1)


def _unpad_win(g):
    d = g.shape[0]
    p_glu, p_q, _ = _layout(d)
    kr = p_q + OFF_KV + KR_LANE
    return jnp.concatenate([g[:, p_q:p_q + OFF_KV], g[:, kr:kr + QK_ROPE], g[:, p_glu:p_q], g[:, :p_glu]], axis=1)


def _col_shards(g):
    k, n = g.shape
    return g.reshape(k, N_SHARD, n // N_SHARD).transpose(1, 0, 2)


def _from_shards(g, name):
    ns, ks, nn = g.shape
    if name in ROW_SHARDED:
        return g.reshape(ns * ks, nn)
    return g.transpose(1, 0, 2).reshape(ks, ns * nn)


EARLY = ("w_in", "w_uq", "w_ukv")
LATE = ("w_o_mla", "w_pw_out", "w_out", "w_ff1", "w_ff2")


def _assemble(names, gathered):
    return {n: _from_shards(g.reshape((N_SHARD, 2 * g.shape[1]) + g.shape[2:]), n) for n, g in zip(names, gathered)}


GROUP_A = ("w_out", "w_ff1", "w_ff2")
GROUP_B = ("w_in", "w_uq", "w_ukv", "w_o_mla", "w_pw_out")


def _pair_halves(g):
    return g.reshape(N_SHARD, 2, g.shape[1] // 2, g.shape[2])


def _pair_sums(names, halves, from_sibling):
    if not halves:
        return []
    cidx = lax.axis_index("c").reshape(1).astype(jnp.int32)
    return [_add_pair(g, l, cidx, "pair_sum_" + n) for n, g, l in zip(names, halves, from_sibling)]


def _local_step(x, target, mod, sp, w, late=None, tm=256):
    comm = late is not None
    w = dict(w)
    nseq, seq, d = x.shape
    t = nseq * seq
    tps = seq // tm
    xf = x.reshape(t, d)
    tg = target.reshape(t, d)
    mod3 = mod.reshape(nseq, N_MOD, d)

    win_p = _pad_win(w["w_in"])
    wuq_p = _pad_heads(w["w_uq"], QK_HEAD)
    wkv3 = w["w_ukv"].reshape(KV_RANK, N_HEADS, QK_NOPE + V_HEAD)
    wk_p = _pad_heads(wkv3[:, :, :QK_NOPE].reshape(KV_RANK, -1), QK_NOPE)
    wv_p = _pad_heads(wkv3[:, :, QK_NOPE:].reshape(KV_RANK, -1), V_HEAD)
    cw = jnp.pad(w["conv_w"], ((0, HALO - CONV_W), (0, 0)))
    pad_g = lambda g: jnp.pad(g, ((0, 0), (0, LANES - QK_HEAD)))
    gq, gk = pad_g(sp["qk_norm_q_g"]), pad_g(sp["qk_norm_k_g"])
    tabs = _rope_tables(seq)

    tm_in, tps_in = (2 * tm, tps // 2) if tps % 2 == 0 else (tm, tps)
    h, zm, zglu, zgate, u0 = _fwd_in(xf, sp["norm1_g"], mod3, win_p, tm_in, tps_in)
    q, k, v, qln, kvn = _mla_prep(zm, sp["q_latent_g"], sp["kv_latent_g"], gq, gk, tabs, wuq_p, wk_p, wv_p, tm, tps)
    attn, gathered = _attn_fwd(q, k, v, nseq, seq, tuple(late) if comm else ())
    if comm:
        w.update(_assemble(LATE, gathered))
    wo_p = jnp.pad(w["w_o_mla"].reshape(N_HEADS, V_HEAD, d), ((0, 0), (0, LANES - V_HEAD), (0, 0))).reshape(HW, d)
    x1, mixed, mpre, ya, yb, u1, u3 = _fwd_mix(attn, u0, zgate, xf, mod3, wo_p, cw, sp["conv_b"], sp["conv_ln_g"],
                                               sp["conv_ln_b"], w["w_pw_out"], w["w_out"], tm, tps)
    h2, a, r, dy, df, dgate2, loss_acc = _fwd_ffn(x1, tg, sp["norm2_g"], mod3, w["w_ff1"], w["w_ff2"], tm, tps)
    da, dx1, dmixed, dshift2, dscale2, dgate1, dg2 = _bwd_ffn(df, a, x1, dy, mixed, sp["norm2_g"], mod3,
                                                              w["w_ff2"], w["w_ff1"], tm, tps)
    gw = {
        "w_out": _tn_matmul(mpre, dmixed, "dw_out").reshape(N_SHARD, d // N_SHARD, d),
        "w_ff1": _tn_matmul(h2, da, "dw_ff1", N_SHARD),
        "w_ff2": _tn_matmul(r, df, "dw_ff2").reshape(N_SHARD, -1, d),
    }
    halves_a = [_pair_halves(gw[n]) for n in GROUP_A] if comm else []
    dya, dyb, dz, do, du1, dlng, dlnb, dcb, from_sibling = _bwd_mix(
        dmixed, zgate, ya, yb, u1, sp["conv_ln_g"], sp["conv_ln_b"], w["w_out"], wo_p, w["w_pw_out"], tm, tuple(halves_a))
    pair_a = _pair_sums(GROUP_A, halves_a, from_sibling)
    dz, dcw = _bwd_conv(dz, du1, u0, zglu, cw, tm, tps)
    gw["conv_w"] = dcw
    dq, dk, dv, land_a = _attn_bwd(q, k, v, do, nseq, seq, tuple(p[1] for p in pair_a))
    dz, dqpre, dkh, dgq, dgk, dgql, dgkvl = _mla_bwd(dz, dq, dk, dv, zm, sp["q_latent_g"], sp["kv_latent_g"], gq, gk,
                                                      tabs, wuq_p, wk_p, wv_p, tm, tps)
    dwk_p = _tn_matmul(kvn, dkh, "dw_uk")
    dwv_p = _tn_matmul(kvn, dv, "dw_uv")
    dwkv = jnp.concatenate([dwk_p.reshape(KV_RANK, N_HEADS, LANES)[:, :, :QK_NOPE],
                            dwv_p.reshape(KV_RANK, N_HEADS, LANES)[:, :, :V_HEAD]], axis=2).reshape(KV_RANK, -1)
    dwo = _tn_matmul(attn, dya, "dw_o").reshape(N_HEADS, LANES, d)[:, :V_HEAD].reshape(MLA_WIDTH, d)
    gw["w_in"] = _col_shards(_unpad_win(_tn_matmul(h, dz, "dw_in")))
    gw["w_uq"] = _col_shards(_unpad_heads(_tn_matmul(qln, dqpre, "dw_uq"), QK_HEAD))
    gw["w_ukv"] = _col_shards(dwkv)
    gw["w_o_mla"] = _col_shards(dwo)
    gw["w_pw_out"] = _tn_matmul(u3, dyb, "dw_pw", N_SHARD)
    pair_b = []
    if comm:
        halves_b = [_pair_halves(gw[n]) for n in GROUP_B]
        pair_b = _pair_sums(GROUP_B, halves_b, _pair_swap(halves_b, "grad_pair_swap"))
    gx, dshift1, dscale1, dg1, land_b = _bwd_in(dz, xf, dx1, sp["norm1_g"], mod3, win_p, tm_in, tps_in,
                                                tuple(p[1] for p in pair_b))
    if comm:
        for n, p, l in zip(GROUP_A + GROUP_B, pair_a + pair_b, land_a + land_b):
            gw[n] = (p[0], l)
    gs = {
        "norm1_g": dg1, "q_latent_g": dgql, "kv_latent_g": dgkvl, "qk_norm_q_g": dgq, "qk_norm_k_g": dgk,
        "conv_b": dcb, "conv_ln_g": dlng, "conv_ln_b": dlnb, "norm2_g": dg2,
    }
    dmod = jnp.concatenate([dshift1, dscale1, dgate1, dshift2, dscale2, dgate2], axis=2).reshape(nseq, N_MOD * d)
    return loss_acc, gx.reshape(nseq, seq, d), dmod, gw, gs


def kernel(x, c, w_ada, b_ada, norm1_g, w_in, q_latent_g, w_uq, kv_latent_g, w_ukv, qk_norm_q_g, qk_norm_k_g, w_o_mla, conv_w, conv_b, conv_ln_g, conv_ln_b, w_pw_out, w_out, norm2_g, w_ff1, w_ff2, loss_target, m_w_ada, m_b_ada, m_norm1_g, m_w_in, m_q_latent_g, m_w_uq, m_kv_latent_g, m_w_ukv, m_qk_norm_q_g, m_qk_norm_k_g, m_w_o_mla, m_conv_w, m_conv_b, m_conv_ln_g, m_conv_ln_b, m_w_pw_out, m_w_out, m_norm2_g, m_w_ff1, m_w_ff2, v_w_ada, v_b_ada, v_norm1_g, v_w_in, v_q_latent_g, v_w_uq, v_kv_latent_g, v_w_ukv, v_qk_norm_q_g, v_qk_norm_k_g, v_w_o_mla, v_conv_w, v_conv_b, v_conv_ln_g, v_conv_ln_b, v_w_pw_out, v_w_out, v_norm2_g, v_w_ff1, v_w_ff2):
    given = dict(locals())
    wts = {n: given[n][0] for n in WEIGHTS}
    mom = {n: given["m_" + n][0] for n in WEIGHTS}
    var = {n: given["v_" + n][0] for n in WEIGHTS}
    vec = lambda a: a.reshape(1, -1)
    nseq, seq, d = x.shape
    ix, iy, ic = _place()
    shard = 2 * ix + iy

    half = lambda n: lax.dynamic_slice_in_dim(wts[n].astype(BF16), ic * (wts[n].shape[0] // 2), wts[n].shape[0] // 2,
                                              axis=0)
    gathered = _all_gather8([half(n) for n in EARLY] + [wts["conv_w"], c], "gather_weights")
    full = _assemble(EARLY, gathered)
    full["conv_w"] = _from_shards(gathered[-2][0::2], "conv_w")
    c_all = gathered[-1].reshape(8 * nseq, d)

    n_ada = wts["w_ada"].shape[1]
    b_sh = lax.dynamic_slice_in_dim(vec(wts["b_ada"]), shard * n_ada, n_ada, axis=1)
    mod_sh = _ada_mod(c_all, wts["w_ada"], b_sh)
    hb = 4 * nseq
    mod_blk = lax.dynamic_slice_in_dim(mod_sh, ic * hb, hb, axis=0)
    (mod_all,) = _all_gather8([mod_blk], "gather_mod")
    mod_mine = lax.dynamic_slice_in_dim(mod_all, (2 * iy + ic) * nseq, nseq, axis=1)
    mod = jnp.concatenate([lax.dynamic_index_in_dim(mod_mine, 2 * s + ix, axis=0, keepdims=False)
                           for s in range(N_SHARD)], axis=1)

    sp = {n: vec(wts[n]) for n in SMALL}
    loss_part, grad_x, dmod, gw, gs = _local_step(x, loss_target, mod, sp, full, [half(n) for n in LATE])

    parts = _all_gather8([dmod, gw["conv_w"], loss_part] + [gs[n] for n in SMALL], "gather_small")
    dmod_all = parts[0].reshape(8 * nseq, N_MOD * d)
    dmod_sh = lax.dynamic_slice_in_dim(dmod_all, shard * n_ada, n_ada, axis=1)
    res = _ada_bwd(c_all, dmod_all, dmod_sh, parts[1:])
    grads = {"w_ada": res[0], "b_ada": res[1]}
    n_cw = wts["conv_w"].shape[1]
    grads["conv_w"] = lax.dynamic_slice_in_dim(res[2], shard * n_cw, n_cw, axis=1)[:CONV_W]
    loss = res[3][0, 0]
    for n, g in zip(SMALL, res[4:]):
        grads[n] = g

    own_c = jnp.stack([shard, ic]).astype(jnp.int32)
    mine_sum = [_add_chips(gw[n][0], gw[n][1], own_c, "chip_sum_" + n) for n in BIG]
    for n, g in zip(BIG, _pair_gather(mine_sum, "grad_pair_gather")):
        grads[n] = g.reshape(wts[n].shape)

    delta, new_m, new_v = {}, {}, {}
    for n in BIG + ("w_ada",):
        delta[n], new_m[n], new_v[n] = _adamw(wts[n], grads[n], mom[n], var[n], "adamw_" + n)
    rest = ("b_ada", "conv_w") + SMALL
    as2d = lambda a: a if a.ndim == 2 else vec(a)
    res = _adamw_small(*[[as2d(t[n]) for n in rest] for t in (wts, grads, mom, var)])
    for dst, arrs in zip((delta, new_m, new_v), res):
        for n, a in zip(rest, arrs):
            dst[n] = a

    outs = [loss, grad_x]
    for group in (grads, delta, new_m, new_v):
        outs += [group[n].reshape(given[n].shape) for n in WEIGHTS]
    return tuple(outs)
```

```python
import jax
import jax.numpy as jnp
from jax import lax
from jax.experimental import pallas as pl
from jax.experimental.pallas import tpu as pltpu

F32 = jnp.float32
BF16 = jnp.bfloat16
MESH = pl.DeviceIdType.MESH
ANY = pl.BlockSpec(memory_space=pl.ANY)

CHUNK = 64
CHUNK_SHIFT = 6
N_HEADS = 8
QK_NOPE = 64
QK_ROPE = 32
QK_HEAD = QK_NOPE + QK_ROPE
V_HEAD = 64
Q_RANK = 256
KV_RANK = 128
MLA_WIDTH = N_HEADS * V_HEAD
CONV_CH = 512
CONV_W = 31
ROPE_THETA = 10000.0
EPS = 1e-6
LANES = 128
SUBLANES = 8
HW = N_HEADS * LANES
OFF_KV = Q_RANK + KV_RANK
OFF_KR = OFF_KV + QK_ROPE
OFF_GLU = OFF_KR + 2 * CONV_CH
KR_LANE = QK_NOPE
MLA_IN = Q_RANK + KV_RANK + LANES
HALO = 32
N_MOD = 6

ADAM_LR = 0.001
ADAM_B1 = 0.9
ADAM_B2 = 0.999
ADAM_EPS = 1e-08
ADAM_WD = 0.01
ADAM_STEP = 10

VMEM_LIMIT = 56 * 1024 * 1024
BQ = 256


def _layout(d):
    p_glu = 2 * d
    p_q = p_glu + 2 * CONV_CH
    return p_glu, p_q, p_q + MLA_IN


def _params(*sem):
    return pltpu.CompilerParams(dimension_semantics=sem, vmem_limit_bytes=VMEM_LIMIT)


def _dot(a, b):
    return jnp.dot(a, b, preferred_element_type=F32)


def _dot_tn(a, b):
    return lax.dot_general(a, b, (((0,), (0,)), ((), ())), preferred_element_type=F32)


def _dot_nt(a, b):
    return lax.dot_general(a, b, (((1,), (1,)), ((), ())), preferred_element_type=F32)


def _acc(ref, val, first):
    @pl.when(first)
    def _():
        ref[...] = val

    @pl.when(jnp.logical_not(first))
    def _():
        ref[...] += val


def _rms(x):
    r = lax.rsqrt(jnp.mean(x * x, axis=-1, keepdims=True) + EPS)
    return x * r, r


def _rms_bwd(n, r, dn):
    return r * (dn - n * jnp.mean(dn * n, axis=-1, keepdims=True))


def _head_rms(sl):
    r = lax.rsqrt(jnp.sum(sl * sl, axis=-1, keepdims=True) * (1.0 / QK_HEAD) + EPS)
    return sl * r, r


def _head_rms_bwd(n, r, dn):
    return r * (dn - n * (jnp.sum(dn * n, axis=-1, keepdims=True) * (1.0 / QK_HEAD)))


def _rope(x, c, s1, s2):
    return x * c + pltpu.roll(x, QK_ROPE // 2, 1) * s1 + pltpu.roll(x, LANES - QK_ROPE // 2, 1) * s2


def _rope_t(dy, c, s1, s2):
    return dy * c + pltpu.roll(dy * s1, LANES - QK_ROPE // 2, 1) + pltpu.roll(dy * s2, QK_ROPE // 2, 1)


def _rope_tables(seq):
    half = QK_ROPE // 2
    inv_freq = ROPE_THETA ** (-jnp.arange(0, QK_ROPE, 2, dtype=F32) / QK_ROPE)
    ang = jnp.arange(seq, dtype=F32)[:, None] * inv_freq[None, :]
    cos, sin = jnp.cos(ang), jnp.sin(ang)
    z = lambda n: jnp.zeros((seq, n), F32)
    tail = LANES - QK_HEAD
    c = jnp.concatenate([jnp.ones((seq, QK_NOPE), F32), cos, cos, jnp.ones((seq, tail), F32)], axis=1)
    s1 = jnp.concatenate([z(QK_NOPE + half), sin, z(tail)], axis=1)
    s2 = jnp.concatenate([z(QK_NOPE), -sin, z(half + tail)], axis=1)
    return c, s1, s2


def _row(tm, w):
    return pl.BlockSpec((tm, w), lambda i: (i, 0))


def _modspec(d, tps):
    return pl.BlockSpec((None, N_MOD, d), lambda i: (i // tps, 0, 0))


def _seqv(w, tps):
    return pl.BlockSpec((None, 1, w), lambda i: (i // tps, 0, 0))


def _full(shape):
    return pl.BlockSpec(shape, lambda i: tuple(0 for _ in shape))


def _sds(shape, dtype):
    return jax.ShapeDtypeStruct(shape, dtype)


CONV_ROWS = 64
CONV_LC = CONV_CH // LANES


def _lane_chunks():
    return [(lc, slice(lc * LANES, (lc + 1) * LANES)) for lc in range(CONV_LC)]


def _fill_shifted(ext_ref, head, body):
    nh = head.shape[0]
    for lc, ls in _lane_chunks():
        ext_ref[0, lc, :nh, :] = head[:, ls]
        ext_ref[0, lc, nh:, :] = body[:, ls]
        rows = ext_ref[0, lc]
        for b in range(1, SUBLANES):
            ext_ref[b, lc] = pltpu.roll(rows, rows.shape[0] - b, 0)


def _shifted_shape(tm):
    return (SUBLANES, CONV_LC, tm + HALO, LANES)


def _conv_chunk(c):
    return c % CONV_LC, pl.multiple_of((c // CONV_LC) * CONV_ROWS, CONV_ROWS)


def _shifted(ext_ref, o, lc, r0):
    a = pl.multiple_of((o // SUBLANES) * SUBLANES + r0, SUBLANES)
    return ext_ref[o % SUBLANES, lc, pl.ds(a, CONV_ROWS), :]


def _by_lane_chunk(a):
    return a.reshape(a.shape[0], CONV_LC, LANES).transpose(1, 0, 2)


def _load_resident(i, pairs):
    @pl.when(i == 0)
    def _():
        for src, dst in pairs:
            pltpu.sync_copy(src, dst)


def _place():
    return lax.axis_index("x"), lax.axis_index("y"), lax.axis_index("c")


def _all_gather8(blocks, name):
    na = len(blocks)

    def body(*refs):
        start, forward, finish = _gather8_phases(refs[:na], refs[na:2 * na], *refs[2 * na:])
        start()
        forward()
        finish()

    outs = pl.pallas_call(
        body, name=name, out_shape=_gather8_shapes(blocks), in_specs=[ANY] * na, out_specs=(ANY,) * na,
        scratch_shapes=_gather8_sems(na),
    )(*blocks)
    return _own_block_placed(outs, blocks)


def _gather8_shapes(blocks):
    return tuple(_sds((8,) + b.shape, b.dtype) for b in blocks)


def _gather8_sems(na):
    return [pltpu.SemaphoreType.DMA((7 * na,)), pltpu.SemaphoreType.DMA((7 * na,))]


def _own_block_placed(outs, blocks):
    ix, iy, ic = _place()
    return tuple(lax.dynamic_update_index_in_dim(o, b, 4 * ix + 2 * iy + ic, 0) for o, b in zip(outs, blocks))


def _gather8_phases(x_refs, out_refs, send_sems, recv_sems):
    na = len(x_refs)
    x, y, c = _place()
    me, sibling = (x, y, c), (x, y, 1 - c)
    chips = [(1 - x, y), (x, 1 - y), (1 - x, 1 - y)]

    def copy(a, k, blk, to, from_input=False):
        dst = out_refs[a].at[4 * blk[0] + 2 * blk[1] + blk[2]]
        return pltpu.make_async_remote_copy(
            src_ref=x_refs[a] if from_input else dst, dst_ref=dst,
            send_sem=send_sems.at[7 * a + k], recv_sem=recv_sems.at[7 * a + k], device_id=to, device_id_type=MESH)

    def first(a):
        return [copy(a, 0, me, sibling, True)] + [copy(a, 1 + j, me, (*chip, c), True) for j, chip in enumerate(chips)]

    def start():
        for a in range(na):
            for cp in first(a):
                cp.start()

    def forward():
        for j, chip in enumerate(chips):
            for a in range(na):
                copy(a, 1 + j, (*chip, c), me).wait_recv()
                copy(a, 4 + j, (*chip, c), sibling).start()

    def finish():
        for a in range(na):
            copy(a, 0, sibling, me).wait_recv()
            for j, chip in enumerate(chips):
                copy(a, 4 + j, (*chip, 1 - c), me).wait_recv()
        for a in range(na):
            for cp in first(a) + [copy(a, 4 + j, (*chip, c), sibling) for j, chip in enumerate(chips)]:
                cp.wait_send()

    return start, forward, finish


def _pair_swap(gs, name):
    na = len(gs)

    def body(*refs):
        start, finish = _swap_phases(refs[:na], refs[na:2 * na], *refs[2 * na:])
        start()
        finish()

    return pl.pallas_call(
        body, name=name, out_shape=_swap_shapes(gs), in_specs=[ANY] * na, out_specs=(ANY,) * na,
        scratch_shapes=_swap_sems(gs),
    )(*gs)


def _swap_shapes(gs):
    return tuple(_sds(g.shape[:1] + g.shape[2:], g.dtype) for g in gs)


def _swap_sems(gs):
    n = sum(g.shape[0] for g in gs)
    return [pltpu.SemaphoreType.DMA((n,)), pltpu.SemaphoreType.DMA((n,))]


def _swap_phases(g_refs, land_refs, send_sems, recv_sems):
    x, y, c = _place()

    def copies():
        cps, k = [], 0
        for g_ref, land_ref in zip(g_refs, land_refs):
            for s in range(g_ref.shape[0]):
                cps.append(pltpu.make_async_remote_copy(
                    src_ref=g_ref.at[s, 1 - c], dst_ref=land_ref.at[s], send_sem=send_sems.at[k],
                    recv_sem=recv_sems.at[k], device_id=(x, y, 1 - c), device_id_type=MESH))
                k += 1
        return cps

    def start():
        for cp in copies():
            cp.start()

    def finish():
        for cp in copies():
            cp.wait()

    return start, finish


def _scatter_shapes(hs):
    return tuple(_sds((3,) + h.shape[1:], h.dtype) for h in hs)


def _scatter_sems(na):
    return [pltpu.SemaphoreType.DMA((3 * na,)), pltpu.SemaphoreType.DMA((3 * na,))]


def _scatter_phases(h_refs, land_refs, send_sems, recv_sems):
    x, y, c = _place()
    chips = [(1 - x, y), (x, 1 - y), (1 - x, 1 - y)]

    def copies():
        return [pltpu.make_async_remote_copy(
            src_ref=h_refs[a].at[2 * tx + ty], dst_ref=land_refs[a].at[j], send_sem=send_sems.at[3 * a + j],
            recv_sem=recv_sems.at[3 * a + j], device_id=(tx, ty, c), device_id_type=MESH)
            for a in range(len(h_refs)) for j, (tx, ty) in enumerate(chips)]

    def start():
        for cp in copies():
            cp.start()

    def finish():
        for cp in copies():
            cp.wait()

    return start, finish


def _pair_gather(fs, name):
    na = len(fs)

    def body(*refs):
        out_refs = refs[na:2 * na]
        send_sems, recv_sems = refs[2 * na:]
        x, y, c = _place()
        sends = [pltpu.make_async_remote_copy(
            src_ref=out_refs[a].at[c], dst_ref=out_refs[a].at[c], send_sem=send_sems.at[a], recv_sem=recv_sems.at[a],
            device_id=(x, y, 1 - c), device_id_type=MESH) for a in range(na)]
        recvs = [pltpu.make_async_remote_copy(
            src_ref=out_refs[a].at[c], dst_ref=out_refs[a].at[1 - c], send_sem=send_sems.at[a],
            recv_sem=recv_sems.at[a], device_id=(x, y, 1 - c), device_id_type=MESH) for a in range(na)]
        for cp in sends:
            cp.start()
        for cp in recvs:
            cp.wait_recv()
        for cp in sends:
            cp.wait_send()

    return pl.pallas_call(
        body, name=name, out_shape=tuple(_sds(f.shape, f.dtype) for f in fs),
        in_specs=[ANY] * na, out_specs=(ANY,) * na, input_output_aliases={a: a for a in range(na)},
        scratch_shapes=[pltpu.SemaphoreType.DMA((na,)), pltpu.SemaphoreType.DMA((na,))],
    )(*fs)


def _row_tile(r, n, itemsize=4, budget=1 << 20):
    if r * n * itemsize <= budget:
        return r
    best = None
    for tr in range(16, r, 16):
        if r % tr == 0 and tr * n * itemsize <= budget:
            best = tr
    assert best is not None, (r, n)
    return best


def _add_pair(g, land, cidx, name):
    ns, _, r, n = g.shape
    tr = _row_tile(r, n)

    def body(c_ref, a_ref, b_ref, o_ref, ob_ref):
        s = a_ref[...] + b_ref[...]
        o_ref[...] = s
        ob_ref[...] = s.astype(BF16)

    out = pl.BlockSpec((None, tr, n), lambda s, i, cr: (s, i, 0))
    return pl.pallas_call(
        body, name=name, out_shape=(_sds((ns, r, n), F32), _sds((ns, r, n), BF16)),
        grid_spec=pltpu.PrefetchScalarGridSpec(
            num_scalar_prefetch=1, grid=(ns, r // tr),
            in_specs=[pl.BlockSpec((None, None, tr, n), lambda s, i, cr: (s, cr[0], i, 0)), out],
            out_specs=(out, out)),
        compiler_params=_params("arbitrary", "arbitrary"),
    )(cidx, g, land)


def _add_chips(h, land, own_c, name):
    _, r, n = h.shape
    tr = _row_tile(r, n)

    def body(o_idx, h_ref, l_ref, o_ref):
        o_ref[...] = ((h_ref[...] + l_ref[0].astype(F32)) + l_ref[1].astype(F32)) + l_ref[2].astype(F32)

    return pl.pallas_call(
        body, name=name, out_shape=_sds((2, r, n), F32),
        grid_spec=pltpu.PrefetchScalarGridSpec(
            num_scalar_prefetch=1, grid=(r // tr,),
            in_specs=[pl.BlockSpec((None, tr, n), lambda i, o: (o[0], i, 0)),
                      pl.BlockSpec((3, tr, n), lambda i, o: (0, i, 0))],
            out_specs=pl.BlockSpec((None, tr, n), lambda i, o: (o[1], i, 0))),
        compiler_params=_params("arbitrary"),
    )(own_c, h, land)


def _adam_math(w, g, m, v):
    nm = ADAM_B1 * m + (1.0 - ADAM_B1) * g
    nv = ADAM_B2 * v + (1.0 - ADAM_B2) * (g * g)
    m_hat = nm / (1.0 - ADAM_B1 ** ADAM_STEP)
    v_hat = nv / (1.0 - ADAM_B2 ** ADAM_STEP)
    return -ADAM_LR * (m_hat / (jnp.sqrt(v_hat) + ADAM_EPS) + ADAM_WD * w), nm, nv


def _adamw(w, g, m, v, name):
    r, n = w.shape
    tr = _row_tile(r, n, budget=1 << 19)

    def body(w_ref, g_ref, m_ref, v_ref, d_ref, nm_ref, nv_ref):
        d_ref[...], nm_ref[...], nv_ref[...] = _adam_math(w_ref[...], g_ref[...], m_ref[...], v_ref[...])

    spec = pl.BlockSpec((tr, n), lambda i: (i, 0))
    return pl.pallas_call(
        body, name=name, out_shape=(_sds((r, n), F32),) * 3, grid=(r // tr,),
        in_specs=[spec] * 4, out_specs=(spec,) * 3, compiler_params=_params("arbitrary"),
    )(w, g, m, v)


def _adamw_small(ws, gs, ms, vs):
    k = len(ws)

    def body(*refs):
        ins, outs = refs[:4 * k], refs[4 * k:]
        for j in range(k):
            d, nm, nv = _adam_math(ins[j][...], ins[k + j][...], ins[2 * k + j][...], ins[3 * k + j][...])
            outs[j][...] = d
            outs[k + j][...] = nm
            outs[2 * k + j][...] = nv

    shapes = tuple(_sds(w.shape, F32) for w in ws)
    res = pl.pallas_call(body, name="adamw_small", out_shape=shapes * 3,
                         compiler_params=pltpu.CompilerParams(vmem_limit_bytes=VMEM_LIMIT))(*ws, *gs, *ms, *vs)
    return res[:k], res[k:2 * k], res[2 * k:]


def _ada_mod(c_all, w_sh, b_sh):
    b, _ = c_all.shape
    n = w_sh.shape[1]

    def body(c_ref, w_ref, b_ref, o_ref):
        cc = c_ref[...]
        ca = (cc * jax.nn.sigmoid(cc)).astype(BF16)
        o_ref[...] = _dot(ca, w_ref[...].astype(BF16)) + b_ref[...]

    return pl.pallas_call(body, name="ada_mod", out_shape=_sds((b, n), F32),
                          compiler_params=pltpu.CompilerParams(vmem_limit_bytes=VMEM_LIMIT))(c_all, w_sh, b_sh)


def _ada_bwd(c_all, dmod_all, dmod_sh, parts):
    b, d = c_all.shape
    n6 = dmod_all.shape[1]
    n = dmod_sh.shape[1]
    k = len(parts)

    def body(*refs):
        c_ref, da_ref, ds_ref = refs[:3]
        p_refs = refs[3:3 + k]
        dw_ref, db_ref = refs[3 + k:5 + k]
        s_refs = refs[5 + k:]
        cc = c_ref[...]
        ca = (cc * jax.nn.sigmoid(cc)).astype(BF16)
        dw_ref[...] = _dot_tn(ca, ds_ref[...].astype(BF16))
        db_ref[...] = jnp.sum(da_ref[...], axis=0, keepdims=True)
        for p_ref, s_ref in zip(p_refs, s_refs):
            tot = p_ref[0]
            for j in range(1, p_ref.shape[0]):
                tot = tot + p_ref[j]
            s_ref[...] = tot

    return pl.pallas_call(
        body, name="ada_bwd",
        out_shape=(_sds((d, n), F32), _sds((1, n6), F32)) + tuple(_sds(p.shape[1:], F32) for p in parts),
        compiler_params=pltpu.CompilerParams(vmem_limit_bytes=VMEM_LIMIT),
    )(c_all, dmod_all, dmod_sh, *parts)


def _fwd_in(x, g1, mod3, win_p, tm, tps):
    t, d = x.shape
    p_glu, p_q, npad = _layout(d)

    def body(x_ref, g_ref, mod_ref, w_hbm, h_ref, zm_ref, zglu_ref, zgate_ref, u0_ref, w_ref):
        _load_resident(pl.program_id(0), [(w_hbm, w_ref)])
        n, _ = _rms(x_ref[...])
        h = ((n * g_ref[...]) * (1.0 + mod_ref[1:2, :]) + mod_ref[0:1, :]).astype(BF16)
        h_ref[...] = h
        z = _dot(h, w_ref[...])
        zgate_ref[...] = z[:, :p_glu]
        zglu = z[:, p_glu:p_q]
        zglu_ref[...] = zglu
        zm_ref[...] = z[:, p_q:]
        u0_ref[...] = zglu[:, :CONV_CH] * jax.nn.sigmoid(zglu[:, CONV_CH:])

    return pl.pallas_call(
        body, name="fwd_in", grid=(t // tm,),
        out_shape=(_sds((t, d), BF16), _sds((t, MLA_IN), F32), _sds((t, 2 * CONV_CH), F32), _sds((t, 2 * d), F32),
                   _sds((t, CONV_CH), F32)),
        in_specs=[_row(tm, d), _full((1, d)), _modspec(d, tps), ANY],
        out_specs=(_row(tm, d), _row(tm, MLA_IN), _row(tm, 2 * CONV_CH), _row(tm, 2 * d), _row(tm, CONV_CH)),
        scratch_shapes=[pltpu.VMEM(win_p.shape, BF16)],
        compiler_params=_params("arbitrary"),
    )(x, g1, mod3, win_p)


def _mla_prep(zm, gql, gkvl, gq, gk, tabs, wuq_p, wk_p, wv_p, tm, tps):
    t = zm.shape[0]
    c_t, s1_t, s2_t = tabs
    tab = pl.BlockSpec((tm, LANES), lambda i: (i % tps, 0))

    def body(zm_ref, gql_ref, gkvl_ref, gq_ref, gk_ref, c_ref, s1_ref, s2_ref, wuq_ref, wk_ref, wv_ref,
             q_ref, k_ref, v_ref, qln_ref, kvn_ref):
        c, s1, s2 = c_ref[...], s1_ref[...], s2_ref[...]
        nq, _ = _rms(zm_ref[:, :Q_RANK])
        qln = (nq * gql_ref[...]).astype(BF16)
        qln_ref[...] = qln
        qpre = _dot(qln, wuq_ref[...])
        nkv, _ = _rms(zm_ref[:, Q_RANK:OFF_KV])
        kvn = (nkv * gkvl_ref[...]).astype(BF16)
        kvn_ref[...] = kvn
        knope = _dot(kvn, wk_ref[...])
        v_ref[...] = _dot(kvn, wv_ref[...]).astype(BF16)
        zkr_v = zm_ref[:, OFF_KV:]
        kr_roped = _rope(zkr_v * gk_ref[...], c, s1, s2)
        for hd in range(N_HEADS):
            sl = slice(hd * LANES, (hd + 1) * LANES)
            n, _ = _head_rms(qpre[:, sl])
            q_ref[:, sl] = _rope(n * gq_ref[...], c, s1, s2).astype(BF16)
            _, r = _head_rms(knope[:, sl] + zkr_v)
            k_ref[:, sl] = (r * (knope[:, sl] * gk_ref[...] + kr_roped)).astype(BF16)

    return pl.pallas_call(
        body, name="mla_prep", grid=(t // tm,),
        out_shape=(_sds((t, HW), BF16),) * 3 + (_sds((t, Q_RANK), BF16), _sds((t, KV_RANK), BF16)),
        in_specs=[_row(tm, MLA_IN), _full((1, Q_RANK)), _full((1, KV_RANK)),
                  _full((1, LANES)), _full((1, LANES)), tab, tab, tab,
                  _full(wuq_p.shape), _full(wk_p.shape), _full(wv_p.shape)],
        out_specs=(_row(tm, HW),) * 3 + (_row(tm, Q_RANK), _row(tm, KV_RANK)),
        compiler_params=_params("arbitrary"),
    )(zm, gql, gkvl, gq, gk, c_t, s1_t, s2_t, wuq_p, wk_p, wv_p)


AHEAD = 2
SM_SCALE = QK_HEAD ** -0.5
EXP2_SCALE = SM_SCALE * 1.4426950408889634


def _diag_mask():
    rc = jnp.right_shift(lax.broadcasted_iota(jnp.int32, (BQ, 1), 0), CHUNK_SHIFT)
    cc = jnp.right_shift(lax.broadcasted_iota(jnp.int32, (1, BQ), 1), CHUNK_SHIFT)
    return rc >= cc


def _scores(q_i, k_ref, lo, e):
    return (_dot_nt(q_i, k_ref[:lo, :]) if lo else None), _dot_nt(q_i, k_ref[lo:e, :])


def _softmax_parts(scores, mask):
    sp, sd = scores
    sd = jnp.where(mask, sd, jnp.finfo(F32).min)
    m = jnp.max(sd, axis=-1, keepdims=True)
    if sp is not None:
        m = jnp.maximum(m, jnp.max(sp, axis=-1, keepdims=True))
    pd = jnp.exp2((sd - m) * EXP2_SCALE)
    l = jnp.sum(pd, axis=-1, keepdims=True)
    pp = None
    if sp is not None:
        pp = jnp.exp2((sp - m) * EXP2_SCALE)
        l = l + jnp.sum(pp, axis=-1, keepdims=True)
    return pp, pd, l


def _attn_fwd(q, k, v, nseq, seq, gather=()):
    t = q.shape[0]
    na = len(gather)
    blk = pl.BlockSpec((seq, LANES), lambda b, h: (b, h))
    n_steps = nseq * N_HEADS

    def body(q_ref, k_ref, v_ref, *rest):
        o_ref = rest[na]
        if na:
            start, forward, finish = _gather8_phases(rest[:na], rest[na + 1:2 * na + 1], *rest[2 * na + 1:])
            step = pl.program_id(0) * N_HEADS + pl.program_id(1)
            pl.when(step == 0)(start)
            pl.when(step == (3 * n_steps) // 4)(forward)
        mask = _diag_mask()
        nb = seq // BQ
        block_scores = lambda j: _scores(q_ref[j * BQ:(j + 1) * BQ, :], k_ref, j * BQ, (j + 1) * BQ)
        ahead = [block_scores(j) for j in range(min(AHEAD, nb))]
        for i in range(nb):
            lo, e = i * BQ, (i + 1) * BQ
            cur = ahead.pop(0)
            if i + AHEAD < nb:
                ahead.append(block_scores(i + AHEAD))
            pp, pd, l = _softmax_parts(cur, mask)
            o = _dot(pd.astype(BF16), v_ref[lo:e, :])
            if lo:
                o = o + _dot(pp.astype(BF16), v_ref[:lo, :])
            o_ref[lo:e, :] = (o * (1.0 / l)).astype(BF16)
        if na:
            pl.when(step == n_steps - 1)(finish)

    res = pl.pallas_call(
        body, name="attn_fwd", grid=(nseq, N_HEADS), out_shape=(_sds((t, HW), BF16),) + _gather8_shapes(gather),
        in_specs=[blk, blk, blk] + [ANY] * na, out_specs=(blk,) + (ANY,) * na,
        scratch_shapes=_gather8_sems(na) if na else [],
        compiler_params=_params("arbitrary", "arbitrary"),
    )(q, k, v, *gather)
    return res[0], (_own_block_placed(res[1:], gather) if na else ())


def _fwd_mix(attn, u0, zgate, x, mod3, wo_p, cw, cb, lng, lnb, wpw, wout, tm, tps):
    t, d = x.shape
    hpt = tm // HALO
    cwc, cbc = _by_lane_chunk(cw), _by_lane_chunk(cb)

    def body(a_ref, u_ref, uh_ref, zg_ref, x_ref, mod_ref, wo_ref, cw_ref, cb_ref, lng_ref, lnb_ref, wpw_ref, wout_ref,
             x1_ref, mixed_ref, mpre_ref, ya_ref, yb_ref, u1_ref, u3_ref, ext_ref):
        i = pl.program_id(0)
        ya = _dot(a_ref[...], wo_ref[...])
        ya_ref[...] = ya
        first = (i % tps) == 0
        _fill_shifted(ext_ref, jnp.where(first, 0.0, uh_ref[...]), u_ref[...])
        for lc, ls in _lane_chunks():
            acc = jnp.broadcast_to(cb_ref[lc], (tm, LANES))
            for kk in range(CONV_W):
                o = HALO - (CONV_W - 1) + kk
                a = (o // SUBLANES) * SUBLANES
                acc = acc + cw_ref[lc, kk:kk + 1, :] * ext_ref[o % SUBLANES, lc, a:a + tm, :]
            u1_ref[:, ls] = acc
        acc = u1_ref[...]
        mu = jnp.mean(acc, axis=-1, keepdims=True)
        xc = acc - mu
        rstd = lax.rsqrt(jnp.mean(xc * xc, axis=-1, keepdims=True) + EPS)
        l = (xc * rstd) * lng_ref[...] + lnb_ref[...]
        u3 = (l * jax.nn.sigmoid(l)).astype(BF16)
        u3_ref[...] = u3
        yb = _dot(u3, wpw_ref[...])
        yb_ref[...] = yb
        zg = zg_ref[...]
        mpre = (jax.nn.sigmoid(zg[:, :d]) * ya + jax.nn.sigmoid(zg[:, d:]) * yb).astype(BF16)
        mpre_ref[...] = mpre
        mixed = _dot(mpre, wout_ref[...])
        mixed_ref[...] = mixed
        x1_ref[...] = x_ref[...] + mod_ref[2:3, :] * mixed

    halo = pl.BlockSpec((HALO, CONV_CH), lambda i: (jnp.maximum(i * hpt - 1, 0), 0))
    return pl.pallas_call(
        body, name="fwd_mix", grid=(t // tm,),
        out_shape=(_sds((t, d), F32), _sds((t, d), F32), _sds((t, d), BF16), _sds((t, d), F32), _sds((t, d), F32),
                   _sds((t, CONV_CH), F32), _sds((t, CONV_CH), BF16)),
        in_specs=[_row(tm, HW), _row(tm, CONV_CH), halo, _row(tm, 2 * d), _row(tm, d), _modspec(d, tps),
                  _full(wo_p.shape), _full(cwc.shape), _full(cbc.shape), _full((1, CONV_CH)), _full((1, CONV_CH)),
                  _full(wpw.shape), _full(wout.shape)],
        out_specs=(_row(tm, d), _row(tm, d), _row(tm, d), _row(tm, d), _row(tm, d), _row(tm, CONV_CH),
                   _row(tm, CONV_CH)),
        scratch_shapes=[pltpu.VMEM(_shifted_shape(tm), F32)],
        compiler_params=_params("arbitrary"),
    )(attn, u0, u0, zgate, x, mod3, wo_p, cwc, cbc, lng, lnb, wpw, wout)


def _fwd_ffn(x1, target, g2, mod3, w1, w2, tm, tps):
    t, d = x1.shape
    dff = w1.shape[1]

    def body(x1_ref, tg_ref, g_ref, mod_ref, w1_hbm, w2_hbm,
             h2_ref, a_ref, r_ref, dy_ref, df_ref, dgate_ref, loss_ref, w1_ref, w2_ref):
        i = pl.program_id(0)
        _load_resident(i, [(w1_hbm, w1_ref), (w2_hbm, w2_ref)])
        x1v = x1_ref[...]
        gate2 = mod_ref[5:6, :]
        n, _ = _rms(x1v)
        h2 = ((n * g_ref[...]) * (1.0 + mod_ref[4:5, :]) + mod_ref[3:4, :]).astype(BF16)
        h2_ref[...] = h2
        a = _dot(h2, w1_ref[...])
        a_ref[...] = a
        r = jnp.square(jnp.maximum(a, 0.0)).astype(BF16)
        r_ref[...] = r
        f = _dot(r, w2_ref[...])
        e = (x1v + gate2 * f) - tg_ref[...]
        part = 0.5 * jnp.sum(jnp.mean(e * e, axis=-1, keepdims=True), axis=0, keepdims=True)
        _acc(loss_ref, jnp.broadcast_to(part, loss_ref.shape), i == 0)
        dy = e * (1.0 / d)
        dy_ref[...] = dy
        df_ref[...] = (dy * gate2).astype(BF16)
        _acc(dgate_ref, jnp.sum(dy * f, axis=0, keepdims=True), (i % tps) == 0)

    nseq = t // (tm * tps)
    return pl.pallas_call(
        body, name="fwd_ffn", grid=(t // tm,),
        out_shape=(_sds((t, d), BF16), _sds((t, dff), F32), _sds((t, dff), BF16), _sds((t, d), F32), _sds((t, d), BF16),
                   _sds((nseq, 1, d), F32), _sds((8, LANES), F32)),
        in_specs=[_row(tm, d), _row(tm, d), _full((1, d)), _modspec(d, tps), ANY, ANY],
        out_specs=(_row(tm, d), _row(tm, dff), _row(tm, dff), _row(tm, d), _row(tm, d), _seqv(d, tps),
                   _full((8, LANES))),
        scratch_shapes=[pltpu.VMEM(w1.shape, BF16), pltpu.VMEM(w2.shape, BF16)],
        compiler_params=_params("arbitrary"),
    )(x1, target, g2, mod3, w1, w2)


def _bwd_ffn(df, a, x1, dy, mixed, g2, mod3, w2, w1, tm, tps):
    t, d = x1.shape
    dff = a.shape[1]

    def body(df_ref, a_ref, x1_ref, dy_ref, mx_ref, g_ref, mod_ref, w2_hbm, w1_hbm,
             da_ref, dx1_ref, dmixed_ref, dshift_ref, dscale_ref, dgate1_ref, dg2_ref, w2_ref, w1_ref):
        i = pl.program_id(0)
        _load_resident(i, [(w2_hbm, w2_ref), (w1_hbm, w1_ref)])
        first_seq = (i % tps) == 0
        dr = _dot_nt(df_ref[...], w2_ref[...])
        da = (dr * (2.0 * jnp.maximum(a_ref[...], 0.0))).astype(BF16)
        da_ref[...] = da
        dh2 = _dot_nt(da, w1_ref[...])
        n, r = _rms(x1_ref[...])
        g = g_ref[...]
        sc1 = 1.0 + mod_ref[4:5, :]
        _acc(dshift_ref, jnp.sum(dh2, axis=0, keepdims=True), first_seq)
        _acc(dscale_ref, jnp.sum(dh2 * (n * g), axis=0, keepdims=True), first_seq)
        _acc(dg2_ref, jnp.sum((dh2 * sc1) * n, axis=0, keepdims=True), i == 0)
        dx1 = dy_ref[...] + _rms_bwd(n, r, (dh2 * sc1) * g)
        dx1_ref[...] = dx1
        _acc(dgate1_ref, jnp.sum(dx1 * mx_ref[...], axis=0, keepdims=True), first_seq)
        dmixed_ref[...] = (dx1 * mod_ref[2:3, :]).astype(BF16)

    nseq = t // (tm * tps)
    sv = _sds((nseq, 1, d), F32)
    return pl.pallas_call(
        body, name="bwd_ffn", grid=(t // tm,),
        out_shape=(_sds((t, dff), BF16), _sds((t, d), F32), _sds((t, d), BF16), sv, sv, sv, _sds((1, d), F32)),
        in_specs=[_row(tm, d), _row(tm, dff), _row(tm, d), _row(tm, d), _row(tm, d), _full((1, d)), _modspec(d, tps),
                  ANY, ANY],
        out_specs=(_row(tm, dff), _row(tm, d), _row(tm, d), _seqv(d, tps), _seqv(d, tps), _seqv(d, tps),
                   _full((1, d))),
        scratch_shapes=[pltpu.VMEM(w2.shape, BF16), pltpu.VMEM(w1.shape, BF16)],
        compiler_params=_params("arbitrary"),
    )(df, a, x1, dy, mixed, g2, mod3, w2, w1)


def _bwd_mix(dmixed, zgate, ya, yb, u1, lng, lnb, wout, wo_p, wpw, tm, swap=()):
    t, d = ya.shape
    _, _, npad = _layout(d)
    nw = len(swap)
    n_steps = t // tm

    def body(dm_ref, zg_ref, ya_ref, yb_ref, u1_ref, lng_ref, lnb_ref, wout_ref, wo_ref, wpw_ref, *rest):
        dya_ref, dyb_ref, dz_ref, do_ref, du1_ref, dlng_ref, dlnb_ref, dcb_ref = rest[nw:nw + 8]
        i = pl.program_id(0)
        if nw:
            start, finish = _swap_phases(rest[:nw], rest[nw + 8:2 * nw + 8], *rest[2 * nw + 8:])
            pl.when(i == 0)(start)
        dmpre = _dot_nt(dm_ref[...], wout_ref[...])
        zg = zg_ref[...]
        ga = jax.nn.sigmoid(zg[:, :d])
        gb = jax.nn.sigmoid(zg[:, d:])
        dya = (dmpre * ga).astype(BF16)
        dyb = (dmpre * gb).astype(BF16)
        dya_ref[...] = dya
        dyb_ref[...] = dyb
        dz_ref[:, :d] = ((dmpre * ya_ref[...]) * (ga * (1.0 - ga))).astype(BF16)
        dz_ref[:, d:] = ((dmpre * yb_ref[...]) * (gb * (1.0 - gb))).astype(BF16)
        do_ref[...] = _dot_nt(dya, wo_ref[...]).astype(BF16)
        du3 = _dot_nt(dyb, wpw_ref[...])
        u1 = u1_ref[...]
        mu = jnp.mean(u1, axis=-1, keepdims=True)
        xc = u1 - mu
        rstd = lax.rsqrt(jnp.mean(xc * xc, axis=-1, keepdims=True) + EPS)
        nh = xc * rstd
        l = nh * lng_ref[...] + lnb_ref[...]
        sg = jax.nn.sigmoid(l)
        dl = du3 * (sg * (1.0 + l * (1.0 - sg)))
        _acc(dlng_ref, jnp.sum(dl * nh, axis=0, keepdims=True), i == 0)
        _acc(dlnb_ref, jnp.sum(dl, axis=0, keepdims=True), i == 0)
        dnh = dl * lng_ref[...]
        du1 = rstd * (dnh - jnp.mean(dnh, axis=-1, keepdims=True) - nh * jnp.mean(dnh * nh, axis=-1, keepdims=True))
        du1_ref[...] = du1
        _acc(dcb_ref, jnp.sum(du1, axis=0, keepdims=True), i == 0)
        if nw:
            pl.when(i == n_steps - 1)(finish)

    cv = _sds((1, CONV_CH), F32)
    res = pl.pallas_call(
        body, name="bwd_mix", grid=(n_steps,),
        out_shape=(_sds((t, d), BF16), _sds((t, d), BF16), _sds((t, npad), BF16), _sds((t, HW), BF16),
                   _sds((t, CONV_CH), F32), cv, cv, cv) + _swap_shapes(swap),
        in_specs=[_row(tm, d), _row(tm, 2 * d), _row(tm, d), _row(tm, d), _row(tm, CONV_CH), _full((1, CONV_CH)),
                  _full((1, CONV_CH)), _full(wout.shape), _full(wo_p.shape), _full(wpw.shape)] + [ANY] * nw,
        out_specs=(_row(tm, d), _row(tm, d), _row(tm, 2 * d), _row(tm, HW), _row(tm, CONV_CH),
                   _full((1, CONV_CH)), _full((1, CONV_CH)), _full((1, CONV_CH))) + (ANY,) * nw,
        scratch_shapes=_swap_sems(swap) if nw else [],
        compiler_params=_params("arbitrary"),
    )(dmixed, zgate, ya, yb, u1, lng, lnb, wout, wo_p, wpw, *swap)
    return res[:8] + (res[8:],)


def _bwd_conv(dz, du1, u0, zglu, cw, tm, tps):
    t = du1.shape[0]
    d = (dz.shape[1] - MLA_IN - 2 * CONV_CH) // 2
    p_glu, _, _ = _layout(d)
    hpt = tm // HALO
    last_blk = t // HALO - 1
    cwc = _by_lane_chunk(cw)

    def body(dz_hbm, du_ref, dun_ref, u_ref, uh_ref, zl_ref, cw_ref, dzl_ref, dcw_ref, ext_ref, dext_ref, dcw8_ref,
             du0_ref):
        i = pl.program_id(0)
        first = (i % tps) == 0
        last = (i % tps) == (tps - 1)
        _fill_shifted(ext_ref, jnp.where(first, 0.0, uh_ref[...]), u_ref[...])
        _fill_shifted(dext_ref, du_ref[...], jnp.where(last, 0.0, dun_ref[...]))

        @pl.when(i == 0)
        def _():
            dcw8_ref[...] = jnp.zeros_like(dcw8_ref)

        groups = CONV_ROWS // SUBLANES

        def conv_chunk(c, carry):
            lc, r0 = _conv_chunk(c)
            du = _shifted(dext_ref, 0, lc, r0)
            du0 = jnp.zeros((CONV_ROWS, LANES), F32)
            for kk in range(CONV_W):
                prod = du * _shifted(ext_ref, HALO - (CONV_W - 1) + kk, lc, r0)
                part = prod[:SUBLANES]
                for g in range(1, groups):
                    part = part + prod[g * SUBLANES:(g + 1) * SUBLANES]
                dcw8_ref[lc, kk] += part
                du0 = du0 + cw_ref[lc, kk:kk + 1, :] * _shifted(dext_ref, CONV_W - 1 - kk, lc, r0)
            du0_ref[lc, pl.ds(r0, CONV_ROWS), :] = du0
            return carry

        lax.fori_loop(0, CONV_LC * (tm // CONV_ROWS), conv_chunk, 0)

        @pl.when(i == pl.num_programs(0) - 1)
        def _():
            for lc, ls in _lane_chunks():
                dcw_ref[:, ls] = jnp.sum(dcw8_ref[lc], axis=1)

        for lc, ls in _lane_chunks():
            du0 = du0_ref[lc]
            ga = zl_ref[:, ls]
            sb = jax.nn.sigmoid(zl_ref[:, CONV_CH + lc * LANES:CONV_CH + (lc + 1) * LANES])
            dzl_ref[:, ls] = (du0 * sb).astype(BF16)
            dzl_ref[:, CONV_CH + lc * LANES:CONV_CH + (lc + 1) * LANES] = ((du0 * ga) * (sb * (1.0 - sb))).astype(BF16)

    prev = pl.BlockSpec((HALO, CONV_CH), lambda i: (jnp.maximum(i * hpt - 1, 0), 0))
    nxt = pl.BlockSpec((HALO, CONV_CH), lambda i: (jnp.minimum((i + 1) * hpt, last_blk), 0))
    glu_blk = p_glu // (2 * CONV_CH)
    return pl.pallas_call(
        body, name="bwd_conv", grid=(t // tm,),
        out_shape=(_sds(dz.shape, BF16), _sds(cw.shape, F32)),
        in_specs=[ANY, _row(tm, CONV_CH), nxt, _row(tm, CONV_CH), prev, _row(tm, 2 * CONV_CH), _full(cwc.shape)],
        out_specs=(pl.BlockSpec((tm, 2 * CONV_CH), lambda i: (i, glu_blk)), _full(cw.shape)),
        scratch_shapes=[pltpu.VMEM(_shifted_shape(tm), F32)] * 2
        + [pltpu.VMEM((CONV_LC, HALO, SUBLANES, LANES), F32), pltpu.VMEM((CONV_LC, tm, LANES), F32)],
        input_output_aliases={0: 0},
        compiler_params=_params("arbitrary"),
    )(dz, du1, du1, u0, u0, zglu, cwc)


def _attn_bwd(q, k, v, do, nseq, seq, scatter=()):
    t = q.shape[0]
    ns = len(scatter)
    blk = pl.BlockSpec((seq, LANES), lambda b, h: (b, h))
    n_steps = nseq * N_HEADS

    def body(q_ref, k_ref, v_ref, do_ref, *rest):
        dq_ref, dk_ref, dv_ref = rest[ns:ns + 3]
        dka_ref, dva_ref = rest[2 * ns + 3:2 * ns + 5]
        if ns:
            start, finish = _scatter_phases(rest[:ns], rest[ns + 3:2 * ns + 3], *rest[2 * ns + 5:])
            step = pl.program_id(0) * N_HEADS + pl.program_id(1)
            pl.when(step == 0)(start)
        dka_ref[...] = jnp.zeros_like(dka_ref)
        dva_ref[...] = jnp.zeros_like(dva_ref)
        mask = _diag_mask()
        nb = seq // BQ
        ahead = _scores(q_ref[:BQ, :], k_ref, 0, BQ), _scores(do_ref[:BQ, :], v_ref, 0, BQ)
        for i in range(nb):
            lo, e = i * BQ, (i + 1) * BQ
            q_i = q_ref[lo:e, :]
            do_i = do_ref[lo:e, :]
            scores, (dpp, dpd) = ahead
            if i + 1 < nb:
                ahead = _scores(q_ref[e:e + BQ, :], k_ref, e, e + BQ), _scores(do_ref[e:e + BQ, :], v_ref, e, e + BQ)
            pp, pd, l = _softmax_parts(scores, mask)
            inv = 1.0 / l
            pd = pd * inv
            delta = jnp.sum(pd * dpd, axis=-1, keepdims=True)
            if lo:
                pp = pp * inv
                delta = delta + jnp.sum(pp * dpp, axis=-1, keepdims=True)
            dsd = (pd * (dpd - delta)).astype(BF16)
            dq = _dot(dsd, k_ref[lo:e, :])
            dka_ref[lo:e, :] += _dot_tn(dsd, q_i)
            dva_ref[lo:e, :] += _dot_tn(pd.astype(BF16), do_i)
            if lo:
                dsp = (pp * (dpp - delta)).astype(BF16)
                dq = dq + _dot(dsp, k_ref[:lo, :])
                dka_ref[:lo, :] += _dot_tn(dsp, q_i)
                dva_ref[:lo, :] += _dot_tn(pp.astype(BF16), do_i)
            dq_ref[lo:e, :] = dq * SM_SCALE
        dk_ref[...] = dka_ref[...] * SM_SCALE
        dv_ref[...] = dva_ref[...].astype(BF16)
        if ns:
            pl.when(step == n_steps - 1)(finish)

    res = pl.pallas_call(
        body, name="attn_bwd", grid=(nseq, N_HEADS),
        out_shape=(_sds((t, HW), F32), _sds((t, HW), F32), _sds((t, HW), BF16)) + _scatter_shapes(scatter),
        in_specs=[blk] * 4 + [ANY] * ns, out_specs=(blk,) * 3 + (ANY,) * ns,
        scratch_shapes=[pltpu.VMEM((seq, LANES), F32), pltpu.VMEM((seq, LANES), F32)]
        + (_scatter_sems(ns) if ns else []),
        compiler_params=_params("arbitrary", "arbitrary"),
    )(q, k, v, do, *scatter)
    return res[0], res[1], res[2], res[3:]


def _mla_bwd(dz, dq, dk, dv, zm, gql, gkvl, gq, gk, tabs, wuq_p, wk_p, wv_p, tm, tps):
    t = zm.shape[0]
    d = (dz.shape[1] - MLA_IN - 2 * CONV_CH) // 2
    _, p_q, _ = _layout(d)
    c_t, s1_t, s2_t = tabs
    tab = pl.BlockSpec((tm, LANES), lambda i: (i % tps, 0))

    def body(dz_hbm, dq_ref, dk_ref, dv_ref, zm_ref, gql_ref, gkvl_ref, gq_ref, gk_ref, c_ref, s1_ref, s2_ref,
             wuq_ref, wk_ref, wv_ref,
             dzm_ref, dqpre_ref, dkh_ref, dgq_ref, dgk_ref, dgql_ref, dgkvl_ref):
        i = pl.program_id(0)
        c, s1, s2 = c_ref[...], s1_ref[...], s2_ref[...]
        nq, rq = _rms(zm_ref[:, :Q_RANK])
        qpre = _dot((nq * gql_ref[...]).astype(BF16), wuq_ref[...])
        nkv, rkv = _rms(zm_ref[:, Q_RANK:OFF_KV])
        knope = _dot((nkv * gkvl_ref[...]).astype(BF16), wk_ref[...])
        zkr_v = zm_ref[:, OFF_KV:]
        gk = gk_ref[...]
        kr_roped = _rope(zkr_v * gk, c, s1, s2)
        dgq = jnp.zeros((1, LANES), F32)
        dgk = jnp.zeros((1, LANES), F32)
        dzkr = jnp.zeros((tm, LANES), F32)
        dt_sum = jnp.zeros((tm, LANES), F32)
        for hd in range(N_HEADS):
            sl = slice(hd * LANES, (hd + 1) * LANES)
            n, r = _head_rms(qpre[:, sl])
            dyr = _rope_t(dq_ref[:, sl], c, s1, s2)
            dgq = dgq + jnp.sum(dyr * n, axis=0, keepdims=True)
            dqpre_ref[:, sl] = _head_rms_bwd(n, r, dyr * gq_ref[...]).astype(BF16)
            kn = knope[:, sl]
            kh = kn + zkr_v
            _, r = _head_rms(kh)
            dk = dk_ref[:, sl]
            dt = dk * r
            dr = jnp.sum(dk * (kn * gk + kr_roped), axis=-1, keepdims=True)
            via_r = (dr * (r * r * r) * (-1.0 / QK_HEAD)) * kh
            dgk = dgk + jnp.sum(dt * kn, axis=0, keepdims=True)
            dt_sum = dt_sum + dt
            dzkr = dzkr + via_r
            dkh_ref[:, sl] = (dt * gk + via_r).astype(BF16)
        de = _rope_t(dt_sum, c, s1, s2)
        dzkr = dzkr + de * gk
        dgk = dgk + jnp.sum(de * zkr_v, axis=0, keepdims=True)
        _acc(dgq_ref, dgq[:, :QK_HEAD], i == 0)
        _acc(dgk_ref, dgk[:, :QK_HEAD], i == 0)
        dzm_ref[:, OFF_KV:] = dzkr.astype(BF16)
        dqln = _dot_nt(dqpre_ref[...], wuq_ref[...])
        _acc(dgql_ref, jnp.sum(dqln * nq, axis=0, keepdims=True), i == 0)
        dzm_ref[:, :Q_RANK] = _rms_bwd(nq, rq, dqln * gql_ref[...]).astype(BF16)
        dkvn = _dot_nt(dkh_ref[...], wk_ref[...]) + _dot_nt(dv_ref[...], wv_ref[...])
        _acc(dgkvl_ref, jnp.sum(dkvn * nkv, axis=0, keepdims=True), i == 0)
        dzm_ref[:, Q_RANK:OFF_KV] = _rms_bwd(nkv, rkv, dkvn * gkvl_ref[...]).astype(BF16)

    return pl.pallas_call(
        body, name="mla_bwd", grid=(t // tm,),
        out_shape=(_sds(dz.shape, BF16), _sds((t, HW), BF16), _sds((t, HW), BF16), _sds((1, QK_HEAD), F32),
                   _sds((1, QK_HEAD), F32), _sds((1, Q_RANK), F32), _sds((1, KV_RANK), F32)),
        in_specs=[ANY, _row(tm, HW), _row(tm, HW), _row(tm, HW), _row(tm, MLA_IN),
                  _full((1, Q_RANK)), _full((1, KV_RANK)), _full((1, LANES)), _full((1, LANES)), tab, tab, tab,
                  _full(wuq_p.shape), _full(wk_p.shape), _full(wv_p.shape)],
        out_specs=(pl.BlockSpec((tm, MLA_IN), lambda i: (i, p_q // MLA_IN)), _row(tm, HW), _row(tm, HW),
                   _full((1, QK_HEAD)), _full((1, QK_HEAD)), _full((1, Q_RANK)), _full((1, KV_RANK))),
        input_output_aliases={0: 0},
        compiler_params=_params("arbitrary"),
    )(dz, dq, dk, dv, zm, gql, gkvl, gq, gk, c_t, s1_t, s2_t, wuq_p, wk_p, wv_p)


def _bwd_in(dz, x, dx1, g1, mod3, win_p, tm, tps, scatter=()):
    t, d = x.shape
    npad = dz.shape[1]

    ns = len(scatter)
    n_steps = t // tm

    def body(dz_ref, x_ref, dx1_ref, g_ref, mod_ref, wt_hbm, *rest):
        gx_ref, dshift_ref, dscale_ref, dg1_ref = rest[ns:ns + 4]
        wt_ref = rest[2 * ns + 4]
        i = pl.program_id(0)
        if ns:
            start, finish = _scatter_phases(rest[:ns], rest[ns + 4:2 * ns + 4], *rest[2 * ns + 5:])
            pl.when(i == 0)(start)
        _load_resident(i, [(wt_hbm, wt_ref)])
        first_seq = (i % tps) == 0
        dh = _dot_nt(dz_ref[...], wt_ref[...])
        n, r = _rms(x_ref[...])
        g = g_ref[...]
        sc1 = 1.0 + mod_ref[1:2, :]
        _acc(dshift_ref, jnp.sum(dh, axis=0, keepdims=True), first_seq)
        _acc(dscale_ref, jnp.sum(dh * (n * g), axis=0, keepdims=True), first_seq)
        _acc(dg1_ref, jnp.sum((dh * sc1) * n, axis=0, keepdims=True), i == 0)
        gx_ref[...] = dx1_ref[...] + _rms_bwd(n, r, (dh * sc1) * g)
        if ns:
            pl.when(i == n_steps - 1)(finish)

    nseq = t // (tm * tps)
    sv = _sds((nseq, 1, d), F32)
    res = pl.pallas_call(
        body, name="bwd_in", grid=(n_steps,),
        out_shape=(_sds((t, d), F32), sv, sv, _sds((1, d), F32)) + _scatter_shapes(scatter),
        in_specs=[_row(tm, npad), _row(tm, d), _row(tm, d), _full((1, d)), _modspec(d, tps), ANY] + [ANY] * ns,
        out_specs=(_row(tm, d), _seqv(d, tps), _seqv(d, tps), _full((1, d))) + (ANY,) * ns,
        scratch_shapes=[pltpu.VMEM(win_p.shape, BF16)] + (_scatter_sems(ns) if ns else []),
        compiler_params=_params("arbitrary"),
    )(dz, x, dx1, g1, mod3, win_p, *scatter)
    return res[0], res[1], res[2], res[3], res[4:]


def _tile_of(n, choices):
    for c in choices:
        if n % c == 0:
            return c
    return n


def _tn_matmul(a, b, name, col_shards=0):
    t, k = a.shape
    n = b.shape[1]
    tk = _tile_of(k, (1024, 512, 256, 128))
    tn = n // col_shards if col_shards else _tile_of(n, (1024, 896, 768, 512, 384, 256, 128))
    tt = _tile_of(t, (1024, 512, 256))

    def body(a_ref, b_ref, o_ref):
        _acc(o_ref, _dot_tn(a_ref[...], b_ref[...]), pl.program_id(2) == 0)

    if col_shards:
        out_shape, out_spec = _sds((col_shards, k, tn), F32), pl.BlockSpec((None, tk, tn), lambda i, j, s: (j, i, 0))
    else:
        out_shape, out_spec = _sds((k, n), F32), pl.BlockSpec((tk, tn), lambda i, j, s: (i, j))
    return pl.pallas_call(
        body, name=name, grid=(k // tk, n // tn, t // tt), out_shape=out_shape,
        in_specs=[pl.BlockSpec((tt, tk), lambda i, j, s: (s, i)), pl.BlockSpec((tt, tn), lambda i, j, s: (s, j))],
        out_specs=out_spec, compiler_params=_params("arbitrary", "arbitrary", "arbitrary"),
    )(a, b)


N_SHARD = 4
COL_SHARDED = ("w_in", "w_uq", "w_ukv", "w_o_mla", "w_pw_out", "w_ff1")
ROW_SHARDED = ("w_out", "w_ff2")
BIG = ("w_in", "w_uq", "w_ukv", "w_o_mla", "w_pw_out", "w_out", "w_ff1", "w_ff2")
SMALL = ("norm1_g", "q_latent_g", "kv_latent_g", "qk_norm_q_g", "qk_norm_k_g", "conv_b", "conv_ln_g", "conv_ln_b",
         "norm2_g")
WEIGHTS = ("w_ada", "b_ada", "norm1_g", "w_in", "q_latent_g", "w_uq", "kv_latent_g", "w_ukv", "qk_norm_q_g",
           "qk_norm_k_g", "w_o_mla", "conv_w", "conv_b", "conv_ln_g", "conv_ln_b", "w_pw_out", "w_out", "norm2_g",
           "w_ff1", "w_ff2")


def _pad_heads(w, width):
    k = w.shape[0]
    w3 = w.reshape(k, N_HEADS, width)
    return jnp.pad(w3, ((0, 0), (0, 0), (0, LANES - width))).reshape(k, HW)


def _unpad_heads(g, width):
    k = g.shape[0]
    return g.reshape(k, N_HEADS, LANES)[:, :, :width].reshape(k, N_HEADS * width)


def _pad_win(w):
    d = w.shape[0]
    z = lambda n: jnp.zeros((d, n), w.dtype)
    return jnp.concatenate([w[:, OFF_GLU:], w[:, OFF_KR:OFF_GLU], w[:, :OFF_KV], z(KR_LANE), w[:, OFF_KV:OFF_KR],
                            z(LANES - KR_LANE - QK_ROPE)], axis=1)


def _unpad_win(g):
    d = g.shape[0]
    p_glu, p_q, _ = _layout(d)
    kr = p_q + OFF_KV + KR_LANE
    return jnp.concatenate([g[:, p_q:p_q + OFF_KV], g[:, kr:kr + QK_ROPE], g[:, p_glu:p_q], g[:, :p_glu]], axis=1)


def _col_shards(g):
    k, n = g.shape
    return g.reshape(k, N_SHARD, n // N_SHARD).transpose(1, 0, 2)


def _from_shards(g, name):
    ns, ks, nn = g.shape
    if name in ROW_SHARDED:
        return g.reshape(ns * ks, nn)
    return g.transpose(1, 0, 2).reshape(ks, ns * nn)


EARLY = ("w_in", "w_uq", "w_ukv")
LATE = ("w_o_mla", "w_pw_out", "w_out", "w_ff1", "w_ff2")


def _assemble(names, gathered):
    return {n: _from_shards(g.reshape((N_SHARD, 2 * g.shape[1]) + g.shape[2:]), n) for n, g in zip(names, gathered)}


GROUP_A = ("w_out", "w_ff1", "w_ff2")
GROUP_B = ("w_in", "w_uq", "w_ukv", "w_o_mla", "w_pw_out")


def _pair_halves(g):
    return g.reshape(N_SHARD, 2, g.shape[1] // 2, g.shape[2])


def _pair_sums(names, halves, from_sibling):
    if not halves:
        return []
    cidx = lax.axis_index("c").reshape(1).astype(jnp.int32)
    return [_add_pair(g, l, cidx, "pair_sum_" + n) for n, g, l in zip(names, halves, from_sibling)]


def _local_step(x, target, mod, sp, w, late=None, tm=256):
    comm = late is not None
    w = dict(w)
    nseq, seq, d = x.shape
    t = nseq * seq
    tps = seq // tm
    xf = x.reshape(t, d)
    tg = target.reshape(t, d)
    mod3 = mod.reshape(nseq, N_MOD, d)

    win_p = _pad_win(w["w_in"])
    wuq_p = _pad_heads(w["w_uq"], QK_HEAD)
    wkv3 = w["w_ukv"].reshape(KV_RANK, N_HEADS, QK_NOPE + V_HEAD)
    wk_p = _pad_heads(wkv3[:, :, :QK_NOPE].reshape(KV_RANK, -1), QK_NOPE)
    wv_p = _pad_heads(wkv3[:, :, QK_NOPE:].reshape(KV_RANK, -1), V_HEAD)
    cw = jnp.pad(w["conv_w"], ((0, HALO - CONV_W), (0, 0)))
    pad_g = lambda g: jnp.pad(g, ((0, 0), (0, LANES - QK_HEAD)))
    gq, gk = pad_g(sp["qk_norm_q_g"]), pad_g(sp["qk_norm_k_g"])
    tabs = _rope_tables(seq)

    tm_in, tps_in = (2 * tm, tps // 2) if tps % 2 == 0 else (tm, tps)
    h, zm, zglu, zgate, u0 = _fwd_in(xf, sp["norm1_g"], mod3, win_p, tm_in, tps_in)
    q, k, v, qln, kvn = _mla_prep(zm, sp["q_latent_g"], sp["kv_latent_g"], gq, gk, tabs, wuq_p, wk_p, wv_p, tm, tps)
    attn, gathered = _attn_fwd(q, k, v, nseq, seq, tuple(late) if comm else ())
    if comm:
        w.update(_assemble(LATE, gathered))
    wo_p = jnp.pad(w["w_o_mla"].reshape(N_HEADS, V_HEAD, d), ((0, 0), (0, LANES - V_HEAD), (0, 0))).reshape(HW, d)
    x1, mixed, mpre, ya, yb, u1, u3 = _fwd_mix(attn, u0, zgate, xf, mod3, wo_p, cw, sp["conv_b"], sp["conv_ln_g"],
                                               sp["conv_ln_b"], w["w_pw_out"], w["w_out"], tm, tps)
    h2, a, r, dy, df, dgate2, loss_acc = _fwd_ffn(x1, tg, sp["norm2_g"], mod3, w["w_ff1"], w["w_ff2"], tm, tps)
    da, dx1, dmixed, dshift2, dscale2, dgate1, dg2 = _bwd_ffn(df, a, x1, dy, mixed, sp["norm2_g"], mod3,
                                                              w["w_ff2"], w["w_ff1"], tm, tps)
    gw = {
        "w_out": _tn_matmul(mpre, dmixed, "dw_out").reshape(N_SHARD, d // N_SHARD, d),
        "w_ff1": _tn_matmul(h2, da, "dw_ff1", N_SHARD),
        "w_ff2": _tn_matmul(r, df, "dw_ff2").reshape(N_SHARD, -1, d),
    }
    halves_a = [_pair_halves(gw[n]) for n in GROUP_A] if comm else []
    dya, dyb, dz, do, du1, dlng, dlnb, dcb, from_sibling = _bwd_mix(
        dmixed, zgate, ya, yb, u1, sp["conv_ln_g"], sp["conv_ln_b"], w["w_out"], wo_p, w["w_pw_out"], tm, tuple(halves_a))
    pair_a = _pair_sums(GROUP_A, halves_a, from_sibling)
    dz, dcw = _bwd_conv(dz, du1, u0, zglu, cw, tm, tps)
    gw["conv_w"] = dcw
    dq, dk, dv, land_a = _attn_bwd(q, k, v, do, nseq, seq, tuple(p[1] for p in pair_a))
    dz, dqpre, dkh, dgq, dgk, dgql, dgkvl = _mla_bwd(dz, dq, dk, dv, zm, sp["q_latent_g"], sp["kv_latent_g"], gq, gk,
                                                      tabs, wuq_p, wk_p, wv_p, tm, tps)
    dwk_p = _tn_matmul(kvn, dkh, "dw_uk")
    dwv_p = _tn_matmul(kvn, dv, "dw_uv")
    dwkv = jnp.concatenate([dwk_p.reshape(KV_RANK, N_HEADS, LANES)[:, :, :QK_NOPE],
                            dwv_p.reshape(KV_RANK, N_HEADS, LANES)[:, :, :V_HEAD]], axis=2).reshape(KV_RANK, -1)
    dwo = _tn_matmul(attn, dya, "dw_o").reshape(N_HEADS, LANES, d)[:, :V_HEAD].reshape(MLA_WIDTH, d)
    gw["w_in"] = _col_shards(_unpad_win(_tn_matmul(h, dz, "dw_in")))
    gw["w_uq"] = _col_shards(_unpad_heads(_tn_matmul(qln, dqpre, "dw_uq"), QK_HEAD))
    gw["w_ukv"] = _col_shards(dwkv)
    gw["w_o_mla"] = _col_shards(dwo)
    gw["w_pw_out"] = _tn_matmul(u3, dyb, "dw_pw", N_SHARD)
    pair_b = []
    if comm:
        halves_b = [_pair_halves(gw[n]) for n in GROUP_B]
        pair_b = _pair_sums(GROUP_B, halves_b, _pair_swap(halves_b, "grad_pair_swap"))
    gx, dshift1, dscale1, dg1, land_b = _bwd_in(dz, xf, dx1, sp["norm1_g"], mod3, win_p, tm_in, tps_in,
                                                tuple(p[1] for p in pair_b))
    if comm:
        for n, p, l in zip(GROUP_A + GROUP_B, pair_a + pair_b, land_a + land_b):
            gw[n] = (p[0], l)
    gs = {
        "norm1_g": dg1, "q_latent_g": dgql, "kv_latent_g": dgkvl, "qk_norm_q_g": dgq, "qk_norm_k_g": dgk,
        "conv_b": dcb, "conv_ln_g": dlng, "conv_ln_b": dlnb, "norm2_g": dg2,
    }
    dmod = jnp.concatenate([dshift1, dscale1, dgate1, dshift2, dscale2, dgate2], axis=2).reshape(nseq, N_MOD * d)
    return loss_acc, gx.reshape(nseq, seq, d), dmod, gw, gs


def kernel(x, c, w_ada, b_ada, norm1_g, w_in, q_latent_g, w_uq, kv_latent_g, w_ukv, qk_norm_q_g, qk_norm_k_g, w_o_mla, conv_w, conv_b, conv_ln_g, conv_ln_b, w_pw_out, w_out, norm2_g, w_ff1, w_ff2, loss_target, m_w_ada, m_b_ada, m_norm1_g, m_w_in, m_q_latent_g, m_w_uq, m_kv_latent_g, m_w_ukv, m_qk_norm_q_g, m_qk_norm_k_g, m_w_o_mla, m_conv_w, m_conv_b, m_conv_ln_g, m_conv_ln_b, m_w_pw_out, m_w_out, m_norm2_g, m_w_ff1, m_w_ff2, v_w_ada, v_b_ada, v_norm1_g, v_w_in, v_q_latent_g, v_w_uq, v_kv_latent_g, v_w_ukv, v_qk_norm_q_g, v_qk_norm_k_g, v_w_o_mla, v_conv_w, v_conv_b, v_conv_ln_g, v_conv_ln_b, v_w_pw_out, v_w_out, v_norm2_g, v_w_ff1, v_w_ff2):
    given = dict(locals())
    wts = {n: given[n][0] for n in WEIGHTS}
    mom = {n: given["m_" + n][0] for n in WEIGHTS}
    var = {n: given["v_" + n][0] for n in WEIGHTS}
    vec = lambda a: a.reshape(1, -1)
    nseq, seq, d = x.shape
    ix, iy, ic = _place()
    shard = 2 * ix + iy

    half = lambda n: lax.dynamic_slice_in_dim(wts[n].astype(BF16), ic * (wts[n].shape[0] // 2), wts[n].shape[0] // 2,
                                              axis=0)
    gathered = _all_gather8([half(n) for n in EARLY] + [wts["conv_w"], c], "gather_weights")
    full = _assemble(EARLY, gathered)
    full["conv_w"] = _from_shards(gathered[-2][0::2], "conv_w")
    c_all = gathered[-1].reshape(8 * nseq, d)

    n_ada = wts["w_ada"].shape[1]
    b_sh = lax.dynamic_slice_in_dim(vec(wts["b_ada"]), shard * n_ada, n_ada, axis=1)
    mod_sh = _ada_mod(c_all, wts["w_ada"], b_sh)
    hb = 4 * nseq
    mod_blk = lax.dynamic_slice_in_dim(mod_sh, ic * hb, hb, axis=0)
    (mod_all,) = _all_gather8([mod_blk], "gather_mod")
    mod_mine = lax.dynamic_slice_in_dim(mod_all, (2 * iy + ic) * nseq, nseq, axis=1)
    mod = jnp.concatenate([lax.dynamic_index_in_dim(mod_mine, 2 * s + ix, axis=0, keepdims=False)
                           for s in range(N_SHARD)], axis=1)

    sp = {n: vec(wts[n]) for n in SMALL}
    loss_part, grad_x, dmod, gw, gs = _local_step(x, loss_target, mod, sp, full, [half(n) for n in LATE])

    parts = _all_gather8([dmod, gw["conv_w"], loss_part] + [gs[n] for n in SMALL], "gather_small")
    dmod_all = parts[0].reshape(8 * nseq, N_MOD * d)
    dmod_sh = lax.dynamic_slice_in_dim(dmod_all, shard * n_ada, n_ada, axis=1)
    res = _ada_bwd(c_all, dmod_all, dmod_sh, parts[1:])
    grads = {"w_ada": res[0], "b_ada": res[1]}
    n_cw = wts["conv_w"].shape[1]
    grads["conv_w"] = lax.dynamic_slice_in_dim(res[2], shard * n_cw, n_cw, axis=1)[:CONV_W]
    loss = res[3][0, 0]
    for n, g in zip(SMALL, res[4:]):
        grads[n] = g

    own_c = jnp.stack([shard, ic]).astype(jnp.int32)
    mine_sum = [_add_chips(gw[n][0], gw[n][1], own_c, "chip_sum_" + n) for n in BIG]
    for n, g in zip(BIG, _pair_gather(mine_sum, "grad_pair_gather")):
        grads[n] = g.reshape(wts[n].shape)

    delta, new_m, new_v = {}, {}, {}
    for n in BIG + ("w_ada",):
        delta[n], new_m[n], new_v[n] = _adamw(wts[n], grads[n], mom[n], var[n], "adamw_" + n)
    rest = ("b_ada", "conv_w") + SMALL
    as2d = lambda a: a if a.ndim == 2 else vec(a)
    res = _adamw_small(*[[as2d(t[n]) for n in rest] for t in (wts, grads, mom, var)])
    for dst, arrs in zip((delta, new_m, new_v), res):
        for n, a in zip(rest, arrs):
            dst[n] = a

    outs = [loss, grad_x]
    for group in (grads, delta, new_m, new_v):
        outs += [group[n].reshape(given[n].shape) for n in WEIGHTS]
    return tuple(outs)
```

```python
import jax
import jax.numpy as jnp
from jax import lax
from jax.experimental import pallas as pl
from jax.experimental.pallas import tpu as pltpu

F32 = jnp.float32
BF16 = jnp.bfloat16
MESH = pl.DeviceIdType.MESH
ANY = pl.BlockSpec(memory_space=pl.ANY)

CHUNK = 64
CHUNK_SHIFT = 6
N_HEADS = 8
QK_NOPE = 64
QK_ROPE = 32
QK_HEAD = QK_NOPE + QK_ROPE
V_HEAD = 64
Q_RANK = 256
KV_RANK = 128
MLA_WIDTH = N_HEADS * V_HEAD
CONV_CH = 512
CONV_W = 31
ROPE_THETA = 10000.0
EPS = 1e-6
LANES = 128
SUBLANES = 8
HW = N_HEADS * LANES
OFF_KV = Q_RANK + KV_RANK
OFF_KR = OFF_KV + QK_ROPE
OFF_GLU = OFF_KR + 2 * CONV_CH
KR_LANE = QK_NOPE
MLA_IN = Q_RANK + KV_RANK + LANES
HALO = 32
N_MOD = 6

ADAM_LR = 0.001
ADAM_B1 = 0.9
ADAM_B2 = 0.999
ADAM_EPS = 1e-08
ADAM_WD = 0.01
ADAM_STEP = 10

VMEM_LIMIT = 56 * 1024 * 1024
BQ = 256


def _layout(d):
    p_glu = 2 * d
    p_q = p_glu + 2 * CONV_CH
    return p_glu, p_q, p_q + MLA_IN


def _params(*sem):
    return pltpu.CompilerParams(dimension_semantics=sem, vmem_limit_bytes=VMEM_LIMIT)


def _dot(a, b):
    return jnp.dot(a, b, preferred_element_type=F32)


def _dot_tn(a, b):
    return lax.dot_general(a, b, (((0,), (0,)), ((), ())), preferred_element_type=F32)


def _dot_nt(a, b):
    return lax.dot_general(a, b, (((1,), (1,)), ((), ())), preferred_element_type=F32)


def _acc(ref, val, first):
    @pl.when(first)
    def _():
        ref[...] = val

    @pl.when(jnp.logical_not(first))
    def _():
        ref[...] += val


def _rms(x):
    r = lax.rsqrt(jnp.mean(x * x, axis=-1, keepdims=True) + EPS)
    return x * r, r


def _rms_bwd(n, r, dn):
    return r * (dn - n * jnp.mean(dn * n, axis=-1, keepdims=True))


def _head_rms(sl):
    r = lax.rsqrt(jnp.sum(sl * sl, axis=-1, keepdims=True) * (1.0 / QK_HEAD) + EPS)
    return sl * r, r


def _head_rms_bwd(n, r, dn):
    return r * (dn - n * (jnp.sum(dn * n, axis=-1, keepdims=True) * (1.0 / QK_HEAD)))


def _rope(x, c, s1, s2):
    return x * c + pltpu.roll(x, QK_ROPE // 2, 1) * s1 + pltpu.roll(x, LANES - QK_ROPE // 2, 1) * s2


def _rope_t(dy, c, s1, s2):
    return dy * c + pltpu.roll(dy * s1, LANES - QK_ROPE // 2, 1) + pltpu.roll(dy * s2, QK_ROPE // 2, 1)


def _rope_tables(seq):
    half = QK_ROPE // 2
    inv_freq = ROPE_THETA ** (-jnp.arange(0, QK_ROPE, 2, dtype=F32) / QK_ROPE)
    ang = jnp.arange(seq, dtype=F32)[:, None] * inv_freq[None, :]
    cos, sin = jnp.cos(ang), jnp.sin(ang)
    z = lambda n: jnp.zeros((seq, n), F32)
    tail = LANES - QK_HEAD
    c = jnp.concatenate([jnp.ones((seq, QK_NOPE), F32), cos, cos, jnp.ones((seq, tail), F32)], axis=1)
    s1 = jnp.concatenate([z(QK_NOPE + half), sin, z(tail)], axis=1)
    s2 = jnp.concatenate([z(QK_NOPE), -sin, z(half + tail)], axis=1)
    return c, s1, s2


def _row(tm, w):
    return pl.BlockSpec((tm, w), lambda i: (i, 0))


def _modspec(d, tps):
    return pl.BlockSpec((None, N_MOD, d), lambda i: (i // tps, 0, 0))


def _seqv(w, tps):
    return pl.BlockSpec((None, 1, w), lambda i: (i // tps, 0, 0))


def _full(shape):
    return pl.BlockSpec(shape, lambda i: tuple(0 for _ in shape))


def _sds(shape, dtype):
    return jax.ShapeDtypeStruct(shape, dtype)


CONV_ROWS = 64
CONV_LC = CONV_CH // LANES


def _lane_chunks():
    return [(lc, slice(lc * LANES, (lc + 1) * LANES)) for lc in range(CONV_LC)]


def _fill_shifted(ext_ref, head, body):
    nh = head.shape[0]
    for lc, ls in _lane_chunks():
        ext_ref[0, lc, :nh, :] = head[:, ls]
        ext_ref[0, lc, nh:, :] = body[:, ls]
        rows = ext_ref[0, lc]
        for b in range(1, SUBLANES):
            ext_ref[b, lc] = pltpu.roll(rows, rows.shape[0] - b, 0)


def _shifted_shape(tm):
    return (SUBLANES, CONV_LC, tm + HALO, LANES)


def _conv_chunk(c):
    return c % CONV_LC, pl.multiple_of((c // CONV_LC) * CONV_ROWS, CONV_ROWS)


def _shifted(ext_ref, o, lc, r0):
    a = pl.multiple_of((o // SUBLANES) * SUBLANES + r0, SUBLANES)
    return ext_ref[o % SUBLANES, lc, pl.ds(a, CONV_ROWS), :]


def _by_lane_chunk(a):
    return a.reshape(a.shape[0], CONV_LC, LANES).transpose(1, 0, 2)


def _load_resident(i, pairs):
    @pl.when(i == 0)
    def _():
        for src, dst in pairs:
            pltpu.sync_copy(src, dst)


def _place():
    return lax.axis_index("x"), lax.axis_index("y"), lax.axis_index("c")


def _all_gather8(blocks, name):
    na = len(blocks)

    def body(*refs):
        start, forward, finish = _gather8_phases(refs[:na], refs[na:2 * na], *refs[2 * na:])
        start()
        forward()
        finish()

    outs = pl.pallas_call(
        body, name=name, out_shape=_gather8_shapes(blocks), in_specs=[ANY] * na, out_specs=(ANY,) * na,
        scratch_shapes=_gather8_sems(na),
    )(*blocks)
    return _own_block_placed(outs, blocks)


def _gather8_shapes(blocks):
    return tuple(_sds((8,) + b.shape, b.dtype) for b in blocks)


def _gather8_sems(na):
    return [pltpu.SemaphoreType.DMA((7 * na,)), pltpu.SemaphoreType.DMA((7 * na,))]


def _own_block_placed(outs, blocks):
    ix, iy, ic = _place()
    return tuple(lax.dynamic_update_index_in_dim(o, b, 4 * ix + 2 * iy + ic, 0) for o, b in zip(outs, blocks))


def _gather8_phases(x_refs, out_refs, send_sems, recv_sems):
    na = len(x_refs)
    x, y, c = _place()
    me, sibling = (x, y, c), (x, y, 1 - c)
    chips = [(1 - x, y), (x, 1 - y), (1 - x, 1 - y)]

    def copy(a, k, blk, to, from_input=False):
        dst = out_refs[a].at[4 * blk[0] + 2 * blk[1] + blk[2]]
        return pltpu.make_async_remote_copy(
            src_ref=x_refs[a] if from_input else dst, dst_ref=dst,
            send_sem=send_sems.at[7 * a + k], recv_sem=recv_sems.at[7 * a + k], device_id=to, device_id_type=MESH)

    def first(a):
        return [copy(a, 0, me, sibling, True)] + [copy(a, 1 + j, me, (*chip, c), True) for j, chip in enumerate(chips)]

    def start():
        for a in range(na):
            for cp in first(a):
                cp.start()

    def forward():
        for j, chip in enumerate(chips):
            for a in range(na):
                copy(a, 1 + j, (*chip, c), me).wait_recv()
                copy(a, 4 + j, (*chip, c), sibling).start()

    def finish():
        for a in range(na):
            copy(a, 0, sibling, me).wait_recv()
            for j, chip in enumerate(chips):
                copy(a, 4 + j, (*chip, 1 - c), me).wait_recv()
        for a in range(na):
            for cp in first(a) + [copy(a, 4 + j, (*chip, c), sibling) for j, chip in enumerate(chips)]:
                cp.wait_send()

    return start, forward, finish


def _pair_swap(gs, name):
    na = len(gs)

    def body(*refs):
        start, finish = _swap_phases(refs[:na], refs[na:2 * na], *refs[2 * na:])
        start()
        finish()

    return pl.pallas_call(
        body, name=name, out_shape=_swap_shapes(gs), in_specs=[ANY] * na, out_specs=(ANY,) * na,
        scratch_shapes=_swap_sems(gs),
    )(*gs)


def _swap_shapes(gs):
    return tuple(_sds(g.shape[:1] + g.shape[2:], g.dtype) for g in gs)


def _swap_sems(gs):
    n = sum(g.shape[0] for g in gs)
    return [pltpu.SemaphoreType.DMA((n,)), pltpu.SemaphoreType.DMA((n,))]


def _swap_phases(g_refs, land_refs, send_sems, recv_sems):
    x, y, c = _place()

    def copies():
        cps, k = [], 0
        for g_ref, land_ref in zip(g_refs, land_refs):
            for s in range(g_ref.shape[0]):
                cps.append(pltpu.make_async_remote_copy(
                    src_ref=g_ref.at[s, 1 - c], dst_ref=land_ref.at[s], send_sem=send_sems.at[k],
                    recv_sem=recv_sems.at[k], device_id=(x, y, 1 - c), device_id_type=MESH))
                k += 1
        return cps

    def start():
        for cp in copies():
            cp.start()

    def finish():
        for cp in copies():
            cp.wait()

    return start, finish


def _scatter_shapes(hs):
    return tuple(_sds((3,) + h.shape[1:], h.dtype) for h in hs)


def _scatter_sems(na):
    return [pltpu.SemaphoreType.DMA((3 * na,)), pltpu.SemaphoreType.DMA((3 * na,))]


def _scatter_phases(h_refs, land_refs, send_sems, recv_sems):
    x, y, c = _place()
    chips = [(1 - x, y), (x, 1 - y), (1 - x, 1 - y)]

    def copies():
        return [pltpu.make_async_remote_copy(
            src_ref=h_refs[a].at[2 * tx + ty], dst_ref=land_refs[a].at[j], send_sem=send_sems.at[3 * a + j],
            recv_sem=recv_sems.at[3 * a + j], device_id=(tx, ty, c), device_id_type=MESH)
            for a in range(len(h_refs)) for j, (tx, ty) in enumerate(chips)]

    def start():
        for cp in copies():
            cp.start()

    def finish():
        for cp in copies():
            cp.wait()

    return start, finish


def _pair_gather(fs, name):
    na = len(fs)

    def body(*refs):
        out_refs = refs[na:2 * na]
        send_sems, recv_sems = refs[2 * na:]
        x, y, c = _place()
        sends = [pltpu.make_async_remote_copy(
            src_ref=out_refs[a].at[c], dst_ref=out_refs[a].at[c], send_sem=send_sems.at[a], recv_sem=recv_sems.at[a],
            device_id=(x, y, 1 - c), device_id_type=MESH) for a in range(na)]
        recvs = [pltpu.make_async_remote_copy(
            src_ref=out_refs[a].at[c], dst_ref=out_refs[a].at[1 - c], send_sem=send_sems.at[a],
            recv_sem=recv_sems.at[a], device_id=(x, y, 1 - c), device_id_type=MESH) for a in range(na)]
        for cp in sends:
            cp.start()
        for cp in recvs:
            cp.wait_recv()
        for cp in sends:
            cp.wait_send()

    return pl.pallas_call(
        body, name=name, out_shape=tuple(_sds(f.shape, f.dtype) for f in fs),
        in_specs=[ANY] * na, out_specs=(ANY,) * na, input_output_aliases={a: a for a in range(na)},
        scratch_shapes=[pltpu.SemaphoreType.DMA((na,)), pltpu.SemaphoreType.DMA((na,))],
    )(*fs)


def _row_tile(r, n, itemsize=4, budget=1 << 20):
    if r * n * itemsize <= budget:
        return r
    best = None
    for tr in range(16, r, 16):
        if r % tr == 0 and tr * n * itemsize <= budget:
            best = tr
    assert best is not None, (r, n)
    return best


def _add_pair(g, land, cidx, name):
    ns, _, r, n = g.shape
    tr = _row_tile(r, n)

    def body(c_ref, a_ref, b_ref, o_ref, ob_ref):
        s = a_ref[...] + b_ref[...]
        o_ref[...] = s
        ob_ref[...] = s.astype(BF16)

    out = pl.BlockSpec((None, tr, n), lambda s, i, cr: (s, i, 0))
    return pl.pallas_call(
        body, name=name, out_shape=(_sds((ns, r, n), F32), _sds((ns, r, n), BF16)),
        grid_spec=pltpu.PrefetchScalarGridSpec(
            num_scalar_prefetch=1, grid=(ns, r // tr),
            in_specs=[pl.BlockSpec((None, None, tr, n), lambda s, i, cr: (s, cr[0], i, 0)), out],
            out_specs=(out, out)),
        compiler_params=_params("arbitrary", "arbitrary"),
    )(cidx, g, land)


def _add_chips(h, land, own_c, name):
    _, r, n = h.shape
    tr = _row_tile(r, n)

    def body(o_idx, h_ref, l_ref, o_ref):
        o_ref[...] = ((h_ref[...] + l_ref[0].astype(F32)) + l_ref[1].astype(F32)) + l_ref[2].astype(F32)

    return pl.pallas_call(
        body, name=name, out_shape=_sds((2, r, n), F32),
        grid_spec=pltpu.PrefetchScalarGridSpec(
            num_scalar_prefetch=1, grid=(r // tr,),
            in_specs=[pl.BlockSpec((None, tr, n), lambda i, o: (o[0], i, 0)),
                      pl.BlockSpec((3, tr, n), lambda i, o: (0, i, 0))],
            out_specs=pl.BlockSpec((None, tr, n), lambda i, o: (o[1], i, 0))),
        compiler_params=_params("arbitrary"),
    )(own_c, h, land)


def _adam_math(w, g, m, v):
    nm = ADAM_B1 * m + (1.0 - ADAM_B1) * g
    nv = ADAM_B2 * v + (1.0 - ADAM_B2) * (g * g)
    m_hat = nm / (1.0 - ADAM_B1 ** ADAM_STEP)
    v_hat = nv / (1.0 - ADAM_B2 ** ADAM_STEP)
    return -ADAM_LR * (m_hat / (jnp.sqrt(v_hat) + ADAM_EPS) + ADAM_WD * w), nm, nv


def _adamw(w, g, m, v, name):
    r, n = w.shape
    tr = _row_tile(r, n, budget=1 << 19)

    def body(w_ref, g_ref, m_ref, v_ref, d_ref, nm_ref, nv_ref):
        d_ref[...], nm_ref[...], nv_ref[...] = _adam_math(w_ref[...], g_ref[...], m_ref[...], v_ref[...])

    spec = pl.BlockSpec((tr, n), lambda i: (i, 0))
    return pl.pallas_call(
        body, name=name, out_shape=(_sds((r, n), F32),) * 3, grid=(r // tr,),
        in_specs=[spec] * 4, out_specs=(spec,) * 3, compiler_params=_params("arbitrary"),
    )(w, g, m, v)


def _adamw_small(ws, gs, ms, vs):
    k = len(ws)

    def body(*refs):
        ins, outs = refs[:4 * k], refs[4 * k:]
        for j in range(k):
            d, nm, nv = _adam_math(ins[j][...], ins[k + j][...], ins[2 * k + j][...], ins[3 * k + j][...])
            outs[j][...] = d
            outs[k + j][...] = nm
            outs[2 * k + j][...] = nv

    shapes = tuple(_sds(w.shape, F32) for w in ws)
    res = pl.pallas_call(body, name="adamw_small", out_shape=shapes * 3,
                         compiler_params=pltpu.CompilerParams(vmem_limit_bytes=VMEM_LIMIT))(*ws, *gs, *ms, *vs)
    return res[:k], res[k:2 * k], res[2 * k:]


def _ada_mod(c_all, w_sh, b_sh):
    b, _ = c_all.shape
    n = w_sh.shape[1]

    def body(c_ref, w_ref, b_ref, o_ref):
        cc = c_ref[...]
        ca = (cc * jax.nn.sigmoid(cc)).astype(BF16)
        o_ref[...] = _dot(ca, w_ref[...].astype(BF16)) + b_ref[...]

    return pl.pallas_call(body, name="ada_mod", out_shape=_sds((b, n), F32),
                          compiler_params=pltpu.CompilerParams(vmem_limit_bytes=VMEM_LIMIT))(c_all, w_sh, b_sh)


def _ada_bwd(c_all, dmod_all, dmod_sh, parts):
    b, d = c_all.shape
    n6 = dmod_all.shape[1]
    n = dmod_sh.shape[1]
    k = len(parts)

    def body(*refs):
        c_ref, da_ref, ds_ref = refs[:3]
        p_refs = refs[3:3 + k]
        dw_ref, db_ref = refs[3 + k:5 + k]
        s_refs = refs[5 + k:]
        cc = c_ref[...]
        ca = (cc * jax.nn.sigmoid(cc)).astype(BF16)
        dw_ref[...] = _dot_tn(ca, ds_ref[...].astype(BF16))
        db_ref[...] = jnp.sum(da_ref[...], axis=0, keepdims=True)
        for p_ref, s_ref in zip(p_refs, s_refs):
            tot = p_ref[0]
            for j in range(1, p_ref.shape[0]):
                tot = tot + p_ref[j]
            s_ref[...] = tot

    return pl.pallas_call(
        body, name="ada_bwd",
        out_shape=(_sds((d, n), F32), _sds((1, n6), F32)) + tuple(_sds(p.shape[1:], F32) for p in parts),
        compiler_params=pltpu.CompilerParams(vmem_limit_bytes=VMEM_LIMIT),
    )(c_all, dmod_all, dmod_sh, *parts)


def _fwd_in(x, g1, mod3, win_p, tm, tps):
    t, d = x.shape
    p_glu, p_q, npad = _layout(d)

    def body(x_ref, g_ref, mod_ref, w_hbm, h_ref, zm_ref, zglu_ref, zgate_ref, u0_ref, w_ref):
        _load_resident(pl.program_id(0), [(w_hbm, w_ref)])
        n, _ = _rms(x_ref[...])
        h = ((n * g_ref[...]) * (1.0 + mod_ref[1:2, :]) + mod_ref[0:1, :]).astype(BF16)
        h_ref[...] = h
        z = _dot(h, w_ref[...])
        zgate_ref[...] = z[:, :p_glu]
        zglu = z[:, p_glu:p_q]
        zglu_ref[...] = zglu
        zm_ref[...] = z[:, p_q:]
        u0_ref[...] = zglu[:, :CONV_CH] * jax.nn.sigmoid(zglu[:, CONV_CH:])

    return pl.pallas_call(
        body, name="fwd_in", grid=(t // tm,),
        out_shape=(_sds((t, d), BF16), _sds((t, MLA_IN), F32), _sds((t, 2 * CONV_CH), F32), _sds((t, 2 * d), F32),
                   _sds((t, CONV_CH), F32)),
        in_specs=[_row(tm, d), _full((1, d)), _modspec(d, tps), ANY],
        out_specs=(_row(tm, d), _row(tm, MLA_IN), _row(tm, 2 * CONV_CH), _row(tm, 2 * d), _row(tm, CONV_CH)),
        scratch_shapes=[pltpu.VMEM(win_p.shape, BF16)],
        compiler_params=_params("arbitrary"),
    )(x, g1, mod3, win_p)


def _mla_prep(zm, gql, gkvl, gq, gk, tabs, wuq_p, wk_p, wv_p, tm, tps):
    t = zm.shape[0]
    c_t, s1_t, s2_t = tabs
    tab = pl.BlockSpec((tm, LANES), lambda i: (i % tps, 0))

    def body(zm_ref, gql_ref, gkvl_ref, gq_ref, gk_ref, c_ref, s1_ref, s2_ref, wuq_ref, wk_ref, wv_ref,
             q_ref, k_ref, v_ref, qln_ref, kvn_ref):
        c, s1, s2 = c_ref[...], s1_ref[...], s2_ref[...]
        nq, _ = _rms(zm_ref[:, :Q_RANK])
        qln = (nq * gql_ref[...]).astype(BF16)
        qln_ref[...] = qln
        qpre = _dot(qln, wuq_ref[...])
        nkv, _ = _rms(zm_ref[:, Q_RANK:OFF_KV])
        kvn = (nkv * gkvl_ref[...]).astype(BF16)
        kvn_ref[...] = kvn
        knope = _dot(kvn, wk_ref[...])
        v_ref[...] = _dot(kvn, wv_ref[...]).astype(BF16)
        zkr_v = zm_ref[:, OFF_KV:]
        kr_roped = _rope(zkr_v * gk_ref[...], c, s1, s2)
        for hd in range(N_HEADS):
            sl = slice(hd * LANES, (hd + 1) * LANES)
            n, _ = _head_rms(qpre[:, sl])
            q_ref[:, sl] = _rope(n * gq_ref[...], c, s1, s2).astype(BF16)
            _, r = _head_rms(knope[:, sl] + zkr_v)
            k_ref[:, sl] = (r * (knope[:, sl] * gk_ref[...] + kr_roped)).astype(BF16)

    return pl.pallas_call(
        body, name="mla_prep", grid=(t // tm,),
        out_shape=(_sds((t, HW), BF16),) * 3 + (_sds((t, Q_RANK), BF16), _sds((t, KV_RANK), BF16)),
        in_specs=[_row(tm, MLA_IN), _full((1, Q_RANK)), _full((1, KV_RANK)),
                  _full((1, LANES)), _full((1, LANES)), tab, tab, tab,
                  _full(wuq_p.shape), _full(wk_p.shape), _full(wv_p.shape)],
        out_specs=(_row(tm, HW),) * 3 + (_row(tm, Q_RANK), _row(tm, KV_RANK)),
        compiler_params=_params("arbitrary"),
    )(zm, gql, gkvl, gq, gk, c_t, s1_t, s2_t, wuq_p, wk_p, wv_p)


AHEAD = 2
SM_SCALE = QK_HEAD ** -0.5
EXP2_SCALE = SM_SCALE * 1.4426950408889634


def _diag_mask():
    rc = jnp.right_shift(lax.broadcasted_iota(jnp.int32, (BQ, 1), 0), CHUNK_SHIFT)
    cc = jnp.right_shift(lax.broadcasted_iota(jnp.int32, (1, BQ), 1), CHUNK_SHIFT)
    return rc >= cc


def _scores(q_i, k_ref, lo, e):
    return (_dot_nt(q_i, k_ref[:lo, :]) if lo else None), _dot_nt(q_i, k_ref[lo:e, :])


def _softmax_parts(scores, mask):
    sp, sd = scores
    sd = jnp.where(mask, sd, jnp.finfo(F32).min)
    m = jnp.max(sd, axis=-1, keepdims=True)
    if sp is not None:
        m = jnp.maximum(m, jnp.max(sp, axis=-1, keepdims=True))
    pd = jnp.exp2((sd - m) * EXP2_SCALE)
    l = jnp.sum(pd, axis=-1, keepdims=True)
    pp = None
    if sp is not None:
        pp = jnp.exp2((sp - m) * EXP2_SCALE)
        l = l + jnp.sum(pp, axis=-1, keepdims=True)
    return pp, pd, l


def _attn_fwd(q, k, v, nseq, seq, gather=()):
    t = q.shape[0]
    na = len(gather)
    blk = pl.BlockSpec((seq, LANES), lambda b, h: (b, h))
    n_steps = nseq * N_HEADS

    def body(q_ref, k_ref, v_ref, *rest):
        o_ref = rest[na]
        if na:
            start, forward, finish = _gather8_phases(rest[:na], rest[na + 1:2 * na + 1], *rest[2 * na + 1:])
            step = pl.program_id(0) * N_HEADS + pl.program_id(1)
            pl.when(step == 0)(start)
            pl.when(step == (3 * n_steps) // 4)(forward)
        mask = _diag_mask()
        nb = seq // BQ
        block_scores = lambda j: _scores(q_ref[j * BQ:(j + 1) * BQ, :], k_ref, j * BQ, (j + 1) * BQ)
        ahead = [block_scores(j) for j in range(min(AHEAD, nb))]
        for i in range(nb):
            lo, e = i * BQ, (i + 1) * BQ
            cur = ahead.pop(0)
            if i + AHEAD < nb:
                ahead.append(block_scores(i + AHEAD))
            pp, pd, l = _softmax_parts(cur, mask)
            o = _dot(pd.astype(BF16), v_ref[lo:e, :])
            if lo:
                o = o + _dot(pp.astype(BF16), v_ref[:lo, :])
            o_ref[lo:e, :] = (o * (1.0 / l)).astype(BF16)
        if na:
            pl.when(step == n_steps - 1)(finish)

    res = pl.pallas_call(
        body, name="attn_fwd", grid=(nseq, N_HEADS), out_shape=(_sds((t, HW), BF16),) + _gather8_shapes(gather),
        in_specs=[blk, blk, blk] + [ANY] * na, out_specs=(blk,) + (ANY,) * na,
        scratch_shapes=_gather8_sems(na) if na else [],
        compiler_params=_params("arbitrary", "arbitrary"),
    )(q, k, v, *gather)
    return res[0], (_own_block_placed(res[1:], gather) if na else ())


def _fwd_mix(attn, u0, zgate, x, mod3, wo_p, cw, cb, lng, lnb, wpw, wout, tm, tps):
    t, d = x.shape
    hpt = tm // HALO
    cwc, cbc = _by_lane_chunk(cw), _by_lane_chunk(cb)

    def body(a_ref, u_ref, uh_ref, zg_ref, x_ref, mod_ref, wo_ref, cw_ref, cb_ref, lng_ref, lnb_ref, wpw_ref, wout_ref,
             x1_ref, mixed_ref, mpre_ref, ya_ref, yb_ref, u1_ref, u3_ref, ext_ref):
        i = pl.program_id(0)
        ya = _dot(a_ref[...], wo_ref[...])
        ya_ref[...] = ya
        first = (i % tps) == 0
        _fill_shifted(ext_ref, jnp.where(first, 0.0, uh_ref[...]), u_ref[...])
        for lc, ls in _lane_chunks():
            acc = jnp.broadcast_to(cb_ref[lc], (tm, LANES))
            for kk in range(CONV_W):
                o = HALO - (CONV_W - 1) + kk
                a = (o // SUBLANES) * SUBLANES
                acc = acc + cw_ref[lc, kk:kk + 1, :] * ext_ref[o % SUBLANES, lc, a:a + tm, :]
            u1_ref[:, ls] = acc
        acc = u1_ref[...]
        mu = jnp.mean(acc, axis=-1, keepdims=True)
        xc = acc - mu
        rstd = lax.rsqrt(jnp.mean(xc * xc, axis=-1, keepdims=True) + EPS)
        l = (xc * rstd) * lng_ref[...] + lnb_ref[...]
        u3 = (l * jax.nn.sigmoid(l)).astype(BF16)
        u3_ref[...] = u3
        yb = _dot(u3, wpw_ref[...])
        yb_ref[...] = yb
        zg = zg_ref[...]
        mpre = (jax.nn.sigmoid(zg[:, :d]) * ya + jax.nn.sigmoid(zg[:, d:]) * yb).astype(BF16)
        mpre_ref[...] = mpre
        mixed = _dot(mpre, wout_ref[...])
        mixed_ref[...] = mixed
        x1_ref[...] = x_ref[...] + mod_ref[2:3, :] * mixed

    halo = pl.BlockSpec((HALO, CONV_CH), lambda i: (jnp.maximum(i * hpt - 1, 0), 0))
    return pl.pallas_call(
        body, name="fwd_mix", grid=(t // tm,),
        out_shape=(_sds((t, d), F32), _sds((t, d), F32), _sds((t, d), BF16), _sds((t, d), F32), _sds((t, d), F32),
                   _sds((t, CONV_CH), F32), _sds((t, CONV_CH), BF16)),
        in_specs=[_row(tm, HW), _row(tm, CONV_CH), halo, _row(tm, 2 * d), _row(tm, d), _modspec(d, tps),
                  _full(wo_p.shape), _full(cwc.shape), _full(cbc.shape), _full((1, CONV_CH)), _full((1, CONV_CH)),
                  _full(wpw.shape), _full(wout.shape)],
        out_specs=(_row(tm, d), _row(tm, d), _row(tm, d), _row(tm, d), _row(tm, d), _row(tm, CONV_CH),
                   _row(tm, CONV_CH)),
        scratch_shapes=[pltpu.VMEM(_shifted_shape(tm), F32)],
        compiler_params=_params("arbitrary"),
    )(attn, u0, u0, zgate, x, mod3, wo_p, cwc, cbc, lng, lnb, wpw, wout)


def _fwd_ffn(x1, target, g2, mod3, w1, w2, tm, tps):
    t, d = x1.shape
    dff = w1.shape[1]

    def body(x1_ref, tg_ref, g_ref, mod_ref, w1_hbm, w2_hbm,
             h2_ref, a_ref, r_ref, dy_ref, df_ref, dgate_ref, loss_ref, w1_ref, w2_ref):
        i = pl.program_id(0)
        _load_resident(i, [(w1_hbm, w1_ref), (w2_hbm, w2_ref)])
        x1v = x1_ref[...]
        gate2 = mod_ref[5:6, :]
        n, _ = _rms(x1v)
        h2 = ((n * g_ref[...]) * (1.0 + mod_ref[4:5, :]) + mod_ref[3:4, :]).astype(BF16)
        h2_ref[...] = h2
        a = _dot(h2, w1_ref[...])
        a_ref[...] = a
        r = jnp.square(jnp.maximum(a, 0.0)).astype(BF16)
        r_ref[...] = r
        f = _dot(r, w2_ref[...])
        e = (x1v + gate2 * f) - tg_ref[...]
        part = 0.5 * jnp.sum(jnp.mean(e * e, axis=-1, keepdims=True), axis=0, keepdims=True)
        _acc(loss_ref, jnp.broadcast_to(part, loss_ref.shape), i == 0)
        dy = e * (1.0 / d)
        dy_ref[...] = dy
        df_ref[...] = (dy * gate2).astype(BF16)
        _acc(dgate_ref, jnp.sum(dy * f, axis=0, keepdims=True), (i % tps) == 0)

    nseq = t // (tm * tps)
    return pl.pallas_call(
        body, name="fwd_ffn", grid=(t // tm,),
        out_shape=(_sds((t, d), BF16), _sds((t, dff), F32), _sds((t, dff), BF16), _sds((t, d), F32), _sds((t, d), BF16),
                   _sds((nseq, 1, d), F32), _sds((8, LANES), F32)),
        in_specs=[_row(tm, d), _row(tm, d), _full((1, d)), _modspec(d, tps), ANY, ANY],
        out_specs=(_row(tm, d), _row(tm, dff), _row(tm, dff), _row(tm, d), _row(tm, d), _seqv(d, tps),
                   _full((8, LANES))),
        scratch_shapes=[pltpu.VMEM(w1.shape, BF16), pltpu.VMEM(w2.shape, BF16)],
        compiler_params=_params("arbitrary"),
    )(x1, target, g2, mod3, w1, w2)


def _bwd_ffn(df, a, x1, dy, mixed, g2, mod3, w2, w1, tm, tps):
    t, d = x1.shape
    dff = a.shape[1]

    def body(df_ref, a_ref, x1_ref, dy_ref, mx_ref, g_ref, mod_ref, w2_hbm, w1_hbm,
             da_ref, dx1_ref, dmixed_ref, dshift_ref, dscale_ref, dgate1_ref, dg2_ref, w2_ref, w1_ref):
        i = pl.program_id(0)
        _load_resident(i, [(w2_hbm, w2_ref), (w1_hbm, w1_ref)])
        first_seq = (i % tps) == 0
        dr = _dot_nt(df_ref[...], w2_ref[...])
        da = (dr * (2.0 * jnp.maximum(a_ref[...], 0.0))).astype(BF16)
        da_ref[...] = da
        dh2 = _dot_nt(da, w1_ref[...])
        n, r = _rms(x1_ref[...])
        g = g_ref[...]
        sc1 = 1.0 + mod_ref[4:5, :]
        _acc(dshift_ref, jnp.sum(dh2, axis=0, keepdims=True), first_seq)
        _acc(dscale_ref, jnp.sum(dh2 * (n * g), axis=0, keepdims=True), first_seq)
        _acc(dg2_ref, jnp.sum((dh2 * sc1) * n, axis=0, keepdims=True), i == 0)
        dx1 = dy_ref[...] + _rms_bwd(n, r, (dh2 * sc1) * g)
        dx1_ref[...] = dx1
        _acc(dgate1_ref, jnp.sum(dx1 * mx_ref[...], axis=0, keepdims=True), first_seq)
        dmixed_ref[...] = (dx1 * mod_ref[2:3, :]).astype(BF16)

    nseq = t // (tm * tps)
    sv = _sds((nseq, 1, d), F32)
    return pl.pallas_call(
        body, name="bwd_ffn", grid=(t // tm,),
        out_shape=(_sds((t, dff), BF16), _sds((t, d), F32), _sds((t, d), BF16), sv, sv, sv, _sds((1, d), F32)),
        in_specs=[_row(tm, d), _row(tm, dff), _row(tm, d), _row(tm, d), _row(tm, d), _full((1, d)), _modspec(d, tps),
                  ANY, ANY],
        out_specs=(_row(tm, dff), _row(tm, d), _row(tm, d), _seqv(d, tps), _seqv(d, tps), _seqv(d, tps),
                   _full((1, d))),
        scratch_shapes=[pltpu.VMEM(w2.shape, BF16), pltpu.VMEM(w1.shape, BF16)],
        compiler_params=_params("arbitrary"),
    )(df, a, x1, dy, mixed, g2, mod3, w2, w1)


def _bwd_mix(dmixed, zgate, ya, yb, u1, lng, lnb, wout, wo_p, wpw, tm, swap=()):
    t, d = ya.shape
    _, _, npad = _layout(d)
    nw = len(swap)
    n_steps = t // tm

    def body(dm_ref, zg_ref, ya_ref, yb_ref, u1_ref, lng_ref, lnb_ref, wout_ref, wo_ref, wpw_ref, *rest):
        dya_ref, dyb_ref, dz_ref, do_ref, du1_ref, dlng_ref, dlnb_ref, dcb_ref = rest[nw:nw + 8]
        i = pl.program_id(0)
        if nw:
            start, finish = _swap_phases(rest[:nw], rest[nw + 8:2 * nw + 8], *rest[2 * nw + 8:])
            pl.when(i == 0)(start)
        dmpre = _dot_nt(dm_ref[...], wout_ref[...])
        zg = zg_ref[...]
        ga = jax.nn.sigmoid(zg[:, :d])
        gb = jax.nn.sigmoid(zg[:, d:])
        dya = (dmpre * ga).astype(BF16)
        dyb = (dmpre * gb).astype(BF16)
        dya_ref[...] = dya
        dyb_ref[...] = dyb
        dz_ref[:, :d] = ((dmpre * ya_ref[...]) * (ga * (1.0 - ga))).astype(BF16)
        dz_ref[:, d:] = ((dmpre * yb_ref[...]) * (gb * (1.0 - gb))).astype(BF16)
        do_ref[...] = _dot_nt(dya, wo_ref[...]).astype(BF16)
        du3 = _dot_nt(dyb, wpw_ref[...])
        u1 = u1_ref[...]
        mu = jnp.mean(u1, axis=-1, keepdims=True)
        xc = u1 - mu
        rstd = lax.rsqrt(jnp.mean(xc * xc, axis=-1, keepdims=True) + EPS)
        nh = xc * rstd
        l = nh * lng_ref[...] + lnb_ref[...]
        sg = jax.nn.sigmoid(l)
        dl = du3 * (sg * (1.0 + l * (1.0 - sg)))
        _acc(dlng_ref, jnp.sum(dl * nh, axis=0, keepdims=True), i == 0)
        _acc(dlnb_ref, jnp.sum(dl, axis=0, keepdims=True), i == 0)
        dnh = dl * lng_ref[...]
        du1 = rstd * (dnh - jnp.mean(dnh, axis=-1, keepdims=True) - nh * jnp.mean(dnh * nh, axis=-1, keepdims=True))
        du1_ref[...] = du1
        _acc(dcb_ref, jnp.sum(du1, axis=0, keepdims=True), i == 0)
        if nw:
            pl.when(i == n_steps - 1)(finish)

    cv = _sds((1, CONV_CH), F32)
    res = pl.pallas_call(
        body, name="bwd_mix", grid=(n_steps,),
        out_shape=(_sds((t, d), BF16), _sds((t, d), BF16), _sds((t, npad), BF16), _sds((t, HW), BF16),
                   _sds((t, CONV_CH), F32), cv, cv, cv) + _swap_shapes(swap),
        in_specs=[_row(tm, d), _row(tm, 2 * d), _row(tm, d), _row(tm, d), _row(tm, CONV_CH), _full((1, CONV_CH)),
                  _full((1, CONV_CH)), _full(wout.shape), _full(wo_p.shape), _full(wpw.shape)] + [ANY] * nw,
        out_specs=(_row(tm, d), _row(tm, d), _row(tm, 2 * d), _row(tm, HW), _row(tm, CONV_CH),
                   _full((1, CONV_CH)), _full((1, CONV_CH)), _full((1, CONV_CH))) + (ANY,) * nw,
        scratch_shapes=_swap_sems(swap) if nw else [],
        compiler_params=_params("arbitrary"),
    )(dmixed, zgate, ya, yb, u1, lng, lnb, wout, wo_p, wpw, *swap)
    return res[:8] + (res[8:],)


def _bwd_conv(dz, du1, u0, zglu, cw, tm, tps):
    t = du1.shape[0]
    d = (dz.shape[1] - MLA_IN - 2 * CONV_CH) // 2
    p_glu, _, _ = _layout(d)
    hpt = tm // HALO
    last_blk = t // HALO - 1
    cwc = _by_lane_chunk(cw)

    def body(dz_hbm, du_ref, dun_ref, u_ref, uh_ref, zl_ref, cw_ref, dzl_ref, dcw_ref, ext_ref, dext_ref, dcw8_ref,
             du0_ref):
        i = pl.program_id(0)
        first = (i % tps) == 0
        last = (i % tps) == (tps - 1)
        _fill_shifted(ext_ref, jnp.where(first, 0.0, uh_ref[...]), u_ref[...])
        _fill_shifted(dext_ref, du_ref[...], jnp.where(last, 0.0, dun_ref[...]))

        @pl.when(i == 0)
        def _():
            dcw8_ref[...] = jnp.zeros_like(dcw8_ref)

        groups = CONV_ROWS // SUBLANES

        def conv_chunk(c, carry):
            lc, r0 = _conv_chunk(c)
            du = _shifted(dext_ref, 0, lc, r0)
            du0 = jnp.zeros((CONV_ROWS, LANES), F32)
            for kk in range(CONV_W):
                prod = du * _shifted(ext_ref, HALO - (CONV_W - 1) + kk, lc, r0)
                part = prod[:SUBLANES]
                for g in range(1, groups):
                    part = part + prod[g * SUBLANES:(g + 1) * SUBLANES]
                dcw8_ref[lc, kk] += part
                du0 = du0 + cw_ref[lc, kk:kk + 1, :] * _shifted(dext_ref, CONV_W - 1 - kk, lc, r0)
            du0_ref[lc, pl.ds(r0, CONV_ROWS), :] = du0
            return carry

        lax.fori_loop(0, CONV_LC * (tm // CONV_ROWS), conv_chunk, 0)

        @pl.when(i == pl.num_programs(0) - 1)
        def _():
            for lc, ls in _lane_chunks():
                dcw_ref[:, ls] = jnp.sum(dcw8_ref[lc], axis=1)

        for lc, ls in _lane_chunks():
            du0 = du0_ref[lc]
            ga = zl_ref[:, ls]
            sb = jax.nn.sigmoid(zl_ref[:, CONV_CH + lc * LANES:CONV_CH + (lc + 1) * LANES])
            dzl_ref[:, ls] = (du0 * sb).astype(BF16)
            dzl_ref[:, CONV_CH + lc * LANES:CONV_CH + (lc + 1) * LANES] = ((du0 * ga) * (sb * (1.0 - sb))).astype(BF16)

    prev = pl.BlockSpec((HALO, CONV_CH), lambda i: (jnp.maximum(i * hpt - 1, 0), 0))
    nxt = pl.BlockSpec((HALO, CONV_CH), lambda i: (jnp.minimum((i + 1) * hpt, last_blk), 0))
    glu_blk = p_glu // (2 * CONV_CH)
    return pl.pallas_call(
        body, name="bwd_conv", grid=(t // tm,),
        out_shape=(_sds(dz.shape, BF16), _sds(cw.shape, F32)),
        in_specs=[ANY, _row(tm, CONV_CH), nxt, _row(tm, CONV_CH), prev, _row(tm, 2 * CONV_CH), _full(cwc.shape)],
        out_specs=(pl.BlockSpec((tm, 2 * CONV_CH), lambda i: (i, glu_blk)), _full(cw.shape)),
        scratch_shapes=[pltpu.VMEM(_shifted_shape(tm), F32)] * 2
        + [pltpu.VMEM((CONV_LC, HALO, SUBLANES, LANES), F32), pltpu.VMEM((CONV_LC, tm, LANES), F32)],
        input_output_aliases={0: 0},
        compiler_params=_params("arbitrary"),
    )(dz, du1, du1, u0, u0, zglu, cwc)


def _attn_bwd(q, k, v, do, nseq, seq, scatter=()):
    t = q.shape[0]
    ns = len(scatter)
    blk = pl.BlockSpec((seq, LANES), lambda b, h: (b, h))
    n_steps = nseq * N_HEADS

    def body(q_ref, k_ref, v_ref, do_ref, *rest):
        dq_ref, dk_ref, dv_ref = rest[ns:ns + 3]
        dka_ref, dva_ref = rest[2 * ns + 3:2 * ns + 5]
        if ns:
            start, finish = _scatter_phases(rest[:ns], rest[ns + 3:2 * ns + 3], *rest[2 * ns + 5:])
            step = pl.program_id(0) * N_HEADS + pl.program_id(1)
            pl.when(step == 0)(start)
        dka_ref[...] = jnp.zeros_like(dka_ref)
        dva_ref[...] = jnp.zeros_like(dva_ref)
        mask = _diag_mask()
        nb = seq // BQ
        block = lambda j: (_scores(q_ref[j * BQ:(j + 1) * BQ, :], k_ref, j * BQ, (j + 1) * BQ),
                           _scores(do_ref[j * BQ:(j + 1) * BQ, :], v_ref, j * BQ, (j + 1) * BQ))
        ahead = [block(j) for j in range(min(AHEAD, nb))]
        for i in range(nb):
            lo, e = i * BQ, (i + 1) * BQ
            q_i = q_ref[lo:e, :]
            do_i = do_ref[lo:e, :]
            scores, (dpp, dpd) = ahead.pop(0)
            if i + AHEAD < nb:
                ahead.append(block(i + AHEAD))
            pp, pd, l = _softmax_parts(scores, mask)
            inv = 1.0 / l
            pd = pd * inv
            delta = jnp.sum(pd * dpd, axis=-1, keepdims=True)
            if lo:
                pp = pp * inv
                delta = delta + jnp.sum(pp * dpp, axis=-1, keepdims=True)
            dsd = (pd * (dpd - delta)).astype(BF16)
            dq = _dot(dsd, k_ref[lo:e, :])
            dka_ref[lo:e, :] += _dot_tn(dsd, q_i)
            dva_ref[lo:e, :] += _dot_tn(pd.astype(BF16), do_i)
            if lo:
                dsp = (pp * (dpp - delta)).astype(BF16)
                dq = dq + _dot(dsp, k_ref[:lo, :])
                dka_ref[:lo, :] += _dot_tn(dsp, q_i)
                dva_ref[:lo, :] += _dot_tn(pp.astype(BF16), do_i)
            dq_ref[lo:e, :] = dq * SM_SCALE
        dk_ref[...] = dka_ref[...] * SM_SCALE
        dv_ref[...] = dva_ref[...].astype(BF16)
        if ns:
            pl.when(step == n_steps - 1)(finish)

    res = pl.pallas_call(
        body, name="attn_bwd", grid=(nseq, N_HEADS),
        out_shape=(_sds((t, HW), F32), _sds((t, HW), F32), _sds((t, HW), BF16)) + _scatter_shapes(scatter),
        in_specs=[blk] * 4 + [ANY] * ns, out_specs=(blk,) * 3 + (ANY,) * ns,
        scratch_shapes=[pltpu.VMEM((seq, LANES), F32), pltpu.VMEM((seq, LANES), F32)]
        + (_scatter_sems(ns) if ns else []),
        compiler_params=_params("arbitrary", "arbitrary"),
    )(q, k, v, do, *scatter)
    return res[0], res[1], res[2], res[3:]


def _mla_bwd(dz, dq, dk, dv, zm, gql, gkvl, gq, gk, tabs, wuq_p, wk_p, wv_p, tm, tps):
    t = zm.shape[0]
    d = (dz.shape[1] - MLA_IN - 2 * CONV_CH) // 2
    _, p_q, _ = _layout(d)
    c_t, s1_t, s2_t = tabs
    tab = pl.BlockSpec((tm, LANES), lambda i: (i % tps, 0))

    def body(dz_hbm, dq_ref, dk_ref, dv_ref, zm_ref, gql_ref, gkvl_ref, gq_ref, gk_ref, c_ref, s1_ref, s2_ref,
             wuq_ref, wk_ref, wv_ref,
             dzm_ref, dqpre_ref, dkh_ref, dgq_ref, dgk_ref, dgql_ref, dgkvl_ref):
        i = pl.program_id(0)
        c, s1, s2 = c_ref[...], s1_ref[...], s2_ref[...]
        nq, rq = _rms(zm_ref[:, :Q_RANK])
        qpre = _dot((nq * gql_ref[...]).astype(BF16), wuq_ref[...])
        nkv, rkv = _rms(zm_ref[:, Q_RANK:OFF_KV])
        knope = _dot((nkv * gkvl_ref[...]).astype(BF16), wk_ref[...])
        zkr_v = zm_ref[:, OFF_KV:]
        gk = gk_ref[...]
        kr_roped = _rope(zkr_v * gk, c, s1, s2)
        dgq = jnp.zeros((1, LANES), F32)
        dgk = jnp.zeros((1, LANES), F32)
        dzkr = jnp.zeros((tm, LANES), F32)
        dt_sum = jnp.zeros((tm, LANES), F32)
        for hd in range(N_HEADS):
            sl = slice(hd * LANES, (hd + 1) * LANES)
            n, r = _head_rms(qpre[:, sl])
            dyr = _rope_t(dq_ref[:, sl], c, s1, s2)
            dgq = dgq + jnp.sum(dyr * n, axis=0, keepdims=True)
            dqpre_ref[:, sl] = _head_rms_bwd(n, r, dyr * gq_ref[...]).astype(BF16)
            kn = knope[:, sl]
            kh = kn + zkr_v
            _, r = _head_rms(kh)
            dk = dk_ref[:, sl]
            dt = dk * r
            dr = jnp.sum(dk * (kn * gk + kr_roped), axis=-1, keepdims=True)
            via_r = (dr * (r * r * r) * (-1.0 / QK_HEAD)) * kh
            dgk = dgk + jnp.sum(dt * kn, axis=0, keepdims=True)
            dt_sum = dt_sum + dt
            dzkr = dzkr + via_r
            dkh_ref[:, sl] = (dt * gk + via_r).astype(BF16)
        de = _rope_t(dt_sum, c, s1, s2)
        dzkr = dzkr + de * gk
        dgk = dgk + jnp.sum(de * zkr_v, axis=0, keepdims=True)
        _acc(dgq_ref, dgq[:, :QK_HEAD], i == 0)
        _acc(dgk_ref, dgk[:, :QK_HEAD], i == 0)
        dzm_ref[:, OFF_KV:] = dzkr.astype(BF16)
        dqln = _dot_nt(dqpre_ref[...], wuq_ref[...])
        _acc(dgql_ref, jnp.sum(dqln * nq, axis=0, keepdims=True), i == 0)
        dzm_ref[:, :Q_RANK] = _rms_bwd(nq, rq, dqln * gql_ref[...]).astype(BF16)
        dkvn = _dot_nt(dkh_ref[...], wk_ref[...]) + _dot_nt(dv_ref[...], wv_ref[...])
        _acc(dgkvl_ref, jnp.sum(dkvn * nkv, axis=0, keepdims=True), i == 0)
        dzm_ref[:, Q_RANK:OFF_KV] = _rms_bwd(nkv, rkv, dkvn * gkvl_ref[...]).astype(BF16)

    return pl.pallas_call(
        body, name="mla_bwd", grid=(t // tm,),
        out_shape=(_sds(dz.shape, BF16), _sds((t, HW), BF16), _sds((t, HW), BF16), _sds((1, QK_HEAD), F32),
                   _sds((1, QK_HEAD), F32), _sds((1, Q_RANK), F32), _sds((1, KV_RANK), F32)),
        in_specs=[ANY, _row(tm, HW), _row(tm, HW), _row(tm, HW), _row(tm, MLA_IN),
                  _full((1, Q_RANK)), _full((1, KV_RANK)), _full((1, LANES)), _full((1, LANES)), tab, tab, tab,
                  _full(wuq_p.shape), _full(wk_p.shape), _full(wv_p.shape)],
        out_specs=(pl.BlockSpec((tm, MLA_IN), lambda i: (i, p_q // MLA_IN)), _row(tm, HW), _row(tm, HW),
                   _full((1, QK_HEAD)), _full((1, QK_HEAD)), _full((1, Q_RANK)), _full((1, KV_RANK))),
        input_output_aliases={0: 0},
        compiler_params=_params("arbitrary"),
    )(dz, dq, dk, dv, zm, gql, gkvl, gq, gk, c_t, s1_t, s2_t, wuq_p, wk_p, wv_p)


def _bwd_in(dz, x, dx1, g1, mod3, win_p, tm, tps, scatter=()):
    t, d = x.shape
    npad = dz.shape[1]

    ns = len(scatter)
    n_steps = t // tm

    def body(dz_ref, x_ref, dx1_ref, g_ref, mod_ref, wt_hbm, *rest):
        gx_ref, dshift_ref, dscale_ref, dg1_ref = rest[ns:ns + 4]
        wt_ref = rest[2 * ns + 4]
        i = pl.program_id(0)
        if ns:
            start, finish = _scatter_phases(rest[:ns], rest[ns + 4:2 * ns + 4], *rest[2 * ns + 5:])
            pl.when(i == 0)(start)
        _load_resident(i, [(wt_hbm, wt_ref)])
        first_seq = (i % tps) == 0
        dh = _dot_nt(dz_ref[...], wt_ref[...])
        n, r = _rms(x_ref[...])
        g = g_ref[...]
        sc1 = 1.0 + mod_ref[1:2, :]
        _acc(dshift_ref, jnp.sum(dh, axis=0, keepdims=True), first_seq)
        _acc(dscale_ref, jnp.sum(dh * (n * g), axis=0, keepdims=True), first_seq)
        _acc(dg1_ref, jnp.sum((dh * sc1) * n, axis=0, keepdims=True), i == 0)
        gx_ref[...] = dx1_ref[...] + _rms_bwd(n, r, (dh * sc1) * g)
        if ns:
            pl.when(i == n_steps - 1)(finish)

    nseq = t // (tm * tps)
    sv = _sds((nseq, 1, d), F32)
    res = pl.pallas_call(
        body, name="bwd_in", grid=(n_steps,),
        out_shape=(_sds((t, d), F32), sv, sv, _sds((1, d), F32)) + _scatter_shapes(scatter),
        in_specs=[_row(tm, npad), _row(tm, d), _row(tm, d), _full((1, d)), _modspec(d, tps), ANY] + [ANY] * ns,
        out_specs=(_row(tm, d), _seqv(d, tps), _seqv(d, tps), _full((1, d))) + (ANY,) * ns,
        scratch_shapes=[pltpu.VMEM(win_p.shape, BF16)] + (_scatter_sems(ns) if ns else []),
        compiler_params=_params("arbitrary"),
    )(dz, x, dx1, g1, mod3, win_p, *scatter)
    return res[0], res[1], res[2], res[3], res[4:]


def _tile_of(n, choices):
    for c in choices:
        if n % c == 0:
            return c
    return n


def _tn_matmul(a, b, name, col_shards=0):
    t, k = a.shape
    n = b.shape[1]
    tk = _tile_of(k, (1024, 512, 256, 128))
    tn = n // col_shards if col_shards else _tile_of(n, (1024, 896, 768, 512, 384, 256, 128))
    tt = _tile_of(t, (4096, 2048, 1024, 512, 256))

    def body(a_ref, b_ref, o_ref):
        _acc(o_ref, _dot_tn(a_ref[...], b_ref[...]), pl.program_id(2) == 0)

    if col_shards:
        out_shape, out_spec = _sds((col_shards, k, tn), F32), pl.BlockSpec((None, tk, tn), lambda i, j, s: (j, i, 0))
    else:
        out_shape, out_spec = _sds((k, n), F32), pl.BlockSpec((tk, tn), lambda i, j, s: (i, j))
    return pl.pallas_call(
        body, name=name, grid=(k // tk, n // tn, t // tt), out_shape=out_shape,
        in_specs=[pl.BlockSpec((tt, tk), lambda i, j, s: (s, i)), pl.BlockSpec((tt, tn), lambda i, j, s: (s, j))],
        out_specs=out_spec, compiler_params=_params("arbitrary", "arbitrary", "arbitrary"),
    )(a, b)


N_SHARD = 4
COL_SHARDED = ("w_in", "w_uq", "w_ukv", "w_o_mla", "w_pw_out", "w_ff1")
ROW_SHARDED = ("w_out", "w_ff2")
BIG = ("w_in", "w_uq", "w_ukv", "w_o_mla", "w_pw_out", "w_out", "w_ff1", "w_ff2")
SMALL = ("norm1_g", "q_latent_g", "kv_latent_g", "qk_norm_q_g", "qk_norm_k_g", "conv_b", "conv_ln_g", "conv_ln_b",
         "norm2_g")
WEIGHTS = ("w_ada", "b_ada", "norm1_g", "w_in", "q_latent_g", "w_uq", "kv_latent_g", "w_ukv", "qk_norm_q_g",
           "qk_norm_k_g", "w_o_mla", "conv_w", "conv_b", "conv_ln_g", "conv_ln_b", "w_pw_out", "w_out", "norm2_g",
           "w_ff1", "w_ff2")


def _pad_heads(w, width):
    k = w.shape[0]
    w3 = w.reshape(k, N_HEADS, width)
    return jnp.pad(w3, ((0, 0), (0, 0), (0, LANES - width))).reshape(k, HW)


def _unpad_heads(g, width):
    k = g.shape[0]
    return g.reshape(k, N_HEADS, LANES)[:, :, :width].reshape(k, N_HEADS * width)


def _pad_win(w):
    d = w.shape[0]
    z = lambda n: jnp.zeros((d, n), w.dtype)
    return jnp.concatenate([w[:, OFF_GLU:], w[:, OFF_KR:OFF_GLU], w[:, :OFF_KV], z(KR_LANE), w[:, OFF_KV:OFF_KR],
                            z(LANES - KR_LANE - QK_ROPE)], axis=1)


def _unpad_win(g):
    d = g.shape[0]
    p_glu, p_q, _ = _layout(d)
    kr = p_q + OFF_KV + KR_LANE
    return jnp.concatenate([g[:, p_q:p_q + OFF_KV], g[:, kr:kr + QK_ROPE], g[:, p_glu:p_q], g[:, :p_glu]], axis=1)


def _col_shards(g):
    k, n = g.shape
    return g.reshape(k, N_SHARD, n // N_SHARD).transpose(1, 0, 2)


def _from_shards(g, name):
    ns, ks, nn = g.shape
    if name in ROW_SHARDED:
        return g.reshape(ns * ks, nn)
    return g.transpose(1, 0, 2).reshape(ks, ns * nn)


EARLY = ("w_in", "w_uq", "w_ukv")
LATE = ("w_o_mla", "w_pw_out", "w_out", "w_ff1", "w_ff2")


def _assemble(names, gathered):
    return {n: _from_shards(g.reshape((N_SHARD, 2 * g.shape[1]) + g.shape[2:]), n) for n, g in zip(names, gathered)}


GROUP_A = ("w_out", "w_ff1", "w_ff2")
GROUP_B = ("w_in", "w_uq", "w_ukv", "w_o_mla", "w_pw_out")


def _pair_halves(g):
    return g.reshape(N_SHARD, 2, g.shape[1] // 2, g.shape[2])


def _pair_sums(names, halves, from_sibling):
    if not halves:
        return []
    cidx = lax.axis_index("c").reshape(1).astype(jnp.int32)
    return [_add_pair(g, l, cidx, "pair_sum_" + n) for n, g, l in zip(names, halves, from_sibling)]


def _local_step(x, target, mod, sp, w, late=None, tm=256):
    comm = late is not None
    w = dict(w)
    nseq, seq, d = x.shape
    t = nseq * seq
    tps = seq // tm
    xf = x.reshape(t, d)
    tg = target.reshape(t, d)
    mod3 = mod.reshape(nseq, N_MOD, d)

    win_p = _pad_win(w["w_in"])
    wuq_p = _pad_heads(w["w_uq"], QK_HEAD)
    wkv3 = w["w_ukv"].reshape(KV_RANK, N_HEADS, QK_NOPE + V_HEAD)
    wk_p = _pad_heads(wkv3[:, :, :QK_NOPE].reshape(KV_RANK, -1), QK_NOPE)
    wv_p = _pad_heads(wkv3[:, :, QK_NOPE:].reshape(KV_RANK, -1), V_HEAD)
    cw = jnp.pad(w["conv_w"], ((0, HALO - CONV_W), (0, 0)))
    pad_g = lambda g: jnp.pad(g, ((0, 0), (0, LANES - QK_HEAD)))
    gq, gk = pad_g(sp["qk_norm_q_g"]), pad_g(sp["qk_norm_k_g"])
    tabs = _rope_tables(seq)

    tm_in, tps_in = (2 * tm, tps // 2) if tps % 2 == 0 else (tm, tps)
    h, zm, zglu, zgate, u0 = _fwd_in(xf, sp["norm1_g"], mod3, win_p, tm_in, tps_in)
    q, k, v, qln, kvn = _mla_prep(zm, sp["q_latent_g"], sp["kv_latent_g"], gq, gk, tabs, wuq_p, wk_p, wv_p, tm, tps)
    attn, gathered = _attn_fwd(q, k, v, nseq, seq, tuple(late) if comm else ())
    if comm:
        w.update(_assemble(LATE, gathered))
    wo_p = jnp.pad(w["w_o_mla"].reshape(N_HEADS, V_HEAD, d), ((0, 0), (0, LANES - V_HEAD), (0, 0))).reshape(HW, d)
    x1, mixed, mpre, ya, yb, u1, u3 = _fwd_mix(attn, u0, zgate, xf, mod3, wo_p, cw, sp["conv_b"], sp["conv_ln_g"],
                                               sp["conv_ln_b"], w["w_pw_out"], w["w_out"], tm, tps)
    h2, a, r, dy, df, dgate2, loss_acc = _fwd_ffn(x1, tg, sp["norm2_g"], mod3, w["w_ff1"], w["w_ff2"], tm, tps)
    da, dx1, dmixed, dshift2, dscale2, dgate1, dg2 = _bwd_ffn(df, a, x1, dy, mixed, sp["norm2_g"], mod3,
                                                              w["w_ff2"], w["w_ff1"], tm, tps)
    gw = {
        "w_out": _tn_matmul(mpre, dmixed, "dw_out").reshape(N_SHARD, d // N_SHARD, d),
        "w_ff1": _tn_matmul(h2, da, "dw_ff1", N_SHARD),
        "w_ff2": _tn_matmul(r, df, "dw_ff2").reshape(N_SHARD, -1, d),
    }
    halves_a = [_pair_halves(gw[n]) for n in GROUP_A] if comm else []
    dya, dyb, dz, do, du1, dlng, dlnb, dcb, from_sibling = _bwd_mix(
        dmixed, zgate, ya, yb, u1, sp["conv_ln_g"], sp["conv_ln_b"], w["w_out"], wo_p, w["w_pw_out"], tm, tuple(halves_a))
    pair_a = _pair_sums(GROUP_A, halves_a, from_sibling)
    dz, dcw = _bwd_conv(dz, du1, u0, zglu, cw, tm, tps)
    gw["conv_w"] = dcw
    dq, dk, dv, land_a = _attn_bwd(q, k, v, do, nseq, seq, tuple(p[1] for p in pair_a))
    dz, dqpre, dkh, dgq, dgk, dgql, dgkvl = _mla_bwd(dz, dq, dk, dv, zm, sp["q_latent_g"], sp["kv_latent_g"], gq, gk,
                                                      tabs, wuq_p, wk_p, wv_p, tm, tps)
    dwk_p = _tn_matmul(kvn, dkh, "dw_uk")
    dwv_p = _tn_matmul(kvn, dv, "dw_uv")
    dwkv = jnp.concatenate([dwk_p.reshape(KV_RANK, N_HEADS, LANES)[:, :, :QK_NOPE],
                            dwv_p.reshape(KV_RANK, N_HEADS, LANES)[:, :, :V_HEAD]], axis=2).reshape(KV_RANK, -1)
    dwo = _tn_matmul(attn, dya, "dw_o").reshape(N_HEADS, LANES, d)[:, :V_HEAD].reshape(MLA_WIDTH, d)
    gw["w_in"] = _col_shards(_unpad_win(_tn_matmul(h, dz, "dw_in")))
    gw["w_uq"] = _col_shards(_unpad_heads(_tn_matmul(qln, dqpre, "dw_uq"), QK_HEAD))
    gw["w_ukv"] = _col_shards(dwkv)
    gw["w_o_mla"] = _col_shards(dwo)
    gw["w_pw_out"] = _tn_matmul(u3, dyb, "dw_pw", N_SHARD)
    pair_b = []
    if comm:
        halves_b = [_pair_halves(gw[n]) for n in GROUP_B]
        pair_b = _pair_sums(GROUP_B, halves_b, _pair_swap(halves_b, "grad_pair_swap"))
    gx, dshift1, dscale1, dg1, land_b = _bwd_in(dz, xf, dx1, sp["norm1_g"], mod3, win_p, tm_in, tps_in,
                                                tuple(p[1] for p in pair_b))
    if comm:
        for n, p, l in zip(GROUP_A + GROUP_B, pair_a + pair_b, land_a + land_b):
            gw[n] = (p[0], l)
    gs = {
        "norm1_g": dg1, "q_latent_g": dgql, "kv_latent_g": dgkvl, "qk_norm_q_g": dgq, "qk_norm_k_g": dgk,
        "conv_b": dcb, "conv_ln_g": dlng, "conv_ln_b": dlnb, "norm2_g": dg2,
    }
    dmod = jnp.concatenate([dshift1, dscale1, dgate1, dshift2, dscale2, dgate2], axis=2).reshape(nseq, N_MOD * d)
    return loss_acc, gx.reshape(nseq, seq, d), dmod, gw, gs


def kernel(x, c, w_ada, b_ada, norm1_g, w_in, q_latent_g, w_uq, kv_latent_g, w_ukv, qk_norm_q_g, qk_norm_k_g, w_o_mla, conv_w, conv_b, conv_ln_g, conv_ln_b, w_pw_out, w_out, norm2_g, w_ff1, w_ff2, loss_target, m_w_ada, m_b_ada, m_norm1_g, m_w_in, m_q_latent_g, m_w_uq, m_kv_latent_g, m_w_ukv, m_qk_norm_q_g, m_qk_norm_k_g, m_w_o_mla, m_conv_w, m_conv_b, m_conv_ln_g, m_conv_ln_b, m_w_pw_out, m_w_out, m_norm2_g, m_w_ff1, m_w_ff2, v_w_ada, v_b_ada, v_norm1_g, v_w_in, v_q_latent_g, v_w_uq, v_kv_latent_g, v_w_ukv, v_qk_norm_q_g, v_qk_norm_k_g, v_w_o_mla, v_conv_w, v_conv_b, v_conv_ln_g, v_conv_ln_b, v_w_pw_out, v_w_out, v_norm2_g, v_w_ff1, v_w_ff2):
    given = dict(locals())
    wts = {n: given[n][0] for n in WEIGHTS}
    mom = {n: given["m_" + n][0] for n in WEIGHTS}
    var = {n: given["v_" + n][0] for n in WEIGHTS}
    vec = lambda a: a.reshape(1, -1)
    nseq, seq, d = x.shape
    ix, iy, ic = _place()
    shard = 2 * ix + iy

    half = lambda n: lax.dynamic_slice_in_dim(wts[n].astype(BF16), ic * (wts[n].shape[0] // 2), wts[n].shape[0] // 2,
                                              axis=0)
    gathered = _all_gather8([half(n) for n in EARLY] + [wts["conv_w"], c], "gather_weights")
    full = _assemble(EARLY, gathered)
    full["conv_w"] = _from_shards(gathered[-2][0::2], "conv_w")
    c_all = gathered[-1].reshape(8 * nseq, d)

    n_ada = wts["w_ada"].shape[1]
    b_sh = lax.dynamic_slice_in_dim(vec(wts["b_ada"]), shard * n_ada, n_ada, axis=1)
    mod_sh = _ada_mod(c_all, wts["w_ada"], b_sh)
    hb = 4 * nseq
    mod_blk = lax.dynamic_slice_in_dim(mod_sh, ic * hb, hb, axis=0)
    (mod_all,) = _all_gather8([mod_blk], "gather_mod")
    mod_mine = lax.dynamic_slice_in_dim(mod_all, (2 * iy + ic) * nseq, nseq, axis=1)
    mod = jnp.concatenate([lax.dynamic_index_in_dim(mod_mine, 2 * s + ix, axis=0, keepdims=False)
                           for s in range(N_SHARD)], axis=1)

    sp = {n: vec(wts[n]) for n in SMALL}
    loss_part, grad_x, dmod, gw, gs = _local_step(x, loss_target, mod, sp, full, [half(n) for n in LATE])

    parts = _all_gather8([dmod, gw["conv_w"], loss_part] + [gs[n] for n in SMALL], "gather_small")
    dmod_all = parts[0].reshape(8 * nseq, N_MOD * d)
    dmod_sh = lax.dynamic_slice_in_dim(dmod_all, shard * n_ada, n_ada, axis=1)
    res = _ada_bwd(c_all, dmod_all, dmod_sh, parts[1:])
    grads = {"w_ada": res[0], "b_ada": res[1]}
    n_cw = wts["conv_w"].shape[1]
    grads["conv_w"] = lax.dynamic_slice_in_dim(res[2], shard * n_cw, n_cw, axis=1)[:CONV_W]
    loss = res[3][0, 0]
    for n, g in zip(SMALL, res[4:]):
        grads[n] = g

    own_c = jnp.stack([shard, ic]).astype(jnp.int32)
    mine_sum = [_add_chips(gw[n][0], gw[n][1], own_c, "chip_sum_" + n) for n in BIG]
    for n, g in zip(BIG, _pair_gather(mine_sum, "grad_pair_gather")):
        grads[n] = g.reshape(wts[n].shape)

    delta, new_m, new_v = {}, {}, {}
    for n in BIG + ("w_ada",):
        delta[n], new_m[n], new_v[n] = _adamw(wts[n], grads[n], mom[n], var[n], "adamw_" + n)
    rest = ("b_ada", "conv_w") + SMALL
    as2d = lambda a: a if a.ndim == 2 else vec(a)
    res = _adamw_small(*[[as2d(t[n]) for n in rest] for t in (wts, grads, mom, var)])
    for dst, arrs in zip((delta, new_m, new_v), res):
        for n, a in zip(rest, arrs):
            dst[n] = a

    outs = [loss, grad_x]
    for group in (grads, delta, new_m, new_v):
        outs += [group[n].reshape(given[n].shape) for n in WEIGHTS]
    return tuple(outs)
```

```python
import jax
import jax.numpy as jnp
from jax import lax
from jax.experimental import pallas as pl
from jax.experimental.pallas import tpu as pltpu

F32 = jnp.float32
BF16 = jnp.bfloat16
MESH = pl.DeviceIdType.MESH
ANY = pl.BlockSpec(memory_space=pl.ANY)

CHUNK = 64
CHUNK_SHIFT = 6
N_HEADS = 8
QK_NOPE = 64
QK_ROPE = 32
QK_HEAD = QK_NOPE + QK_ROPE
V_HEAD = 64
Q_RANK = 256
KV_RANK = 128
MLA_WIDTH = N_HEADS * V_HEAD
CONV_CH = 512
CONV_W = 31
ROPE_THETA = 10000.0
EPS = 1e-6
LANES = 128
SUBLANES = 8
HW = N_HEADS * LANES
OFF_KV = Q_RANK + KV_RANK
OFF_KR = OFF_KV + QK_ROPE
OFF_GLU = OFF_KR + 2 * CONV_CH
KR_LANE = QK_NOPE
MLA_IN = Q_RANK + KV_RANK + LANES
HALO = 32
N_MOD = 6

ADAM_LR = 0.001
ADAM_B1 = 0.9
ADAM_B2 = 0.999
ADAM_EPS = 1e-08
ADAM_WD = 0.01
ADAM_STEP = 10

VMEM_LIMIT = 56 * 1024 * 1024
BQ = 256


def _layout(d):
    p_glu = 2 * d
    p_q = p_glu + 2 * CONV_CH
    return p_glu, p_q, p_q + MLA_IN


def _params(*sem):
    return pltpu.CompilerParams(dimension_semantics=sem, vmem_limit_bytes=VMEM_LIMIT)


def _dot(a, b):
    return jnp.dot(a, b, preferred_element_type=F32)


def _dot_tn(a, b):
    return lax.dot_general(a, b, (((0,), (0,)), ((), ())), preferred_element_type=F32)


def _dot_nt(a, b):
    return lax.dot_general(a, b, (((1,), (1,)), ((), ())), preferred_element_type=F32)


def _acc(ref, val, first):
    @pl.when(first)
    def _():
        ref[...] = val

    @pl.when(jnp.logical_not(first))
    def _():
        ref[...] += val


def _rms(x):
    r = lax.rsqrt(jnp.mean(x * x, axis=-1, keepdims=True) + EPS)
    return x * r, r


def _rms_bwd(n, r, dn):
    return r * (dn - n * jnp.mean(dn * n, axis=-1, keepdims=True))


def _head_rms(sl):
    r = lax.rsqrt(jnp.sum(sl * sl, axis=-1, keepdims=True) * (1.0 / QK_HEAD) + EPS)
    return sl * r, r


def _head_rms_bwd(n, r, dn):
    return r * (dn - n * (jnp.sum(dn * n, axis=-1, keepdims=True) * (1.0 / QK_HEAD)))


def _rope(x, c, s1, s2):
    return x * c + pltpu.roll(x, QK_ROPE // 2, 1) * s1 + pltpu.roll(x, LANES - QK_ROPE // 2, 1) * s2


def _rope_t(dy, c, s1, s2):
    return dy * c + pltpu.roll(dy * s1, LANES - QK_ROPE // 2, 1) + pltpu.roll(dy * s2, QK_ROPE // 2, 1)


def _rope_tables(seq):
    half = QK_ROPE // 2
    inv_freq = ROPE_THETA ** (-jnp.arange(0, QK_ROPE, 2, dtype=F32) / QK_ROPE)
    ang = jnp.arange(seq, dtype=F32)[:, None] * inv_freq[None, :]
    cos, sin = jnp.cos(ang), jnp.sin(ang)
    z = lambda n: jnp.zeros((seq, n), F32)
    tail = LANES - QK_HEAD
    c = jnp.concatenate([jnp.ones((seq, QK_NOPE), F32), cos, cos, jnp.ones((seq, tail), F32)], axis=1)
    s1 = jnp.concatenate([z(QK_NOPE + half), sin, z(tail)], axis=1)
    s2 = jnp.concatenate([z(QK_NOPE), -sin, z(half + tail)], axis=1)
    return c, s1, s2


def _row(tm, w):
    return pl.BlockSpec((tm, w), lambda i: (i, 0))


def _modspec(d, tps):
    return pl.BlockSpec((None, N_MOD, d), lambda i: (i // tps, 0, 0))


def _seqv(w, tps):
    return pl.BlockSpec((None, 1, w), lambda i: (i // tps, 0, 0))


def _full(shape):
    return pl.BlockSpec(shape, lambda i: tuple(0 for _ in shape))


def _sds(shape, dtype):
    return jax.ShapeDtypeStruct(shape, dtype)


CONV_ROWS = 64
CONV_LC = CONV_CH // LANES


def _lane_chunks():
    return [(lc, slice(lc * LANES, (lc + 1) * LANES)) for lc in range(CONV_LC)]


def _fill_shifted(ext_ref, head, body):
    nh = head.shape[0]
    for lc, ls in _lane_chunks():
        ext_ref[0, lc, :nh, :] = head[:, ls]
        ext_ref[0, lc, nh:, :] = body[:, ls]
        rows = ext_ref[0, lc]
        for b in range(1, SUBLANES):
            ext_ref[b, lc] = pltpu.roll(rows, rows.shape[0] - b, 0)


def _shifted_shape(tm):
    return (SUBLANES, CONV_LC, tm + HALO, LANES)


def _conv_chunk(c):
    return c % CONV_LC, pl.multiple_of((c // CONV_LC) * CONV_ROWS, CONV_ROWS)


def _shifted(ext_ref, o, lc, r0):
    a = pl.multiple_of((o // SUBLANES) * SUBLANES + r0, SUBLANES)
    return ext_ref[o % SUBLANES, lc, pl.ds(a, CONV_ROWS), :]


def _by_lane_chunk(a):
    return a.reshape(a.shape[0], CONV_LC, LANES).transpose(1, 0, 2)


def _load_resident(i, pairs):
    @pl.when(i == 0)
    def _():
        for src, dst in pairs:
            pltpu.sync_copy(src, dst)


def _place():
    return lax.axis_index("x"), lax.axis_index("y"), lax.axis_index("c")


def _all_gather8(blocks, name):
    na = len(blocks)

    def body(*refs):
        start, forward, finish = _gather8_phases(refs[:na], refs[na:2 * na], *refs[2 * na:])
        start()
        forward()
        finish()

    outs = pl.pallas_call(
        body, name=name, out_shape=_gather8_shapes(blocks), in_specs=[ANY] * na, out_specs=(ANY,) * na,
        scratch_shapes=_gather8_sems(na),
    )(*blocks)
    return _own_block_placed(outs, blocks)


def _gather8_shapes(blocks):
    return tuple(_sds((8,) + b.shape, b.dtype) for b in blocks)


def _gather8_sems(na):
    return [pltpu.SemaphoreType.DMA((7 * na,)), pltpu.SemaphoreType.DMA((7 * na,))]


def _own_block_placed(outs, blocks):
    ix, iy, ic = _place()
    return tuple(lax.dynamic_update_index_in_dim(o, b, 4 * ix + 2 * iy + ic, 0) for o, b in zip(outs, blocks))


def _gather8_phases(x_refs, out_refs, send_sems, recv_sems):
    na = len(x_refs)
    x, y, c = _place()
    me, sibling = (x, y, c), (x, y, 1 - c)
    chips = [(1 - x, y), (x, 1 - y), (1 - x, 1 - y)]

    def copy(a, k, blk, to, from_input=False):
        dst = out_refs[a].at[4 * blk[0] + 2 * blk[1] + blk[2]]
        return pltpu.make_async_remote_copy(
            src_ref=x_refs[a] if from_input else dst, dst_ref=dst,
            send_sem=send_sems.at[7 * a + k], recv_sem=recv_sems.at[7 * a + k], device_id=to, device_id_type=MESH)

    def first(a):
        return [copy(a, 0, me, sibling, True)] + [copy(a, 1 + j, me, (*chip, c), True) for j, chip in enumerate(chips)]

    def start():
        for a in range(na):
            for cp in first(a):
                cp.start()

    def forward():
        for j, chip in enumerate(chips):
            for a in range(na):
                copy(a, 1 + j, (*chip, c), me).wait_recv()
                copy(a, 4 + j, (*chip, c), sibling).start()

    def finish():
        for a in range(na):
            copy(a, 0, sibling, me).wait_recv()
            for j, chip in enumerate(chips):
                copy(a, 4 + j, (*chip, 1 - c), me).wait_recv()
        for a in range(na):
            for cp in first(a) + [copy(a, 4 + j, (*chip, c), sibling) for j, chip in enumerate(chips)]:
                cp.wait_send()

    return start, forward, finish


def _pair_swap(gs, name):
    na = len(gs)

    def body(*refs):
        start, finish = _swap_phases(refs[:na], refs[na:2 * na], *refs[2 * na:])
        start()
        finish()

    return pl.pallas_call(
        body, name=name, out_shape=_swap_shapes(gs), in_specs=[ANY] * na, out_specs=(ANY,) * na,
        scratch_shapes=_swap_sems(gs),
    )(*gs)


def _swap_shapes(gs):
    return tuple(_sds(g.shape[:1] + g.shape[2:], g.dtype) for g in gs)


def _swap_sems(gs):
    n = sum(g.shape[0] for g in gs)
    return [pltpu.SemaphoreType.DMA((n,)), pltpu.SemaphoreType.DMA((n,))]


def _swap_phases(g_refs, land_refs, send_sems, recv_sems):
    x, y, c = _place()

    def copies():
        cps, k = [], 0
        for g_ref, land_ref in zip(g_refs, land_refs):
            for s in range(g_ref.shape[0]):
                cps.append(pltpu.make_async_remote_copy(
                    src_ref=g_ref.at[s, 1 - c], dst_ref=land_ref.at[s], send_sem=send_sems.at[k],
                    recv_sem=recv_sems.at[k], device_id=(x, y, 1 - c), device_id_type=MESH))
                k += 1
        return cps

    def start():
        for cp in copies():
            cp.start()

    def finish():
        for cp in copies():
            cp.wait()

    return start, finish


def _scatter_shapes(hs):
    return tuple(_sds((3,) + h.shape[1:], h.dtype) for h in hs)


def _scatter_sems(na):
    return [pltpu.SemaphoreType.DMA((3 * na,)), pltpu.SemaphoreType.DMA((3 * na,))]


def _scatter_phases(h_refs, land_refs, send_sems, recv_sems):
    x, y, c = _place()
    chips = [(1 - x, y), (x, 1 - y), (1 - x, 1 - y)]

    def copies():
        return [pltpu.make_async_remote_copy(
            src_ref=h_refs[a].at[2 * tx + ty], dst_ref=land_refs[a].at[j], send_sem=send_sems.at[3 * a + j],
            recv_sem=recv_sems.at[3 * a + j], device_id=(tx, ty, c), device_id_type=MESH)
            for a in range(len(h_refs)) for j, (tx, ty) in enumerate(chips)]

    def start():
        for cp in copies():
            cp.start()

    def finish():
        for cp in copies():
            cp.wait()

    return start, finish


def _pair_gather(fs, name):
    na = len(fs)

    def body(*refs):
        out_refs = refs[na:2 * na]
        send_sems, recv_sems = refs[2 * na:]
        x, y, c = _place()
        sends = [pltpu.make_async_remote_copy(
            src_ref=out_refs[a].at[c], dst_ref=out_refs[a].at[c], send_sem=send_sems.at[a], recv_sem=recv_sems.at[a],
            device_id=(x, y, 1 - c), device_id_type=MESH) for a in range(na)]
        recvs = [pltpu.make_async_remote_copy(
            src_ref=out_refs[a].at[c], dst_ref=out_refs[a].at[1 - c], send_sem=send_sems.at[a],
            recv_sem=recv_sems.at[a], device_id=(x, y, 1 - c), device_id_type=MESH) for a in range(na)]
        for cp in sends:
            cp.start()
        for cp in recvs:
            cp.wait_recv()
        for cp in sends:
            cp.wait_send()

    return pl.pallas_call(
        body, name=name, out_shape=tuple(_sds(f.shape, f.dtype) for f in fs),
        in_specs=[ANY] * na, out_specs=(ANY,) * na, input_output_aliases={a: a for a in range(na)},
        scratch_shapes=[pltpu.SemaphoreType.DMA((na,)), pltpu.SemaphoreType.DMA((na,))],
    )(*fs)


def _row_tile(r, n, itemsize=4, budget=1 << 20):
    if r * n * itemsize <= budget:
        return r
    best = None
    for tr in range(16, r, 16):
        if r % tr == 0 and tr * n * itemsize <= budget:
            best = tr
    assert best is not None, (r, n)
    return best


def _add_pair(g, land, cidx, name):
    ns, _, r, n = g.shape
    tr = _row_tile(r, n)

    def body(c_ref, a_ref, b_ref, o_ref, ob_ref):
        s = a_ref[...] + b_ref[...]
        o_ref[...] = s
        ob_ref[...] = s.astype(BF16)

    out = pl.BlockSpec((None, tr, n), lambda s, i, cr: (s, i, 0))
    return pl.pallas_call(
        body, name=name, out_shape=(_sds((ns, r, n), F32), _sds((ns, r, n), BF16)),
        grid_spec=pltpu.PrefetchScalarGridSpec(
            num_scalar_prefetch=1, grid=(ns, r // tr),
            in_specs=[pl.BlockSpec((None, None, tr, n), lambda s, i, cr: (s, cr[0], i, 0)), out],
            out_specs=(out, out)),
        compiler_params=_params("arbitrary", "arbitrary"),
    )(cidx, g, land)


def _add_chips(h, land, own_c, name):
    _, r, n = h.shape
    tr = _row_tile(r, n)

    def body(o_idx, h_ref, l_ref, o_ref):
        o_ref[...] = ((h_ref[...] + l_ref[0].astype(F32)) + l_ref[1].astype(F32)) + l_ref[2].astype(F32)

    return pl.pallas_call(
        body, name=name, out_shape=_sds((2, r, n), F32),
        grid_spec=pltpu.PrefetchScalarGridSpec(
            num_scalar_prefetch=1, grid=(r // tr,),
            in_specs=[pl.BlockSpec((None, tr, n), lambda i, o: (o[0], i, 0)),
                      pl.BlockSpec((3, tr, n), lambda i, o: (0, i, 0))],
            out_specs=pl.BlockSpec((None, tr, n), lambda i, o: (o[1], i, 0))),
        compiler_params=_params("arbitrary"),
    )(own_c, h, land)


def _adam_math(w, g, m, v):
    nm = ADAM_B1 * m + (1.0 - ADAM_B1) * g
    nv = ADAM_B2 * v + (1.0 - ADAM_B2) * (g * g)
    m_hat = nm / (1.0 - ADAM_B1 ** ADAM_STEP)
    v_hat = nv / (1.0 - ADAM_B2 ** ADAM_STEP)
    return -ADAM_LR * (m_hat / (jnp.sqrt(v_hat) + ADAM_EPS) + ADAM_WD * w), nm, nv


def _adamw(w, g, m, v, name):
    r, n = w.shape
    tr = _row_tile(r, n, budget=1 << 19)

    def body(w_ref, g_ref, m_ref, v_ref, d_ref, nm_ref, nv_ref):
        d_ref[...], nm_ref[...], nv_ref[...] = _adam_math(w_ref[...], g_ref[...], m_ref[...], v_ref[...])

    spec = pl.BlockSpec((tr, n), lambda i: (i, 0))
    return pl.pallas_call(
        body, name=name, out_shape=(_sds((r, n), F32),) * 3, grid=(r // tr,),
        in_specs=[spec] * 4, out_specs=(spec,) * 3, compiler_params=_params("arbitrary"),
    )(w, g, m, v)


def _adamw_small(ws, gs, ms, vs):
    k = len(ws)

    def body(*refs):
        ins, outs = refs[:4 * k], refs[4 * k:]
        for j in range(k):
            d, nm, nv = _adam_math(ins[j][...], ins[k + j][...], ins[2 * k + j][...], ins[3 * k + j][...])
            outs[j][...] = d
            outs[k + j][...] = nm
            outs[2 * k + j][...] = nv

    shapes = tuple(_sds(w.shape, F32) for w in ws)
    res = pl.pallas_call(body, name="adamw_small", out_shape=shapes * 3,
                         compiler_params=pltpu.CompilerParams(vmem_limit_bytes=VMEM_LIMIT))(*ws, *gs, *ms, *vs)
    return res[:k], res[k:2 * k], res[2 * k:]


def _ada_mod(c_all, w_sh, b_sh):
    b, _ = c_all.shape
    n = w_sh.shape[1]

    def body(c_ref, w_ref, b_ref, o_ref):
        cc = c_ref[...]
        ca = (cc * jax.nn.sigmoid(cc)).astype(BF16)
        o_ref[...] = _dot(ca, w_ref[...].astype(BF16)) + b_ref[...]

    return pl.pallas_call(body, name="ada_mod", out_shape=_sds((b, n), F32),
                          compiler_params=pltpu.CompilerParams(vmem_limit_bytes=VMEM_LIMIT))(c_all, w_sh, b_sh)


def _ada_bwd(c_all, dmod_all, dmod_sh, parts):
    b, d = c_all.shape
    n6 = dmod_all.shape[1]
    n = dmod_sh.shape[1]
    k = len(parts)

    def body(*refs):
        c_ref, da_ref, ds_ref = refs[:3]
        p_refs = refs[3:3 + k]
        dw_ref, db_ref = refs[3 + k:5 + k]
        s_refs = refs[5 + k:]
        cc = c_ref[...]
        ca = (cc * jax.nn.sigmoid(cc)).astype(BF16)
        dw_ref[...] = _dot_tn(ca, ds_ref[...].astype(BF16))
        db_ref[...] = jnp.sum(da_ref[...], axis=0, keepdims=True)
        for p_ref, s_ref in zip(p_refs, s_refs):
            tot = p_ref[0]
            for j in range(1, p_ref.shape[0]):
                tot = tot + p_ref[j]
            s_ref[...] = tot

    return pl.pallas_call(
        body, name="ada_bwd",
        out_shape=(_sds((d, n), F32), _sds((1, n6), F32)) + tuple(_sds(p.shape[1:], F32) for p in parts),
        compiler_params=pltpu.CompilerParams(vmem_limit_bytes=VMEM_LIMIT),
    )(c_all, dmod_all, dmod_sh, *parts)


def _fwd_in(x, g1, mod3, win_p, tm, tps):
    t, d = x.shape
    p_glu, p_q, npad = _layout(d)

    def body(x_ref, g_ref, mod_ref, w_hbm, h_ref, zm_ref, zglu_ref, zgate_ref, u0_ref, w_ref):
        _load_resident(pl.program_id(0), [(w_hbm, w_ref)])
        n, _ = _rms(x_ref[...])
        h = ((n * g_ref[...]) * (1.0 + mod_ref[1:2, :]) + mod_ref[0:1, :]).astype(BF16)
        h_ref[...] = h
        z = _dot(h, w_ref[...])
        zgate_ref[...] = z[:, :p_glu]
        zglu = z[:, p_glu:p_q]
        zglu_ref[...] = zglu
        zm_ref[...] = z[:, p_q:]
        u0_ref[...] = zglu[:, :CONV_CH] * jax.nn.sigmoid(zglu[:, CONV_CH:])

    return pl.pallas_call(
        body, name="fwd_in", grid=(t // tm,),
        out_shape=(_sds((t, d), BF16), _sds((t, MLA_IN), F32), _sds((t, 2 * CONV_CH), F32), _sds((t, 2 * d), F32),
                   _sds((t, CONV_CH), F32)),
        in_specs=[_row(tm, d), _full((1, d)), _modspec(d, tps), ANY],
        out_specs=(_row(tm, d), _row(tm, MLA_IN), _row(tm, 2 * CONV_CH), _row(tm, 2 * d), _row(tm, CONV_CH)),
        scratch_shapes=[pltpu.VMEM(win_p.shape, BF16)],
        compiler_params=_params("arbitrary"),
    )(x, g1, mod3, win_p)


def _mla_prep(zm, gql, gkvl, gq, gk, tabs, wuq_p, wk_p, wv_p, tm, tps):
    t = zm.shape[0]
    c_t, s1_t, s2_t = tabs
    tab = pl.BlockSpec((tm, LANES), lambda i: (i % tps, 0))

    def body(zm_ref, gql_ref, gkvl_ref, gq_ref, gk_ref, c_ref, s1_ref, s2_ref, wuq_ref, wk_ref, wv_ref,
             q_ref, k_ref, v_ref, qln_ref, kvn_ref):
        c, s1, s2 = c_ref[...], s1_ref[...], s2_ref[...]
        nq, _ = _rms(zm_ref[:, :Q_RANK])
        qln = (nq * gql_ref[...]).astype(BF16)
        qln_ref[...] = qln
        qpre = _dot(qln, wuq_ref[...])
        nkv, _ = _rms(zm_ref[:, Q_RANK:OFF_KV])
        kvn = (nkv * gkvl_ref[...]).astype(BF16)
        kvn_ref[...] = kvn
        knope = _dot(kvn, wk_ref[...])
        v_ref[...] = _dot(kvn, wv_ref[...]).astype(BF16)
        zkr_v = zm_ref[:, OFF_KV:]
        kr_roped = _rope(zkr_v * gk_ref[...], c, s1, s2)
        slabs = [slice(hd * LANES, (hd + 1) * LANES) for hd in range(N_HEADS)]
        rq = [_head_rms(qpre[:, sl])[1] for sl in slabs]
        rk = [_head_rms(knope[:, sl] + zkr_v)[1] for sl in slabs]
        for hd, sl in enumerate(slabs):
            q_ref[:, sl] = _rope((qpre[:, sl] * rq[hd]) * gq_ref[...], c, s1, s2).astype(BF16)
            k_ref[:, sl] = (rk[hd] * (knope[:, sl] * gk_ref[...] + kr_roped)).astype(BF16)

    return pl.pallas_call(
        body, name="mla_prep", grid=(t // tm,),
        out_shape=(_sds((t, HW), BF16),) * 3 + (_sds((t, Q_RANK), BF16), _sds((t, KV_RANK), BF16)),
        in_specs=[_row(tm, MLA_IN), _full((1, Q_RANK)), _full((1, KV_RANK)),
                  _full((1, LANES)), _full((1, LANES)), tab, tab, tab,
                  _full(wuq_p.shape), _full(wk_p.shape), _full(wv_p.shape)],
        out_specs=(_row(tm, HW),) * 3 + (_row(tm, Q_RANK), _row(tm, KV_RANK)),
        compiler_params=_params("arbitrary"),
    )(zm, gql, gkvl, gq, gk, c_t, s1_t, s2_t, wuq_p, wk_p, wv_p)


AHEAD = 2
SM_SCALE = QK_HEAD ** -0.5
EXP2_SCALE = SM_SCALE * 1.4426950408889634


def _diag_mask():
    rc = jnp.right_shift(lax.broadcasted_iota(jnp.int32, (BQ, 1), 0), CHUNK_SHIFT)
    cc = jnp.right_shift(lax.broadcasted_iota(jnp.int32, (1, BQ), 1), CHUNK_SHIFT)
    return rc >= cc


def _scores(q_i, k_ref, lo, e):
    return (_dot_nt(q_i, k_ref[:lo, :]) if lo else None), _dot_nt(q_i, k_ref[lo:e, :])


def _softmax_parts(scores, mask):
    sp, sd = scores
    sd = jnp.where(mask, sd, jnp.finfo(F32).min)
    m = jnp.max(sd, axis=-1, keepdims=True)
    if sp is not None:
        m = jnp.maximum(m, jnp.max(sp, axis=-1, keepdims=True))
    pd = jnp.exp2((sd - m) * EXP2_SCALE)
    l = jnp.sum(pd, axis=-1, keepdims=True)
    pp = None
    if sp is not None:
        pp = jnp.exp2((sp - m) * EXP2_SCALE)
        l = l + jnp.sum(pp, axis=-1, keepdims=True)
    return pp, pd, l


def _attn_fwd(q, k, v, nseq, seq, gather=()):
    t = q.shape[0]
    na = len(gather)
    blk = pl.BlockSpec((seq, LANES), lambda b, h: (b, h))
    n_steps = nseq * N_HEADS

    def body(q_ref, k_ref, v_ref, *rest):
        o_ref = rest[na]
        if na:
            start, forward, finish = _gather8_phases(rest[:na], rest[na + 1:2 * na + 1], *rest[2 * na + 1:])
            step = pl.program_id(0) * N_HEADS + pl.program_id(1)
            pl.when(step == 0)(start)
            pl.when(step == (3 * n_steps) // 4)(forward)
        mask = _diag_mask()
        nb = seq // BQ
        block_scores = lambda j: _scores(q_ref[j * BQ:(j + 1) * BQ, :], k_ref, j * BQ, (j + 1) * BQ)
        ahead = [block_scores(j) for j in range(min(AHEAD, nb))]
        for i in range(nb):
            lo, e = i * BQ, (i + 1) * BQ
            cur = ahead.pop(0)
            if i + AHEAD < nb:
                ahead.append(block_scores(i + AHEAD))
            pp, pd, l = _softmax_parts(cur, mask)
            o = _dot(pd.astype(BF16), v_ref[lo:e, :])
            if lo:
                o = o + _dot(pp.astype(BF16), v_ref[:lo, :])
            o_ref[lo:e, :] = (o * (1.0 / l)).astype(BF16)
        if na:
            pl.when(step == n_steps - 1)(finish)

    res = pl.pallas_call(
        body, name="attn_fwd", grid=(nseq, N_HEADS), out_shape=(_sds((t, HW), BF16),) + _gather8_shapes(gather),
        in_specs=[blk, blk, blk] + [ANY] * na, out_specs=(blk,) + (ANY,) * na,
        scratch_shapes=_gather8_sems(na) if na else [],
        compiler_params=_params("arbitrary", "arbitrary"),
    )(q, k, v, *gather)
    return res[0], (_own_block_placed(res[1:], gather) if na else ())


def _fwd_mix(attn, u0, zgate, x, mod3, wo_p, cw, cb, lng, lnb, wpw, wout, tm, tps):
    t, d = x.shape
    hpt = tm // HALO
    cwc, cbc = _by_lane_chunk(cw), _by_lane_chunk(cb)

    def body(a_ref, u_ref, uh_ref, zg_ref, x_ref, mod_ref, wo_ref, cw_ref, cb_ref, lng_ref, lnb_ref, wpw_ref, wout_ref,
             x1_ref, mixed_ref, mpre_ref, ya_ref, yb_ref, u1_ref, u3_ref, ext_ref):
        i = pl.program_id(0)
        ya = _dot(a_ref[...], wo_ref[...])
        ya_ref[...] = ya
        first = (i % tps) == 0
        _fill_shifted(ext_ref, jnp.where(first, 0.0, uh_ref[...]), u_ref[...])
        for lc, ls in _lane_chunks():
            acc = jnp.broadcast_to(cb_ref[lc], (tm, LANES))
            for kk in range(CONV_W):
                o = HALO - (CONV_W - 1) + kk
                a = (o // SUBLANES) * SUBLANES
                acc = acc + cw_ref[lc, kk:kk + 1, :] * ext_ref[o % SUBLANES, lc, a:a + tm, :]
            u1_ref[:, ls] = acc
        acc = u1_ref[...]
        mu = jnp.mean(acc, axis=-1, keepdims=True)
        xc = acc - mu
        rstd = lax.rsqrt(jnp.mean(xc * xc, axis=-1, keepdims=True) + EPS)
        l = (xc * rstd) * lng_ref[...] + lnb_ref[...]
        u3 = (l * jax.nn.sigmoid(l)).astype(BF16)
        u3_ref[...] = u3
        yb = _dot(u3, wpw_ref[...])
        yb_ref[...] = yb
        zg = zg_ref[...]
        mpre = (jax.nn.sigmoid(zg[:, :d]) * ya + jax.nn.sigmoid(zg[:, d:]) * yb).astype(BF16)
        mpre_ref[...] = mpre
        mixed = _dot(mpre, wout_ref[...])
        mixed_ref[...] = mixed
        x1_ref[...] = x_ref[...] + mod_ref[2:3, :] * mixed

    halo = pl.BlockSpec((HALO, CONV_CH), lambda i: (jnp.maximum(i * hpt - 1, 0), 0))
    return pl.pallas_call(
        body, name="fwd_mix", grid=(t // tm,),
        out_shape=(_sds((t, d), F32), _sds((t, d), F32), _sds((t, d), BF16), _sds((t, d), F32), _sds((t, d), F32),
                   _sds((t, CONV_CH), F32), _sds((t, CONV_CH), BF16)),
        in_specs=[_row(tm, HW), _row(tm, CONV_CH), halo, _row(tm, 2 * d), _row(tm, d), _modspec(d, tps),
                  _full(wo_p.shape), _full(cwc.shape), _full(cbc.shape), _full((1, CONV_CH)), _full((1, CONV_CH)),
                  _full(wpw.shape), _full(wout.shape)],
        out_specs=(_row(tm, d), _row(tm, d), _row(tm, d), _row(tm, d), _row(tm, d), _row(tm, CONV_CH),
                   _row(tm, CONV_CH)),
        scratch_shapes=[pltpu.VMEM(_shifted_shape(tm), F32)],
        compiler_params=_params("arbitrary"),
    )(attn, u0, u0, zgate, x, mod3, wo_p, cwc, cbc, lng, lnb, wpw, wout)


def _fwd_ffn(x1, target, g2, mod3, w1, w2, tm, tps):
    t, d = x1.shape
    dff = w1.shape[1]

    def body(x1_ref, tg_ref, g_ref, mod_ref, w1_hbm, w2_hbm,
             h2_ref, a_ref, r_ref, dy_ref, df_ref, dgate_ref, loss_ref, w1_ref, w2_ref):
        i = pl.program_id(0)
        _load_resident(i, [(w1_hbm, w1_ref), (w2_hbm, w2_ref)])
        x1v = x1_ref[...]
        gate2 = mod_ref[5:6, :]
        n, _ = _rms(x1v)
        h2 = ((n * g_ref[...]) * (1.0 + mod_ref[4:5, :]) + mod_ref[3:4, :]).astype(BF16)
        h2_ref[...] = h2
        a = _dot(h2, w1_ref[...])
        a_ref[...] = a
        r = jnp.square(jnp.maximum(a, 0.0)).astype(BF16)
        r_ref[...] = r
        f = _dot(r, w2_ref[...])
        e = (x1v + gate2 * f) - tg_ref[...]
        part = 0.5 * jnp.sum(jnp.mean(e * e, axis=-1, keepdims=True), axis=0, keepdims=True)
        _acc(loss_ref, jnp.broadcast_to(part, loss_ref.shape), i == 0)
        dy = e * (1.0 / d)
        dy_ref[...] = dy
        df_ref[...] = (dy * gate2).astype(BF16)
        _acc(dgate_ref, jnp.sum(dy * f, axis=0, keepdims=True), (i % tps) == 0)

    nseq = t // (tm * tps)
    return pl.pallas_call(
        body, name="fwd_ffn", grid=(t // tm,),
        out_shape=(_sds((t, d), BF16), _sds((t, dff), F32), _sds((t, dff), BF16), _sds((t, d), F32), _sds((t, d), BF16),
                   _sds((nseq, 1, d), F32), _sds((8, LANES), F32)),
        in_specs=[_row(tm, d), _row(tm, d), _full((1, d)), _modspec(d, tps), ANY, ANY],
        out_specs=(_row(tm, d), _row(tm, dff), _row(tm, dff), _row(tm, d), _row(tm, d), _seqv(d, tps),
                   _full((8, LANES))),
        scratch_shapes=[pltpu.VMEM(w1.shape, BF16), pltpu.VMEM(w2.shape, BF16)],
        compiler_params=_params("arbitrary"),
    )(x1, target, g2, mod3, w1, w2)


def _bwd_ffn(df, a, x1, dy, mixed, g2, mod3, w2, w1, tm, tps):
    t, d = x1.shape
    dff = a.shape[1]

    def body(df_ref, a_ref, x1_ref, dy_ref, mx_ref, g_ref, mod_ref, w2_hbm, w1_hbm,
             da_ref, dx1_ref, dmixed_ref, dshift_ref, dscale_ref, dgate1_ref, dg2_ref, w2_ref, w1_ref):
        i = pl.program_id(0)
        _load_resident(i, [(w2_hbm, w2_ref), (w1_hbm, w1_ref)])
        first_seq = (i % tps) == 0
        dr = _dot_nt(df_ref[...], w2_ref[...])
        da = (dr * (2.0 * jnp.maximum(a_ref[...], 0.0))).astype(BF16)
        da_ref[...] = da
        dh2 = _dot_nt(da, w1_ref[...])
        n, r = _rms(x1_ref[...])
        g = g_ref[...]
        sc1 = 1.0 + mod_ref[4:5, :]
        _acc(dshift_ref, jnp.sum(dh2, axis=0, keepdims=True), first_seq)
        _acc(dscale_ref, jnp.sum(dh2 * (n * g), axis=0, keepdims=True), first_seq)
        _acc(dg2_ref, jnp.sum((dh2 * sc1) * n, axis=0, keepdims=True), i == 0)
        dx1 = dy_ref[...] + _rms_bwd(n, r, (dh2 * sc1) * g)
        dx1_ref[...] = dx1
        _acc(dgate1_ref, jnp.sum(dx1 * mx_ref[...], axis=0, keepdims=True), first_seq)
        dmixed_ref[...] = (dx1 * mod_ref[2:3, :]).astype(BF16)

    nseq = t // (tm * tps)
    sv = _sds((nseq, 1, d), F32)
    return pl.pallas_call(
        body, name="bwd_ffn", grid=(t // tm,),
        out_shape=(_sds((t, dff), BF16), _sds((t, d), F32), _sds((t, d), BF16), sv, sv, sv, _sds((1, d), F32)),
        in_specs=[_row(tm, d), _row(tm, dff), _row(tm, d), _row(tm, d), _row(tm, d), _full((1, d)), _modspec(d, tps),
                  ANY, ANY],
        out_specs=(_row(tm, dff), _row(tm, d), _row(tm, d), _seqv(d, tps), _seqv(d, tps), _seqv(d, tps),
                   _full((1, d))),
        scratch_shapes=[pltpu.VMEM(w2.shape, BF16), pltpu.VMEM(w1.shape, BF16)],
        compiler_params=_params("arbitrary"),
    )(df, a, x1, dy, mixed, g2, mod3, w2, w1)


def _bwd_mix(dmixed, zgate, ya, yb, u1, lng, lnb, wout, wo_p, wpw, tm, swap=()):
    t, d = ya.shape
    _, _, npad = _layout(d)
    nw = len(swap)
    n_steps = t // tm

    def body(dm_ref, zg_ref, ya_ref, yb_ref, u1_ref, lng_ref, lnb_ref, wout_ref, wo_ref, wpw_ref, *rest):
        dya_ref, dyb_ref, dz_ref, do_ref, du1_ref, dlng_ref, dlnb_ref, dcb_ref = rest[nw:nw + 8]
        i = pl.program_id(0)
        if nw:
            start, finish = _swap_phases(rest[:nw], rest[nw + 8:2 * nw + 8], *rest[2 * nw + 8:])
            pl.when(i == 0)(start)
        dmpre = _dot_nt(dm_ref[...], wout_ref[...])
        zg = zg_ref[...]
        ga = jax.nn.sigmoid(zg[:, :d])
        gb = jax.nn.sigmoid(zg[:, d:])
        dya = (dmpre * ga).astype(BF16)
        dyb = (dmpre * gb).astype(BF16)
        dya_ref[...] = dya
        dyb_ref[...] = dyb
        dz_ref[:, :d] = ((dmpre * ya_ref[...]) * (ga * (1.0 - ga))).astype(BF16)
        dz_ref[:, d:] = ((dmpre * yb_ref[...]) * (gb * (1.0 - gb))).astype(BF16)
        do_ref[...] = _dot_nt(dya, wo_ref[...]).astype(BF16)
        du3 = _dot_nt(dyb, wpw_ref[...])
        u1 = u1_ref[...]
        mu = jnp.mean(u1, axis=-1, keepdims=True)
        xc = u1 - mu
        rstd = lax.rsqrt(jnp.mean(xc * xc, axis=-1, keepdims=True) + EPS)
        nh = xc * rstd
        l = nh * lng_ref[...] + lnb_ref[...]
        sg = jax.nn.sigmoid(l)
        dl = du3 * (sg * (1.0 + l * (1.0 - sg)))
        _acc(dlng_ref, jnp.sum(dl * nh, axis=0, keepdims=True), i == 0)
        _acc(dlnb_ref, jnp.sum(dl, axis=0, keepdims=True), i == 0)
        dnh = dl * lng_ref[...]
        du1 = rstd * (dnh - jnp.mean(dnh, axis=-1, keepdims=True) - nh * jnp.mean(dnh * nh, axis=-1, keepdims=True))
        du1_ref[...] = du1
        _acc(dcb_ref, jnp.sum(du1, axis=0, keepdims=True), i == 0)
        if nw:
            pl.when(i == n_steps - 1)(finish)

    cv = _sds((1, CONV_CH), F32)
    res = pl.pallas_call(
        body, name="bwd_mix", grid=(n_steps,),
        out_shape=(_sds((t, d), BF16), _sds((t, d), BF16), _sds((t, npad), BF16), _sds((t, HW), BF16),
                   _sds((t, CONV_CH), F32), cv, cv, cv) + _swap_shapes(swap),
        in_specs=[_row(tm, d), _row(tm, 2 * d), _row(tm, d), _row(tm, d), _row(tm, CONV_CH), _full((1, CONV_CH)),
                  _full((1, CONV_CH)), _full(wout.shape), _full(wo_p.shape), _full(wpw.shape)] + [ANY] * nw,
        out_specs=(_row(tm, d), _row(tm, d), _row(tm, 2 * d), _row(tm, HW), _row(tm, CONV_CH),
                   _full((1, CONV_CH)), _full((1, CONV_CH)), _full((1, CONV_CH))) + (ANY,) * nw,
        scratch_shapes=_swap_sems(swap) if nw else [],
        compiler_params=_params("arbitrary"),
    )(dmixed, zgate, ya, yb, u1, lng, lnb, wout, wo_p, wpw, *swap)
    return res[:8] + (res[8:],)


def _bwd_conv(dz, du1, u0, zglu, cw, tm, tps):
    t = du1.shape[0]
    d = (dz.shape[1] - MLA_IN - 2 * CONV_CH) // 2
    p_glu, _, _ = _layout(d)
    hpt = tm // HALO
    last_blk = t // HALO - 1
    cwc = _by_lane_chunk(cw)

    def body(dz_hbm, du_ref, dun_ref, u_ref, uh_ref, zl_ref, cw_ref, dzl_ref, dcw_ref, ext_ref, dext_ref, dcw8_ref,
             du0_ref):
        i = pl.program_id(0)
        first = (i % tps) == 0
        last = (i % tps) == (tps - 1)
        _fill_shifted(ext_ref, jnp.where(first, 0.0, uh_ref[...]), u_ref[...])
        _fill_shifted(dext_ref, du_ref[...], jnp.where(last, 0.0, dun_ref[...]))

        @pl.when(i == 0)
        def _():
            dcw8_ref[...] = jnp.zeros_like(dcw8_ref)

        groups = CONV_ROWS // SUBLANES

        def conv_chunk(c, carry):
            lc, r0 = _conv_chunk(c)
            du = _shifted(dext_ref, 0, lc, r0)
            du0 = jnp.zeros((CONV_ROWS, LANES), F32)
            for kk in range(CONV_W):
                prod = du * _shifted(ext_ref, HALO - (CONV_W - 1) + kk, lc, r0)
                part = prod[:SUBLANES]
                for g in range(1, groups):
                    part = part + prod[g * SUBLANES:(g + 1) * SUBLANES]
                dcw8_ref[lc, kk] += part
                du0 = du0 + cw_ref[lc, kk:kk + 1, :] * _shifted(dext_ref, CONV_W - 1 - kk, lc, r0)
            du0_ref[lc, pl.ds(r0, CONV_ROWS), :] = du0
            return carry

        lax.fori_loop(0, CONV_LC * (tm // CONV_ROWS), conv_chunk, 0)

        @pl.when(i == pl.num_programs(0) - 1)
        def _():
            for lc, ls in _lane_chunks():
                dcw_ref[:, ls] = jnp.sum(dcw8_ref[lc], axis=1)

        for lc, ls in _lane_chunks():
            du0 = du0_ref[lc]
            ga = zl_ref[:, ls]
            sb = jax.nn.sigmoid(zl_ref[:, CONV_CH + lc * LANES:CONV_CH + (lc + 1) * LANES])
            dzl_ref[:, ls] = (du0 * sb).astype(BF16)
            dzl_ref[:, CONV_CH + lc * LANES:CONV_CH + (lc + 1) * LANES] = ((du0 * ga) * (sb * (1.0 - sb))).astype(BF16)

    prev = pl.BlockSpec((HALO, CONV_CH), lambda i: (jnp.maximum(i * hpt - 1, 0), 0))
    nxt = pl.BlockSpec((HALO, CONV_CH), lambda i: (jnp.minimum((i + 1) * hpt, last_blk), 0))
    glu_blk = p_glu // (2 * CONV_CH)
    return pl.pallas_call(
        body, name="bwd_conv", grid=(t // tm,),
        out_shape=(_sds(dz.shape, BF16), _sds(cw.shape, F32)),
        in_specs=[ANY, _row(tm, CONV_CH), nxt, _row(tm, CONV_CH), prev, _row(tm, 2 * CONV_CH), _full(cwc.shape)],
        out_specs=(pl.BlockSpec((tm, 2 * CONV_CH), lambda i: (i, glu_blk)), _full(cw.shape)),
        scratch_shapes=[pltpu.VMEM(_shifted_shape(tm), F32)] * 2
        + [pltpu.VMEM((CONV_LC, HALO, SUBLANES, LANES), F32), pltpu.VMEM((CONV_LC, tm, LANES), F32)],
        input_output_aliases={0: 0},
        compiler_params=_params("arbitrary"),
    )(dz, du1, du1, u0, u0, zglu, cwc)


def _attn_bwd(q, k, v, do, nseq, seq, scatter=()):
    t = q.shape[0]
    ns = len(scatter)
    blk = pl.BlockSpec((seq, LANES), lambda b, h: (b, h))
    n_steps = nseq * N_HEADS

    def body(q_ref, k_ref, v_ref, do_ref, *rest):
        dq_ref, dk_ref, dv_ref = rest[ns:ns + 3]
        dka_ref, dva_ref = rest[2 * ns + 3:2 * ns + 5]
        if ns:
            start, finish = _scatter_phases(rest[:ns], rest[ns + 3:2 * ns + 3], *rest[2 * ns + 5:])
            step = pl.program_id(0) * N_HEADS + pl.program_id(1)
            pl.when(step == 0)(start)
        dka_ref[...] = jnp.zeros_like(dka_ref)
        dva_ref[...] = jnp.zeros_like(dva_ref)
        mask = _diag_mask()
        nb = seq // BQ
        block = lambda j: (_scores(q_ref[j * BQ:(j + 1) * BQ, :], k_ref, j * BQ, (j + 1) * BQ),
                           _scores(do_ref[j * BQ:(j + 1) * BQ, :], v_ref, j * BQ, (j + 1) * BQ))
        ahead = [block(j) for j in range(min(AHEAD, nb))]
        for i in range(nb):
            lo, e = i * BQ, (i + 1) * BQ
            q_i = q_ref[lo:e, :]
            do_i = do_ref[lo:e, :]
            scores, (dpp, dpd) = ahead.pop(0)
            if i + AHEAD < nb:
                ahead.append(block(i + AHEAD))
            pp, pd, l = _softmax_parts(scores, mask)
            inv = 1.0 / l
            pd = pd * inv
            delta = jnp.sum(pd * dpd, axis=-1, keepdims=True)
            if lo:
                pp = pp * inv
                delta = delta + jnp.sum(pp * dpp, axis=-1, keepdims=True)
            dsd = (pd * (dpd - delta)).astype(BF16)
            dq = _dot(dsd, k_ref[lo:e, :])
            dka_ref[lo:e, :] += _dot_tn(dsd, q_i)
            dva_ref[lo:e, :] += _dot_tn(pd.astype(BF16), do_i)
            if lo:
                dsp = (pp * (dpp - delta)).astype(BF16)
                dq = dq + _dot(dsp, k_ref[:lo, :])
                dka_ref[:lo, :] += _dot_tn(dsp, q_i)
                dva_ref[:lo, :] += _dot_tn(pp.astype(BF16), do_i)
            dq_ref[lo:e, :] = dq * SM_SCALE
        dk_ref[...] = dka_ref[...] * SM_SCALE
        dv_ref[...] = dva_ref[...].astype(BF16)
        if ns:
            pl.when(step == n_steps - 1)(finish)

    res = pl.pallas_call(
        body, name="attn_bwd", grid=(nseq, N_HEADS),
        out_shape=(_sds((t, HW), F32), _sds((t, HW), F32), _sds((t, HW), BF16)) + _scatter_shapes(scatter),
        in_specs=[blk] * 4 + [ANY] * ns, out_specs=(blk,) * 3 + (ANY,) * ns,
        scratch_shapes=[pltpu.VMEM((seq, LANES), F32), pltpu.VMEM((seq, LANES), F32)]
        + (_scatter_sems(ns) if ns else []),
        compiler_params=_params("arbitrary", "arbitrary"),
    )(q, k, v, do, *scatter)
    return res[0], res[1], res[2], res[3:]


def _mla_bwd(dz, dq, dk, dv, zm, gql, gkvl, gq, gk, tabs, wuq_p, wk_p, wv_p, tm, tps):
    t = zm.shape[0]
    d = (dz.shape[1] - MLA_IN - 2 * CONV_CH) // 2
    _, p_q, _ = _layout(d)
    c_t, s1_t, s2_t = tabs
    tab = pl.BlockSpec((tm, LANES), lambda i: (i % tps, 0))

    def body(dz_hbm, dq_ref, dk_ref, dv_ref, zm_ref, gql_ref, gkvl_ref, gq_ref, gk_ref, c_ref, s1_ref, s2_ref,
             wuq_ref, wk_ref, wv_ref,
             dzm_ref, dqpre_ref, dkh_ref, dgq_ref, dgk_ref, dgql_ref, dgkvl_ref):
        i = pl.program_id(0)
        c, s1, s2 = c_ref[...], s1_ref[...], s2_ref[...]
        nq, rq = _rms(zm_ref[:, :Q_RANK])
        qpre = _dot((nq * gql_ref[...]).astype(BF16), wuq_ref[...])
        nkv, rkv = _rms(zm_ref[:, Q_RANK:OFF_KV])
        knope = _dot((nkv * gkvl_ref[...]).astype(BF16), wk_ref[...])
        zkr_v = zm_ref[:, OFF_KV:]
        gk = gk_ref[...]
        kr_roped = _rope(zkr_v * gk, c, s1, s2)
        dgq = jnp.zeros((1, LANES), F32)
        dgk = jnp.zeros((1, LANES), F32)
        dzkr = jnp.zeros((tm, LANES), F32)
        dt_sum = jnp.zeros((tm, LANES), F32)
        slabs = [slice(hd * LANES, (hd + 1) * LANES) for hd in range(N_HEADS)]
        gq = gq_ref[...]
        rqh = [_head_rms(qpre[:, sl])[1] for sl in slabs]
        rkh = [_head_rms(knope[:, sl] + zkr_v)[1] for sl in slabs]
        dyr = [_rope_t(dq_ref[:, sl], c, s1, s2) for sl in slabs]
        nqh = [qpre[:, sl] * rqh[hd] for hd, sl in enumerate(slabs)]
        sq = [jnp.sum((dyr[hd] * gq) * nqh[hd], axis=-1, keepdims=True) for hd in range(N_HEADS)]
        dr = [jnp.sum(dk_ref[:, sl] * (knope[:, sl] * gk + kr_roped), axis=-1, keepdims=True) for sl in slabs]
        for hd, sl in enumerate(slabs):
            dgq = dgq + jnp.sum(dyr[hd] * nqh[hd], axis=0, keepdims=True)
            dqpre_ref[:, sl] = (rqh[hd] * (dyr[hd] * gq - nqh[hd] * (sq[hd] * (1.0 / QK_HEAD)))).astype(BF16)
            kn = knope[:, sl]
            r = rkh[hd]
            dt = dk_ref[:, sl] * r
            via_r = (dr[hd] * (r * r * r) * (-1.0 / QK_HEAD)) * (kn + zkr_v)
            dgk = dgk + jnp.sum(dt * kn, axis=0, keepdims=True)
            dt_sum = dt_sum + dt
            dzkr = dzkr + via_r
            dkh_ref[:, sl] = (dt * gk + via_r).astype(BF16)
        de = _rope_t(dt_sum, c, s1, s2)
        dzkr = dzkr + de * gk
        dgk = dgk + jnp.sum(de * zkr_v, axis=0, keepdims=True)
        _acc(dgq_ref, dgq[:, :QK_HEAD], i == 0)
        _acc(dgk_ref, dgk[:, :QK_HEAD], i == 0)
        dzm_ref[:, OFF_KV:] = dzkr.astype(BF16)
        dqln = _dot_nt(dqpre_ref[...], wuq_ref[...])
        _acc(dgql_ref, jnp.sum(dqln * nq, axis=0, keepdims=True), i == 0)
        dzm_ref[:, :Q_RANK] = _rms_bwd(nq, rq, dqln * gql_ref[...]).astype(BF16)
        dkvn = _dot_nt(dkh_ref[...], wk_ref[...]) + _dot_nt(dv_ref[...], wv_ref[...])
        _acc(dgkvl_ref, jnp.sum(dkvn * nkv, axis=0, keepdims=True), i == 0)
        dzm_ref[:, Q_RANK:OFF_KV] = _rms_bwd(nkv, rkv, dkvn * gkvl_ref[...]).astype(BF16)

    return pl.pallas_call(
        body, name="mla_bwd", grid=(t // tm,),
        out_shape=(_sds(dz.shape, BF16), _sds((t, HW), BF16), _sds((t, HW), BF16), _sds((1, QK_HEAD), F32),
                   _sds((1, QK_HEAD), F32), _sds((1, Q_RANK), F32), _sds((1, KV_RANK), F32)),
        in_specs=[ANY, _row(tm, HW), _row(tm, HW), _row(tm, HW), _row(tm, MLA_IN),
                  _full((1, Q_RANK)), _full((1, KV_RANK)), _full((1, LANES)), _full((1, LANES)), tab, tab, tab,
                  _full(wuq_p.shape), _full(wk_p.shape), _full(wv_p.shape)],
        out_specs=(pl.BlockSpec((tm, MLA_IN), lambda i: (i, p_q // MLA_IN)), _row(tm, HW), _row(tm, HW),
                   _full((1, QK_HEAD)), _full((1, QK_HEAD)), _full((1, Q_RANK)), _full((1, KV_RANK))),
        input_output_aliases={0: 0},
        compiler_params=_params("arbitrary"),
    )(dz, dq, dk, dv, zm, gql, gkvl, gq, gk, c_t, s1_t, s2_t, wuq_p, wk_p, wv_p)


def _bwd_in(dz, x, dx1, g1, mod3, win_p, tm, tps, scatter=()):
    t, d = x.shape
    npad = dz.shape[1]

    ns = len(scatter)
    n_steps = t // tm

    def body(dz_ref, x_ref, dx1_ref, g_ref, mod_ref, wt_hbm, *rest):
        gx_ref, dshift_ref, dscale_ref, dg1_ref = rest[ns:ns + 4]
        wt_ref = rest[2 * ns + 4]
        i = pl.program_id(0)
        if ns:
            start, finish = _scatter_phases(rest[:ns], rest[ns + 4:2 * ns + 4], *rest[2 * ns + 5:])
            pl.when(i == 0)(start)
        _load_resident(i, [(wt_hbm, wt_ref)])
        first_seq = (i % tps) == 0
        dh = _dot_nt(dz_ref[...], wt_ref[...])
        n, r = _rms(x_ref[...])
        g = g_ref[...]
        sc1 = 1.0 + mod_ref[1:2, :]
        _acc(dshift_ref, jnp.sum(dh, axis=0, keepdims=True), first_seq)
        _acc(dscale_ref, jnp.sum(dh * (n * g), axis=0, keepdims=True), first_seq)
        _acc(dg1_ref, jnp.sum((dh * sc1) * n, axis=0, keepdims=True), i == 0)
        gx_ref[...] = dx1_ref[...] + _rms_bwd(n, r, (dh * sc1) * g)
        if ns:
            pl.when(i == n_steps - 1)(finish)

    nseq = t // (tm * tps)
    sv = _sds((nseq, 1, d), F32)
    res = pl.pallas_call(
        body, name="bwd_in", grid=(n_steps,),
        out_shape=(_sds((t, d), F32), sv, sv, _sds((1, d), F32)) + _scatter_shapes(scatter),
        in_specs=[_row(tm, npad), _row(tm, d), _row(tm, d), _full((1, d)), _modspec(d, tps), ANY] + [ANY] * ns,
        out_specs=(_row(tm, d), _seqv(d, tps), _seqv(d, tps), _full((1, d))) + (ANY,) * ns,
        scratch_shapes=[pltpu.VMEM(win_p.shape, BF16)] + (_scatter_sems(ns) if ns else []),
        compiler_params=_params("arbitrary"),
    )(dz, x, dx1, g1, mod3, win_p, *scatter)
    return res[0], res[1], res[2], res[3], res[4:]


def _tile_of(n, choices):
    for c in choices:
        if n % c == 0:
            return c
    return n


def _tn_matmul(a, b, name, col_shards=0):
    t, k = a.shape
    n = b.shape[1]
    tk = _tile_of(k, (1024, 512, 256, 128))
    tn = n // col_shards if col_shards else _tile_of(n, (1024, 896, 768, 512, 384, 256, 128))
    tt = _tile_of(t, (4096, 2048, 1024, 512, 256))

    def body(a_ref, b_ref, o_ref):
        _acc(o_ref, _dot_tn(a_ref[...], b_ref[...]), pl.program_id(2) == 0)

    if col_shards:
        out_shape, out_spec = _sds((col_shards, k, tn), F32), pl.BlockSpec((None, tk, tn), lambda i, j, s: (j, i, 0))
    else:
        out_shape, out_spec = _sds((k, n), F32), pl.BlockSpec((tk, tn), lambda i, j, s: (i, j))
    return pl.pallas_call(
        body, name=name, grid=(k // tk, n // tn, t // tt), out_shape=out_shape,
        in_specs=[pl.BlockSpec((tt, tk), lambda i, j, s: (s, i)), pl.BlockSpec((tt, tn), lambda i, j, s: (s, j))],
        out_specs=out_spec, compiler_params=_params("arbitrary", "arbitrary", "arbitrary"),
    )(a, b)


N_SHARD = 4
COL_SHARDED = ("w_in", "w_uq", "w_ukv", "w_o_mla", "w_pw_out", "w_ff1")
ROW_SHARDED = ("w_out", "w_ff2")
BIG = ("w_in", "w_uq", "w_ukv", "w_o_mla", "w_pw_out", "w_out", "w_ff1", "w_ff2")
SMALL = ("norm1_g", "q_latent_g", "kv_latent_g", "qk_norm_q_g", "qk_norm_k_g", "conv_b", "conv_ln_g", "conv_ln_b",
         "norm2_g")
WEIGHTS = ("w_ada", "b_ada", "norm1_g", "w_in", "q_latent_g", "w_uq", "kv_latent_g", "w_ukv", "qk_norm_q_g",
           "qk_norm_k_g", "w_o_mla", "conv_w", "conv_b", "conv_ln_g", "conv_ln_b", "w_pw_out", "w_out", "norm2_g",
           "w_ff1", "w_ff2")


def _pad_heads(w, width):
    k = w.shape[0]
    w3 = w.reshape(k, N_HEADS, width)
    return jnp.pad(w3, ((0, 0), (0, 0), (0, LANES - width))).reshape(k, HW)


def _unpad_heads(g, width):
    k = g.shape[0]
    return g.reshape(k, N_HEADS, LANES)[:, :, :width].reshape(k, N_HEADS * width)


def _pad_win(w):
    d = w.shape[0]
    z = lambda n: jnp.zeros((d, n), w.dtype)
    return jnp.concatenate([w[:, OFF_GLU:], w[:, OFF_KR:OFF_GLU], w[:, :OFF_KV], z(KR_LANE), w[:, OFF_KV:OFF_KR],
                            z(LANES - KR_LANE - QK_ROPE)], axis=1)


def _unpad_win(g):
    d = g.shape[0]
    p_glu, p_q, _ = _layout(d)
    kr = p_q + OFF_KV + KR_LANE
    return jnp.concatenate([g[:, p_q:p_q + OFF_KV], g[:, kr:kr + QK_ROPE], g[:, p_glu:p_q], g[:, :p_glu]], axis=1)


def _col_shards(g):
    k, n = g.shape
    return g.reshape(k, N_SHARD, n // N_SHARD).transpose(1, 0, 2)


def _from_shards(g, name):
    ns, ks, nn = g.shape
    if name in ROW_SHARDED:
        return g.reshape(ns * ks, nn)
    return g.transpose(1, 0, 2).reshape(ks, ns * nn)


EARLY = ("w_in", "w_uq", "w_ukv")
LATE = ("w_o_mla", "w_pw_out", "w_out", "w_ff1", "w_ff2")


def _assemble(names, gathered):
    return {n: _from_shards(g.reshape((N_SHARD, 2 * g.shape[1]) + g.shape[2:]), n) for n, g in zip(names, gathered)}


GROUP_A = ("w_out", "w_ff1", "w_ff2")
GROUP_B = ("w_in", "w_uq", "w_ukv", "w_o_mla", "w_pw_out")


def _pair_halves(g):
    return g.reshape(N_SHARD, 2, g.shape[1] // 2, g.shape[2])


def _pair_sums(names, halves, from_sibling):
    if not halves:
        return []
    cidx = lax.axis_index("c").reshape(1).astype(jnp.int32)
    return [_add_pair(g, l, cidx, "pair_sum_" + n) for n, g, l in zip(names, halves, from_sibling)]


def _local_step(x, target, mod, sp, w, late=None, tm=256):
    comm = late is not None
    w = dict(w)
    nseq, seq, d = x.shape
    t = nseq * seq
    tps = seq // tm
    xf = x.reshape(t, d)
    tg = target.reshape(t, d)
    mod3 = mod.reshape(nseq, N_MOD, d)

    win_p = _pad_win(w["w_in"])
    wuq_p = _pad_heads(w["w_uq"], QK_HEAD)
    wkv3 = w["w_ukv"].reshape(KV_RANK, N_HEADS, QK_NOPE + V_HEAD)
    wk_p = _pad_heads(wkv3[:, :, :QK_NOPE].reshape(KV_RANK, -1), QK_NOPE)
    wv_p = _pad_heads(wkv3[:, :, QK_NOPE:].reshape(KV_RANK, -1), V_HEAD)
    cw = jnp.pad(w["conv_w"], ((0, HALO - CONV_W), (0, 0)))
    pad_g = lambda g: jnp.pad(g, ((0, 0), (0, LANES - QK_HEAD)))
    gq, gk = pad_g(sp["qk_norm_q_g"]), pad_g(sp["qk_norm_k_g"])
    tabs = _rope_tables(seq)

    tm_in, tps_in = (2 * tm, tps // 2) if tps % 2 == 0 else (tm, tps)
    h, zm, zglu, zgate, u0 = _fwd_in(xf, sp["norm1_g"], mod3, win_p, tm_in, tps_in)
    q, k, v, qln, kvn = _mla_prep(zm, sp["q_latent_g"], sp["kv_latent_g"], gq, gk, tabs, wuq_p, wk_p, wv_p, tm, tps)
    attn, gathered = _attn_fwd(q, k, v, nseq, seq, tuple(late) if comm else ())
    if comm:
        w.update(_assemble(LATE, gathered))
    wo_p = jnp.pad(w["w_o_mla"].reshape(N_HEADS, V_HEAD, d), ((0, 0), (0, LANES - V_HEAD), (0, 0))).reshape(HW, d)
    x1, mixed, mpre, ya, yb, u1, u3 = _fwd_mix(attn, u0, zgate, xf, mod3, wo_p, cw, sp["conv_b"], sp["conv_ln_g"],
                                               sp["conv_ln_b"], w["w_pw_out"], w["w_out"], tm, tps)
    h2, a, r, dy, df, dgate2, loss_acc = _fwd_ffn(x1, tg, sp["norm2_g"], mod3, w["w_ff1"], w["w_ff2"], tm, tps)
    da, dx1, dmixed, dshift2, dscale2, dgate1, dg2 = _bwd_ffn(df, a, x1, dy, mixed, sp["norm2_g"], mod3,
                                                              w["w_ff2"], w["w_ff1"], tm, tps)
    gw = {
        "w_out": _tn_matmul(mpre, dmixed, "dw_out").reshape(N_SHARD, d // N_SHARD, d),
        "w_ff1": _tn_matmul(h2, da, "dw_ff1", N_SHARD),
        "w_ff2": _tn_matmul(r, df, "dw_ff2").reshape(N_SHARD, -1, d),
    }
    halves_a = [_pair_halves(gw[n]) for n in GROUP_A] if comm else []
    dya, dyb, dz, do, du1, dlng, dlnb, dcb, from_sibling = _bwd_mix(
        dmixed, zgate, ya, yb, u1, sp["conv_ln_g"], sp["conv_ln_b"], w["w_out"], wo_p, w["w_pw_out"], tm, tuple(halves_a))
    pair_a = _pair_sums(GROUP_A, halves_a, from_sibling)
    dz, dcw = _bwd_conv(dz, du1, u0, zglu, cw, tm, tps)
    gw["conv_w"] = dcw
    dq, dk, dv, land_a = _attn_bwd(q, k, v, do, nseq, seq, tuple(p[1] for p in pair_a))
    dz, dqpre, dkh, dgq, dgk, dgql, dgkvl = _mla_bwd(dz, dq, dk, dv, zm, sp["q_latent_g"], sp["kv_latent_g"], gq, gk,
                                                      tabs, wuq_p, wk_p, wv_p, tm, tps)
    dwk_p = _tn_matmul(kvn, dkh, "dw_uk")
    dwv_p = _tn_matmul(kvn, dv, "dw_uv")
    dwkv = jnp.concatenate([dwk_p.reshape(KV_RANK, N_HEADS, LANES)[:, :, :QK_NOPE],
                            dwv_p.reshape(KV_RANK, N_HEADS, LANES)[:, :, :V_HEAD]], axis=2).reshape(KV_RANK, -1)
    dwo = _tn_matmul(attn, dya, "dw_o").reshape(N_HEADS, LANES, d)[:, :V_HEAD].reshape(MLA_WIDTH, d)
    gw["w_in"] = _col_shards(_unpad_win(_tn_matmul(h, dz, "dw_in")))
    gw["w_uq"] = _col_shards(_unpad_heads(_tn_matmul(qln, dqpre, "dw_uq"), QK_HEAD))
    gw["w_ukv"] = _col_shards(dwkv)
    gw["w_o_mla"] = _col_shards(dwo)
    gw["w_pw_out"] = _tn_matmul(u3, dyb, "dw_pw", N_SHARD)
    pair_b = []
    if comm:
        halves_b = [_pair_halves(gw[n]) for n in GROUP_B]
        pair_b = _pair_sums(GROUP_B, halves_b, _pair_swap(halves_b, "grad_pair_swap"))
    gx, dshift1, dscale1, dg1, land_b = _bwd_in(dz, xf, dx1, sp["norm1_g"], mod3, win_p, tm_in, tps_in,
                                                tuple(p[1] for p in pair_b))
    if comm:
        for n, p, l in zip(GROUP_A + GROUP_B, pair_a + pair_b, land_a + land_b):
            gw[n] = (p[0], l)
    gs = {
        "norm1_g": dg1, "q_latent_g": dgql, "kv_latent_g": dgkvl, "qk_norm_q_g": dgq, "qk_norm_k_g": dgk,
        "conv_b": dcb, "conv_ln_g": dlng, "conv_ln_b": dlnb, "norm2_g": dg2,
    }
    dmod = jnp.concatenate([dshift1, dscale1, dgate1, dshift2, dscale2, dgate2], axis=2).reshape(nseq, N_MOD * d)
    return loss_acc, gx.reshape(nseq, seq, d), dmod, gw, gs


def kernel(x, c, w_ada, b_ada, norm1_g, w_in, q_latent_g, w_uq, kv_latent_g, w_ukv, qk_norm_q_g, qk_norm_k_g, w_o_mla, conv_w, conv_b, conv_ln_g, conv_ln_b, w_pw_out, w_out, norm2_g, w_ff1, w_ff2, loss_target, m_w_ada, m_b_ada, m_norm1_g, m_w_in, m_q_latent_g, m_w_uq, m_kv_latent_g, m_w_ukv, m_qk_norm_q_g, m_qk_norm_k_g, m_w_o_mla, m_conv_w, m_conv_b, m_conv_ln_g, m_conv_ln_b, m_w_pw_out, m_w_out, m_norm2_g, m_w_ff1, m_w_ff2, v_w_ada, v_b_ada, v_norm1_g, v_w_in, v_q_latent_g, v_w_uq, v_kv_latent_g, v_w_ukv, v_qk_norm_q_g, v_qk_norm_k_g, v_w_o_mla, v_conv_w, v_conv_b, v_conv_ln_g, v_conv_ln_b, v_w_pw_out, v_w_out, v_norm2_g, v_w_ff1, v_w_ff2):
    given = dict(locals())
    wts = {n: given[n][0] for n in WEIGHTS}
    mom = {n: given["m_" + n][0] for n in WEIGHTS}
    var = {n: given["v_" + n][0] for n in WEIGHTS}
    vec = lambda a: a.reshape(1, -1)
    nseq, seq, d = x.shape
    ix, iy, ic = _place()
    shard = 2 * ix + iy

    half = lambda n: lax.dynamic_slice_in_dim(wts[n].astype(BF16), ic * (wts[n].shape[0] // 2), wts[n].shape[0] // 2,
                                              axis=0)
    gathered = _all_gather8([half(n) for n in EARLY] + [wts["conv_w"], c], "gather_weights")
    full = _assemble(EARLY, gathered)
    full["conv_w"] = _from_shards(gathered[-2][0::2], "conv_w")
    c_all = gathered[-1].reshape(8 * nseq, d)

    n_ada = wts["w_ada"].shape[1]
    b_sh = lax.dynamic_slice_in_dim(vec(wts["b_ada"]), shard * n_ada, n_ada, axis=1)
    mod_sh = _ada_mod(c_all, wts["w_ada"], b_sh)
    hb = 4 * nseq
    mod_blk = lax.dynamic_slice_in_dim(mod_sh, ic * hb, hb, axis=0)
    (mod_all,) = _all_gather8([mod_blk], "gather_mod")
    mod_mine = lax.dynamic_slice_in_dim(mod_all, (2 * iy + ic) * nseq, nseq, axis=1)
    mod = jnp.concatenate([lax.dynamic_index_in_dim(mod_mine, 2 * s + ix, axis=0, keepdims=False)
                           for s in range(N_SHARD)], axis=1)

    sp = {n: vec(wts[n]) for n in SMALL}
    loss_part, grad_x, dmod, gw, gs = _local_step(x, loss_target, mod, sp, full, [half(n) for n in LATE])

    parts = _all_gather8([dmod, gw["conv_w"], loss_part] + [gs[n] for n in SMALL], "gather_small")
    dmod_all = parts[0].reshape(8 * nseq, N_MOD * d)
    dmod_sh = lax.dynamic_slice_in_dim(dmod_all, shard * n_ada, n_ada, axis=1)
    res = _ada_bwd(c_all, dmod_all, dmod_sh, parts[1:])
    grads = {"w_ada": res[0], "b_ada": res[1]}
    n_cw = wts["conv_w"].shape[1]
    grads["conv_w"] = lax.dynamic_slice_in_dim(res[2], shard * n_cw, n_cw, axis=1)[:CONV_W]
    loss = res[3][0, 0]
    for n, g in zip(SMALL, res[4:]):
        grads[n] = g

    own_c = jnp.stack([shard, ic]).astype(jnp.int32)
    mine_sum = [_add_chips(gw[n][0], gw[n][1], own_c, "chip_sum_" + n) for n in BIG]
    for n, g in zip(BIG, _pair_gather(mine_sum, "grad_pair_gather")):
        grads[n] = g.reshape(wts[n].shape)

    delta, new_m, new_v = {}, {}, {}
    for n in BIG + ("w_ada",):
        delta[n], new_m[n], new_v[n] = _adamw(wts[n], grads[n], mom[n], var[n], "adamw_" + n)
    rest = ("b_ada", "conv_w") + SMALL
    as2d = lambda a: a if a.ndim == 2 else vec(a)
    res = _adamw_small(*[[as2d(t[n]) for n in rest] for t in (wts, grads, mom, var)])
    for dst, arrs in zip((delta, new_m, new_v), res):
        for n, a in zip(rest, arrs):
            dst[n] = a

    outs = [loss, grad_x]
    for group in (grads, delta, new_m, new_v):
        outs += [group[n].reshape(given[n].shape) for n in WEIGHTS]
    return tuple(outs)
```

```python
import jax
import jax.numpy as jnp
from jax import lax
from jax.experimental import pallas as pl
from jax.experimental.pallas import tpu as pltpu

F32 = jnp.float32
BF16 = jnp.bfloat16
MESH = pl.DeviceIdType.MESH
ANY = pl.BlockSpec(memory_space=pl.ANY)

CHUNK = 64
CHUNK_SHIFT = 6
N_HEADS = 8
QK_NOPE = 64
QK_ROPE = 32
QK_HEAD = QK_NOPE + QK_ROPE
V_HEAD = 64
Q_RANK = 256
KV_RANK = 128
MLA_WIDTH = N_HEADS * V_HEAD
CONV_CH = 512
CONV_W = 31
ROPE_THETA = 10000.0
EPS = 1e-6
LANES = 128
SUBLANES = 8
HW = N_HEADS * LANES
OFF_KV = Q_RANK + KV_RANK
OFF_KR = OFF_KV + QK_ROPE
OFF_GLU = OFF_KR + 2 * CONV_CH
KR_LANE = QK_NOPE
MLA_IN = Q_RANK + KV_RANK + LANES
HALO = 32
N_MOD = 6

ADAM_LR = 0.001
ADAM_B1 = 0.9
ADAM_B2 = 0.999
ADAM_EPS = 1e-08
ADAM_WD = 0.01
ADAM_STEP = 10

VMEM_LIMIT = 56 * 1024 * 1024
BQ = 256


def _layout(d):
    p_glu = 2 * d
    p_q = p_glu + 2 * CONV_CH
    return p_glu, p_q, p_q + MLA_IN


def _params(*sem):
    return pltpu.CompilerParams(dimension_semantics=sem, vmem_limit_bytes=VMEM_LIMIT)


def _dot(a, b):
    return jnp.dot(a, b, preferred_element_type=F32)


def _dot_tn(a, b):
    return lax.dot_general(a, b, (((0,), (0,)), ((), ())), preferred_element_type=F32)


def _dot_nt(a, b):
    return lax.dot_general(a, b, (((1,), (1,)), ((), ())), preferred_element_type=F32)


def _acc(ref, val, first):
    @pl.when(first)
    def _():
        ref[...] = val

    @pl.when(jnp.logical_not(first))
    def _():
        ref[...] += val


def _rms(x):
    r = lax.rsqrt(jnp.mean(x * x, axis=-1, keepdims=True) + EPS)
    return x * r, r


def _rms_bwd(n, r, dn):
    return r * (dn - n * jnp.mean(dn * n, axis=-1, keepdims=True))


def _head_rms(sl):
    r = lax.rsqrt(jnp.sum(sl * sl, axis=-1, keepdims=True) * (1.0 / QK_HEAD) + EPS)
    return sl * r, r


def _head_rms_bwd(n, r, dn):
    return r * (dn - n * (jnp.sum(dn * n, axis=-1, keepdims=True) * (1.0 / QK_HEAD)))


def _rope(x, c, s1, s2):
    return x * c + pltpu.roll(x, QK_ROPE // 2, 1) * s1 + pltpu.roll(x, LANES - QK_ROPE // 2, 1) * s2


def _rope_t(dy, c, s1, s2):
    return dy * c + pltpu.roll(dy * s1, LANES - QK_ROPE // 2, 1) + pltpu.roll(dy * s2, QK_ROPE // 2, 1)


def _rope_tables(seq):
    half = QK_ROPE // 2
    inv_freq = ROPE_THETA ** (-jnp.arange(0, QK_ROPE, 2, dtype=F32) / QK_ROPE)
    ang = jnp.arange(seq, dtype=F32)[:, None] * inv_freq[None, :]
    cos, sin = jnp.cos(ang), jnp.sin(ang)
    z = lambda n: jnp.zeros((seq, n), F32)
    tail = LANES - QK_HEAD
    c = jnp.concatenate([jnp.ones((seq, QK_NOPE), F32), cos, cos, jnp.ones((seq, tail), F32)], axis=1)
    s1 = jnp.concatenate([z(QK_NOPE + half), sin, z(tail)], axis=1)
    s2 = jnp.concatenate([z(QK_NOPE), -sin, z(half + tail)], axis=1)
    return c, s1, s2


def _row(tm, w):
    return pl.BlockSpec((tm, w), lambda i: (i, 0))


def _modspec(d, tps):
    return pl.BlockSpec((None, N_MOD, d), lambda i: (i // tps, 0, 0))


def _seqv(w, tps):
    return pl.BlockSpec((None, 1, w), lambda i: (i // tps, 0, 0))


def _full(shape):
    return pl.BlockSpec(shape, lambda i: tuple(0 for _ in shape))


def _sds(shape, dtype):
    return jax.ShapeDtypeStruct(shape, dtype)


CONV_ROWS = 64
CONV_LC = CONV_CH // LANES


def _lane_chunks():
    return [(lc, slice(lc * LANES, (lc + 1) * LANES)) for lc in range(CONV_LC)]


def _fill_shifted(ext_ref, head, body):
    nh = head.shape[0]
    for lc, ls in _lane_chunks():
        ext_ref[0, lc, :nh, :] = head[:, ls]
        ext_ref[0, lc, nh:, :] = body[:, ls]
        rows = ext_ref[0, lc]
        for b in range(1, SUBLANES):
            ext_ref[b, lc] = pltpu.roll(rows, rows.shape[0] - b, 0)


def _shifted_shape(tm):
    return (SUBLANES, CONV_LC, tm + HALO, LANES)


def _conv_chunk(c):
    return c % CONV_LC, pl.multiple_of((c // CONV_LC) * CONV_ROWS, CONV_ROWS)


def _shifted(ext_ref, o, lc, r0):
    a = pl.multiple_of((o // SUBLANES) * SUBLANES + r0, SUBLANES)
    return ext_ref[o % SUBLANES, lc, pl.ds(a, CONV_ROWS), :]


def _by_lane_chunk(a):
    return a.reshape(a.shape[0], CONV_LC, LANES).transpose(1, 0, 2)


def _load_resident(i, pairs):
    @pl.when(i == 0)
    def _():
        for src, dst in pairs:
            pltpu.sync_copy(src, dst)


def _place():
    return lax.axis_index("x"), lax.axis_index("y"), lax.axis_index("c")


def _all_gather8(blocks, name):
    na = len(blocks)

    def body(*refs):
        start, forward, finish = _gather8_phases(refs[:na], refs[na:2 * na], *refs[2 * na:])
        start()
        forward()
        finish()

    outs = pl.pallas_call(
        body, name=name, out_shape=_gather8_shapes(blocks), in_specs=[ANY] * na, out_specs=(ANY,) * na,
        scratch_shapes=_gather8_sems(na),
    )(*blocks)
    return _own_block_placed(outs, blocks)


def _gather8_shapes(blocks):
    return tuple(_sds((8,) + b.shape, b.dtype) for b in blocks)


def _gather8_sems(na):
    return [pltpu.SemaphoreType.DMA((7 * na,)), pltpu.SemaphoreType.DMA((7 * na,))]


def _own_block_placed(outs, blocks):
    ix, iy, ic = _place()
    return tuple(lax.dynamic_update_index_in_dim(o, b, 4 * ix + 2 * iy + ic, 0) for o, b in zip(outs, blocks))


def _gather8_phases(x_refs, out_refs, send_sems, recv_sems):
    na = len(x_refs)
    x, y, c = _place()
    me, sibling = (x, y, c), (x, y, 1 - c)
    chips = [(1 - x, y), (x, 1 - y), (1 - x, 1 - y)]

    def copy(a, k, blk, to, from_input=False):
        dst = out_refs[a].at[4 * blk[0] + 2 * blk[1] + blk[2]]
        return pltpu.make_async_remote_copy(
            src_ref=x_refs[a] if from_input else dst, dst_ref=dst,
            send_sem=send_sems.at[7 * a + k], recv_sem=recv_sems.at[7 * a + k], device_id=to, device_id_type=MESH)

    def first(a):
        return [copy(a, 0, me, sibling, True)] + [copy(a, 1 + j, me, (*chip, c), True) for j, chip in enumerate(chips)]

    def start():
        for a in range(na):
            for cp in first(a):
                cp.start()

    def forward():
        for j, chip in enumerate(chips):
            for a in range(na):
                copy(a, 1 + j, (*chip, c), me).wait_recv()
                copy(a, 4 + j, (*chip, c), sibling).start()

    def finish():
        for a in range(na):
            copy(a, 0, sibling, me).wait_recv()
            for j, chip in enumerate(chips):
                copy(a, 4 + j, (*chip, 1 - c), me).wait_recv()
        for a in range(na):
            for cp in first(a) + [copy(a, 4 + j, (*chip, c), sibling) for j, chip in enumerate(chips)]:
                cp.wait_send()

    return start, forward, finish


def _pair_swap(gs, name):
    na = len(gs)

    def body(*refs):
        start, finish = _swap_phases(refs[:na], refs[na:2 * na], *refs[2 * na:])
        start()
        finish()

    return pl.pallas_call(
        body, name=name, out_shape=_swap_shapes(gs), in_specs=[ANY] * na, out_specs=(ANY,) * na,
        scratch_shapes=_swap_sems(gs),
    )(*gs)


def _swap_shapes(gs):
    return tuple(_sds(g.shape[:1] + g.shape[2:], g.dtype) for g in gs)


def _swap_sems(gs):
    n = sum(g.shape[0] for g in gs)
    return [pltpu.SemaphoreType.DMA((n,)), pltpu.SemaphoreType.DMA((n,))]


def _swap_phases(g_refs, land_refs, send_sems, recv_sems):
    x, y, c = _place()

    def copies():
        cps, k = [], 0
        for g_ref, land_ref in zip(g_refs, land_refs):
            for s in range(g_ref.shape[0]):
                cps.append(pltpu.make_async_remote_copy(
                    src_ref=g_ref.at[s, 1 - c], dst_ref=land_ref.at[s], send_sem=send_sems.at[k],
                    recv_sem=recv_sems.at[k], device_id=(x, y, 1 - c), device_id_type=MESH))
                k += 1
        return cps

    def start():
        for cp in copies():
            cp.start()

    def finish():
        for cp in copies():
            cp.wait()

    return start, finish


def _scatter_shapes(hs):
    return tuple(_sds((3,) + h.shape[1:], h.dtype) for h in hs)


def _scatter_sems(na):
    return [pltpu.SemaphoreType.DMA((3 * na,)), pltpu.SemaphoreType.DMA((3 * na,))]


def _scatter_phases(h_refs, land_refs, send_sems, recv_sems):
    x, y, c = _place()
    chips = [(1 - x, y), (x, 1 - y), (1 - x, 1 - y)]

    def copies():
        return [pltpu.make_async_remote_copy(
            src_ref=h_refs[a].at[2 * tx + ty], dst_ref=land_refs[a].at[j], send_sem=send_sems.at[3 * a + j],
            recv_sem=recv_sems.at[3 * a + j], device_id=(tx, ty, c), device_id_type=MESH)
            for a in range(len(h_refs)) for j, (tx, ty) in enumerate(chips)]

    def start():
        for cp in copies():
            cp.start()

    def finish():
        for cp in copies():
            cp.wait()

    return start, finish


def _pair_gather(fs, name):
    na = len(fs)

    def body(*refs):
        out_refs = refs[na:2 * na]
        send_sems, recv_sems = refs[2 * na:]
        x, y, c = _place()
        sends = [pltpu.make_async_remote_copy(
            src_ref=out_refs[a].at[c], dst_ref=out_refs[a].at[c], send_sem=send_sems.at[a], recv_sem=recv_sems.at[a],
            device_id=(x, y, 1 - c), device_id_type=MESH) for a in range(na)]
        recvs = [pltpu.make_async_remote_copy(
            src_ref=out_refs[a].at[c], dst_ref=out_refs[a].at[1 - c], send_sem=send_sems.at[a],
            recv_sem=recv_sems.at[a], device_id=(x, y, 1 - c), device_id_type=MESH) for a in range(na)]
        for cp in sends:
            cp.start()
        for cp in recvs:
            cp.wait_recv()
        for cp in sends:
            cp.wait_send()

    return pl.pallas_call(
        body, name=name, out_shape=tuple(_sds(f.shape, f.dtype) for f in fs),
        in_specs=[ANY] * na, out_specs=(ANY,) * na, input_output_aliases={a: a for a in range(na)},
        scratch_shapes=[pltpu.SemaphoreType.DMA((na,)), pltpu.SemaphoreType.DMA((na,))],
    )(*fs)


def _row_tile(r, n, itemsize=4, budget=1 << 21):
    if r * n * itemsize <= budget:
        return r
    best = None
    for tr in range(16, r, 16):
        if r % tr == 0 and tr * n * itemsize <= budget:
            best = tr
    assert best is not None, (r, n)
    return best


def _add_pair(g, land, cidx, name):
    ns, _, r, n = g.shape
    tr = _row_tile(r, n)

    def body(c_ref, a_ref, b_ref, o_ref, ob_ref):
        s = a_ref[...] + b_ref[...]
        o_ref[...] = s
        ob_ref[...] = s.astype(BF16)

    out = pl.BlockSpec((None, tr, n), lambda s, i, cr: (s, i, 0))
    return pl.pallas_call(
        body, name=name, out_shape=(_sds((ns, r, n), F32), _sds((ns, r, n), BF16)),
        grid_spec=pltpu.PrefetchScalarGridSpec(
            num_scalar_prefetch=1, grid=(ns, r // tr),
            in_specs=[pl.BlockSpec((None, None, tr, n), lambda s, i, cr: (s, cr[0], i, 0)), out],
            out_specs=(out, out)),
        compiler_params=_params("arbitrary", "arbitrary"),
    )(cidx, g, land)


def _add_chips(h, land, own_c, name):
    _, r, n = h.shape
    tr = _row_tile(r, n)

    def body(o_idx, h_ref, l_ref, o_ref):
        o_ref[...] = ((h_ref[...] + l_ref[0].astype(F32)) + l_ref[1].astype(F32)) + l_ref[2].astype(F32)

    return pl.pallas_call(
        body, name=name, out_shape=_sds((2, r, n), F32),
        grid_spec=pltpu.PrefetchScalarGridSpec(
            num_scalar_prefetch=1, grid=(r // tr,),
            in_specs=[pl.BlockSpec((None, tr, n), lambda i, o: (o[0], i, 0)),
                      pl.BlockSpec((3, tr, n), lambda i, o: (0, i, 0))],
            out_specs=pl.BlockSpec((None, tr, n), lambda i, o: (o[1], i, 0))),
        compiler_params=_params("arbitrary"),
    )(own_c, h, land)


def _adam_math(w, g, m, v):
    nm = ADAM_B1 * m + (1.0 - ADAM_B1) * g
    nv = ADAM_B2 * v + (1.0 - ADAM_B2) * (g * g)
    m_hat = nm / (1.0 - ADAM_B1 ** ADAM_STEP)
    v_hat = nv / (1.0 - ADAM_B2 ** ADAM_STEP)
    return -ADAM_LR * (m_hat / (jnp.sqrt(v_hat) + ADAM_EPS) + ADAM_WD * w), nm, nv


def _adamw(w, g, m, v, name):
    r, n = w.shape
    tr = _row_tile(r, n)

    def body(w_ref, g_ref, m_ref, v_ref, d_ref, nm_ref, nv_ref):
        d_ref[...], nm_ref[...], nv_ref[...] = _adam_math(w_ref[...], g_ref[...], m_ref[...], v_ref[...])

    spec = pl.BlockSpec((tr, n), lambda i: (i, 0))
    return pl.pallas_call(
        body, name=name, out_shape=(_sds((r, n), F32),) * 3, grid=(r // tr,),
        in_specs=[spec] * 4, out_specs=(spec,) * 3, compiler_params=_params("arbitrary"),
    )(w, g, m, v)


def _adamw_small(ws, gs, ms, vs):
    k = len(ws)

    def body(*refs):
        ins, outs = refs[:4 * k], refs[4 * k:]
        for j in range(k):
            d, nm, nv = _adam_math(ins[j][...], ins[k + j][...], ins[2 * k + j][...], ins[3 * k + j][...])
            outs[j][...] = d
            outs[k + j][...] = nm
            outs[2 * k + j][...] = nv

    shapes = tuple(_sds(w.shape, F32) for w in ws)
    res = pl.pallas_call(body, name="adamw_small", out_shape=shapes * 3,
                         compiler_params=pltpu.CompilerParams(vmem_limit_bytes=VMEM_LIMIT))(*ws, *gs, *ms, *vs)
    return res[:k], res[k:2 * k], res[2 * k:]


def _ada_mod(c_all, w_sh, b_sh):
    b, _ = c_all.shape
    n = w_sh.shape[1]

    def body(c_ref, w_ref, b_ref, o_ref):
        cc = c_ref[...]
        ca = (cc * jax.nn.sigmoid(cc)).astype(BF16)
        o_ref[...] = _dot(ca, w_ref[...].astype(BF16)) + b_ref[...]

    return pl.pallas_call(body, name="ada_mod", out_shape=_sds((b, n), F32),
                          compiler_params=pltpu.CompilerParams(vmem_limit_bytes=VMEM_LIMIT))(c_all, w_sh, b_sh)


def _ada_bwd(c_all, dmod_all, dmod_sh, parts):
    b, d = c_all.shape
    n6 = dmod_all.shape[1]
    n = dmod_sh.shape[1]
    k = len(parts)

    def body(*refs):
        c_ref, da_ref, ds_ref = refs[:3]
        p_refs = refs[3:3 + k]
        dw_ref, db_ref = refs[3 + k:5 + k]
        s_refs = refs[5 + k:]
        cc = c_ref[...]
        ca = (cc * jax.nn.sigmoid(cc)).astype(BF16)
        dw_ref[...] = _dot_tn(ca, ds_ref[...].astype(BF16))
        db_ref[...] = jnp.sum(da_ref[...], axis=0, keepdims=True)
        for p_ref, s_ref in zip(p_refs, s_refs):
            tot = p_ref[0]
            for j in range(1, p_ref.shape[0]):
                tot = tot + p_ref[j]
            s_ref[...] = tot

    return pl.pallas_call(
        body, name="ada_bwd",
        out_shape=(_sds((d, n), F32), _sds((1, n6), F32)) + tuple(_sds(p.shape[1:], F32) for p in parts),
        compiler_params=pltpu.CompilerParams(vmem_limit_bytes=VMEM_LIMIT),
    )(c_all, dmod_all, dmod_sh, *parts)


def _fwd_in(x, g1, mod3, win_p, tm, tps):
    t, d = x.shape
    p_glu, p_q, npad = _layout(d)

    def body(x_ref, g_ref, mod_ref, w_hbm, h_ref, zm_ref, zglu_ref, zgate_ref, u0_ref, w_ref):
        _load_resident(pl.program_id(0), [(w_hbm, w_ref)])
        n, _ = _rms(x_ref[...])
        h = ((n * g_ref[...]) * (1.0 + mod_ref[1:2, :]) + mod_ref[0:1, :]).astype(BF16)
        h_ref[...] = h
        z = _dot(h, w_ref[...])
        zgate_ref[...] = z[:, :p_glu]
        zglu = z[:, p_glu:p_q]
        zglu_ref[...] = zglu
        zm_ref[...] = z[:, p_q:]
        u0_ref[...] = zglu[:, :CONV_CH] * jax.nn.sigmoid(zglu[:, CONV_CH:])

    return pl.pallas_call(
        body, name="fwd_in", grid=(t // tm,),
        out_shape=(_sds((t, d), BF16), _sds((t, MLA_IN), F32), _sds((t, 2 * CONV_CH), F32), _sds((t, 2 * d), F32),
                   _sds((t, CONV_CH), F32)),
        in_specs=[_row(tm, d), _full((1, d)), _modspec(d, tps), ANY],
        out_specs=(_row(tm, d), _row(tm, MLA_IN), _row(tm, 2 * CONV_CH), _row(tm, 2 * d), _row(tm, CONV_CH)),
        scratch_shapes=[pltpu.VMEM(win_p.shape, BF16)],
        compiler_params=_params("arbitrary"),
    )(x, g1, mod3, win_p)


def _mla_prep(zm, gql, gkvl, gq, gk, tabs, wuq_p, wk_p, wv_p, tm, tps):
    t = zm.shape[0]
    c_t, s1_t, s2_t = tabs
    tab = pl.BlockSpec((tm, LANES), lambda i: (i % tps, 0))

    def body(zm_ref, gql_ref, gkvl_ref, gq_ref, gk_ref, c_ref, s1_ref, s2_ref, wuq_ref, wk_ref, wv_ref,
             q_ref, k_ref, v_ref, qln_ref, kvn_ref):
        c, s1, s2 = c_ref[...], s1_ref[...], s2_ref[...]
        nq, _ = _rms(zm_ref[:, :Q_RANK])
        qln = (nq * gql_ref[...]).astype(BF16)
        qln_ref[...] = qln
        qpre = _dot(qln, wuq_ref[...])
        nkv, _ = _rms(zm_ref[:, Q_RANK:OFF_KV])
        kvn = (nkv * gkvl_ref[...]).astype(BF16)
        kvn_ref[...] = kvn
        knope = _dot(kvn, wk_ref[...])
        v_ref[...] = _dot(kvn, wv_ref[...]).astype(BF16)
        zkr_v = zm_ref[:, OFF_KV:]
        kr_roped = _rope(zkr_v * gk_ref[...], c, s1, s2)
        slabs = [slice(hd * LANES, (hd + 1) * LANES) for hd in range(N_HEADS)]
        rq = [_head_rms(qpre[:, sl])[1] for sl in slabs]
        rk = [_head_rms(knope[:, sl] + zkr_v)[1] for sl in slabs]
        for hd, sl in enumerate(slabs):
            q_ref[:, sl] = _rope((qpre[:, sl] * rq[hd]) * gq_ref[...], c, s1, s2).astype(BF16)
            k_ref[:, sl] = (rk[hd] * (knope[:, sl] * gk_ref[...] + kr_roped)).astype(BF16)

    return pl.pallas_call(
        body, name="mla_prep", grid=(t // tm,),
        out_shape=(_sds((t, HW), BF16),) * 3 + (_sds((t, Q_RANK), BF16), _sds((t, KV_RANK), BF16)),
        in_specs=[_row(tm, MLA_IN), _full((1, Q_RANK)), _full((1, KV_RANK)),
                  _full((1, LANES)), _full((1, LANES)), tab, tab, tab,
                  _full(wuq_p.shape), _full(wk_p.shape), _full(wv_p.shape)],
        out_specs=(_row(tm, HW),) * 3 + (_row(tm, Q_RANK), _row(tm, KV_RANK)),
        compiler_params=_params("arbitrary"),
    )(zm, gql, gkvl, gq, gk, c_t, s1_t, s2_t, wuq_p, wk_p, wv_p)


AHEAD = 2
ROW_BAND = 256
SM_SCALE = QK_HEAD ** -0.5
EXP2_SCALE = SM_SCALE * 1.4426950408889634


def _diag_mask():
    rc = jnp.right_shift(lax.broadcasted_iota(jnp.int32, (BQ, 1), 0), CHUNK_SHIFT)
    cc = jnp.right_shift(lax.broadcasted_iota(jnp.int32, (1, BQ), 1), CHUNK_SHIFT)
    return rc >= cc


def _scores(q_i, k_ref, lo, e):
    return (_dot_nt(q_i, k_ref[:lo, :]) if lo else None), _dot_nt(q_i, k_ref[lo:e, :])


def _softmax_parts(scores, mask):
    sp, sd = scores
    sd = jnp.where(mask, sd, jnp.finfo(F32).min)
    m = jnp.max(sd, axis=-1, keepdims=True)
    if sp is not None:
        m = jnp.maximum(m, jnp.max(sp, axis=-1, keepdims=True))
    pd = jnp.exp2((sd - m) * EXP2_SCALE)
    l = jnp.sum(pd, axis=-1, keepdims=True)
    pp = None
    if sp is not None:
        pp = jnp.exp2((sp - m) * EXP2_SCALE)
        l = l + jnp.sum(pp, axis=-1, keepdims=True)
    return pp, pd, l


def _attn_fwd(q, k, v, nseq, seq, gather=()):
    t = q.shape[0]
    na = len(gather)
    blk = pl.BlockSpec((seq, LANES), lambda b, h: (b, h))
    n_steps = nseq * N_HEADS

    def body(q_ref, k_ref, v_ref, *rest):
        o_ref = rest[na]
        if na:
            start, forward, finish = _gather8_phases(rest[:na], rest[na + 1:2 * na + 1], *rest[2 * na + 1:])
            step = pl.program_id(0) * N_HEADS + pl.program_id(1)
            pl.when(step == 0)(start)
            pl.when(step == (7 * n_steps) // 8)(forward)
        mask = _diag_mask()
        nb = seq // BQ
        block_scores = lambda j: _scores(q_ref[j * BQ:(j + 1) * BQ, :], k_ref, j * BQ, (j + 1) * BQ)
        ahead = [block_scores(j) for j in range(min(AHEAD, nb))]
        for i in range(nb):
            lo, e = i * BQ, (i + 1) * BQ
            cur = ahead.pop(0)
            if i + AHEAD < nb:
                ahead.append(block_scores(i + AHEAD))
            pp, pd, l = _softmax_parts(cur, mask)
            o = _dot(pd.astype(BF16), v_ref[lo:e, :])
            if lo:
                o = o + _dot(pp.astype(BF16), v_ref[:lo, :])
            o_ref[lo:e, :] = (o * (1.0 / l)).astype(BF16)
        if na:
            pl.when(step == n_steps - 1)(finish)

    res = pl.pallas_call(
        body, name="attn_fwd", grid=(nseq, N_HEADS), out_shape=(_sds((t, HW), BF16),) + _gather8_shapes(gather),
        in_specs=[blk, blk, blk] + [ANY] * na, out_specs=(blk,) + (ANY,) * na,
        scratch_shapes=_gather8_sems(na) if na else [],
        compiler_params=_params("arbitrary", "arbitrary"),
    )(q, k, v, *gather)
    return res[0], (_own_block_placed(res[1:], gather) if na else ())


def _fwd_mix(attn, u0, zgate, x, mod3, wo_p, cw, cb, lng, lnb, wpw, wout, tm, tps):
    t, d = x.shape
    hpt = tm // HALO
    cwc, cbc = _by_lane_chunk(cw), _by_lane_chunk(cb)

    def body(a_ref, u_ref, uh_ref, zg_ref, x_ref, mod_ref, wo_ref, cw_ref, cb_ref, lng_ref, lnb_ref, wpw_ref, wout_ref,
             x1_ref, mixed_ref, mpre_ref, ya_ref, yb_ref, u1_ref, u3_ref, ext_ref):
        i = pl.program_id(0)
        ya = _dot(a_ref[...], wo_ref[...])
        ya_ref[...] = ya
        first = (i % tps) == 0
        _fill_shifted(ext_ref, jnp.where(first, 0.0, uh_ref[...]), u_ref[...])
        for lc, ls in _lane_chunks():
            acc = jnp.broadcast_to(cb_ref[lc], (tm, LANES))
            for kk in range(CONV_W):
                o = HALO - (CONV_W - 1) + kk
                a = (o // SUBLANES) * SUBLANES
                acc = acc + cw_ref[lc, kk:kk + 1, :] * ext_ref[o % SUBLANES, lc, a:a + tm, :]
            u1_ref[:, ls] = acc
        acc = u1_ref[...]
        mu = jnp.mean(acc, axis=-1, keepdims=True)
        xc = acc - mu
        rstd = lax.rsqrt(jnp.mean(xc * xc, axis=-1, keepdims=True) + EPS)
        l = (xc * rstd) * lng_ref[...] + lnb_ref[...]
        u3 = (l * jax.nn.sigmoid(l)).astype(BF16)
        u3_ref[...] = u3
        yb = _dot(u3, wpw_ref[...])
        yb_ref[...] = yb
        zg = zg_ref[...]
        mpre = (jax.nn.sigmoid(zg[:, :d]) * ya + jax.nn.sigmoid(zg[:, d:]) * yb).astype(BF16)
        mpre_ref[...] = mpre
        mixed = _dot(mpre, wout_ref[...])
        mixed_ref[...] = mixed
        x1_ref[...] = x_ref[...] + mod_ref[2:3, :] * mixed

    halo = pl.BlockSpec((HALO, CONV_CH), lambda i: (jnp.maximum(i * hpt - 1, 0), 0))
    return pl.pallas_call(
        body, name="fwd_mix", grid=(t // tm,),
        out_shape=(_sds((t, d), F32), _sds((t, d), F32), _sds((t, d), BF16), _sds((t, d), F32), _sds((t, d), F32),
                   _sds((t, CONV_CH), F32), _sds((t, CONV_CH), BF16)),
        in_specs=[_row(tm, HW), _row(tm, CONV_CH), halo, _row(tm, 2 * d), _row(tm, d), _modspec(d, tps),
                  _full(wo_p.shape), _full(cwc.shape), _full(cbc.shape), _full((1, CONV_CH)), _full((1, CONV_CH)),
                  _full(wpw.shape), _full(wout.shape)],
        out_specs=(_row(tm, d), _row(tm, d), _row(tm, d), _row(tm, d), _row(tm, d), _row(tm, CONV_CH),
                   _row(tm, CONV_CH)),
        scratch_shapes=[pltpu.VMEM(_shifted_shape(tm), F32)],
        compiler_params=_params("arbitrary"),
    )(attn, u0, u0, zgate, x, mod3, wo_p, cwc, cbc, lng, lnb, wpw, wout)


def _fwd_ffn(x1, target, g2, mod3, w1, w2, tm, tps):
    t, d = x1.shape
    dff = w1.shape[1]

    def body(x1_ref, tg_ref, g_ref, mod_ref, w1_hbm, w2_hbm,
             h2_ref, a_ref, r_ref, dy_ref, df_ref, dgate_ref, loss_ref, w1_ref, w2_ref):
        i = pl.program_id(0)
        _load_resident(i, [(w1_hbm, w1_ref), (w2_hbm, w2_ref)])
        x1v = x1_ref[...]
        gate2 = mod_ref[5:6, :]
        n, _ = _rms(x1v)
        h2 = ((n * g_ref[...]) * (1.0 + mod_ref[4:5, :]) + mod_ref[3:4, :]).astype(BF16)
        h2_ref[...] = h2
        a = _dot(h2, w1_ref[...])
        a_ref[...] = a
        r = jnp.square(jnp.maximum(a, 0.0)).astype(BF16)
        r_ref[...] = r
        f = _dot(r, w2_ref[...])
        e = (x1v + gate2 * f) - tg_ref[...]
        part = 0.5 * jnp.sum(jnp.mean(e * e, axis=-1, keepdims=True), axis=0, keepdims=True)
        _acc(loss_ref, jnp.broadcast_to(part, loss_ref.shape), i == 0)
        dy = e * (1.0 / d)
        dy_ref[...] = dy
        df_ref[...] = (dy * gate2).astype(BF16)
        _acc(dgate_ref, jnp.sum(dy * f, axis=0, keepdims=True), (i % tps) == 0)

    nseq = t // (tm * tps)
    return pl.pallas_call(
        body, name="fwd_ffn", grid=(t // tm,),
        out_shape=(_sds((t, d), BF16), _sds((t, dff), F32), _sds((t, dff), BF16), _sds((t, d), F32), _sds((t, d), BF16),
                   _sds((nseq, 1, d), F32), _sds((8, LANES), F32)),
        in_specs=[_row(tm, d), _row(tm, d), _full((1, d)), _modspec(d, tps), ANY, ANY],
        out_specs=(_row(tm, d), _row(tm, dff), _row(tm, dff), _row(tm, d), _row(tm, d), _seqv(d, tps),
                   _full((8, LANES))),
        scratch_shapes=[pltpu.VMEM(w1.shape, BF16), pltpu.VMEM(w2.shape, BF16)],
        compiler_params=_params("arbitrary"),
    )(x1, target, g2, mod3, w1, w2)


def _bwd_ffn(df, a, x1, dy, mixed, g2, mod3, w2, w1, tm, tps):
    t, d = x1.shape
    dff = a.shape[1]

    def body(df_ref, a_ref, x1_ref, dy_ref, mx_ref, g_ref, mod_ref, w2_hbm, w1_hbm,
             da_ref, dx1_ref, dmixed_ref, dshift_ref, dscale_ref, dgate1_ref, dg2_ref, w2_ref, w1_ref):
        i = pl.program_id(0)
        _load_resident(i, [(w2_hbm, w2_ref), (w1_hbm, w1_ref)])
        first_seq = (i % tps) == 0
        dr = _dot_nt(df_ref[...], w2_ref[...])
        da = (dr * (2.0 * jnp.maximum(a_ref[...], 0.0))).astype(BF16)
        da_ref[...] = da
        dh2 = _dot_nt(da, w1_ref[...])
        n, r = _rms(x1_ref[...])
        g = g_ref[...]
        sc1 = 1.0 + mod_ref[4:5, :]
        _acc(dshift_ref, jnp.sum(dh2, axis=0, keepdims=True), first_seq)
        _acc(dscale_ref, jnp.sum(dh2 * (n * g), axis=0, keepdims=True), first_seq)
        _acc(dg2_ref, jnp.sum((dh2 * sc1) * n, axis=0, keepdims=True), i == 0)
        dx1 = dy_ref[...] + _rms_bwd(n, r, (dh2 * sc1) * g)
        dx1_ref[...] = dx1
        _acc(dgate1_ref, jnp.sum(dx1 * mx_ref[...], axis=0, keepdims=True), first_seq)
        dmixed_ref[...] = (dx1 * mod_ref[2:3, :]).astype(BF16)

    nseq = t // (tm * tps)
    sv = _sds((nseq, 1, d), F32)
    return pl.pallas_call(
        body, name="bwd_ffn", grid=(t // tm,),
        out_shape=(_sds((t, dff), BF16), _sds((t, d), F32), _sds((t, d), BF16), sv, sv, sv, _sds((1, d), F32)),
        in_specs=[_row(tm, d), _row(tm, dff), _row(tm, d), _row(tm, d), _row(tm, d), _full((1, d)), _modspec(d, tps),
                  ANY, ANY],
        out_specs=(_row(tm, dff), _row(tm, d), _row(tm, d), _seqv(d, tps), _seqv(d, tps), _seqv(d, tps),
                   _full((1, d))),
        scratch_shapes=[pltpu.VMEM(w2.shape, BF16), pltpu.VMEM(w1.shape, BF16)],
        compiler_params=_params("arbitrary"),
    )(df, a, x1, dy, mixed, g2, mod3, w2, w1)


def _bwd_mix(dmixed, zgate, ya, yb, u1, lng, lnb, wout, wo_p, wpw, tm, swap=()):
    t, d = ya.shape
    _, _, npad = _layout(d)
    nw = len(swap)
    n_steps = t // tm

    def body(dm_ref, zg_ref, ya_ref, yb_ref, u1_ref, lng_ref, lnb_ref, wout_ref, wo_ref, wpw_ref, *rest):
        dya_ref, dyb_ref, dz_ref, do_ref, du1_ref, dlng_ref, dlnb_ref, dcb_ref = rest[nw:nw + 8]
        i = pl.program_id(0)
        if nw:
            start, finish = _swap_phases(rest[:nw], rest[nw + 8:2 * nw + 8], *rest[2 * nw + 8:])
            pl.when(i == 0)(start)
        dmpre = _dot_nt(dm_ref[...], wout_ref[...])
        zg = zg_ref[...]
        ga = jax.nn.sigmoid(zg[:, :d])
        gb = jax.nn.sigmoid(zg[:, d:])
        dya = (dmpre * ga).astype(BF16)
        dyb = (dmpre * gb).astype(BF16)
        dya_ref[...] = dya
        dyb_ref[...] = dyb
        dz_ref[:, :d] = ((dmpre * ya_ref[...]) * (ga * (1.0 - ga))).astype(BF16)
        dz_ref[:, d:] = ((dmpre * yb_ref[...]) * (gb * (1.0 - gb))).astype(BF16)
        do_ref[...] = _dot_nt(dya, wo_ref[...]).astype(BF16)
        du3 = _dot_nt(dyb, wpw_ref[...])
        u1 = u1_ref[...]
        mu = jnp.mean(u1, axis=-1, keepdims=True)
        xc = u1 - mu
        rstd = lax.rsqrt(jnp.mean(xc * xc, axis=-1, keepdims=True) + EPS)
        nh = xc * rstd
        l = nh * lng_ref[...] + lnb_ref[...]
        sg = jax.nn.sigmoid(l)
        dl = du3 * (sg * (1.0 + l * (1.0 - sg)))
        _acc(dlng_ref, jnp.sum(dl * nh, axis=0, keepdims=True), i == 0)
        _acc(dlnb_ref, jnp.sum(dl, axis=0, keepdims=True), i == 0)
        dnh = dl * lng_ref[...]
        du1 = rstd * (dnh - jnp.mean(dnh, axis=-1, keepdims=True) - nh * jnp.mean(dnh * nh, axis=-1, keepdims=True))
        du1_ref[...] = du1
        _acc(dcb_ref, jnp.sum(du1, axis=0, keepdims=True), i == 0)
        if nw:
            pl.when(i == n_steps - 1)(finish)

    cv = _sds((1, CONV_CH), F32)
    res = pl.pallas_call(
        body, name="bwd_mix", grid=(n_steps,),
        out_shape=(_sds((t, d), BF16), _sds((t, d), BF16), _sds((t, npad), BF16), _sds((t, HW), BF16),
                   _sds((t, CONV_CH), F32), cv, cv, cv) + _swap_shapes(swap),
        in_specs=[_row(tm, d), _row(tm, 2 * d), _row(tm, d), _row(tm, d), _row(tm, CONV_CH), _full((1, CONV_CH)),
                  _full((1, CONV_CH)), _full(wout.shape), _full(wo_p.shape), _full(wpw.shape)] + [ANY] * nw,
        out_specs=(_row(tm, d), _row(tm, d), _row(tm, 2 * d), _row(tm, HW), _row(tm, CONV_CH),
                   _full((1, CONV_CH)), _full((1, CONV_CH)), _full((1, CONV_CH))) + (ANY,) * nw,
        scratch_shapes=_swap_sems(swap) if nw else [],
        compiler_params=_params("arbitrary"),
    )(dmixed, zgate, ya, yb, u1, lng, lnb, wout, wo_p, wpw, *swap)
    return res[:8] + (res[8:],)


def _bwd_conv(dz, du1, u0, zglu, cw, tm, tps):
    t = du1.shape[0]
    d = (dz.shape[1] - MLA_IN - 2 * CONV_CH) // 2
    p_glu, _, _ = _layout(d)
    hpt = tm // HALO
    last_blk = t // HALO - 1
    cwc = _by_lane_chunk(cw)

    def body(dz_hbm, du_ref, dun_ref, u_ref, uh_ref, zl_ref, cw_ref, dzl_ref, dcw_ref, ext_ref, dext_ref, dcw8_ref,
             du0_ref):
        i = pl.program_id(0)
        first = (i % tps) == 0
        last = (i % tps) == (tps - 1)
        _fill_shifted(ext_ref, jnp.where(first, 0.0, uh_ref[...]), u_ref[...])
        _fill_shifted(dext_ref, du_ref[...], jnp.where(last, 0.0, dun_ref[...]))

        @pl.when(i == 0)
        def _():
            dcw8_ref[...] = jnp.zeros_like(dcw8_ref)

        groups = CONV_ROWS // SUBLANES

        def conv_chunk(c, carry):
            lc, r0 = _conv_chunk(c)
            du = _shifted(dext_ref, 0, lc, r0)
            du0 = jnp.zeros((CONV_ROWS, LANES), F32)
            for kk in range(CONV_W):
                prod = du * _shifted(ext_ref, HALO - (CONV_W - 1) + kk, lc, r0)
                part = prod[:SUBLANES]
                for g in range(1, groups):
                    part = part + prod[g * SUBLANES:(g + 1) * SUBLANES]
                dcw8_ref[lc, kk] += part
                du0 = du0 + cw_ref[lc, kk:kk + 1, :] * _shifted(dext_ref, CONV_W - 1 - kk, lc, r0)
            du0_ref[lc, pl.ds(r0, CONV_ROWS), :] = du0
            return carry

        lax.fori_loop(0, CONV_LC * (tm // CONV_ROWS), conv_chunk, 0)

        @pl.when(i == pl.num_programs(0) - 1)
        def _():
            for lc, ls in _lane_chunks():
                dcw_ref[:, ls] = jnp.sum(dcw8_ref[lc], axis=1)

        for lc, ls in _lane_chunks():
            du0 = du0_ref[lc]
            ga = zl_ref[:, ls]
            sb = jax.nn.sigmoid(zl_ref[:, CONV_CH + lc * LANES:CONV_CH + (lc + 1) * LANES])
            dzl_ref[:, ls] = (du0 * sb).astype(BF16)
            dzl_ref[:, CONV_CH + lc * LANES:CONV_CH + (lc + 1) * LANES] = ((du0 * ga) * (sb * (1.0 - sb))).astype(BF16)

    prev = pl.BlockSpec((HALO, CONV_CH), lambda i: (jnp.maximum(i * hpt - 1, 0), 0))
    nxt = pl.BlockSpec((HALO, CONV_CH), lambda i: (jnp.minimum((i + 1) * hpt, last_blk), 0))
    glu_blk = p_glu // (2 * CONV_CH)
    return pl.pallas_call(
        body, name="bwd_conv", grid=(t // tm,),
        out_shape=(_sds(dz.shape, BF16), _sds(cw.shape, F32)),
        in_specs=[ANY, _row(tm, CONV_CH), nxt, _row(tm, CONV_CH), prev, _row(tm, 2 * CONV_CH), _full(cwc.shape)],
        out_specs=(pl.BlockSpec((tm, 2 * CONV_CH), lambda i: (i, glu_blk)), _full(cw.shape)),
        scratch_shapes=[pltpu.VMEM(_shifted_shape(tm), F32)] * 2
        + [pltpu.VMEM((CONV_LC, HALO, SUBLANES, LANES), F32), pltpu.VMEM((CONV_LC, tm, LANES), F32)],
        input_output_aliases={0: 0},
        compiler_params=_params("arbitrary"),
    )(dz, du1, du1, u0, u0, zglu, cwc)


def _attn_bwd(q, k, v, do, nseq, seq, scatter=()):
    t = q.shape[0]
    ns = len(scatter)
    blk = pl.BlockSpec((seq, LANES), lambda b, h: (b, h))
    n_steps = nseq * N_HEADS

    def body(q_ref, k_ref, v_ref, do_ref, *rest):
        dq_ref, dk_ref, dv_ref = rest[ns:ns + 3]
        dka_ref, dva_ref = rest[2 * ns + 3:2 * ns + 5]
        if ns:
            start, finish = _scatter_phases(rest[:ns], rest[ns + 3:2 * ns + 3], *rest[2 * ns + 5:])
            step = pl.program_id(0) * N_HEADS + pl.program_id(1)
            pl.when(step == 0)(start)
        dka_ref[...] = jnp.zeros_like(dka_ref)
        dva_ref[...] = jnp.zeros_like(dva_ref)
        mask = _diag_mask()
        nb = seq // BQ
        block = lambda j: (_scores(q_ref[j * BQ:(j + 1) * BQ, :], k_ref, j * BQ, (j + 1) * BQ),
                           _scores(do_ref[j * BQ:(j + 1) * BQ, :], v_ref, j * BQ, (j + 1) * BQ))
        ahead = [block(j) for j in range(min(AHEAD, nb))]
        for i in range(nb):
            lo, e = i * BQ, (i + 1) * BQ
            q_i = q_ref[lo:e, :]
            do_i = do_ref[lo:e, :]
            scores, (dpp, dpd) = ahead.pop(0)
            if i + AHEAD < nb:
                ahead.append(block(i + AHEAD))
            pp, pd, l = _softmax_parts(scores, mask)
            inv = 1.0 / l
            pd = pd * inv
            delta = jnp.sum(pd * dpd, axis=-1, keepdims=True)
            if lo:
                pp = pp * inv
                delta = delta + jnp.sum(pp * dpp, axis=-1, keepdims=True)
            dsd = (pd * (dpd - delta)).astype(BF16)
            dq = _dot(dsd, k_ref[lo:e, :])
            dka_ref[lo:e, :] += _dot_tn(dsd, q_i)
            dva_ref[lo:e, :] += _dot_tn(pd.astype(BF16), do_i)
            if lo:
                dsp = (pp * (dpp - delta)).astype(BF16)
                dq = dq + _dot(dsp, k_ref[:lo, :])
                dka_ref[:lo, :] += _dot_tn(dsp, q_i)
                dva_ref[:lo, :] += _dot_tn(pp.astype(BF16), do_i)
            dq_ref[lo:e, :] = dq * SM_SCALE
        dk_ref[...] = dka_ref[...] * SM_SCALE
        dv_ref[...] = dva_ref[...].astype(BF16)
        if ns:
            pl.when(step == n_steps - 1)(finish)

    res = pl.pallas_call(
        body, name="attn_bwd", grid=(nseq, N_HEADS),
        out_shape=(_sds((t, HW), F32), _sds((t, HW), F32), _sds((t, HW), BF16)) + _scatter_shapes(scatter),
        in_specs=[blk] * 4 + [ANY] * ns, out_specs=(blk,) * 3 + (ANY,) * ns,
        scratch_shapes=[pltpu.VMEM((seq, LANES), F32), pltpu.VMEM((seq, LANES), F32)]
        + (_scatter_sems(ns) if ns else []),
        compiler_params=_params("arbitrary", "arbitrary"),
    )(q, k, v, do, *scatter)
    return res[0], res[1], res[2], res[3:]


def _mla_bwd(dz, dq, dk, dv, zm, gql, gkvl, gq, gk, tabs, wuq_p, wk_p, wv_p, tm, tps):
    t = zm.shape[0]
    d = (dz.shape[1] - MLA_IN - 2 * CONV_CH) // 2
    _, p_q, _ = _layout(d)
    c_t, s1_t, s2_t = tabs
    tab = pl.BlockSpec((tm, LANES), lambda i: (i % tps, 0))

    def body(dz_hbm, dq_ref, dk_ref, dv_ref, zm_ref, gql_ref, gkvl_ref, gq_ref, gk_ref, c_ref, s1_ref, s2_ref,
             wuq_ref, wk_ref, wv_ref,
             dzm_ref, dqpre_ref, dkh_ref, dgq_ref, dgk_ref, dgql_ref, dgkvl_ref):
        i = pl.program_id(0)
        c, s1, s2 = c_ref[...], s1_ref[...], s2_ref[...]
        nq, rq = _rms(zm_ref[:, :Q_RANK])
        qpre = _dot((nq * gql_ref[...]).astype(BF16), wuq_ref[...])
        nkv, rkv = _rms(zm_ref[:, Q_RANK:OFF_KV])
        knope = _dot((nkv * gkvl_ref[...]).astype(BF16), wk_ref[...])
        zkr_v = zm_ref[:, OFF_KV:]
        gk = gk_ref[...]
        kr_roped = _rope(zkr_v * gk, c, s1, s2)
        dgq = jnp.zeros((1, LANES), F32)
        dgk = jnp.zeros((1, LANES), F32)
        dzkr = jnp.zeros((tm, LANES), F32)
        dt_sum = jnp.zeros((tm, LANES), F32)
        slabs = [slice(hd * LANES, (hd + 1) * LANES) for hd in range(N_HEADS)]
        gq = gq_ref[...]
        rqh = [_head_rms(qpre[:, sl])[1] for sl in slabs]
        rkh = [_head_rms(knope[:, sl] + zkr_v)[1] for sl in slabs]
        dyr = [_rope_t(dq_ref[:, sl], c, s1, s2) for sl in slabs]
        nqh = [qpre[:, sl] * rqh[hd] for hd, sl in enumerate(slabs)]
        sq = [jnp.sum((dyr[hd] * gq) * nqh[hd], axis=-1, keepdims=True) for hd in range(N_HEADS)]
        dr = [jnp.sum(dk_ref[:, sl] * (knope[:, sl] * gk + kr_roped), axis=-1, keepdims=True) for sl in slabs]
        for hd, sl in enumerate(slabs):
            dgq = dgq + jnp.sum(dyr[hd] * nqh[hd], axis=0, keepdims=True)
            dqpre_ref[:, sl] = (rqh[hd] * (dyr[hd] * gq - nqh[hd] * (sq[hd] * (1.0 / QK_HEAD)))).astype(BF16)
            kn = knope[:, sl]
            r = rkh[hd]
            dt = dk_ref[:, sl] * r
            via_r = (dr[hd] * (r * r * r) * (-1.0 / QK_HEAD)) * (kn + zkr_v)
            dgk = dgk + jnp.sum(dt * kn, axis=0, keepdims=True)
            dt_sum = dt_sum + dt
            dzkr = dzkr + via_r
            dkh_ref[:, sl] = (dt * gk + via_r).astype(BF16)
        de = _rope_t(dt_sum, c, s1, s2)
        dzkr = dzkr + de * gk
        dgk = dgk + jnp.sum(de * zkr_v, axis=0, keepdims=True)
        _acc(dgq_ref, dgq[:, :QK_HEAD], i == 0)
        _acc(dgk_ref, dgk[:, :QK_HEAD], i == 0)
        dzm_ref[:, OFF_KV:] = dzkr.astype(BF16)
        dqln = _dot_nt(dqpre_ref[...], wuq_ref[...])
        _acc(dgql_ref, jnp.sum(dqln * nq, axis=0, keepdims=True), i == 0)
        dzm_ref[:, :Q_RANK] = _rms_bwd(nq, rq, dqln * gql_ref[...]).astype(BF16)
        dkvn = _dot_nt(dkh_ref[...], wk_ref[...]) + _dot_nt(dv_ref[...], wv_ref[...])
        _acc(dgkvl_ref, jnp.sum(dkvn * nkv, axis=0, keepdims=True), i == 0)
        dzm_ref[:, Q_RANK:OFF_KV] = _rms_bwd(nkv, rkv, dkvn * gkvl_ref[...]).astype(BF16)

    return pl.pallas_call(
        body, name="mla_bwd", grid=(t // tm,),
        out_shape=(_sds(dz.shape, BF16), _sds((t, HW), BF16), _sds((t, HW), BF16), _sds((1, QK_HEAD), F32),
                   _sds((1, QK_HEAD), F32), _sds((1, Q_RANK), F32), _sds((1, KV_RANK), F32)),
        in_specs=[ANY, _row(tm, HW), _row(tm, HW), _row(tm, HW), _row(tm, MLA_IN),
                  _full((1, Q_RANK)), _full((1, KV_RANK)), _full((1, LANES)), _full((1, LANES)), tab, tab, tab,
                  _full(wuq_p.shape), _full(wk_p.shape), _full(wv_p.shape)],
        out_specs=(pl.BlockSpec((tm, MLA_IN), lambda i: (i, p_q // MLA_IN)), _row(tm, HW), _row(tm, HW),
                   _full((1, QK_HEAD)), _full((1, QK_HEAD)), _full((1, Q_RANK)), _full((1, KV_RANK))),
        input_output_aliases={0: 0},
        compiler_params=_params("arbitrary"),
    )(dz, dq, dk, dv, zm, gql, gkvl, gq, gk, c_t, s1_t, s2_t, wuq_p, wk_p, wv_p)


def _bwd_in(dz, x, dx1, g1, mod3, win_p, tm, tps, scatter=()):
    t, d = x.shape
    npad = dz.shape[1]

    ns = len(scatter)
    n_steps = t // tm

    def body(dz_ref, x_ref, dx1_ref, g_ref, mod_ref, wt_hbm, *rest):
        gx_ref, dshift_ref, dscale_ref, dg1_ref = rest[ns:ns + 4]
        wt_ref = rest[2 * ns + 4]
        i = pl.program_id(0)
        if ns:
            start, finish = _scatter_phases(rest[:ns], rest[ns + 4:2 * ns + 4], *rest[2 * ns + 5:])
            pl.when(i == 0)(start)
        _load_resident(i, [(wt_hbm, wt_ref)])
        first_seq = (i % tps) == 0
        g = g_ref[...]
        sc1 = 1.0 + mod_ref[1:2, :]
        nb = max(tm // ROW_BAND, 1)
        bands = [slice(b * (tm // nb), (b + 1) * (tm // nb)) for b in range(nb)]
        dhs = [_dot_nt(dz_ref[rows, :], wt_ref[...]) for rows in bands]
        sums = [jnp.zeros((1, d), F32)] * 3
        col = lambda v: jnp.sum(v, axis=0, keepdims=True)
        for rows, dh in zip(bands, dhs):
            n, r = _rms(x_ref[rows, :])
            sums = [sums[0] + col(dh), sums[1] + col(dh * (n * g)), sums[2] + col((dh * sc1) * n)]
            gx_ref[rows, :] = dx1_ref[rows, :] + _rms_bwd(n, r, (dh * sc1) * g)
        _acc(dshift_ref, sums[0], first_seq)
        _acc(dscale_ref, sums[1], first_seq)
        _acc(dg1_ref, sums[2], i == 0)
        if ns:
            pl.when(i == n_steps - 1)(finish)

    nseq = t // (tm * tps)
    sv = _sds((nseq, 1, d), F32)
    res = pl.pallas_call(
        body, name="bwd_in", grid=(n_steps,),
        out_shape=(_sds((t, d), F32), sv, sv, _sds((1, d), F32)) + _scatter_shapes(scatter),
        in_specs=[_row(tm, npad), _row(tm, d), _row(tm, d), _full((1, d)), _modspec(d, tps), ANY] + [ANY] * ns,
        out_specs=(_row(tm, d), _seqv(d, tps), _seqv(d, tps), _full((1, d))) + (ANY,) * ns,
        scratch_shapes=[pltpu.VMEM(win_p.shape, BF16)] + (_scatter_sems(ns) if ns else []),
        compiler_params=_params("arbitrary"),
    )(dz, x, dx1, g1, mod3, win_p, *scatter)
    return res[0], res[1], res[2], res[3], res[4:]


def _tile_of(n, choices):
    for c in choices:
        if n % c == 0:
            return c
    return n


def _tn_matmul(a, b, name, col_shards=0):
    t, k = a.shape
    n = b.shape[1]
    tk = _tile_of(k, (1024, 512, 256, 128))
    tn = n // col_shards if col_shards else _tile_of(n, (1024, 896, 768, 512, 384, 256, 128))
    tt = _tile_of(t, (4096, 2048, 1024, 512, 256))

    def body(a_ref, b_ref, o_ref):
        _acc(o_ref, _dot_tn(a_ref[...], b_ref[...]), pl.program_id(2) == 0)

    if col_shards:
        out_shape, out_spec = _sds((col_shards, k, tn), F32), pl.BlockSpec((None, tk, tn), lambda i, j, s: (j, i, 0))
    else:
        out_shape, out_spec = _sds((k, n), F32), pl.BlockSpec((tk, tn), lambda i, j, s: (i, j))
    return pl.pallas_call(
        body, name=name, grid=(k // tk, n // tn, t // tt), out_shape=out_shape,
        in_specs=[pl.BlockSpec((tt, tk), lambda i, j, s: (s, i)), pl.BlockSpec((tt, tn), lambda i, j, s: (s, j))],
        out_specs=out_spec, compiler_params=_params("arbitrary", "arbitrary", "arbitrary"),
    )(a, b)


N_SHARD = 4
COL_SHARDED = ("w_in", "w_uq", "w_ukv", "w_o_mla", "w_pw_out", "w_ff1")
ROW_SHARDED = ("w_out", "w_ff2")
BIG = ("w_in", "w_uq", "w_ukv", "w_o_mla", "w_pw_out", "w_out", "w_ff1", "w_ff2")
SMALL = ("norm1_g", "q_latent_g", "kv_latent_g", "qk_norm_q_g", "qk_norm_k_g", "conv_b", "conv_ln_g", "conv_ln_b",
         "norm2_g")
WEIGHTS = ("w_ada", "b_ada", "norm1_g", "w_in", "q_latent_g", "w_uq", "kv_latent_g", "w_ukv", "qk_norm_q_g",
           "qk_norm_k_g", "w_o_mla", "conv_w", "conv_b", "conv_ln_g", "conv_ln_b", "w_pw_out", "w_out", "norm2_g",
           "w_ff1", "w_ff2")


def _pad_heads(w, width):
    k = w.shape[0]
    w3 = w.reshape(k, N_HEADS, width)
    return jnp.pad(w3, ((0, 0), (0, 0), (0, LANES - width))).reshape(k, HW)


def _unpad_heads(g, width):
    k = g.shape[0]
    return g.reshape(k, N_HEADS, LANES)[:, :, :width].reshape(k, N_HEADS * width)


def _pad_win(w):
    d = w.shape[0]
    z = lambda n: jnp.zeros((d, n), w.dtype)
    return jnp.concatenate([w[:, OFF_GLU:], w[:, OFF_KR:OFF_GLU], w[:, :OFF_KV], z(KR_LANE), w[:, OFF_KV:OFF_KR],
                            z(LANES - KR_LANE - QK_ROPE)], axis=1)


def _unpad_win(g):
    d = g.shape[0]
    p_glu, p_q, _ = _layout(d)
    kr = p_q + OFF_KV + KR_LANE
    return jnp.concatenate([g[:, p_q:p_q + OFF_KV], g[:, kr:kr + QK_ROPE], g[:, p_glu:p_q], g[:, :p_glu]], axis=1)


def _col_shards(g):
    k, n = g.shape
    return g.reshape(k, N_SHARD, n // N_SHARD).transpose(1, 0, 2)


def _from_shards(g, name):
    ns, ks, nn = g.shape
    if name in ROW_SHARDED:
        return g.reshape(ns * ks, nn)
    return g.transpose(1, 0, 2).reshape(ks, ns * nn)


EARLY = ("w_in", "w_uq", "w_ukv")
LATE = ("w_o_mla", "w_pw_out", "w_out", "w_ff1", "w_ff2")


def _assemble(names, gathered):
    return {n: _from_shards(g.reshape((N_SHARD, 2 * g.shape[1]) + g.shape[2:]), n) for n, g in zip(names, gathered)}


GROUP_A = ("w_out", "w_ff1", "w_ff2")
GROUP_B = ("w_in", "w_uq", "w_ukv", "w_o_mla", "w_pw_out")


def _pair_halves(g):
    return g.reshape(N_SHARD, 2, g.shape[1] // 2, g.shape[2])


def _pair_sums(names, halves, from_sibling):
    if not halves:
        return []
    cidx = lax.axis_index("c").reshape(1).astype(jnp.int32)
    return [_add_pair(g, l, cidx, "pair_sum_" + n) for n, g, l in zip(names, halves, from_sibling)]


def _local_step(x, target, mod, sp, w, late=None, tm=256):
    comm = late is not None
    w = dict(w)
    nseq, seq, d = x.shape
    t = nseq * seq
    tps = seq // tm
    xf = x.reshape(t, d)
    tg = target.reshape(t, d)
    mod3 = mod.reshape(nseq, N_MOD, d)

    win_p = _pad_win(w["w_in"])
    wuq_p = _pad_heads(w["w_uq"], QK_HEAD)
    wkv3 = w["w_ukv"].reshape(KV_RANK, N_HEADS, QK_NOPE + V_HEAD)
    wk_p = _pad_heads(wkv3[:, :, :QK_NOPE].reshape(KV_RANK, -1), QK_NOPE)
    wv_p = _pad_heads(wkv3[:, :, QK_NOPE:].reshape(KV_RANK, -1), V_HEAD)
    cw = jnp.pad(w["conv_w"], ((0, HALO - CONV_W), (0, 0)))
    pad_g = lambda g: jnp.pad(g, ((0, 0), (0, LANES - QK_HEAD)))
    gq, gk = pad_g(sp["qk_norm_q_g"]), pad_g(sp["qk_norm_k_g"])
    tabs = _rope_tables(seq)

    tm_in, tps_in = (2 * tm, tps // 2) if tps % 2 == 0 else (tm, tps)
    h, zm, zglu, zgate, u0 = _fwd_in(xf, sp["norm1_g"], mod3, win_p, tm_in, tps_in)
    q, k, v, qln, kvn = _mla_prep(zm, sp["q_latent_g"], sp["kv_latent_g"], gq, gk, tabs, wuq_p, wk_p, wv_p, tm, tps)
    attn, gathered = _attn_fwd(q, k, v, nseq, seq, tuple(late) if comm else ())
    if comm:
        w.update(_assemble(LATE, gathered))
    wo_p = jnp.pad(w["w_o_mla"].reshape(N_HEADS, V_HEAD, d), ((0, 0), (0, LANES - V_HEAD), (0, 0))).reshape(HW, d)
    x1, mixed, mpre, ya, yb, u1, u3 = _fwd_mix(attn, u0, zgate, xf, mod3, wo_p, cw, sp["conv_b"], sp["conv_ln_g"],
                                               sp["conv_ln_b"], w["w_pw_out"], w["w_out"], tm, tps)
    h2, a, r, dy, df, dgate2, loss_acc = _fwd_ffn(x1, tg, sp["norm2_g"], mod3, w["w_ff1"], w["w_ff2"], tm, tps)
    da, dx1, dmixed, dshift2, dscale2, dgate1, dg2 = _bwd_ffn(df, a, x1, dy, mixed, sp["norm2_g"], mod3,
                                                              w["w_ff2"], w["w_ff1"], tm, tps)
    gw = {
        "w_out": _tn_matmul(mpre, dmixed, "dw_out").reshape(N_SHARD, d // N_SHARD, d),
        "w_ff1": _tn_matmul(h2, da, "dw_ff1", N_SHARD),
        "w_ff2": _tn_matmul(r, df, "dw_ff2").reshape(N_SHARD, -1, d),
    }
    halves_a = [_pair_halves(gw[n]) for n in GROUP_A] if comm else []
    dya, dyb, dz, do, du1, dlng, dlnb, dcb, from_sibling = _bwd_mix(
        dmixed, zgate, ya, yb, u1, sp["conv_ln_g"], sp["conv_ln_b"], w["w_out"], wo_p, w["w_pw_out"], tm, tuple(halves_a))
    pair_a = _pair_sums(GROUP_A, halves_a, from_sibling)
    dz, dcw = _bwd_conv(dz, du1, u0, zglu, cw, tm, tps)
    gw["conv_w"] = dcw
    dq, dk, dv, land_a = _attn_bwd(q, k, v, do, nseq, seq, tuple(p[1] for p in pair_a))
    dz, dqpre, dkh, dgq, dgk, dgql, dgkvl = _mla_bwd(dz, dq, dk, dv, zm, sp["q_latent_g"], sp["kv_latent_g"], gq, gk,
                                                      tabs, wuq_p, wk_p, wv_p, tm, tps)
    dwk_p = _tn_matmul(kvn, dkh, "dw_uk")
    dwv_p = _tn_matmul(kvn, dv, "dw_uv")
    dwkv = jnp.concatenate([dwk_p.reshape(KV_RANK, N_HEADS, LANES)[:, :, :QK_NOPE],
                            dwv_p.reshape(KV_RANK, N_HEADS, LANES)[:, :, :V_HEAD]], axis=2).reshape(KV_RANK, -1)
    dwo = _tn_matmul(attn, dya, "dw_o").reshape(N_HEADS, LANES, d)[:, :V_HEAD].reshape(MLA_WIDTH, d)
    gw["w_in"] = _col_shards(_unpad_win(_tn_matmul(h, dz, "dw_in")))
    gw["w_uq"] = _col_shards(_unpad_heads(_tn_matmul(qln, dqpre, "dw_uq"), QK_HEAD))
    gw["w_ukv"] = _col_shards(dwkv)
    gw["w_o_mla"] = _col_shards(dwo)
    gw["w_pw_out"] = _tn_matmul(u3, dyb, "dw_pw", N_SHARD)
    pair_b = []
    if comm:
        halves_b = [_pair_halves(gw[n]) for n in GROUP_B]
        pair_b = _pair_sums(GROUP_B, halves_b, _pair_swap(halves_b, "grad_pair_swap"))
    gx, dshift1, dscale1, dg1, land_b = _bwd_in(dz, xf, dx1, sp["norm1_g"], mod3, win_p, tm_in, tps_in,
                                                tuple(p[1] for p in pair_b))
    if comm:
        for n, p, l in zip(GROUP_A + GROUP_B, pair_a + pair_b, land_a + land_b):
            gw[n] = (p[0], l)
    gs = {
        "norm1_g": dg1, "q_latent_g": dgql, "kv_latent_g": dgkvl, "qk_norm_q_g": dgq, "qk_norm_k_g": dgk,
        "conv_b": dcb, "conv_ln_g": dlng, "conv_ln_b": dlnb, "norm2_g": dg2,
    }
    dmod = jnp.concatenate([dshift1, dscale1, dgate1, dshift2, dscale2, dgate2], axis=2).reshape(nseq, N_MOD * d)
    return loss_acc, gx.reshape(nseq, seq, d), dmod, gw, gs


def kernel(x, c, w_ada, b_ada, norm1_g, w_in, q_latent_g, w_uq, kv_latent_g, w_ukv, qk_norm_q_g, qk_norm_k_g, w_o_mla, conv_w, conv_b, conv_ln_g, conv_ln_b, w_pw_out, w_out, norm2_g, w_ff1, w_ff2, loss_target, m_w_ada, m_b_ada, m_norm1_g, m_w_in, m_q_latent_g, m_w_uq, m_kv_latent_g, m_w_ukv, m_qk_norm_q_g, m_qk_norm_k_g, m_w_o_mla, m_conv_w, m_conv_b, m_conv_ln_g, m_conv_ln_b, m_w_pw_out, m_w_out, m_norm2_g, m_w_ff1, m_w_ff2, v_w_ada, v_b_ada, v_norm1_g, v_w_in, v_q_latent_g, v_w_uq, v_kv_latent_g, v_w_ukv, v_qk_norm_q_g, v_qk_norm_k_g, v_w_o_mla, v_conv_w, v_conv_b, v_conv_ln_g, v_conv_ln_b, v_w_pw_out, v_w_out, v_norm2_g, v_w_ff1, v_w_ff2):
    given = dict(locals())
    wts = {n: given[n][0] for n in WEIGHTS}
    mom = {n: given["m_" + n][0] for n in WEIGHTS}
    var = {n: given["v_" + n][0] for n in WEIGHTS}
    vec = lambda a: a.reshape(1, -1)
    nseq, seq, d = x.shape
    ix, iy, ic = _place()
    shard = 2 * ix + iy

    half = lambda n: lax.dynamic_slice_in_dim(wts[n].astype(BF16), ic * (wts[n].shape[0] // 2), wts[n].shape[0] // 2,
                                              axis=0)
    gathered = _all_gather8([half(n) for n in EARLY] + [wts["conv_w"], c], "gather_weights")
    full = _assemble(EARLY, gathered)
    full["conv_w"] = _from_shards(gathered[-2][0::2], "conv_w")
    c_all = gathered[-1].reshape(8 * nseq, d)

    n_ada = wts["w_ada"].shape[1]
    b_sh = lax.dynamic_slice_in_dim(vec(wts["b_ada"]), shard * n_ada, n_ada, axis=1)
    mod_sh = _ada_mod(c_all, wts["w_ada"], b_sh)
    hb = 4 * nseq
    mod_blk = lax.dynamic_slice_in_dim(mod_sh, ic * hb, hb, axis=0)
    (mod_all,) = _all_gather8([mod_blk], "gather_mod")
    mod_mine = lax.dynamic_slice_in_dim(mod_all, (2 * iy + ic) * nseq, nseq, axis=1)
    mod = jnp.concatenate([lax.dynamic_index_in_dim(mod_mine, 2 * s + ix, axis=0, keepdims=False)
                           for s in range(N_SHARD)], axis=1)

    sp = {n: vec(wts[n]) for n in SMALL}
    loss_part, grad_x, dmod, gw, gs = _local_step(x, loss_target, mod, sp, full, [half(n) for n in LATE])

    parts = _all_gather8([dmod, gw["conv_w"], loss_part] + [gs[n] for n in SMALL], "gather_small")
    dmod_all = parts[0].reshape(8 * nseq, N_MOD * d)
    dmod_sh = lax.dynamic_slice_in_dim(dmod_all, shard * n_ada, n_ada, axis=1)
    res = _ada_bwd(c_all, dmod_all, dmod_sh, parts[1:])
    grads = {"w_ada": res[0], "b_ada": res[1]}
    n_cw = wts["conv_w"].shape[1]
    grads["conv_w"] = lax.dynamic_slice_in_dim(res[2], shard * n_cw, n_cw, axis=1)[:CONV_W]
    loss = res[3][0, 0]
    for n, g in zip(SMALL, res[4:]):
        grads[n] = g

    own_c = jnp.stack([shard, ic]).astype(jnp.int32)
    mine_sum = [_add_chips(gw[n][0], gw[n][1], own_c, "chip_sum_" + n) for n in BIG]
    for n, g in zip(BIG, _pair_gather(mine_sum, "grad_pair_gather")):
        grads[n] = g.reshape(wts[n].shape)

    delta, new_m, new_v = {}, {}, {}
    for n in BIG + ("w_ada",):
        delta[n], new_m[n], new_v[n] = _adamw(wts[n], grads[n], mom[n], var[n], "adamw_" + n)
    rest = ("b_ada", "conv_w") + SMALL
    as2d = lambda a: a if a.ndim == 2 else vec(a)
    res = _adamw_small(*[[as2d(t[n]) for n in rest] for t in (wts, grads, mom, var)])
    for dst, arrs in zip((delta, new_m, new_v), res):
        for n, a in zip(rest, arrs):
            dst[n] = a

    outs = [loss, grad_x]
    for group in (grads, delta, new_m, new_v):
        outs += [group[n].reshape(given[n].shape) for n in WEIGHTS]
    return tuple(outs)
```

```python
import jax
import jax.numpy as jnp
from jax import lax
from jax.experimental import pallas as pl
from jax.experimental.pallas import tpu as pltpu

F32 = jnp.float32
BF16 = jnp.bfloat16
MESH = pl.DeviceIdType.MESH
ANY = pl.BlockSpec(memory_space=pl.ANY)

CHUNK = 64
CHUNK_SHIFT = 6
N_HEADS = 8
QK_NOPE = 64
QK_ROPE = 32
QK_HEAD = QK_NOPE + QK_ROPE
V_HEAD = 64
Q_RANK = 256
KV_RANK = 128
MLA_WIDTH = N_HEADS * V_HEAD
CONV_CH = 512
CONV_W = 31
ROPE_THETA = 10000.0
EPS = 1e-6
LANES = 128
SUBLANES = 8
HW = N_HEADS * LANES
OFF_KV = Q_RANK + KV_RANK
OFF_KR = OFF_KV + QK_ROPE
OFF_GLU = OFF_KR + 2 * CONV_CH
KR_LANE = QK_NOPE
MLA_IN = Q_RANK + KV_RANK + LANES
HALO = 32
N_MOD = 6

ADAM_LR = 0.001
ADAM_B1 = 0.9
ADAM_B2 = 0.999
ADAM_EPS = 1e-08
ADAM_WD = 0.01
ADAM_STEP = 10

VMEM_LIMIT = 56 * 1024 * 1024
BQ = 256


def _layout(d):
    p_glu = 2 * d
    p_q = p_glu + 2 * CONV_CH
    return p_glu, p_q, p_q + MLA_IN


def _params(*sem):
    return pltpu.CompilerParams(dimension_semantics=sem, vmem_limit_bytes=VMEM_LIMIT)


def _dot(a, b):
    return jnp.dot(a, b, preferred_element_type=F32)


def _dot_tn(a, b):
    return lax.dot_general(a, b, (((0,), (0,)), ((), ())), preferred_element_type=F32)


def _dot_nt(a, b):
    return lax.dot_general(a, b, (((1,), (1,)), ((), ())), preferred_element_type=F32)


def _acc(ref, val, first):
    @pl.when(first)
    def _():
        ref[...] = val

    @pl.when(jnp.logical_not(first))
    def _():
        ref[...] += val


def _rms(x):
    r = lax.rsqrt(jnp.mean(x * x, axis=-1, keepdims=True) + EPS)
    return x * r, r


def _rms_bwd(n, r, dn):
    return r * (dn - n * jnp.mean(dn * n, axis=-1, keepdims=True))


def _head_rms(sl):
    r = lax.rsqrt(jnp.sum(sl * sl, axis=-1, keepdims=True) * (1.0 / QK_HEAD) + EPS)
    return sl * r, r


def _head_rms_bwd(n, r, dn):
    return r * (dn - n * (jnp.sum(dn * n, axis=-1, keepdims=True) * (1.0 / QK_HEAD)))


def _rope(x, c, s1, s2):
    return x * c + pltpu.roll(x, QK_ROPE // 2, 1) * s1 + pltpu.roll(x, LANES - QK_ROPE // 2, 1) * s2


def _rope_t(dy, c, s1, s2):
    return dy * c + pltpu.roll(dy * s1, LANES - QK_ROPE // 2, 1) + pltpu.roll(dy * s2, QK_ROPE // 2, 1)


def _rope_tables(seq):
    half = QK_ROPE // 2
    inv_freq = ROPE_THETA ** (-jnp.arange(0, QK_ROPE, 2, dtype=F32) / QK_ROPE)
    ang = jnp.arange(seq, dtype=F32)[:, None] * inv_freq[None, :]
    cos, sin = jnp.cos(ang), jnp.sin(ang)
    z = lambda n: jnp.zeros((seq, n), F32)
    tail = LANES - QK_HEAD
    c = jnp.concatenate([jnp.ones((seq, QK_NOPE), F32), cos, cos, jnp.ones((seq, tail), F32)], axis=1)
    s1 = jnp.concatenate([z(QK_NOPE + half), sin, z(tail)], axis=1)
    s2 = jnp.concatenate([z(QK_NOPE), -sin, z(half + tail)], axis=1)
    return c, s1, s2


def _row(tm, w):
    return pl.BlockSpec((tm, w), lambda i: (i, 0))


def _modspec(d, tps):
    return pl.BlockSpec((None, N_MOD, d), lambda i: (i // tps, 0, 0))


def _seqv(w, tps):
    return pl.BlockSpec((None, 1, w), lambda i: (i // tps, 0, 0))


def _full(shape):
    return pl.BlockSpec(shape, lambda i: tuple(0 for _ in shape))


def _sds(shape, dtype):
    return jax.ShapeDtypeStruct(shape, dtype)


CONV_ROWS = 64
CONV_LC = CONV_CH // LANES


def _lane_chunks():
    return [(lc, slice(lc * LANES, (lc + 1) * LANES)) for lc in range(CONV_LC)]


def _fill_shifted(ext_ref, head, body):
    nh = head.shape[0]
    for lc, ls in _lane_chunks():
        ext_ref[0, lc, :nh, :] = head[:, ls]
        ext_ref[0, lc, nh:, :] = body[:, ls]
        rows = ext_ref[0, lc]
        for b in range(1, SUBLANES):
            ext_ref[b, lc] = pltpu.roll(rows, rows.shape[0] - b, 0)


def _shifted_shape(tm):
    return (SUBLANES, CONV_LC, tm + HALO, LANES)


def _conv_chunk(c):
    return c % CONV_LC, pl.multiple_of((c // CONV_LC) * CONV_ROWS, CONV_ROWS)


def _shifted(ext_ref, o, lc, r0):
    a = pl.multiple_of((o // SUBLANES) * SUBLANES + r0, SUBLANES)
    return ext_ref[o % SUBLANES, lc, pl.ds(a, CONV_ROWS), :]


def _by_lane_chunk(a):
    return a.reshape(a.shape[0], CONV_LC, LANES).transpose(1, 0, 2)


def _load_resident(i, pairs):
    @pl.when(i == 0)
    def _():
        for src, dst in pairs:
            pltpu.sync_copy(src, dst)


def _place():
    return lax.axis_index("x"), lax.axis_index("y"), lax.axis_index("c")


def _all_gather8(blocks, name):
    na = len(blocks)

    def body(*refs):
        start, forward, finish = _gather8_phases(refs[:na], refs[na:2 * na], *refs[2 * na:])
        start()
        forward()
        finish()

    outs = pl.pallas_call(
        body, name=name, out_shape=_gather8_shapes(blocks), in_specs=[ANY] * na, out_specs=(ANY,) * na,
        scratch_shapes=_gather8_sems(na),
    )(*blocks)
    return _own_block_placed(outs, blocks)


def _gather8_shapes(blocks):
    return tuple(_sds((8,) + b.shape, b.dtype) for b in blocks)


def _gather8_sems(na):
    return [pltpu.SemaphoreType.DMA((7 * na,)), pltpu.SemaphoreType.DMA((7 * na,))]


def _own_block_placed(outs, blocks):
    ix, iy, ic = _place()
    return tuple(lax.dynamic_update_index_in_dim(o, b, 4 * ix + 2 * iy + ic, 0) for o, b in zip(outs, blocks))


def _gather8_phases(x_refs, out_refs, send_sems, recv_sems):
    na = len(x_refs)
    x, y, c = _place()
    me, sibling = (x, y, c), (x, y, 1 - c)
    chips = [(1 - x, y), (x, 1 - y), (1 - x, 1 - y)]

    def copy(a, k, blk, to, from_input=False):
        dst = out_refs[a].at[4 * blk[0] + 2 * blk[1] + blk[2]]
        return pltpu.make_async_remote_copy(
            src_ref=x_refs[a] if from_input else dst, dst_ref=dst,
            send_sem=send_sems.at[7 * a + k], recv_sem=recv_sems.at[7 * a + k], device_id=to, device_id_type=MESH)

    def first(a):
        return [copy(a, 0, me, sibling, True)] + [copy(a, 1 + j, me, (*chip, c), True) for j, chip in enumerate(chips)]

    def start():
        for a in range(na):
            for cp in first(a):
                cp.start()

    def forward():
        for j, chip in enumerate(chips):
            for a in range(na):
                copy(a, 1 + j, (*chip, c), me).wait_recv()
                copy(a, 4 + j, (*chip, c), sibling).start()

    def finish():
        for a in range(na):
            copy(a, 0, sibling, me).wait_recv()
            for j, chip in enumerate(chips):
                copy(a, 4 + j, (*chip, 1 - c), me).wait_recv()
        for a in range(na):
            for cp in first(a) + [copy(a, 4 + j, (*chip, c), sibling) for j, chip in enumerate(chips)]:
                cp.wait_send()

    return start, forward, finish


def _pair_swap(gs, name):
    na = len(gs)

    def body(*refs):
        start, finish = _swap_phases(refs[:na], refs[na:2 * na], *refs[2 * na:])
        start()
        finish()

    return pl.pallas_call(
        body, name=name, out_shape=_swap_shapes(gs), in_specs=[ANY] * na, out_specs=(ANY,) * na,
        scratch_shapes=_swap_sems(gs),
    )(*gs)


def _swap_shapes(gs):
    return tuple(_sds(g.shape[:1] + g.shape[2:], g.dtype) for g in gs)


def _swap_sems(gs):
    n = sum(g.shape[0] for g in gs)
    return [pltpu.SemaphoreType.DMA((n,)), pltpu.SemaphoreType.DMA((n,))]


def _swap_phases(g_refs, land_refs, send_sems, recv_sems):
    x, y, c = _place()

    def copies():
        cps, k = [], 0
        for g_ref, land_ref in zip(g_refs, land_refs):
            for s in range(g_ref.shape[0]):
                cps.append(pltpu.make_async_remote_copy(
                    src_ref=g_ref.at[s, 1 - c], dst_ref=land_ref.at[s], send_sem=send_sems.at[k],
                    recv_sem=recv_sems.at[k], device_id=(x, y, 1 - c), device_id_type=MESH))
                k += 1
        return cps

    def start():
        for cp in copies():
            cp.start()

    def finish():
        for cp in copies():
            cp.wait()

    return start, finish


def _scatter_shapes(hs):
    return tuple(_sds((3,) + h.shape[1:], h.dtype) for h in hs)


def _scatter_sems(na):
    return [pltpu.SemaphoreType.DMA((3 * na,)), pltpu.SemaphoreType.DMA((3 * na,))]


def _scatter_phases(h_refs, land_refs, send_sems, recv_sems):
    x, y, c = _place()
    chips = [(1 - x, y), (x, 1 - y), (1 - x, 1 - y)]

    def copies():
        return [pltpu.make_async_remote_copy(
            src_ref=h_refs[a].at[2 * tx + ty], dst_ref=land_refs[a].at[j], send_sem=send_sems.at[3 * a + j],
            recv_sem=recv_sems.at[3 * a + j], device_id=(tx, ty, c), device_id_type=MESH)
            for a in range(len(h_refs)) for j, (tx, ty) in enumerate(chips)]

    def start():
        for cp in copies():
            cp.start()

    def finish():
        for cp in copies():
            cp.wait()

    return start, finish


def _pair_gather(fs, name):
    na = len(fs)

    def body(*refs):
        out_refs = refs[na:2 * na]
        send_sems, recv_sems = refs[2 * na:]
        x, y, c = _place()
        sends = [pltpu.make_async_remote_copy(
            src_ref=out_refs[a].at[c], dst_ref=out_refs[a].at[c], send_sem=send_sems.at[a], recv_sem=recv_sems.at[a],
            device_id=(x, y, 1 - c), device_id_type=MESH) for a in range(na)]
        recvs = [pltpu.make_async_remote_copy(
            src_ref=out_refs[a].at[c], dst_ref=out_refs[a].at[1 - c], send_sem=send_sems.at[a],
            recv_sem=recv_sems.at[a], device_id=(x, y, 1 - c), device_id_type=MESH) for a in range(na)]
        for cp in sends:
            cp.start()
        for cp in recvs:
            cp.wait_recv()
        for cp in sends:
            cp.wait_send()

    return pl.pallas_call(
        body, name=name, out_shape=tuple(_sds(f.shape, f.dtype) for f in fs),
        in_specs=[ANY] * na, out_specs=(ANY,) * na, input_output_aliases={a: a for a in range(na)},
        scratch_shapes=[pltpu.SemaphoreType.DMA((na,)), pltpu.SemaphoreType.DMA((na,))],
    )(*fs)


def _row_tile(r, n, itemsize=4, budget=1 << 21):
    if r * n * itemsize <= budget:
        return r
    best = None
    for tr in range(16, r, 16):
        if r % tr == 0 and tr * n * itemsize <= budget:
            best = tr
    assert best is not None, (r, n)
    return best


def _add_pair(g, land, cidx, name):
    ns, _, r, n = g.shape
    tr = _row_tile(r, n)

    def body(c_ref, a_ref, b_ref, o_ref, ob_ref):
        s = a_ref[...] + b_ref[...]
        o_ref[...] = s
        ob_ref[...] = s.astype(BF16)

    out = pl.BlockSpec((None, tr, n), lambda s, i, cr: (s, i, 0))
    return pl.pallas_call(
        body, name=name, out_shape=(_sds((ns, r, n), F32), _sds((ns, r, n), BF16)),
        grid_spec=pltpu.PrefetchScalarGridSpec(
            num_scalar_prefetch=1, grid=(ns, r // tr),
            in_specs=[pl.BlockSpec((None, None, tr, n), lambda s, i, cr: (s, cr[0], i, 0)), out],
            out_specs=(out, out)),
        compiler_params=_params("arbitrary", "arbitrary"),
    )(cidx, g, land)


def _add_pair_whole(gs, lands, cidx, name):
    k = len(gs)

    def body(c_ref, *refs):
        for a_ref, b_ref, o_ref, ob_ref in zip(refs[:k], refs[k:2 * k], refs[2 * k:3 * k], refs[3 * k:]):
            s = a_ref[...] + b_ref[...]
            o_ref[...] = s
            ob_ref[...] = s.astype(BF16)

    half = lambda g: pl.BlockSpec((g.shape[0], None) + g.shape[2:], lambda i, cr: (0, cr[0], 0, 0))
    whole = lambda g: pl.BlockSpec(g.shape[:1] + g.shape[2:], lambda i, cr: (0, 0, 0))
    shapes = lambda dt: tuple(_sds(g.shape[:1] + g.shape[2:], dt) for g in gs)
    res = pl.pallas_call(
        body, name=name, out_shape=shapes(F32) + shapes(BF16),
        grid_spec=pltpu.PrefetchScalarGridSpec(
            num_scalar_prefetch=1, grid=(1,),
            in_specs=[half(g) for g in gs] + [whole(g) for g in gs],
            out_specs=tuple(whole(g) for g in gs) * 2),
        compiler_params=_params("arbitrary"),
    )(cidx, *gs, *lands)
    return list(zip(res[:k], res[k:]))


def _add_chips_whole(hs, lands, own_c, name):
    k = len(hs)

    def body(o_idx, *refs):
        for h_ref, l_ref, o_ref in zip(refs[:k], refs[k:2 * k], refs[2 * k:]):
            o_ref[...] = ((h_ref[...] + l_ref[0].astype(F32)) + l_ref[1].astype(F32)) + l_ref[2].astype(F32)

    return pl.pallas_call(
        body, name=name, out_shape=tuple(_sds((2,) + h.shape[1:], F32) for h in hs),
        grid_spec=pltpu.PrefetchScalarGridSpec(
            num_scalar_prefetch=1, grid=(1,),
            in_specs=[pl.BlockSpec((None,) + h.shape[1:], lambda i, o: (o[0], 0, 0)) for h in hs]
            + [pl.BlockSpec(l.shape, lambda i, o: (0, 0, 0)) for l in lands],
            out_specs=tuple(pl.BlockSpec((None,) + h.shape[1:], lambda i, o: (o[1], 0, 0)) for h in hs)),
        compiler_params=_params("arbitrary"),
    )(own_c, *hs, *lands)


def _add_chips(h, land, own_c, name):
    _, r, n = h.shape
    tr = _row_tile(r, n)

    def body(o_idx, h_ref, l_ref, o_ref):
        o_ref[...] = ((h_ref[...] + l_ref[0].astype(F32)) + l_ref[1].astype(F32)) + l_ref[2].astype(F32)

    return pl.pallas_call(
        body, name=name, out_shape=_sds((2, r, n), F32),
        grid_spec=pltpu.PrefetchScalarGridSpec(
            num_scalar_prefetch=1, grid=(r // tr,),
            in_specs=[pl.BlockSpec((None, tr, n), lambda i, o: (o[0], i, 0)),
                      pl.BlockSpec((3, tr, n), lambda i, o: (0, i, 0))],
            out_specs=pl.BlockSpec((None, tr, n), lambda i, o: (o[1], i, 0))),
        compiler_params=_params("arbitrary"),
    )(own_c, h, land)


def _adam_math(w, g, m, v):
    nm = ADAM_B1 * m + (1.0 - ADAM_B1) * g
    nv = ADAM_B2 * v + (1.0 - ADAM_B2) * (g * g)
    m_hat = nm / (1.0 - ADAM_B1 ** ADAM_STEP)
    v_hat = nv / (1.0 - ADAM_B2 ** ADAM_STEP)
    return -ADAM_LR * (m_hat / (jnp.sqrt(v_hat) + ADAM_EPS) + ADAM_WD * w), nm, nv


def _adamw(w, g, m, v, name):
    r, n = w.shape
    tr = _row_tile(r, n)

    def body(w_ref, g_ref, m_ref, v_ref, d_ref, nm_ref, nv_ref):
        d_ref[...], nm_ref[...], nv_ref[...] = _adam_math(w_ref[...], g_ref[...], m_ref[...], v_ref[...])

    spec = pl.BlockSpec((tr, n), lambda i: (i, 0))
    return pl.pallas_call(
        body, name=name, out_shape=(_sds((r, n), F32),) * 3, grid=(r // tr,),
        in_specs=[spec] * 4, out_specs=(spec,) * 3, compiler_params=_params("arbitrary"),
    )(w, g, m, v)


def _adamw_small(ws, gs, ms, vs):
    k = len(ws)

    def body(*refs):
        ins, outs = refs[:4 * k], refs[4 * k:]
        for j in range(k):
            d, nm, nv = _adam_math(ins[j][...], ins[k + j][...], ins[2 * k + j][...], ins[3 * k + j][...])
            outs[j][...] = d
            outs[k + j][...] = nm
            outs[2 * k + j][...] = nv

    shapes = tuple(_sds(w.shape, F32) for w in ws)
    res = pl.pallas_call(body, name="adamw_small", out_shape=shapes * 3,
                         compiler_params=pltpu.CompilerParams(vmem_limit_bytes=VMEM_LIMIT))(*ws, *gs, *ms, *vs)
    return res[:k], res[k:2 * k], res[2 * k:]


def _ada_mod(c_all, w_sh, b_sh):
    b, _ = c_all.shape
    n = w_sh.shape[1]

    def body(c_ref, w_ref, b_ref, o_ref):
        cc = c_ref[...]
        ca = (cc * jax.nn.sigmoid(cc)).astype(BF16)
        o_ref[...] = _dot(ca, w_ref[...].astype(BF16)) + b_ref[...]

    return pl.pallas_call(body, name="ada_mod", out_shape=_sds((b, n), F32),
                          compiler_params=pltpu.CompilerParams(vmem_limit_bytes=VMEM_LIMIT))(c_all, w_sh, b_sh)


def _ada_bwd(c_all, dmod_all, dmod_sh, parts):
    b, d = c_all.shape
    n6 = dmod_all.shape[1]
    n = dmod_sh.shape[1]
    k = len(parts)

    def body(*refs):
        c_ref, da_ref, ds_ref = refs[:3]
        p_refs = refs[3:3 + k]
        dw_ref, db_ref = refs[3 + k:5 + k]
        s_refs = refs[5 + k:]
        cc = c_ref[...]
        ca = (cc * jax.nn.sigmoid(cc)).astype(BF16)
        dw_ref[...] = _dot_tn(ca, ds_ref[...].astype(BF16))
        db_ref[...] = jnp.sum(da_ref[...], axis=0, keepdims=True)
        for p_ref, s_ref in zip(p_refs, s_refs):
            tot = p_ref[0]
            for j in range(1, p_ref.shape[0]):
                tot = tot + p_ref[j]
            s_ref[...] = tot

    return pl.pallas_call(
        body, name="ada_bwd",
        out_shape=(_sds((d, n), F32), _sds((1, n6), F32)) + tuple(_sds(p.shape[1:], F32) for p in parts),
        compiler_params=pltpu.CompilerParams(vmem_limit_bytes=VMEM_LIMIT),
    )(c_all, dmod_all, dmod_sh, *parts)


def _fwd_in(x, g1, mod3, win_p, tm, tps):
    t, d = x.shape
    p_glu, p_q, npad = _layout(d)

    def body(x_ref, g_ref, mod_ref, w_hbm, h_ref, zm_ref, zglu_ref, zgate_ref, u0_ref, w_ref):
        _load_resident(pl.program_id(0), [(w_hbm, w_ref)])
        n, _ = _rms(x_ref[...])
        h = ((n * g_ref[...]) * (1.0 + mod_ref[1:2, :]) + mod_ref[0:1, :]).astype(BF16)
        h_ref[...] = h
        z = _dot(h, w_ref[...])
        zgate_ref[...] = z[:, :p_glu]
        zglu = z[:, p_glu:p_q]
        zglu_ref[...] = zglu
        zm_ref[...] = z[:, p_q:]
        u0_ref[...] = zglu[:, :CONV_CH] * jax.nn.sigmoid(zglu[:, CONV_CH:])

    return pl.pallas_call(
        body, name="fwd_in", grid=(t // tm,),
        out_shape=(_sds((t, d), BF16), _sds((t, MLA_IN), F32), _sds((t, 2 * CONV_CH), F32), _sds((t, 2 * d), F32),
                   _sds((t, CONV_CH), F32)),
        in_specs=[_row(tm, d), _full((1, d)), _modspec(d, tps), ANY],
        out_specs=(_row(tm, d), _row(tm, MLA_IN), _row(tm, 2 * CONV_CH), _row(tm, 2 * d), _row(tm, CONV_CH)),
        scratch_shapes=[pltpu.VMEM(win_p.shape, BF16)],
        compiler_params=_params("arbitrary"),
    )(x, g1, mod3, win_p)


def _mla_prep(zm, gql, gkvl, gq, gk, tabs, wuq_p, wk_p, wv_p, tm, tps):
    t = zm.shape[0]
    c_t, s1_t, s2_t = tabs
    tab = pl.BlockSpec((tm, LANES), lambda i: (i % tps, 0))

    def body(zm_ref, gql_ref, gkvl_ref, gq_ref, gk_ref, c_ref, s1_ref, s2_ref, wuq_ref, wk_ref, wv_ref,
             q_ref, k_ref, v_ref, qln_ref, kvn_ref):
        c, s1, s2 = c_ref[...], s1_ref[...], s2_ref[...]
        nq, _ = _rms(zm_ref[:, :Q_RANK])
        qln = (nq * gql_ref[...]).astype(BF16)
        qln_ref[...] = qln
        qpre = _dot(qln, wuq_ref[...])
        nkv, _ = _rms(zm_ref[:, Q_RANK:OFF_KV])
        kvn = (nkv * gkvl_ref[...]).astype(BF16)
        kvn_ref[...] = kvn
        knope = _dot(kvn, wk_ref[...])
        v_ref[...] = _dot(kvn, wv_ref[...]).astype(BF16)
        zkr_v = zm_ref[:, OFF_KV:]
        kr_roped = _rope(zkr_v * gk_ref[...], c, s1, s2)
        slabs = [slice(hd * LANES, (hd + 1) * LANES) for hd in range(N_HEADS)]
        rq = [_head_rms(qpre[:, sl])[1] for sl in slabs]
        rk = [_head_rms(knope[:, sl] + zkr_v)[1] for sl in slabs]
        for hd, sl in enumerate(slabs):
            q_ref[:, sl] = _rope((qpre[:, sl] * rq[hd]) * gq_ref[...], c, s1, s2).astype(BF16)
            k_ref[:, sl] = (rk[hd] * (knope[:, sl] * gk_ref[...] + kr_roped)).astype(BF16)

    return pl.pallas_call(
        body, name="mla_prep", grid=(t // tm,),
        out_shape=(_sds((t, HW), BF16),) * 3 + (_sds((t, Q_RANK), BF16), _sds((t, KV_RANK), BF16)),
        in_specs=[_row(tm, MLA_IN), _full((1, Q_RANK)), _full((1, KV_RANK)),
                  _full((1, LANES)), _full((1, LANES)), tab, tab, tab,
                  _full(wuq_p.shape), _full(wk_p.shape), _full(wv_p.shape)],
        out_specs=(_row(tm, HW),) * 3 + (_row(tm, Q_RANK), _row(tm, KV_RANK)),
        compiler_params=_params("arbitrary"),
    )(zm, gql, gkvl, gq, gk, c_t, s1_t, s2_t, wuq_p, wk_p, wv_p)


AHEAD = 2
ROW_BAND = 256
SM_SCALE = QK_HEAD ** -0.5
EXP2_SCALE = SM_SCALE * 1.4426950408889634


def _diag_mask():
    rc = jnp.right_shift(lax.broadcasted_iota(jnp.int32, (BQ, 1), 0), CHUNK_SHIFT)
    cc = jnp.right_shift(lax.broadcasted_iota(jnp.int32, (1, BQ), 1), CHUNK_SHIFT)
    return rc >= cc


def _scores(q_i, k_ref, lo, e):
    return (_dot_nt(q_i, k_ref[:lo, :]) if lo else None), _dot_nt(q_i, k_ref[lo:e, :])


def _softmax_parts(scores, mask):
    sp, sd = scores
    sd = jnp.where(mask, sd, jnp.finfo(F32).min)
    m = jnp.max(sd, axis=-1, keepdims=True)
    if sp is not None:
        m = jnp.maximum(m, jnp.max(sp, axis=-1, keepdims=True))
    pd = jnp.exp2((sd - m) * EXP2_SCALE)
    l = jnp.sum(pd, axis=-1, keepdims=True)
    pp = None
    if sp is not None:
        pp = jnp.exp2((sp - m) * EXP2_SCALE)
        l = l + jnp.sum(pp, axis=-1, keepdims=True)
    return pp, pd, l


def _attn_fwd(q, k, v, nseq, seq, gather=()):
    t = q.shape[0]
    na = len(gather)
    blk = pl.BlockSpec((seq, LANES), lambda b, h: (b, h))
    n_steps = nseq * N_HEADS

    def body(q_ref, k_ref, v_ref, *rest):
        o_ref = rest[na]
        if na:
            start, forward, finish = _gather8_phases(rest[:na], rest[na + 1:2 * na + 1], *rest[2 * na + 1:])
            step = pl.program_id(0) * N_HEADS + pl.program_id(1)
            pl.when(step == 0)(start)
            pl.when(step == (7 * n_steps) // 8)(forward)
        mask = _diag_mask()
        nb = seq // BQ
        block_scores = lambda j: _scores(q_ref[j * BQ:(j + 1) * BQ, :], k_ref, j * BQ, (j + 1) * BQ)
        ahead = [block_scores(j) for j in range(min(AHEAD, nb))]
        for i in range(nb):
            lo, e = i * BQ, (i + 1) * BQ
            cur = ahead.pop(0)
            if i + AHEAD < nb:
                ahead.append(block_scores(i + AHEAD))
            pp, pd, l = _softmax_parts(cur, mask)
            o = _dot(pd.astype(BF16), v_ref[lo:e, :])
            if lo:
                o = o + _dot(pp.astype(BF16), v_ref[:lo, :])
            o_ref[lo:e, :] = (o * (1.0 / l)).astype(BF16)
        if na:
            pl.when(step == n_steps - 1)(finish)

    res = pl.pallas_call(
        body, name="attn_fwd", grid=(nseq, N_HEADS), out_shape=(_sds((t, HW), BF16),) + _gather8_shapes(gather),
        in_specs=[blk, blk, blk] + [ANY] * na, out_specs=(blk,) + (ANY,) * na,
        scratch_shapes=_gather8_sems(na) if na else [],
        compiler_params=_params("arbitrary", "arbitrary"),
    )(q, k, v, *gather)
    return res[0], (_own_block_placed(res[1:], gather) if na else ())


def _fwd_mix(attn, u0, zgate, x, mod3, wo_p, cw, cb, lng, lnb, wpw, wout, tm, tps):
    t, d = x.shape
    hpt = tm // HALO
    cwc, cbc = _by_lane_chunk(cw), _by_lane_chunk(cb)

    def body(a_ref, u_ref, uh_ref, zg_ref, x_ref, mod_ref, wo_ref, cw_ref, cb_ref, lng_ref, lnb_ref, wpw_ref, wout_ref,
             x1_ref, mixed_ref, mpre_ref, ya_ref, yb_ref, u1_ref, u3_ref, ext_ref):
        i = pl.program_id(0)
        ya = _dot(a_ref[...], wo_ref[...])
        ya_ref[...] = ya
        first = (i % tps) == 0
        _fill_shifted(ext_ref, jnp.where(first, 0.0, uh_ref[...]), u_ref[...])
        for lc, ls in _lane_chunks():
            acc = jnp.broadcast_to(cb_ref[lc], (tm, LANES))
            for kk in range(CONV_W):
                o = HALO - (CONV_W - 1) + kk
                a = (o // SUBLANES) * SUBLANES
                acc = acc + cw_ref[lc, kk:kk + 1, :] * ext_ref[o % SUBLANES, lc, a:a + tm, :]
            u1_ref[:, ls] = acc
        acc = u1_ref[...]
        mu = jnp.mean(acc, axis=-1, keepdims=True)
        xc = acc - mu
        rstd = lax.rsqrt(jnp.mean(xc * xc, axis=-1, keepdims=True) + EPS)
        l = (xc * rstd) * lng_ref[...] + lnb_ref[...]
        u3 = (l * jax.nn.sigmoid(l)).astype(BF16)
        u3_ref[...] = u3
        yb = _dot(u3, wpw_ref[...])
        yb_ref[...] = yb
        zg = zg_ref[...]
        mpre = (jax.nn.sigmoid(zg[:, :d]) * ya + jax.nn.sigmoid(zg[:, d:]) * yb).astype(BF16)
        mpre_ref[...] = mpre
        mixed = _dot(mpre, wout_ref[...])
        mixed_ref[...] = mixed
        x1_ref[...] = x_ref[...] + mod_ref[2:3, :] * mixed

    halo = pl.BlockSpec((HALO, CONV_CH), lambda i: (jnp.maximum(i * hpt - 1, 0), 0))
    return pl.pallas_call(
        body, name="fwd_mix", grid=(t // tm,),
        out_shape=(_sds((t, d), F32), _sds((t, d), F32), _sds((t, d), BF16), _sds((t, d), F32), _sds((t, d), F32),
                   _sds((t, CONV_CH), F32), _sds((t, CONV_CH), BF16)),
        in_specs=[_row(tm, HW), _row(tm, CONV_CH), halo, _row(tm, 2 * d), _row(tm, d), _modspec(d, tps),
                  _full(wo_p.shape), _full(cwc.shape), _full(cbc.shape), _full((1, CONV_CH)), _full((1, CONV_CH)),
                  _full(wpw.shape), _full(wout.shape)],
        out_specs=(_row(tm, d), _row(tm, d), _row(tm, d), _row(tm, d), _row(tm, d), _row(tm, CONV_CH),
                   _row(tm, CONV_CH)),
        scratch_shapes=[pltpu.VMEM(_shifted_shape(tm), F32)],
        compiler_params=_params("arbitrary"),
    )(attn, u0, u0, zgate, x, mod3, wo_p, cwc, cbc, lng, lnb, wpw, wout)


def _fwd_ffn(x1, target, g2, mod3, w1, w2, tm, tps):
    t, d = x1.shape
    dff = w1.shape[1]

    def body(x1_ref, tg_ref, g_ref, mod_ref, w1_hbm, w2_hbm,
             h2_ref, a_ref, r_ref, dy_ref, df_ref, dgate_ref, loss_ref, w1_ref, w2_ref):
        i = pl.program_id(0)
        _load_resident(i, [(w1_hbm, w1_ref), (w2_hbm, w2_ref)])
        x1v = x1_ref[...]
        gate2 = mod_ref[5:6, :]
        n, _ = _rms(x1v)
        h2 = ((n * g_ref[...]) * (1.0 + mod_ref[4:5, :]) + mod_ref[3:4, :]).astype(BF16)
        h2_ref[...] = h2
        a = _dot(h2, w1_ref[...])
        a_ref[...] = a
        r = jnp.square(jnp.maximum(a, 0.0)).astype(BF16)
        r_ref[...] = r
        f = _dot(r, w2_ref[...])
        e = (x1v + gate2 * f) - tg_ref[...]
        part = 0.5 * jnp.sum(jnp.mean(e * e, axis=-1, keepdims=True), axis=0, keepdims=True)
        _acc(loss_ref, jnp.broadcast_to(part, loss_ref.shape), i == 0)
        dy = e * (1.0 / d)
        dy_ref[...] = dy
        df_ref[...] = (dy * gate2).astype(BF16)
        _acc(dgate_ref, jnp.sum(dy * f, axis=0, keepdims=True), (i % tps) == 0)

    nseq = t // (tm * tps)
    return pl.pallas_call(
        body, name="fwd_ffn", grid=(t // tm,),
        out_shape=(_sds((t, d), BF16), _sds((t, dff), F32), _sds((t, dff), BF16), _sds((t, d), F32), _sds((t, d), BF16),
                   _sds((nseq, 1, d), F32), _sds((8, LANES), F32)),
        in_specs=[_row(tm, d), _row(tm, d), _full((1, d)), _modspec(d, tps), ANY, ANY],
        out_specs=(_row(tm, d), _row(tm, dff), _row(tm, dff), _row(tm, d), _row(tm, d), _seqv(d, tps),
                   _full((8, LANES))),
        scratch_shapes=[pltpu.VMEM(w1.shape, BF16), pltpu.VMEM(w2.shape, BF16)],
        compiler_params=_params("arbitrary"),
    )(x1, target, g2, mod3, w1, w2)


def _bwd_ffn(df, a, x1, dy, mixed, g2, mod3, w2, w1, tm, tps):
    t, d = x1.shape
    dff = a.shape[1]

    def body(df_ref, a_ref, x1_ref, dy_ref, mx_ref, g_ref, mod_ref, w2_hbm, w1_hbm,
             da_ref, dx1_ref, dmixed_ref, dshift_ref, dscale_ref, dgate1_ref, dg2_ref, w2_ref, w1_ref):
        i = pl.program_id(0)
        _load_resident(i, [(w2_hbm, w2_ref), (w1_hbm, w1_ref)])
        first_seq = (i % tps) == 0
        dr = _dot_nt(df_ref[...], w2_ref[...])
        da = (dr * (2.0 * jnp.maximum(a_ref[...], 0.0))).astype(BF16)
        da_ref[...] = da
        dh2 = _dot_nt(da, w1_ref[...])
        n, r = _rms(x1_ref[...])
        g = g_ref[...]
        sc1 = 1.0 + mod_ref[4:5, :]
        _acc(dshift_ref, jnp.sum(dh2, axis=0, keepdims=True), first_seq)
        _acc(dscale_ref, jnp.sum(dh2 * (n * g), axis=0, keepdims=True), first_seq)
        _acc(dg2_ref, jnp.sum((dh2 * sc1) * n, axis=0, keepdims=True), i == 0)
        dx1 = dy_ref[...] + _rms_bwd(n, r, (dh2 * sc1) * g)
        dx1_ref[...] = dx1
        _acc(dgate1_ref, jnp.sum(dx1 * mx_ref[...], axis=0, keepdims=True), first_seq)
        dmixed_ref[...] = (dx1 * mod_ref[2:3, :]).astype(BF16)

    nseq = t // (tm * tps)
    sv = _sds((nseq, 1, d), F32)
    return pl.pallas_call(
        body, name="bwd_ffn", grid=(t // tm,),
        out_shape=(_sds((t, dff), BF16), _sds((t, d), F32), _sds((t, d), BF16), sv, sv, sv, _sds((1, d), F32)),
        in_specs=[_row(tm, d), _row(tm, dff), _row(tm, d), _row(tm, d), _row(tm, d), _full((1, d)), _modspec(d, tps),
                  ANY, ANY],
        out_specs=(_row(tm, dff), _row(tm, d), _row(tm, d), _seqv(d, tps), _seqv(d, tps), _seqv(d, tps),
                   _full((1, d))),
        scratch_shapes=[pltpu.VMEM(w2.shape, BF16), pltpu.VMEM(w1.shape, BF16)],
        compiler_params=_params("arbitrary"),
    )(df, a, x1, dy, mixed, g2, mod3, w2, w1)


def _bwd_mix(dmixed, zgate, ya, yb, u1, lng, lnb, wout, wo_p, wpw, tm, swap=()):
    t, d = ya.shape
    _, _, npad = _layout(d)
    nw = len(swap)
    n_steps = t // tm

    def body(dm_ref, zg_ref, ya_ref, yb_ref, u1_ref, lng_ref, lnb_ref, wout_ref, wo_ref, wpw_ref, *rest):
        dya_ref, dyb_ref, dz_ref, do_ref, du1_ref, dlng_ref, dlnb_ref, dcb_ref = rest[nw:nw + 8]
        i = pl.program_id(0)
        if nw:
            start, finish = _swap_phases(rest[:nw], rest[nw + 8:2 * nw + 8], *rest[2 * nw + 8:])
            pl.when(i == 0)(start)
        dmpre = _dot_nt(dm_ref[...], wout_ref[...])
        zg = zg_ref[...]
        ga = jax.nn.sigmoid(zg[:, :d])
        gb = jax.nn.sigmoid(zg[:, d:])
        dya = (dmpre * ga).astype(BF16)
        dyb = (dmpre * gb).astype(BF16)
        dya_ref[...] = dya
        dyb_ref[...] = dyb
        dz_ref[:, :d] = ((dmpre * ya_ref[...]) * (ga * (1.0 - ga))).astype(BF16)
        dz_ref[:, d:] = ((dmpre * yb_ref[...]) * (gb * (1.0 - gb))).astype(BF16)
        do_ref[...] = _dot_nt(dya, wo_ref[...]).astype(BF16)
        du3 = _dot_nt(dyb, wpw_ref[...])
        u1 = u1_ref[...]
        mu = jnp.mean(u1, axis=-1, keepdims=True)
        xc = u1 - mu
        rstd = lax.rsqrt(jnp.mean(xc * xc, axis=-1, keepdims=True) + EPS)
        nh = xc * rstd
        l = nh * lng_ref[...] + lnb_ref[...]
        sg = jax.nn.sigmoid(l)
        dl = du3 * (sg * (1.0 + l * (1.0 - sg)))
        _acc(dlng_ref, jnp.sum(dl * nh, axis=0, keepdims=True), i == 0)
        _acc(dlnb_ref, jnp.sum(dl, axis=0, keepdims=True), i == 0)
        dnh = dl * lng_ref[...]
        du1 = rstd * (dnh - jnp.mean(dnh, axis=-1, keepdims=True) - nh * jnp.mean(dnh * nh, axis=-1, keepdims=True))
        du1_ref[...] = du1
        _acc(dcb_ref, jnp.sum(du1, axis=0, keepdims=True), i == 0)
        if nw:
            pl.when(i == n_steps - 1)(finish)

    cv = _sds((1, CONV_CH), F32)
    res = pl.pallas_call(
        body, name="bwd_mix", grid=(n_steps,),
        out_shape=(_sds((t, d), BF16), _sds((t, d), BF16), _sds((t, npad), BF16), _sds((t, HW), BF16),
                   _sds((t, CONV_CH), F32), cv, cv, cv) + _swap_shapes(swap),
        in_specs=[_row(tm, d), _row(tm, 2 * d), _row(tm, d), _row(tm, d), _row(tm, CONV_CH), _full((1, CONV_CH)),
                  _full((1, CONV_CH)), _full(wout.shape), _full(wo_p.shape), _full(wpw.shape)] + [ANY] * nw,
        out_specs=(_row(tm, d), _row(tm, d), _row(tm, 2 * d), _row(tm, HW), _row(tm, CONV_CH),
                   _full((1, CONV_CH)), _full((1, CONV_CH)), _full((1, CONV_CH))) + (ANY,) * nw,
        scratch_shapes=_swap_sems(swap) if nw else [],
        compiler_params=_params("arbitrary"),
    )(dmixed, zgate, ya, yb, u1, lng, lnb, wout, wo_p, wpw, *swap)
    return res[:8] + (res[8:],)


def _bwd_conv(dz, du1, u0, zglu, cw, tm, tps):
    t = du1.shape[0]
    d = (dz.shape[1] - MLA_IN - 2 * CONV_CH) // 2
    p_glu, _, _ = _layout(d)
    hpt = tm // HALO
    last_blk = t // HALO - 1
    cwc = _by_lane_chunk(cw)

    def body(dz_hbm, du_ref, dun_ref, u_ref, zl_ref, cw_ref, dzl_ref, dcw_ref, dext_ref, uc_ref, dcw8_ref, du0_ref):
        i = pl.program_id(0)
        last = (i % tps) == (tps - 1)
        _fill_shifted(dext_ref, du_ref[...], jnp.where(last, 0.0, dun_ref[...]))
        for lc, ls in _lane_chunks():
            uc_ref[lc] = u_ref[:, ls]

        @pl.when(i == 0)
        def _():
            dcw8_ref[...] = jnp.zeros_like(dcw8_ref)

        groups = CONV_ROWS // SUBLANES

        def conv_chunk(c, carry):
            lc, r0 = _conv_chunk(c)
            u = uc_ref[lc, pl.ds(r0, CONV_ROWS), :]
            du0 = jnp.zeros((CONV_ROWS, LANES), F32)
            for kk in range(CONV_W):
                win = _shifted(dext_ref, CONV_W - 1 - kk, lc, r0)
                prod = u * win
                part = prod[:SUBLANES]
                for g in range(1, groups):
                    part = part + prod[g * SUBLANES:(g + 1) * SUBLANES]
                dcw8_ref[lc, kk] += part
                du0 = du0 + cw_ref[lc, kk:kk + 1, :] * win
            du0_ref[lc, pl.ds(r0, CONV_ROWS), :] = du0
            return carry

        lax.fori_loop(0, CONV_LC * (tm // CONV_ROWS), conv_chunk, 0)

        @pl.when(i == pl.num_programs(0) - 1)
        def _():
            for lc, ls in _lane_chunks():
                dcw_ref[:, ls] = jnp.sum(dcw8_ref[lc], axis=1)

        for lc, ls in _lane_chunks():
            du0 = du0_ref[lc]
            ga = zl_ref[:, ls]
            sb = jax.nn.sigmoid(zl_ref[:, CONV_CH + lc * LANES:CONV_CH + (lc + 1) * LANES])
            dzl_ref[:, ls] = (du0 * sb).astype(BF16)
            dzl_ref[:, CONV_CH + lc * LANES:CONV_CH + (lc + 1) * LANES] = ((du0 * ga) * (sb * (1.0 - sb))).astype(BF16)

    nxt = pl.BlockSpec((HALO, CONV_CH), lambda i: (jnp.minimum((i + 1) * hpt, last_blk), 0))
    glu_blk = p_glu // (2 * CONV_CH)
    return pl.pallas_call(
        body, name="bwd_conv", grid=(t // tm,),
        out_shape=(_sds(dz.shape, BF16), _sds(cw.shape, F32)),
        in_specs=[ANY, _row(tm, CONV_CH), nxt, _row(tm, CONV_CH), _row(tm, 2 * CONV_CH), _full(cwc.shape)],
        out_specs=(pl.BlockSpec((tm, 2 * CONV_CH), lambda i: (i, glu_blk)), _full(cw.shape)),
        scratch_shapes=[pltpu.VMEM(_shifted_shape(tm), F32), pltpu.VMEM((CONV_LC, tm, LANES), F32),
                        pltpu.VMEM((CONV_LC, HALO, SUBLANES, LANES), F32), pltpu.VMEM((CONV_LC, tm, LANES), F32)],
        input_output_aliases={0: 0},
        compiler_params=_params("arbitrary"),
    )(dz, du1, du1, u0, zglu, cwc)


def _attn_bwd(q, k, v, do, nseq, seq, scatter=()):
    t = q.shape[0]
    ns = len(scatter)
    blk = pl.BlockSpec((seq, LANES), lambda b, h: (b, h))
    n_steps = nseq * N_HEADS

    def body(q_ref, k_ref, v_ref, do_ref, *rest):
        dq_ref, dk_ref, dv_ref = rest[ns:ns + 3]
        dka_ref, dva_ref = rest[2 * ns + 3:2 * ns + 5]
        if ns:
            start, finish = _scatter_phases(rest[:ns], rest[ns + 3:2 * ns + 3], *rest[2 * ns + 5:])
            step = pl.program_id(0) * N_HEADS + pl.program_id(1)
            pl.when(step == 0)(start)
        dka_ref[...] = jnp.zeros_like(dka_ref)
        dva_ref[...] = jnp.zeros_like(dva_ref)
        mask = _diag_mask()
        nb = seq // BQ
        block = lambda j: (_scores(q_ref[j * BQ:(j + 1) * BQ, :], k_ref, j * BQ, (j + 1) * BQ),
                           _scores(do_ref[j * BQ:(j + 1) * BQ, :], v_ref, j * BQ, (j + 1) * BQ))
        ahead = [block(j) for j in range(min(AHEAD, nb))]
        for i in range(nb):
            lo, e = i * BQ, (i + 1) * BQ
            q_i = q_ref[lo:e, :]
            do_i = do_ref[lo:e, :]
            scores, (dpp, dpd) = ahead.pop(0)
            if i + AHEAD < nb:
                ahead.append(block(i + AHEAD))
            pp, pd, l = _softmax_parts(scores, mask)
            inv = 1.0 / l
            pd = pd * inv
            delta = jnp.sum(pd * dpd, axis=-1, keepdims=True)
            if lo:
                pp = pp * inv
                delta = delta + jnp.sum(pp * dpp, axis=-1, keepdims=True)
            dsd = (pd * (dpd - delta)).astype(BF16)
            dq = _dot(dsd, k_ref[lo:e, :])
            dka_ref[lo:e, :] += _dot_tn(dsd, q_i)
            dva_ref[lo:e, :] += _dot_tn(pd.astype(BF16), do_i)
            if lo:
                dsp = (pp * (dpp - delta)).astype(BF16)
                dq = dq + _dot(dsp, k_ref[:lo, :])
                dka_ref[:lo, :] += _dot_tn(dsp, q_i)
                dva_ref[:lo, :] += _dot_tn(pp.astype(BF16), do_i)
            dq_ref[lo:e, :] = dq * SM_SCALE
        dk_ref[...] = dka_ref[...] * SM_SCALE
        dv_ref[...] = dva_ref[...].astype(BF16)
        if ns:
            pl.when(step == n_steps - 1)(finish)

    res = pl.pallas_call(
        body, name="attn_bwd", grid=(nseq, N_HEADS),
        out_shape=(_sds((t, HW), F32), _sds((t, HW), F32), _sds((t, HW), BF16)) + _scatter_shapes(scatter),
        in_specs=[blk] * 4 + [ANY] * ns, out_specs=(blk,) * 3 + (ANY,) * ns,
        scratch_shapes=[pltpu.VMEM((seq, LANES), F32), pltpu.VMEM((seq, LANES), F32)]
        + (_scatter_sems(ns) if ns else []),
        compiler_params=_params("arbitrary", "arbitrary"),
    )(q, k, v, do, *scatter)
    return res[0], res[1], res[2], res[3:]


def _mla_bwd(dz, dq, dk, dv, zm, gql, gkvl, gq, gk, tabs, wuq_p, wk_p, wv_p, tm, tps):
    t = zm.shape[0]
    d = (dz.shape[1] - MLA_IN - 2 * CONV_CH) // 2
    _, p_q, _ = _layout(d)
    c_t, s1_t, s2_t = tabs
    tab = pl.BlockSpec((tm, LANES), lambda i: (i % tps, 0))

    def body(dz_hbm, dq_ref, dk_ref, dv_ref, zm_ref, gql_ref, gkvl_ref, gq_ref, gk_ref, c_ref, s1_ref, s2_ref,
             wuq_ref, wk_ref, wv_ref,
             dzm_ref, dqpre_ref, dkh_ref, dgq_ref, dgk_ref, dgql_ref, dgkvl_ref):
        i = pl.program_id(0)
        c, s1, s2 = c_ref[...], s1_ref[...], s2_ref[...]
        nq, rq = _rms(zm_ref[:, :Q_RANK])
        qpre = _dot((nq * gql_ref[...]).astype(BF16), wuq_ref[...])
        nkv, rkv = _rms(zm_ref[:, Q_RANK:OFF_KV])
        knope = _dot((nkv * gkvl_ref[...]).astype(BF16), wk_ref[...])
        zkr_v = zm_ref[:, OFF_KV:]
        gk = gk_ref[...]
        kr_roped = _rope(zkr_v * gk, c, s1, s2)
        dgq = jnp.zeros((1, LANES), F32)
        dgk = jnp.zeros((1, LANES), F32)
        dzkr = jnp.zeros((tm, LANES), F32)
        dt_sum = jnp.zeros((tm, LANES), F32)
        slabs = [slice(hd * LANES, (hd + 1) * LANES) for hd in range(N_HEADS)]
        gq = gq_ref[...]
        rqh = [_head_rms(qpre[:, sl])[1] for sl in slabs]
        rkh = [_head_rms(knope[:, sl] + zkr_v)[1] for sl in slabs]
        dyr = [_rope_t(dq_ref[:, sl], c, s1, s2) for sl in slabs]
        nqh = [qpre[:, sl] * rqh[hd] for hd, sl in enumerate(slabs)]
        sq = [jnp.sum((dyr[hd] * gq) * nqh[hd], axis=-1, keepdims=True) for hd in range(N_HEADS)]
        dr = [jnp.sum(dk_ref[:, sl] * (knope[:, sl] * gk + kr_roped), axis=-1, keepdims=True) for sl in slabs]
        for hd, sl in enumerate(slabs):
            dgq = dgq + jnp.sum(dyr[hd] * nqh[hd], axis=0, keepdims=True)
            dqpre_ref[:, sl] = (rqh[hd] * (dyr[hd] * gq - nqh[hd] * (sq[hd] * (1.0 / QK_HEAD)))).astype(BF16)
            kn = knope[:, sl]
            r = rkh[hd]
            dt = dk_ref[:, sl] * r
            via_r = (dr[hd] * (r * r * r) * (-1.0 / QK_HEAD)) * (kn + zkr_v)
            dgk = dgk + jnp.sum(dt * kn, axis=0, keepdims=True)
            dt_sum = dt_sum + dt
            dzkr = dzkr + via_r
            dkh_ref[:, sl] = (dt * gk + via_r).astype(BF16)
        de = _rope_t(dt_sum, c, s1, s2)
        dzkr = dzkr + de * gk
        dgk = dgk + jnp.sum(de * zkr_v, axis=0, keepdims=True)
        _acc(dgq_ref, dgq[:, :QK_HEAD], i == 0)
        _acc(dgk_ref, dgk[:, :QK_HEAD], i == 0)
        dzm_ref[:, OFF_KV:] = dzkr.astype(BF16)
        dqln = _dot_nt(dqpre_ref[...], wuq_ref[...])
        _acc(dgql_ref, jnp.sum(dqln * nq, axis=0, keepdims=True), i == 0)
        dzm_ref[:, :Q_RANK] = _rms_bwd(nq, rq, dqln * gql_ref[...]).astype(BF16)
        dkvn = _dot_nt(dkh_ref[...], wk_ref[...]) + _dot_nt(dv_ref[...], wv_ref[...])
        _acc(dgkvl_ref, jnp.sum(dkvn * nkv, axis=0, keepdims=True), i == 0)
        dzm_ref[:, Q_RANK:OFF_KV] = _rms_bwd(nkv, rkv, dkvn * gkvl_ref[...]).astype(BF16)

    return pl.pallas_call(
        body, name="mla_bwd", grid=(t // tm,),
        out_shape=(_sds(dz.shape, BF16), _sds((t, HW), BF16), _sds((t, HW), BF16), _sds((1, QK_HEAD), F32),
                   _sds((1, QK_HEAD), F32), _sds((1, Q_RANK), F32), _sds((1, KV_RANK), F32)),
        in_specs=[ANY, _row(tm, HW), _row(tm, HW), _row(tm, HW), _row(tm, MLA_IN),
                  _full((1, Q_RANK)), _full((1, KV_RANK)), _full((1, LANES)), _full((1, LANES)), tab, tab, tab,
                  _full(wuq_p.shape), _full(wk_p.shape), _full(wv_p.shape)],
        out_specs=(pl.BlockSpec((tm, MLA_IN), lambda i: (i, p_q // MLA_IN)), _row(tm, HW), _row(tm, HW),
                   _full((1, QK_HEAD)), _full((1, QK_HEAD)), _full((1, Q_RANK)), _full((1, KV_RANK))),
        input_output_aliases={0: 0},
        compiler_params=_params("arbitrary"),
    )(dz, dq, dk, dv, zm, gql, gkvl, gq, gk, c_t, s1_t, s2_t, wuq_p, wk_p, wv_p)


def _bwd_in(dz, x, dx1, g1, mod3, win_p, tm, tps, scatter=()):
    t, d = x.shape
    npad = dz.shape[1]

    ns = len(scatter)
    n_steps = t // tm

    def body(dz_ref, x_ref, dx1_ref, g_ref, mod_ref, wt_hbm, *rest):
        gx_ref, dshift_ref, dscale_ref, dg1_ref = rest[ns:ns + 4]
        wt_ref = rest[2 * ns + 4]
        i = pl.program_id(0)
        if ns:
            start, finish = _scatter_phases(rest[:ns], rest[ns + 4:2 * ns + 4], *rest[2 * ns + 5:])
            pl.when(i == 0)(start)
        _load_resident(i, [(wt_hbm, wt_ref)])
        first_seq = (i % tps) == 0
        g = g_ref[...]
        sc1 = 1.0 + mod_ref[1:2, :]
        nb = max(tm // ROW_BAND, 1)
        bands = [slice(b * (tm // nb), (b + 1) * (tm // nb)) for b in range(nb)]
        dhs = [_dot_nt(dz_ref[rows, :], wt_ref[...]) for rows in bands]
        sums = [jnp.zeros((1, d), F32)] * 3
        col = lambda v: jnp.sum(v, axis=0, keepdims=True)
        for rows, dh in zip(bands, dhs):
            n, r = _rms(x_ref[rows, :])
            sums = [sums[0] + col(dh), sums[1] + col(dh * (n * g)), sums[2] + col((dh * sc1) * n)]
            gx_ref[rows, :] = dx1_ref[rows, :] + _rms_bwd(n, r, (dh * sc1) * g)
        _acc(dshift_ref, sums[0], first_seq)
        _acc(dscale_ref, sums[1], first_seq)
        _acc(dg1_ref, sums[2], i == 0)
        if ns:
            pl.when(i == n_steps - 1)(finish)

    nseq = t // (tm * tps)
    sv = _sds((nseq, 1, d), F32)
    res = pl.pallas_call(
        body, name="bwd_in", grid=(n_steps,),
        out_shape=(_sds((t, d), F32), sv, sv, _sds((1, d), F32)) + _scatter_shapes(scatter),
        in_specs=[_row(tm, npad), _row(tm, d), _row(tm, d), _full((1, d)), _modspec(d, tps), ANY] + [ANY] * ns,
        out_specs=(_row(tm, d), _seqv(d, tps), _seqv(d, tps), _full((1, d))) + (ANY,) * ns,
        scratch_shapes=[pltpu.VMEM(win_p.shape, BF16)] + (_scatter_sems(ns) if ns else []),
        compiler_params=_params("arbitrary"),
    )(dz, x, dx1, g1, mod3, win_p, *scatter)
    return res[0], res[1], res[2], res[3], res[4:]


def _tile_of(n, choices):
    for c in choices:
        if n % c == 0:
            return c
    return n


def _tn_matmul(a, b, name, col_shards=0):
    t, k = a.shape
    n = b.shape[1]
    tk = _tile_of(k, (1024, 512, 256, 128))
    tn = n // col_shards if col_shards else _tile_of(n, (1024, 896, 768, 512, 384, 256, 128))
    tt = _tile_of(t, (4096, 2048, 1024, 512, 256))

    def body(a_ref, b_ref, o_ref):
        _acc(o_ref, _dot_tn(a_ref[...], b_ref[...]), pl.program_id(2) == 0)

    if col_shards:
        out_shape, out_spec = _sds((col_shards, k, tn), F32), pl.BlockSpec((None, tk, tn), lambda i, j, s: (j, i, 0))
    else:
        out_shape, out_spec = _sds((k, n), F32), pl.BlockSpec((tk, tn), lambda i, j, s: (i, j))
    return pl.pallas_call(
        body, name=name, grid=(k // tk, n // tn, t // tt), out_shape=out_shape,
        in_specs=[pl.BlockSpec((tt, tk), lambda i, j, s: (s, i)), pl.BlockSpec((tt, tn), lambda i, j, s: (s, j))],
        out_specs=out_spec, compiler_params=_params("arbitrary", "arbitrary", "arbitrary"),
    )(a, b)


N_SHARD = 4
COL_SHARDED = ("w_in", "w_uq", "w_ukv", "w_o_mla", "w_pw_out", "w_ff1")
ROW_SHARDED = ("w_out", "w_ff2")
BIG = ("w_in", "w_uq", "w_ukv", "w_o_mla", "w_pw_out", "w_out", "w_ff1", "w_ff2")
SMALL = ("norm1_g", "q_latent_g", "kv_latent_g", "qk_norm_q_g", "qk_norm_k_g", "conv_b", "conv_ln_g", "conv_ln_b",
         "norm2_g")
WEIGHTS = ("w_ada", "b_ada", "norm1_g", "w_in", "q_latent_g", "w_uq", "kv_latent_g", "w_ukv", "qk_norm_q_g",
           "qk_norm_k_g", "w_o_mla", "conv_w", "conv_b", "conv_ln_g", "conv_ln_b", "w_pw_out", "w_out", "norm2_g",
           "w_ff1", "w_ff2")


def _pad_heads(w, width):
    k = w.shape[0]
    w3 = w.reshape(k, N_HEADS, width)
    return jnp.pad(w3, ((0, 0), (0, 0), (0, LANES - width))).reshape(k, HW)


def _unpad_heads(g, width):
    k = g.shape[0]
    return g.reshape(k, N_HEADS, LANES)[:, :, :width].reshape(k, N_HEADS * width)


def _pad_win(w):
    d = w.shape[0]
    z = lambda n: jnp.zeros((d, n), w.dtype)
    return jnp.concatenate([w[:, OFF_GLU:], w[:, OFF_KR:OFF_GLU], w[:, :OFF_KV], z(KR_LANE), w[:, OFF_KV:OFF_KR],
                            z(LANES - KR_LANE - QK_ROPE)], axis=1)


def _unpad_win(g):
    d = g.shape[0]
    p_glu, p_q, _ = _layout(d)
    kr = p_q + OFF_KV + KR_LANE
    return jnp.concatenate([g[:, p_q:p_q + OFF_KV], g[:, kr:kr + QK_ROPE], g[:, p_glu:p_q], g[:, :p_glu]], axis=1)


def _col_shards(g):
    k, n = g.shape
    return g.reshape(k, N_SHARD, n // N_SHARD).transpose(1, 0, 2)


def _from_shards(g, name):
    ns, ks, nn = g.shape
    if name in ROW_SHARDED:
        return g.reshape(ns * ks, nn)
    return g.transpose(1, 0, 2).reshape(ks, ns * nn)


EARLY = ("w_in", "w_uq", "w_ukv")
LATE = ("w_o_mla", "w_pw_out", "w_out", "w_ff1", "w_ff2")


def _assemble(names, gathered):
    return {n: _from_shards(g.reshape((N_SHARD, 2 * g.shape[1]) + g.shape[2:]), n) for n, g in zip(names, gathered)}


LARGE = ("w_in", "w_ff1", "w_ff2")
GROUP_A = ("w_out", "w_ff1", "w_ff2")
GROUP_B = ("w_in", "w_uq", "w_ukv", "w_o_mla", "w_pw_out")


def _pair_halves(g):
    return g.reshape(N_SHARD, 2, g.shape[1] // 2, g.shape[2])


def _pair_sums(names, halves, from_sibling):
    if not halves:
        return []
    cidx = lax.axis_index("c").reshape(1).astype(jnp.int32)
    out = {n: _add_pair(g, l, cidx, "pair_sum_" + n)
           for n, g, l in zip(names, halves, from_sibling) if n in LARGE}
    small = [j for j, n in enumerate(names) if n not in LARGE]
    if small:
        res = _add_pair_whole([halves[j] for j in small], [from_sibling[j] for j in small], cidx,
                              "pair_sum_small_" + names[small[0]])
        out.update({names[j]: r for j, r in zip(small, res)})
    return [out[n] for n in names]


def _local_step(x, target, mod, sp, w, late=None, tm=256):
    comm = late is not None
    w = dict(w)
    nseq, seq, d = x.shape
    t = nseq * seq
    tps = seq // tm
    xf = x.reshape(t, d)
    tg = target.reshape(t, d)
    mod3 = mod.reshape(nseq, N_MOD, d)

    win_p = _pad_win(w["w_in"])
    wuq_p = _pad_heads(w["w_uq"], QK_HEAD)
    wkv3 = w["w_ukv"].reshape(KV_RANK, N_HEADS, QK_NOPE + V_HEAD)
    wk_p = _pad_heads(wkv3[:, :, :QK_NOPE].reshape(KV_RANK, -1), QK_NOPE)
    wv_p = _pad_heads(wkv3[:, :, QK_NOPE:].reshape(KV_RANK, -1), V_HEAD)
    cw = jnp.pad(w["conv_w"], ((0, HALO - CONV_W), (0, 0)))
    pad_g = lambda g: jnp.pad(g, ((0, 0), (0, LANES - QK_HEAD)))
    gq, gk = pad_g(sp["qk_norm_q_g"]), pad_g(sp["qk_norm_k_g"])
    tabs = _rope_tables(seq)

    tm_in, tps_in = (2 * tm, tps // 2) if tps % 2 == 0 else (tm, tps)
    h, zm, zglu, zgate, u0 = _fwd_in(xf, sp["norm1_g"], mod3, win_p, tm_in, tps_in)
    q, k, v, qln, kvn = _mla_prep(zm, sp["q_latent_g"], sp["kv_latent_g"], gq, gk, tabs, wuq_p, wk_p, wv_p, tm, tps)
    attn, gathered = _attn_fwd(q, k, v, nseq, seq, tuple(late) if comm else ())
    if comm:
        w.update(_assemble(LATE, gathered))
    wo_p = jnp.pad(w["w_o_mla"].reshape(N_HEADS, V_HEAD, d), ((0, 0), (0, LANES - V_HEAD), (0, 0))).reshape(HW, d)
    x1, mixed, mpre, ya, yb, u1, u3 = _fwd_mix(attn, u0, zgate, xf, mod3, wo_p, cw, sp["conv_b"], sp["conv_ln_g"],
                                               sp["conv_ln_b"], w["w_pw_out"], w["w_out"], tm, tps)
    h2, a, r, dy, df, dgate2, loss_acc = _fwd_ffn(x1, tg, sp["norm2_g"], mod3, w["w_ff1"], w["w_ff2"], tm, tps)
    da, dx1, dmixed, dshift2, dscale2, dgate1, dg2 = _bwd_ffn(df, a, x1, dy, mixed, sp["norm2_g"], mod3,
                                                              w["w_ff2"], w["w_ff1"], tm, tps)
    gw = {
        "w_out": _tn_matmul(mpre, dmixed, "dw_out").reshape(N_SHARD, d // N_SHARD, d),
        "w_ff1": _tn_matmul(h2, da, "dw_ff1", N_SHARD),
        "w_ff2": _tn_matmul(r, df, "dw_ff2").reshape(N_SHARD, -1, d),
    }
    halves_a = [_pair_halves(gw[n]) for n in GROUP_A] if comm else []
    dya, dyb, dz, do, du1, dlng, dlnb, dcb, from_sibling = _bwd_mix(
        dmixed, zgate, ya, yb, u1, sp["conv_ln_g"], sp["conv_ln_b"], w["w_out"], wo_p, w["w_pw_out"], tm, tuple(halves_a))
    pair_a = _pair_sums(GROUP_A, halves_a, from_sibling)
    dz, dcw = _bwd_conv(dz, du1, u0, zglu, cw, tm, tps)
    gw["conv_w"] = dcw
    dq, dk, dv, land_a = _attn_bwd(q, k, v, do, nseq, seq, tuple(p[1] for p in pair_a))
    dz, dqpre, dkh, dgq, dgk, dgql, dgkvl = _mla_bwd(dz, dq, dk, dv, zm, sp["q_latent_g"], sp["kv_latent_g"], gq, gk,
                                                      tabs, wuq_p, wk_p, wv_p, tm, tps)
    dwk_p = _tn_matmul(kvn, dkh, "dw_uk")
    dwv_p = _tn_matmul(kvn, dv, "dw_uv")
    dwkv = jnp.concatenate([dwk_p.reshape(KV_RANK, N_HEADS, LANES)[:, :, :QK_NOPE],
                            dwv_p.reshape(KV_RANK, N_HEADS, LANES)[:, :, :V_HEAD]], axis=2).reshape(KV_RANK, -1)
    dwo = _tn_matmul(attn, dya, "dw_o").reshape(N_HEADS, LANES, d)[:, :V_HEAD].reshape(MLA_WIDTH, d)
    gw["w_in"] = _col_shards(_unpad_win(_tn_matmul(h, dz, "dw_in")))
    gw["w_uq"] = _col_shards(_unpad_heads(_tn_matmul(qln, dqpre, "dw_uq"), QK_HEAD))
    gw["w_ukv"] = _col_shards(dwkv)
    gw["w_o_mla"] = _col_shards(dwo)
    gw["w_pw_out"] = _tn_matmul(u3, dyb, "dw_pw", N_SHARD)
    pair_b = []
    if comm:
        halves_b = [_pair_halves(gw[n]) for n in GROUP_B]
        pair_b = _pair_sums(GROUP_B, halves_b, _pair_swap(halves_b, "grad_pair_swap"))
    gx, dshift1, dscale1, dg1, land_b = _bwd_in(dz, xf, dx1, sp["norm1_g"], mod3, win_p, tm_in, tps_in,
                                                tuple(p[1] for p in pair_b))
    if comm:
        for n, p, l in zip(GROUP_A + GROUP_B, pair_a + pair_b, land_a + land_b):
            gw[n] = (p[0], l)
    gs = {
        "norm1_g": dg1, "q_latent_g": dgql, "kv_latent_g": dgkvl, "qk_norm_q_g": dgq, "qk_norm_k_g": dgk,
        "conv_b": dcb, "conv_ln_g": dlng, "conv_ln_b": dlnb, "norm2_g": dg2,
    }
    dmod = jnp.concatenate([dshift1, dscale1, dgate1, dshift2, dscale2, dgate2], axis=2).reshape(nseq, N_MOD * d)
    return loss_acc, gx.reshape(nseq, seq, d), dmod, gw, gs


def kernel(x, c, w_ada, b_ada, norm1_g, w_in, q_latent_g, w_uq, kv_latent_g, w_ukv, qk_norm_q_g, qk_norm_k_g, w_o_mla, conv_w, conv_b, conv_ln_g, conv_ln_b, w_pw_out, w_out, norm2_g, w_ff1, w_ff2, loss_target, m_w_ada, m_b_ada, m_norm1_g, m_w_in, m_q_latent_g, m_w_uq, m_kv_latent_g, m_w_ukv, m_qk_norm_q_g, m_qk_norm_k_g, m_w_o_mla, m_conv_w, m_conv_b, m_conv_ln_g, m_conv_ln_b, m_w_pw_out, m_w_out, m_norm2_g, m_w_ff1, m_w_ff2, v_w_ada, v_b_ada, v_norm1_g, v_w_in, v_q_latent_g, v_w_uq, v_kv_latent_g, v_w_ukv, v_qk_norm_q_g, v_qk_norm_k_g, v_w_o_mla, v_conv_w, v_conv_b, v_conv_ln_g, v_conv_ln_b, v_w_pw_out, v_w_out, v_norm2_g, v_w_ff1, v_w_ff2):
    given = dict(locals())
    wts = {n: given[n][0] for n in WEIGHTS}
    mom = {n: given["m_" + n][0] for n in WEIGHTS}
    var = {n: given["v_" + n][0] for n in WEIGHTS}
    vec = lambda a: a.reshape(1, -1)
    nseq, seq, d = x.shape
    ix, iy, ic = _place()
    shard = 2 * ix + iy

    half = lambda n: lax.dynamic_slice_in_dim(wts[n].astype(BF16), ic * (wts[n].shape[0] // 2), wts[n].shape[0] // 2,
                                              axis=0)
    gathered = _all_gather8([half(n) for n in EARLY] + [wts["conv_w"], c], "gather_weights")
    full = _assemble(EARLY, gathered)
    full["conv_w"] = _from_shards(gathered[-2][0::2], "conv_w")
    c_all = gathered[-1].reshape(8 * nseq, d)

    n_ada = wts["w_ada"].shape[1]
    b_sh = lax.dynamic_slice_in_dim(vec(wts["b_ada"]), shard * n_ada, n_ada, axis=1)
    mod_sh = _ada_mod(c_all, wts["w_ada"], b_sh)
    hb = 4 * nseq
    mod_blk = lax.dynamic_slice_in_dim(mod_sh, ic * hb, hb, axis=0)
    (mod_all,) = _all_gather8([mod_blk], "gather_mod")
    mod_mine = lax.dynamic_slice_in_dim(mod_all, (2 * iy + ic) * nseq, nseq, axis=1)
    mod = jnp.concatenate([lax.dynamic_index_in_dim(mod_mine, 2 * s + ix, axis=0, keepdims=False)
                           for s in range(N_SHARD)], axis=1)

    sp = {n: vec(wts[n]) for n in SMALL}
    loss_part, grad_x, dmod, gw, gs = _local_step(x, loss_target, mod, sp, full, [half(n) for n in LATE])

    parts = _all_gather8([dmod, gw["conv_w"], loss_part] + [gs[n] for n in SMALL], "gather_small")
    dmod_all = parts[0].reshape(8 * nseq, N_MOD * d)
    dmod_sh = lax.dynamic_slice_in_dim(dmod_all, shard * n_ada, n_ada, axis=1)
    res = _ada_bwd(c_all, dmod_all, dmod_sh, parts[1:])
    grads = {"w_ada": res[0], "b_ada": res[1]}
    n_cw = wts["conv_w"].shape[1]
    grads["conv_w"] = lax.dynamic_slice_in_dim(res[2], shard * n_cw, n_cw, axis=1)[:CONV_W]
    loss = res[3][0, 0]
    for n, g in zip(SMALL, res[4:]):
        grads[n] = g

    own_c = jnp.stack([shard, ic]).astype(jnp.int32)
    mine_sum = {n: _add_chips(gw[n][0], gw[n][1], own_c, "chip_sum_" + n) for n in LARGE}
    few = tuple(n for n in BIG if n not in LARGE)
    mine_sum.update(zip(few, _add_chips_whole([gw[n][0] for n in few], [gw[n][1] for n in few], own_c, "chip_sum_small")))
    for n, g in zip(BIG, _pair_gather([mine_sum[n] for n in BIG], "grad_pair_gather")):
        grads[n] = g.reshape(wts[n].shape)

    delta, new_m, new_v = {}, {}, {}
    for n in LARGE + ("w_ada",):
        delta[n], new_m[n], new_v[n] = _adamw(wts[n], grads[n], mom[n], var[n], "adamw_" + n)
    rest = ("b_ada", "conv_w") + SMALL + few
    as2d = lambda a: a if a.ndim == 2 else vec(a)
    res = _adamw_small(*[[as2d(t[n]) for n in rest] for t in (wts, grads, mom, var)])
    for dst, arrs in zip((delta, new_m, new_v), res):
        for n, a in zip(rest, arrs):
            dst[n] = a

    outs = [loss, grad_x]
    for group in (grads, delta, new_m, new_v):
        outs += [group[n].reshape(given[n].shape) for n in WEIGHTS]
    return tuple(outs)
```

```python
import jax
import jax.numpy as jnp
from jax import lax
from jax.experimental import pallas as pl
from jax.experimental.pallas import tpu as pltpu

F32 = jnp.float32
BF16 = jnp.bfloat16
MESH = pl.DeviceIdType.MESH
ANY = pl.BlockSpec(memory_space=pl.ANY)

CHUNK = 64
CHUNK_SHIFT = 6
N_HEADS = 8
QK_NOPE = 64
QK_ROPE = 32
QK_HEAD = QK_NOPE + QK_ROPE
V_HEAD = 64
Q_RANK = 256
KV_RANK = 128
MLA_WIDTH = N_HEADS * V_HEAD
CONV_CH = 512
CONV_W = 31
ROPE_THETA = 10000.0
EPS = 1e-6
LANES = 128
SUBLANES = 8
HW = N_HEADS * LANES
OFF_KV = Q_RANK + KV_RANK
OFF_KR = OFF_KV + QK_ROPE
OFF_GLU = OFF_KR + 2 * CONV_CH
KR_LANE = QK_NOPE
MLA_IN = Q_RANK + KV_RANK + LANES
HALO = 32
N_MOD = 6

ADAM_LR = 0.001
ADAM_B1 = 0.9
ADAM_B2 = 0.999
ADAM_EPS = 1e-08
ADAM_WD = 0.01
ADAM_STEP = 10

VMEM_LIMIT = 56 * 1024 * 1024
BQ = 256


def _layout(d):
    p_glu = 2 * d
    p_q = p_glu + 2 * CONV_CH
    return p_glu, p_q, p_q + MLA_IN


def _params(*sem):
    return pltpu.CompilerParams(dimension_semantics=sem, vmem_limit_bytes=VMEM_LIMIT)


def _dot(a, b):
    return jnp.dot(a, b, preferred_element_type=F32)


def _dot_tn(a, b):
    return lax.dot_general(a, b, (((0,), (0,)), ((), ())), preferred_element_type=F32)


def _dot_nt(a, b):
    return lax.dot_general(a, b, (((1,), (1,)), ((), ())), preferred_element_type=F32)


def _acc(ref, val, first):
    @pl.when(first)
    def _():
        ref[...] = val

    @pl.when(jnp.logical_not(first))
    def _():
        ref[...] += val


def _rms(x):
    r = lax.rsqrt(jnp.mean(x * x, axis=-1, keepdims=True) + EPS)
    return x * r, r


def _rms_bwd(n, r, dn):
    return r * (dn - n * jnp.mean(dn * n, axis=-1, keepdims=True))


def _head_rms(sl):
    r = lax.rsqrt(jnp.sum(sl * sl, axis=-1, keepdims=True) * (1.0 / QK_HEAD) + EPS)
    return sl * r, r


def _head_rms_bwd(n, r, dn):
    return r * (dn - n * (jnp.sum(dn * n, axis=-1, keepdims=True) * (1.0 / QK_HEAD)))


def _rope(x, c, s1, s2):
    return x * c + pltpu.roll(x, QK_ROPE // 2, 1) * s1 + pltpu.roll(x, LANES - QK_ROPE // 2, 1) * s2


def _rope_t(dy, c, s1, s2):
    return dy * c + pltpu.roll(dy * s1, LANES - QK_ROPE // 2, 1) + pltpu.roll(dy * s2, QK_ROPE // 2, 1)


def _rope_tables(seq):
    half = QK_ROPE // 2
    inv_freq = ROPE_THETA ** (-jnp.arange(0, QK_ROPE, 2, dtype=F32) / QK_ROPE)
    ang = jnp.arange(seq, dtype=F32)[:, None] * inv_freq[None, :]
    cos, sin = jnp.cos(ang), jnp.sin(ang)
    z = lambda n: jnp.zeros((seq, n), F32)
    tail = LANES - QK_HEAD
    c = jnp.concatenate([jnp.ones((seq, QK_NOPE), F32), cos, cos, jnp.ones((seq, tail), F32)], axis=1)
    s1 = jnp.concatenate([z(QK_NOPE + half), sin, z(tail)], axis=1)
    s2 = jnp.concatenate([z(QK_NOPE), -sin, z(half + tail)], axis=1)
    return c, s1, s2


def _row(tm, w):
    return pl.BlockSpec((tm, w), lambda i: (i, 0))


def _modspec(d, tps):
    return pl.BlockSpec((None, N_MOD, d), lambda i: (i // tps, 0, 0))


def _seqv(w, tps):
    return pl.BlockSpec((None, 1, w), lambda i: (i // tps, 0, 0))


def _full(shape):
    return pl.BlockSpec(shape, lambda i: tuple(0 for _ in shape))


def _sds(shape, dtype):
    return jax.ShapeDtypeStruct(shape, dtype)


CONV_ROWS = 64
CONV_LC = CONV_CH // LANES


def _lane_chunks():
    return [(lc, slice(lc * LANES, (lc + 1) * LANES)) for lc in range(CONV_LC)]


def _fill_shifted(ext_ref, head, body):
    nh = head.shape[0]
    for lc, ls in _lane_chunks():
        ext_ref[0, lc, :nh, :] = head[:, ls]
        ext_ref[0, lc, nh:, :] = body[:, ls]
        rows = ext_ref[0, lc]
        for b in range(1, SUBLANES):
            ext_ref[b, lc] = pltpu.roll(rows, rows.shape[0] - b, 0)


def _shifted_shape(tm):
    return (SUBLANES, CONV_LC, tm + HALO, LANES)


def _conv_chunk(c):
    return c % CONV_LC, pl.multiple_of((c // CONV_LC) * CONV_ROWS, CONV_ROWS)


def _shifted(ext_ref, o, lc, r0):
    a = pl.multiple_of((o // SUBLANES) * SUBLANES + r0, SUBLANES)
    return ext_ref[o % SUBLANES, lc, pl.ds(a, CONV_ROWS), :]


def _by_lane_chunk(a):
    return a.reshape(a.shape[0], CONV_LC, LANES).transpose(1, 0, 2)


def _load_resident(i, pairs):
    @pl.when(i == 0)
    def _():
        for src, dst in pairs:
            pltpu.sync_copy(src, dst)


def _place():
    return lax.axis_index("x"), lax.axis_index("y"), lax.axis_index("c")


def _all_gather8(blocks, name):
    na = len(blocks)

    def body(*refs):
        start, forward, finish = _gather8_phases(refs[:na], refs[na:2 * na], *refs[2 * na:])
        start()
        forward()
        finish()

    outs = pl.pallas_call(
        body, name=name, out_shape=_gather8_shapes(blocks), in_specs=[ANY] * na, out_specs=(ANY,) * na,
        scratch_shapes=_gather8_sems(na),
    )(*blocks)
    return _own_block_placed(outs, blocks)


def _gather8_shapes(blocks):
    return tuple(_sds((8,) + b.shape, b.dtype) for b in blocks)


def _gather8_sems(na):
    return [pltpu.SemaphoreType.DMA((7 * na,)), pltpu.SemaphoreType.DMA((7 * na,))]


def _own_block_placed(outs, blocks):
    ix, iy, ic = _place()
    return tuple(lax.dynamic_update_index_in_dim(o, b, 4 * ix + 2 * iy + ic, 0) for o, b in zip(outs, blocks))


def _gather8_phases(x_refs, out_refs, send_sems, recv_sems):
    na = len(x_refs)
    x, y, c = _place()
    me, sibling = (x, y, c), (x, y, 1 - c)
    chips = [(1 - x, y), (x, 1 - y), (1 - x, 1 - y)]

    def copy(a, k, blk, to, from_input=False):
        dst = out_refs[a].at[4 * blk[0] + 2 * blk[1] + blk[2]]
        return pltpu.make_async_remote_copy(
            src_ref=x_refs[a] if from_input else dst, dst_ref=dst,
            send_sem=send_sems.at[7 * a + k], recv_sem=recv_sems.at[7 * a + k], device_id=to, device_id_type=MESH)

    def first(a):
        return [copy(a, 0, me, sibling, True)] + [copy(a, 1 + j, me, (*chip, c), True) for j, chip in enumerate(chips)]

    def start():
        for a in range(na):
            for cp in first(a):
                cp.start()

    def forward():
        for j, chip in enumerate(chips):
            for a in range(na):
                copy(a, 1 + j, (*chip, c), me).wait_recv()
                copy(a, 4 + j, (*chip, c), sibling).start()

    def finish():
        for a in range(na):
            copy(a, 0, sibling, me).wait_recv()
            for j, chip in enumerate(chips):
                copy(a, 4 + j, (*chip, 1 - c), me).wait_recv()
        for a in range(na):
            for cp in first(a) + [copy(a, 4 + j, (*chip, c), sibling) for j, chip in enumerate(chips)]:
                cp.wait_send()

    return start, forward, finish


def _pair_swap(gs, name):
    na = len(gs)

    def body(*refs):
        start, finish = _swap_phases(refs[:na], refs[na:2 * na], *refs[2 * na:])
        start()
        finish()

    return pl.pallas_call(
        body, name=name, out_shape=_swap_shapes(gs), in_specs=[ANY] * na, out_specs=(ANY,) * na,
        scratch_shapes=_swap_sems(gs),
    )(*gs)


def _swap_shapes(gs):
    return tuple(_sds(g.shape[:1] + g.shape[2:], g.dtype) for g in gs)


def _swap_sems(gs):
    n = sum(g.shape[0] for g in gs)
    return [pltpu.SemaphoreType.DMA((n,)), pltpu.SemaphoreType.DMA((n,))]


def _swap_phases(g_refs, land_refs, send_sems, recv_sems):
    x, y, c = _place()

    def copies():
        cps, k = [], 0
        for g_ref, land_ref in zip(g_refs, land_refs):
            for s in range(g_ref.shape[0]):
                cps.append(pltpu.make_async_remote_copy(
                    src_ref=g_ref.at[s, 1 - c], dst_ref=land_ref.at[s], send_sem=send_sems.at[k],
                    recv_sem=recv_sems.at[k], device_id=(x, y, 1 - c), device_id_type=MESH))
                k += 1
        return cps

    def start():
        for cp in copies():
            cp.start()

    def finish():
        for cp in copies():
            cp.wait()

    return start, finish


def _scatter_shapes(hs):
    return tuple(_sds((3,) + h.shape[1:], h.dtype) for h in hs)


def _scatter_sems(na):
    return [pltpu.SemaphoreType.DMA((3 * na,)), pltpu.SemaphoreType.DMA((3 * na,))]


def _scatter_phases(h_refs, land_refs, send_sems, recv_sems):
    x, y, c = _place()
    chips = [(1 - x, y), (x, 1 - y), (1 - x, 1 - y)]

    def copies():
        return [pltpu.make_async_remote_copy(
            src_ref=h_refs[a].at[2 * tx + ty], dst_ref=land_refs[a].at[j], send_sem=send_sems.at[3 * a + j],
            recv_sem=recv_sems.at[3 * a + j], device_id=(tx, ty, c), device_id_type=MESH)
            for a in range(len(h_refs)) for j, (tx, ty) in enumerate(chips)]

    def start():
        for cp in copies():
            cp.start()

    def finish():
        for cp in copies():
            cp.wait()

    return start, finish


def _pair_gather_and_all_gather8(fs, blocks, name):
    nf, nb = len(fs), len(blocks)

    def body(*refs):
        f_refs = refs[nf + nb:2 * nf + nb]
        b_out = refs[2 * nf + nb:2 * nf + 2 * nb]
        send_sems, recv_sems, g_send, g_recv = refs[2 * nf + 2 * nb:]
        x, y, c = _place()
        start, forward, finish = _gather8_phases(refs[nf:nf + nb], b_out, g_send, g_recv)
        sends = [pltpu.make_async_remote_copy(
            src_ref=f_refs[a].at[c], dst_ref=f_refs[a].at[c], send_sem=send_sems.at[a], recv_sem=recv_sems.at[a],
            device_id=(x, y, 1 - c), device_id_type=MESH) for a in range(nf)]
        recvs = [pltpu.make_async_remote_copy(
            src_ref=f_refs[a].at[c], dst_ref=f_refs[a].at[1 - c], send_sem=send_sems.at[a],
            recv_sem=recv_sems.at[a], device_id=(x, y, 1 - c), device_id_type=MESH) for a in range(nf)]
        start()
        for cp in sends:
            cp.start()
        forward()
        finish()
        for cp in recvs:
            cp.wait_recv()
        for cp in sends:
            cp.wait_send()

    res = pl.pallas_call(
        body, name=name, out_shape=tuple(_sds(f.shape, f.dtype) for f in fs) + _gather8_shapes(blocks),
        in_specs=[ANY] * (nf + nb), out_specs=(ANY,) * (nf + nb), input_output_aliases={a: a for a in range(nf)},
        scratch_shapes=[pltpu.SemaphoreType.DMA((nf,)), pltpu.SemaphoreType.DMA((nf,))] + _gather8_sems(nb),
    )(*fs, *blocks)
    return res[:nf], _own_block_placed(res[nf:], blocks)


def _row_tile(r, n, itemsize=4, budget=1 << 21):
    if r * n * itemsize <= budget:
        return r
    best = None
    for tr in range(16, r, 16):
        if r % tr == 0 and tr * n * itemsize <= budget:
            best = tr
    assert best is not None, (r, n)
    return best


def _add_pair(g, land, cidx, name):
    ns, _, r, n = g.shape
    tr = _row_tile(r, n)

    def body(c_ref, a_ref, b_ref, o_ref, ob_ref):
        s = a_ref[...] + b_ref[...]
        o_ref[...] = s
        ob_ref[...] = s.astype(BF16)

    out = pl.BlockSpec((None, tr, n), lambda s, i, cr: (s, i, 0))
    return pl.pallas_call(
        body, name=name, out_shape=(_sds((ns, r, n), F32), _sds((ns, r, n), BF16)),
        grid_spec=pltpu.PrefetchScalarGridSpec(
            num_scalar_prefetch=1, grid=(ns, r // tr),
            in_specs=[pl.BlockSpec((None, None, tr, n), lambda s, i, cr: (s, cr[0], i, 0)), out],
            out_specs=(out, out)),
        compiler_params=_params("arbitrary", "arbitrary"),
    )(cidx, g, land)


def _add_pair_whole(gs, lands, cidx, name):
    k = len(gs)

    def body(c_ref, *refs):
        for a_ref, b_ref, o_ref, ob_ref in zip(refs[:k], refs[k:2 * k], refs[2 * k:3 * k], refs[3 * k:]):
            s = a_ref[...] + b_ref[...]
            o_ref[...] = s
            ob_ref[...] = s.astype(BF16)

    half = lambda g: pl.BlockSpec((g.shape[0], None) + g.shape[2:], lambda i, cr: (0, cr[0], 0, 0))
    whole = lambda g: pl.BlockSpec(g.shape[:1] + g.shape[2:], lambda i, cr: (0, 0, 0))
    shapes = lambda dt: tuple(_sds(g.shape[:1] + g.shape[2:], dt) for g in gs)
    res = pl.pallas_call(
        body, name=name, out_shape=shapes(F32) + shapes(BF16),
        grid_spec=pltpu.PrefetchScalarGridSpec(
            num_scalar_prefetch=1, grid=(1,),
            in_specs=[half(g) for g in gs] + [whole(g) for g in gs],
            out_specs=tuple(whole(g) for g in gs) * 2),
        compiler_params=_params("arbitrary"),
    )(cidx, *gs, *lands)
    return list(zip(res[:k], res[k:]))


def _add_chips_whole(hs, lands, own_c, name):
    k = len(hs)

    def body(o_idx, *refs):
        for h_ref, l_ref, o_ref in zip(refs[:k], refs[k:2 * k], refs[2 * k:]):
            o_ref[...] = ((h_ref[...] + l_ref[0].astype(F32)) + l_ref[1].astype(F32)) + l_ref[2].astype(F32)

    return pl.pallas_call(
        body, name=name, out_shape=tuple(_sds((2,) + h.shape[1:], F32) for h in hs),
        grid_spec=pltpu.PrefetchScalarGridSpec(
            num_scalar_prefetch=1, grid=(1,),
            in_specs=[pl.BlockSpec((None,) + h.shape[1:], lambda i, o: (o[0], 0, 0)) for h in hs]
            + [pl.BlockSpec(l.shape, lambda i, o: (0, 0, 0)) for l in lands],
            out_specs=tuple(pl.BlockSpec((None,) + h.shape[1:], lambda i, o: (o[1], 0, 0)) for h in hs)),
        compiler_params=_params("arbitrary"),
    )(own_c, *hs, *lands)


def _add_chips(h, land, own_c, name):
    _, r, n = h.shape
    tr = _row_tile(r, n)

    def body(o_idx, h_ref, l_ref, o_ref):
        o_ref[...] = ((h_ref[...] + l_ref[0].astype(F32)) + l_ref[1].astype(F32)) + l_ref[2].astype(F32)

    return pl.pallas_call(
        body, name=name, out_shape=_sds((2, r, n), F32),
        grid_spec=pltpu.PrefetchScalarGridSpec(
            num_scalar_prefetch=1, grid=(r // tr,),
            in_specs=[pl.BlockSpec((None, tr, n), lambda i, o: (o[0], i, 0)),
                      pl.BlockSpec((3, tr, n), lambda i, o: (0, i, 0))],
            out_specs=pl.BlockSpec((None, tr, n), lambda i, o: (o[1], i, 0))),
        compiler_params=_params("arbitrary"),
    )(own_c, h, land)


def _adam_math(w, g, m, v):
    nm = ADAM_B1 * m + (1.0 - ADAM_B1) * g
    nv = ADAM_B2 * v + (1.0 - ADAM_B2) * (g * g)
    m_hat = nm / (1.0 - ADAM_B1 ** ADAM_STEP)
    v_hat = nv / (1.0 - ADAM_B2 ** ADAM_STEP)
    return -ADAM_LR * (m_hat / (jnp.sqrt(v_hat) + ADAM_EPS) + ADAM_WD * w), nm, nv


def _adamw(w, g, m, v, name):
    r, n = w.shape
    tr = _row_tile(r, n)

    def body(w_ref, g_ref, m_ref, v_ref, d_ref, nm_ref, nv_ref):
        d_ref[...], nm_ref[...], nv_ref[...] = _adam_math(w_ref[...], g_ref[...], m_ref[...], v_ref[...])

    spec = pl.BlockSpec((tr, n), lambda i: (i, 0))
    return pl.pallas_call(
        body, name=name, out_shape=(_sds((r, n), F32),) * 3, grid=(r // tr,),
        in_specs=[spec] * 4, out_specs=(spec,) * 3, compiler_params=_params("arbitrary"),
    )(w, g, m, v)


def _adamw_small(ws, gs, ms, vs):
    k = len(ws)

    def body(*refs):
        ins, outs = refs[:4 * k], refs[4 * k:]
        for j in range(k):
            d, nm, nv = _adam_math(ins[j][...], ins[k + j][...], ins[2 * k + j][...], ins[3 * k + j][...])
            outs[j][...] = d
            outs[k + j][...] = nm
            outs[2 * k + j][...] = nv

    shapes = tuple(_sds(w.shape, F32) for w in ws)
    res = pl.pallas_call(body, name="adamw_small", out_shape=shapes * 3,
                         compiler_params=pltpu.CompilerParams(vmem_limit_bytes=VMEM_LIMIT))(*ws, *gs, *ms, *vs)
    return res[:k], res[k:2 * k], res[2 * k:]


def _ada_mod(c_all, w_sh, b_sh):
    b, _ = c_all.shape
    n = w_sh.shape[1]

    def body(c_ref, w_ref, b_ref, o_ref):
        cc = c_ref[...]
        ca = (cc * jax.nn.sigmoid(cc)).astype(BF16)
        o_ref[...] = _dot(ca, w_ref[...].astype(BF16)) + b_ref[...]

    return pl.pallas_call(body, name="ada_mod", out_shape=_sds((b, n), F32),
                          compiler_params=pltpu.CompilerParams(vmem_limit_bytes=VMEM_LIMIT))(c_all, w_sh, b_sh)


def _ada_bwd(c_all, dmod_all, dmod_sh, parts):
    b, d = c_all.shape
    n6 = dmod_all.shape[1]
    n = dmod_sh.shape[1]
    k = len(parts)

    def body(*refs):
        c_ref, da_ref, ds_ref = refs[:3]
        p_refs = refs[3:3 + k]
        dw_ref, db_ref = refs[3 + k:5 + k]
        s_refs = refs[5 + k:]
        cc = c_ref[...]
        ca = (cc * jax.nn.sigmoid(cc)).astype(BF16)
        dw_ref[...] = _dot_tn(ca, ds_ref[...].astype(BF16))
        db_ref[...] = jnp.sum(da_ref[...], axis=0, keepdims=True)
        for p_ref, s_ref in zip(p_refs, s_refs):
            tot = p_ref[0]
            for j in range(1, p_ref.shape[0]):
                tot = tot + p_ref[j]
            s_ref[...] = tot

    return pl.pallas_call(
        body, name="ada_bwd",
        out_shape=(_sds((d, n), F32), _sds((1, n6), F32)) + tuple(_sds(p.shape[1:], F32) for p in parts),
        compiler_params=pltpu.CompilerParams(vmem_limit_bytes=VMEM_LIMIT),
    )(c_all, dmod_all, dmod_sh, *parts)


def _fwd_in(x, g1, mod3, win_p, tm, tps):
    t, d = x.shape
    p_glu, p_q, npad = _layout(d)

    def body(x_ref, g_ref, mod_ref, w_hbm, h_ref, zm_ref, zglu_ref, zgate_ref, u0_ref, w_ref):
        _load_resident(pl.program_id(0), [(w_hbm, w_ref)])
        n, _ = _rms(x_ref[...])
        h = ((n * g_ref[...]) * (1.0 + mod_ref[1:2, :]) + mod_ref[0:1, :]).astype(BF16)
        h_ref[...] = h
        z = _dot(h, w_ref[...])
        zgate_ref[...] = z[:, :p_glu]
        zglu = z[:, p_glu:p_q]
        zglu_ref[...] = zglu
        zm_ref[...] = z[:, p_q:]
        u0_ref[...] = zglu[:, :CONV_CH] * jax.nn.sigmoid(zglu[:, CONV_CH:])

    return pl.pallas_call(
        body, name="fwd_in", grid=(t // tm,),
        out_shape=(_sds((t, d), BF16), _sds((t, MLA_IN), F32), _sds((t, 2 * CONV_CH), F32), _sds((t, 2 * d), F32),
                   _sds((t, CONV_CH), F32)),
        in_specs=[_row(tm, d), _full((1, d)), _modspec(d, tps), ANY],
        out_specs=(_row(tm, d), _row(tm, MLA_IN), _row(tm, 2 * CONV_CH), _row(tm, 2 * d), _row(tm, CONV_CH)),
        scratch_shapes=[pltpu.VMEM(win_p.shape, BF16)],
        compiler_params=_params("arbitrary"),
    )(x, g1, mod3, win_p)


def _mla_prep(zm, gql, gkvl, gq, gk, tabs, wuq_p, wk_p, wv_p, tm, tps):
    t = zm.shape[0]
    c_t, s1_t, s2_t = tabs
    tab = pl.BlockSpec((tm, LANES), lambda i: (i % tps, 0))

    def body(zm_ref, gql_ref, gkvl_ref, gq_ref, gk_ref, c_ref, s1_ref, s2_ref, wuq_ref, wk_ref, wv_ref,
             q_ref, k_ref, v_ref, qln_ref, kvn_ref):
        c, s1, s2 = c_ref[...], s1_ref[...], s2_ref[...]
        nq, _ = _rms(zm_ref[:, :Q_RANK])
        qln = (nq * gql_ref[...]).astype(BF16)
        qln_ref[...] = qln
        qpre = _dot(qln, wuq_ref[...])
        nkv, _ = _rms(zm_ref[:, Q_RANK:OFF_KV])
        kvn = (nkv * gkvl_ref[...]).astype(BF16)
        kvn_ref[...] = kvn
        knope = _dot(kvn, wk_ref[...])
        v_ref[...] = _dot(kvn, wv_ref[...]).astype(BF16)
        zkr_v = zm_ref[:, OFF_KV:]
        kr_roped = _rope(zkr_v * gk_ref[...], c, s1, s2)
        slabs = [slice(hd * LANES, (hd + 1) * LANES) for hd in range(N_HEADS)]
        rq = [_head_rms(qpre[:, sl])[1] for sl in slabs]
        rk = [_head_rms(knope[:, sl] + zkr_v)[1] for sl in slabs]
        for hd, sl in enumerate(slabs):
            q_ref[:, sl] = _rope((qpre[:, sl] * rq[hd]) * gq_ref[...], c, s1, s2).astype(BF16)
            k_ref[:, sl] = (rk[hd] * (knope[:, sl] * gk_ref[...] + kr_roped)).astype(BF16)

    return pl.pallas_call(
        body, name="mla_prep", grid=(t // tm,),
        out_shape=(_sds((t, HW), BF16),) * 3 + (_sds((t, Q_RANK), BF16), _sds((t, KV_RANK), BF16)),
        in_specs=[_row(tm, MLA_IN), _full((1, Q_RANK)), _full((1, KV_RANK)),
                  _full((1, LANES)), _full((1, LANES)), tab, tab, tab,
                  _full(wuq_p.shape), _full(wk_p.shape), _full(wv_p.shape)],
        out_specs=(_row(tm, HW),) * 3 + (_row(tm, Q_RANK), _row(tm, KV_RANK)),
        compiler_params=_params("arbitrary"),
    )(zm, gql, gkvl, gq, gk, c_t, s1_t, s2_t, wuq_p, wk_p, wv_p)


AHEAD = 2
ROW_BAND = 256
SM_SCALE = QK_HEAD ** -0.5
EXP2_SCALE = SM_SCALE * 1.4426950408889634


def _diag_mask():
    rc = jnp.right_shift(lax.broadcasted_iota(jnp.int32, (BQ, 1), 0), CHUNK_SHIFT)
    cc = jnp.right_shift(lax.broadcasted_iota(jnp.int32, (1, BQ), 1), CHUNK_SHIFT)
    return rc >= cc


def _scores(q_i, k_ref, lo, e):
    return (_dot_nt(q_i, k_ref[:lo, :]) if lo else None), _dot_nt(q_i, k_ref[lo:e, :])


def _softmax_parts(scores, mask):
    sp, sd = scores
    sd = jnp.where(mask, sd, jnp.finfo(F32).min)
    m = jnp.max(sd, axis=-1, keepdims=True)
    if sp is not None:
        m = jnp.maximum(m, jnp.max(sp, axis=-1, keepdims=True))
    pd = jnp.exp2((sd - m) * EXP2_SCALE)
    l = jnp.sum(pd, axis=-1, keepdims=True)
    pp = None
    if sp is not None:
        pp = jnp.exp2((sp - m) * EXP2_SCALE)
        l = l + jnp.sum(pp, axis=-1, keepdims=True)
    return pp, pd, l


def _attn_fwd(q, k, v, nseq, seq, gather=()):
    t = q.shape[0]
    na = len(gather)
    blk = pl.BlockSpec((seq, LANES), lambda b, h: (b, h))
    n_steps = nseq * N_HEADS

    def body(q_ref, k_ref, v_ref, *rest):
        o_ref = rest[na]
        if na:
            start, forward, finish = _gather8_phases(rest[:na], rest[na + 1:2 * na + 1], *rest[2 * na + 1:])
            step = pl.program_id(0) * N_HEADS + pl.program_id(1)
            pl.when(step == 0)(start)
            pl.when(step == (7 * n_steps) // 8)(forward)
        mask = _diag_mask()
        nb = seq // BQ
        block_scores = lambda j: _scores(q_ref[j * BQ:(j + 1) * BQ, :], k_ref, j * BQ, (j + 1) * BQ)
        ahead = [block_scores(j) for j in range(min(AHEAD, nb))]
        for i in range(nb):
            lo, e = i * BQ, (i + 1) * BQ
            cur = ahead.pop(0)
            if i + AHEAD < nb:
                ahead.append(block_scores(i + AHEAD))
            pp, pd, l = _softmax_parts(cur, mask)
            o = _dot(pd.astype(BF16), v_ref[lo:e, :])
            if lo:
                o = o + _dot(pp.astype(BF16), v_ref[:lo, :])
            o_ref[lo:e, :] = (o * (1.0 / l)).astype(BF16)
        if na:
            pl.when(step == n_steps - 1)(finish)

    res = pl.pallas_call(
        body, name="attn_fwd", grid=(nseq, N_HEADS), out_shape=(_sds((t, HW), BF16),) + _gather8_shapes(gather),
        in_specs=[blk, blk, blk] + [ANY] * na, out_specs=(blk,) + (ANY,) * na,
        scratch_shapes=_gather8_sems(na) if na else [],
        compiler_params=_params("arbitrary", "arbitrary"),
    )(q, k, v, *gather)
    return res[0], (_own_block_placed(res[1:], gather) if na else ())


def _fwd_mix(attn, u0, zgate, x, mod3, wo_p, cw, cb, lng, lnb, wpw, wout, tm, tps):
    t, d = x.shape
    hpt = tm // HALO
    cwc, cbc = _by_lane_chunk(cw), _by_lane_chunk(cb)

    def body(a_ref, u_ref, uh_ref, zg_ref, x_ref, mod_ref, wo_ref, cw_ref, cb_ref, lng_ref, lnb_ref, wpw_ref, wout_ref,
             x1_ref, mixed_ref, mpre_ref, ya_ref, yb_ref, u1_ref, u3_ref, ext_ref):
        i = pl.program_id(0)
        ya = _dot(a_ref[...], wo_ref[...])
        ya_ref[...] = ya
        first = (i % tps) == 0
        _fill_shifted(ext_ref, jnp.where(first, 0.0, uh_ref[...]), u_ref[...])
        for lc, ls in _lane_chunks():
            acc = jnp.broadcast_to(cb_ref[lc], (tm, LANES))
            for kk in range(CONV_W):
                o = HALO - (CONV_W - 1) + kk
                a = (o // SUBLANES) * SUBLANES
                acc = acc + cw_ref[lc, kk:kk + 1, :] * ext_ref[o % SUBLANES, lc, a:a + tm, :]
            u1_ref[:, ls] = acc
        acc = u1_ref[...]
        mu = jnp.mean(acc, axis=-1, keepdims=True)
        xc = acc - mu
        rstd = lax.rsqrt(jnp.mean(xc * xc, axis=-1, keepdims=True) + EPS)
        l = (xc * rstd) * lng_ref[...] + lnb_ref[...]
        u3 = (l * jax.nn.sigmoid(l)).astype(BF16)
        u3_ref[...] = u3
        yb = _dot(u3, wpw_ref[...])
        yb_ref[...] = yb
        zg = zg_ref[...]
        mpre = (jax.nn.sigmoid(zg[:, :d]) * ya + jax.nn.sigmoid(zg[:, d:]) * yb).astype(BF16)
        mpre_ref[...] = mpre
        mixed = _dot(mpre, wout_ref[...])
        mixed_ref[...] = mixed
        x1_ref[...] = x_ref[...] + mod_ref[2:3, :] * mixed

    halo = pl.BlockSpec((HALO, CONV_CH), lambda i: (jnp.maximum(i * hpt - 1, 0), 0))
    return pl.pallas_call(
        body, name="fwd_mix", grid=(t // tm,),
        out_shape=(_sds((t, d), F32), _sds((t, d), F32), _sds((t, d), BF16), _sds((t, d), F32), _sds((t, d), F32),
                   _sds((t, CONV_CH), F32), _sds((t, CONV_CH), BF16)),
        in_specs=[_row(tm, HW), _row(tm, CONV_CH), halo, _row(tm, 2 * d), _row(tm, d), _modspec(d, tps),
                  _full(wo_p.shape), _full(cwc.shape), _full(cbc.shape), _full((1, CONV_CH)), _full((1, CONV_CH)),
                  _full(wpw.shape), _full(wout.shape)],
        out_specs=(_row(tm, d), _row(tm, d), _row(tm, d), _row(tm, d), _row(tm, d), _row(tm, CONV_CH),
                   _row(tm, CONV_CH)),
        scratch_shapes=[pltpu.VMEM(_shifted_shape(tm), F32)],
        compiler_params=_params("arbitrary"),
    )(attn, u0, u0, zgate, x, mod3, wo_p, cwc, cbc, lng, lnb, wpw, wout)


def _fwd_ffn(x1, target, g2, mod3, w1, w2, tm, tps):
    t, d = x1.shape
    dff = w1.shape[1]

    def body(x1_ref, tg_ref, g_ref, mod_ref, w1_hbm, w2_hbm,
             h2_ref, a_ref, r_ref, dy_ref, df_ref, dgate_ref, loss_ref, w1_ref, w2_ref):
        i = pl.program_id(0)
        _load_resident(i, [(w1_hbm, w1_ref), (w2_hbm, w2_ref)])
        x1v = x1_ref[...]
        gate2 = mod_ref[5:6, :]
        n, _ = _rms(x1v)
        h2 = ((n * g_ref[...]) * (1.0 + mod_ref[4:5, :]) + mod_ref[3:4, :]).astype(BF16)
        h2_ref[...] = h2
        a = _dot(h2, w1_ref[...])
        a_ref[...] = a
        r = jnp.square(jnp.maximum(a, 0.0)).astype(BF16)
        r_ref[...] = r
        f = _dot(r, w2_ref[...])
        e = (x1v + gate2 * f) - tg_ref[...]
        part = 0.5 * jnp.sum(jnp.mean(e * e, axis=-1, keepdims=True), axis=0, keepdims=True)
        _acc(loss_ref, jnp.broadcast_to(part, loss_ref.shape), i == 0)
        dy = e * (1.0 / d)
        dy_ref[...] = dy
        df_ref[...] = (dy * gate2).astype(BF16)
        _acc(dgate_ref, jnp.sum(dy * f, axis=0, keepdims=True), (i % tps) == 0)

    nseq = t // (tm * tps)
    return pl.pallas_call(
        body, name="fwd_ffn", grid=(t // tm,),
        out_shape=(_sds((t, d), BF16), _sds((t, dff), F32), _sds((t, dff), BF16), _sds((t, d), F32), _sds((t, d), BF16),
                   _sds((nseq, 1, d), F32), _sds((8, LANES), F32)),
        in_specs=[_row(tm, d), _row(tm, d), _full((1, d)), _modspec(d, tps), ANY, ANY],
        out_specs=(_row(tm, d), _row(tm, dff), _row(tm, dff), _row(tm, d), _row(tm, d), _seqv(d, tps),
                   _full((8, LANES))),
        scratch_shapes=[pltpu.VMEM(w1.shape, BF16), pltpu.VMEM(w2.shape, BF16)],
        compiler_params=_params("arbitrary"),
    )(x1, target, g2, mod3, w1, w2)


def _bwd_ffn(df, a, x1, dy, mixed, g2, mod3, w2, w1, tm, tps):
    t, d = x1.shape
    dff = a.shape[1]

    def body(df_ref, a_ref, x1_ref, dy_ref, mx_ref, g_ref, mod_ref, w2_hbm, w1_hbm,
             da_ref, dx1_ref, dmixed_ref, dshift_ref, dscale_ref, dgate1_ref, dg2_ref, w2_ref, w1_ref):
        i = pl.program_id(0)
        _load_resident(i, [(w2_hbm, w2_ref), (w1_hbm, w1_ref)])
        first_seq = (i % tps) == 0
        dr = _dot_nt(df_ref[...], w2_ref[...])
        da = (dr * (2.0 * jnp.maximum(a_ref[...], 0.0))).astype(BF16)
        da_ref[...] = da
        dh2 = _dot_nt(da, w1_ref[...])
        n, r = _rms(x1_ref[...])
        g = g_ref[...]
        sc1 = 1.0 + mod_ref[4:5, :]
        _acc(dshift_ref, jnp.sum(dh2, axis=0, keepdims=True), first_seq)
        _acc(dscale_ref, jnp.sum(dh2 * (n * g), axis=0, keepdims=True), first_seq)
        _acc(dg2_ref, jnp.sum((dh2 * sc1) * n, axis=0, keepdims=True), i == 0)
        dx1 = dy_ref[...] + _rms_bwd(n, r, (dh2 * sc1) * g)
        dx1_ref[...] = dx1
        _acc(dgate1_ref, jnp.sum(dx1 * mx_ref[...], axis=0, keepdims=True), first_seq)
        dmixed_ref[...] = (dx1 * mod_ref[2:3, :]).astype(BF16)

    nseq = t // (tm * tps)
    sv = _sds((nseq, 1, d), F32)
    return pl.pallas_call(
        body, name="bwd_ffn", grid=(t // tm,),
        out_shape=(_sds((t, dff), BF16), _sds((t, d), F32), _sds((t, d), BF16), sv, sv, sv, _sds((1, d), F32)),
        in_specs=[_row(tm, d), _row(tm, dff), _row(tm, d), _row(tm, d), _row(tm, d), _full((1, d)), _modspec(d, tps),
                  ANY, ANY],
        out_specs=(_row(tm, dff), _row(tm, d), _row(tm, d), _seqv(d, tps), _seqv(d, tps), _seqv(d, tps),
                   _full((1, d))),
        scratch_shapes=[pltpu.VMEM(w2.shape, BF16), pltpu.VMEM(w1.shape, BF16)],
        compiler_params=_params("arbitrary"),
    )(df, a, x1, dy, mixed, g2, mod3, w2, w1)


def _bwd_mix(dmixed, zgate, ya, yb, u1, lng, lnb, wout, wo_p, wpw, tm, swap=()):
    t, d = ya.shape
    _, _, npad = _layout(d)
    nw = len(swap)
    n_steps = t // tm

    def body(dm_ref, zg_ref, ya_ref, yb_ref, u1_ref, lng_ref, lnb_ref, wout_ref, wo_ref, wpw_ref, *rest):
        dya_ref, dyb_ref, dz_ref, do_ref, du1_ref, dlng_ref, dlnb_ref, dcb_ref = rest[nw:nw + 8]
        i = pl.program_id(0)
        if nw:
            start, finish = _swap_phases(rest[:nw], rest[nw + 8:2 * nw + 8], *rest[2 * nw + 8:])
            pl.when(i == 0)(start)
        dmpre = _dot_nt(dm_ref[...], wout_ref[...])
        zg = zg_ref[...]
        ga = jax.nn.sigmoid(zg[:, :d])
        gb = jax.nn.sigmoid(zg[:, d:])
        dya = (dmpre * ga).astype(BF16)
        dyb = (dmpre * gb).astype(BF16)
        dya_ref[...] = dya
        dyb_ref[...] = dyb
        dz_ref[:, :d] = ((dmpre * ya_ref[...]) * (ga * (1.0 - ga))).astype(BF16)
        dz_ref[:, d:] = ((dmpre * yb_ref[...]) * (gb * (1.0 - gb))).astype(BF16)
        do_ref[...] = _dot_nt(dya, wo_ref[...]).astype(BF16)
        du3 = _dot_nt(dyb, wpw_ref[...])
        u1 = u1_ref[...]
        mu = jnp.mean(u1, axis=-1, keepdims=True)
        xc = u1 - mu
        rstd = lax.rsqrt(jnp.mean(xc * xc, axis=-1, keepdims=True) + EPS)
        nh = xc * rstd
        l = nh * lng_ref[...] + lnb_ref[...]
        sg = jax.nn.sigmoid(l)
        dl = du3 * (sg * (1.0 + l * (1.0 - sg)))
        _acc(dlng_ref, jnp.sum(dl * nh, axis=0, keepdims=True), i == 0)
        _acc(dlnb_ref, jnp.sum(dl, axis=0, keepdims=True), i == 0)
        dnh = dl * lng_ref[...]
        du1 = rstd * (dnh - jnp.mean(dnh, axis=-1, keepdims=True) - nh * jnp.mean(dnh * nh, axis=-1, keepdims=True))
        du1_ref[...] = du1
        _acc(dcb_ref, jnp.sum(du1, axis=0, keepdims=True), i == 0)
        if nw:
            pl.when(i == n_steps - 1)(finish)

    cv = _sds((1, CONV_CH), F32)
    res = pl.pallas_call(
        body, name="bwd_mix", grid=(n_steps,),
        out_shape=(_sds((t, d), BF16), _sds((t, d), BF16), _sds((t, npad), BF16), _sds((t, HW), BF16),
                   _sds((t, CONV_CH), F32), cv, cv, cv) + _swap_shapes(swap),
        in_specs=[_row(tm, d), _row(tm, 2 * d), _row(tm, d), _row(tm, d), _row(tm, CONV_CH), _full((1, CONV_CH)),
                  _full((1, CONV_CH)), _full(wout.shape), _full(wo_p.shape), _full(wpw.shape)] + [ANY] * nw,
        out_specs=(_row(tm, d), _row(tm, d), _row(tm, 2 * d), _row(tm, HW), _row(tm, CONV_CH),
                   _full((1, CONV_CH)), _full((1, CONV_CH)), _full((1, CONV_CH))) + (ANY,) * nw,
        scratch_shapes=_swap_sems(swap) if nw else [],
        compiler_params=_params("arbitrary"),
    )(dmixed, zgate, ya, yb, u1, lng, lnb, wout, wo_p, wpw, *swap)
    return res[:8] + (res[8:],)


def _bwd_conv(dz, du1, u0, zglu, cw, tm, tps):
    t = du1.shape[0]
    d = (dz.shape[1] - MLA_IN - 2 * CONV_CH) // 2
    p_glu, _, _ = _layout(d)
    hpt = tm // HALO
    last_blk = t // HALO - 1
    cwc = _by_lane_chunk(cw)

    def body(dz_hbm, du_ref, dun_ref, u_ref, zl_ref, cw_ref, dzl_ref, dcw_ref, dext_ref, uc_ref, dcw8_ref, du0_ref):
        i = pl.program_id(0)
        last = (i % tps) == (tps - 1)
        _fill_shifted(dext_ref, du_ref[...], jnp.where(last, 0.0, dun_ref[...]))
        for lc, ls in _lane_chunks():
            uc_ref[lc] = u_ref[:, ls]

        @pl.when(i == 0)
        def _():
            dcw8_ref[...] = jnp.zeros_like(dcw8_ref)

        groups = CONV_ROWS // SUBLANES

        def conv_chunk(c, carry):
            lc, r0 = _conv_chunk(c)
            u = uc_ref[lc, pl.ds(r0, CONV_ROWS), :]
            du0 = jnp.zeros((CONV_ROWS, LANES), F32)
            for kk in range(CONV_W):
                win = _shifted(dext_ref, CONV_W - 1 - kk, lc, r0)
                prod = u * win
                part = prod[:SUBLANES]
                for g in range(1, groups):
                    part = part + prod[g * SUBLANES:(g + 1) * SUBLANES]
                dcw8_ref[lc, kk] += part
                du0 = du0 + cw_ref[lc, kk:kk + 1, :] * win
            du0_ref[lc, pl.ds(r0, CONV_ROWS), :] = du0
            return carry

        lax.fori_loop(0, CONV_LC * (tm // CONV_ROWS), conv_chunk, 0)

        @pl.when(i == pl.num_programs(0) - 1)
        def _():
            for lc, ls in _lane_chunks():
                dcw_ref[:, ls] = jnp.sum(dcw8_ref[lc], axis=1)

        for lc, ls in _lane_chunks():
            du0 = du0_ref[lc]
            ga = zl_ref[:, ls]
            sb = jax.nn.sigmoid(zl_ref[:, CONV_CH + lc * LANES:CONV_CH + (lc + 1) * LANES])
            dzl_ref[:, ls] = (du0 * sb).astype(BF16)
            dzl_ref[:, CONV_CH + lc * LANES:CONV_CH + (lc + 1) * LANES] = ((du0 * ga) * (sb * (1.0 - sb))).astype(BF16)

    nxt = pl.BlockSpec((HALO, CONV_CH), lambda i: (jnp.minimum((i + 1) * hpt, last_blk), 0))
    glu_blk = p_glu // (2 * CONV_CH)
    return pl.pallas_call(
        body, name="bwd_conv", grid=(t // tm,),
        out_shape=(_sds(dz.shape, BF16), _sds(cw.shape, F32)),
        in_specs=[ANY, _row(tm, CONV_CH), nxt, _row(tm, CONV_CH), _row(tm, 2 * CONV_CH), _full(cwc.shape)],
        out_specs=(pl.BlockSpec((tm, 2 * CONV_CH), lambda i: (i, glu_blk)), _full(cw.shape)),
        scratch_shapes=[pltpu.VMEM(_shifted_shape(tm), F32), pltpu.VMEM((CONV_LC, tm, LANES), F32),
                        pltpu.VMEM((CONV_LC, HALO, SUBLANES, LANES), F32), pltpu.VMEM((CONV_LC, tm, LANES), F32)],
        input_output_aliases={0: 0},
        compiler_params=_params("arbitrary"),
    )(dz, du1, du1, u0, zglu, cwc)


def _attn_bwd(q, k, v, do, nseq, seq, scatter=()):
    t = q.shape[0]
    ns = len(scatter)
    blk = pl.BlockSpec((seq, LANES), lambda b, h: (b, h))
    n_steps = nseq * N_HEADS

    def body(q_ref, k_ref, v_ref, do_ref, *rest):
        dq_ref, dk_ref, dv_ref = rest[ns:ns + 3]
        dka_ref, dva_ref = rest[2 * ns + 3:2 * ns + 5]
        if ns:
            start, finish = _scatter_phases(rest[:ns], rest[ns + 3:2 * ns + 3], *rest[2 * ns + 5:])
            step = pl.program_id(0) * N_HEADS + pl.program_id(1)
            pl.when(step == 0)(start)
        dka_ref[...] = jnp.zeros_like(dka_ref)
        dva_ref[...] = jnp.zeros_like(dva_ref)
        mask = _diag_mask()
        nb = seq // BQ
        block = lambda j: (_scores(q_ref[j * BQ:(j + 1) * BQ, :], k_ref, j * BQ, (j + 1) * BQ),
                           _scores(do_ref[j * BQ:(j + 1) * BQ, :], v_ref, j * BQ, (j + 1) * BQ))
        ahead = [block(j) for j in range(min(AHEAD, nb))]
        for i in range(nb):
            lo, e = i * BQ, (i + 1) * BQ
            q_i = q_ref[lo:e, :]
            do_i = do_ref[lo:e, :]
            scores, (dpp, dpd) = ahead.pop(0)
            if i + AHEAD < nb:
                ahead.append(block(i + AHEAD))
            pp, pd, l = _softmax_parts(scores, mask)
            inv = 1.0 / l
            pd = pd * inv
            delta = jnp.sum(pd * dpd, axis=-1, keepdims=True)
            if lo:
                pp = pp * inv
                delta = delta + jnp.sum(pp * dpp, axis=-1, keepdims=True)
            dsd = (pd * (dpd - delta)).astype(BF16)
            dq = _dot(dsd, k_ref[lo:e, :])
            dka_ref[lo:e, :] += _dot_tn(dsd, q_i)
            dva_ref[lo:e, :] += _dot_tn(pd.astype(BF16), do_i)
            if lo:
                dsp = (pp * (dpp - delta)).astype(BF16)
                dq = dq + _dot(dsp, k_ref[:lo, :])
                dka_ref[:lo, :] += _dot_tn(dsp, q_i)
                dva_ref[:lo, :] += _dot_tn(pp.astype(BF16), do_i)
            dq_ref[lo:e, :] = dq * SM_SCALE
        dk_ref[...] = dka_ref[...] * SM_SCALE
        dv_ref[...] = dva_ref[...].astype(BF16)
        if ns:
            pl.when(step == n_steps - 1)(finish)

    res = pl.pallas_call(
        body, name="attn_bwd", grid=(nseq, N_HEADS),
        out_shape=(_sds((t, HW), F32), _sds((t, HW), F32), _sds((t, HW), BF16)) + _scatter_shapes(scatter),
        in_specs=[blk] * 4 + [ANY] * ns, out_specs=(blk,) * 3 + (ANY,) * ns,
        scratch_shapes=[pltpu.VMEM((seq, LANES), F32), pltpu.VMEM((seq, LANES), F32)]
        + (_scatter_sems(ns) if ns else []),
        compiler_params=_params("arbitrary", "arbitrary"),
    )(q, k, v, do, *scatter)
    return res[0], res[1], res[2], res[3:]


def _mla_bwd(dz, dq, dk, dv, zm, gql, gkvl, gq, gk, tabs, wuq_p, wk_p, wv_p, tm, tps):
    t = zm.shape[0]
    d = (dz.shape[1] - MLA_IN - 2 * CONV_CH) // 2
    _, p_q, _ = _layout(d)
    c_t, s1_t, s2_t = tabs
    tab = pl.BlockSpec((tm, LANES), lambda i: (i % tps, 0))

    def body(dz_hbm, dq_ref, dk_ref, dv_ref, zm_ref, gql_ref, gkvl_ref, gq_ref, gk_ref, c_ref, s1_ref, s2_ref,
             wuq_ref, wk_ref, wv_ref,
             dzm_ref, dqpre_ref, dkh_ref, dgq_ref, dgk_ref, dgql_ref, dgkvl_ref):
        i = pl.program_id(0)
        c, s1, s2 = c_ref[...], s1_ref[...], s2_ref[...]
        nq, rq = _rms(zm_ref[:, :Q_RANK])
        qpre = _dot((nq * gql_ref[...]).astype(BF16), wuq_ref[...])
        nkv, rkv = _rms(zm_ref[:, Q_RANK:OFF_KV])
        knope = _dot((nkv * gkvl_ref[...]).astype(BF16), wk_ref[...])
        zkr_v = zm_ref[:, OFF_KV:]
        gk = gk_ref[...]
        kr_roped = _rope(zkr_v * gk, c, s1, s2)
        dgq = jnp.zeros((1, LANES), F32)
        dgk = jnp.zeros((1, LANES), F32)
        dzkr = jnp.zeros((tm, LANES), F32)
        dt_sum = jnp.zeros((tm, LANES), F32)
        slabs = [slice(hd * LANES, (hd + 1) * LANES) for hd in range(N_HEADS)]
        gq = gq_ref[...]
        rqh = [_head_rms(qpre[:, sl])[1] for sl in slabs]
        rkh = [_head_rms(knope[:, sl] + zkr_v)[1] for sl in slabs]
        dyr = [_rope_t(dq_ref[:, sl], c, s1, s2) for sl in slabs]
        nqh = [qpre[:, sl] * rqh[hd] for hd, sl in enumerate(slabs)]
        sq = [jnp.sum((dyr[hd] * gq) * nqh[hd], axis=-1, keepdims=True) for hd in range(N_HEADS)]
        dr = [jnp.sum(dk_ref[:, sl] * (knope[:, sl] * gk + kr_roped), axis=-1, keepdims=True) for sl in slabs]
        for hd, sl in enumerate(slabs):
            dgq = dgq + jnp.sum(dyr[hd] * nqh[hd], axis=0, keepdims=True)
            dqpre_ref[:, sl] = (rqh[hd] * (dyr[hd] * gq - nqh[hd] * (sq[hd] * (1.0 / QK_HEAD)))).astype(BF16)
            kn = knope[:, sl]
            r = rkh[hd]
            dt = dk_ref[:, sl] * r
            via_r = (dr[hd] * (r * r * r) * (-1.0 / QK_HEAD)) * (kn + zkr_v)
            dgk = dgk + jnp.sum(dt * kn, axis=0, keepdims=True)
            dt_sum = dt_sum + dt
            dzkr = dzkr + via_r
            dkh_ref[:, sl] = (dt * gk + via_r).astype(BF16)
        de = _rope_t(dt_sum, c, s1, s2)
        dzkr = dzkr + de * gk
        dgk = dgk + jnp.sum(de * zkr_v, axis=0, keepdims=True)
        _acc(dgq_ref, dgq[:, :QK_HEAD], i == 0)
        _acc(dgk_ref, dgk[:, :QK_HEAD], i == 0)
        dzm_ref[:, OFF_KV:] = dzkr.astype(BF16)
        dqln = _dot_nt(dqpre_ref[...], wuq_ref[...])
        _acc(dgql_ref, jnp.sum(dqln * nq, axis=0, keepdims=True), i == 0)
        dzm_ref[:, :Q_RANK] = _rms_bwd(nq, rq, dqln * gql_ref[...]).astype(BF16)
        dkvn = _dot_nt(dkh_ref[...], wk_ref[...]) + _dot_nt(dv_ref[...], wv_ref[...])
        _acc(dgkvl_ref, jnp.sum(dkvn * nkv, axis=0, keepdims=True), i == 0)
        dzm_ref[:, Q_RANK:OFF_KV] = _rms_bwd(nkv, rkv, dkvn * gkvl_ref[...]).astype(BF16)

    return pl.pallas_call(
        body, name="mla_bwd", grid=(t // tm,),
        out_shape=(_sds(dz.shape, BF16), _sds((t, HW), BF16), _sds((t, HW), BF16), _sds((1, QK_HEAD), F32),
                   _sds((1, QK_HEAD), F32), _sds((1, Q_RANK), F32), _sds((1, KV_RANK), F32)),
        in_specs=[ANY, _row(tm, HW), _row(tm, HW), _row(tm, HW), _row(tm, MLA_IN),
                  _full((1, Q_RANK)), _full((1, KV_RANK)), _full((1, LANES)), _full((1, LANES)), tab, tab, tab,
                  _full(wuq_p.shape), _full(wk_p.shape), _full(wv_p.shape)],
        out_specs=(pl.BlockSpec((tm, MLA_IN), lambda i: (i, p_q // MLA_IN)), _row(tm, HW), _row(tm, HW),
                   _full((1, QK_HEAD)), _full((1, QK_HEAD)), _full((1, Q_RANK)), _full((1, KV_RANK))),
        input_output_aliases={0: 0},
        compiler_params=_params("arbitrary"),
    )(dz, dq, dk, dv, zm, gql, gkvl, gq, gk, c_t, s1_t, s2_t, wuq_p, wk_p, wv_p)


def _bwd_in(dz, x, dx1, g1, mod3, win_p, tm, tps, scatter=()):
    t, d = x.shape
    npad = dz.shape[1]

    ns = len(scatter)
    n_steps = t // tm

    def body(dz_ref, x_ref, dx1_ref, g_ref, mod_ref, wt_hbm, *rest):
        gx_ref, dshift_ref, dscale_ref, dg1_ref = rest[ns:ns + 4]
        wt_ref = rest[2 * ns + 4]
        i = pl.program_id(0)
        if ns:
            start, finish = _scatter_phases(rest[:ns], rest[ns + 4:2 * ns + 4], *rest[2 * ns + 5:])
            pl.when(i == 0)(start)
        _load_resident(i, [(wt_hbm, wt_ref)])
        first_seq = (i % tps) == 0
        g = g_ref[...]
        sc1 = 1.0 + mod_ref[1:2, :]
        nb = max(tm // ROW_BAND, 1)
        bands = [slice(b * (tm // nb), (b + 1) * (tm // nb)) for b in range(nb)]
        dhs = [_dot_nt(dz_ref[rows, :], wt_ref[...]) for rows in bands]
        sums = [jnp.zeros((1, d), F32)] * 3
        col = lambda v: jnp.sum(v, axis=0, keepdims=True)
        for rows, dh in zip(bands, dhs):
            n, r = _rms(x_ref[rows, :])
            sums = [sums[0] + col(dh), sums[1] + col(dh * (n * g)), sums[2] + col((dh * sc1) * n)]
            gx_ref[rows, :] = dx1_ref[rows, :] + _rms_bwd(n, r, (dh * sc1) * g)
        _acc(dshift_ref, sums[0], first_seq)
        _acc(dscale_ref, sums[1], first_seq)
        _acc(dg1_ref, sums[2], i == 0)
        if ns:
            pl.when(i == n_steps - 1)(finish)

    nseq = t // (tm * tps)
    sv = _sds((nseq, 1, d), F32)
    res = pl.pallas_call(
        body, name="bwd_in", grid=(n_steps,),
        out_shape=(_sds((t, d), F32), sv, sv, _sds((1, d), F32)) + _scatter_shapes(scatter),
        in_specs=[_row(tm, npad), _row(tm, d), _row(tm, d), _full((1, d)), _modspec(d, tps), ANY] + [ANY] * ns,
        out_specs=(_row(tm, d), _seqv(d, tps), _seqv(d, tps), _full((1, d))) + (ANY,) * ns,
        scratch_shapes=[pltpu.VMEM(win_p.shape, BF16)] + (_scatter_sems(ns) if ns else []),
        compiler_params=_params("arbitrary"),
    )(dz, x, dx1, g1, mod3, win_p, *scatter)
    return res[0], res[1], res[2], res[3], res[4:]


def _tile_of(n, choices):
    for c in choices:
        if n % c == 0:
            return c
    return n


def _tn_matmul(a, b, name, col_shards=0):
    t, k = a.shape
    n = b.shape[1]
    tk = _tile_of(k, (1024, 512, 256, 128))
    tn = n // col_shards if col_shards else _tile_of(n, (1024, 896, 768, 512, 384, 256, 128))
    tt = _tile_of(t, (4096, 2048, 1024, 512, 256))

    def body(a_ref, b_ref, o_ref):
        _acc(o_ref, _dot_tn(a_ref[...], b_ref[...]), pl.program_id(2) == 0)

    if col_shards:
        out_shape, out_spec = _sds((col_shards, k, tn), F32), pl.BlockSpec((None, tk, tn), lambda i, j, s: (j, i, 0))
    else:
        out_shape, out_spec = _sds((k, n), F32), pl.BlockSpec((tk, tn), lambda i, j, s: (i, j))
    return pl.pallas_call(
        body, name=name, grid=(k // tk, n // tn, t // tt), out_shape=out_shape,
        in_specs=[pl.BlockSpec((tt, tk), lambda i, j, s: (s, i)), pl.BlockSpec((tt, tn), lambda i, j, s: (s, j))],
        out_specs=out_spec, compiler_params=_params("arbitrary", "arbitrary", "arbitrary"),
    )(a, b)


N_SHARD = 4
COL_SHARDED = ("w_in", "w_uq", "w_ukv", "w_o_mla", "w_pw_out", "w_ff1")
ROW_SHARDED = ("w_out", "w_ff2")
BIG = ("w_in", "w_uq", "w_ukv", "w_o_mla", "w_pw_out", "w_out", "w_ff1", "w_ff2")
SMALL = ("norm1_g", "q_latent_g", "kv_latent_g", "qk_norm_q_g", "qk_norm_k_g", "conv_b", "conv_ln_g", "conv_ln_b",
         "norm2_g")
WEIGHTS = ("w_ada", "b_ada", "norm1_g", "w_in", "q_latent_g", "w_uq", "kv_latent_g", "w_ukv", "qk_norm_q_g",
           "qk_norm_k_g", "w_o_mla", "conv_w", "conv_b", "conv_ln_g", "conv_ln_b", "w_pw_out", "w_out", "norm2_g",
           "w_ff1", "w_ff2")


def _pad_heads(w, width):
    k = w.shape[0]
    w3 = w.reshape(k, N_HEADS, width)
    return jnp.pad(w3, ((0, 0), (0, 0), (0, LANES - width))).reshape(k, HW)


def _unpad_heads(g, width):
    k = g.shape[0]
    return g.reshape(k, N_HEADS, LANES)[:, :, :width].reshape(k, N_HEADS * width)


def _pad_win(w):
    d = w.shape[0]
    z = lambda n: jnp.zeros((d, n), w.dtype)
    return jnp.concatenate([w[:, OFF_GLU:], w[:, OFF_KR:OFF_GLU], w[:, :OFF_KV], z(KR_LANE), w[:, OFF_KV:OFF_KR],
                            z(LANES - KR_LANE - QK_ROPE)], axis=1)


def _unpad_win(g):
    d = g.shape[0]
    p_glu, p_q, _ = _layout(d)
    kr = p_q + OFF_KV + KR_LANE
    return jnp.concatenate([g[:, p_q:p_q + OFF_KV], g[:, kr:kr + QK_ROPE], g[:, p_glu:p_q], g[:, :p_glu]], axis=1)


def _col_shards(g):
    k, n = g.shape
    return g.reshape(k, N_SHARD, n // N_SHARD).transpose(1, 0, 2)


def _from_shards(g, name):
    ns, ks, nn = g.shape
    if name in ROW_SHARDED:
        return g.reshape(ns * ks, nn)
    return g.transpose(1, 0, 2).reshape(ks, ns * nn)


EARLY = ("w_in", "w_uq", "w_ukv")
LATE = ("w_o_mla", "w_pw_out", "w_out", "w_ff1", "w_ff2")


def _assemble(names, gathered):
    return {n: _from_shards(g.reshape((N_SHARD, 2 * g.shape[1]) + g.shape[2:]), n) for n, g in zip(names, gathered)}


LARGE = ("w_in", "w_ff1", "w_ff2")
GROUP_A = ("w_out", "w_ff1", "w_ff2")
GROUP_B = ("w_in", "w_uq", "w_ukv", "w_o_mla", "w_pw_out")


def _pair_halves(g):
    return g.reshape(N_SHARD, 2, g.shape[1] // 2, g.shape[2])


def _pair_sums(names, halves, from_sibling):
    if not halves:
        return []
    cidx = lax.axis_index("c").reshape(1).astype(jnp.int32)
    out = {n: _add_pair(g, l, cidx, "pair_sum_" + n)
           for n, g, l in zip(names, halves, from_sibling) if n in LARGE}
    small = [j for j, n in enumerate(names) if n not in LARGE]
    if small:
        res = _add_pair_whole([halves[j] for j in small], [from_sibling[j] for j in small], cidx,
                              "pair_sum_small_" + names[small[0]])
        out.update({names[j]: r for j, r in zip(small, res)})
    return [out[n] for n in names]


def _local_step(x, target, mod, sp, w, late=None, tm=256):
    comm = late is not None
    w = dict(w)
    nseq, seq, d = x.shape
    t = nseq * seq
    tps = seq // tm
    xf = x.reshape(t, d)
    tg = target.reshape(t, d)
    mod3 = mod.reshape(nseq, N_MOD, d)

    win_p = _pad_win(w["w_in"])
    wuq_p = _pad_heads(w["w_uq"], QK_HEAD)
    wkv3 = w["w_ukv"].reshape(KV_RANK, N_HEADS, QK_NOPE + V_HEAD)
    wk_p = _pad_heads(wkv3[:, :, :QK_NOPE].reshape(KV_RANK, -1), QK_NOPE)
    wv_p = _pad_heads(wkv3[:, :, QK_NOPE:].reshape(KV_RANK, -1), V_HEAD)
    cw = jnp.pad(w["conv_w"], ((0, HALO - CONV_W), (0, 0)))
    pad_g = lambda g: jnp.pad(g, ((0, 0), (0, LANES - QK_HEAD)))
    gq, gk = pad_g(sp["qk_norm_q_g"]), pad_g(sp["qk_norm_k_g"])
    tabs = _rope_tables(seq)

    tm_in, tps_in = (2 * tm, tps // 2) if tps % 2 == 0 else (tm, tps)
    h, zm, zglu, zgate, u0 = _fwd_in(xf, sp["norm1_g"], mod3, win_p, tm_in, tps_in)
    q, k, v, qln, kvn = _mla_prep(zm, sp["q_latent_g"], sp["kv_latent_g"], gq, gk, tabs, wuq_p, wk_p, wv_p, tm, tps)
    attn, gathered = _attn_fwd(q, k, v, nseq, seq, tuple(late) if comm else ())
    if comm:
        w.update(_assemble(LATE, gathered))
    wo_p = jnp.pad(w["w_o_mla"].reshape(N_HEADS, V_HEAD, d), ((0, 0), (0, LANES - V_HEAD), (0, 0))).reshape(HW, d)
    x1, mixed, mpre, ya, yb, u1, u3 = _fwd_mix(attn, u0, zgate, xf, mod3, wo_p, cw, sp["conv_b"], sp["conv_ln_g"],
                                               sp["conv_ln_b"], w["w_pw_out"], w["w_out"], tm, tps)
    h2, a, r, dy, df, dgate2, loss_acc = _fwd_ffn(x1, tg, sp["norm2_g"], mod3, w["w_ff1"], w["w_ff2"], tm, tps)
    da, dx1, dmixed, dshift2, dscale2, dgate1, dg2 = _bwd_ffn(df, a, x1, dy, mixed, sp["norm2_g"], mod3,
                                                              w["w_ff2"], w["w_ff1"], tm, tps)
    gw = {
        "w_out": _tn_matmul(mpre, dmixed, "dw_out").reshape(N_SHARD, d // N_SHARD, d),
        "w_ff1": _tn_matmul(h2, da, "dw_ff1", N_SHARD),
        "w_ff2": _tn_matmul(r, df, "dw_ff2").reshape(N_SHARD, -1, d),
    }
    halves_a = [_pair_halves(gw[n]) for n in GROUP_A] if comm else []
    dya, dyb, dz, do, du1, dlng, dlnb, dcb, from_sibling = _bwd_mix(
        dmixed, zgate, ya, yb, u1, sp["conv_ln_g"], sp["conv_ln_b"], w["w_out"], wo_p, w["w_pw_out"], tm, tuple(halves_a))
    pair_a = _pair_sums(GROUP_A, halves_a, from_sibling)
    dz, dcw = _bwd_conv(dz, du1, u0, zglu, cw, tm, tps)
    gw["conv_w"] = dcw
    dq, dk, dv, land_a = _attn_bwd(q, k, v, do, nseq, seq, tuple(p[1] for p in pair_a))
    dz, dqpre, dkh, dgq, dgk, dgql, dgkvl = _mla_bwd(dz, dq, dk, dv, zm, sp["q_latent_g"], sp["kv_latent_g"], gq, gk,
                                                      tabs, wuq_p, wk_p, wv_p, tm, tps)
    dwk_p = _tn_matmul(kvn, dkh, "dw_uk")
    dwv_p = _tn_matmul(kvn, dv, "dw_uv")
    dwkv = jnp.concatenate([dwk_p.reshape(KV_RANK, N_HEADS, LANES)[:, :, :QK_NOPE],
                            dwv_p.reshape(KV_RANK, N_HEADS, LANES)[:, :, :V_HEAD]], axis=2).reshape(KV_RANK, -1)
    dwo = _tn_matmul(attn, dya, "dw_o").reshape(N_HEADS, LANES, d)[:, :V_HEAD].reshape(MLA_WIDTH, d)
    gw["w_in"] = _col_shards(_unpad_win(_tn_matmul(h, dz, "dw_in")))
    gw["w_uq"] = _col_shards(_unpad_heads(_tn_matmul(qln, dqpre, "dw_uq"), QK_HEAD))
    gw["w_ukv"] = _col_shards(dwkv)
    gw["w_o_mla"] = _col_shards(dwo)
    gw["w_pw_out"] = _tn_matmul(u3, dyb, "dw_pw", N_SHARD)
    pair_b = []
    if comm:
        halves_b = [_pair_halves(gw[n]) for n in GROUP_B]
        pair_b = _pair_sums(GROUP_B, halves_b, _pair_swap(halves_b, "grad_pair_swap"))
    gx, dshift1, dscale1, dg1, land_b = _bwd_in(dz, xf, dx1, sp["norm1_g"], mod3, win_p, tm_in, tps_in,
                                                tuple(p[1] for p in pair_b))
    if comm:
        for n, p, l in zip(GROUP_A + GROUP_B, pair_a + pair_b, land_a + land_b):
            gw[n] = (p[0], l)
    gs = {
        "norm1_g": dg1, "q_latent_g": dgql, "kv_latent_g": dgkvl, "qk_norm_q_g": dgq, "qk_norm_k_g": dgk,
        "conv_b": dcb, "conv_ln_g": dlng, "conv_ln_b": dlnb, "norm2_g": dg2,
    }
    dmod = jnp.concatenate([dshift1, dscale1, dgate1, dshift2, dscale2, dgate2], axis=2).reshape(nseq, N_MOD * d)
    return loss_acc, gx.reshape(nseq, seq, d), dmod, gw, gs


def kernel(x, c, w_ada, b_ada, norm1_g, w_in, q_latent_g, w_uq, kv_latent_g, w_ukv, qk_norm_q_g, qk_norm_k_g, w_o_mla, conv_w, conv_b, conv_ln_g, conv_ln_b, w_pw_out, w_out, norm2_g, w_ff1, w_ff2, loss_target, m_w_ada, m_b_ada, m_norm1_g, m_w_in, m_q_latent_g, m_w_uq, m_kv_latent_g, m_w_ukv, m_qk_norm_q_g, m_qk_norm_k_g, m_w_o_mla, m_conv_w, m_conv_b, m_conv_ln_g, m_conv_ln_b, m_w_pw_out, m_w_out, m_norm2_g, m_w_ff1, m_w_ff2, v_w_ada, v_b_ada, v_norm1_g, v_w_in, v_q_latent_g, v_w_uq, v_kv_latent_g, v_w_ukv, v_qk_norm_q_g, v_qk_norm_k_g, v_w_o_mla, v_conv_w, v_conv_b, v_conv_ln_g, v_conv_ln_b, v_w_pw_out, v_w_out, v_norm2_g, v_w_ff1, v_w_ff2):
    given = dict(locals())
    wts = {n: given[n][0] for n in WEIGHTS}
    mom = {n: given["m_" + n][0] for n in WEIGHTS}
    var = {n: given["v_" + n][0] for n in WEIGHTS}
    vec = lambda a: a.reshape(1, -1)
    nseq, seq, d = x.shape
    ix, iy, ic = _place()
    shard = 2 * ix + iy

    half = lambda n: lax.dynamic_slice_in_dim(wts[n].astype(BF16), ic * (wts[n].shape[0] // 2), wts[n].shape[0] // 2,
                                              axis=0)
    gathered = _all_gather8([half(n) for n in EARLY] + [wts["conv_w"], c], "gather_weights")
    full = _assemble(EARLY, gathered)
    full["conv_w"] = _from_shards(gathered[-2][0::2], "conv_w")
    c_all = gathered[-1].reshape(8 * nseq, d)

    n_ada = wts["w_ada"].shape[1]
    b_sh = lax.dynamic_slice_in_dim(vec(wts["b_ada"]), shard * n_ada, n_ada, axis=1)
    mod_sh = _ada_mod(c_all, wts["w_ada"], b_sh)
    hb = 4 * nseq
    mod_blk = lax.dynamic_slice_in_dim(mod_sh, ic * hb, hb, axis=0)
    (mod_all,) = _all_gather8([mod_blk], "gather_mod")
    mod_mine = lax.dynamic_slice_in_dim(mod_all, (2 * iy + ic) * nseq, nseq, axis=1)
    mod = jnp.concatenate([lax.dynamic_index_in_dim(mod_mine, 2 * s + ix, axis=0, keepdims=False)
                           for s in range(N_SHARD)], axis=1)

    sp = {n: vec(wts[n]) for n in SMALL}
    loss_part, grad_x, dmod, gw, gs = _local_step(x, loss_target, mod, sp, full, [half(n) for n in LATE])

    own_c = jnp.stack([shard, ic]).astype(jnp.int32)
    mine_sum = {n: _add_chips(gw[n][0], gw[n][1], own_c, "chip_sum_" + n) for n in LARGE}
    few = tuple(n for n in BIG if n not in LARGE)
    mine_sum.update(zip(few, _add_chips_whole([gw[n][0] for n in few], [gw[n][1] for n in few], own_c, "chip_sum_small")))
    summed, parts = _pair_gather_and_all_gather8(
        [mine_sum[n] for n in BIG], [dmod, gw["conv_w"], loss_part] + [gs[n] for n in SMALL], "tail_exchange")

    dmod_all = parts[0].reshape(8 * nseq, N_MOD * d)
    dmod_sh = lax.dynamic_slice_in_dim(dmod_all, shard * n_ada, n_ada, axis=1)
    res = _ada_bwd(c_all, dmod_all, dmod_sh, parts[1:])
    grads = {"w_ada": res[0], "b_ada": res[1]}
    n_cw = wts["conv_w"].shape[1]
    grads["conv_w"] = lax.dynamic_slice_in_dim(res[2], shard * n_cw, n_cw, axis=1)[:CONV_W]
    loss = res[3][0, 0]
    for n, g in zip(SMALL, res[4:]):
        grads[n] = g
    for n, g in zip(BIG, summed):
        grads[n] = g.reshape(wts[n].shape)

    delta, new_m, new_v = {}, {}, {}
    for n in LARGE + ("w_ada",):
        delta[n], new_m[n], new_v[n] = _adamw(wts[n], grads[n], mom[n], var[n], "adamw_" + n)
    rest = ("b_ada", "conv_w") + SMALL + few
    as2d = lambda a: a if a.ndim == 2 else vec(a)
    res = _adamw_small(*[[as2d(t[n]) for n in rest] for t in (wts, grads, mom, var)])
    for dst, arrs in zip((delta, new_m, new_v), res):
        for n, a in zip(rest, arrs):
            dst[n] = a

    outs = [loss, grad_x]
    for group in (grads, delta, new_m, new_v):
        outs += [group[n].reshape(given[n].shape) for n in WEIGHTS]
    return tuple(outs)
```

```python
import jax
import jax.numpy as jnp
from jax import lax
from jax.experimental import pallas as pl
from jax.experimental.pallas import tpu as pltpu

F32 = jnp.float32
BF16 = jnp.bfloat16
MESH = pl.DeviceIdType.MESH
ANY = pl.BlockSpec(memory_space=pl.ANY)

CHUNK = 64
CHUNK_SHIFT = 6
N_HEADS = 8
QK_NOPE = 64
QK_ROPE = 32
QK_HEAD = QK_NOPE + QK_ROPE
V_HEAD = 64
Q_RANK = 256
KV_RANK = 128
MLA_WIDTH = N_HEADS * V_HEAD
CONV_CH = 512
CONV_W = 31
ROPE_THETA = 10000.0
EPS = 1e-6
LANES = 128
SUBLANES = 8
HW = N_HEADS * LANES
OFF_KV = Q_RANK + KV_RANK
OFF_KR = OFF_KV + QK_ROPE
OFF_GLU = OFF_KR + 2 * CONV_CH
KR_LANE = QK_NOPE
MLA_IN = Q_RANK + KV_RANK + LANES
HALO = 32
N_MOD = 6

ADAM_LR = 0.001
ADAM_B1 = 0.9
ADAM_B2 = 0.999
ADAM_EPS = 1e-08
ADAM_WD = 0.01
ADAM_STEP = 10

VMEM_LIMIT = 56 * 1024 * 1024
BQ = 256


def _layout(d):
    p_glu = 2 * d
    p_q = p_glu + 2 * CONV_CH
    return p_glu, p_q, p_q + MLA_IN


def _params(*sem):
    return pltpu.CompilerParams(dimension_semantics=sem, vmem_limit_bytes=VMEM_LIMIT)


def _dot(a, b):
    return jnp.dot(a, b, preferred_element_type=F32)


def _dot_tn(a, b):
    return lax.dot_general(a, b, (((0,), (0,)), ((), ())), preferred_element_type=F32)


def _dot_nt(a, b):
    return lax.dot_general(a, b, (((1,), (1,)), ((), ())), preferred_element_type=F32)


def _acc(ref, val, first):
    @pl.when(first)
    def _():
        ref[...] = val

    @pl.when(jnp.logical_not(first))
    def _():
        ref[...] += val


def _rms(x):
    r = lax.rsqrt(jnp.mean(x * x, axis=-1, keepdims=True) + EPS)
    return x * r, r


def _rms_bwd(n, r, dn):
    return r * (dn - n * jnp.mean(dn * n, axis=-1, keepdims=True))


def _head_rms(sl):
    r = lax.rsqrt(jnp.sum(sl * sl, axis=-1, keepdims=True) * (1.0 / QK_HEAD) + EPS)
    return sl * r, r


def _head_rms_bwd(n, r, dn):
    return r * (dn - n * (jnp.sum(dn * n, axis=-1, keepdims=True) * (1.0 / QK_HEAD)))


def _rope(x, c, s1, s2):
    return x * c + pltpu.roll(x, QK_ROPE // 2, 1) * s1 + pltpu.roll(x, LANES - QK_ROPE // 2, 1) * s2


def _rope_t(dy, c, s1, s2):
    return dy * c + pltpu.roll(dy * s1, LANES - QK_ROPE // 2, 1) + pltpu.roll(dy * s2, QK_ROPE // 2, 1)


def _rope_tables(seq):
    half = QK_ROPE // 2
    inv_freq = ROPE_THETA ** (-jnp.arange(0, QK_ROPE, 2, dtype=F32) / QK_ROPE)
    ang = jnp.arange(seq, dtype=F32)[:, None] * inv_freq[None, :]
    cos, sin = jnp.cos(ang), jnp.sin(ang)
    z = lambda n: jnp.zeros((seq, n), F32)
    tail = LANES - QK_HEAD
    c = jnp.concatenate([jnp.ones((seq, QK_NOPE), F32), cos, cos, jnp.ones((seq, tail), F32)], axis=1)
    s1 = jnp.concatenate([z(QK_NOPE + half), sin, z(tail)], axis=1)
    s2 = jnp.concatenate([z(QK_NOPE), -sin, z(half + tail)], axis=1)
    return c, s1, s2


def _row(tm, w):
    return pl.BlockSpec((tm, w), lambda i: (i, 0))


def _modspec(d, tps):
    return pl.BlockSpec((None, N_MOD, d), lambda i: (i // tps, 0, 0))


def _seqv(w, tps):
    return pl.BlockSpec((None, 1, w), lambda i: (i // tps, 0, 0))


def _full(shape):
    return pl.BlockSpec(shape, lambda i: tuple(0 for _ in shape))


def _sds(shape, dtype):
    return jax.ShapeDtypeStruct(shape, dtype)


CONV_ROWS = 64
CONV_LC = CONV_CH // LANES


def _lane_chunks():
    return [(lc, slice(lc * LANES, (lc + 1) * LANES)) for lc in range(CONV_LC)]


def _fill_shifted(ext_ref, head, body):
    nh = head.shape[0]
    for lc, ls in _lane_chunks():
        ext_ref[0, lc, :nh, :] = head[:, ls]
        ext_ref[0, lc, nh:, :] = body[:, ls]
        rows = ext_ref[0, lc]
        for b in range(1, SUBLANES):
            ext_ref[b, lc] = pltpu.roll(rows, rows.shape[0] - b, 0)


def _shifted_shape(tm):
    return (SUBLANES, CONV_LC, tm + HALO, LANES)


def _conv_chunk(c):
    return c % CONV_LC, pl.multiple_of((c // CONV_LC) * CONV_ROWS, CONV_ROWS)


def _shifted(ext_ref, o, lc, r0):
    a = pl.multiple_of((o // SUBLANES) * SUBLANES + r0, SUBLANES)
    return ext_ref[o % SUBLANES, lc, pl.ds(a, CONV_ROWS), :]


def _by_lane_chunk(a):
    return a.reshape(a.shape[0], CONV_LC, LANES).transpose(1, 0, 2)


def _load_resident(i, pairs):
    @pl.when(i == 0)
    def _():
        for src, dst in pairs:
            pltpu.sync_copy(src, dst)


def _place():
    return lax.axis_index("x"), lax.axis_index("y"), lax.axis_index("c")


def _all_gather8(blocks, name):
    na = len(blocks)

    def body(*refs):
        start, forward, finish = _gather8_phases(refs[:na], refs[na:2 * na], *refs[2 * na:])
        start()
        forward()
        finish()

    outs = pl.pallas_call(
        body, name=name, out_shape=_gather8_shapes(blocks), in_specs=[ANY] * na, out_specs=(ANY,) * na,
        scratch_shapes=_gather8_sems(na),
    )(*blocks)
    return _own_block_placed(outs, blocks)


def _gather8_shapes(blocks):
    return tuple(_sds((8,) + b.shape, b.dtype) for b in blocks)


def _gather8_sems(na):
    return [pltpu.SemaphoreType.DMA((7 * na,)), pltpu.SemaphoreType.DMA((7 * na,))]


def _own_block_placed(outs, blocks):
    ix, iy, ic = _place()
    return tuple(lax.dynamic_update_index_in_dim(o, b, 4 * ix + 2 * iy + ic, 0) for o, b in zip(outs, blocks))


def _gather8_phases(x_refs, out_refs, send_sems, recv_sems):
    na = len(x_refs)
    x, y, c = _place()
    me, sibling = (x, y, c), (x, y, 1 - c)
    chips = [(1 - x, y), (x, 1 - y), (1 - x, 1 - y)]

    def copy(a, k, blk, to, from_input=False):
        dst = out_refs[a].at[4 * blk[0] + 2 * blk[1] + blk[2]]
        return pltpu.make_async_remote_copy(
            src_ref=x_refs[a] if from_input else dst, dst_ref=dst,
            send_sem=send_sems.at[7 * a + k], recv_sem=recv_sems.at[7 * a + k], device_id=to, device_id_type=MESH)

    def first(a):
        return [copy(a, 0, me, sibling, True)] + [copy(a, 1 + j, me, (*chip, c), True) for j, chip in enumerate(chips)]

    def start():
        for a in range(na):
            for cp in first(a):
                cp.start()

    def forward():
        for j, chip in enumerate(chips):
            for a in range(na):
                copy(a, 1 + j, (*chip, c), me).wait_recv()
                copy(a, 4 + j, (*chip, c), sibling).start()

    def finish():
        for a in range(na):
            copy(a, 0, sibling, me).wait_recv()
            for j, chip in enumerate(chips):
                copy(a, 4 + j, (*chip, 1 - c), me).wait_recv()
        for a in range(na):
            for cp in first(a) + [copy(a, 4 + j, (*chip, c), sibling) for j, chip in enumerate(chips)]:
                cp.wait_send()

    return start, forward, finish


def _pair_swap(gs, name):
    na = len(gs)

    def body(*refs):
        start, finish = _swap_phases(refs[:na], refs[na:2 * na], *refs[2 * na:])
        start()
        finish()

    return pl.pallas_call(
        body, name=name, out_shape=_swap_shapes(gs), in_specs=[ANY] * na, out_specs=(ANY,) * na,
        scratch_shapes=_swap_sems(gs),
    )(*gs)


def _swap_shapes(gs):
    return tuple(_sds(g.shape[:1] + g.shape[2:], g.dtype) for g in gs)


def _swap_sems(gs):
    n = sum(g.shape[0] for g in gs)
    return [pltpu.SemaphoreType.DMA((n,)), pltpu.SemaphoreType.DMA((n,))]


def _swap_phases(g_refs, land_refs, send_sems, recv_sems):
    x, y, c = _place()

    def copies():
        cps, k = [], 0
        for g_ref, land_ref in zip(g_refs, land_refs):
            for s in range(g_ref.shape[0]):
                cps.append(pltpu.make_async_remote_copy(
                    src_ref=g_ref.at[s, 1 - c], dst_ref=land_ref.at[s], send_sem=send_sems.at[k],
                    recv_sem=recv_sems.at[k], device_id=(x, y, 1 - c), device_id_type=MESH))
                k += 1
        return cps

    def start():
        for cp in copies():
            cp.start()

    def finish():
        for cp in copies():
            cp.wait()

    return start, finish


def _scatter_shapes(hs):
    return tuple(_sds((3,) + h.shape[1:], h.dtype) for h in hs)


def _scatter_sems(na):
    return [pltpu.SemaphoreType.DMA((3 * na,)), pltpu.SemaphoreType.DMA((3 * na,))]


def _scatter_phases(h_refs, land_refs, send_sems, recv_sems):
    x, y, c = _place()
    chips = [(1 - x, y), (x, 1 - y), (1 - x, 1 - y)]

    def copies():
        return [pltpu.make_async_remote_copy(
            src_ref=h_refs[a].at[2 * tx + ty], dst_ref=land_refs[a].at[j], send_sem=send_sems.at[3 * a + j],
            recv_sem=recv_sems.at[3 * a + j], device_id=(tx, ty, c), device_id_type=MESH)
            for a in range(len(h_refs)) for j, (tx, ty) in enumerate(chips)]

    def start():
        for cp in copies():
            cp.start()

    def finish():
        for cp in copies():
            cp.wait()

    return start, finish


def _pair_gather_and_all_gather8(fs, blocks, name):
    nf, nb = len(fs), len(blocks)

    def body(*refs):
        f_refs = refs[nf + nb:2 * nf + nb]
        b_out = refs[2 * nf + nb:2 * nf + 2 * nb]
        send_sems, recv_sems, g_send, g_recv = refs[2 * nf + 2 * nb:]
        x, y, c = _place()
        start, forward, finish = _gather8_phases(refs[nf:nf + nb], b_out, g_send, g_recv)
        sends = [pltpu.make_async_remote_copy(
            src_ref=f_refs[a].at[c], dst_ref=f_refs[a].at[c], send_sem=send_sems.at[a], recv_sem=recv_sems.at[a],
            device_id=(x, y, 1 - c), device_id_type=MESH) for a in range(nf)]
        recvs = [pltpu.make_async_remote_copy(
            src_ref=f_refs[a].at[c], dst_ref=f_refs[a].at[1 - c], send_sem=send_sems.at[a],
            recv_sem=recv_sems.at[a], device_id=(x, y, 1 - c), device_id_type=MESH) for a in range(nf)]
        start()
        for cp in sends:
            cp.start()
        forward()
        finish()
        for cp in recvs:
            cp.wait_recv()
        for cp in sends:
            cp.wait_send()

    res = pl.pallas_call(
        body, name=name, out_shape=tuple(_sds(f.shape, f.dtype) for f in fs) + _gather8_shapes(blocks),
        in_specs=[ANY] * (nf + nb), out_specs=(ANY,) * (nf + nb), input_output_aliases={a: a for a in range(nf)},
        scratch_shapes=[pltpu.SemaphoreType.DMA((nf,)), pltpu.SemaphoreType.DMA((nf,))] + _gather8_sems(nb),
    )(*fs, *blocks)
    return res[:nf], _own_block_placed(res[nf:], blocks)


def _row_tile(r, n, itemsize=4, budget=1 << 21):
    if r * n * itemsize <= budget:
        return r
    best = None
    for tr in range(16, r, 16):
        if r % tr == 0 and tr * n * itemsize <= budget:
            best = tr
    assert best is not None, (r, n)
    return best


def _add_pair(g, land, cidx, name):
    ns, _, r, n = g.shape
    tr = _row_tile(r, n)

    def body(c_ref, a_ref, b_ref, o_ref, ob_ref):
        s = a_ref[...] + b_ref[...]
        o_ref[...] = s
        ob_ref[...] = s.astype(BF16)

    out = pl.BlockSpec((None, tr, n), lambda s, i, cr: (s, i, 0))
    return pl.pallas_call(
        body, name=name, out_shape=(_sds((ns, r, n), F32), _sds((ns, r, n), BF16)),
        grid_spec=pltpu.PrefetchScalarGridSpec(
            num_scalar_prefetch=1, grid=(ns, r // tr),
            in_specs=[pl.BlockSpec((None, None, tr, n), lambda s, i, cr: (s, cr[0], i, 0)), out],
            out_specs=(out, out)),
        compiler_params=_params("arbitrary", "arbitrary"),
    )(cidx, g, land)


def _add_pair_whole(gs, lands, cidx, name):
    k = len(gs)

    def body(c_ref, *refs):
        for a_ref, b_ref, o_ref, ob_ref in zip(refs[:k], refs[k:2 * k], refs[2 * k:3 * k], refs[3 * k:]):
            s = a_ref[...] + b_ref[...]
            o_ref[...] = s
            ob_ref[...] = s.astype(BF16)

    half = lambda g: pl.BlockSpec((g.shape[0], None) + g.shape[2:], lambda i, cr: (0, cr[0], 0, 0))
    whole = lambda g: pl.BlockSpec(g.shape[:1] + g.shape[2:], lambda i, cr: (0, 0, 0))
    shapes = lambda dt: tuple(_sds(g.shape[:1] + g.shape[2:], dt) for g in gs)
    res = pl.pallas_call(
        body, name=name, out_shape=shapes(F32) + shapes(BF16),
        grid_spec=pltpu.PrefetchScalarGridSpec(
            num_scalar_prefetch=1, grid=(1,),
            in_specs=[half(g) for g in gs] + [whole(g) for g in gs],
            out_specs=tuple(whole(g) for g in gs) * 2),
        compiler_params=_params("arbitrary"),
    )(cidx, *gs, *lands)
    return list(zip(res[:k], res[k:]))


def _add_chips_whole(hs, lands, own_c, name):
    k = len(hs)

    def body(o_idx, *refs):
        for h_ref, l_ref, o_ref in zip(refs[:k], refs[k:2 * k], refs[2 * k:]):
            o_ref[...] = ((h_ref[...] + l_ref[0].astype(F32)) + l_ref[1].astype(F32)) + l_ref[2].astype(F32)

    return pl.pallas_call(
        body, name=name, out_shape=tuple(_sds((2,) + h.shape[1:], F32) for h in hs),
        grid_spec=pltpu.PrefetchScalarGridSpec(
            num_scalar_prefetch=1, grid=(1,),
            in_specs=[pl.BlockSpec((None,) + h.shape[1:], lambda i, o: (o[0], 0, 0)) for h in hs]
            + [pl.BlockSpec(l.shape, lambda i, o: (0, 0, 0)) for l in lands],
            out_specs=tuple(pl.BlockSpec((None,) + h.shape[1:], lambda i, o: (o[1], 0, 0)) for h in hs)),
        compiler_params=_params("arbitrary"),
    )(own_c, *hs, *lands)


def _add_chips(h, land, own_c, name):
    _, r, n = h.shape
    tr = _row_tile(r, n)

    def body(o_idx, h_ref, l_ref, o_ref):
        o_ref[...] = ((h_ref[...] + l_ref[0].astype(F32)) + l_ref[1].astype(F32)) + l_ref[2].astype(F32)

    return pl.pallas_call(
        body, name=name, out_shape=_sds((2, r, n), F32),
        grid_spec=pltpu.PrefetchScalarGridSpec(
            num_scalar_prefetch=1, grid=(r // tr,),
            in_specs=[pl.BlockSpec((None, tr, n), lambda i, o: (o[0], i, 0)),
                      pl.BlockSpec((3, tr, n), lambda i, o: (0, i, 0))],
            out_specs=pl.BlockSpec((None, tr, n), lambda i, o: (o[1], i, 0))),
        compiler_params=_params("arbitrary"),
    )(own_c, h, land)


def _adam_math(w, g, m, v):
    nm = ADAM_B1 * m + (1.0 - ADAM_B1) * g
    nv = ADAM_B2 * v + (1.0 - ADAM_B2) * (g * g)
    m_hat = nm / (1.0 - ADAM_B1 ** ADAM_STEP)
    v_hat = nv / (1.0 - ADAM_B2 ** ADAM_STEP)
    return -ADAM_LR * (m_hat / (jnp.sqrt(v_hat) + ADAM_EPS) + ADAM_WD * w), nm, nv


def _adamw(w, g, m, v, name):
    r, n = w.shape
    tr = _row_tile(r, n)

    def body(w_ref, g_ref, m_ref, v_ref, d_ref, nm_ref, nv_ref):
        d_ref[...], nm_ref[...], nv_ref[...] = _adam_math(w_ref[...], g_ref[...], m_ref[...], v_ref[...])

    spec = pl.BlockSpec((tr, n), lambda i: (i, 0))
    return pl.pallas_call(
        body, name=name, out_shape=(_sds((r, n), F32),) * 3, grid=(r // tr,),
        in_specs=[spec] * 4, out_specs=(spec,) * 3, compiler_params=_params("arbitrary"),
    )(w, g, m, v)


def _adamw_small(ws, gs, ms, vs):
    k = len(ws)

    def body(*refs):
        ins, outs = refs[:4 * k], refs[4 * k:]
        for j in range(k):
            d, nm, nv = _adam_math(ins[j][...], ins[k + j][...], ins[2 * k + j][...], ins[3 * k + j][...])
            outs[j][...] = d
            outs[k + j][...] = nm
            outs[2 * k + j][...] = nv

    shapes = tuple(_sds(w.shape, F32) for w in ws)
    res = pl.pallas_call(body, name="adamw_small", out_shape=shapes * 3,
                         compiler_params=pltpu.CompilerParams(vmem_limit_bytes=VMEM_LIMIT))(*ws, *gs, *ms, *vs)
    return res[:k], res[k:2 * k], res[2 * k:]


def _ada_mod(c_all, w_sh, b_sh):
    b, _ = c_all.shape
    n = w_sh.shape[1]

    def body(c_ref, w_ref, b_ref, o_ref):
        cc = c_ref[...]
        ca = (cc * jax.nn.sigmoid(cc)).astype(BF16)
        o_ref[...] = _dot(ca, w_ref[...].astype(BF16)) + b_ref[...]

    return pl.pallas_call(body, name="ada_mod", out_shape=_sds((b, n), F32),
                          compiler_params=pltpu.CompilerParams(vmem_limit_bytes=VMEM_LIMIT))(c_all, w_sh, b_sh)


def _ada_bwd(c_all, dmod_all, dmod_sh, parts):
    b, d = c_all.shape
    n6 = dmod_all.shape[1]
    n = dmod_sh.shape[1]
    k = len(parts)

    def body(*refs):
        c_ref, da_ref, ds_ref = refs[:3]
        p_refs = refs[3:3 + k]
        dw_ref, db_ref = refs[3 + k:5 + k]
        s_refs = refs[5 + k:]
        cc = c_ref[...]
        ca = (cc * jax.nn.sigmoid(cc)).astype(BF16)
        dw_ref[...] = _dot_tn(ca, ds_ref[...].astype(BF16))
        db_ref[...] = jnp.sum(da_ref[...], axis=0, keepdims=True)
        for p_ref, s_ref in zip(p_refs, s_refs):
            tot = p_ref[0]
            for j in range(1, p_ref.shape[0]):
                tot = tot + p_ref[j]
            s_ref[...] = tot

    return pl.pallas_call(
        body, name="ada_bwd",
        out_shape=(_sds((d, n), F32), _sds((1, n6), F32)) + tuple(_sds(p.shape[1:], F32) for p in parts),
        compiler_params=pltpu.CompilerParams(vmem_limit_bytes=VMEM_LIMIT),
    )(c_all, dmod_all, dmod_sh, *parts)


def _fwd_in(x, g1, mod3, win_p, tm, tps):
    t, d = x.shape
    p_glu, p_q, npad = _layout(d)

    def body(x_ref, g_ref, mod_ref, w_hbm, h_ref, zm_ref, zglu_ref, zgate_ref, u0_ref, w_ref):
        _load_resident(pl.program_id(0), [(w_hbm, w_ref)])
        n, _ = _rms(x_ref[...])
        h = ((n * g_ref[...]) * (1.0 + mod_ref[1:2, :]) + mod_ref[0:1, :]).astype(BF16)
        h_ref[...] = h
        z = _dot(h, w_ref[...])
        zgate_ref[...] = z[:, :p_glu]
        zglu = z[:, p_glu:p_q]
        zglu_ref[...] = zglu
        zm_ref[...] = z[:, p_q:]
        u0_ref[...] = zglu[:, :CONV_CH] * jax.nn.sigmoid(zglu[:, CONV_CH:])

    return pl.pallas_call(
        body, name="fwd_in", grid=(t // tm,),
        out_shape=(_sds((t, d), BF16), _sds((t, MLA_IN), F32), _sds((t, 2 * CONV_CH), F32), _sds((t, 2 * d), F32),
                   _sds((t, CONV_CH), F32)),
        in_specs=[_row(tm, d), _full((1, d)), _modspec(d, tps), ANY],
        out_specs=(_row(tm, d), _row(tm, MLA_IN), _row(tm, 2 * CONV_CH), _row(tm, 2 * d), _row(tm, CONV_CH)),
        scratch_shapes=[pltpu.VMEM(win_p.shape, BF16)],
        compiler_params=_params("arbitrary"),
    )(x, g1, mod3, win_p)


def _mla_prep(zm, gql, gkvl, gq, gk, tabs, wuq_p, wk_p, wv_p, tm, tps):
    t = zm.shape[0]
    c_t, s1_t, s2_t = tabs
    tab = pl.BlockSpec((tm, LANES), lambda i: (i % tps, 0))

    def body(zm_ref, gql_ref, gkvl_ref, gq_ref, gk_ref, c_ref, s1_ref, s2_ref, wuq_ref, wk_ref, wv_ref,
             q_ref, k_ref, v_ref, qln_ref, kvn_ref):
        c, s1, s2 = c_ref[...], s1_ref[...], s2_ref[...]
        nq, _ = _rms(zm_ref[:, :Q_RANK])
        qln = (nq * gql_ref[...]).astype(BF16)
        qln_ref[...] = qln
        qpre = _dot(qln, wuq_ref[...])
        nkv, _ = _rms(zm_ref[:, Q_RANK:OFF_KV])
        kvn = (nkv * gkvl_ref[...]).astype(BF16)
        kvn_ref[...] = kvn
        knope = _dot(kvn, wk_ref[...])
        v_ref[...] = _dot(kvn, wv_ref[...]).astype(BF16)
        zkr_v = zm_ref[:, OFF_KV:]
        kr_roped = _rope(zkr_v * gk_ref[...], c, s1, s2)
        slabs = [slice(hd * LANES, (hd + 1) * LANES) for hd in range(N_HEADS)]
        rq = [_head_rms(qpre[:, sl])[1] for sl in slabs]
        rk = [_head_rms(knope[:, sl] + zkr_v)[1] for sl in slabs]
        for hd, sl in enumerate(slabs):
            q_ref[:, sl] = _rope((qpre[:, sl] * rq[hd]) * gq_ref[...], c, s1, s2).astype(BF16)
            k_ref[:, sl] = (rk[hd] * (knope[:, sl] * gk_ref[...] + kr_roped)).astype(BF16)

    return pl.pallas_call(
        body, name="mla_prep", grid=(t // tm,),
        out_shape=(_sds((t, HW), BF16),) * 3 + (_sds((t, Q_RANK), BF16), _sds((t, KV_RANK), BF16)),
        in_specs=[_row(tm, MLA_IN), _full((1, Q_RANK)), _full((1, KV_RANK)),
                  _full((1, LANES)), _full((1, LANES)), tab, tab, tab,
                  _full(wuq_p.shape), _full(wk_p.shape), _full(wv_p.shape)],
        out_specs=(_row(tm, HW),) * 3 + (_row(tm, Q_RANK), _row(tm, KV_RANK)),
        compiler_params=_params("arbitrary"),
    )(zm, gql, gkvl, gq, gk, c_t, s1_t, s2_t, wuq_p, wk_p, wv_p)


AHEAD = 2
ROW_BAND = 256
SM_SCALE = QK_HEAD ** -0.5
EXP2_SCALE = SM_SCALE * 1.4426950408889634


def _diag_mask():
    rc = jnp.right_shift(lax.broadcasted_iota(jnp.int32, (BQ, 1), 0), CHUNK_SHIFT)
    cc = jnp.right_shift(lax.broadcasted_iota(jnp.int32, (1, BQ), 1), CHUNK_SHIFT)
    return rc >= cc


def _scores(q_i, k_ref, lo, e):
    return (_dot_nt(q_i, k_ref[:lo, :]) if lo else None), _dot_nt(q_i, k_ref[lo:e, :])


def _softmax_parts(scores, mask):
    sp, sd = scores
    sd = jnp.where(mask, sd, jnp.finfo(F32).min)
    m = jnp.max(sd, axis=-1, keepdims=True)
    if sp is not None:
        m = jnp.maximum(m, jnp.max(sp, axis=-1, keepdims=True))
    pd = jnp.exp2((sd - m) * EXP2_SCALE)
    l = jnp.sum(pd, axis=-1, keepdims=True)
    pp = None
    if sp is not None:
        pp = jnp.exp2((sp - m) * EXP2_SCALE)
        l = l + jnp.sum(pp, axis=-1, keepdims=True)
    return pp, pd, l


def _attn_fwd(q, k, v, nseq, seq, gather=()):
    t = q.shape[0]
    na = len(gather)
    blk = pl.BlockSpec((seq, LANES), lambda b, h: (b, h))
    n_steps = nseq * N_HEADS

    def body(q_ref, k_ref, v_ref, *rest):
        o_ref = rest[na]
        if na:
            start, forward, finish = _gather8_phases(rest[:na], rest[na + 1:2 * na + 1], *rest[2 * na + 1:])
            step = pl.program_id(0) * N_HEADS + pl.program_id(1)
            pl.when(step == 0)(start)
            pl.when(step == (7 * n_steps) // 8)(forward)
        mask = _diag_mask()
        nb = seq // BQ
        block_scores = lambda j: _scores(q_ref[j * BQ:(j + 1) * BQ, :], k_ref, j * BQ, (j + 1) * BQ)
        ahead = [block_scores(j) for j in range(min(AHEAD, nb))]
        for i in range(nb):
            lo, e = i * BQ, (i + 1) * BQ
            cur = ahead.pop(0)
            if i + AHEAD < nb:
                ahead.append(block_scores(i + AHEAD))
            pp, pd, l = _softmax_parts(cur, mask)
            o = _dot(pd.astype(BF16), v_ref[lo:e, :])
            if lo:
                o = o + _dot(pp.astype(BF16), v_ref[:lo, :])
            o_ref[lo:e, :] = (o * (1.0 / l)).astype(BF16)
        if na:
            pl.when(step == n_steps - 1)(finish)

    res = pl.pallas_call(
        body, name="attn_fwd", grid=(nseq, N_HEADS), out_shape=(_sds((t, HW), BF16),) + _gather8_shapes(gather),
        in_specs=[blk, blk, blk] + [ANY] * na, out_specs=(blk,) + (ANY,) * na,
        scratch_shapes=_gather8_sems(na) if na else [],
        compiler_params=_params("arbitrary", "arbitrary"),
    )(q, k, v, *gather)
    return res[0], (_own_block_placed(res[1:], gather) if na else ())


def _fwd_mix(attn, u0, zgate, x, mod3, wo_p, cw, cb, lng, lnb, wpw, wout, tm, tps):
    t, d = x.shape
    hpt = tm // HALO
    cwc, cbc = _by_lane_chunk(cw), _by_lane_chunk(cb)

    def body(a_ref, u_ref, uh_ref, zg_ref, x_ref, mod_ref, wo_ref, cw_ref, cb_ref, lng_ref, lnb_ref, wpw_ref, wout_ref,
             x1_ref, mixed_ref, mpre_ref, ya_ref, yb_ref, u1_ref, u3_ref, ext_ref):
        i = pl.program_id(0)
        ya = _dot(a_ref[...], wo_ref[...])
        ya_ref[...] = ya
        first = (i % tps) == 0
        _fill_shifted(ext_ref, jnp.where(first, 0.0, uh_ref[...]), u_ref[...])
        for lc, ls in _lane_chunks():
            acc = jnp.broadcast_to(cb_ref[lc], (tm, LANES))
            for kk in range(CONV_W):
                o = HALO - (CONV_W - 1) + kk
                a = (o // SUBLANES) * SUBLANES
                acc = acc + cw_ref[lc, kk:kk + 1, :] * ext_ref[o % SUBLANES, lc, a:a + tm, :]
            u1_ref[:, ls] = acc
        acc = u1_ref[...]
        mu = jnp.mean(acc, axis=-1, keepdims=True)
        xc = acc - mu
        rstd = lax.rsqrt(jnp.mean(xc * xc, axis=-1, keepdims=True) + EPS)
        l = (xc * rstd) * lng_ref[...] + lnb_ref[...]
        u3 = (l * jax.nn.sigmoid(l)).astype(BF16)
        u3_ref[...] = u3
        yb = _dot(u3, wpw_ref[...])
        yb_ref[...] = yb
        zg = zg_ref[...]
        mpre = (jax.nn.sigmoid(zg[:, :d]) * ya + jax.nn.sigmoid(zg[:, d:]) * yb).astype(BF16)
        mpre_ref[...] = mpre
        mixed = _dot(mpre, wout_ref[...])
        mixed_ref[...] = mixed
        x1_ref[...] = x_ref[...] + mod_ref[2:3, :] * mixed

    halo = pl.BlockSpec((HALO, CONV_CH), lambda i: (jnp.maximum(i * hpt - 1, 0), 0))
    return pl.pallas_call(
        body, name="fwd_mix", grid=(t // tm,),
        out_shape=(_sds((t, d), F32), _sds((t, d), F32), _sds((t, d), BF16), _sds((t, d), F32), _sds((t, d), F32),
                   _sds((t, CONV_CH), F32), _sds((t, CONV_CH), BF16)),
        in_specs=[_row(tm, HW), _row(tm, CONV_CH), halo, _row(tm, 2 * d), _row(tm, d), _modspec(d, tps),
                  _full(wo_p.shape), _full(cwc.shape), _full(cbc.shape), _full((1, CONV_CH)), _full((1, CONV_CH)),
                  _full(wpw.shape), _full(wout.shape)],
        out_specs=(_row(tm, d), _row(tm, d), _row(tm, d), _row(tm, d), _row(tm, d), _row(tm, CONV_CH),
                   _row(tm, CONV_CH)),
        scratch_shapes=[pltpu.VMEM(_shifted_shape(tm), F32)],
        compiler_params=_params("arbitrary"),
    )(attn, u0, u0, zgate, x, mod3, wo_p, cwc, cbc, lng, lnb, wpw, wout)


def _shards_into_columns(w_hbm, w_ref):
    ns = w_hbm.shape[2]
    return [(w_hbm.at[s], w_ref.at[:, pl.ds(s * ns, ns)]) for s in range(w_hbm.shape[0])]


def _fwd_ffn(x1, target, g2, mod3, w1, w2, tm, tps):
    t, d = x1.shape
    dff = w1.shape[0] * w1.shape[2]

    def body(x1_ref, tg_ref, g_ref, mod_ref, w1_hbm, w2_hbm,
             h2_ref, a_ref, r_ref, dy_ref, df_ref, dgate_ref, loss_ref, w1_ref, w2_ref):
        i = pl.program_id(0)
        _load_resident(i, _shards_into_columns(w1_hbm, w1_ref) + [(w2_hbm, w2_ref)])
        x1v = x1_ref[...]
        gate2 = mod_ref[5:6, :]
        n, _ = _rms(x1v)
        h2 = ((n * g_ref[...]) * (1.0 + mod_ref[4:5, :]) + mod_ref[3:4, :]).astype(BF16)
        h2_ref[...] = h2
        a = _dot(h2, w1_ref[...])
        a_ref[...] = a
        r = jnp.square(jnp.maximum(a, 0.0)).astype(BF16)
        r_ref[...] = r
        f = _dot(r, w2_ref[...])
        e = (x1v + gate2 * f) - tg_ref[...]
        part = 0.5 * jnp.sum(jnp.mean(e * e, axis=-1, keepdims=True), axis=0, keepdims=True)
        _acc(loss_ref, jnp.broadcast_to(part, loss_ref.shape), i == 0)
        dy = e * (1.0 / d)
        dy_ref[...] = dy
        df_ref[...] = (dy * gate2).astype(BF16)
        _acc(dgate_ref, jnp.sum(dy * f, axis=0, keepdims=True), (i % tps) == 0)

    nseq = t // (tm * tps)
    return pl.pallas_call(
        body, name="fwd_ffn", grid=(t // tm,),
        out_shape=(_sds((t, d), BF16), _sds((t, dff), F32), _sds((t, dff), BF16), _sds((t, d), F32), _sds((t, d), BF16),
                   _sds((nseq, 1, d), F32), _sds((8, LANES), F32)),
        in_specs=[_row(tm, d), _row(tm, d), _full((1, d)), _modspec(d, tps), ANY, ANY],
        out_specs=(_row(tm, d), _row(tm, dff), _row(tm, dff), _row(tm, d), _row(tm, d), _seqv(d, tps),
                   _full((8, LANES))),
        scratch_shapes=[pltpu.VMEM((d, dff), BF16), pltpu.VMEM(w2.shape, BF16)],
        compiler_params=_params("arbitrary"),
    )(x1, target, g2, mod3, w1, w2)


def _bwd_ffn(df, a, x1, dy, mixed, g2, mod3, w2, w1, tm, tps):
    t, d = x1.shape
    dff = a.shape[1]

    def body(df_ref, a_ref, x1_ref, dy_ref, mx_ref, g_ref, mod_ref, w2_hbm, w1_hbm,
             da_ref, dx1_ref, dmixed_ref, dshift_ref, dscale_ref, dgate1_ref, dg2_ref, w2_ref, w1_ref):
        i = pl.program_id(0)
        _load_resident(i, [(w2_hbm, w2_ref)] + _shards_into_columns(w1_hbm, w1_ref))
        first_seq = (i % tps) == 0
        dr = _dot_nt(df_ref[...], w2_ref[...])
        da = (dr * (2.0 * jnp.maximum(a_ref[...], 0.0))).astype(BF16)
        da_ref[...] = da
        dh2 = _dot_nt(da, w1_ref[...])
        n, r = _rms(x1_ref[...])
        g = g_ref[...]
        sc1 = 1.0 + mod_ref[4:5, :]
        _acc(dshift_ref, jnp.sum(dh2, axis=0, keepdims=True), first_seq)
        _acc(dscale_ref, jnp.sum(dh2 * (n * g), axis=0, keepdims=True), first_seq)
        _acc(dg2_ref, jnp.sum((dh2 * sc1) * n, axis=0, keepdims=True), i == 0)
        dx1 = dy_ref[...] + _rms_bwd(n, r, (dh2 * sc1) * g)
        dx1_ref[...] = dx1
        _acc(dgate1_ref, jnp.sum(dx1 * mx_ref[...], axis=0, keepdims=True), first_seq)
        dmixed_ref[...] = (dx1 * mod_ref[2:3, :]).astype(BF16)

    nseq = t // (tm * tps)
    sv = _sds((nseq, 1, d), F32)
    return pl.pallas_call(
        body, name="bwd_ffn", grid=(t // tm,),
        out_shape=(_sds((t, dff), BF16), _sds((t, d), F32), _sds((t, d), BF16), sv, sv, sv, _sds((1, d), F32)),
        in_specs=[_row(tm, d), _row(tm, dff), _row(tm, d), _row(tm, d), _row(tm, d), _full((1, d)), _modspec(d, tps),
                  ANY, ANY],
        out_specs=(_row(tm, dff), _row(tm, d), _row(tm, d), _seqv(d, tps), _seqv(d, tps), _seqv(d, tps),
                   _full((1, d))),
        scratch_shapes=[pltpu.VMEM(w2.shape, BF16), pltpu.VMEM((d, dff), BF16)],
        compiler_params=_params("arbitrary"),
    )(df, a, x1, dy, mixed, g2, mod3, w2, w1)


def _bwd_mix(dmixed, zgate, ya, yb, u1, lng, lnb, wout, wo_p, wpw, tm, swap=()):
    t, d = ya.shape
    _, _, npad = _layout(d)
    nw = len(swap)
    n_steps = t // tm

    def body(dm_ref, zg_ref, ya_ref, yb_ref, u1_ref, lng_ref, lnb_ref, wout_ref, wo_ref, wpw_ref, *rest):
        dya_ref, dyb_ref, dz_ref, do_ref, du1_ref, dlng_ref, dlnb_ref, dcb_ref = rest[nw:nw + 8]
        i = pl.program_id(0)
        if nw:
            start, finish = _swap_phases(rest[:nw], rest[nw + 8:2 * nw + 8], *rest[2 * nw + 8:])
            pl.when(i == 0)(start)
        dmpre = _dot_nt(dm_ref[...], wout_ref[...])
        zg = zg_ref[...]
        ga = jax.nn.sigmoid(zg[:, :d])
        gb = jax.nn.sigmoid(zg[:, d:])
        dya = (dmpre * ga).astype(BF16)
        dyb = (dmpre * gb).astype(BF16)
        dya_ref[...] = dya
        dyb_ref[...] = dyb
        dz_ref[:, :d] = ((dmpre * ya_ref[...]) * (ga * (1.0 - ga))).astype(BF16)
        dz_ref[:, d:] = ((dmpre * yb_ref[...]) * (gb * (1.0 - gb))).astype(BF16)
        do_ref[...] = _dot_nt(dya, wo_ref[...]).astype(BF16)
        du3 = _dot_nt(dyb, wpw_ref[...])
        u1 = u1_ref[...]
        mu = jnp.mean(u1, axis=-1, keepdims=True)
        xc = u1 - mu
        rstd = lax.rsqrt(jnp.mean(xc * xc, axis=-1, keepdims=True) + EPS)
        nh = xc * rstd
        l = nh * lng_ref[...] + lnb_ref[...]
        sg = jax.nn.sigmoid(l)
        dl = du3 * (sg * (1.0 + l * (1.0 - sg)))
        _acc(dlng_ref, jnp.sum(dl * nh, axis=0, keepdims=True), i == 0)
        _acc(dlnb_ref, jnp.sum(dl, axis=0, keepdims=True), i == 0)
        dnh = dl * lng_ref[...]
        du1 = rstd * (dnh - jnp.mean(dnh, axis=-1, keepdims=True) - nh * jnp.mean(dnh * nh, axis=-1, keepdims=True))
        du1_ref[...] = du1
        _acc(dcb_ref, jnp.sum(du1, axis=0, keepdims=True), i == 0)
        if nw:
            pl.when(i == n_steps - 1)(finish)

    cv = _sds((1, CONV_CH), F32)
    res = pl.pallas_call(
        body, name="bwd_mix", grid=(n_steps,),
        out_shape=(_sds((t, d), BF16), _sds((t, d), BF16), _sds((t, npad), BF16), _sds((t, HW), BF16),
                   _sds((t, CONV_CH), F32), cv, cv, cv) + _swap_shapes(swap),
        in_specs=[_row(tm, d), _row(tm, 2 * d), _row(tm, d), _row(tm, d), _row(tm, CONV_CH), _full((1, CONV_CH)),
                  _full((1, CONV_CH)), _full(wout.shape), _full(wo_p.shape), _full(wpw.shape)] + [ANY] * nw,
        out_specs=(_row(tm, d), _row(tm, d), _row(tm, 2 * d), _row(tm, HW), _row(tm, CONV_CH),
                   _full((1, CONV_CH)), _full((1, CONV_CH)), _full((1, CONV_CH))) + (ANY,) * nw,
        scratch_shapes=_swap_sems(swap) if nw else [],
        compiler_params=_params("arbitrary"),
    )(dmixed, zgate, ya, yb, u1, lng, lnb, wout, wo_p, wpw, *swap)
    return res[:8] + (res[8:],)


def _bwd_conv(dz, du1, u0, zglu, cw, tm, tps):
    t = du1.shape[0]
    d = (dz.shape[1] - MLA_IN - 2 * CONV_CH) // 2
    p_glu, _, _ = _layout(d)
    hpt = tm // HALO
    last_blk = t // HALO - 1
    cwc = _by_lane_chunk(cw)

    def body(dz_hbm, du_ref, dun_ref, u_ref, zl_ref, cw_ref, dzl_ref, dcw_ref, dext_ref, uc_ref, dcw8_ref, du0_ref):
        i = pl.program_id(0)
        last = (i % tps) == (tps - 1)
        _fill_shifted(dext_ref, du_ref[...], jnp.where(last, 0.0, dun_ref[...]))
        for lc, ls in _lane_chunks():
            uc_ref[lc] = u_ref[:, ls]

        @pl.when(i == 0)
        def _():
            dcw8_ref[...] = jnp.zeros_like(dcw8_ref)

        groups = CONV_ROWS // SUBLANES

        def conv_chunk(c, carry):
            lc, r0 = _conv_chunk(c)
            u = uc_ref[lc, pl.ds(r0, CONV_ROWS), :]
            du0 = jnp.zeros((CONV_ROWS, LANES), F32)
            for kk in range(CONV_W):
                win = _shifted(dext_ref, CONV_W - 1 - kk, lc, r0)
                prod = u * win
                part = prod[:SUBLANES]
                for g in range(1, groups):
                    part = part + prod[g * SUBLANES:(g + 1) * SUBLANES]
                dcw8_ref[lc, kk] += part
                du0 = du0 + cw_ref[lc, kk:kk + 1, :] * win
            du0_ref[lc, pl.ds(r0, CONV_ROWS), :] = du0
            return carry

        lax.fori_loop(0, CONV_LC * (tm // CONV_ROWS), conv_chunk, 0)

        @pl.when(i == pl.num_programs(0) - 1)
        def _():
            for lc, ls in _lane_chunks():
                dcw_ref[:, ls] = jnp.sum(dcw8_ref[lc], axis=1)

        for lc, ls in _lane_chunks():
            du0 = du0_ref[lc]
            ga = zl_ref[:, ls]
            sb = jax.nn.sigmoid(zl_ref[:, CONV_CH + lc * LANES:CONV_CH + (lc + 1) * LANES])
            dzl_ref[:, ls] = (du0 * sb).astype(BF16)
            dzl_ref[:, CONV_CH + lc * LANES:CONV_CH + (lc + 1) * LANES] = ((du0 * ga) * (sb * (1.0 - sb))).astype(BF16)

    nxt = pl.BlockSpec((HALO, CONV_CH), lambda i: (jnp.minimum((i + 1) * hpt, last_blk), 0))
    glu_blk = p_glu // (2 * CONV_CH)
    return pl.pallas_call(
        body, name="bwd_conv", grid=(t // tm,),
        out_shape=(_sds(dz.shape, BF16), _sds(cw.shape, F32)),
        in_specs=[ANY, _row(tm, CONV_CH), nxt, _row(tm, CONV_CH), _row(tm, 2 * CONV_CH), _full(cwc.shape)],
        out_specs=(pl.BlockSpec((tm, 2 * CONV_CH), lambda i: (i, glu_blk)), _full(cw.shape)),
        scratch_shapes=[pltpu.VMEM(_shifted_shape(tm), F32), pltpu.VMEM((CONV_LC, tm, LANES), F32),
                        pltpu.VMEM((CONV_LC, HALO, SUBLANES, LANES), F32), pltpu.VMEM((CONV_LC, tm, LANES), F32)],
        input_output_aliases={0: 0},
        compiler_params=_params("arbitrary"),
    )(dz, du1, du1, u0, zglu, cwc)


def _attn_bwd(q, k, v, do, nseq, seq, scatter=()):
    t = q.shape[0]
    ns = len(scatter)
    blk = pl.BlockSpec((seq, LANES), lambda b, h: (b, h))
    n_steps = nseq * N_HEADS

    def body(q_ref, k_ref, v_ref, do_ref, *rest):
        dq_ref, dk_ref, dv_ref = rest[ns:ns + 3]
        dka_ref, dva_ref = rest[2 * ns + 3:2 * ns + 5]
        if ns:
            start, finish = _scatter_phases(rest[:ns], rest[ns + 3:2 * ns + 3], *rest[2 * ns + 5:])
            step = pl.program_id(0) * N_HEADS + pl.program_id(1)
            pl.when(step == 0)(start)
        dka_ref[...] = jnp.zeros_like(dka_ref)
        dva_ref[...] = jnp.zeros_like(dva_ref)
        mask = _diag_mask()
        nb = seq // BQ
        block = lambda j: (_scores(q_ref[j * BQ:(j + 1) * BQ, :], k_ref, j * BQ, (j + 1) * BQ),
                           _scores(do_ref[j * BQ:(j + 1) * BQ, :], v_ref, j * BQ, (j + 1) * BQ))
        ahead = [block(j) for j in range(min(AHEAD, nb))]
        for i in range(nb):
            lo, e = i * BQ, (i + 1) * BQ
            q_i = q_ref[lo:e, :]
            do_i = do_ref[lo:e, :]
            scores, (dpp, dpd) = ahead.pop(0)
            if i + AHEAD < nb:
                ahead.append(block(i + AHEAD))
            pp, pd, l = _softmax_parts(scores, mask)
            inv = 1.0 / l
            pd = pd * inv
            delta = jnp.sum(pd * dpd, axis=-1, keepdims=True)
            if lo:
                pp = pp * inv
                delta = delta + jnp.sum(pp * dpp, axis=-1, keepdims=True)
            dsd = (pd * (dpd - delta)).astype(BF16)
            dq = _dot(dsd, k_ref[lo:e, :])
            dka_ref[lo:e, :] += _dot_tn(dsd, q_i)
            dva_ref[lo:e, :] += _dot_tn(pd.astype(BF16), do_i)
            if lo:
                dsp = (pp * (dpp - delta)).astype(BF16)
                dq = dq + _dot(dsp, k_ref[:lo, :])
                dka_ref[:lo, :] += _dot_tn(dsp, q_i)
                dva_ref[:lo, :] += _dot_tn(pp.astype(BF16), do_i)
            dq_ref[lo:e, :] = dq * SM_SCALE
        dk_ref[...] = dka_ref[...] * SM_SCALE
        dv_ref[...] = dva_ref[...].astype(BF16)
        if ns:
            pl.when(step == n_steps - 1)(finish)

    res = pl.pallas_call(
        body, name="attn_bwd", grid=(nseq, N_HEADS),
        out_shape=(_sds((t, HW), F32), _sds((t, HW), F32), _sds((t, HW), BF16)) + _scatter_shapes(scatter),
        in_specs=[blk] * 4 + [ANY] * ns, out_specs=(blk,) * 3 + (ANY,) * ns,
        scratch_shapes=[pltpu.VMEM((seq, LANES), F32), pltpu.VMEM((seq, LANES), F32)]
        + (_scatter_sems(ns) if ns else []),
        compiler_params=_params("arbitrary", "arbitrary"),
    )(q, k, v, do, *scatter)
    return res[0], res[1], res[2], res[3:]


def _mla_bwd(dz, dq, dk, dv, zm, gql, gkvl, gq, gk, tabs, wuq_p, wk_p, wv_p, tm, tps):
    t = zm.shape[0]
    d = (dz.shape[1] - MLA_IN - 2 * CONV_CH) // 2
    _, p_q, _ = _layout(d)
    c_t, s1_t, s2_t = tabs
    tab = pl.BlockSpec((tm, LANES), lambda i: (i % tps, 0))

    def body(dz_hbm, dq_ref, dk_ref, dv_ref, zm_ref, gql_ref, gkvl_ref, gq_ref, gk_ref, c_ref, s1_ref, s2_ref,
             wuq_ref, wk_ref, wv_ref,
             dzm_ref, dqpre_ref, dkh_ref, dgq_ref, dgk_ref, dgql_ref, dgkvl_ref):
        i = pl.program_id(0)
        c, s1, s2 = c_ref[...], s1_ref[...], s2_ref[...]
        nq, rq = _rms(zm_ref[:, :Q_RANK])
        qpre = _dot((nq * gql_ref[...]).astype(BF16), wuq_ref[...])
        nkv, rkv = _rms(zm_ref[:, Q_RANK:OFF_KV])
        knope = _dot((nkv * gkvl_ref[...]).astype(BF16), wk_ref[...])
        zkr_v = zm_ref[:, OFF_KV:]
        gk = gk_ref[...]
        kr_roped = _rope(zkr_v * gk, c, s1, s2)
        dgq = jnp.zeros((1, LANES), F32)
        dgk = jnp.zeros((1, LANES), F32)
        dzkr = jnp.zeros((tm, LANES), F32)
        dt_sum = jnp.zeros((tm, LANES), F32)
        slabs = [slice(hd * LANES, (hd + 1) * LANES) for hd in range(N_HEADS)]
        gq = gq_ref[...]
        rqh = [_head_rms(qpre[:, sl])[1] for sl in slabs]
        rkh = [_head_rms(knope[:, sl] + zkr_v)[1] for sl in slabs]
        dyr = [_rope_t(dq_ref[:, sl], c, s1, s2) for sl in slabs]
        nqh = [qpre[:, sl] * rqh[hd] for hd, sl in enumerate(slabs)]
        sq = [jnp.sum((dyr[hd] * gq) * nqh[hd], axis=-1, keepdims=True) for hd in range(N_HEADS)]
        dr = [jnp.sum(dk_ref[:, sl] * (knope[:, sl] * gk + kr_roped), axis=-1, keepdims=True) for sl in slabs]
        for hd, sl in enumerate(slabs):
            dgq = dgq + jnp.sum(dyr[hd] * nqh[hd], axis=0, keepdims=True)
            dqpre_ref[:, sl] = (rqh[hd] * (dyr[hd] * gq - nqh[hd] * (sq[hd] * (1.0 / QK_HEAD)))).astype(BF16)
            kn = knope[:, sl]
            r = rkh[hd]
            dt = dk_ref[:, sl] * r
            via_r = (dr[hd] * (r * r * r) * (-1.0 / QK_HEAD)) * (kn + zkr_v)
            dgk = dgk + jnp.sum(dt * kn, axis=0, keepdims=True)
            dt_sum = dt_sum + dt
            dzkr = dzkr + via_r
            dkh_ref[:, sl] = (dt * gk + via_r).astype(BF16)
        de = _rope_t(dt_sum, c, s1, s2)
        dzkr = dzkr + de * gk
        dgk = dgk + jnp.sum(de * zkr_v, axis=0, keepdims=True)
        _acc(dgq_ref, dgq[:, :QK_HEAD], i == 0)
        _acc(dgk_ref, dgk[:, :QK_HEAD], i == 0)
        dzm_ref[:, OFF_KV:] = dzkr.astype(BF16)
        dqln = _dot_nt(dqpre_ref[...], wuq_ref[...])
        _acc(dgql_ref, jnp.sum(dqln * nq, axis=0, keepdims=True), i == 0)
        dzm_ref[:, :Q_RANK] = _rms_bwd(nq, rq, dqln * gql_ref[...]).astype(BF16)
        dkvn = _dot_nt(dkh_ref[...], wk_ref[...]) + _dot_nt(dv_ref[...], wv_ref[...])
        _acc(dgkvl_ref, jnp.sum(dkvn * nkv, axis=0, keepdims=True), i == 0)
        dzm_ref[:, Q_RANK:OFF_KV] = _rms_bwd(nkv, rkv, dkvn * gkvl_ref[...]).astype(BF16)

    return pl.pallas_call(
        body, name="mla_bwd", grid=(t // tm,),
        out_shape=(_sds(dz.shape, BF16), _sds((t, HW), BF16), _sds((t, HW), BF16), _sds((1, QK_HEAD), F32),
                   _sds((1, QK_HEAD), F32), _sds((1, Q_RANK), F32), _sds((1, KV_RANK), F32)),
        in_specs=[ANY, _row(tm, HW), _row(tm, HW), _row(tm, HW), _row(tm, MLA_IN),
                  _full((1, Q_RANK)), _full((1, KV_RANK)), _full((1, LANES)), _full((1, LANES)), tab, tab, tab,
                  _full(wuq_p.shape), _full(wk_p.shape), _full(wv_p.shape)],
        out_specs=(pl.BlockSpec((tm, MLA_IN), lambda i: (i, p_q // MLA_IN)), _row(tm, HW), _row(tm, HW),
                   _full((1, QK_HEAD)), _full((1, QK_HEAD)), _full((1, Q_RANK)), _full((1, KV_RANK))),
        input_output_aliases={0: 0},
        compiler_params=_params("arbitrary"),
    )(dz, dq, dk, dv, zm, gql, gkvl, gq, gk, c_t, s1_t, s2_t, wuq_p, wk_p, wv_p)


def _bwd_in(dz, x, dx1, g1, mod3, win_p, tm, tps, scatter=()):
    t, d = x.shape
    npad = dz.shape[1]

    ns = len(scatter)
    n_steps = t // tm

    def body(dz_ref, x_ref, dx1_ref, g_ref, mod_ref, wt_hbm, *rest):
        gx_ref, dshift_ref, dscale_ref, dg1_ref = rest[ns:ns + 4]
        wt_ref = rest[2 * ns + 4]
        i = pl.program_id(0)
        if ns:
            start, finish = _scatter_phases(rest[:ns], rest[ns + 4:2 * ns + 4], *rest[2 * ns + 5:])
            pl.when(i == 0)(start)
        _load_resident(i, [(wt_hbm, wt_ref)])
        first_seq = (i % tps) == 0
        g = g_ref[...]
        sc1 = 1.0 + mod_ref[1:2, :]
        nb = max(tm // ROW_BAND, 1)
        bands = [slice(b * (tm // nb), (b + 1) * (tm // nb)) for b in range(nb)]
        dhs = [_dot_nt(dz_ref[rows, :], wt_ref[...]) for rows in bands]
        sums = [jnp.zeros((1, d), F32)] * 3
        col = lambda v: jnp.sum(v, axis=0, keepdims=True)
        for rows, dh in zip(bands, dhs):
            n, r = _rms(x_ref[rows, :])
            sums = [sums[0] + col(dh), sums[1] + col(dh * (n * g)), sums[2] + col((dh * sc1) * n)]
            gx_ref[rows, :] = dx1_ref[rows, :] + _rms_bwd(n, r, (dh * sc1) * g)
        _acc(dshift_ref, sums[0], first_seq)
        _acc(dscale_ref, sums[1], first_seq)
        _acc(dg1_ref, sums[2], i == 0)
        if ns:
            pl.when(i == n_steps - 1)(finish)

    nseq = t // (tm * tps)
    sv = _sds((nseq, 1, d), F32)
    res = pl.pallas_call(
        body, name="bwd_in", grid=(n_steps,),
        out_shape=(_sds((t, d), F32), sv, sv, _sds((1, d), F32)) + _scatter_shapes(scatter),
        in_specs=[_row(tm, npad), _row(tm, d), _row(tm, d), _full((1, d)), _modspec(d, tps), ANY] + [ANY] * ns,
        out_specs=(_row(tm, d), _seqv(d, tps), _seqv(d, tps), _full((1, d))) + (ANY,) * ns,
        scratch_shapes=[pltpu.VMEM(win_p.shape, BF16)] + (_scatter_sems(ns) if ns else []),
        compiler_params=_params("arbitrary"),
    )(dz, x, dx1, g1, mod3, win_p, *scatter)
    return res[0], res[1], res[2], res[3], res[4:]


def _tile_of(n, choices):
    for c in choices:
        if n % c == 0:
            return c
    return n


def _tn_matmul(a, b, name, col_shards=0):
    t, k = a.shape
    n = b.shape[1]
    tk = _tile_of(k, (1024, 512, 256, 128))
    tn = n // col_shards if col_shards else _tile_of(n, (1024, 896, 768, 512, 384, 256, 128))
    tt = _tile_of(t, (4096, 2048, 1024, 512, 256))

    def body(a_ref, b_ref, o_ref):
        _acc(o_ref, _dot_tn(a_ref[...], b_ref[...]), pl.program_id(2) == 0)

    if col_shards:
        out_shape, out_spec = _sds((col_shards, k, tn), F32), pl.BlockSpec((None, tk, tn), lambda i, j, s: (j, i, 0))
    else:
        out_shape, out_spec = _sds((k, n), F32), pl.BlockSpec((tk, tn), lambda i, j, s: (i, j))
    return pl.pallas_call(
        body, name=name, grid=(k // tk, n // tn, t // tt), out_shape=out_shape,
        in_specs=[pl.BlockSpec((tt, tk), lambda i, j, s: (s, i)), pl.BlockSpec((tt, tn), lambda i, j, s: (s, j))],
        out_specs=out_spec, compiler_params=_params("arbitrary", "arbitrary", "arbitrary"),
    )(a, b)


N_SHARD = 4
COL_SHARDED = ("w_in", "w_uq", "w_ukv", "w_o_mla", "w_pw_out", "w_ff1")
ROW_SHARDED = ("w_out", "w_ff2")
BIG = ("w_in", "w_uq", "w_ukv", "w_o_mla", "w_pw_out", "w_out", "w_ff1", "w_ff2")
SMALL = ("norm1_g", "q_latent_g", "kv_latent_g", "qk_norm_q_g", "qk_norm_k_g", "conv_b", "conv_ln_g", "conv_ln_b",
         "norm2_g")
WEIGHTS = ("w_ada", "b_ada", "norm1_g", "w_in", "q_latent_g", "w_uq", "kv_latent_g", "w_ukv", "qk_norm_q_g",
           "qk_norm_k_g", "w_o_mla", "conv_w", "conv_b", "conv_ln_g", "conv_ln_b", "w_pw_out", "w_out", "norm2_g",
           "w_ff1", "w_ff2")


def _pad_heads(w, width):
    k = w.shape[0]
    w3 = w.reshape(k, N_HEADS, width)
    return jnp.pad(w3, ((0, 0), (0, 0), (0, LANES - width))).reshape(k, HW)


def _unpad_heads(g, width):
    k = g.shape[0]
    return g.reshape(k, N_HEADS, LANES)[:, :, :width].reshape(k, N_HEADS * width)


def _win_segments(d):
    return [(OFF_GLU, OFF_GLU + 2 * d), (OFF_KR, OFF_GLU), (0, OFF_KV), KR_LANE, (OFF_KV, OFF_KR),
            LANES - KR_LANE - QK_ROPE]


def _pad_win(g4):
    _, d, ws = g4.shape
    parts = []
    for seg in _win_segments(d):
        if isinstance(seg, int):
            parts.append(jnp.zeros((d, seg), g4.dtype))
            continue
        a, b = seg
        while a < b:
            s = a // ws
            e = min(b, (s + 1) * ws)
            parts.append(g4[s, :, a - s * ws:e - s * ws])
            a = e
    return jnp.concatenate(parts, axis=1)


def _unpad_win(gp):
    d = gp.shape[0]
    ws = (OFF_GLU + 2 * d) // N_SHARD
    pieces, p = [], 0
    for seg in _win_segments(d):
        if isinstance(seg, int):
            p += seg
        else:
            pieces.append((seg[0], seg[1], p))
            p += seg[1] - seg[0]
    shards = []
    for s in range(N_SHARD):
        lo, hi = s * ws, (s + 1) * ws
        cols = [gp[:, p0 + max(a, lo) - a:p0 + min(b, hi) - a] for a, b, p0 in sorted(pieces) if max(a, lo) < min(b, hi)]
        shards.append(jnp.concatenate(cols, axis=1))
    return jnp.stack(shards)


def _col_shards(g):
    k, n = g.shape
    return g.reshape(k, N_SHARD, n // N_SHARD).transpose(1, 0, 2)


def _from_shards(g, name):
    ns, ks, nn = g.shape
    if name in ROW_SHARDED:
        return g.reshape(ns * ks, nn)
    return g.transpose(1, 0, 2).reshape(ks, ns * nn)


BY_SHARD = ("w_in", "w_ff1")
EARLY = ("w_in", "w_uq", "w_ukv")
LATE = ("w_o_mla", "w_pw_out", "w_out", "w_ff1", "w_ff2")


def _assemble(names, gathered):
    by_shard = {n: g.reshape((N_SHARD, 2 * g.shape[1]) + g.shape[2:]) for n, g in zip(names, gathered)}
    return {n: g if n in BY_SHARD else _from_shards(g, n) for n, g in by_shard.items()}


LARGE = ("w_in", "w_ff1", "w_ff2")
GROUP_A = ("w_out", "w_ff1", "w_ff2")
GROUP_B = ("w_in", "w_uq", "w_ukv", "w_o_mla", "w_pw_out")


def _pair_halves(g):
    return g.reshape(N_SHARD, 2, g.shape[1] // 2, g.shape[2])


def _pair_sums(names, halves, from_sibling):
    if not halves:
        return []
    cidx = lax.axis_index("c").reshape(1).astype(jnp.int32)
    out = {n: _add_pair(g, l, cidx, "pair_sum_" + n)
           for n, g, l in zip(names, halves, from_sibling) if n in LARGE}
    small = [j for j, n in enumerate(names) if n not in LARGE]
    if small:
        res = _add_pair_whole([halves[j] for j in small], [from_sibling[j] for j in small], cidx,
                              "pair_sum_small_" + names[small[0]])
        out.update({names[j]: r for j, r in zip(small, res)})
    return [out[n] for n in names]


def _local_step(x, target, mod, sp, w, late=None, tm=256):
    comm = late is not None
    w = dict(w)
    nseq, seq, d = x.shape
    t = nseq * seq
    tps = seq // tm
    xf = x.reshape(t, d)
    tg = target.reshape(t, d)
    mod3 = mod.reshape(nseq, N_MOD, d)

    win_p = _pad_win(w["w_in"])
    wuq_p = _pad_heads(w["w_uq"], QK_HEAD)
    wkv3 = w["w_ukv"].reshape(KV_RANK, N_HEADS, QK_NOPE + V_HEAD)
    wk_p = _pad_heads(wkv3[:, :, :QK_NOPE].reshape(KV_RANK, -1), QK_NOPE)
    wv_p = _pad_heads(wkv3[:, :, QK_NOPE:].reshape(KV_RANK, -1), V_HEAD)
    cw = jnp.pad(w["conv_w"], ((0, HALO - CONV_W), (0, 0)))
    pad_g = lambda g: jnp.pad(g, ((0, 0), (0, LANES - QK_HEAD)))
    gq, gk = pad_g(sp["qk_norm_q_g"]), pad_g(sp["qk_norm_k_g"])
    tabs = _rope_tables(seq)

    tm_in, tps_in = (2 * tm, tps // 2) if tps % 2 == 0 else (tm, tps)
    h, zm, zglu, zgate, u0 = _fwd_in(xf, sp["norm1_g"], mod3, win_p, tm_in, tps_in)
    q, k, v, qln, kvn = _mla_prep(zm, sp["q_latent_g"], sp["kv_latent_g"], gq, gk, tabs, wuq_p, wk_p, wv_p, tm, tps)
    attn, gathered = _attn_fwd(q, k, v, nseq, seq, tuple(late) if comm else ())
    if comm:
        w.update(_assemble(LATE, gathered))
    wo_p = jnp.pad(w["w_o_mla"].reshape(N_HEADS, V_HEAD, d), ((0, 0), (0, LANES - V_HEAD), (0, 0))).reshape(HW, d)
    x1, mixed, mpre, ya, yb, u1, u3 = _fwd_mix(attn, u0, zgate, xf, mod3, wo_p, cw, sp["conv_b"], sp["conv_ln_g"],
                                               sp["conv_ln_b"], w["w_pw_out"], w["w_out"], tm, tps)
    h2, a, r, dy, df, dgate2, loss_acc = _fwd_ffn(x1, tg, sp["norm2_g"], mod3, w["w_ff1"], w["w_ff2"], tm, tps)
    da, dx1, dmixed, dshift2, dscale2, dgate1, dg2 = _bwd_ffn(df, a, x1, dy, mixed, sp["norm2_g"], mod3,
                                                              w["w_ff2"], w["w_ff1"], tm, tps)
    gw = {
        "w_out": _tn_matmul(mpre, dmixed, "dw_out").reshape(N_SHARD, d // N_SHARD, d),
        "w_ff1": _tn_matmul(h2, da, "dw_ff1", N_SHARD),
        "w_ff2": _tn_matmul(r, df, "dw_ff2").reshape(N_SHARD, -1, d),
    }
    halves_a = [_pair_halves(gw[n]) for n in GROUP_A] if comm else []
    dya, dyb, dz, do, du1, dlng, dlnb, dcb, from_sibling = _bwd_mix(
        dmixed, zgate, ya, yb, u1, sp["conv_ln_g"], sp["conv_ln_b"], w["w_out"], wo_p, w["w_pw_out"], tm, tuple(halves_a))
    pair_a = _pair_sums(GROUP_A, halves_a, from_sibling)
    dz, dcw = _bwd_conv(dz, du1, u0, zglu, cw, tm, tps)
    gw["conv_w"] = dcw
    dq, dk, dv, land_a = _attn_bwd(q, k, v, do, nseq, seq, tuple(p[1] for p in pair_a))
    dz, dqpre, dkh, dgq, dgk, dgql, dgkvl = _mla_bwd(dz, dq, dk, dv, zm, sp["q_latent_g"], sp["kv_latent_g"], gq, gk,
                                                      tabs, wuq_p, wk_p, wv_p, tm, tps)
    dwk_p = _tn_matmul(kvn, dkh, "dw_uk")
    dwv_p = _tn_matmul(kvn, dv, "dw_uv")
    dwkv = jnp.concatenate([dwk_p.reshape(KV_RANK, N_HEADS, LANES)[:, :, :QK_NOPE],
                            dwv_p.reshape(KV_RANK, N_HEADS, LANES)[:, :, :V_HEAD]], axis=2).reshape(KV_RANK, -1)
    dwo = _tn_matmul(attn, dya, "dw_o").reshape(N_HEADS, LANES, d)[:, :V_HEAD].reshape(MLA_WIDTH, d)
    gw["w_in"] = _unpad_win(_tn_matmul(h, dz, "dw_in"))
    gw["w_uq"] = _col_shards(_unpad_heads(_tn_matmul(qln, dqpre, "dw_uq"), QK_HEAD))
    gw["w_ukv"] = _col_shards(dwkv)
    gw["w_o_mla"] = _col_shards(dwo)
    gw["w_pw_out"] = _tn_matmul(u3, dyb, "dw_pw", N_SHARD)
    pair_b = []
    if comm:
        halves_b = [_pair_halves(gw[n]) for n in GROUP_B]
        pair_b = _pair_sums(GROUP_B, halves_b, _pair_swap(halves_b, "grad_pair_swap"))
    gx, dshift1, dscale1, dg1, land_b = _bwd_in(dz, xf, dx1, sp["norm1_g"], mod3, win_p, tm_in, tps_in,
                                                tuple(p[1] for p in pair_b))
    if comm:
        for n, p, l in zip(GROUP_A + GROUP_B, pair_a + pair_b, land_a + land_b):
            gw[n] = (p[0], l)
    gs = {
        "norm1_g": dg1, "q_latent_g": dgql, "kv_latent_g": dgkvl, "qk_norm_q_g": dgq, "qk_norm_k_g": dgk,
        "conv_b": dcb, "conv_ln_g": dlng, "conv_ln_b": dlnb, "norm2_g": dg2,
    }
    dmod = jnp.concatenate([dshift1, dscale1, dgate1, dshift2, dscale2, dgate2], axis=2).reshape(nseq, N_MOD * d)
    return loss_acc, gx.reshape(nseq, seq, d), dmod, gw, gs


def kernel(x, c, w_ada, b_ada, norm1_g, w_in, q_latent_g, w_uq, kv_latent_g, w_ukv, qk_norm_q_g, qk_norm_k_g, w_o_mla, conv_w, conv_b, conv_ln_g, conv_ln_b, w_pw_out, w_out, norm2_g, w_ff1, w_ff2, loss_target, m_w_ada, m_b_ada, m_norm1_g, m_w_in, m_q_latent_g, m_w_uq, m_kv_latent_g, m_w_ukv, m_qk_norm_q_g, m_qk_norm_k_g, m_w_o_mla, m_conv_w, m_conv_b, m_conv_ln_g, m_conv_ln_b, m_w_pw_out, m_w_out, m_norm2_g, m_w_ff1, m_w_ff2, v_w_ada, v_b_ada, v_norm1_g, v_w_in, v_q_latent_g, v_w_uq, v_kv_latent_g, v_w_ukv, v_qk_norm_q_g, v_qk_norm_k_g, v_w_o_mla, v_conv_w, v_conv_b, v_conv_ln_g, v_conv_ln_b, v_w_pw_out, v_w_out, v_norm2_g, v_w_ff1, v_w_ff2):
    given = dict(locals())
    wts = {n: given[n][0] for n in WEIGHTS}
    mom = {n: given["m_" + n][0] for n in WEIGHTS}
    var = {n: given["v_" + n][0] for n in WEIGHTS}
    vec = lambda a: a.reshape(1, -1)
    nseq, seq, d = x.shape
    ix, iy, ic = _place()
    shard = 2 * ix + iy

    half = lambda n: lax.dynamic_slice_in_dim(wts[n].astype(BF16), ic * (wts[n].shape[0] // 2), wts[n].shape[0] // 2,
                                              axis=0)
    gathered = _all_gather8([half(n) for n in EARLY] + [wts["conv_w"], c], "gather_weights")
    full = _assemble(EARLY, gathered)
    full["conv_w"] = _from_shards(gathered[-2][0::2], "conv_w")
    c_all = gathered[-1].reshape(8 * nseq, d)

    n_ada = wts["w_ada"].shape[1]
    b_sh = lax.dynamic_slice_in_dim(vec(wts["b_ada"]), shard * n_ada, n_ada, axis=1)
    mod_sh = _ada_mod(c_all, wts["w_ada"], b_sh)
    hb = 4 * nseq
    mod_blk = lax.dynamic_slice_in_dim(mod_sh, ic * hb, hb, axis=0)
    (mod_all,) = _all_gather8([mod_blk], "gather_mod")
    mod_mine = lax.dynamic_slice_in_dim(mod_all, (2 * iy + ic) * nseq, nseq, axis=1)
    mod = jnp.concatenate([lax.dynamic_index_in_dim(mod_mine, 2 * s + ix, axis=0, keepdims=False)
                           for s in range(N_SHARD)], axis=1)

    sp = {n: vec(wts[n]) for n in SMALL}
    loss_part, grad_x, dmod, gw, gs = _local_step(x, loss_target, mod, sp, full, [half(n) for n in LATE])

    own_c = jnp.stack([shard, ic]).astype(jnp.int32)
    mine_sum = {n: _add_chips(gw[n][0], gw[n][1], own_c, "chip_sum_" + n) for n in LARGE}
    few = tuple(n for n in BIG if n not in LARGE)
    mine_sum.update(zip(few, _add_chips_whole([gw[n][0] for n in few], [gw[n][1] for n in few], own_c, "chip_sum_small")))
    summed, parts = _pair_gather_and_all_gather8(
        [mine_sum[n] for n in BIG], [dmod, gw["conv_w"], loss_part] + [gs[n] for n in SMALL], "tail_exchange")

    dmod_all = parts[0].reshape(8 * nseq, N_MOD * d)
    dmod_sh = lax.dynamic_slice_in_dim(dmod_all, shard * n_ada, n_ada, axis=1)
    res = _ada_bwd(c_all, dmod_all, dmod_sh, parts[1:])
    grads = {"w_ada": res[0], "b_ada": res[1]}
    n_cw = wts["conv_w"].shape[1]
    grads["conv_w"] = lax.dynamic_slice_in_dim(res[2], shard * n_cw, n_cw, axis=1)[:CONV_W]
    loss = res[3][0, 0]
    for n, g in zip(SMALL, res[4:]):
        grads[n] = g
    for n, g in zip(BIG, summed):
        grads[n] = g.reshape(wts[n].shape)

    delta, new_m, new_v = {}, {}, {}
    for n in LARGE + ("w_ada",):
        delta[n], new_m[n], new_v[n] = _adamw(wts[n], grads[n], mom[n], var[n], "adamw_" + n)
    rest = ("b_ada", "conv_w") + SMALL + few
    as2d = lambda a: a if a.ndim == 2 else vec(a)
    res = _adamw_small(*[[as2d(t[n]) for n in rest] for t in (wts, grads, mom, var)])
    for dst, arrs in zip((delta, new_m, new_v), res):
        for n, a in zip(rest, arrs):
            dst[n] = a

    outs = [loss, grad_x]
    for group in (grads, delta, new_m, new_v):
        outs += [group[n].reshape(given[n].shape) for n in WEIGHTS]
    return tuple(outs)
```

```python
import jax
import jax.numpy as jnp
from jax import lax
from jax.experimental import pallas as pl
from jax.experimental.pallas import tpu as pltpu

F32 = jnp.float32
BF16 = jnp.bfloat16
MESH = pl.DeviceIdType.MESH
ANY = pl.BlockSpec(memory_space=pl.ANY)

CHUNK = 64
CHUNK_SHIFT = 6
N_HEADS = 8
QK_NOPE = 64
QK_ROPE = 32
QK_HEAD = QK_NOPE + QK_ROPE
V_HEAD = 64
Q_RANK = 256
KV_RANK = 128
MLA_WIDTH = N_HEADS * V_HEAD
CONV_CH = 512
CONV_W = 31
ROPE_THETA = 10000.0
EPS = 1e-6
LANES = 128
SUBLANES = 8
HW = N_HEADS * LANES
OFF_KV = Q_RANK + KV_RANK
OFF_KR = OFF_KV + QK_ROPE
OFF_GLU = OFF_KR + 2 * CONV_CH
KR_LANE = QK_NOPE
MLA_IN = Q_RANK + KV_RANK + LANES
HALO = 32
N_MOD = 6

ADAM_LR = 0.001
ADAM_B1 = 0.9
ADAM_B2 = 0.999
ADAM_EPS = 1e-08
ADAM_WD = 0.01
ADAM_STEP = 10

VMEM_LIMIT = 56 * 1024 * 1024
BQ = 256


def _layout(d):
    p_glu = 2 * d
    p_q = p_glu + 2 * CONV_CH
    return p_glu, p_q, p_q + MLA_IN


def _params(*sem):
    return pltpu.CompilerParams(dimension_semantics=sem, vmem_limit_bytes=VMEM_LIMIT)


def _dot(a, b):
    return jnp.dot(a, b, preferred_element_type=F32)


def _dot_tn(a, b):
    return lax.dot_general(a, b, (((0,), (0,)), ((), ())), preferred_element_type=F32)


def _dot_nt(a, b):
    return lax.dot_general(a, b, (((1,), (1,)), ((), ())), preferred_element_type=F32)


def _acc(ref, val, first):
    @pl.when(first)
    def _():
        ref[...] = val

    @pl.when(jnp.logical_not(first))
    def _():
        ref[...] += val


def _rms(x):
    r = lax.rsqrt(jnp.mean(x * x, axis=-1, keepdims=True) + EPS)
    return x * r, r


def _rms_bwd(n, r, dn):
    return r * (dn - n * jnp.mean(dn * n, axis=-1, keepdims=True))


def _head_rms(sl):
    r = lax.rsqrt(jnp.sum(sl * sl, axis=-1, keepdims=True) * (1.0 / QK_HEAD) + EPS)
    return sl * r, r


def _head_rms_bwd(n, r, dn):
    return r * (dn - n * (jnp.sum(dn * n, axis=-1, keepdims=True) * (1.0 / QK_HEAD)))


def _rope(x, c, s1, s2):
    return x * c + pltpu.roll(x, QK_ROPE // 2, 1) * s1 + pltpu.roll(x, LANES - QK_ROPE // 2, 1) * s2


def _rope_t(dy, c, s1, s2):
    return dy * c + pltpu.roll(dy * s1, LANES - QK_ROPE // 2, 1) + pltpu.roll(dy * s2, QK_ROPE // 2, 1)


def _rope_tables(seq):
    half = QK_ROPE // 2
    inv_freq = ROPE_THETA ** (-jnp.arange(0, QK_ROPE, 2, dtype=F32) / QK_ROPE)
    ang = jnp.arange(seq, dtype=F32)[:, None] * inv_freq[None, :]
    cos, sin = jnp.cos(ang), jnp.sin(ang)
    z = lambda n: jnp.zeros((seq, n), F32)
    tail = LANES - QK_HEAD
    c = jnp.concatenate([jnp.ones((seq, QK_NOPE), F32), cos, cos, jnp.ones((seq, tail), F32)], axis=1)
    s1 = jnp.concatenate([z(QK_NOPE + half), sin, z(tail)], axis=1)
    s2 = jnp.concatenate([z(QK_NOPE), -sin, z(half + tail)], axis=1)
    return c, s1, s2


def _row(tm, w):
    return pl.BlockSpec((tm, w), lambda i: (i, 0))


def _modspec(d, tps):
    return pl.BlockSpec((None, N_MOD, d), lambda i: (i // tps, 0, 0))


def _seqv(w, tps):
    return pl.BlockSpec((None, 1, w), lambda i: (i // tps, 0, 0))


def _full(shape):
    return pl.BlockSpec(shape, lambda i: tuple(0 for _ in shape))


def _sds(shape, dtype):
    return jax.ShapeDtypeStruct(shape, dtype)


CONV_ROWS = 64
CONV_LC = CONV_CH // LANES


def _lane_chunks():
    return [(lc, slice(lc * LANES, (lc + 1) * LANES)) for lc in range(CONV_LC)]


def _fill_shifted(ext_ref, head, body):
    nh = head.shape[0]
    for lc, ls in _lane_chunks():
        ext_ref[0, lc, :nh, :] = head[:, ls]
        ext_ref[0, lc, nh:, :] = body[:, ls]
        rows = ext_ref[0, lc]
        for b in range(1, SUBLANES):
            ext_ref[b, lc] = pltpu.roll(rows, rows.shape[0] - b, 0)


def _shifted_shape(tm):
    return (SUBLANES, CONV_LC, tm + HALO, LANES)


def _conv_chunk(c):
    return c % CONV_LC, pl.multiple_of((c // CONV_LC) * CONV_ROWS, CONV_ROWS)


def _shifted(ext_ref, o, lc, r0):
    a = pl.multiple_of((o // SUBLANES) * SUBLANES + r0, SUBLANES)
    return ext_ref[o % SUBLANES, lc, pl.ds(a, CONV_ROWS), :]


def _by_lane_chunk(a):
    return a.reshape(a.shape[0], CONV_LC, LANES).transpose(1, 0, 2)


def _load_resident(i, pairs):
    @pl.when(i == 0)
    def _():
        for src, dst in pairs:
            pltpu.sync_copy(src, dst)


def _place():
    return lax.axis_index("x"), lax.axis_index("y"), lax.axis_index("c")


def _all_gather8(blocks, name):
    na = len(blocks)

    def body(*refs):
        start, forward, finish = _gather8_phases(refs[:na], refs[na:2 * na], *refs[2 * na:])
        start()
        forward()
        finish()

    outs = pl.pallas_call(
        body, name=name, out_shape=_gather8_shapes(blocks), in_specs=[ANY] * na, out_specs=(ANY,) * na,
        scratch_shapes=_gather8_sems(na),
    )(*blocks)
    return _own_block_placed(outs, blocks)


def _gather8_shapes(blocks):
    return tuple(_sds((8,) + b.shape, b.dtype) for b in blocks)


def _gather8_sems(na):
    return [pltpu.SemaphoreType.DMA((7 * na,)), pltpu.SemaphoreType.DMA((7 * na,))]


def _own_block_placed(outs, blocks):
    ix, iy, ic = _place()
    return tuple(lax.dynamic_update_index_in_dim(o, b, 4 * ix + 2 * iy + ic, 0) for o, b in zip(outs, blocks))


def _gather8_phases(x_refs, out_refs, send_sems, recv_sems):
    na = len(x_refs)
    x, y, c = _place()
    me, sibling = (x, y, c), (x, y, 1 - c)
    chips = [(1 - x, y), (x, 1 - y), (1 - x, 1 - y)]

    def copy(a, k, blk, to, from_input=False):
        dst = out_refs[a].at[4 * blk[0] + 2 * blk[1] + blk[2]]
        return pltpu.make_async_remote_copy(
            src_ref=x_refs[a] if from_input else dst, dst_ref=dst,
            send_sem=send_sems.at[7 * a + k], recv_sem=recv_sems.at[7 * a + k], device_id=to, device_id_type=MESH)

    def first(a):
        return [copy(a, 0, me, sibling, True)] + [copy(a, 1 + j, me, (*chip, c), True) for j, chip in enumerate(chips)]

    def start():
        for a in range(na):
            for cp in first(a):
                cp.start()

    def forward():
        for j, chip in enumerate(chips):
            for a in range(na):
                copy(a, 1 + j, (*chip, c), me).wait_recv()
                copy(a, 4 + j, (*chip, c), sibling).start()

    def finish():
        for a in range(na):
            copy(a, 0, sibling, me).wait_recv()
            for j, chip in enumerate(chips):
                copy(a, 4 + j, (*chip, 1 - c), me).wait_recv()
        for a in range(na):
            for cp in first(a) + [copy(a, 4 + j, (*chip, c), sibling) for j, chip in enumerate(chips)]:
                cp.wait_send()

    return start, forward, finish


def _pair_swap(gs, name):
    na = len(gs)

    def body(*refs):
        start, finish = _swap_phases(refs[:na], refs[na:2 * na], *refs[2 * na:])
        start()
        finish()

    return pl.pallas_call(
        body, name=name, out_shape=_swap_shapes(gs), in_specs=[ANY] * na, out_specs=(ANY,) * na,
        scratch_shapes=_swap_sems(gs),
    )(*gs)


def _swap_shapes(gs):
    return tuple(_sds(g.shape[:1] + g.shape[2:], g.dtype) for g in gs)


def _swap_sems(gs):
    n = sum(g.shape[0] for g in gs)
    return [pltpu.SemaphoreType.DMA((n,)), pltpu.SemaphoreType.DMA((n,))]


def _swap_phases(g_refs, land_refs, send_sems, recv_sems):
    x, y, c = _place()

    def copies():
        cps, k = [], 0
        for g_ref, land_ref in zip(g_refs, land_refs):
            for s in range(g_ref.shape[0]):
                cps.append(pltpu.make_async_remote_copy(
                    src_ref=g_ref.at[s, 1 - c], dst_ref=land_ref.at[s], send_sem=send_sems.at[k],
                    recv_sem=recv_sems.at[k], device_id=(x, y, 1 - c), device_id_type=MESH))
                k += 1
        return cps

    def start():
        for cp in copies():
            cp.start()

    def finish():
        for cp in copies():
            cp.wait()

    return start, finish


def _scatter_shapes(hs):
    return tuple(_sds((3,) + h.shape[1:], h.dtype) for h in hs)


def _scatter_sems(na):
    return [pltpu.SemaphoreType.DMA((3 * na,)), pltpu.SemaphoreType.DMA((3 * na,))]


def _scatter_phases(h_refs, land_refs, send_sems, recv_sems):
    x, y, c = _place()
    chips = [(1 - x, y), (x, 1 - y), (1 - x, 1 - y)]

    def copies():
        return [pltpu.make_async_remote_copy(
            src_ref=h_refs[a].at[2 * tx + ty], dst_ref=land_refs[a].at[j], send_sem=send_sems.at[3 * a + j],
            recv_sem=recv_sems.at[3 * a + j], device_id=(tx, ty, c), device_id_type=MESH)
            for a in range(len(h_refs)) for j, (tx, ty) in enumerate(chips)]

    def start():
        for cp in copies():
            cp.start()

    def finish():
        for cp in copies():
            cp.wait()

    return start, finish


def _pair_gather_and_all_gather8(fs, blocks, name):
    nf, nb = len(fs), len(blocks)

    def body(*refs):
        f_refs = refs[nf + nb:2 * nf + nb]
        b_out = refs[2 * nf + nb:2 * nf + 2 * nb]
        send_sems, recv_sems, g_send, g_recv = refs[2 * nf + 2 * nb:]
        x, y, c = _place()
        start, forward, finish = _gather8_phases(refs[nf:nf + nb], b_out, g_send, g_recv)
        sends = [pltpu.make_async_remote_copy(
            src_ref=f_refs[a].at[c], dst_ref=f_refs[a].at[c], send_sem=send_sems.at[a], recv_sem=recv_sems.at[a],
            device_id=(x, y, 1 - c), device_id_type=MESH) for a in range(nf)]
        recvs = [pltpu.make_async_remote_copy(
            src_ref=f_refs[a].at[c], dst_ref=f_refs[a].at[1 - c], send_sem=send_sems.at[a],
            recv_sem=recv_sems.at[a], device_id=(x, y, 1 - c), device_id_type=MESH) for a in range(nf)]
        start()
        for cp in sends:
            cp.start()
        forward()
        finish()
        for cp in recvs:
            cp.wait_recv()
        for cp in sends:
            cp.wait_send()

    res = pl.pallas_call(
        body, name=name, out_shape=tuple(_sds(f.shape, f.dtype) for f in fs) + _gather8_shapes(blocks),
        in_specs=[ANY] * (nf + nb), out_specs=(ANY,) * (nf + nb), input_output_aliases={a: a for a in range(nf)},
        scratch_shapes=[pltpu.SemaphoreType.DMA((nf,)), pltpu.SemaphoreType.DMA((nf,))] + _gather8_sems(nb),
    )(*fs, *blocks)
    return res[:nf], _own_block_placed(res[nf:], blocks)


def _row_tile(r, n, itemsize=4, budget=1 << 21):
    if r * n * itemsize <= budget:
        return r
    best = None
    for tr in range(16, r, 16):
        if r % tr == 0 and tr * n * itemsize <= budget:
            best = tr
    assert best is not None, (r, n)
    return best


def _add_pair(g, land, cidx, name):
    ns, _, r, n = g.shape
    tr = _row_tile(r, n)

    def body(c_ref, a_ref, b_ref, o_ref, ob_ref):
        s = a_ref[...] + b_ref[...]
        o_ref[...] = s
        ob_ref[...] = s.astype(BF16)

    out = pl.BlockSpec((None, tr, n), lambda s, i, cr: (s, i, 0))
    return pl.pallas_call(
        body, name=name, out_shape=(_sds((ns, r, n), F32), _sds((ns, r, n), BF16)),
        grid_spec=pltpu.PrefetchScalarGridSpec(
            num_scalar_prefetch=1, grid=(ns, r // tr),
            in_specs=[pl.BlockSpec((None, None, tr, n), lambda s, i, cr: (s, cr[0], i, 0)), out],
            out_specs=(out, out)),
        compiler_params=_params("arbitrary", "arbitrary"),
    )(cidx, g, land)


def _add_pair_whole(gs, lands, cidx, name):
    k = len(gs)

    def body(c_ref, *refs):
        for a_ref, b_ref, o_ref, ob_ref in zip(refs[:k], refs[k:2 * k], refs[2 * k:3 * k], refs[3 * k:]):
            s = a_ref[...] + b_ref[...]
            o_ref[...] = s
            ob_ref[...] = s.astype(BF16)

    half = lambda g: pl.BlockSpec((g.shape[0], None) + g.shape[2:], lambda i, cr: (0, cr[0], 0, 0))
    whole = lambda g: pl.BlockSpec(g.shape[:1] + g.shape[2:], lambda i, cr: (0, 0, 0))
    shapes = lambda dt: tuple(_sds(g.shape[:1] + g.shape[2:], dt) for g in gs)
    res = pl.pallas_call(
        body, name=name, out_shape=shapes(F32) + shapes(BF16),
        grid_spec=pltpu.PrefetchScalarGridSpec(
            num_scalar_prefetch=1, grid=(1,),
            in_specs=[half(g) for g in gs] + [whole(g) for g in gs],
            out_specs=tuple(whole(g) for g in gs) * 2),
        compiler_params=_params("arbitrary"),
    )(cidx, *gs, *lands)
    return list(zip(res[:k], res[k:]))


def _add_chips_whole(hs, lands, own_c, name):
    k = len(hs)

    def body(o_idx, *refs):
        for h_ref, l_ref, o_ref in zip(refs[:k], refs[k:2 * k], refs[2 * k:]):
            o_ref[...] = ((h_ref[...] + l_ref[0].astype(F32)) + l_ref[1].astype(F32)) + l_ref[2].astype(F32)

    return pl.pallas_call(
        body, name=name, out_shape=tuple(_sds((2,) + h.shape[1:], F32) for h in hs),
        grid_spec=pltpu.PrefetchScalarGridSpec(
            num_scalar_prefetch=1, grid=(1,),
            in_specs=[pl.BlockSpec((None,) + h.shape[1:], lambda i, o: (o[0], 0, 0)) for h in hs]
            + [pl.BlockSpec(l.shape, lambda i, o: (0, 0, 0)) for l in lands],
            out_specs=tuple(pl.BlockSpec((None,) + h.shape[1:], lambda i, o: (o[1], 0, 0)) for h in hs)),
        compiler_params=_params("arbitrary"),
    )(own_c, *hs, *lands)


def _add_chips(h, land, own_c, name):
    _, r, n = h.shape
    tr = _row_tile(r, n)

    def body(o_idx, h_ref, l_ref, o_ref):
        o_ref[...] = ((h_ref[...] + l_ref[0].astype(F32)) + l_ref[1].astype(F32)) + l_ref[2].astype(F32)

    return pl.pallas_call(
        body, name=name, out_shape=_sds((2, r, n), F32),
        grid_spec=pltpu.PrefetchScalarGridSpec(
            num_scalar_prefetch=1, grid=(r // tr,),
            in_specs=[pl.BlockSpec((None, tr, n), lambda i, o: (o[0], i, 0)),
                      pl.BlockSpec((3, tr, n), lambda i, o: (0, i, 0))],
            out_specs=pl.BlockSpec((None, tr, n), lambda i, o: (o[1], i, 0))),
        compiler_params=_params("arbitrary"),
    )(own_c, h, land)


def _adam_math(w, g, m, v):
    nm = ADAM_B1 * m + (1.0 - ADAM_B1) * g
    nv = ADAM_B2 * v + (1.0 - ADAM_B2) * (g * g)
    m_hat = nm / (1.0 - ADAM_B1 ** ADAM_STEP)
    v_hat = nv / (1.0 - ADAM_B2 ** ADAM_STEP)
    return -ADAM_LR * (m_hat / (jnp.sqrt(v_hat) + ADAM_EPS) + ADAM_WD * w), nm, nv


def _adamw(w, g, m, v, name):
    r, n = w.shape

    def body(w_ref, g_ref, m_ref, v_ref, d_ref, nm_ref, nv_ref):
        d_ref[...], nm_ref[...], nv_ref[...] = _adam_math(w_ref[...], g_ref[...], m_ref[...], v_ref[...])

    if r % 16 == 0:
        tr = _row_tile(r, n)
        steps, spec = r // tr, pl.BlockSpec((tr, n), lambda i: (i, 0))
    else:
        tc = 4 * LANES
        steps, spec = n // tc, pl.BlockSpec((r, tc), lambda j: (0, j))
    return pl.pallas_call(
        body, name=name, out_shape=(_sds((r, n), F32),) * 3, grid=(steps,),
        in_specs=[spec] * 4, out_specs=(spec,) * 3, compiler_params=_params("arbitrary"),
    )(w, g, m, v)


def _adamw_small(ws, gs, ms, vs):
    k = len(ws)

    def body(*refs):
        ins, outs = refs[:4 * k], refs[4 * k:]
        for j in range(k):
            d, nm, nv = _adam_math(ins[j][...], ins[k + j][...], ins[2 * k + j][...], ins[3 * k + j][...])
            outs[j][...] = d
            outs[k + j][...] = nm
            outs[2 * k + j][...] = nv

    shapes = tuple(_sds(w.shape, F32) for w in ws)
    res = pl.pallas_call(body, name="adamw_small", out_shape=shapes * 3,
                         compiler_params=pltpu.CompilerParams(vmem_limit_bytes=VMEM_LIMIT))(*ws, *gs, *ms, *vs)
    return res[:k], res[k:2 * k], res[2 * k:]


def _ada_mod(c_all, w_sh, b_sh):
    b, _ = c_all.shape
    n = w_sh.shape[1]

    def body(c_ref, w_ref, b_ref, o_ref):
        cc = c_ref[...]
        ca = (cc * jax.nn.sigmoid(cc)).astype(BF16)
        o_ref[...] = _dot(ca, w_ref[...].astype(BF16)) + b_ref[...]

    return pl.pallas_call(body, name="ada_mod", out_shape=_sds((b, n), F32),
                          compiler_params=pltpu.CompilerParams(vmem_limit_bytes=VMEM_LIMIT))(c_all, w_sh, b_sh)


def _ada_bwd(c_all, dmod_all, dmod_sh, parts):
    b, d = c_all.shape
    n6 = dmod_all.shape[1]
    n = dmod_sh.shape[1]
    k = len(parts)

    def body(*refs):
        c_ref, da_ref, ds_ref = refs[:3]
        p_refs = refs[3:3 + k]
        dw_ref, db_ref = refs[3 + k:5 + k]
        s_refs = refs[5 + k:]
        cc = c_ref[...]
        ca = (cc * jax.nn.sigmoid(cc)).astype(BF16)
        dw_ref[...] = _dot_tn(ca, ds_ref[...].astype(BF16))
        db_ref[...] = jnp.sum(da_ref[...], axis=0, keepdims=True)
        for p_ref, s_ref in zip(p_refs, s_refs):
            tot = p_ref[0]
            for j in range(1, p_ref.shape[0]):
                tot = tot + p_ref[j]
            s_ref[...] = tot

    return pl.pallas_call(
        body, name="ada_bwd",
        out_shape=(_sds((d, n), F32), _sds((1, n6), F32)) + tuple(_sds(p.shape[1:], F32) for p in parts),
        compiler_params=pltpu.CompilerParams(vmem_limit_bytes=VMEM_LIMIT),
    )(c_all, dmod_all, dmod_sh, *parts)


def _fwd_in(x, g1, mod3, win_p, tm, tps):
    t, d = x.shape
    p_glu, p_q, npad = _layout(d)

    def body(x_ref, g_ref, mod_ref, w_hbm, h_ref, zm_ref, zglu_ref, zgate_ref, u0_ref, w_ref):
        _load_resident(pl.program_id(0), [(w_hbm, w_ref)])
        n, _ = _rms(x_ref[...])
        h = ((n * g_ref[...]) * (1.0 + mod_ref[1:2, :]) + mod_ref[0:1, :]).astype(BF16)
        h_ref[...] = h
        z = _dot(h, w_ref[...])
        zgate_ref[...] = z[:, :p_glu]
        zglu = z[:, p_glu:p_q]
        zglu_ref[...] = zglu
        zm_ref[...] = z[:, p_q:]
        u0_ref[...] = zglu[:, :CONV_CH] * jax.nn.sigmoid(zglu[:, CONV_CH:])

    return pl.pallas_call(
        body, name="fwd_in", grid=(t // tm,),
        out_shape=(_sds((t, d), BF16), _sds((t, MLA_IN), F32), _sds((t, 2 * CONV_CH), F32), _sds((t, 2 * d), F32),
                   _sds((t, CONV_CH), F32)),
        in_specs=[_row(tm, d), _full((1, d)), _modspec(d, tps), ANY],
        out_specs=(_row(tm, d), _row(tm, MLA_IN), _row(tm, 2 * CONV_CH), _row(tm, 2 * d), _row(tm, CONV_CH)),
        scratch_shapes=[pltpu.VMEM(win_p.shape, BF16)],
        compiler_params=_params("arbitrary"),
    )(x, g1, mod3, win_p)


def _mla_prep(zm, gql, gkvl, gq, gk, tabs, wuq_p, wk_p, wv_p, tm, tps):
    t = zm.shape[0]
    c_t, s1_t, s2_t = tabs
    tab = pl.BlockSpec((tm, LANES), lambda i: (i % tps, 0))

    def body(zm_ref, gql_ref, gkvl_ref, gq_ref, gk_ref, c_ref, s1_ref, s2_ref, wuq_ref, wk_ref, wv_ref,
             q_ref, k_ref, v_ref, qln_ref, kvn_ref):
        c, s1, s2 = c_ref[...], s1_ref[...], s2_ref[...]
        nq, _ = _rms(zm_ref[:, :Q_RANK])
        qln = (nq * gql_ref[...]).astype(BF16)
        qln_ref[...] = qln
        qpre = _dot(qln, wuq_ref[...])
        nkv, _ = _rms(zm_ref[:, Q_RANK:OFF_KV])
        kvn = (nkv * gkvl_ref[...]).astype(BF16)
        kvn_ref[...] = kvn
        knope = _dot(kvn, wk_ref[...])
        v_ref[...] = _dot(kvn, wv_ref[...]).astype(BF16)
        zkr_v = zm_ref[:, OFF_KV:]
        kr_roped = _rope(zkr_v * gk_ref[...], c, s1, s2)
        slabs = [slice(hd * LANES, (hd + 1) * LANES) for hd in range(N_HEADS)]
        rq = [_head_rms(qpre[:, sl])[1] for sl in slabs]
        rk = [_head_rms(knope[:, sl] + zkr_v)[1] for sl in slabs]
        for hd, sl in enumerate(slabs):
            q_ref[:, sl] = _rope((qpre[:, sl] * rq[hd]) * gq_ref[...], c, s1, s2).astype(BF16)
            k_ref[:, sl] = (rk[hd] * (knope[:, sl] * gk_ref[...] + kr_roped)).astype(BF16)

    return pl.pallas_call(
        body, name="mla_prep", grid=(t // tm,),
        out_shape=(_sds((t, HW), BF16),) * 3 + (_sds((t, Q_RANK), BF16), _sds((t, KV_RANK), BF16)),
        in_specs=[_row(tm, MLA_IN), _full((1, Q_RANK)), _full((1, KV_RANK)),
                  _full((1, LANES)), _full((1, LANES)), tab, tab, tab,
                  _full(wuq_p.shape), _full(wk_p.shape), _full(wv_p.shape)],
        out_specs=(_row(tm, HW),) * 3 + (_row(tm, Q_RANK), _row(tm, KV_RANK)),
        compiler_params=_params("arbitrary"),
    )(zm, gql, gkvl, gq, gk, c_t, s1_t, s2_t, wuq_p, wk_p, wv_p)


AHEAD = 2
ROW_BAND = 256
SM_SCALE = QK_HEAD ** -0.5
EXP2_SCALE = SM_SCALE * 1.4426950408889634


def _diag_mask():
    rc = jnp.right_shift(lax.broadcasted_iota(jnp.int32, (BQ, 1), 0), CHUNK_SHIFT)
    cc = jnp.right_shift(lax.broadcasted_iota(jnp.int32, (1, BQ), 1), CHUNK_SHIFT)
    return rc >= cc


def _scores(q_i, k_ref, lo, e):
    return (_dot_nt(q_i, k_ref[:lo, :]) if lo else None), _dot_nt(q_i, k_ref[lo:e, :])


def _softmax_parts(scores, mask):
    sp, sd = scores
    sd = jnp.where(mask, sd, jnp.finfo(F32).min)
    m = jnp.max(sd, axis=-1, keepdims=True)
    if sp is not None:
        m = jnp.maximum(m, jnp.max(sp, axis=-1, keepdims=True))
    pd = jnp.exp2((sd - m) * EXP2_SCALE)
    l = jnp.sum(pd, axis=-1, keepdims=True)
    pp = None
    if sp is not None:
        pp = jnp.exp2((sp - m) * EXP2_SCALE)
        l = l + jnp.sum(pp, axis=-1, keepdims=True)
    return pp, pd, l


def _attn_fwd(q, k, v, nseq, seq, gather=()):
    t = q.shape[0]
    na = len(gather)
    blk = pl.BlockSpec((seq, LANES), lambda b, h: (b, h))
    n_steps = nseq * N_HEADS

    def body(q_ref, k_ref, v_ref, *rest):
        o_ref = rest[na]
        if na:
            start, forward, finish = _gather8_phases(rest[:na], rest[na + 1:2 * na + 1], *rest[2 * na + 1:])
            step = pl.program_id(0) * N_HEADS + pl.program_id(1)
            pl.when(step == 0)(start)
            pl.when(step == (7 * n_steps) // 8)(forward)
        mask = _diag_mask()
        nb = seq // BQ
        block_scores = lambda j: _scores(q_ref[j * BQ:(j + 1) * BQ, :], k_ref, j * BQ, (j + 1) * BQ)
        ahead = [block_scores(j) for j in range(min(AHEAD, nb))]
        for i in range(nb):
            lo, e = i * BQ, (i + 1) * BQ
            cur = ahead.pop(0)
            if i + AHEAD < nb:
                ahead.append(block_scores(i + AHEAD))
            pp, pd, l = _softmax_parts(cur, mask)
            o = _dot(pd.astype(BF16), v_ref[lo:e, :])
            if lo:
                o = o + _dot(pp.astype(BF16), v_ref[:lo, :])
            o_ref[lo:e, :] = (o * (1.0 / l)).astype(BF16)
        if na:
            pl.when(step == n_steps - 1)(finish)

    res = pl.pallas_call(
        body, name="attn_fwd", grid=(nseq, N_HEADS), out_shape=(_sds((t, HW), BF16),) + _gather8_shapes(gather),
        in_specs=[blk, blk, blk] + [ANY] * na, out_specs=(blk,) + (ANY,) * na,
        scratch_shapes=_gather8_sems(na) if na else [],
        compiler_params=_params("arbitrary", "arbitrary"),
    )(q, k, v, *gather)
    return res[0], (_own_block_placed(res[1:], gather) if na else ())


def _fwd_mix(attn, u0, zgate, x, mod3, wo_p, cw, cb, lng, lnb, wpw, wout, tm, tps):
    t, d = x.shape
    hpt = tm // HALO
    cwc, cbc = _by_lane_chunk(cw), _by_lane_chunk(cb)

    def body(a_ref, u_ref, uh_ref, zg_ref, x_ref, mod_ref, wo_ref, cw_ref, cb_ref, lng_ref, lnb_ref, wpw_ref, wout_ref,
             x1_ref, mixed_ref, mpre_ref, ya_ref, yb_ref, u1_ref, u3_ref, ext_ref):
        i = pl.program_id(0)
        ya = _dot(a_ref[...], wo_ref[...])
        ya_ref[...] = ya
        first = (i % tps) == 0
        _fill_shifted(ext_ref, jnp.where(first, 0.0, uh_ref[...]), u_ref[...])
        for lc, ls in _lane_chunks():
            acc = jnp.broadcast_to(cb_ref[lc], (tm, LANES))
            for kk in range(CONV_W):
                o = HALO - (CONV_W - 1) + kk
                a = (o // SUBLANES) * SUBLANES
                acc = acc + cw_ref[lc, kk:kk + 1, :] * ext_ref[o % SUBLANES, lc, a:a + tm, :]
            u1_ref[:, ls] = acc
        acc = u1_ref[...]
        mu = jnp.mean(acc, axis=-1, keepdims=True)
        xc = acc - mu
        rstd = lax.rsqrt(jnp.mean(xc * xc, axis=-1, keepdims=True) + EPS)
        l = (xc * rstd) * lng_ref[...] + lnb_ref[...]
        u3 = (l * jax.nn.sigmoid(l)).astype(BF16)
        u3_ref[...] = u3
        yb = _dot(u3, wpw_ref[...])
        yb_ref[...] = yb
        zg = zg_ref[...]
        mpre = (jax.nn.sigmoid(zg[:, :d]) * ya + jax.nn.sigmoid(zg[:, d:]) * yb).astype(BF16)
        mpre_ref[...] = mpre
        mixed = _dot(mpre, wout_ref[...])
        mixed_ref[...] = mixed
        x1_ref[...] = x_ref[...] + mod_ref[2:3, :] * mixed

    halo = pl.BlockSpec((HALO, CONV_CH), lambda i: (jnp.maximum(i * hpt - 1, 0), 0))
    return pl.pallas_call(
        body, name="fwd_mix", grid=(t // tm,),
        out_shape=(_sds((t, d), F32), _sds((t, d), F32), _sds((t, d), BF16), _sds((t, d), F32), _sds((t, d), F32),
                   _sds((t, CONV_CH), F32), _sds((t, CONV_CH), BF16)),
        in_specs=[_row(tm, HW), _row(tm, CONV_CH), halo, _row(tm, 2 * d), _row(tm, d), _modspec(d, tps),
                  _full(wo_p.shape), _full(cwc.shape), _full(cbc.shape), _full((1, CONV_CH)), _full((1, CONV_CH)),
                  _full(wpw.shape), _full(wout.shape)],
        out_specs=(_row(tm, d), _row(tm, d), _row(tm, d), _row(tm, d), _row(tm, d), _row(tm, CONV_CH),
                   _row(tm, CONV_CH)),
        scratch_shapes=[pltpu.VMEM(_shifted_shape(tm), F32)],
        compiler_params=_params("arbitrary"),
    )(attn, u0, u0, zgate, x, mod3, wo_p, cwc, cbc, lng, lnb, wpw, wout)


def _shards_into_columns(w_hbm, w_ref):
    ns = w_hbm.shape[2]
    return [(w_hbm.at[s], w_ref.at[:, pl.ds(s * ns, ns)]) for s in range(w_hbm.shape[0])]


def _fwd_ffn(x1, target, g2, mod3, w1, w2, tm, tps):
    t, d = x1.shape
    dff = w1.shape[0] * w1.shape[2]

    def body(x1_ref, tg_ref, g_ref, mod_ref, w1_hbm, w2_hbm,
             h2_ref, a_ref, r_ref, dy_ref, df_ref, dgate_ref, loss_ref, w1_ref, w2_ref):
        i = pl.program_id(0)
        _load_resident(i, _shards_into_columns(w1_hbm, w1_ref) + [(w2_hbm, w2_ref)])
        x1v = x1_ref[...]
        gate2 = mod_ref[5:6, :]
        n, _ = _rms(x1v)
        h2 = ((n * g_ref[...]) * (1.0 + mod_ref[4:5, :]) + mod_ref[3:4, :]).astype(BF16)
        h2_ref[...] = h2
        a = _dot(h2, w1_ref[...])
        a_ref[...] = a
        r = jnp.square(jnp.maximum(a, 0.0)).astype(BF16)
        r_ref[...] = r
        f = _dot(r, w2_ref[...])
        e = (x1v + gate2 * f) - tg_ref[...]
        part = 0.5 * jnp.sum(jnp.mean(e * e, axis=-1, keepdims=True), axis=0, keepdims=True)
        _acc(loss_ref, jnp.broadcast_to(part, loss_ref.shape), i == 0)
        dy = e * (1.0 / d)
        dy_ref[...] = dy
        df_ref[...] = (dy * gate2).astype(BF16)
        _acc(dgate_ref, jnp.sum(dy * f, axis=0, keepdims=True), (i % tps) == 0)

    nseq = t // (tm * tps)
    return pl.pallas_call(
        body, name="fwd_ffn", grid=(t // tm,),
        out_shape=(_sds((t, d), BF16), _sds((t, dff), F32), _sds((t, dff), BF16), _sds((t, d), F32), _sds((t, d), BF16),
                   _sds((nseq, 1, d), F32), _sds((8, LANES), F32)),
        in_specs=[_row(tm, d), _row(tm, d), _full((1, d)), _modspec(d, tps), ANY, ANY],
        out_specs=(_row(tm, d), _row(tm, dff), _row(tm, dff), _row(tm, d), _row(tm, d), _seqv(d, tps),
                   _full((8, LANES))),
        scratch_shapes=[pltpu.VMEM((d, dff), BF16), pltpu.VMEM(w2.shape, BF16)],
        compiler_params=_params("arbitrary"),
    )(x1, target, g2, mod3, w1, w2)


def _bwd_ffn(df, a, x1, dy, mixed, g2, mod3, w2, w1, tm, tps):
    t, d = x1.shape
    dff = a.shape[1]

    def body(df_ref, a_ref, x1_ref, dy_ref, mx_ref, g_ref, mod_ref, w2_hbm, w1_hbm,
             da_ref, dx1_ref, dmixed_ref, dshift_ref, dscale_ref, dgate1_ref, dg2_ref, w2_ref, w1_ref):
        i = pl.program_id(0)
        _load_resident(i, [(w2_hbm, w2_ref)] + _shards_into_columns(w1_hbm, w1_ref))
        first_seq = (i % tps) == 0
        dr = _dot_nt(df_ref[...], w2_ref[...])
        da = (dr * (2.0 * jnp.maximum(a_ref[...], 0.0))).astype(BF16)
        da_ref[...] = da
        dh2 = _dot_nt(da, w1_ref[...])
        n, r = _rms(x1_ref[...])
        g = g_ref[...]
        sc1 = 1.0 + mod_ref[4:5, :]
        _acc(dshift_ref, jnp.sum(dh2, axis=0, keepdims=True), first_seq)
        _acc(dscale_ref, jnp.sum(dh2 * (n * g), axis=0, keepdims=True), first_seq)
        _acc(dg2_ref, jnp.sum((dh2 * sc1) * n, axis=0, keepdims=True), i == 0)
        dx1 = dy_ref[...] + _rms_bwd(n, r, (dh2 * sc1) * g)
        dx1_ref[...] = dx1
        _acc(dgate1_ref, jnp.sum(dx1 * mx_ref[...], axis=0, keepdims=True), first_seq)
        dmixed_ref[...] = (dx1 * mod_ref[2:3, :]).astype(BF16)

    nseq = t // (tm * tps)
    sv = _sds((nseq, 1, d), F32)
    return pl.pallas_call(
        body, name="bwd_ffn", grid=(t // tm,),
        out_shape=(_sds((t, dff), BF16), _sds((t, d), F32), _sds((t, d), BF16), sv, sv, sv, _sds((1, d), F32)),
        in_specs=[_row(tm, d), _row(tm, dff), _row(tm, d), _row(tm, d), _row(tm, d), _full((1, d)), _modspec(d, tps),
                  ANY, ANY],
        out_specs=(_row(tm, dff), _row(tm, d), _row(tm, d), _seqv(d, tps), _seqv(d, tps), _seqv(d, tps),
                   _full((1, d))),
        scratch_shapes=[pltpu.VMEM(w2.shape, BF16), pltpu.VMEM((d, dff), BF16)],
        compiler_params=_params("arbitrary"),
    )(df, a, x1, dy, mixed, g2, mod3, w2, w1)


def _bwd_mix(dmixed, zgate, ya, yb, u1, lng, lnb, wout, wo_p, wpw, tm, swap=()):
    t, d = ya.shape
    _, _, npad = _layout(d)
    nw = len(swap)
    n_steps = t // tm

    def body(dm_ref, zg_ref, ya_ref, yb_ref, u1_ref, lng_ref, lnb_ref, wout_ref, wo_ref, wpw_ref, *rest):
        dya_ref, dyb_ref, dz_ref, do_ref, du1_ref, dlng_ref, dlnb_ref, dcb_ref = rest[nw:nw + 8]
        i = pl.program_id(0)
        if nw:
            start, finish = _swap_phases(rest[:nw], rest[nw + 8:2 * nw + 8], *rest[2 * nw + 8:])
            pl.when(i == 0)(start)
        dmpre = _dot_nt(dm_ref[...], wout_ref[...])
        zg = zg_ref[...]
        ga = jax.nn.sigmoid(zg[:, :d])
        gb = jax.nn.sigmoid(zg[:, d:])
        dya = (dmpre * ga).astype(BF16)
        dyb = (dmpre * gb).astype(BF16)
        dya_ref[...] = dya
        dyb_ref[...] = dyb
        dz_ref[:, :d] = ((dmpre * ya_ref[...]) * (ga * (1.0 - ga))).astype(BF16)
        dz_ref[:, d:] = ((dmpre * yb_ref[...]) * (gb * (1.0 - gb))).astype(BF16)
        do_ref[...] = _dot_nt(dya, wo_ref[...]).astype(BF16)
        du3 = _dot_nt(dyb, wpw_ref[...])
        u1 = u1_ref[...]
        mu = jnp.mean(u1, axis=-1, keepdims=True)
        xc = u1 - mu
        rstd = lax.rsqrt(jnp.mean(xc * xc, axis=-1, keepdims=True) + EPS)
        nh = xc * rstd
        l = nh * lng_ref[...] + lnb_ref[...]
        sg = jax.nn.sigmoid(l)
        dl = du3 * (sg * (1.0 + l * (1.0 - sg)))
        _acc(dlng_ref, jnp.sum(dl * nh, axis=0, keepdims=True), i == 0)
        _acc(dlnb_ref, jnp.sum(dl, axis=0, keepdims=True), i == 0)
        dnh = dl * lng_ref[...]
        du1 = rstd * (dnh - jnp.mean(dnh, axis=-1, keepdims=True) - nh * jnp.mean(dnh * nh, axis=-1, keepdims=True))
        du1_ref[...] = du1
        _acc(dcb_ref, jnp.sum(du1, axis=0, keepdims=True), i == 0)
        if nw:
            pl.when(i == n_steps - 1)(finish)

    cv = _sds((1, CONV_CH), F32)
    res = pl.pallas_call(
        body, name="bwd_mix", grid=(n_steps,),
        out_shape=(_sds((t, d), BF16), _sds((t, d), BF16), _sds((t, npad), BF16), _sds((t, HW), BF16),
                   _sds((t, CONV_CH), F32), cv, cv, cv) + _swap_shapes(swap),
        in_specs=[_row(tm, d), _row(tm, 2 * d), _row(tm, d), _row(tm, d), _row(tm, CONV_CH), _full((1, CONV_CH)),
                  _full((1, CONV_CH)), _full(wout.shape), _full(wo_p.shape), _full(wpw.shape)] + [ANY] * nw,
        out_specs=(_row(tm, d), _row(tm, d), _row(tm, 2 * d), _row(tm, HW), _row(tm, CONV_CH),
                   _full((1, CONV_CH)), _full((1, CONV_CH)), _full((1, CONV_CH))) + (ANY,) * nw,
        scratch_shapes=_swap_sems(swap) if nw else [],
        compiler_params=_params("arbitrary"),
    )(dmixed, zgate, ya, yb, u1, lng, lnb, wout, wo_p, wpw, *swap)
    return res[:8] + (res[8:],)


def _bwd_conv(dz, du1, u0, zglu, cw, tm, tps):
    t = du1.shape[0]
    d = (dz.shape[1] - MLA_IN - 2 * CONV_CH) // 2
    p_glu, _, _ = _layout(d)
    hpt = tm // HALO
    last_blk = t // HALO - 1
    cwc = _by_lane_chunk(cw)

    def body(dz_hbm, du_ref, dun_ref, u_ref, zl_ref, cw_ref, dzl_ref, dcw_ref, dext_ref, uc_ref, dcw8_ref, du0_ref):
        i = pl.program_id(0)
        last = (i % tps) == (tps - 1)
        _fill_shifted(dext_ref, du_ref[...], jnp.where(last, 0.0, dun_ref[...]))
        for lc, ls in _lane_chunks():
            uc_ref[lc] = u_ref[:, ls]

        @pl.when(i == 0)
        def _():
            dcw8_ref[...] = jnp.zeros_like(dcw8_ref)

        groups = CONV_ROWS // SUBLANES

        def conv_chunk(c, carry):
            lc, r0 = _conv_chunk(c)
            u = uc_ref[lc, pl.ds(r0, CONV_ROWS), :]
            du0 = jnp.zeros((CONV_ROWS, LANES), F32)
            for kk in range(CONV_W):
                win = _shifted(dext_ref, CONV_W - 1 - kk, lc, r0)
                prod = u * win
                part = prod[:SUBLANES]
                for g in range(1, groups):
                    part = part + prod[g * SUBLANES:(g + 1) * SUBLANES]
                dcw8_ref[lc, kk] += part
                du0 = du0 + cw_ref[lc, kk:kk + 1, :] * win
            du0_ref[lc, pl.ds(r0, CONV_ROWS), :] = du0
            return carry

        lax.fori_loop(0, CONV_LC * (tm // CONV_ROWS), conv_chunk, 0)

        @pl.when(i == pl.num_programs(0) - 1)
        def _():
            for lc, ls in _lane_chunks():
                dcw_ref[:, ls] = jnp.sum(dcw8_ref[lc], axis=1)

        for lc, ls in _lane_chunks():
            du0 = du0_ref[lc]
            ga = zl_ref[:, ls]
            sb = jax.nn.sigmoid(zl_ref[:, CONV_CH + lc * LANES:CONV_CH + (lc + 1) * LANES])
            dzl_ref[:, ls] = (du0 * sb).astype(BF16)
            dzl_ref[:, CONV_CH + lc * LANES:CONV_CH + (lc + 1) * LANES] = ((du0 * ga) * (sb * (1.0 - sb))).astype(BF16)

    nxt = pl.BlockSpec((HALO, CONV_CH), lambda i: (jnp.minimum((i + 1) * hpt, last_blk), 0))
    glu_blk = p_glu // (2 * CONV_CH)
    return pl.pallas_call(
        body, name="bwd_conv", grid=(t // tm,),
        out_shape=(_sds(dz.shape, BF16), _sds(cw.shape, F32)),
        in_specs=[ANY, _row(tm, CONV_CH), nxt, _row(tm, CONV_CH), _row(tm, 2 * CONV_CH), _full(cwc.shape)],
        out_specs=(pl.BlockSpec((tm, 2 * CONV_CH), lambda i: (i, glu_blk)), _full(cw.shape)),
        scratch_shapes=[pltpu.VMEM(_shifted_shape(tm), F32), pltpu.VMEM((CONV_LC, tm, LANES), F32),
                        pltpu.VMEM((CONV_LC, HALO, SUBLANES, LANES), F32), pltpu.VMEM((CONV_LC, tm, LANES), F32)],
        input_output_aliases={0: 0},
        compiler_params=_params("arbitrary"),
    )(dz, du1, du1, u0, zglu, cwc)


def _attn_bwd(q, k, v, do, nseq, seq, scatter=()):
    t = q.shape[0]
    ns = len(scatter)
    blk = pl.BlockSpec((seq, LANES), lambda b, h: (b, h))
    n_steps = nseq * N_HEADS

    def body(q_ref, k_ref, v_ref, do_ref, *rest):
        dq_ref, dk_ref, dv_ref = rest[ns:ns + 3]
        dka_ref, dva_ref = rest[2 * ns + 3:2 * ns + 5]
        if ns:
            start, finish = _scatter_phases(rest[:ns], rest[ns + 3:2 * ns + 3], *rest[2 * ns + 5:])
            step = pl.program_id(0) * N_HEADS + pl.program_id(1)
            pl.when(step == 0)(start)
        dka_ref[...] = jnp.zeros_like(dka_ref)
        dva_ref[...] = jnp.zeros_like(dva_ref)
        mask = _diag_mask()
        nb = seq // BQ
        block = lambda j: (_scores(q_ref[j * BQ:(j + 1) * BQ, :], k_ref, j * BQ, (j + 1) * BQ),
                           _scores(do_ref[j * BQ:(j + 1) * BQ, :], v_ref, j * BQ, (j + 1) * BQ))
        ahead = [block(j) for j in range(min(AHEAD, nb))]
        for i in range(nb):
            lo, e = i * BQ, (i + 1) * BQ
            q_i = q_ref[lo:e, :]
            do_i = do_ref[lo:e, :]
            scores, (dpp, dpd) = ahead.pop(0)
            if i + AHEAD < nb:
                ahead.append(block(i + AHEAD))
            pp, pd, l = _softmax_parts(scores, mask)
            inv = 1.0 / l
            pd = pd * inv
            delta = jnp.sum(pd * dpd, axis=-1, keepdims=True)
            if lo:
                pp = pp * inv
                delta = delta + jnp.sum(pp * dpp, axis=-1, keepdims=True)
            dsd = (pd * (dpd - delta)).astype(BF16)
            dq = _dot(dsd, k_ref[lo:e, :])
            dka_ref[lo:e, :] += _dot_tn(dsd, q_i)
            dva_ref[lo:e, :] += _dot_tn(pd.astype(BF16), do_i)
            if lo:
                dsp = (pp * (dpp - delta)).astype(BF16)
                dq = dq + _dot(dsp, k_ref[:lo, :])
                dka_ref[:lo, :] += _dot_tn(dsp, q_i)
                dva_ref[:lo, :] += _dot_tn(pp.astype(BF16), do_i)
            dq_ref[lo:e, :] = dq * SM_SCALE
        dk_ref[...] = dka_ref[...] * SM_SCALE
        dv_ref[...] = dva_ref[...].astype(BF16)
        if ns:
            pl.when(step == n_steps - 1)(finish)

    res = pl.pallas_call(
        body, name="attn_bwd", grid=(nseq, N_HEADS),
        out_shape=(_sds((t, HW), F32), _sds((t, HW), F32), _sds((t, HW), BF16)) + _scatter_shapes(scatter),
        in_specs=[blk] * 4 + [ANY] * ns, out_specs=(blk,) * 3 + (ANY,) * ns,
        scratch_shapes=[pltpu.VMEM((seq, LANES), F32), pltpu.VMEM((seq, LANES), F32)]
        + (_scatter_sems(ns) if ns else []),
        compiler_params=_params("arbitrary", "arbitrary"),
    )(q, k, v, do, *scatter)
    return res[0], res[1], res[2], res[3:]


def _mla_bwd(dz, dq, dk, dv, zm, gql, gkvl, gq, gk, tabs, wuq_p, wk_p, wv_p, tm, tps):
    t = zm.shape[0]
    d = (dz.shape[1] - MLA_IN - 2 * CONV_CH) // 2
    _, p_q, _ = _layout(d)
    c_t, s1_t, s2_t = tabs
    tab = pl.BlockSpec((tm, LANES), lambda i: (i % tps, 0))

    def body(dz_hbm, dq_ref, dk_ref, dv_ref, zm_ref, gql_ref, gkvl_ref, gq_ref, gk_ref, c_ref, s1_ref, s2_ref,
             wuq_ref, wk_ref, wv_ref,
             dzm_ref, dqpre_ref, dkh_ref, dgq_ref, dgk_ref, dgql_ref, dgkvl_ref):
        i = pl.program_id(0)
        c, s1, s2 = c_ref[...], s1_ref[...], s2_ref[...]
        nq, rq = _rms(zm_ref[:, :Q_RANK])
        qpre = _dot((nq * gql_ref[...]).astype(BF16), wuq_ref[...])
        nkv, rkv = _rms(zm_ref[:, Q_RANK:OFF_KV])
        knope = _dot((nkv * gkvl_ref[...]).astype(BF16), wk_ref[...])
        zkr_v = zm_ref[:, OFF_KV:]
        gk = gk_ref[...]
        kr_roped = _rope(zkr_v * gk, c, s1, s2)
        dgq = jnp.zeros((1, LANES), F32)
        dgk = jnp.zeros((1, LANES), F32)
        dzkr = jnp.zeros((tm, LANES), F32)
        dt_sum = jnp.zeros((tm, LANES), F32)
        slabs = [slice(hd * LANES, (hd + 1) * LANES) for hd in range(N_HEADS)]
        gq = gq_ref[...]
        rqh = [_head_rms(qpre[:, sl])[1] for sl in slabs]
        rkh = [_head_rms(knope[:, sl] + zkr_v)[1] for sl in slabs]
        dyr = [_rope_t(dq_ref[:, sl], c, s1, s2) for sl in slabs]
        nqh = [qpre[:, sl] * rqh[hd] for hd, sl in enumerate(slabs)]
        sq = [jnp.sum((dyr[hd] * gq) * nqh[hd], axis=-1, keepdims=True) for hd in range(N_HEADS)]
        dr = [jnp.sum(dk_ref[:, sl] * (knope[:, sl] * gk + kr_roped), axis=-1, keepdims=True) for sl in slabs]
        for hd, sl in enumerate(slabs):
            dgq = dgq + jnp.sum(dyr[hd] * nqh[hd], axis=0, keepdims=True)
            dqpre_ref[:, sl] = (rqh[hd] * (dyr[hd] * gq - nqh[hd] * (sq[hd] * (1.0 / QK_HEAD)))).astype(BF16)
            kn = knope[:, sl]
            r = rkh[hd]
            dt = dk_ref[:, sl] * r
            via_r = (dr[hd] * (r * r * r) * (-1.0 / QK_HEAD)) * (kn + zkr_v)
            dgk = dgk + jnp.sum(dt * kn, axis=0, keepdims=True)
            dt_sum = dt_sum + dt
            dzkr = dzkr + via_r
            dkh_ref[:, sl] = (dt * gk + via_r).astype(BF16)
        de = _rope_t(dt_sum, c, s1, s2)
        dzkr = dzkr + de * gk
        dgk = dgk + jnp.sum(de * zkr_v, axis=0, keepdims=True)
        _acc(dgq_ref, dgq[:, :QK_HEAD], i == 0)
        _acc(dgk_ref, dgk[:, :QK_HEAD], i == 0)
        dzm_ref[:, OFF_KV:] = dzkr.astype(BF16)
        dqln = _dot_nt(dqpre_ref[...], wuq_ref[...])
        _acc(dgql_ref, jnp.sum(dqln * nq, axis=0, keepdims=True), i == 0)
        dzm_ref[:, :Q_RANK] = _rms_bwd(nq, rq, dqln * gql_ref[...]).astype(BF16)
        dkvn = _dot_nt(dkh_ref[...], wk_ref[...]) + _dot_nt(dv_ref[...], wv_ref[...])
        _acc(dgkvl_ref, jnp.sum(dkvn * nkv, axis=0, keepdims=True), i == 0)
        dzm_ref[:, Q_RANK:OFF_KV] = _rms_bwd(nkv, rkv, dkvn * gkvl_ref[...]).astype(BF16)

    return pl.pallas_call(
        body, name="mla_bwd", grid=(t // tm,),
        out_shape=(_sds(dz.shape, BF16), _sds((t, HW), BF16), _sds((t, HW), BF16), _sds((1, QK_HEAD), F32),
                   _sds((1, QK_HEAD), F32), _sds((1, Q_RANK), F32), _sds((1, KV_RANK), F32)),
        in_specs=[ANY, _row(tm, HW), _row(tm, HW), _row(tm, HW), _row(tm, MLA_IN),
                  _full((1, Q_RANK)), _full((1, KV_RANK)), _full((1, LANES)), _full((1, LANES)), tab, tab, tab,
                  _full(wuq_p.shape), _full(wk_p.shape), _full(wv_p.shape)],
        out_specs=(pl.BlockSpec((tm, MLA_IN), lambda i: (i, p_q // MLA_IN)), _row(tm, HW), _row(tm, HW),
                   _full((1, QK_HEAD)), _full((1, QK_HEAD)), _full((1, Q_RANK)), _full((1, KV_RANK))),
        input_output_aliases={0: 0},
        compiler_params=_params("arbitrary"),
    )(dz, dq, dk, dv, zm, gql, gkvl, gq, gk, c_t, s1_t, s2_t, wuq_p, wk_p, wv_p)


def _bwd_in(dz, x, dx1, g1, mod3, win_p, tm, tps, scatter=()):
    t, d = x.shape
    npad = dz.shape[1]

    ns = len(scatter)
    n_steps = t // tm

    def body(dz_ref, x_ref, dx1_ref, g_ref, mod_ref, wt_hbm, *rest):
        gx_ref, dshift_ref, dscale_ref, dg1_ref = rest[ns:ns + 4]
        wt_ref = rest[2 * ns + 4]
        i = pl.program_id(0)
        if ns:
            start, finish = _scatter_phases(rest[:ns], rest[ns + 4:2 * ns + 4], *rest[2 * ns + 5:])
            pl.when(i == 0)(start)
        _load_resident(i, [(wt_hbm, wt_ref)])
        first_seq = (i % tps) == 0
        g = g_ref[...]
        sc1 = 1.0 + mod_ref[1:2, :]
        nb = max(tm // ROW_BAND, 1)
        bands = [slice(b * (tm // nb), (b + 1) * (tm // nb)) for b in range(nb)]
        dhs = [_dot_nt(dz_ref[rows, :], wt_ref[...]) for rows in bands]
        sums = [jnp.zeros((1, d), F32)] * 3
        col = lambda v: jnp.sum(v, axis=0, keepdims=True)
        for rows, dh in zip(bands, dhs):
            n, r = _rms(x_ref[rows, :])
            sums = [sums[0] + col(dh), sums[1] + col(dh * (n * g)), sums[2] + col((dh * sc1) * n)]
            gx_ref[rows, :] = dx1_ref[rows, :] + _rms_bwd(n, r, (dh * sc1) * g)
        _acc(dshift_ref, sums[0], first_seq)
        _acc(dscale_ref, sums[1], first_seq)
        _acc(dg1_ref, sums[2], i == 0)
        if ns:
            pl.when(i == n_steps - 1)(finish)

    nseq = t // (tm * tps)
    sv = _sds((nseq, 1, d), F32)
    res = pl.pallas_call(
        body, name="bwd_in", grid=(n_steps,),
        out_shape=(_sds((t, d), F32), sv, sv, _sds((1, d), F32)) + _scatter_shapes(scatter),
        in_specs=[_row(tm, npad), _row(tm, d), _row(tm, d), _full((1, d)), _modspec(d, tps), ANY] + [ANY] * ns,
        out_specs=(_row(tm, d), _seqv(d, tps), _seqv(d, tps), _full((1, d))) + (ANY,) * ns,
        scratch_shapes=[pltpu.VMEM(win_p.shape, BF16)] + (_scatter_sems(ns) if ns else []),
        compiler_params=_params("arbitrary"),
    )(dz, x, dx1, g1, mod3, win_p, *scatter)
    return res[0], res[1], res[2], res[3], res[4:]


def _tile_of(n, choices):
    for c in choices:
        if n % c == 0:
            return c
    return n


def _tn_matmul(a, b, name, col_shards=0):
    t, k = a.shape
    n = b.shape[1]
    tk = _tile_of(k, (1024, 512, 256, 128))
    tn = n // col_shards if col_shards else _tile_of(n, (1024, 896, 768, 512, 384, 256, 128))
    tt = _tile_of(t, (4096, 2048, 1024, 512, 256))

    def body(a_ref, b_ref, o_ref):
        _acc(o_ref, _dot_tn(a_ref[...], b_ref[...]), pl.program_id(2) == 0)

    if col_shards:
        out_shape, out_spec = _sds((col_shards, k, tn), F32), pl.BlockSpec((None, tk, tn), lambda i, j, s: (j, i, 0))
    else:
        out_shape, out_spec = _sds((k, n), F32), pl.BlockSpec((tk, tn), lambda i, j, s: (i, j))
    return pl.pallas_call(
        body, name=name, grid=(k // tk, n // tn, t // tt), out_shape=out_shape,
        in_specs=[pl.BlockSpec((tt, tk), lambda i, j, s: (s, i)), pl.BlockSpec((tt, tn), lambda i, j, s: (s, j))],
        out_specs=out_spec, compiler_params=_params("arbitrary", "arbitrary", "arbitrary"),
    )(a, b)


N_SHARD = 4
COL_SHARDED = ("w_in", "w_uq", "w_ukv", "w_o_mla", "w_pw_out", "w_ff1")
ROW_SHARDED = ("w_out", "w_ff2")
BIG = ("w_in", "w_uq", "w_ukv", "w_o_mla", "w_pw_out", "w_out", "w_ff1", "w_ff2")
SMALL = ("norm1_g", "q_latent_g", "kv_latent_g", "qk_norm_q_g", "qk_norm_k_g", "conv_b", "conv_ln_g", "conv_ln_b",
         "norm2_g")
WEIGHTS = ("w_ada", "b_ada", "norm1_g", "w_in", "q_latent_g", "w_uq", "kv_latent_g", "w_ukv", "qk_norm_q_g",
           "qk_norm_k_g", "w_o_mla", "conv_w", "conv_b", "conv_ln_g", "conv_ln_b", "w_pw_out", "w_out", "norm2_g",
           "w_ff1", "w_ff2")


def _pad_heads(w, width):
    k = w.shape[0]
    w3 = w.reshape(k, N_HEADS, width)
    return jnp.pad(w3, ((0, 0), (0, 0), (0, LANES - width))).reshape(k, HW)


def _unpad_heads(g, width):
    k = g.shape[0]
    return g.reshape(k, N_HEADS, LANES)[:, :, :width].reshape(k, N_HEADS * width)


def _win_segments(d):
    return [(OFF_GLU, OFF_GLU + 2 * d), (OFF_KR, OFF_GLU), (0, OFF_KV), KR_LANE, (OFF_KV, OFF_KR),
            LANES - KR_LANE - QK_ROPE]


def _pad_win(g4):
    _, d, ws = g4.shape
    parts = []
    for seg in _win_segments(d):
        if isinstance(seg, int):
            parts.append(jnp.zeros((d, seg), g4.dtype))
            continue
        a, b = seg
        while a < b:
            s = a // ws
            e = min(b, (s + 1) * ws)
            parts.append(g4[s, :, a - s * ws:e - s * ws])
            a = e
    return jnp.concatenate(parts, axis=1)


def _unpad_win(gp):
    d = gp.shape[0]
    ws = (OFF_GLU + 2 * d) // N_SHARD
    pieces, p = [], 0
    for seg in _win_segments(d):
        if isinstance(seg, int):
            p += seg
        else:
            pieces.append((seg[0], seg[1], p))
            p += seg[1] - seg[0]
    shards = []
    for s in range(N_SHARD):
        lo, hi = s * ws, (s + 1) * ws
        cols = [gp[:, p0 + max(a, lo) - a:p0 + min(b, hi) - a] for a, b, p0 in sorted(pieces) if max(a, lo) < min(b, hi)]
        shards.append(jnp.concatenate(cols, axis=1))
    return jnp.stack(shards)


def _col_shards(g):
    k, n = g.shape
    return g.reshape(k, N_SHARD, n // N_SHARD).transpose(1, 0, 2)


def _from_shards(g, name):
    ns, ks, nn = g.shape
    if name in ROW_SHARDED:
        return g.reshape(ns * ks, nn)
    return g.transpose(1, 0, 2).reshape(ks, ns * nn)


BY_SHARD = ("w_in", "w_ff1")
EARLY = ("w_in", "w_uq", "w_ukv")
LATE = ("w_o_mla", "w_pw_out", "w_out", "w_ff1", "w_ff2")


def _assemble(names, gathered):
    by_shard = {n: g.reshape((N_SHARD, 2 * g.shape[1]) + g.shape[2:]) for n, g in zip(names, gathered)}
    return {n: g if n in BY_SHARD else _from_shards(g, n) for n, g in by_shard.items()}


LARGE = ("w_in", "w_ff1", "w_ff2")
GROUP_A = ("w_out", "w_ff1", "w_ff2")
GROUP_B = ("w_in", "w_uq", "w_ukv", "w_o_mla", "w_pw_out")


def _pair_halves(g):
    return g.reshape(N_SHARD, 2, g.shape[1] // 2, g.shape[2])


def _pair_sums(names, halves, from_sibling):
    if not halves:
        return []
    cidx = lax.axis_index("c").reshape(1).astype(jnp.int32)
    out = {n: _add_pair(g, l, cidx, "pair_sum_" + n)
           for n, g, l in zip(names, halves, from_sibling) if n in LARGE}
    small = [j for j, n in enumerate(names) if n not in LARGE]
    if small:
        res = _add_pair_whole([halves[j] for j in small], [from_sibling[j] for j in small], cidx,
                              "pair_sum_small_" + names[small[0]])
        out.update({names[j]: r for j, r in zip(small, res)})
    return [out[n] for n in names]


def _local_step(x, target, mod, sp, w, late=None, tm=256):
    comm = late is not None
    w = dict(w)
    nseq, seq, d = x.shape
    t = nseq * seq
    tps = seq // tm
    xf = x.reshape(t, d)
    tg = target.reshape(t, d)
    mod3 = mod.reshape(nseq, N_MOD, d)

    win_p = _pad_win(w["w_in"])
    wuq_p = _pad_heads(w["w_uq"], QK_HEAD)
    wkv3 = w["w_ukv"].reshape(KV_RANK, N_HEADS, QK_NOPE + V_HEAD)
    wk_p = _pad_heads(wkv3[:, :, :QK_NOPE].reshape(KV_RANK, -1), QK_NOPE)
    wv_p = _pad_heads(wkv3[:, :, QK_NOPE:].reshape(KV_RANK, -1), V_HEAD)
    cw = jnp.pad(w["conv_w"], ((0, HALO - CONV_W), (0, 0)))
    pad_g = lambda g: jnp.pad(g, ((0, 0), (0, LANES - QK_HEAD)))
    gq, gk = pad_g(sp["qk_norm_q_g"]), pad_g(sp["qk_norm_k_g"])
    tabs = _rope_tables(seq)

    tm_in, tps_in = (2 * tm, tps // 2) if tps % 2 == 0 else (tm, tps)
    h, zm, zglu, zgate, u0 = _fwd_in(xf, sp["norm1_g"], mod3, win_p, tm_in, tps_in)
    q, k, v, qln, kvn = _mla_prep(zm, sp["q_latent_g"], sp["kv_latent_g"], gq, gk, tabs, wuq_p, wk_p, wv_p, tm, tps)
    attn, gathered = _attn_fwd(q, k, v, nseq, seq, tuple(late) if comm else ())
    if comm:
        w.update(_assemble(LATE, gathered))
    wo_p = jnp.pad(w["w_o_mla"].reshape(N_HEADS, V_HEAD, d), ((0, 0), (0, LANES - V_HEAD), (0, 0))).reshape(HW, d)
    x1, mixed, mpre, ya, yb, u1, u3 = _fwd_mix(attn, u0, zgate, xf, mod3, wo_p, cw, sp["conv_b"], sp["conv_ln_g"],
                                               sp["conv_ln_b"], w["w_pw_out"], w["w_out"], tm, tps)
    h2, a, r, dy, df, dgate2, loss_acc = _fwd_ffn(x1, tg, sp["norm2_g"], mod3, w["w_ff1"], w["w_ff2"], tm, tps)
    da, dx1, dmixed, dshift2, dscale2, dgate1, dg2 = _bwd_ffn(df, a, x1, dy, mixed, sp["norm2_g"], mod3,
                                                              w["w_ff2"], w["w_ff1"], tm, tps)
    gw = {
        "w_out": _tn_matmul(mpre, dmixed, "dw_out").reshape(N_SHARD, d // N_SHARD, d),
        "w_ff1": _tn_matmul(h2, da, "dw_ff1", N_SHARD),
        "w_ff2": _tn_matmul(r, df, "dw_ff2").reshape(N_SHARD, -1, d),
    }
    halves_a = [_pair_halves(gw[n]) for n in GROUP_A] if comm else []
    dya, dyb, dz, do, du1, dlng, dlnb, dcb, from_sibling = _bwd_mix(
        dmixed, zgate, ya, yb, u1, sp["conv_ln_g"], sp["conv_ln_b"], w["w_out"], wo_p, w["w_pw_out"], tm, tuple(halves_a))
    pair_a = _pair_sums(GROUP_A, halves_a, from_sibling)
    dz, dcw = _bwd_conv(dz, du1, u0, zglu, cw, tm, tps)
    gw["conv_w"] = dcw
    dq, dk, dv, land_a = _attn_bwd(q, k, v, do, nseq, seq, tuple(p[1] for p in pair_a))
    dz, dqpre, dkh, dgq, dgk, dgql, dgkvl = _mla_bwd(dz, dq, dk, dv, zm, sp["q_latent_g"], sp["kv_latent_g"], gq, gk,
                                                      tabs, wuq_p, wk_p, wv_p, tm, tps)
    dwk_p = _tn_matmul(kvn, dkh, "dw_uk")
    dwv_p = _tn_matmul(kvn, dv, "dw_uv")
    dwkv = jnp.concatenate([dwk_p.reshape(KV_RANK, N_HEADS, LANES)[:, :, :QK_NOPE],
                            dwv_p.reshape(KV_RANK, N_HEADS, LANES)[:, :, :V_HEAD]], axis=2).reshape(KV_RANK, -1)
    dwo = _tn_matmul(attn, dya, "dw_o").reshape(N_HEADS, LANES, d)[:, :V_HEAD].reshape(MLA_WIDTH, d)
    gw["w_in"] = _unpad_win(_tn_matmul(h, dz, "dw_in"))
    gw["w_uq"] = _col_shards(_unpad_heads(_tn_matmul(qln, dqpre, "dw_uq"), QK_HEAD))
    gw["w_ukv"] = _col_shards(dwkv)
    gw["w_o_mla"] = _col_shards(dwo)
    gw["w_pw_out"] = _tn_matmul(u3, dyb, "dw_pw", N_SHARD)
    pair_b = []
    if comm:
        halves_b = [_pair_halves(gw[n]) for n in GROUP_B]
        pair_b = _pair_sums(GROUP_B, halves_b, _pair_swap(halves_b, "grad_pair_swap"))
    gx, dshift1, dscale1, dg1, land_b = _bwd_in(dz, xf, dx1, sp["norm1_g"], mod3, win_p, tm_in, tps_in,
                                                tuple(p[1] for p in pair_b))
    if comm:
        for n, p, l in zip(GROUP_A + GROUP_B, pair_a + pair_b, land_a + land_b):
            gw[n] = (p[0], l)
    gs = {
        "norm1_g": dg1, "q_latent_g": dgql, "kv_latent_g": dgkvl, "qk_norm_q_g": dgq, "qk_norm_k_g": dgk,
        "conv_b": dcb, "conv_ln_g": dlng, "conv_ln_b": dlnb, "norm2_g": dg2,
    }
    dmod = jnp.concatenate([dshift1, dscale1, dgate1, dshift2, dscale2, dgate2], axis=2).reshape(nseq, N_MOD * d)
    return loss_acc, gx.reshape(nseq, seq, d), dmod, gw, gs


def kernel(x, c, w_ada, b_ada, norm1_g, w_in, q_latent_g, w_uq, kv_latent_g, w_ukv, qk_norm_q_g, qk_norm_k_g, w_o_mla, conv_w, conv_b, conv_ln_g, conv_ln_b, w_pw_out, w_out, norm2_g, w_ff1, w_ff2, loss_target, m_w_ada, m_b_ada, m_norm1_g, m_w_in, m_q_latent_g, m_w_uq, m_kv_latent_g, m_w_ukv, m_qk_norm_q_g, m_qk_norm_k_g, m_w_o_mla, m_conv_w, m_conv_b, m_conv_ln_g, m_conv_ln_b, m_w_pw_out, m_w_out, m_norm2_g, m_w_ff1, m_w_ff2, v_w_ada, v_b_ada, v_norm1_g, v_w_in, v_q_latent_g, v_w_uq, v_kv_latent_g, v_w_ukv, v_qk_norm_q_g, v_qk_norm_k_g, v_w_o_mla, v_conv_w, v_conv_b, v_conv_ln_g, v_conv_ln_b, v_w_pw_out, v_w_out, v_norm2_g, v_w_ff1, v_w_ff2):
    given = dict(locals())
    wts = {n: given[n][0] for n in WEIGHTS}
    mom = {n: given["m_" + n][0] for n in WEIGHTS}
    var = {n: given["v_" + n][0] for n in WEIGHTS}
    vec = lambda a: a.reshape(1, -1)
    nseq, seq, d = x.shape
    ix, iy, ic = _place()
    shard = 2 * ix + iy

    half = lambda n: lax.dynamic_slice_in_dim(wts[n].astype(BF16), ic * (wts[n].shape[0] // 2), wts[n].shape[0] // 2,
                                              axis=0)
    gathered = _all_gather8([half(n) for n in EARLY] + [wts["conv_w"], c], "gather_weights")
    full = _assemble(EARLY, gathered)
    full["conv_w"] = _from_shards(gathered[-2][0::2], "conv_w")
    c_all = gathered[-1].reshape(8 * nseq, d)

    n_ada = wts["w_ada"].shape[1]
    b_sh = lax.dynamic_slice_in_dim(vec(wts["b_ada"]), shard * n_ada, n_ada, axis=1)
    mod_sh = _ada_mod(c_all, wts["w_ada"], b_sh)
    hb = 4 * nseq
    mod_blk = lax.dynamic_slice_in_dim(mod_sh, ic * hb, hb, axis=0)
    (mod_all,) = _all_gather8([mod_blk], "gather_mod")
    mod_mine = lax.dynamic_slice_in_dim(mod_all, (2 * iy + ic) * nseq, nseq, axis=1)
    mod = jnp.concatenate([lax.dynamic_index_in_dim(mod_mine, 2 * s + ix, axis=0, keepdims=False)
                           for s in range(N_SHARD)], axis=1)

    sp = {n: vec(wts[n]) for n in SMALL}
    loss_part, grad_x, dmod, gw, gs = _local_step(x, loss_target, mod, sp, full, [half(n) for n in LATE])

    own_c = jnp.stack([shard, ic]).astype(jnp.int32)
    mine_sum = {n: _add_chips(gw[n][0], gw[n][1], own_c, "chip_sum_" + n) for n in LARGE}
    few = tuple(n for n in BIG if n not in LARGE)
    mine_sum.update(zip(few, _add_chips_whole([gw[n][0] for n in few], [gw[n][1] for n in few], own_c, "chip_sum_small")))
    summed, parts = _pair_gather_and_all_gather8(
        [mine_sum[n] for n in BIG], [dmod, gw["conv_w"], loss_part] + [gs[n] for n in SMALL], "tail_exchange")

    dmod_all = parts[0].reshape(8 * nseq, N_MOD * d)
    dmod_sh = lax.dynamic_slice_in_dim(dmod_all, shard * n_ada, n_ada, axis=1)
    res = _ada_bwd(c_all, dmod_all, dmod_sh, parts[1:])
    grads = {"w_ada": res[0], "b_ada": res[1]}
    n_cw = wts["conv_w"].shape[1]
    grads["conv_w"] = lax.dynamic_slice_in_dim(res[2], shard * n_cw, n_cw, axis=1)[:CONV_W]
    loss = res[3][0, 0]
    for n, g in zip(SMALL, res[4:]):
        grads[n] = g
    for n, g in zip(BIG, summed):
        grads[n] = g.reshape(wts[n].shape)

    delta, new_m, new_v = {}, {}, {}
    for n in LARGE + ("w_ada",):
        if n == "w_in":
            res = _adamw(wts[n].T, grads[n].T, mom[n].T, var[n].T, "adamw_" + n)
            delta[n], new_m[n], new_v[n] = (a.T for a in res)
        else:
            delta[n], new_m[n], new_v[n] = _adamw(wts[n], grads[n], mom[n], var[n], "adamw_" + n)
    rest = ("b_ada", "conv_w") + SMALL + few
    as2d = lambda a: a if a.ndim == 2 else vec(a)
    res = _adamw_small(*[[as2d(t[n]) for n in rest] for t in (wts, grads, mom, var)])
    for dst, arrs in zip((delta, new_m, new_v), res):
        for n, a in zip(rest, arrs):
            dst[n] = a

    outs = [loss, grad_x]
    for group in (grads, delta, new_m, new_v):
        outs += [group[n].reshape(given[n].shape) for n in WEIGHTS]
    return tuple(outs)
```

```python
import jax
import jax.numpy as jnp
from jax import lax
from jax.experimental import pallas as pl
from jax.experimental.pallas import tpu as pltpu

F32 = jnp.float32
BF16 = jnp.bfloat16
MESH = pl.DeviceIdType.MESH
ANY = pl.BlockSpec(memory_space=pl.ANY)

CHUNK = 64
CHUNK_SHIFT = 6
N_HEADS = 8
QK_NOPE = 64
QK_ROPE = 32
QK_HEAD = QK_NOPE + QK_ROPE
V_HEAD = 64
Q_RANK = 256
KV_RANK = 128
MLA_WIDTH = N_HEADS * V_HEAD
CONV_CH = 512
CONV_W = 31
ROPE_THETA = 10000.0
EPS = 1e-6
LANES = 128
SUBLANES = 8
HW = N_HEADS * LANES
OFF_KV = Q_RANK + KV_RANK
OFF_KR = OFF_KV + QK_ROPE
OFF_GLU = OFF_KR + 2 * CONV_CH
KR_LANE = QK_NOPE
MLA_IN = Q_RANK + KV_RANK + LANES
HALO = 32
N_MOD = 6

ADAM_LR = 0.001
ADAM_B1 = 0.9
ADAM_B2 = 0.999
ADAM_EPS = 1e-08
ADAM_WD = 0.01
ADAM_STEP = 10

VMEM_LIMIT = 56 * 1024 * 1024
BQ = 256


def _layout(d):
    p_glu = 2 * d
    p_q = p_glu + 2 * CONV_CH
    return p_glu, p_q, p_q + MLA_IN


def _params(*sem):
    return pltpu.CompilerParams(dimension_semantics=sem, vmem_limit_bytes=VMEM_LIMIT)


def _dot(a, b):
    return jnp.dot(a, b, preferred_element_type=F32)


def _dot_tn(a, b):
    return lax.dot_general(a, b, (((0,), (0,)), ((), ())), preferred_element_type=F32)


def _dot_nt(a, b):
    return lax.dot_general(a, b, (((1,), (1,)), ((), ())), preferred_element_type=F32)


def _acc(ref, val, first):
    @pl.when(first)
    def _():
        ref[...] = val

    @pl.when(jnp.logical_not(first))
    def _():
        ref[...] += val


def _rms(x):
    r = lax.rsqrt(jnp.mean(x * x, axis=-1, keepdims=True) + EPS)
    return x * r, r


def _rms_bwd(n, r, dn):
    return r * (dn - n * jnp.mean(dn * n, axis=-1, keepdims=True))


def _head_rms(sl):
    r = lax.rsqrt(jnp.sum(sl * sl, axis=-1, keepdims=True) * (1.0 / QK_HEAD) + EPS)
    return sl * r, r


def _head_rms_bwd(n, r, dn):
    return r * (dn - n * (jnp.sum(dn * n, axis=-1, keepdims=True) * (1.0 / QK_HEAD)))


def _rope(x, c, s1, s2):
    return x * c + pltpu.roll(x, QK_ROPE // 2, 1) * s1 + pltpu.roll(x, LANES - QK_ROPE // 2, 1) * s2


def _rope_t(dy, c, s1, s2):
    return dy * c + pltpu.roll(dy * s1, LANES - QK_ROPE // 2, 1) + pltpu.roll(dy * s2, QK_ROPE // 2, 1)


def _rope_tables(seq):
    half = QK_ROPE // 2
    inv_freq = ROPE_THETA ** (-jnp.arange(0, QK_ROPE, 2, dtype=F32) / QK_ROPE)
    ang = jnp.arange(seq, dtype=F32)[:, None] * inv_freq[None, :]
    cos, sin = jnp.cos(ang), jnp.sin(ang)
    z = lambda n: jnp.zeros((seq, n), F32)
    tail = LANES - QK_HEAD
    c = jnp.concatenate([jnp.ones((seq, QK_NOPE), F32), cos, cos, jnp.ones((seq, tail), F32)], axis=1)
    s1 = jnp.concatenate([z(QK_NOPE + half), sin, z(tail)], axis=1)
    s2 = jnp.concatenate([z(QK_NOPE), -sin, z(half + tail)], axis=1)
    return c, s1, s2


def _row(tm, w):
    return pl.BlockSpec((tm, w), lambda i: (i, 0))


def _modspec(d, tps):
    return pl.BlockSpec((None, N_MOD, d), lambda i: (i // tps, 0, 0))


def _seqv(w, tps):
    return pl.BlockSpec((None, 1, w), lambda i: (i // tps, 0, 0))


def _full(shape):
    return pl.BlockSpec(shape, lambda i: tuple(0 for _ in shape))


def _sds(shape, dtype):
    return jax.ShapeDtypeStruct(shape, dtype)


CONV_ROWS = 64
CONV_LC = CONV_CH // LANES


def _lane_chunks():
    return [(lc, slice(lc * LANES, (lc + 1) * LANES)) for lc in range(CONV_LC)]


def _fill_shifted(ext_ref, head, body):
    nh = head.shape[0]
    for lc, ls in _lane_chunks():
        ext_ref[0, lc, :nh, :] = head[:, ls]
        ext_ref[0, lc, nh:, :] = body[:, ls]
        rows = ext_ref[0, lc]
        for b in range(1, SUBLANES):
            ext_ref[b, lc] = pltpu.roll(rows, rows.shape[0] - b, 0)


def _shifted_shape(tm):
    return (SUBLANES, CONV_LC, tm + HALO, LANES)


def _conv_chunk(c):
    return c % CONV_LC, pl.multiple_of((c // CONV_LC) * CONV_ROWS, CONV_ROWS)


def _shifted(ext_ref, o, lc, r0):
    a = pl.multiple_of((o // SUBLANES) * SUBLANES + r0, SUBLANES)
    return ext_ref[o % SUBLANES, lc, pl.ds(a, CONV_ROWS), :]


def _by_lane_chunk(a):
    return a.reshape(a.shape[0], CONV_LC, LANES).transpose(1, 0, 2)


def _load_resident(i, pairs):
    @pl.when(i == 0)
    def _():
        for src, dst in pairs:
            pltpu.sync_copy(src, dst)


def _place():
    return lax.axis_index("x"), lax.axis_index("y"), lax.axis_index("c")


def _all_gather8(blocks, name):
    na = len(blocks)

    def body(*refs):
        start, forward, finish = _gather8_phases(refs[:na], refs[na:2 * na], *refs[2 * na:])
        start()
        forward()
        finish()

    outs = pl.pallas_call(
        body, name=name, out_shape=_gather8_shapes(blocks), in_specs=[ANY] * na, out_specs=(ANY,) * na,
        scratch_shapes=_gather8_sems(na),
    )(*blocks)
    return _own_block_placed(outs, blocks)


def _gather8_shapes(blocks):
    return tuple(_sds((8,) + b.shape, b.dtype) for b in blocks)


def _gather8_sems(na):
    return [pltpu.SemaphoreType.DMA((7 * na,)), pltpu.SemaphoreType.DMA((7 * na,))]


def _own_block_placed(outs, blocks):
    ix, iy, ic = _place()
    return tuple(lax.dynamic_update_index_in_dim(o, b, 4 * ix + 2 * iy + ic, 0) for o, b in zip(outs, blocks))


def _gather8_phases(x_refs, out_refs, send_sems, recv_sems):
    na = len(x_refs)
    x, y, c = _place()
    me, sibling = (x, y, c), (x, y, 1 - c)
    chips = [(1 - x, y), (x, 1 - y), (1 - x, 1 - y)]

    def copy(a, k, blk, to, from_input=False):
        dst = out_refs[a].at[4 * blk[0] + 2 * blk[1] + blk[2]]
        return pltpu.make_async_remote_copy(
            src_ref=x_refs[a] if from_input else dst, dst_ref=dst,
            send_sem=send_sems.at[7 * a + k], recv_sem=recv_sems.at[7 * a + k], device_id=to, device_id_type=MESH)

    def first(a):
        return [copy(a, 0, me, sibling, True)] + [copy(a, 1 + j, me, (*chip, c), True) for j, chip in enumerate(chips)]

    def start():
        for a in range(na):
            for cp in first(a):
                cp.start()

    def forward():
        for j, chip in enumerate(chips):
            for a in range(na):
                copy(a, 1 + j, (*chip, c), me).wait_recv()
                copy(a, 4 + j, (*chip, c), sibling).start()

    def finish():
        for a in range(na):
            copy(a, 0, sibling, me).wait_recv()
            for j, chip in enumerate(chips):
                copy(a, 4 + j, (*chip, 1 - c), me).wait_recv()
        for a in range(na):
            for cp in first(a) + [copy(a, 4 + j, (*chip, c), sibling) for j, chip in enumerate(chips)]:
                cp.wait_send()

    return start, forward, finish


def _pair_swap(gs, name):
    na = len(gs)

    def body(*refs):
        start, finish = _swap_phases(refs[:na], refs[na:2 * na], *refs[2 * na:])
        start()
        finish()

    return pl.pallas_call(
        body, name=name, out_shape=_swap_shapes(gs), in_specs=[ANY] * na, out_specs=(ANY,) * na,
        scratch_shapes=_swap_sems(gs),
    )(*gs)


def _swap_shapes(gs):
    return tuple(_sds(g.shape[:1] + g.shape[2:], g.dtype) for g in gs)


def _swap_sems(gs):
    n = sum(g.shape[0] for g in gs)
    return [pltpu.SemaphoreType.DMA((n,)), pltpu.SemaphoreType.DMA((n,))]


def _swap_phases(g_refs, land_refs, send_sems, recv_sems):
    x, y, c = _place()

    def copies():
        cps, k = [], 0
        for g_ref, land_ref in zip(g_refs, land_refs):
            for s in range(g_ref.shape[0]):
                cps.append(pltpu.make_async_remote_copy(
                    src_ref=g_ref.at[s, 1 - c], dst_ref=land_ref.at[s], send_sem=send_sems.at[k],
                    recv_sem=recv_sems.at[k], device_id=(x, y, 1 - c), device_id_type=MESH))
                k += 1
        return cps

    def start():
        for cp in copies():
            cp.start()

    def finish():
        for cp in copies():
            cp.wait()

    return start, finish


def _scatter_shapes(hs):
    return tuple(_sds((3,) + h.shape[1:], h.dtype) for h in hs)


def _scatter_sems(na):
    return [pltpu.SemaphoreType.DMA((3 * na,)), pltpu.SemaphoreType.DMA((3 * na,))]


def _scatter_phases(h_refs, land_refs, send_sems, recv_sems):
    x, y, c = _place()
    chips = [(1 - x, y), (x, 1 - y), (1 - x, 1 - y)]

    def copies():
        return [pltpu.make_async_remote_copy(
            src_ref=h_refs[a].at[2 * tx + ty], dst_ref=land_refs[a].at[j], send_sem=send_sems.at[3 * a + j],
            recv_sem=recv_sems.at[3 * a + j], device_id=(tx, ty, c), device_id_type=MESH)
            for a in range(len(h_refs)) for j, (tx, ty) in enumerate(chips)]

    def start():
        for cp in copies():
            cp.start()

    def finish():
        for cp in copies():
            cp.wait()

    return start, finish


def _pair_gather_and_all_gather8(fs, blocks, name):
    nf, nb = len(fs), len(blocks)

    def body(*refs):
        f_refs = refs[nf + nb:2 * nf + nb]
        b_out = refs[2 * nf + nb:2 * nf + 2 * nb]
        send_sems, recv_sems, g_send, g_recv = refs[2 * nf + 2 * nb:]
        x, y, c = _place()
        start, forward, finish = _gather8_phases(refs[nf:nf + nb], b_out, g_send, g_recv)
        sends = [pltpu.make_async_remote_copy(
            src_ref=f_refs[a].at[c], dst_ref=f_refs[a].at[c], send_sem=send_sems.at[a], recv_sem=recv_sems.at[a],
            device_id=(x, y, 1 - c), device_id_type=MESH) for a in range(nf)]
        recvs = [pltpu.make_async_remote_copy(
            src_ref=f_refs[a].at[c], dst_ref=f_refs[a].at[1 - c], send_sem=send_sems.at[a],
            recv_sem=recv_sems.at[a], device_id=(x, y, 1 - c), device_id_type=MESH) for a in range(nf)]
        start()
        for cp in sends:
            cp.start()
        forward()
        finish()
        for cp in recvs:
            cp.wait_recv()
        for cp in sends:
            cp.wait_send()

    res = pl.pallas_call(
        body, name=name, out_shape=tuple(_sds(f.shape, f.dtype) for f in fs) + _gather8_shapes(blocks),
        in_specs=[ANY] * (nf + nb), out_specs=(ANY,) * (nf + nb), input_output_aliases={a: a for a in range(nf)},
        scratch_shapes=[pltpu.SemaphoreType.DMA((nf,)), pltpu.SemaphoreType.DMA((nf,))] + _gather8_sems(nb),
    )(*fs, *blocks)
    return res[:nf], _own_block_placed(res[nf:], blocks)


def _row_tile(r, n, itemsize=4, budget=1 << 21):
    if r * n * itemsize <= budget:
        return r
    best = None
    for tr in range(16, r, 16):
        if r % tr == 0 and tr * n * itemsize <= budget:
            best = tr
    assert best is not None, (r, n)
    return best


def _add_pair(g, land, cidx, name):
    ns, _, r, n = g.shape
    tr = _row_tile(r, n)

    def body(c_ref, a_ref, b_ref, o_ref, ob_ref):
        s = a_ref[...] + b_ref[...]
        o_ref[...] = s
        ob_ref[...] = s.astype(BF16)

    out = pl.BlockSpec((None, tr, n), lambda s, i, cr: (s, i, 0))
    return pl.pallas_call(
        body, name=name, out_shape=(_sds((ns, r, n), F32), _sds((ns, r, n), BF16)),
        grid_spec=pltpu.PrefetchScalarGridSpec(
            num_scalar_prefetch=1, grid=(ns, r // tr),
            in_specs=[pl.BlockSpec((None, None, tr, n), lambda s, i, cr: (s, cr[0], i, 0)), out],
            out_specs=(out, out)),
        compiler_params=_params("arbitrary", "arbitrary"),
    )(cidx, g, land)


def _add_pair_whole(gs, lands, cidx, name):
    k = len(gs)

    def body(c_ref, *refs):
        for a_ref, b_ref, o_ref, ob_ref in zip(refs[:k], refs[k:2 * k], refs[2 * k:3 * k], refs[3 * k:]):
            s = a_ref[...] + b_ref[...]
            o_ref[...] = s
            ob_ref[...] = s.astype(BF16)

    half = lambda g: pl.BlockSpec((g.shape[0], None) + g.shape[2:], lambda i, cr: (0, cr[0], 0, 0))
    whole = lambda g: pl.BlockSpec(g.shape[:1] + g.shape[2:], lambda i, cr: (0, 0, 0))
    shapes = lambda dt: tuple(_sds(g.shape[:1] + g.shape[2:], dt) for g in gs)
    res = pl.pallas_call(
        body, name=name, out_shape=shapes(F32) + shapes(BF16),
        grid_spec=pltpu.PrefetchScalarGridSpec(
            num_scalar_prefetch=1, grid=(1,),
            in_specs=[half(g) for g in gs] + [whole(g) for g in gs],
            out_specs=tuple(whole(g) for g in gs) * 2),
        compiler_params=_params("arbitrary"),
    )(cidx, *gs, *lands)
    return list(zip(res[:k], res[k:]))


def _add_chips_whole(hs, lands, own_c, name):
    k = len(hs)

    def body(o_idx, *refs):
        for h_ref, l_ref, o_ref in zip(refs[:k], refs[k:2 * k], refs[2 * k:]):
            o_ref[...] = ((h_ref[...] + l_ref[0].astype(F32)) + l_ref[1].astype(F32)) + l_ref[2].astype(F32)

    return pl.pallas_call(
        body, name=name, out_shape=tuple(_sds((2,) + h.shape[1:], F32) for h in hs),
        grid_spec=pltpu.PrefetchScalarGridSpec(
            num_scalar_prefetch=1, grid=(1,),
            in_specs=[pl.BlockSpec((None,) + h.shape[1:], lambda i, o: (o[0], 0, 0)) for h in hs]
            + [pl.BlockSpec(l.shape, lambda i, o: (0, 0, 0)) for l in lands],
            out_specs=tuple(pl.BlockSpec((None,) + h.shape[1:], lambda i, o: (o[1], 0, 0)) for h in hs)),
        compiler_params=_params("arbitrary"),
    )(own_c, *hs, *lands)


def _add_chips(h, land, own_c, name):
    _, r, n = h.shape
    tr = _row_tile(r, n)

    def body(o_idx, h_ref, l_ref, o_ref):
        o_ref[...] = ((h_ref[...] + l_ref[0].astype(F32)) + l_ref[1].astype(F32)) + l_ref[2].astype(F32)

    return pl.pallas_call(
        body, name=name, out_shape=_sds((2, r, n), F32),
        grid_spec=pltpu.PrefetchScalarGridSpec(
            num_scalar_prefetch=1, grid=(r // tr,),
            in_specs=[pl.BlockSpec((None, tr, n), lambda i, o: (o[0], i, 0)),
                      pl.BlockSpec((3, tr, n), lambda i, o: (0, i, 0))],
            out_specs=pl.BlockSpec((None, tr, n), lambda i, o: (o[1], i, 0))),
        compiler_params=_params("arbitrary"),
    )(own_c, h, land)


def _adam_math(w, g, m, v):
    nm = ADAM_B1 * m + (1.0 - ADAM_B1) * g
    nv = ADAM_B2 * v + (1.0 - ADAM_B2) * (g * g)
    m_hat = nm / (1.0 - ADAM_B1 ** ADAM_STEP)
    v_hat = nv / (1.0 - ADAM_B2 ** ADAM_STEP)
    return -ADAM_LR * (m_hat / (jnp.sqrt(v_hat) + ADAM_EPS) + ADAM_WD * w), nm, nv


def _adamw(w, g, m, v, name):
    r, n = w.shape

    def body(w_ref, g_ref, m_ref, v_ref, d_ref, nm_ref, nv_ref):
        d_ref[...], nm_ref[...], nv_ref[...] = _adam_math(w_ref[...], g_ref[...], m_ref[...], v_ref[...])

    if r % 16 == 0:
        tr = _row_tile(r, n)
        steps, spec = r // tr, pl.BlockSpec((tr, n), lambda i: (i, 0))
    else:
        tc = 4 * LANES
        steps, spec = n // tc, pl.BlockSpec((r, tc), lambda j: (0, j))
    return pl.pallas_call(
        body, name=name, out_shape=(_sds((r, n), F32),) * 3, grid=(steps,),
        in_specs=[spec] * 4, out_specs=(spec,) * 3, compiler_params=_params("arbitrary"),
    )(w, g, m, v)


def _adamw_small(ws, gs, ms, vs):
    k = len(ws)

    def body(*refs):
        ins, outs = refs[:4 * k], refs[4 * k:]
        for j in range(k):
            d, nm, nv = _adam_math(ins[j][...], ins[k + j][...], ins[2 * k + j][...], ins[3 * k + j][...])
            outs[j][...] = d
            outs[k + j][...] = nm
            outs[2 * k + j][...] = nv

    shapes = tuple(_sds(w.shape, F32) for w in ws)
    res = pl.pallas_call(body, name="adamw_small", out_shape=shapes * 3,
                         compiler_params=pltpu.CompilerParams(vmem_limit_bytes=VMEM_LIMIT))(*ws, *gs, *ms, *vs)
    return res[:k], res[k:2 * k], res[2 * k:]


def _ada_mod(c_all, w_sh, b_sh):
    b, _ = c_all.shape
    n = w_sh.shape[1]

    def body(c_ref, w_ref, b_ref, o_ref):
        cc = c_ref[...]
        ca = (cc * jax.nn.sigmoid(cc)).astype(BF16)
        o_ref[...] = _dot(ca, w_ref[...].astype(BF16)) + b_ref[...]

    return pl.pallas_call(body, name="ada_mod", out_shape=_sds((b, n), F32),
                          compiler_params=pltpu.CompilerParams(vmem_limit_bytes=VMEM_LIMIT))(c_all, w_sh, b_sh)


def _ada_bwd(c_all, dmod_all, dmod_sh, parts):
    b, d = c_all.shape
    n6 = dmod_all.shape[1]
    n = dmod_sh.shape[1]
    k = len(parts)

    def body(*refs):
        c_ref, da_ref, ds_ref = refs[:3]
        p_refs = refs[3:3 + k]
        dw_ref, db_ref = refs[3 + k:5 + k]
        s_refs = refs[5 + k:]
        cc = c_ref[...]
        ca = (cc * jax.nn.sigmoid(cc)).astype(BF16)
        dw_ref[...] = _dot_tn(ca, ds_ref[...].astype(BF16))
        db_ref[...] = jnp.sum(da_ref[...], axis=0, keepdims=True)
        for p_ref, s_ref in zip(p_refs, s_refs):
            tot = p_ref[0]
            for j in range(1, p_ref.shape[0]):
                tot = tot + p_ref[j]
            s_ref[...] = tot

    return pl.pallas_call(
        body, name="ada_bwd",
        out_shape=(_sds((d, n), F32), _sds((1, n6), F32)) + tuple(_sds(p.shape[1:], F32) for p in parts),
        compiler_params=pltpu.CompilerParams(vmem_limit_bytes=VMEM_LIMIT),
    )(c_all, dmod_all, dmod_sh, *parts)


def _fwd_in(x, g1, mod3, win_p, tm, tps):
    t, d = x.shape
    p_glu, p_q, npad = _layout(d)

    def body(x_ref, g_ref, mod_ref, w_hbm, h_ref, zm_ref, zglu_ref, zgate_ref, u0_ref, w_ref):
        _load_resident(pl.program_id(0), [(w_hbm, w_ref)])
        n, _ = _rms(x_ref[...])
        h = ((n * g_ref[...]) * (1.0 + mod_ref[1:2, :]) + mod_ref[0:1, :]).astype(BF16)
        h_ref[...] = h
        z = _dot(h, w_ref[...])
        zgate_ref[...] = z[:, :p_glu]
        zglu = z[:, p_glu:p_q]
        zglu_ref[...] = zglu
        zm_ref[...] = z[:, p_q:]
        u0_ref[...] = zglu[:, :CONV_CH] * jax.nn.sigmoid(zglu[:, CONV_CH:])

    return pl.pallas_call(
        body, name="fwd_in", grid=(t // tm,),
        out_shape=(_sds((t, d), BF16), _sds((t, MLA_IN), F32), _sds((t, 2 * CONV_CH), F32), _sds((t, 2 * d), F32),
                   _sds((t, CONV_CH), F32)),
        in_specs=[_row(tm, d), _full((1, d)), _modspec(d, tps), ANY],
        out_specs=(_row(tm, d), _row(tm, MLA_IN), _row(tm, 2 * CONV_CH), _row(tm, 2 * d), _row(tm, CONV_CH)),
        scratch_shapes=[pltpu.VMEM(win_p.shape, BF16)],
        compiler_params=_params("arbitrary"),
    )(x, g1, mod3, win_p)


def _mla_prep(zm, gql, gkvl, gq, gk, tabs, wuq_p, wk_p, wv_p, tm, tps):
    t = zm.shape[0]
    c_t, s1_t, s2_t = tabs
    tab = pl.BlockSpec((tm, LANES), lambda i: (i % tps, 0))

    def body(zm_ref, gql_ref, gkvl_ref, gq_ref, gk_ref, c_ref, s1_ref, s2_ref, wuq_ref, wk_ref, wv_ref,
             q_ref, k_ref, v_ref, qln_ref, kvn_ref):
        c, s1, s2 = c_ref[...], s1_ref[...], s2_ref[...]
        nq, _ = _rms(zm_ref[:, :Q_RANK])
        qln = (nq * gql_ref[...]).astype(BF16)
        qln_ref[...] = qln
        qpre = _dot(qln, wuq_ref[...])
        nkv, _ = _rms(zm_ref[:, Q_RANK:OFF_KV])
        kvn = (nkv * gkvl_ref[...]).astype(BF16)
        kvn_ref[...] = kvn
        knope = _dot(kvn, wk_ref[...])
        v_ref[...] = _dot(kvn, wv_ref[...]).astype(BF16)
        zkr_v = zm_ref[:, OFF_KV:]
        kr_roped = _rope(zkr_v * gk_ref[...], c, s1, s2)
        slabs = [slice(hd * LANES, (hd + 1) * LANES) for hd in range(N_HEADS)]
        rq = [_head_rms(qpre[:, sl])[1] for sl in slabs]
        rk = [_head_rms(knope[:, sl] + zkr_v)[1] for sl in slabs]
        for hd, sl in enumerate(slabs):
            q_ref[:, sl] = _rope((qpre[:, sl] * rq[hd]) * gq_ref[...], c, s1, s2).astype(BF16)
            k_ref[:, sl] = (rk[hd] * (knope[:, sl] * gk_ref[...] + kr_roped)).astype(BF16)

    return pl.pallas_call(
        body, name="mla_prep", grid=(t // tm,),
        out_shape=(_sds((t, HW), BF16),) * 3 + (_sds((t, Q_RANK), BF16), _sds((t, KV_RANK), BF16)),
        in_specs=[_row(tm, MLA_IN), _full((1, Q_RANK)), _full((1, KV_RANK)),
                  _full((1, LANES)), _full((1, LANES)), tab, tab, tab,
                  _full(wuq_p.shape), _full(wk_p.shape), _full(wv_p.shape)],
        out_specs=(_row(tm, HW),) * 3 + (_row(tm, Q_RANK), _row(tm, KV_RANK)),
        compiler_params=_params("arbitrary"),
    )(zm, gql, gkvl, gq, gk, c_t, s1_t, s2_t, wuq_p, wk_p, wv_p)


AHEAD = 2
ROW_BAND = 256
SM_SCALE = QK_HEAD ** -0.5
EXP2_SCALE = SM_SCALE * 1.4426950408889634


def _diag_mask():
    rc = jnp.right_shift(lax.broadcasted_iota(jnp.int32, (BQ, 1), 0), CHUNK_SHIFT)
    cc = jnp.right_shift(lax.broadcasted_iota(jnp.int32, (1, BQ), 1), CHUNK_SHIFT)
    return rc >= cc


def _scores(q_i, k_ref, lo, e):
    return (_dot_nt(q_i, k_ref[:lo, :]) if lo else None), _dot_nt(q_i, k_ref[lo:e, :])


def _softmax_parts(scores, mask):
    sp, sd = scores
    sd = jnp.where(mask, sd, jnp.finfo(F32).min)
    m = jnp.max(sd, axis=-1, keepdims=True)
    if sp is not None:
        m = jnp.maximum(m, jnp.max(sp, axis=-1, keepdims=True))
    pd = jnp.exp2((sd - m) * EXP2_SCALE)
    l = jnp.sum(pd, axis=-1, keepdims=True)
    pp = None
    if sp is not None:
        pp = jnp.exp2((sp - m) * EXP2_SCALE)
        l = l + jnp.sum(pp, axis=-1, keepdims=True)
    return pp, pd, l


def _attn_fwd(q, k, v, nseq, seq, gather=()):
    t = q.shape[0]
    na = len(gather)
    blk = pl.BlockSpec((seq, LANES), lambda b, h: (b, h))
    n_steps = nseq * N_HEADS

    def body(q_ref, k_ref, v_ref, *rest):
        o_ref = rest[na]
        if na:
            start, forward, finish = _gather8_phases(rest[:na], rest[na + 1:2 * na + 1], *rest[2 * na + 1:])
            step = pl.program_id(0) * N_HEADS + pl.program_id(1)
            pl.when(step == 0)(start)
            pl.when(step == (7 * n_steps) // 8)(forward)
        mask = _diag_mask()
        nb = seq // BQ
        block_scores = lambda j: _scores(q_ref[j * BQ:(j + 1) * BQ, :], k_ref, j * BQ, (j + 1) * BQ)
        ahead = [block_scores(j) for j in range(min(AHEAD, nb))]
        for i in range(nb):
            lo, e = i * BQ, (i + 1) * BQ
            cur = ahead.pop(0)
            if i + AHEAD < nb:
                ahead.append(block_scores(i + AHEAD))
            pp, pd, l = _softmax_parts(cur, mask)
            o = _dot(pd.astype(BF16), v_ref[lo:e, :])
            if lo:
                o = o + _dot(pp.astype(BF16), v_ref[:lo, :])
            o_ref[lo:e, :] = (o * (1.0 / l)).astype(BF16)
        if na:
            pl.when(step == n_steps - 1)(finish)

    res = pl.pallas_call(
        body, name="attn_fwd", grid=(nseq, N_HEADS), out_shape=(_sds((t, HW), BF16),) + _gather8_shapes(gather),
        in_specs=[blk, blk, blk] + [ANY] * na, out_specs=(blk,) + (ANY,) * na,
        scratch_shapes=_gather8_sems(na) if na else [],
        compiler_params=_params("arbitrary", "arbitrary"),
    )(q, k, v, *gather)
    return res[0], (_own_block_placed(res[1:], gather) if na else ())


def _fwd_mix(attn, u0, zgate, x, mod3, wo_p, cw, cb, lng, lnb, wpw, wout, tm, tps):
    t, d = x.shape
    hpt = tm // HALO
    cwc, cbc = _by_lane_chunk(cw), _by_lane_chunk(cb)

    def body(a_ref, u_ref, uh_ref, zg_ref, x_ref, mod_ref, wo_ref, cw_ref, cb_ref, lng_ref, lnb_ref, wpw_ref, wout_ref,
             x1_ref, mixed_ref, mpre_ref, ya_ref, yb_ref, u1_ref, u3_ref, ext_ref):
        i = pl.program_id(0)
        ya = _dot(a_ref[...], wo_ref[...])
        ya_ref[...] = ya
        first = (i % tps) == 0
        _fill_shifted(ext_ref, jnp.where(first, 0.0, uh_ref[...]), u_ref[...])
        for lc, ls in _lane_chunks():
            acc = jnp.broadcast_to(cb_ref[lc], (tm, LANES))
            for kk in range(CONV_W):
                o = HALO - (CONV_W - 1) + kk
                a = (o // SUBLANES) * SUBLANES
                acc = acc + cw_ref[lc, kk:kk + 1, :] * ext_ref[o % SUBLANES, lc, a:a + tm, :]
            u1_ref[:, ls] = acc
        acc = u1_ref[...]
        mu = jnp.mean(acc, axis=-1, keepdims=True)
        xc = acc - mu
        rstd = lax.rsqrt(jnp.mean(xc * xc, axis=-1, keepdims=True) + EPS)
        l = (xc * rstd) * lng_ref[...] + lnb_ref[...]
        u3 = (l * jax.nn.sigmoid(l)).astype(BF16)
        u3_ref[...] = u3
        yb = _dot(u3, wpw_ref[...])
        yb_ref[...] = yb
        zg = zg_ref[...]
        mpre = (jax.nn.sigmoid(zg[:, :d]) * ya + jax.nn.sigmoid(zg[:, d:]) * yb).astype(BF16)
        mpre_ref[...] = mpre
        mixed = _dot(mpre, wout_ref[...])
        mixed_ref[...] = mixed
        x1_ref[...] = x_ref[...] + mod_ref[2:3, :] * mixed

    halo = pl.BlockSpec((HALO, CONV_CH), lambda i: (jnp.maximum(i * hpt - 1, 0), 0))
    return pl.pallas_call(
        body, name="fwd_mix", grid=(t // tm,),
        out_shape=(_sds((t, d), F32), _sds((t, d), F32), _sds((t, d), BF16), _sds((t, d), F32), _sds((t, d), F32),
                   _sds((t, CONV_CH), F32), _sds((t, CONV_CH), BF16)),
        in_specs=[_row(tm, HW), _row(tm, CONV_CH), halo, _row(tm, 2 * d), _row(tm, d), _modspec(d, tps),
                  _full(wo_p.shape), _full(cwc.shape), _full(cbc.shape), _full((1, CONV_CH)), _full((1, CONV_CH)),
                  _full(wpw.shape), _full(wout.shape)],
        out_specs=(_row(tm, d), _row(tm, d), _row(tm, d), _row(tm, d), _row(tm, d), _row(tm, CONV_CH),
                   _row(tm, CONV_CH)),
        scratch_shapes=[pltpu.VMEM(_shifted_shape(tm), F32)],
        compiler_params=_params("arbitrary"),
    )(attn, u0, u0, zgate, x, mod3, wo_p, cwc, cbc, lng, lnb, wpw, wout)


def _shards_into_columns(w_hbm, w_ref):
    ns = w_hbm.shape[2]
    return [(w_hbm.at[s], w_ref.at[:, pl.ds(s * ns, ns)]) for s in range(w_hbm.shape[0])]


def _fwd_ffn(x1, target, g2, mod3, w1, w2, tm, tps):
    t, d = x1.shape
    dff = w1.shape[0] * w1.shape[2]

    def body(x1_ref, tg_ref, g_ref, mod_ref, w1_hbm, w2_hbm,
             h2_ref, a_ref, r_ref, dy_ref, df_ref, dgate_ref, loss_ref, w1_ref, w2_ref):
        i = pl.program_id(0)
        _load_resident(i, _shards_into_columns(w1_hbm, w1_ref) + [(w2_hbm, w2_ref)])
        x1v = x1_ref[...]
        gate2 = mod_ref[5:6, :]
        n, _ = _rms(x1v)
        h2 = ((n * g_ref[...]) * (1.0 + mod_ref[4:5, :]) + mod_ref[3:4, :]).astype(BF16)
        h2_ref[...] = h2
        a = _dot(h2, w1_ref[...])
        a_ref[...] = a
        r = jnp.square(jnp.maximum(a, 0.0)).astype(BF16)
        r_ref[...] = r
        f = _dot(r, w2_ref[...])
        e = (x1v + gate2 * f) - tg_ref[...]
        part = 0.5 * jnp.sum(jnp.mean(e * e, axis=-1, keepdims=True), axis=0, keepdims=True)
        _acc(loss_ref, jnp.broadcast_to(part, loss_ref.shape), i == 0)
        dy = e * (1.0 / d)
        dy_ref[...] = dy
        df_ref[...] = (dy * gate2).astype(BF16)
        _acc(dgate_ref, jnp.sum(dy * f, axis=0, keepdims=True), (i % tps) == 0)

    nseq = t // (tm * tps)
    return pl.pallas_call(
        body, name="fwd_ffn", grid=(t // tm,),
        out_shape=(_sds((t, d), BF16), _sds((t, dff), F32), _sds((t, dff), BF16), _sds((t, d), F32), _sds((t, d), BF16),
                   _sds((nseq, 1, d), F32), _sds((8, LANES), F32)),
        in_specs=[_row(tm, d), _row(tm, d), _full((1, d)), _modspec(d, tps), ANY, ANY],
        out_specs=(_row(tm, d), _row(tm, dff), _row(tm, dff), _row(tm, d), _row(tm, d), _seqv(d, tps),
                   _full((8, LANES))),
        scratch_shapes=[pltpu.VMEM((d, dff), BF16), pltpu.VMEM(w2.shape, BF16)],
        compiler_params=_params("arbitrary"),
    )(x1, target, g2, mod3, w1, w2)


def _bwd_ffn(df, a, x1, dy, mixed, g2, mod3, w2, w1, tm, tps):
    t, d = x1.shape
    dff = a.shape[1]

    def body(df_ref, a_ref, x1_ref, dy_ref, mx_ref, g_ref, mod_ref, w2_hbm, w1_hbm,
             da_ref, dx1_ref, dmixed_ref, dshift_ref, dscale_ref, dgate1_ref, dg2_ref, w2_ref, w1_ref):
        i = pl.program_id(0)
        _load_resident(i, [(w2_hbm, w2_ref)] + _shards_into_columns(w1_hbm, w1_ref))
        first_seq = (i % tps) == 0
        dr = _dot_nt(df_ref[...], w2_ref[...])
        da = (dr * (2.0 * jnp.maximum(a_ref[...], 0.0))).astype(BF16)
        da_ref[...] = da
        dh2 = _dot_nt(da, w1_ref[...])
        n, r = _rms(x1_ref[...])
        g = g_ref[...]
        sc1 = 1.0 + mod_ref[4:5, :]
        _acc(dshift_ref, jnp.sum(dh2, axis=0, keepdims=True), first_seq)
        _acc(dscale_ref, jnp.sum(dh2 * (n * g), axis=0, keepdims=True), first_seq)
        _acc(dg2_ref, jnp.sum((dh2 * sc1) * n, axis=0, keepdims=True), i == 0)
        dx1 = dy_ref[...] + _rms_bwd(n, r, (dh2 * sc1) * g)
        dx1_ref[...] = dx1
        _acc(dgate1_ref, jnp.sum(dx1 * mx_ref[...], axis=0, keepdims=True), first_seq)
        dmixed_ref[...] = (dx1 * mod_ref[2:3, :]).astype(BF16)

    nseq = t // (tm * tps)
    sv = _sds((nseq, 1, d), F32)
    return pl.pallas_call(
        body, name="bwd_ffn", grid=(t // tm,),
        out_shape=(_sds((t, dff), BF16), _sds((t, d), F32), _sds((t, d), BF16), sv, sv, sv, _sds((1, d), F32)),
        in_specs=[_row(tm, d), _row(tm, dff), _row(tm, d), _row(tm, d), _row(tm, d), _full((1, d)), _modspec(d, tps),
                  ANY, ANY],
        out_specs=(_row(tm, dff), _row(tm, d), _row(tm, d), _seqv(d, tps), _seqv(d, tps), _seqv(d, tps),
                   _full((1, d))),
        scratch_shapes=[pltpu.VMEM(w2.shape, BF16), pltpu.VMEM((d, dff), BF16)],
        compiler_params=_params("arbitrary"),
    )(df, a, x1, dy, mixed, g2, mod3, w2, w1)


def _bwd_mix(dmixed, zgate, ya, yb, u1, lng, lnb, wout, wo_p, wpw, tm, swap=()):
    t, d = ya.shape
    _, _, npad = _layout(d)
    nw = len(swap)
    n_steps = t // tm

    def body(dm_ref, zg_ref, ya_ref, yb_ref, u1_ref, lng_ref, lnb_ref, wout_ref, wo_ref, wpw_ref, *rest):
        dya_ref, dyb_ref, dz_ref, do_ref, du1_ref, dlng_ref, dlnb_ref, dcb_ref = rest[nw:nw + 8]
        i = pl.program_id(0)
        if nw:
            start, finish = _swap_phases(rest[:nw], rest[nw + 8:2 * nw + 8], *rest[2 * nw + 8:])
            pl.when(i == 0)(start)
        nb = max(tm // ROW_BAND, 1)
        bands = [slice(b * (tm // nb), (b + 1) * (tm // nb)) for b in range(nb)]
        col = lambda v: jnp.sum(v, axis=0, keepdims=True)
        dmpre = [_dot_nt(dm_ref[rows, :], wout_ref[...]) for rows in bands]
        dyab = []
        for rows, dmp in zip(bands, dmpre):
            ga = jax.nn.sigmoid(zg_ref[rows, :d])
            gb = jax.nn.sigmoid(zg_ref[rows, d:])
            dya = (dmp * ga).astype(BF16)
            dyb = (dmp * gb).astype(BF16)
            dya_ref[rows, :] = dya
            dyb_ref[rows, :] = dyb
            dz_ref[rows, :d] = ((dmp * ya_ref[rows, :]) * (ga * (1.0 - ga))).astype(BF16)
            dz_ref[rows, d:] = ((dmp * yb_ref[rows, :]) * (gb * (1.0 - gb))).astype(BF16)
            dyab.append((dya, dyb))
        du3s = []
        for rows, (dya, dyb) in zip(bands, dyab):
            do_ref[rows, :] = _dot_nt(dya, wo_ref[...]).astype(BF16)
            du3s.append(_dot_nt(dyb, wpw_ref[...]))
        sums = [jnp.zeros((1, CONV_CH), F32)] * 3
        for rows, du3 in zip(bands, du3s):
            u1 = u1_ref[rows, :]
            mu = jnp.mean(u1, axis=-1, keepdims=True)
            xc = u1 - mu
            rstd = lax.rsqrt(jnp.mean(xc * xc, axis=-1, keepdims=True) + EPS)
            nh = xc * rstd
            l = nh * lng_ref[...] + lnb_ref[...]
            sg = jax.nn.sigmoid(l)
            dl = du3 * (sg * (1.0 + l * (1.0 - sg)))
            dnh = dl * lng_ref[...]
            du1 = rstd * (dnh - jnp.mean(dnh, axis=-1, keepdims=True)
                          - nh * jnp.mean(dnh * nh, axis=-1, keepdims=True))
            du1_ref[rows, :] = du1
            sums = [sums[0] + col(dl * nh), sums[1] + col(dl), sums[2] + col(du1)]
        _acc(dlng_ref, sums[0], i == 0)
        _acc(dlnb_ref, sums[1], i == 0)
        _acc(dcb_ref, sums[2], i == 0)
        if nw:
            pl.when(i == n_steps - 1)(finish)

    cv = _sds((1, CONV_CH), F32)
    res = pl.pallas_call(
        body, name="bwd_mix", grid=(n_steps,),
        out_shape=(_sds((t, d), BF16), _sds((t, d), BF16), _sds((t, npad), BF16), _sds((t, HW), BF16),
                   _sds((t, CONV_CH), F32), cv, cv, cv) + _swap_shapes(swap),
        in_specs=[_row(tm, d), _row(tm, 2 * d), _row(tm, d), _row(tm, d), _row(tm, CONV_CH), _full((1, CONV_CH)),
                  _full((1, CONV_CH)), _full(wout.shape), _full(wo_p.shape), _full(wpw.shape)] + [ANY] * nw,
        out_specs=(_row(tm, d), _row(tm, d), _row(tm, 2 * d), _row(tm, HW), _row(tm, CONV_CH),
                   _full((1, CONV_CH)), _full((1, CONV_CH)), _full((1, CONV_CH))) + (ANY,) * nw,
        scratch_shapes=_swap_sems(swap) if nw else [],
        compiler_params=_params("arbitrary"),
    )(dmixed, zgate, ya, yb, u1, lng, lnb, wout, wo_p, wpw, *swap)
    return res[:8] + (res[8:],)


def _bwd_conv(dz, du1, u0, zglu, cw, tm, tps):
    t = du1.shape[0]
    d = (dz.shape[1] - MLA_IN - 2 * CONV_CH) // 2
    p_glu, _, _ = _layout(d)
    hpt = tm // HALO
    last_blk = t // HALO - 1
    cwc = _by_lane_chunk(cw)

    def body(dz_hbm, du_ref, dun_ref, u_ref, zl_ref, cw_ref, dzl_ref, dcw_ref, dext_ref, uc_ref, dcw8_ref, du0_ref):
        i = pl.program_id(0)
        last = (i % tps) == (tps - 1)
        _fill_shifted(dext_ref, du_ref[...], jnp.where(last, 0.0, dun_ref[...]))
        for lc, ls in _lane_chunks():
            uc_ref[lc] = u_ref[:, ls]

        @pl.when(i == 0)
        def _():
            dcw8_ref[...] = jnp.zeros_like(dcw8_ref)

        groups = CONV_ROWS // SUBLANES

        def conv_chunk(c, carry):
            lc, r0 = _conv_chunk(c)
            u = uc_ref[lc, pl.ds(r0, CONV_ROWS), :]
            du0 = jnp.zeros((CONV_ROWS, LANES), F32)
            for kk in range(CONV_W):
                win = _shifted(dext_ref, CONV_W - 1 - kk, lc, r0)
                prod = u * win
                part = prod[:SUBLANES]
                for g in range(1, groups):
                    part = part + prod[g * SUBLANES:(g + 1) * SUBLANES]
                dcw8_ref[lc, kk] += part
                du0 = du0 + cw_ref[lc, kk:kk + 1, :] * win
            du0_ref[lc, pl.ds(r0, CONV_ROWS), :] = du0
            return carry

        lax.fori_loop(0, CONV_LC * (tm // CONV_ROWS), conv_chunk, 0)

        @pl.when(i == pl.num_programs(0) - 1)
        def _():
            for lc, ls in _lane_chunks():
                dcw_ref[:, ls] = jnp.sum(dcw8_ref[lc], axis=1)

        for lc, ls in _lane_chunks():
            du0 = du0_ref[lc]
            ga = zl_ref[:, ls]
            sb = jax.nn.sigmoid(zl_ref[:, CONV_CH + lc * LANES:CONV_CH + (lc + 1) * LANES])
            dzl_ref[:, ls] = (du0 * sb).astype(BF16)
            dzl_ref[:, CONV_CH + lc * LANES:CONV_CH + (lc + 1) * LANES] = ((du0 * ga) * (sb * (1.0 - sb))).astype(BF16)

    nxt = pl.BlockSpec((HALO, CONV_CH), lambda i: (jnp.minimum((i + 1) * hpt, last_blk), 0))
    glu_blk = p_glu // (2 * CONV_CH)
    return pl.pallas_call(
        body, name="bwd_conv", grid=(t // tm,),
        out_shape=(_sds(dz.shape, BF16), _sds(cw.shape, F32)),
        in_specs=[ANY, _row(tm, CONV_CH), nxt, _row(tm, CONV_CH), _row(tm, 2 * CONV_CH), _full(cwc.shape)],
        out_specs=(pl.BlockSpec((tm, 2 * CONV_CH), lambda i: (i, glu_blk)), _full(cw.shape)),
        scratch_shapes=[pltpu.VMEM(_shifted_shape(tm), F32), pltpu.VMEM((CONV_LC, tm, LANES), F32),
                        pltpu.VMEM((CONV_LC, HALO, SUBLANES, LANES), F32), pltpu.VMEM((CONV_LC, tm, LANES), F32)],
        input_output_aliases={0: 0},
        compiler_params=_params("arbitrary"),
    )(dz, du1, du1, u0, zglu, cwc)


def _attn_bwd(q, k, v, do, nseq, seq, scatter=()):
    t = q.shape[0]
    ns = len(scatter)
    blk = pl.BlockSpec((seq, LANES), lambda b, h: (b, h))
    n_steps = nseq * N_HEADS

    def body(q_ref, k_ref, v_ref, do_ref, *rest):
        dq_ref, dk_ref, dv_ref = rest[ns:ns + 3]
        dka_ref, dva_ref = rest[2 * ns + 3:2 * ns + 5]
        if ns:
            start, finish = _scatter_phases(rest[:ns], rest[ns + 3:2 * ns + 3], *rest[2 * ns + 5:])
            step = pl.program_id(0) * N_HEADS + pl.program_id(1)
            pl.when(step == 0)(start)
        dka_ref[...] = jnp.zeros_like(dka_ref)
        dva_ref[...] = jnp.zeros_like(dva_ref)
        mask = _diag_mask()
        nb = seq // BQ
        block = lambda j: (_scores(q_ref[j * BQ:(j + 1) * BQ, :], k_ref, j * BQ, (j + 1) * BQ),
                           _scores(do_ref[j * BQ:(j + 1) * BQ, :], v_ref, j * BQ, (j + 1) * BQ))
        ahead = [block(j) for j in range(min(AHEAD, nb))]
        for i in range(nb):
            lo, e = i * BQ, (i + 1) * BQ
            q_i = q_ref[lo:e, :]
            do_i = do_ref[lo:e, :]
            scores, (dpp, dpd) = ahead.pop(0)
            if i + AHEAD < nb:
                ahead.append(block(i + AHEAD))
            pp, pd, l = _softmax_parts(scores, mask)
            inv = 1.0 / l
            pd = pd * inv
            delta = jnp.sum(pd * dpd, axis=-1, keepdims=True)
            if lo:
                pp = pp * inv
                delta = delta + jnp.sum(pp * dpp, axis=-1, keepdims=True)
            dsd = (pd * (dpd - delta)).astype(BF16)
            dq = _dot(dsd, k_ref[lo:e, :])
            dka_ref[lo:e, :] += _dot_tn(dsd, q_i)
            dva_ref[lo:e, :] += _dot_tn(pd.astype(BF16), do_i)
            if lo:
                dsp = (pp * (dpp - delta)).astype(BF16)
                dq = dq + _dot(dsp, k_ref[:lo, :])
                dka_ref[:lo, :] += _dot_tn(dsp, q_i)
                dva_ref[:lo, :] += _dot_tn(pp.astype(BF16), do_i)
            dq_ref[lo:e, :] = dq * SM_SCALE
        dk_ref[...] = dka_ref[...] * SM_SCALE
        dv_ref[...] = dva_ref[...].astype(BF16)
        if ns:
            pl.when(step == n_steps - 1)(finish)

    res = pl.pallas_call(
        body, name="attn_bwd", grid=(nseq, N_HEADS),
        out_shape=(_sds((t, HW), F32), _sds((t, HW), F32), _sds((t, HW), BF16)) + _scatter_shapes(scatter),
        in_specs=[blk] * 4 + [ANY] * ns, out_specs=(blk,) * 3 + (ANY,) * ns,
        scratch_shapes=[pltpu.VMEM((seq, LANES), F32), pltpu.VMEM((seq, LANES), F32)]
        + (_scatter_sems(ns) if ns else []),
        compiler_params=_params("arbitrary", "arbitrary"),
    )(q, k, v, do, *scatter)
    return res[0], res[1], res[2], res[3:]


def _mla_bwd(dz, dq, dk, dv, zm, gql, gkvl, gq, gk, tabs, wuq_p, wk_p, wv_p, tm, tps):
    t = zm.shape[0]
    d = (dz.shape[1] - MLA_IN - 2 * CONV_CH) // 2
    _, p_q, _ = _layout(d)
    c_t, s1_t, s2_t = tabs
    tab = pl.BlockSpec((tm, LANES), lambda i: (i % tps, 0))

    def body(dz_hbm, dq_ref, dk_ref, dv_ref, zm_ref, gql_ref, gkvl_ref, gq_ref, gk_ref, c_ref, s1_ref, s2_ref,
             wuq_ref, wk_ref, wv_ref,
             dzm_ref, dqpre_ref, dkh_ref, dgq_ref, dgk_ref, dgql_ref, dgkvl_ref):
        i = pl.program_id(0)
        c, s1, s2 = c_ref[...], s1_ref[...], s2_ref[...]
        nq, rq = _rms(zm_ref[:, :Q_RANK])
        qpre = _dot((nq * gql_ref[...]).astype(BF16), wuq_ref[...])
        nkv, rkv = _rms(zm_ref[:, Q_RANK:OFF_KV])
        knope = _dot((nkv * gkvl_ref[...]).astype(BF16), wk_ref[...])
        zkr_v = zm_ref[:, OFF_KV:]
        gk = gk_ref[...]
        kr_roped = _rope(zkr_v * gk, c, s1, s2)
        dgq = jnp.zeros((1, LANES), F32)
        dgk = jnp.zeros((1, LANES), F32)
        dzkr = jnp.zeros((tm, LANES), F32)
        dt_sum = jnp.zeros((tm, LANES), F32)
        slabs = [slice(hd * LANES, (hd + 1) * LANES) for hd in range(N_HEADS)]
        gq = gq_ref[...]
        rqh = [_head_rms(qpre[:, sl])[1] for sl in slabs]
        rkh = [_head_rms(knope[:, sl] + zkr_v)[1] for sl in slabs]
        dyr = [_rope_t(dq_ref[:, sl], c, s1, s2) for sl in slabs]
        nqh = [qpre[:, sl] * rqh[hd] for hd, sl in enumerate(slabs)]
        sq = [jnp.sum((dyr[hd] * gq) * nqh[hd], axis=-1, keepdims=True) for hd in range(N_HEADS)]
        dr = [jnp.sum(dk_ref[:, sl] * (knope[:, sl] * gk + kr_roped), axis=-1, keepdims=True) for sl in slabs]
        for hd, sl in enumerate(slabs):
            dgq = dgq + jnp.sum(dyr[hd] * nqh[hd], axis=0, keepdims=True)
            dqpre_ref[:, sl] = (rqh[hd] * (dyr[hd] * gq - nqh[hd] * (sq[hd] * (1.0 / QK_HEAD)))).astype(BF16)
            kn = knope[:, sl]
            r = rkh[hd]
            dt = dk_ref[:, sl] * r
            via_r = (dr[hd] * (r * r * r) * (-1.0 / QK_HEAD)) * (kn + zkr_v)
            dgk = dgk + jnp.sum(dt * kn, axis=0, keepdims=True)
            dt_sum = dt_sum + dt
            dzkr = dzkr + via_r
            dkh_ref[:, sl] = (dt * gk + via_r).astype(BF16)
        de = _rope_t(dt_sum, c, s1, s2)
        dzkr = dzkr + de * gk
        dgk = dgk + jnp.sum(de * zkr_v, axis=0, keepdims=True)
        _acc(dgq_ref, dgq[:, :QK_HEAD], i == 0)
        _acc(dgk_ref, dgk[:, :QK_HEAD], i == 0)
        dzm_ref[:, OFF_KV:] = dzkr.astype(BF16)
        dqln = _dot_nt(dqpre_ref[...], wuq_ref[...])
        _acc(dgql_ref, jnp.sum(dqln * nq, axis=0, keepdims=True), i == 0)
        dzm_ref[:, :Q_RANK] = _rms_bwd(nq, rq, dqln * gql_ref[...]).astype(BF16)
        dkvn = _dot_nt(dkh_ref[...], wk_ref[...]) + _dot_nt(dv_ref[...], wv_ref[...])
        _acc(dgkvl_ref, jnp.sum(dkvn * nkv, axis=0, keepdims=True), i == 0)
        dzm_ref[:, Q_RANK:OFF_KV] = _rms_bwd(nkv, rkv, dkvn * gkvl_ref[...]).astype(BF16)

    return pl.pallas_call(
        body, name="mla_bwd", grid=(t // tm,),
        out_shape=(_sds(dz.shape, BF16), _sds((t, HW), BF16), _sds((t, HW), BF16), _sds((1, QK_HEAD), F32),
                   _sds((1, QK_HEAD), F32), _sds((1, Q_RANK), F32), _sds((1, KV_RANK), F32)),
        in_specs=[ANY, _row(tm, HW), _row(tm, HW), _row(tm, HW), _row(tm, MLA_IN),
                  _full((1, Q_RANK)), _full((1, KV_RANK)), _full((1, LANES)), _full((1, LANES)), tab, tab, tab,
                  _full(wuq_p.shape), _full(wk_p.shape), _full(wv_p.shape)],
        out_specs=(pl.BlockSpec((tm, MLA_IN), lambda i: (i, p_q // MLA_IN)), _row(tm, HW), _row(tm, HW),
                   _full((1, QK_HEAD)), _full((1, QK_HEAD)), _full((1, Q_RANK)), _full((1, KV_RANK))),
        input_output_aliases={0: 0},
        compiler_params=_params("arbitrary"),
    )(dz, dq, dk, dv, zm, gql, gkvl, gq, gk, c_t, s1_t, s2_t, wuq_p, wk_p, wv_p)


def _bwd_in(dz, x, dx1, g1, mod3, win_p, tm, tps, scatter=()):
    t, d = x.shape
    npad = dz.shape[1]

    ns = len(scatter)
    n_steps = t // tm

    def body(dz_ref, x_ref, dx1_ref, g_ref, mod_ref, wt_hbm, *rest):
        gx_ref, dshift_ref, dscale_ref, dg1_ref = rest[ns:ns + 4]
        wt_ref = rest[2 * ns + 4]
        i = pl.program_id(0)
        if ns:
            start, finish = _scatter_phases(rest[:ns], rest[ns + 4:2 * ns + 4], *rest[2 * ns + 5:])
            pl.when(i == 0)(start)
        _load_resident(i, [(wt_hbm, wt_ref)])
        first_seq = (i % tps) == 0
        g = g_ref[...]
        sc1 = 1.0 + mod_ref[1:2, :]
        nb = max(tm // ROW_BAND, 1)
        bands = [slice(b * (tm // nb), (b + 1) * (tm // nb)) for b in range(nb)]
        dhs = [_dot_nt(dz_ref[rows, :], wt_ref[...]) for rows in bands]
        sums = [jnp.zeros((1, d), F32)] * 3
        col = lambda v: jnp.sum(v, axis=0, keepdims=True)
        for rows, dh in zip(bands, dhs):
            n, r = _rms(x_ref[rows, :])
            sums = [sums[0] + col(dh), sums[1] + col(dh * (n * g)), sums[2] + col((dh * sc1) * n)]
            gx_ref[rows, :] = dx1_ref[rows, :] + _rms_bwd(n, r, (dh * sc1) * g)
        _acc(dshift_ref, sums[0], first_seq)
        _acc(dscale_ref, sums[1], first_seq)
        _acc(dg1_ref, sums[2], i == 0)
        if ns:
            pl.when(i == n_steps - 1)(finish)

    nseq = t // (tm * tps)
    sv = _sds((nseq, 1, d), F32)
    res = pl.pallas_call(
        body, name="bwd_in", grid=(n_steps,),
        out_shape=(_sds((t, d), F32), sv, sv, _sds((1, d), F32)) + _scatter_shapes(scatter),
        in_specs=[_row(tm, npad), _row(tm, d), _row(tm, d), _full((1, d)), _modspec(d, tps), ANY] + [ANY] * ns,
        out_specs=(_row(tm, d), _seqv(d, tps), _seqv(d, tps), _full((1, d))) + (ANY,) * ns,
        scratch_shapes=[pltpu.VMEM(win_p.shape, BF16)] + (_scatter_sems(ns) if ns else []),
        compiler_params=_params("arbitrary"),
    )(dz, x, dx1, g1, mod3, win_p, *scatter)
    return res[0], res[1], res[2], res[3], res[4:]


def _tile_of(n, choices):
    for c in choices:
        if n % c == 0:
            return c
    return n


def _tn_matmul(a, b, name, col_shards=0):
    t, k = a.shape
    n = b.shape[1]
    tk = _tile_of(k, (1024, 512, 256, 128))
    tn = n // col_shards if col_shards else _tile_of(n, (1024, 896, 768, 512, 384, 256, 128))
    tt = _tile_of(t, (4096, 2048, 1024, 512, 256))

    def body(a_ref, b_ref, o_ref):
        _acc(o_ref, _dot_tn(a_ref[...], b_ref[...]), pl.program_id(2) == 0)

    if col_shards:
        out_shape, out_spec = _sds((col_shards, k, tn), F32), pl.BlockSpec((None, tk, tn), lambda i, j, s: (j, i, 0))
    else:
        out_shape, out_spec = _sds((k, n), F32), pl.BlockSpec((tk, tn), lambda i, j, s: (i, j))
    return pl.pallas_call(
        body, name=name, grid=(k // tk, n // tn, t // tt), out_shape=out_shape,
        in_specs=[pl.BlockSpec((tt, tk), lambda i, j, s: (s, i)), pl.BlockSpec((tt, tn), lambda i, j, s: (s, j))],
        out_specs=out_spec, compiler_params=_params("arbitrary", "arbitrary", "arbitrary"),
    )(a, b)


N_SHARD = 4
COL_SHARDED = ("w_in", "w_uq", "w_ukv", "w_o_mla", "w_pw_out", "w_ff1")
ROW_SHARDED = ("w_out", "w_ff2")
BIG = ("w_in", "w_uq", "w_ukv", "w_o_mla", "w_pw_out", "w_out", "w_ff1", "w_ff2")
SMALL = ("norm1_g", "q_latent_g", "kv_latent_g", "qk_norm_q_g", "qk_norm_k_g", "conv_b", "conv_ln_g", "conv_ln_b",
         "norm2_g")
WEIGHTS = ("w_ada", "b_ada", "norm1_g", "w_in", "q_latent_g", "w_uq", "kv_latent_g", "w_ukv", "qk_norm_q_g",
           "qk_norm_k_g", "w_o_mla", "conv_w", "conv_b", "conv_ln_g", "conv_ln_b", "w_pw_out", "w_out", "norm2_g",
           "w_ff1", "w_ff2")


def _pad_heads(w, width):
    k = w.shape[0]
    w3 = w.reshape(k, N_HEADS, width)
    return jnp.pad(w3, ((0, 0), (0, 0), (0, LANES - width))).reshape(k, HW)


def _unpad_heads(g, width):
    k = g.shape[0]
    return g.reshape(k, N_HEADS, LANES)[:, :, :width].reshape(k, N_HEADS * width)


def _win_segments(d):
    return [(OFF_GLU, OFF_GLU + 2 * d), (OFF_KR, OFF_GLU), (0, OFF_KV), KR_LANE, (OFF_KV, OFF_KR),
            LANES - KR_LANE - QK_ROPE]


def _pad_win(g4):
    _, d, ws = g4.shape
    parts = []
    for seg in _win_segments(d):
        if isinstance(seg, int):
            parts.append(jnp.zeros((d, seg), g4.dtype))
            continue
        a, b = seg
        while a < b:
            s = a // ws
            e = min(b, (s + 1) * ws)
            parts.append(g4[s, :, a - s * ws:e - s * ws])
            a = e
    return jnp.concatenate(parts, axis=1)


def _unpad_win(gp):
    d = gp.shape[0]
    ws = (OFF_GLU + 2 * d) // N_SHARD
    pieces, p = [], 0
    for seg in _win_segments(d):
        if isinstance(seg, int):
            p += seg
        else:
            pieces.append((seg[0], seg[1], p))
            p += seg[1] - seg[0]
    shards = []
    for s in range(N_SHARD):
        lo, hi = s * ws, (s + 1) * ws
        cols = [gp[:, p0 + max(a, lo) - a:p0 + min(b, hi) - a] for a, b, p0 in sorted(pieces) if max(a, lo) < min(b, hi)]
        shards.append(jnp.concatenate(cols, axis=1))
    return jnp.stack(shards)


def _col_shards(g):
    k, n = g.shape
    return g.reshape(k, N_SHARD, n // N_SHARD).transpose(1, 0, 2)


def _from_shards(g, name):
    ns, ks, nn = g.shape
    if name in ROW_SHARDED:
        return g.reshape(ns * ks, nn)
    return g.transpose(1, 0, 2).reshape(ks, ns * nn)


BY_SHARD = ("w_in", "w_ff1")
EARLY = ("w_in", "w_uq", "w_ukv")
LATE = ("w_o_mla", "w_pw_out", "w_out", "w_ff1", "w_ff2")


def _assemble(names, gathered):
    by_shard = {n: g.reshape((N_SHARD, 2 * g.shape[1]) + g.shape[2:]) for n, g in zip(names, gathered)}
    return {n: g if n in BY_SHARD else _from_shards(g, n) for n, g in by_shard.items()}


LARGE = ("w_in", "w_ff1", "w_ff2")
GROUP_A = ("w_out", "w_ff1", "w_ff2")
GROUP_B = ("w_in", "w_uq", "w_ukv", "w_o_mla", "w_pw_out")


def _pair_halves(g):
    return g.reshape(N_SHARD, 2, g.shape[1] // 2, g.shape[2])


def _pair_sums(names, halves, from_sibling):
    if not halves:
        return []
    cidx = lax.axis_index("c").reshape(1).astype(jnp.int32)
    out = {n: _add_pair(g, l, cidx, "pair_sum_" + n)
           for n, g, l in zip(names, halves, from_sibling) if n in LARGE}
    small = [j for j, n in enumerate(names) if n not in LARGE]
    if small:
        res = _add_pair_whole([halves[j] for j in small], [from_sibling[j] for j in small], cidx,
                              "pair_sum_small_" + names[small[0]])
        out.update({names[j]: r for j, r in zip(small, res)})
    return [out[n] for n in names]


def _local_step(x, target, mod, sp, w, late=None, tm=256):
    comm = late is not None
    w = dict(w)
    nseq, seq, d = x.shape
    t = nseq * seq
    tps = seq // tm
    xf = x.reshape(t, d)
    tg = target.reshape(t, d)
    mod3 = mod.reshape(nseq, N_MOD, d)

    win_p = _pad_win(w["w_in"])
    wuq_p = _pad_heads(w["w_uq"], QK_HEAD)
    wkv3 = w["w_ukv"].reshape(KV_RANK, N_HEADS, QK_NOPE + V_HEAD)
    wk_p = _pad_heads(wkv3[:, :, :QK_NOPE].reshape(KV_RANK, -1), QK_NOPE)
    wv_p = _pad_heads(wkv3[:, :, QK_NOPE:].reshape(KV_RANK, -1), V_HEAD)
    cw = jnp.pad(w["conv_w"], ((0, HALO - CONV_W), (0, 0)))
    pad_g = lambda g: jnp.pad(g, ((0, 0), (0, LANES - QK_HEAD)))
    gq, gk = pad_g(sp["qk_norm_q_g"]), pad_g(sp["qk_norm_k_g"])
    tabs = _rope_tables(seq)

    tm_in, tps_in = (2 * tm, tps // 2) if tps % 2 == 0 else (tm, tps)
    h, zm, zglu, zgate, u0 = _fwd_in(xf, sp["norm1_g"], mod3, win_p, tm_in, tps_in)
    q, k, v, qln, kvn = _mla_prep(zm, sp["q_latent_g"], sp["kv_latent_g"], gq, gk, tabs, wuq_p, wk_p, wv_p, tm, tps)
    attn, gathered = _attn_fwd(q, k, v, nseq, seq, tuple(late) if comm else ())
    if comm:
        w.update(_assemble(LATE, gathered))
    wo_p = jnp.pad(w["w_o_mla"].reshape(N_HEADS, V_HEAD, d), ((0, 0), (0, LANES - V_HEAD), (0, 0))).reshape(HW, d)
    x1, mixed, mpre, ya, yb, u1, u3 = _fwd_mix(attn, u0, zgate, xf, mod3, wo_p, cw, sp["conv_b"], sp["conv_ln_g"],
                                               sp["conv_ln_b"], w["w_pw_out"], w["w_out"], tm, tps)
    h2, a, r, dy, df, dgate2, loss_acc = _fwd_ffn(x1, tg, sp["norm2_g"], mod3, w["w_ff1"], w["w_ff2"], tm, tps)
    da, dx1, dmixed, dshift2, dscale2, dgate1, dg2 = _bwd_ffn(df, a, x1, dy, mixed, sp["norm2_g"], mod3,
                                                              w["w_ff2"], w["w_ff1"], tm, tps)
    gw = {
        "w_out": _tn_matmul(mpre, dmixed, "dw_out").reshape(N_SHARD, d // N_SHARD, d),
        "w_ff1": _tn_matmul(h2, da, "dw_ff1", N_SHARD),
        "w_ff2": _tn_matmul(r, df, "dw_ff2").reshape(N_SHARD, -1, d),
    }
    halves_a = [_pair_halves(gw[n]) for n in GROUP_A] if comm else []
    dya, dyb, dz, do, du1, dlng, dlnb, dcb, from_sibling = _bwd_mix(
        dmixed, zgate, ya, yb, u1, sp["conv_ln_g"], sp["conv_ln_b"], w["w_out"], wo_p, w["w_pw_out"], tm_in, tuple(halves_a))
    pair_a = _pair_sums(GROUP_A, halves_a, from_sibling)
    dz, dcw = _bwd_conv(dz, du1, u0, zglu, cw, tm, tps)
    gw["conv_w"] = dcw
    dq, dk, dv, land_a = _attn_bwd(q, k, v, do, nseq, seq, tuple(p[1] for p in pair_a))
    dz, dqpre, dkh, dgq, dgk, dgql, dgkvl = _mla_bwd(dz, dq, dk, dv, zm, sp["q_latent_g"], sp["kv_latent_g"], gq, gk,
                                                      tabs, wuq_p, wk_p, wv_p, tm, tps)
    dwk_p = _tn_matmul(kvn, dkh, "dw_uk")
    dwv_p = _tn_matmul(kvn, dv, "dw_uv")
    dwkv = jnp.concatenate([dwk_p.reshape(KV_RANK, N_HEADS, LANES)[:, :, :QK_NOPE],
                            dwv_p.reshape(KV_RANK, N_HEADS, LANES)[:, :, :V_HEAD]], axis=2).reshape(KV_RANK, -1)
    dwo = _tn_matmul(attn, dya, "dw_o").reshape(N_HEADS, LANES, d)[:, :V_HEAD].reshape(MLA_WIDTH, d)
    gw["w_in"] = _unpad_win(_tn_matmul(h, dz, "dw_in"))
    gw["w_uq"] = _col_shards(_unpad_heads(_tn_matmul(qln, dqpre, "dw_uq"), QK_HEAD))
    gw["w_ukv"] = _col_shards(dwkv)
    gw["w_o_mla"] = _col_shards(dwo)
    gw["w_pw_out"] = _tn_matmul(u3, dyb, "dw_pw", N_SHARD)
    pair_b = []
    if comm:
        halves_b = [_pair_halves(gw[n]) for n in GROUP_B]
        pair_b = _pair_sums(GROUP_B, halves_b, _pair_swap(halves_b, "grad_pair_swap"))
    gx, dshift1, dscale1, dg1, land_b = _bwd_in(dz, xf, dx1, sp["norm1_g"], mod3, win_p, tm_in, tps_in,
                                                tuple(p[1] for p in pair_b))
    if comm:
        for n, p, l in zip(GROUP_A + GROUP_B, pair_a + pair_b, land_a + land_b):
            gw[n] = (p[0], l)
    gs = {
        "norm1_g": dg1, "q_latent_g": dgql, "kv_latent_g": dgkvl, "qk_norm_q_g": dgq, "qk_norm_k_g": dgk,
        "conv_b": dcb, "conv_ln_g": dlng, "conv_ln_b": dlnb, "norm2_g": dg2,
    }
    dmod = jnp.concatenate([dshift1, dscale1, dgate1, dshift2, dscale2, dgate2], axis=2).reshape(nseq, N_MOD * d)
    return loss_acc, gx.reshape(nseq, seq, d), dmod, gw, gs


def kernel(x, c, w_ada, b_ada, norm1_g, w_in, q_latent_g, w_uq, kv_latent_g, w_ukv, qk_norm_q_g, qk_norm_k_g, w_o_mla, conv_w, conv_b, conv_ln_g, conv_ln_b, w_pw_out, w_out, norm2_g, w_ff1, w_ff2, loss_target, m_w_ada, m_b_ada, m_norm1_g, m_w_in, m_q_latent_g, m_w_uq, m_kv_latent_g, m_w_ukv, m_qk_norm_q_g, m_qk_norm_k_g, m_w_o_mla, m_conv_w, m_conv_b, m_conv_ln_g, m_conv_ln_b, m_w_pw_out, m_w_out, m_norm2_g, m_w_ff1, m_w_ff2, v_w_ada, v_b_ada, v_norm1_g, v_w_in, v_q_latent_g, v_w_uq, v_kv_latent_g, v_w_ukv, v_qk_norm_q_g, v_qk_norm_k_g, v_w_o_mla, v_conv_w, v_conv_b, v_conv_ln_g, v_conv_ln_b, v_w_pw_out, v_w_out, v_norm2_g, v_w_ff1, v_w_ff2):
    given = dict(locals())
    wts = {n: given[n][0] for n in WEIGHTS}
    mom = {n: given["m_" + n][0] for n in WEIGHTS}
    var = {n: given["v_" + n][0] for n in WEIGHTS}
    vec = lambda a: a.reshape(1, -1)
    nseq, seq, d = x.shape
    ix, iy, ic = _place()
    shard = 2 * ix + iy

    half = lambda n: lax.dynamic_slice_in_dim(wts[n].astype(BF16), ic * (wts[n].shape[0] // 2), wts[n].shape[0] // 2,
                                              axis=0)
    gathered = _all_gather8([half(n) for n in EARLY] + [wts["conv_w"], c], "gather_weights")
    full = _assemble(EARLY, gathered)
    full["conv_w"] = _from_shards(gathered[-2][0::2], "conv_w")
    c_all = gathered[-1].reshape(8 * nseq, d)

    n_ada = wts["w_ada"].shape[1]
    b_sh = lax.dynamic_slice_in_dim(vec(wts["b_ada"]), shard * n_ada, n_ada, axis=1)
    mod_sh = _ada_mod(c_all, wts["w_ada"], b_sh)
    hb = 4 * nseq
    mod_blk = lax.dynamic_slice_in_dim(mod_sh, ic * hb, hb, axis=0)
    (mod_all,) = _all_gather8([mod_blk], "gather_mod")
    mod_mine = lax.dynamic_slice_in_dim(mod_all, (2 * iy + ic) * nseq, nseq, axis=1)
    mod = jnp.concatenate([lax.dynamic_index_in_dim(mod_mine, 2 * s + ix, axis=0, keepdims=False)
                           for s in range(N_SHARD)], axis=1)

    sp = {n: vec(wts[n]) for n in SMALL}
    loss_part, grad_x, dmod, gw, gs = _local_step(x, loss_target, mod, sp, full, [half(n) for n in LATE])

    own_c = jnp.stack([shard, ic]).astype(jnp.int32)
    mine_sum = {n: _add_chips(gw[n][0], gw[n][1], own_c, "chip_sum_" + n) for n in LARGE}
    few = tuple(n for n in BIG if n not in LARGE)
    mine_sum.update(zip(few, _add_chips_whole([gw[n][0] for n in few], [gw[n][1] for n in few], own_c, "chip_sum_small")))
    summed, parts = _pair_gather_and_all_gather8(
        [mine_sum[n] for n in BIG], [dmod, gw["conv_w"], loss_part] + [gs[n] for n in SMALL], "tail_exchange")

    dmod_all = parts[0].reshape(8 * nseq, N_MOD * d)
    dmod_sh = lax.dynamic_slice_in_dim(dmod_all, shard * n_ada, n_ada, axis=1)
    res = _ada_bwd(c_all, dmod_all, dmod_sh, parts[1:])
    grads = {"w_ada": res[0], "b_ada": res[1]}
    n_cw = wts["conv_w"].shape[1]
    grads["conv_w"] = lax.dynamic_slice_in_dim(res[2], shard * n_cw, n_cw, axis=1)[:CONV_W]
    loss = res[3][0, 0]
    for n, g in zip(SMALL, res[4:]):
        grads[n] = g
    for n, g in zip(BIG, summed):
        grads[n] = g.reshape(wts[n].shape)

    delta, new_m, new_v = {}, {}, {}
    for n in LARGE + ("w_ada",):
        if n == "w_in":
            res = _adamw(wts[n].T, grads[n].T, mom[n].T, var[n].T, "adamw_" + n)
            delta[n], new_m[n], new_v[n] = (a.T for a in res)
        else:
            delta[n], new_m[n], new_v[n] = _adamw(wts[n], grads[n], mom[n], var[n], "adamw_" + n)
    rest = ("b_ada", "conv_w") + SMALL + few
    as2d = lambda a: a if a.ndim == 2 else vec(a)
    res = _adamw_small(*[[as2d(t[n]) for n in rest] for t in (wts, grads, mom, var)])
    for dst, arrs in zip((delta, new_m, new_v), res):
        for n, a in zip(rest, arrs):
            dst[n] = a

    outs = [loss, grad_x]
    for group in (grads, delta, new_m, new_v):
        outs += [group[n].reshape(given[n].shape) for n in WEIGHTS]
    return tuple(outs)
```

```python
import jax
import jax.numpy as jnp
from jax import lax
from jax.experimental import pallas as pl
from jax.experimental.pallas import tpu as pltpu

F32 = jnp.float32
BF16 = jnp.bfloat16
MESH = pl.DeviceIdType.MESH
ANY = pl.BlockSpec(memory_space=pl.ANY)

CHUNK = 64
CHUNK_SHIFT = 6
N_HEADS = 8
QK_NOPE = 64
QK_ROPE = 32
QK_HEAD = QK_NOPE + QK_ROPE
V_HEAD = 64
Q_RANK = 256
KV_RANK = 128
MLA_WIDTH = N_HEADS * V_HEAD
CONV_CH = 512
CONV_W = 31
ROPE_THETA = 10000.0
EPS = 1e-6
LANES = 128
SUBLANES = 8
HW = N_HEADS * LANES
OFF_KV = Q_RANK + KV_RANK
OFF_KR = OFF_KV + QK_ROPE
OFF_GLU = OFF_KR + 2 * CONV_CH
KR_LANE = QK_NOPE
MLA_IN = Q_RANK + KV_RANK + LANES
HALO = 32
N_MOD = 6

ADAM_LR = 0.001
ADAM_B1 = 0.9
ADAM_B2 = 0.999
ADAM_EPS = 1e-08
ADAM_WD = 0.01
ADAM_STEP = 10

VMEM_LIMIT = 56 * 1024 * 1024
BQ = 256


def _layout(d):
    p_glu = 2 * d
    p_q = p_glu + 2 * CONV_CH
    return p_glu, p_q, p_q + MLA_IN


def _params(*sem):
    return pltpu.CompilerParams(dimension_semantics=sem, vmem_limit_bytes=VMEM_LIMIT)


def _dot(a, b):
    return jnp.dot(a, b, preferred_element_type=F32)


def _dot_tn(a, b):
    return lax.dot_general(a, b, (((0,), (0,)), ((), ())), preferred_element_type=F32)


def _dot_nt(a, b):
    return lax.dot_general(a, b, (((1,), (1,)), ((), ())), preferred_element_type=F32)


def _acc(ref, val, first):
    @pl.when(first)
    def _():
        ref[...] = val

    @pl.when(jnp.logical_not(first))
    def _():
        ref[...] += val


def _rms(x):
    r = lax.rsqrt(jnp.mean(x * x, axis=-1, keepdims=True) + EPS)
    return x * r, r


def _rms_bwd(n, r, dn):
    return r * (dn - n * jnp.mean(dn * n, axis=-1, keepdims=True))


def _head_rms(sl):
    r = lax.rsqrt(jnp.sum(sl * sl, axis=-1, keepdims=True) * (1.0 / QK_HEAD) + EPS)
    return sl * r, r


def _head_rms_bwd(n, r, dn):
    return r * (dn - n * (jnp.sum(dn * n, axis=-1, keepdims=True) * (1.0 / QK_HEAD)))


def _rope(x, c, s1, s2):
    return x * c + pltpu.roll(x, QK_ROPE // 2, 1) * s1 + pltpu.roll(x, LANES - QK_ROPE // 2, 1) * s2


def _rope_t(dy, c, s1, s2):
    return dy * c + pltpu.roll(dy * s1, LANES - QK_ROPE // 2, 1) + pltpu.roll(dy * s2, QK_ROPE // 2, 1)


def _rope_tables(seq):
    half = QK_ROPE // 2
    inv_freq = ROPE_THETA ** (-jnp.arange(0, QK_ROPE, 2, dtype=F32) / QK_ROPE)
    ang = jnp.arange(seq, dtype=F32)[:, None] * inv_freq[None, :]
    cos, sin = jnp.cos(ang), jnp.sin(ang)
    z = lambda n: jnp.zeros((seq, n), F32)
    tail = LANES - QK_HEAD
    c = jnp.concatenate([jnp.ones((seq, QK_NOPE), F32), cos, cos, jnp.ones((seq, tail), F32)], axis=1)
    s1 = jnp.concatenate([z(QK_NOPE + half), sin, z(tail)], axis=1)
    s2 = jnp.concatenate([z(QK_NOPE), -sin, z(half + tail)], axis=1)
    return c, s1, s2


def _row(tm, w):
    return pl.BlockSpec((tm, w), lambda i: (i, 0))


def _modspec(d, tps):
    return pl.BlockSpec((None, N_MOD, d), lambda i: (i // tps, 0, 0))


def _seqv(w, tps):
    return pl.BlockSpec((None, 1, w), lambda i: (i // tps, 0, 0))


def _full(shape):
    return pl.BlockSpec(shape, lambda i: tuple(0 for _ in shape))


def _sds(shape, dtype):
    return jax.ShapeDtypeStruct(shape, dtype)


CONV_ROWS = 64
CONV_LC = CONV_CH // LANES


def _lane_chunks():
    return [(lc, slice(lc * LANES, (lc + 1) * LANES)) for lc in range(CONV_LC)]


def _fill_shifted(ext_ref, head, body):
    nh = head.shape[0]
    for lc, ls in _lane_chunks():
        ext_ref[0, lc, :nh, :] = head[:, ls]
        ext_ref[0, lc, nh:, :] = body[:, ls]
        rows = ext_ref[0, lc]
        for b in range(1, SUBLANES):
            ext_ref[b, lc] = pltpu.roll(rows, rows.shape[0] - b, 0)


def _shifted_shape(tm):
    return (SUBLANES, CONV_LC, tm + HALO, LANES)


def _conv_chunk(c):
    return c % CONV_LC, pl.multiple_of((c // CONV_LC) * CONV_ROWS, CONV_ROWS)


def _shifted(ext_ref, o, lc, r0):
    a = pl.multiple_of((o // SUBLANES) * SUBLANES + r0, SUBLANES)
    return ext_ref[o % SUBLANES, lc, pl.ds(a, CONV_ROWS), :]


def _by_lane_chunk(a):
    return a.reshape(a.shape[0], CONV_LC, LANES).transpose(1, 0, 2)


def _load_resident(i, pairs):
    @pl.when(i == 0)
    def _():
        for src, dst in pairs:
            pltpu.sync_copy(src, dst)


def _place():
    return lax.axis_index("x"), lax.axis_index("y"), lax.axis_index("c")


def _all_gather8(blocks, name):
    na = len(blocks)

    def body(*refs):
        start, forward, finish = _gather8_phases(refs[:na], refs[na:2 * na], *refs[2 * na:])
        start()
        forward()
        finish()

    outs = pl.pallas_call(
        body, name=name, out_shape=_gather8_shapes(blocks), in_specs=[ANY] * na, out_specs=(ANY,) * na,
        scratch_shapes=_gather8_sems(na),
    )(*blocks)
    return _own_block_placed(outs, blocks)


def _gather8_shapes(blocks):
    return tuple(_sds((8,) + b.shape, b.dtype) for b in blocks)


def _gather8_sems(na):
    return [pltpu.SemaphoreType.DMA((7 * na,)), pltpu.SemaphoreType.DMA((7 * na,))]


def _own_block_placed(outs, blocks):
    ix, iy, ic = _place()
    return tuple(lax.dynamic_update_index_in_dim(o, b, 4 * ix + 2 * iy + ic, 0) for o, b in zip(outs, blocks))


def _gather8_phases(x_refs, out_refs, send_sems, recv_sems):
    na = len(x_refs)
    x, y, c = _place()
    me, sibling = (x, y, c), (x, y, 1 - c)
    chips = [(1 - x, y), (x, 1 - y), (1 - x, 1 - y)]

    def copy(a, k, blk, to, from_input=False):
        dst = out_refs[a].at[4 * blk[0] + 2 * blk[1] + blk[2]]
        return pltpu.make_async_remote_copy(
            src_ref=x_refs[a] if from_input else dst, dst_ref=dst,
            send_sem=send_sems.at[7 * a + k], recv_sem=recv_sems.at[7 * a + k], device_id=to, device_id_type=MESH)

    def first(a):
        return [copy(a, 0, me, sibling, True)] + [copy(a, 1 + j, me, (*chip, c), True) for j, chip in enumerate(chips)]

    def start():
        for a in range(na):
            for cp in first(a):
                cp.start()

    def forward():
        for j, chip in enumerate(chips):
            for a in range(na):
                copy(a, 1 + j, (*chip, c), me).wait_recv()
                copy(a, 4 + j, (*chip, c), sibling).start()

    def finish():
        for a in range(na):
            copy(a, 0, sibling, me).wait_recv()
            for j, chip in enumerate(chips):
                copy(a, 4 + j, (*chip, 1 - c), me).wait_recv()
        for a in range(na):
            for cp in first(a) + [copy(a, 4 + j, (*chip, c), sibling) for j, chip in enumerate(chips)]:
                cp.wait_send()

    return start, forward, finish


def _pair_swap(gs, name):
    na = len(gs)

    def body(*refs):
        start, finish = _swap_phases(refs[:na], refs[na:2 * na], *refs[2 * na:])
        start()
        finish()

    return pl.pallas_call(
        body, name=name, out_shape=_swap_shapes(gs), in_specs=[ANY] * na, out_specs=(ANY,) * na,
        scratch_shapes=_swap_sems(gs),
    )(*gs)


def _swap_shapes(gs):
    return tuple(_sds(g.shape[:1] + g.shape[2:], g.dtype) for g in gs)


def _swap_sems(gs):
    n = sum(g.shape[0] for g in gs)
    return [pltpu.SemaphoreType.DMA((n,)), pltpu.SemaphoreType.DMA((n,))]


def _swap_phases(g_refs, land_refs, send_sems, recv_sems):
    x, y, c = _place()

    def copies():
        cps, k = [], 0
        for g_ref, land_ref in zip(g_refs, land_refs):
            for s in range(g_ref.shape[0]):
                cps.append(pltpu.make_async_remote_copy(
                    src_ref=g_ref.at[s, 1 - c], dst_ref=land_ref.at[s], send_sem=send_sems.at[k],
                    recv_sem=recv_sems.at[k], device_id=(x, y, 1 - c), device_id_type=MESH))
                k += 1
        return cps

    def start():
        for cp in copies():
            cp.start()

    def finish():
        for cp in copies():
            cp.wait()

    return start, finish


def _scatter_shapes(hs):
    return tuple(_sds((3,) + h.shape[1:], h.dtype) for h in hs)


def _scatter_sems(na):
    return [pltpu.SemaphoreType.DMA((3 * na,)), pltpu.SemaphoreType.DMA((3 * na,))]


def _scatter_phases(h_refs, land_refs, send_sems, recv_sems):
    x, y, c = _place()
    chips = [(1 - x, y), (x, 1 - y), (1 - x, 1 - y)]

    def copies():
        return [pltpu.make_async_remote_copy(
            src_ref=h_refs[a].at[2 * tx + ty], dst_ref=land_refs[a].at[j], send_sem=send_sems.at[3 * a + j],
            recv_sem=recv_sems.at[3 * a + j], device_id=(tx, ty, c), device_id_type=MESH)
            for a in range(len(h_refs)) for j, (tx, ty) in enumerate(chips)]

    def start():
        for cp in copies():
            cp.start()

    def finish():
        for cp in copies():
            cp.wait()

    return start, finish


def _pair_gather_and_all_gather8(fs, blocks, name):
    nf, nb = len(fs), len(blocks)

    def body(*refs):
        f_refs = refs[nf + nb:2 * nf + nb]
        b_out = refs[2 * nf + nb:2 * nf + 2 * nb]
        send_sems, recv_sems, g_send, g_recv = refs[2 * nf + 2 * nb:]
        x, y, c = _place()
        start, forward, finish = _gather8_phases(refs[nf:nf + nb], b_out, g_send, g_recv)
        sends = [pltpu.make_async_remote_copy(
            src_ref=f_refs[a].at[c], dst_ref=f_refs[a].at[c], send_sem=send_sems.at[a], recv_sem=recv_sems.at[a],
            device_id=(x, y, 1 - c), device_id_type=MESH) for a in range(nf)]
        recvs = [pltpu.make_async_remote_copy(
            src_ref=f_refs[a].at[c], dst_ref=f_refs[a].at[1 - c], send_sem=send_sems.at[a],
            recv_sem=recv_sems.at[a], device_id=(x, y, 1 - c), device_id_type=MESH) for a in range(nf)]
        start()
        for cp in sends:
            cp.start()
        forward()
        finish()
        for cp in recvs:
            cp.wait_recv()
        for cp in sends:
            cp.wait_send()

    res = pl.pallas_call(
        body, name=name, out_shape=tuple(_sds(f.shape, f.dtype) for f in fs) + _gather8_shapes(blocks),
        in_specs=[ANY] * (nf + nb), out_specs=(ANY,) * (nf + nb), input_output_aliases={a: a for a in range(nf)},
        scratch_shapes=[pltpu.SemaphoreType.DMA((nf,)), pltpu.SemaphoreType.DMA((nf,))] + _gather8_sems(nb),
    )(*fs, *blocks)
    return res[:nf], _own_block_placed(res[nf:], blocks)


def _row_tile(r, n, itemsize=4, budget=1 << 21):
    if r * n * itemsize <= budget:
        return r
    best = None
    for tr in range(16, r, 16):
        if r % tr == 0 and tr * n * itemsize <= budget:
            best = tr
    assert best is not None, (r, n)
    return best


def _add_pair(g, land, cidx, name):
    ns, _, r, n = g.shape
    tr = _row_tile(r, n)

    def body(c_ref, a_ref, b_ref, o_ref, ob_ref):
        s = a_ref[...] + b_ref[...]
        o_ref[...] = s
        ob_ref[...] = s.astype(BF16)

    out = pl.BlockSpec((None, tr, n), lambda s, i, cr: (s, i, 0))
    return pl.pallas_call(
        body, name=name, out_shape=(_sds((ns, r, n), F32), _sds((ns, r, n), BF16)),
        grid_spec=pltpu.PrefetchScalarGridSpec(
            num_scalar_prefetch=1, grid=(ns, r // tr),
            in_specs=[pl.BlockSpec((None, None, tr, n), lambda s, i, cr: (s, cr[0], i, 0)), out],
            out_specs=(out, out)),
        compiler_params=_params("arbitrary", "arbitrary"),
    )(cidx, g, land)


def _add_pair_whole(gs, lands, cidx, name):
    k = len(gs)

    def body(c_ref, *refs):
        for a_ref, b_ref, o_ref, ob_ref in zip(refs[:k], refs[k:2 * k], refs[2 * k:3 * k], refs[3 * k:]):
            s = a_ref[...] + b_ref[...]
            o_ref[...] = s
            ob_ref[...] = s.astype(BF16)

    half = lambda g: pl.BlockSpec((g.shape[0], None) + g.shape[2:], lambda i, cr: (0, cr[0], 0, 0))
    whole = lambda g: pl.BlockSpec(g.shape[:1] + g.shape[2:], lambda i, cr: (0, 0, 0))
    shapes = lambda dt: tuple(_sds(g.shape[:1] + g.shape[2:], dt) for g in gs)
    res = pl.pallas_call(
        body, name=name, out_shape=shapes(F32) + shapes(BF16),
        grid_spec=pltpu.PrefetchScalarGridSpec(
            num_scalar_prefetch=1, grid=(1,),
            in_specs=[half(g) for g in gs] + [whole(g) for g in gs],
            out_specs=tuple(whole(g) for g in gs) * 2),
        compiler_params=_params("arbitrary"),
    )(cidx, *gs, *lands)
    return list(zip(res[:k], res[k:]))


def _add_chips_whole(hs, lands, own_c, name):
    k = len(hs)

    def body(o_idx, *refs):
        for h_ref, l_ref, o_ref in zip(refs[:k], refs[k:2 * k], refs[2 * k:]):
            o_ref[...] = ((h_ref[...] + l_ref[0].astype(F32)) + l_ref[1].astype(F32)) + l_ref[2].astype(F32)

    return pl.pallas_call(
        body, name=name, out_shape=tuple(_sds((2,) + h.shape[1:], F32) for h in hs),
        grid_spec=pltpu.PrefetchScalarGridSpec(
            num_scalar_prefetch=1, grid=(1,),
            in_specs=[pl.BlockSpec((None,) + h.shape[1:], lambda i, o: (o[0], 0, 0)) for h in hs]
            + [pl.BlockSpec(l.shape, lambda i, o: (0, 0, 0)) for l in lands],
            out_specs=tuple(pl.BlockSpec((None,) + h.shape[1:], lambda i, o: (o[1], 0, 0)) for h in hs)),
        compiler_params=_params("arbitrary"),
    )(own_c, *hs, *lands)


def _add_chips(h, land, own_c, name):
    _, r, n = h.shape
    tr = _row_tile(r, n)

    def body(o_idx, h_ref, l_ref, o_ref):
        o_ref[...] = ((h_ref[...] + l_ref[0].astype(F32)) + l_ref[1].astype(F32)) + l_ref[2].astype(F32)

    return pl.pallas_call(
        body, name=name, out_shape=_sds((2, r, n), F32),
        grid_spec=pltpu.PrefetchScalarGridSpec(
            num_scalar_prefetch=1, grid=(r // tr,),
            in_specs=[pl.BlockSpec((None, tr, n), lambda i, o: (o[0], i, 0)),
                      pl.BlockSpec((3, tr, n), lambda i, o: (0, i, 0))],
            out_specs=pl.BlockSpec((None, tr, n), lambda i, o: (o[1], i, 0))),
        compiler_params=_params("arbitrary"),
    )(own_c, h, land)


def _adam_math(w, g, m, v):
    nm = ADAM_B1 * m + (1.0 - ADAM_B1) * g
    nv = ADAM_B2 * v + (1.0 - ADAM_B2) * (g * g)
    m_hat = nm / (1.0 - ADAM_B1 ** ADAM_STEP)
    v_hat = nv / (1.0 - ADAM_B2 ** ADAM_STEP)
    return -ADAM_LR * (m_hat / (jnp.sqrt(v_hat) + ADAM_EPS) + ADAM_WD * w), nm, nv


def _adamw(w, g, m, v, name):
    r, n = w.shape

    def body(w_ref, g_ref, m_ref, v_ref, d_ref, nm_ref, nv_ref):
        d_ref[...], nm_ref[...], nv_ref[...] = _adam_math(w_ref[...], g_ref[...], m_ref[...], v_ref[...])

    if r % 16 == 0:
        tr = _row_tile(r, n)
        steps, spec = r // tr, pl.BlockSpec((tr, n), lambda i: (i, 0))
    else:
        tc = 4 * LANES
        steps, spec = n // tc, pl.BlockSpec((r, tc), lambda j: (0, j))
    return pl.pallas_call(
        body, name=name, out_shape=(_sds((r, n), F32),) * 3, grid=(steps,),
        in_specs=[spec] * 4, out_specs=(spec,) * 3, compiler_params=_params("arbitrary"),
    )(w, g, m, v)


def _adamw_small(ws, gs, ms, vs):
    k = len(ws)

    def body(*refs):
        ins, outs = refs[:4 * k], refs[4 * k:]
        for j in range(k):
            d, nm, nv = _adam_math(ins[j][...], ins[k + j][...], ins[2 * k + j][...], ins[3 * k + j][...])
            outs[j][...] = d
            outs[k + j][...] = nm
            outs[2 * k + j][...] = nv

    shapes = tuple(_sds(w.shape, F32) for w in ws)
    res = pl.pallas_call(body, name="adamw_small", out_shape=shapes * 3,
                         compiler_params=pltpu.CompilerParams(vmem_limit_bytes=VMEM_LIMIT))(*ws, *gs, *ms, *vs)
    return res[:k], res[k:2 * k], res[2 * k:]


def _ada_mod(c_all, w_sh, b_sh):
    b, _ = c_all.shape
    n = w_sh.shape[1]

    def body(c_ref, w_ref, b_ref, o_ref):
        cc = c_ref[...]
        ca = (cc * jax.nn.sigmoid(cc)).astype(BF16)
        o_ref[...] = _dot(ca, w_ref[...].astype(BF16)) + b_ref[...]

    return pl.pallas_call(body, name="ada_mod", out_shape=_sds((b, n), F32),
                          compiler_params=pltpu.CompilerParams(vmem_limit_bytes=VMEM_LIMIT))(c_all, w_sh, b_sh)


def _ada_bwd(c_all, dmod_all, dmod_sh, parts):
    b, d = c_all.shape
    n6 = dmod_all.shape[1]
    n = dmod_sh.shape[1]
    k = len(parts)

    def body(*refs):
        c_ref, da_ref, ds_ref = refs[:3]
        p_refs = refs[3:3 + k]
        dw_ref, db_ref = refs[3 + k:5 + k]
        s_refs = refs[5 + k:]
        cc = c_ref[...]
        ca = (cc * jax.nn.sigmoid(cc)).astype(BF16)
        dw_ref[...] = _dot_tn(ca, ds_ref[...].astype(BF16))
        db_ref[...] = jnp.sum(da_ref[...], axis=0, keepdims=True)
        for p_ref, s_ref in zip(p_refs, s_refs):
            tot = p_ref[0]
            for j in range(1, p_ref.shape[0]):
                tot = tot + p_ref[j]
            s_ref[...] = tot

    return pl.pallas_call(
        body, name="ada_bwd",
        out_shape=(_sds((d, n), F32), _sds((1, n6), F32)) + tuple(_sds(p.shape[1:], F32) for p in parts),
        compiler_params=pltpu.CompilerParams(vmem_limit_bytes=VMEM_LIMIT),
    )(c_all, dmod_all, dmod_sh, *parts)


def _fwd_in(x, g1, mod3, win_p, tm, tps):
    t, d = x.shape
    p_glu, p_q, npad = _layout(d)

    def body(x_ref, g_ref, mod_ref, w_hbm, h_ref, zm_ref, zglu_ref, zgate_ref, u0_ref, w_ref):
        _load_resident(pl.program_id(0), [(w_hbm, w_ref)])
        n, _ = _rms(x_ref[...])
        h = ((n * g_ref[...]) * (1.0 + mod_ref[1:2, :]) + mod_ref[0:1, :]).astype(BF16)
        h_ref[...] = h
        z = _dot(h, w_ref[...])
        zgate_ref[...] = z[:, :p_glu]
        zglu = z[:, p_glu:p_q]
        zglu_ref[...] = zglu
        zm_ref[...] = z[:, p_q:]
        u0_ref[...] = zglu[:, :CONV_CH] * jax.nn.sigmoid(zglu[:, CONV_CH:])

    return pl.pallas_call(
        body, name="fwd_in", grid=(t // tm,),
        out_shape=(_sds((t, d), BF16), _sds((t, MLA_IN), F32), _sds((t, 2 * CONV_CH), F32), _sds((t, 2 * d), F32),
                   _sds((t, CONV_CH), F32)),
        in_specs=[_row(tm, d), _full((1, d)), _modspec(d, tps), ANY],
        out_specs=(_row(tm, d), _row(tm, MLA_IN), _row(tm, 2 * CONV_CH), _row(tm, 2 * d), _row(tm, CONV_CH)),
        scratch_shapes=[pltpu.VMEM(win_p.shape, BF16)],
        compiler_params=_params("arbitrary"),
    )(x, g1, mod3, win_p)


def _mla_prep(zm, gql, gkvl, gq, gk, tabs, wuq_p, wk_p, wv_p, tm, tps):
    t = zm.shape[0]
    c_t, s1_t, s2_t = tabs
    tab = pl.BlockSpec((tm, LANES), lambda i: (i % tps, 0))

    def body(zm_ref, gql_ref, gkvl_ref, gq_ref, gk_ref, c_ref, s1_ref, s2_ref, wuq_ref, wk_ref, wv_ref,
             q_ref, k_ref, v_ref, qln_ref, kvn_ref):
        c, s1, s2 = c_ref[...], s1_ref[...], s2_ref[...]
        nq, _ = _rms(zm_ref[:, :Q_RANK])
        qln = (nq * gql_ref[...]).astype(BF16)
        qln_ref[...] = qln
        qpre = _dot(qln, wuq_ref[...])
        nkv, _ = _rms(zm_ref[:, Q_RANK:OFF_KV])
        kvn = (nkv * gkvl_ref[...]).astype(BF16)
        kvn_ref[...] = kvn
        knope = _dot(kvn, wk_ref[...])
        v_ref[...] = _dot(kvn, wv_ref[...]).astype(BF16)
        zkr_v = zm_ref[:, OFF_KV:]
        kr_roped = _rope(zkr_v * gk_ref[...], c, s1, s2)
        slabs = [slice(hd * LANES, (hd + 1) * LANES) for hd in range(N_HEADS)]
        rq = [_head_rms(qpre[:, sl])[1] for sl in slabs]
        rk = [_head_rms(knope[:, sl] + zkr_v)[1] for sl in slabs]
        for hd, sl in enumerate(slabs):
            q_ref[:, sl] = _rope((qpre[:, sl] * rq[hd]) * gq_ref[...], c, s1, s2).astype(BF16)
            k_ref[:, sl] = (rk[hd] * (knope[:, sl] * gk_ref[...] + kr_roped)).astype(BF16)

    return pl.pallas_call(
        body, name="mla_prep", grid=(t // tm,),
        out_shape=(_sds((t, HW), BF16),) * 3 + (_sds((t, Q_RANK), BF16), _sds((t, KV_RANK), BF16)),
        in_specs=[_row(tm, MLA_IN), _full((1, Q_RANK)), _full((1, KV_RANK)),
                  _full((1, LANES)), _full((1, LANES)), tab, tab, tab,
                  _full(wuq_p.shape), _full(wk_p.shape), _full(wv_p.shape)],
        out_specs=(_row(tm, HW),) * 3 + (_row(tm, Q_RANK), _row(tm, KV_RANK)),
        compiler_params=_params("arbitrary"),
    )(zm, gql, gkvl, gq, gk, c_t, s1_t, s2_t, wuq_p, wk_p, wv_p)


AHEAD = 2
ROW_BAND = 256
SM_SCALE = QK_HEAD ** -0.5
EXP2_SCALE = SM_SCALE * 1.4426950408889634


def _diag_mask():
    rc = jnp.right_shift(lax.broadcasted_iota(jnp.int32, (BQ, 1), 0), CHUNK_SHIFT)
    cc = jnp.right_shift(lax.broadcasted_iota(jnp.int32, (1, BQ), 1), CHUNK_SHIFT)
    return rc >= cc


def _scores(q_i, k_ref, lo, e):
    return (_dot_nt(q_i, k_ref[:lo, :]) if lo else None), _dot_nt(q_i, k_ref[lo:e, :])


def _softmax_parts(scores, mask):
    sp, sd = scores
    sd = jnp.where(mask, sd, jnp.finfo(F32).min)
    m = jnp.max(sd, axis=-1, keepdims=True)
    if sp is not None:
        m = jnp.maximum(m, jnp.max(sp, axis=-1, keepdims=True))
    pd = jnp.exp2((sd - m) * EXP2_SCALE)
    l = jnp.sum(pd, axis=-1, keepdims=True)
    pp = None
    if sp is not None:
        pp = jnp.exp2((sp - m) * EXP2_SCALE)
        l = l + jnp.sum(pp, axis=-1, keepdims=True)
    return pp, pd, l


def _attn_fwd(q, k, v, nseq, seq, gather=()):
    t = q.shape[0]
    na = len(gather)
    blk = pl.BlockSpec((seq, LANES), lambda b, h: (b, h))
    n_steps = nseq * N_HEADS

    def body(q_ref, k_ref, v_ref, *rest):
        o_ref = rest[na]
        if na:
            start, forward, finish = _gather8_phases(rest[:na], rest[na + 1:2 * na + 1], *rest[2 * na + 1:])
            step = pl.program_id(0) * N_HEADS + pl.program_id(1)
            pl.when(step == 0)(start)
            pl.when(step == (7 * n_steps) // 8)(forward)
        mask = _diag_mask()
        nb = seq // BQ
        block_scores = lambda j: _scores(q_ref[j * BQ:(j + 1) * BQ, :], k_ref, j * BQ, (j + 1) * BQ)
        ahead = [block_scores(j) for j in range(min(AHEAD, nb))]
        for i in range(nb):
            lo, e = i * BQ, (i + 1) * BQ
            cur = ahead.pop(0)
            if i + AHEAD < nb:
                ahead.append(block_scores(i + AHEAD))
            pp, pd, l = _softmax_parts(cur, mask)
            o = _dot(pd.astype(BF16), v_ref[lo:e, :])
            if lo:
                o = o + _dot(pp.astype(BF16), v_ref[:lo, :])
            o_ref[lo:e, :] = (o * (1.0 / l)).astype(BF16)
        if na:
            pl.when(step == n_steps - 1)(finish)

    res = pl.pallas_call(
        body, name="attn_fwd", grid=(nseq, N_HEADS), out_shape=(_sds((t, HW), BF16),) + _gather8_shapes(gather),
        in_specs=[blk, blk, blk] + [ANY] * na, out_specs=(blk,) + (ANY,) * na,
        scratch_shapes=_gather8_sems(na) if na else [],
        compiler_params=_params("arbitrary", "arbitrary"),
    )(q, k, v, *gather)
    return res[0], (_own_block_placed(res[1:], gather) if na else ())


def _fwd_mix(attn, u0, zgate, x, mod3, wo_p, cw, cb, lng, lnb, wpw, wout, tm, tps):
    t, d = x.shape
    hpt = tm // HALO
    cwc, cbc = _by_lane_chunk(cw), _by_lane_chunk(cb)

    def body(a_ref, u_ref, uh_ref, zg_ref, x_ref, mod_ref, wo_ref, cw_ref, cb_ref, lng_ref, lnb_ref, wpw_ref, wout_ref,
             x1_ref, mixed_ref, mpre_ref, ya_ref, yb_ref, u1_ref, u3_ref, ext_ref):
        i = pl.program_id(0)
        ya = _dot(a_ref[...], wo_ref[...])
        ya_ref[...] = ya
        first = (i % tps) == 0
        _fill_shifted(ext_ref, jnp.where(first, 0.0, uh_ref[...]), u_ref[...])
        for lc, ls in _lane_chunks():
            acc = jnp.broadcast_to(cb_ref[lc], (tm, LANES))
            for kk in range(CONV_W):
                o = HALO - (CONV_W - 1) + kk
                a = (o // SUBLANES) * SUBLANES
                acc = acc + cw_ref[lc, kk:kk + 1, :] * ext_ref[o % SUBLANES, lc, a:a + tm, :]
            u1_ref[:, ls] = acc
        acc = u1_ref[...]
        mu = jnp.mean(acc, axis=-1, keepdims=True)
        xc = acc - mu
        rstd = lax.rsqrt(jnp.mean(xc * xc, axis=-1, keepdims=True) + EPS)
        l = (xc * rstd) * lng_ref[...] + lnb_ref[...]
        u3 = (l * jax.nn.sigmoid(l)).astype(BF16)
        u3_ref[...] = u3
        yb = _dot(u3, wpw_ref[...])
        yb_ref[...] = yb
        zg = zg_ref[...]
        mpre = (jax.nn.sigmoid(zg[:, :d]) * ya + jax.nn.sigmoid(zg[:, d:]) * yb).astype(BF16)
        mpre_ref[...] = mpre
        mixed = _dot(mpre, wout_ref[...])
        mixed_ref[...] = mixed
        x1_ref[...] = x_ref[...] + mod_ref[2:3, :] * mixed

    halo = pl.BlockSpec((HALO, CONV_CH), lambda i: (jnp.maximum(i * hpt - 1, 0), 0))
    return pl.pallas_call(
        body, name="fwd_mix", grid=(t // tm,),
        out_shape=(_sds((t, d), F32), _sds((t, d), F32), _sds((t, d), BF16), _sds((t, d), F32), _sds((t, d), F32),
                   _sds((t, CONV_CH), F32), _sds((t, CONV_CH), BF16)),
        in_specs=[_row(tm, HW), _row(tm, CONV_CH), halo, _row(tm, 2 * d), _row(tm, d), _modspec(d, tps),
                  _full(wo_p.shape), _full(cwc.shape), _full(cbc.shape), _full((1, CONV_CH)), _full((1, CONV_CH)),
                  _full(wpw.shape), _full(wout.shape)],
        out_specs=(_row(tm, d), _row(tm, d), _row(tm, d), _row(tm, d), _row(tm, d), _row(tm, CONV_CH),
                   _row(tm, CONV_CH)),
        scratch_shapes=[pltpu.VMEM(_shifted_shape(tm), F32)],
        compiler_params=_params("arbitrary"),
    )(attn, u0, u0, zgate, x, mod3, wo_p, cwc, cbc, lng, lnb, wpw, wout)


def _shards_into_columns(w_hbm, w_ref):
    ns = w_hbm.shape[2]
    return [(w_hbm.at[s], w_ref.at[:, pl.ds(s * ns, ns)]) for s in range(w_hbm.shape[0])]


def _fwd_ffn(x1, target, g2, mod3, w1, w2, tm, tps):
    t, d = x1.shape
    dff = w1.shape[0] * w1.shape[2]

    def body(x1_ref, tg_ref, g_ref, mod_ref, w1_hbm, w2_hbm,
             h2_ref, a_ref, r_ref, dy_ref, df_ref, dgate_ref, loss_ref, w1_ref, w2_ref):
        i = pl.program_id(0)
        _load_resident(i, _shards_into_columns(w1_hbm, w1_ref) + [(w2_hbm, w2_ref)])
        x1v = x1_ref[...]
        gate2 = mod_ref[5:6, :]
        n, _ = _rms(x1v)
        h2 = ((n * g_ref[...]) * (1.0 + mod_ref[4:5, :]) + mod_ref[3:4, :]).astype(BF16)
        h2_ref[...] = h2
        a = _dot(h2, w1_ref[...])
        a_ref[...] = a
        r = jnp.square(jnp.maximum(a, 0.0)).astype(BF16)
        r_ref[...] = r
        f = _dot(r, w2_ref[...])
        e = (x1v + gate2 * f) - tg_ref[...]
        part = 0.5 * jnp.sum(jnp.mean(e * e, axis=-1, keepdims=True), axis=0, keepdims=True)
        _acc(loss_ref, jnp.broadcast_to(part, loss_ref.shape), i == 0)
        dy = e * (1.0 / d)
        dy_ref[...] = dy
        df_ref[...] = (dy * gate2).astype(BF16)
        _acc(dgate_ref, jnp.sum(dy * f, axis=0, keepdims=True), (i % tps) == 0)

    nseq = t // (tm * tps)
    return pl.pallas_call(
        body, name="fwd_ffn", grid=(t // tm,),
        out_shape=(_sds((t, d), BF16), _sds((t, dff), F32), _sds((t, dff), BF16), _sds((t, d), F32), _sds((t, d), BF16),
                   _sds((nseq, 1, d), F32), _sds((8, LANES), F32)),
        in_specs=[_row(tm, d), _row(tm, d), _full((1, d)), _modspec(d, tps), ANY, ANY],
        out_specs=(_row(tm, d), _row(tm, dff), _row(tm, dff), _row(tm, d), _row(tm, d), _seqv(d, tps),
                   _full((8, LANES))),
        scratch_shapes=[pltpu.VMEM((d, dff), BF16), pltpu.VMEM(w2.shape, BF16)],
        compiler_params=_params("arbitrary"),
    )(x1, target, g2, mod3, w1, w2)


def _bwd_ffn(df, a, x1, dy, mixed, g2, mod3, w2, w1, tm, tps):
    t, d = x1.shape
    dff = a.shape[1]

    def body(df_ref, a_ref, x1_ref, dy_ref, mx_ref, g_ref, mod_ref, w2_hbm, w1_hbm,
             da_ref, dx1_ref, dmixed_ref, dshift_ref, dscale_ref, dgate1_ref, dg2_ref, w2_ref, w1_ref):
        i = pl.program_id(0)
        _load_resident(i, [(w2_hbm, w2_ref)] + _shards_into_columns(w1_hbm, w1_ref))
        first_seq = (i % tps) == 0
        dr = _dot_nt(df_ref[...], w2_ref[...])
        da = (dr * (2.0 * jnp.maximum(a_ref[...], 0.0))).astype(BF16)
        da_ref[...] = da
        dh2 = _dot_nt(da, w1_ref[...])
        n, r = _rms(x1_ref[...])
        g = g_ref[...]
        sc1 = 1.0 + mod_ref[4:5, :]
        _acc(dshift_ref, jnp.sum(dh2, axis=0, keepdims=True), first_seq)
        _acc(dscale_ref, jnp.sum(dh2 * (n * g), axis=0, keepdims=True), first_seq)
        _acc(dg2_ref, jnp.sum((dh2 * sc1) * n, axis=0, keepdims=True), i == 0)
        dx1 = dy_ref[...] + _rms_bwd(n, r, (dh2 * sc1) * g)
        dx1_ref[...] = dx1
        _acc(dgate1_ref, jnp.sum(dx1 * mx_ref[...], axis=0, keepdims=True), first_seq)
        dmixed_ref[...] = (dx1 * mod_ref[2:3, :]).astype(BF16)

    nseq = t // (tm * tps)
    sv = _sds((nseq, 1, d), F32)
    return pl.pallas_call(
        body, name="bwd_ffn", grid=(t // tm,),
        out_shape=(_sds((t, dff), BF16), _sds((t, d), F32), _sds((t, d), BF16), sv, sv, sv, _sds((1, d), F32)),
        in_specs=[_row(tm, d), _row(tm, dff), _row(tm, d), _row(tm, d), _row(tm, d), _full((1, d)), _modspec(d, tps),
                  ANY, ANY],
        out_specs=(_row(tm, dff), _row(tm, d), _row(tm, d), _seqv(d, tps), _seqv(d, tps), _seqv(d, tps),
                   _full((1, d))),
        scratch_shapes=[pltpu.VMEM(w2.shape, BF16), pltpu.VMEM((d, dff), BF16)],
        compiler_params=_params("arbitrary"),
    )(df, a, x1, dy, mixed, g2, mod3, w2, w1)


def _bwd_mix(dmixed, zgate, ya, yb, u1, lng, lnb, wout, wo_p, wpw, tm, swap=()):
    t, d = ya.shape
    _, _, npad = _layout(d)
    nw = len(swap)
    n_steps = t // tm

    def body(dm_ref, zg_ref, ya_ref, yb_ref, u1_ref, lng_ref, lnb_ref, wout_ref, wo_ref, wpw_ref, *rest):
        dya_ref, dyb_ref, dz_ref, do_ref, du1_ref, dlng_ref, dlnb_ref, dcb_ref = rest[nw:nw + 8]
        i = pl.program_id(0)
        if nw:
            start, finish = _swap_phases(rest[:nw], rest[nw + 8:2 * nw + 8], *rest[2 * nw + 8:])
            pl.when(i == 0)(start)
        nb = max(tm // ROW_BAND, 1)
        bands = [slice(b * (tm // nb), (b + 1) * (tm // nb)) for b in range(nb)]
        col = lambda v: jnp.sum(v, axis=0, keepdims=True)
        dmpre = [_dot_nt(dm_ref[rows, :], wout_ref[...]) for rows in bands]
        dyab = []
        for rows, dmp in zip(bands, dmpre):
            ga = jax.nn.sigmoid(zg_ref[rows, :d])
            gb = jax.nn.sigmoid(zg_ref[rows, d:])
            dya = (dmp * ga).astype(BF16)
            dyb = (dmp * gb).astype(BF16)
            dya_ref[rows, :] = dya
            dyb_ref[rows, :] = dyb
            dz_ref[rows, :d] = ((dmp * ya_ref[rows, :]) * (ga * (1.0 - ga))).astype(BF16)
            dz_ref[rows, d:] = ((dmp * yb_ref[rows, :]) * (gb * (1.0 - gb))).astype(BF16)
            dyab.append((dya, dyb))
        du3s = []
        for rows, (dya, dyb) in zip(bands, dyab):
            do_ref[rows, :] = _dot_nt(dya, wo_ref[...]).astype(BF16)
            du3s.append(_dot_nt(dyb, wpw_ref[...]))
        sums = [jnp.zeros((1, CONV_CH), F32)] * 3
        for rows, du3 in zip(bands, du3s):
            u1 = u1_ref[rows, :]
            mu = jnp.mean(u1, axis=-1, keepdims=True)
            xc = u1 - mu
            rstd = lax.rsqrt(jnp.mean(xc * xc, axis=-1, keepdims=True) + EPS)
            nh = xc * rstd
            l = nh * lng_ref[...] + lnb_ref[...]
            sg = jax.nn.sigmoid(l)
            dl = du3 * (sg * (1.0 + l * (1.0 - sg)))
            dnh = dl * lng_ref[...]
            du1 = rstd * (dnh - jnp.mean(dnh, axis=-1, keepdims=True)
                          - nh * jnp.mean(dnh * nh, axis=-1, keepdims=True))
            du1_ref[rows, :] = du1
            sums = [sums[0] + col(dl * nh), sums[1] + col(dl), sums[2] + col(du1)]
        _acc(dlng_ref, sums[0], i == 0)
        _acc(dlnb_ref, sums[1], i == 0)
        _acc(dcb_ref, sums[2], i == 0)
        if nw:
            pl.when(i == n_steps - 1)(finish)

    cv = _sds((1, CONV_CH), F32)
    res = pl.pallas_call(
        body, name="bwd_mix", grid=(n_steps,),
        out_shape=(_sds((t, d), BF16), _sds((t, d), BF16), _sds((t, npad), BF16), _sds((t, HW), BF16),
                   _sds((t, CONV_CH), F32), cv, cv, cv) + _swap_shapes(swap),
        in_specs=[_row(tm, d), _row(tm, 2 * d), _row(tm, d), _row(tm, d), _row(tm, CONV_CH), _full((1, CONV_CH)),
                  _full((1, CONV_CH)), _full(wout.shape), _full(wo_p.shape), _full(wpw.shape)] + [ANY] * nw,
        out_specs=(_row(tm, d), _row(tm, d), _row(tm, 2 * d), _row(tm, HW), _row(tm, CONV_CH),
                   _full((1, CONV_CH)), _full((1, CONV_CH)), _full((1, CONV_CH))) + (ANY,) * nw,
        scratch_shapes=_swap_sems(swap) if nw else [],
        compiler_params=_params("arbitrary"),
    )(dmixed, zgate, ya, yb, u1, lng, lnb, wout, wo_p, wpw, *swap)
    return res[:8] + (res[8:],)


def _bwd_conv(dz, du1, u0, zglu, cw, tm, tps):
    t = du1.shape[0]
    d = (dz.shape[1] - MLA_IN - 2 * CONV_CH) // 2
    p_glu, _, _ = _layout(d)
    hpt = tm // HALO
    last_blk = t // HALO - 1
    cwc = _by_lane_chunk(cw)

    def body(dz_hbm, du_ref, dun_ref, u_ref, zl_ref, cw_ref, dzl_ref, dcw_ref, dext_ref, uc_ref, dcw8_ref, du0_ref):
        i = pl.program_id(0)
        last = (i % tps) == (tps - 1)
        _fill_shifted(dext_ref, du_ref[...], jnp.where(last, 0.0, dun_ref[...]))
        for lc, ls in _lane_chunks():
            uc_ref[lc] = u_ref[:, ls]

        @pl.when(i == 0)
        def _():
            dcw8_ref[...] = jnp.zeros_like(dcw8_ref)

        groups = CONV_ROWS // SUBLANES

        def conv_chunk(c, carry):
            lc, r0 = _conv_chunk(c)
            u = uc_ref[lc, pl.ds(r0, CONV_ROWS), :]
            du0 = jnp.zeros((CONV_ROWS, LANES), F32)
            for kk in range(CONV_W):
                win = _shifted(dext_ref, CONV_W - 1 - kk, lc, r0)
                prod = u * win
                part = prod[:SUBLANES]
                for g in range(1, groups):
                    part = part + prod[g * SUBLANES:(g + 1) * SUBLANES]
                dcw8_ref[lc, kk] += part
                du0 = du0 + cw_ref[lc, kk:kk + 1, :] * win
            du0_ref[lc, pl.ds(r0, CONV_ROWS), :] = du0
            return carry

        lax.fori_loop(0, CONV_LC * (tm // CONV_ROWS), conv_chunk, 0)

        @pl.when(i == pl.num_programs(0) - 1)
        def _():
            for lc, ls in _lane_chunks():
                dcw_ref[:, ls] = jnp.sum(dcw8_ref[lc], axis=1)

        for lc, ls in _lane_chunks():
            du0 = du0_ref[lc]
            ga = zl_ref[:, ls]
            sb = jax.nn.sigmoid(zl_ref[:, CONV_CH + lc * LANES:CONV_CH + (lc + 1) * LANES])
            dzl_ref[:, ls] = (du0 * sb).astype(BF16)
            dzl_ref[:, CONV_CH + lc * LANES:CONV_CH + (lc + 1) * LANES] = ((du0 * ga) * (sb * (1.0 - sb))).astype(BF16)

    nxt = pl.BlockSpec((HALO, CONV_CH), lambda i: (jnp.minimum((i + 1) * hpt, last_blk), 0))
    glu_blk = p_glu // (2 * CONV_CH)
    return pl.pallas_call(
        body, name="bwd_conv", grid=(t // tm,),
        out_shape=(_sds(dz.shape, BF16), _sds(cw.shape, F32)),
        in_specs=[ANY, _row(tm, CONV_CH), nxt, _row(tm, CONV_CH), _row(tm, 2 * CONV_CH), _full(cwc.shape)],
        out_specs=(pl.BlockSpec((tm, 2 * CONV_CH), lambda i: (i, glu_blk)), _full(cw.shape)),
        scratch_shapes=[pltpu.VMEM(_shifted_shape(tm), F32), pltpu.VMEM((CONV_LC, tm, LANES), F32),
                        pltpu.VMEM((CONV_LC, HALO, SUBLANES, LANES), F32), pltpu.VMEM((CONV_LC, tm, LANES), F32)],
        input_output_aliases={0: 0},
        compiler_params=_params("arbitrary"),
    )(dz, du1, du1, u0, zglu, cwc)


def _attn_bwd(q, k, v, do, nseq, seq, scatter=()):
    t = q.shape[0]
    ns = len(scatter)
    blk = pl.BlockSpec((seq, LANES), lambda b, h: (b, h))
    n_steps = nseq * N_HEADS

    def body(q_ref, k_ref, v_ref, do_ref, *rest):
        dq_ref, dk_ref, dv_ref = rest[ns:ns + 3]
        dka_ref, dva_ref = rest[2 * ns + 3:2 * ns + 5]
        if ns:
            start, finish = _scatter_phases(rest[:ns], rest[ns + 3:2 * ns + 3], *rest[2 * ns + 5:])
            step = pl.program_id(0) * N_HEADS + pl.program_id(1)
            pl.when(step == 0)(start)
        dka_ref[...] = jnp.zeros_like(dka_ref)
        dva_ref[...] = jnp.zeros_like(dva_ref)
        mask = _diag_mask()
        nb = seq // BQ
        block = lambda j: (_scores(q_ref[j * BQ:(j + 1) * BQ, :], k_ref, j * BQ, (j + 1) * BQ),
                           _scores(do_ref[j * BQ:(j + 1) * BQ, :], v_ref, j * BQ, (j + 1) * BQ))
        ahead = [block(j) for j in range(min(AHEAD, nb))]

        def second_stage(lo, e, dsd, dsp, pdb, ppb):
            q_i = q_ref[lo:e, :]
            do_i = do_ref[lo:e, :]
            dq = _dot(dsd, k_ref[lo:e, :])
            dka_ref[lo:e, :] += _dot_tn(dsd, q_i)
            dva_ref[lo:e, :] += _dot_tn(pdb, do_i)
            if lo:
                dq = dq + _dot(dsp, k_ref[:lo, :])
                dka_ref[:lo, :] += _dot_tn(dsp, q_i)
                dva_ref[:lo, :] += _dot_tn(ppb, do_i)
            dq_ref[lo:e, :] = dq * SM_SCALE

        held = None
        for i in range(nb):
            lo, e = i * BQ, (i + 1) * BQ
            scores, (dpp, dpd) = ahead.pop(0)
            if i + AHEAD < nb:
                ahead.append(block(i + AHEAD))
            pp, pd, l = _softmax_parts(scores, mask)
            inv = 1.0 / l
            pd = pd * inv
            delta = jnp.sum(pd * dpd, axis=-1, keepdims=True)
            if lo:
                pp = pp * inv
                delta = delta + jnp.sum(pp * dpp, axis=-1, keepdims=True)
            dsd = (pd * (dpd - delta)).astype(BF16)
            dsp = (pp * (dpp - delta)).astype(BF16) if lo else None
            if held is not None:
                second_stage(*held)
            held = (lo, e, dsd, dsp, pd.astype(BF16), pp.astype(BF16) if lo else None)
        second_stage(*held)
        dk_ref[...] = dka_ref[...] * SM_SCALE
        dv_ref[...] = dva_ref[...].astype(BF16)
        if ns:
            pl.when(step == n_steps - 1)(finish)

    res = pl.pallas_call(
        body, name="attn_bwd", grid=(nseq, N_HEADS),
        out_shape=(_sds((t, HW), F32), _sds((t, HW), F32), _sds((t, HW), BF16)) + _scatter_shapes(scatter),
        in_specs=[blk] * 4 + [ANY] * ns, out_specs=(blk,) * 3 + (ANY,) * ns,
        scratch_shapes=[pltpu.VMEM((seq, LANES), F32), pltpu.VMEM((seq, LANES), F32)]
        + (_scatter_sems(ns) if ns else []),
        compiler_params=_params("arbitrary", "arbitrary"),
    )(q, k, v, do, *scatter)
    return res[0], res[1], res[2], res[3:]


def _mla_bwd(dz, dq, dk, dv, zm, gql, gkvl, gq, gk, tabs, wuq_p, wk_p, wv_p, tm, tps):
    t = zm.shape[0]
    d = (dz.shape[1] - MLA_IN - 2 * CONV_CH) // 2
    _, p_q, _ = _layout(d)
    c_t, s1_t, s2_t = tabs
    tab = pl.BlockSpec((tm, LANES), lambda i: (i % tps, 0))

    def body(dz_hbm, dq_ref, dk_ref, dv_ref, zm_ref, gql_ref, gkvl_ref, gq_ref, gk_ref, c_ref, s1_ref, s2_ref,
             wuq_ref, wk_ref, wv_ref,
             dzm_ref, dqpre_ref, dkh_ref, dgq_ref, dgk_ref, dgql_ref, dgkvl_ref):
        i = pl.program_id(0)
        c, s1, s2 = c_ref[...], s1_ref[...], s2_ref[...]
        nq, rq = _rms(zm_ref[:, :Q_RANK])
        qpre = _dot((nq * gql_ref[...]).astype(BF16), wuq_ref[...])
        nkv, rkv = _rms(zm_ref[:, Q_RANK:OFF_KV])
        knope = _dot((nkv * gkvl_ref[...]).astype(BF16), wk_ref[...])
        zkr_v = zm_ref[:, OFF_KV:]
        gk = gk_ref[...]
        kr_roped = _rope(zkr_v * gk, c, s1, s2)
        dgq = jnp.zeros((1, LANES), F32)
        dgk = jnp.zeros((1, LANES), F32)
        dzkr = jnp.zeros((tm, LANES), F32)
        dt_sum = jnp.zeros((tm, LANES), F32)
        slabs = [slice(hd * LANES, (hd + 1) * LANES) for hd in range(N_HEADS)]
        gq = gq_ref[...]
        rqh = [_head_rms(qpre[:, sl])[1] for sl in slabs]
        rkh = [_head_rms(knope[:, sl] + zkr_v)[1] for sl in slabs]
        dyr = [_rope_t(dq_ref[:, sl], c, s1, s2) for sl in slabs]
        nqh = [qpre[:, sl] * rqh[hd] for hd, sl in enumerate(slabs)]
        sq = [jnp.sum((dyr[hd] * gq) * nqh[hd], axis=-1, keepdims=True) for hd in range(N_HEADS)]
        dr = [jnp.sum(dk_ref[:, sl] * (knope[:, sl] * gk + kr_roped), axis=-1, keepdims=True) for sl in slabs]
        for hd, sl in enumerate(slabs):
            dgq = dgq + jnp.sum(dyr[hd] * nqh[hd], axis=0, keepdims=True)
            dqpre_ref[:, sl] = (rqh[hd] * (dyr[hd] * gq - nqh[hd] * (sq[hd] * (1.0 / QK_HEAD)))).astype(BF16)
            kn = knope[:, sl]
            r = rkh[hd]
            dt = dk_ref[:, sl] * r
            via_r = (dr[hd] * (r * r * r) * (-1.0 / QK_HEAD)) * (kn + zkr_v)
            dgk = dgk + jnp.sum(dt * kn, axis=0, keepdims=True)
            dt_sum = dt_sum + dt
            dzkr = dzkr + via_r
            dkh_ref[:, sl] = (dt * gk + via_r).astype(BF16)
        de = _rope_t(dt_sum, c, s1, s2)
        dzkr = dzkr + de * gk
        dgk = dgk + jnp.sum(de * zkr_v, axis=0, keepdims=True)
        _acc(dgq_ref, dgq[:, :QK_HEAD], i == 0)
        _acc(dgk_ref, dgk[:, :QK_HEAD], i == 0)
        dzm_ref[:, OFF_KV:] = dzkr.astype(BF16)
        dqln = _dot_nt(dqpre_ref[...], wuq_ref[...])
        _acc(dgql_ref, jnp.sum(dqln * nq, axis=0, keepdims=True), i == 0)
        dzm_ref[:, :Q_RANK] = _rms_bwd(nq, rq, dqln * gql_ref[...]).astype(BF16)
        dkvn = _dot_nt(dkh_ref[...], wk_ref[...]) + _dot_nt(dv_ref[...], wv_ref[...])
        _acc(dgkvl_ref, jnp.sum(dkvn * nkv, axis=0, keepdims=True), i == 0)
        dzm_ref[:, Q_RANK:OFF_KV] = _rms_bwd(nkv, rkv, dkvn * gkvl_ref[...]).astype(BF16)

    return pl.pallas_call(
        body, name="mla_bwd", grid=(t // tm,),
        out_shape=(_sds(dz.shape, BF16), _sds((t, HW), BF16), _sds((t, HW), BF16), _sds((1, QK_HEAD), F32),
                   _sds((1, QK_HEAD), F32), _sds((1, Q_RANK), F32), _sds((1, KV_RANK), F32)),
        in_specs=[ANY, _row(tm, HW), _row(tm, HW), _row(tm, HW), _row(tm, MLA_IN),
                  _full((1, Q_RANK)), _full((1, KV_RANK)), _full((1, LANES)), _full((1, LANES)), tab, tab, tab,
                  _full(wuq_p.shape), _full(wk_p.shape), _full(wv_p.shape)],
        out_specs=(pl.BlockSpec((tm, MLA_IN), lambda i: (i, p_q // MLA_IN)), _row(tm, HW), _row(tm, HW),
                   _full((1, QK_HEAD)), _full((1, QK_HEAD)), _full((1, Q_RANK)), _full((1, KV_RANK))),
        input_output_aliases={0: 0},
        compiler_params=_params("arbitrary"),
    )(dz, dq, dk, dv, zm, gql, gkvl, gq, gk, c_t, s1_t, s2_t, wuq_p, wk_p, wv_p)


def _bwd_in(dz, x, dx1, g1, mod3, win_p, tm, tps, scatter=()):
    t, d = x.shape
    npad = dz.shape[1]

    ns = len(scatter)
    n_steps = t // tm

    def body(dz_ref, x_ref, dx1_ref, g_ref, mod_ref, wt_hbm, *rest):
        gx_ref, dshift_ref, dscale_ref, dg1_ref = rest[ns:ns + 4]
        wt_ref = rest[2 * ns + 4]
        i = pl.program_id(0)
        if ns:
            start, finish = _scatter_phases(rest[:ns], rest[ns + 4:2 * ns + 4], *rest[2 * ns + 5:])
            pl.when(i == 0)(start)
        _load_resident(i, [(wt_hbm, wt_ref)])
        first_seq = (i % tps) == 0
        g = g_ref[...]
        sc1 = 1.0 + mod_ref[1:2, :]
        nb = max(tm // ROW_BAND, 1)
        bands = [slice(b * (tm // nb), (b + 1) * (tm // nb)) for b in range(nb)]
        dhs = [_dot_nt(dz_ref[rows, :], wt_ref[...]) for rows in bands]
        sums = [jnp.zeros((1, d), F32)] * 3
        col = lambda v: jnp.sum(v, axis=0, keepdims=True)
        for rows, dh in zip(bands, dhs):
            n, r = _rms(x_ref[rows, :])
            sums = [sums[0] + col(dh), sums[1] + col(dh * (n * g)), sums[2] + col((dh * sc1) * n)]
            gx_ref[rows, :] = dx1_ref[rows, :] + _rms_bwd(n, r, (dh * sc1) * g)
        _acc(dshift_ref, sums[0], first_seq)
        _acc(dscale_ref, sums[1], first_seq)
        _acc(dg1_ref, sums[2], i == 0)
        if ns:
            pl.when(i == n_steps - 1)(finish)

    nseq = t // (tm * tps)
    sv = _sds((nseq, 1, d), F32)
    res = pl.pallas_call(
        body, name="bwd_in", grid=(n_steps,),
        out_shape=(_sds((t, d), F32), sv, sv, _sds((1, d), F32)) + _scatter_shapes(scatter),
        in_specs=[_row(tm, npad), _row(tm, d), _row(tm, d), _full((1, d)), _modspec(d, tps), ANY] + [ANY] * ns,
        out_specs=(_row(tm, d), _seqv(d, tps), _seqv(d, tps), _full((1, d))) + (ANY,) * ns,
        scratch_shapes=[pltpu.VMEM(win_p.shape, BF16)] + (_scatter_sems(ns) if ns else []),
        compiler_params=_params("arbitrary"),
    )(dz, x, dx1, g1, mod3, win_p, *scatter)
    return res[0], res[1], res[2], res[3], res[4:]


def _tile_of(n, choices):
    for c in choices:
        if n % c == 0:
            return c
    return n


def _tn_matmul(a, b, name, col_shards=0):
    t, k = a.shape
    n = b.shape[1]
    tk = _tile_of(k, (1024, 512, 256, 128))
    tn = n // col_shards if col_shards else _tile_of(n, (1024, 896, 768, 512, 384, 256, 128))
    tt = _tile_of(t, (4096, 2048, 1024, 512, 256))

    def body(a_ref, b_ref, o_ref):
        _acc(o_ref, _dot_tn(a_ref[...], b_ref[...]), pl.program_id(2) == 0)

    if col_shards:
        out_shape, out_spec = _sds((col_shards, k, tn), F32), pl.BlockSpec((None, tk, tn), lambda i, j, s: (j, i, 0))
    else:
        out_shape, out_spec = _sds((k, n), F32), pl.BlockSpec((tk, tn), lambda i, j, s: (i, j))
    return pl.pallas_call(
        body, name=name, grid=(k // tk, n // tn, t // tt), out_shape=out_shape,
        in_specs=[pl.BlockSpec((tt, tk), lambda i, j, s: (s, i)), pl.BlockSpec((tt, tn), lambda i, j, s: (s, j))],
        out_specs=out_spec, compiler_params=_params("arbitrary", "arbitrary", "arbitrary"),
    )(a, b)


N_SHARD = 4
COL_SHARDED = ("w_in", "w_uq", "w_ukv", "w_o_mla", "w_pw_out", "w_ff1")
ROW_SHARDED = ("w_out", "w_ff2")
BIG = ("w_in", "w_uq", "w_ukv", "w_o_mla", "w_pw_out", "w_out", "w_ff1", "w_ff2")
SMALL = ("norm1_g", "q_latent_g", "kv_latent_g", "qk_norm_q_g", "qk_norm_k_g", "conv_b", "conv_ln_g", "conv_ln_b",
         "norm2_g")
WEIGHTS = ("w_ada", "b_ada", "norm1_g", "w_in", "q_latent_g", "w_uq", "kv_latent_g", "w_ukv", "qk_norm_q_g",
           "qk_norm_k_g", "w_o_mla", "conv_w", "conv_b", "conv_ln_g", "conv_ln_b", "w_pw_out", "w_out", "norm2_g",
           "w_ff1", "w_ff2")


def _pad_heads(w, width):
    k = w.shape[0]
    w3 = w.reshape(k, N_HEADS, width)
    return jnp.pad(w3, ((0, 0), (0, 0), (0, LANES - width))).reshape(k, HW)


def _unpad_heads(g, width):
    k = g.shape[0]
    return g.reshape(k, N_HEADS, LANES)[:, :, :width].reshape(k, N_HEADS * width)


def _win_segments(d):
    return [(OFF_GLU, OFF_GLU + 2 * d), (OFF_KR, OFF_GLU), (0, OFF_KV), KR_LANE, (OFF_KV, OFF_KR),
            LANES - KR_LANE - QK_ROPE]


def _pad_win(g4):
    _, d, ws = g4.shape
    parts = []
    for seg in _win_segments(d):
        if isinstance(seg, int):
            parts.append(jnp.zeros((d, seg), g4.dtype))
            continue
        a, b = seg
        while a < b:
            s = a // ws
            e = min(b, (s + 1) * ws)
            parts.append(g4[s, :, a - s * ws:e - s * ws])
            a = e
    return jnp.concatenate(parts, axis=1)


def _unpad_win(gp):
    d = gp.shape[0]
    ws = (OFF_GLU + 2 * d) // N_SHARD
    pieces, p = [], 0
    for seg in _win_segments(d):
        if isinstance(seg, int):
            p += seg
        else:
            pieces.append((seg[0], seg[1], p))
            p += seg[1] - seg[0]
    shards = []
    for s in range(N_SHARD):
        lo, hi = s * ws, (s + 1) * ws
        cols = [gp[:, p0 + max(a, lo) - a:p0 + min(b, hi) - a] for a, b, p0 in sorted(pieces) if max(a, lo) < min(b, hi)]
        shards.append(jnp.concatenate(cols, axis=1))
    return jnp.stack(shards)


def _col_shards(g):
    k, n = g.shape
    return g.reshape(k, N_SHARD, n // N_SHARD).transpose(1, 0, 2)


def _from_shards(g, name):
    ns, ks, nn = g.shape
    if name in ROW_SHARDED:
        return g.reshape(ns * ks, nn)
    return g.transpose(1, 0, 2).reshape(ks, ns * nn)


BY_SHARD = ("w_in", "w_ff1")
EARLY = ("w_in", "w_uq", "w_ukv")
LATE = ("w_o_mla", "w_pw_out", "w_out", "w_ff1", "w_ff2")


def _assemble(names, gathered):
    by_shard = {n: g.reshape((N_SHARD, 2 * g.shape[1]) + g.shape[2:]) for n, g in zip(names, gathered)}
    return {n: g if n in BY_SHARD else _from_shards(g, n) for n, g in by_shard.items()}


LARGE = ("w_in", "w_ff1", "w_ff2")
GROUP_A = ("w_out", "w_ff1", "w_ff2")
GROUP_B = ("w_in", "w_uq", "w_ukv", "w_o_mla", "w_pw_out")


def _pair_halves(g):
    return g.reshape(N_SHARD, 2, g.shape[1] // 2, g.shape[2])


def _pair_sums(names, halves, from_sibling):
    if not halves:
        return []
    cidx = lax.axis_index("c").reshape(1).astype(jnp.int32)
    out = {n: _add_pair(g, l, cidx, "pair_sum_" + n)
           for n, g, l in zip(names, halves, from_sibling) if n in LARGE}
    small = [j for j, n in enumerate(names) if n not in LARGE]
    if small:
        res = _add_pair_whole([halves[j] for j in small], [from_sibling[j] for j in small], cidx,
                              "pair_sum_small_" + names[small[0]])
        out.update({names[j]: r for j, r in zip(small, res)})
    return [out[n] for n in names]


def _local_step(x, target, mod, sp, w, late=None, tm=256):
    comm = late is not None
    w = dict(w)
    nseq, seq, d = x.shape
    t = nseq * seq
    tps = seq // tm
    xf = x.reshape(t, d)
    tg = target.reshape(t, d)
    mod3 = mod.reshape(nseq, N_MOD, d)

    win_p = _pad_win(w["w_in"])
    wuq_p = _pad_heads(w["w_uq"], QK_HEAD)
    wkv3 = w["w_ukv"].reshape(KV_RANK, N_HEADS, QK_NOPE + V_HEAD)
    wk_p = _pad_heads(wkv3[:, :, :QK_NOPE].reshape(KV_RANK, -1), QK_NOPE)
    wv_p = _pad_heads(wkv3[:, :, QK_NOPE:].reshape(KV_RANK, -1), V_HEAD)
    cw = jnp.pad(w["conv_w"], ((0, HALO - CONV_W), (0, 0)))
    pad_g = lambda g: jnp.pad(g, ((0, 0), (0, LANES - QK_HEAD)))
    gq, gk = pad_g(sp["qk_norm_q_g"]), pad_g(sp["qk_norm_k_g"])
    tabs = _rope_tables(seq)

    tm_in, tps_in = (2 * tm, tps // 2) if tps % 2 == 0 else (tm, tps)
    h, zm, zglu, zgate, u0 = _fwd_in(xf, sp["norm1_g"], mod3, win_p, tm_in, tps_in)
    q, k, v, qln, kvn = _mla_prep(zm, sp["q_latent_g"], sp["kv_latent_g"], gq, gk, tabs, wuq_p, wk_p, wv_p, tm, tps)
    attn, gathered = _attn_fwd(q, k, v, nseq, seq, tuple(late) if comm else ())
    if comm:
        w.update(_assemble(LATE, gathered))
    wo_p = jnp.pad(w["w_o_mla"].reshape(N_HEADS, V_HEAD, d), ((0, 0), (0, LANES - V_HEAD), (0, 0))).reshape(HW, d)
    x1, mixed, mpre, ya, yb, u1, u3 = _fwd_mix(attn, u0, zgate, xf, mod3, wo_p, cw, sp["conv_b"], sp["conv_ln_g"],
                                               sp["conv_ln_b"], w["w_pw_out"], w["w_out"], tm, tps)
    h2, a, r, dy, df, dgate2, loss_acc = _fwd_ffn(x1, tg, sp["norm2_g"], mod3, w["w_ff1"], w["w_ff2"], tm, tps)
    da, dx1, dmixed, dshift2, dscale2, dgate1, dg2 = _bwd_ffn(df, a, x1, dy, mixed, sp["norm2_g"], mod3,
                                                              w["w_ff2"], w["w_ff1"], tm, tps)
    gw = {
        "w_out": _tn_matmul(mpre, dmixed, "dw_out").reshape(N_SHARD, d // N_SHARD, d),
        "w_ff1": _tn_matmul(h2, da, "dw_ff1", N_SHARD),
        "w_ff2": _tn_matmul(r, df, "dw_ff2").reshape(N_SHARD, -1, d),
    }
    halves_a = [_pair_halves(gw[n]) for n in GROUP_A] if comm else []
    dya, dyb, dz, do, du1, dlng, dlnb, dcb, from_sibling = _bwd_mix(
        dmixed, zgate, ya, yb, u1, sp["conv_ln_g"], sp["conv_ln_b"], w["w_out"], wo_p, w["w_pw_out"], tm_in, tuple(halves_a))
    pair_a = _pair_sums(GROUP_A, halves_a, from_sibling)
    dz, dcw = _bwd_conv(dz, du1, u0, zglu, cw, tm, tps)
    gw["conv_w"] = dcw
    dq, dk, dv, land_a = _attn_bwd(q, k, v, do, nseq, seq, tuple(p[1] for p in pair_a))
    dz, dqpre, dkh, dgq, dgk, dgql, dgkvl = _mla_bwd(dz, dq, dk, dv, zm, sp["q_latent_g"], sp["kv_latent_g"], gq, gk,
                                                      tabs, wuq_p, wk_p, wv_p, tm, tps)
    dwk_p = _tn_matmul(kvn, dkh, "dw_uk")
    dwv_p = _tn_matmul(kvn, dv, "dw_uv")
    dwkv = jnp.concatenate([dwk_p.reshape(KV_RANK, N_HEADS, LANES)[:, :, :QK_NOPE],
                            dwv_p.reshape(KV_RANK, N_HEADS, LANES)[:, :, :V_HEAD]], axis=2).reshape(KV_RANK, -1)
    dwo = _tn_matmul(attn, dya, "dw_o").reshape(N_HEADS, LANES, d)[:, :V_HEAD].reshape(MLA_WIDTH, d)
    gw["w_in"] = _unpad_win(_tn_matmul(h, dz, "dw_in"))
    gw["w_uq"] = _col_shards(_unpad_heads(_tn_matmul(qln, dqpre, "dw_uq"), QK_HEAD))
    gw["w_ukv"] = _col_shards(dwkv)
    gw["w_o_mla"] = _col_shards(dwo)
    gw["w_pw_out"] = _tn_matmul(u3, dyb, "dw_pw", N_SHARD)
    pair_b = []
    if comm:
        halves_b = [_pair_halves(gw[n]) for n in GROUP_B]
        pair_b = _pair_sums(GROUP_B, halves_b, _pair_swap(halves_b, "grad_pair_swap"))
    gx, dshift1, dscale1, dg1, land_b = _bwd_in(dz, xf, dx1, sp["norm1_g"], mod3, win_p, tm_in, tps_in,
                                                tuple(p[1] for p in pair_b))
    if comm:
        for n, p, l in zip(GROUP_A + GROUP_B, pair_a + pair_b, land_a + land_b):
            gw[n] = (p[0], l)
    gs = {
        "norm1_g": dg1, "q_latent_g": dgql, "kv_latent_g": dgkvl, "qk_norm_q_g": dgq, "qk_norm_k_g": dgk,
        "conv_b": dcb, "conv_ln_g": dlng, "conv_ln_b": dlnb, "norm2_g": dg2,
    }
    dmod = jnp.concatenate([dshift1, dscale1, dgate1, dshift2, dscale2, dgate2], axis=2).reshape(nseq, N_MOD * d)
    return loss_acc, gx.reshape(nseq, seq, d), dmod, gw, gs


def kernel(x, c, w_ada, b_ada, norm1_g, w_in, q_latent_g, w_uq, kv_latent_g, w_ukv, qk_norm_q_g, qk_norm_k_g, w_o_mla, conv_w, conv_b, conv_ln_g, conv_ln_b, w_pw_out, w_out, norm2_g, w_ff1, w_ff2, loss_target, m_w_ada, m_b_ada, m_norm1_g, m_w_in, m_q_latent_g, m_w_uq, m_kv_latent_g, m_w_ukv, m_qk_norm_q_g, m_qk_norm_k_g, m_w_o_mla, m_conv_w, m_conv_b, m_conv_ln_g, m_conv_ln_b, m_w_pw_out, m_w_out, m_norm2_g, m_w_ff1, m_w_ff2, v_w_ada, v_b_ada, v_norm1_g, v_w_in, v_q_latent_g, v_w_uq, v_kv_latent_g, v_w_ukv, v_qk_norm_q_g, v_qk_norm_k_g, v_w_o_mla, v_conv_w, v_conv_b, v_conv_ln_g, v_conv_ln_b, v_w_pw_out, v_w_out, v_norm2_g, v_w_ff1, v_w_ff2):
    given = dict(locals())
    wts = {n: given[n][0] for n in WEIGHTS}
    mom = {n: given["m_" + n][0] for n in WEIGHTS}
    var = {n: given["v_" + n][0] for n in WEIGHTS}
    vec = lambda a: a.reshape(1, -1)
    nseq, seq, d = x.shape
    ix, iy, ic = _place()
    shard = 2 * ix + iy

    half = lambda n: lax.dynamic_slice_in_dim(wts[n].astype(BF16), ic * (wts[n].shape[0] // 2), wts[n].shape[0] // 2,
                                              axis=0)
    gathered = _all_gather8([half(n) for n in EARLY] + [wts["conv_w"], c], "gather_weights")
    full = _assemble(EARLY, gathered)
    full["conv_w"] = _from_shards(gathered[-2][0::2], "conv_w")
    c_all = gathered[-1].reshape(8 * nseq, d)

    n_ada = wts["w_ada"].shape[1]
    b_sh = lax.dynamic_slice_in_dim(vec(wts["b_ada"]), shard * n_ada, n_ada, axis=1)
    mod_sh = _ada_mod(c_all, wts["w_ada"], b_sh)
    hb = 4 * nseq
    mod_blk = lax.dynamic_slice_in_dim(mod_sh, ic * hb, hb, axis=0)
    (mod_all,) = _all_gather8([mod_blk], "gather_mod")
    mod_mine = lax.dynamic_slice_in_dim(mod_all, (2 * iy + ic) * nseq, nseq, axis=1)
    mod = jnp.concatenate([lax.dynamic_index_in_dim(mod_mine, 2 * s + ix, axis=0, keepdims=False)
                           for s in range(N_SHARD)], axis=1)

    sp = {n: vec(wts[n]) for n in SMALL}
    loss_part, grad_x, dmod, gw, gs = _local_step(x, loss_target, mod, sp, full, [half(n) for n in LATE])

    own_c = jnp.stack([shard, ic]).astype(jnp.int32)
    mine_sum = {n: _add_chips(gw[n][0], gw[n][1], own_c, "chip_sum_" + n) for n in LARGE}
    few = tuple(n for n in BIG if n not in LARGE)
    mine_sum.update(zip(few, _add_chips_whole([gw[n][0] for n in few], [gw[n][1] for n in few], own_c, "chip_sum_small")))
    summed, parts = _pair_gather_and_all_gather8(
        [mine_sum[n] for n in BIG], [dmod, gw["conv_w"], loss_part] + [gs[n] for n in SMALL], "tail_exchange")

    dmod_all = parts[0].reshape(8 * nseq, N_MOD * d)
    dmod_sh = lax.dynamic_slice_in_dim(dmod_all, shard * n_ada, n_ada, axis=1)
    res = _ada_bwd(c_all, dmod_all, dmod_sh, parts[1:])
    grads = {"w_ada": res[0], "b_ada": res[1]}
    n_cw = wts["conv_w"].shape[1]
    grads["conv_w"] = lax.dynamic_slice_in_dim(res[2], shard * n_cw, n_cw, axis=1)[:CONV_W]
    loss = res[3][0, 0]
    for n, g in zip(SMALL, res[4:]):
        grads[n] = g
    for n, g in zip(BIG, summed):
        grads[n] = g.reshape(wts[n].shape)

    delta, new_m, new_v = {}, {}, {}
    for n in LARGE + ("w_ada",):
        if n == "w_in":
            res = _adamw(wts[n].T, grads[n].T, mom[n].T, var[n].T, "adamw_" + n)
            delta[n], new_m[n], new_v[n] = (a.T for a in res)
        else:
            delta[n], new_m[n], new_v[n] = _adamw(wts[n], grads[n], mom[n], var[n], "adamw_" + n)
    rest = ("b_ada", "conv_w") + SMALL + few
    as2d = lambda a: a if a.ndim == 2 else vec(a)
    res = _adamw_small(*[[as2d(t[n]) for n in rest] for t in (wts, grads, mom, var)])
    for dst, arrs in zip((delta, new_m, new_v), res):
        for n, a in zip(rest, arrs):
            dst[n] = a

    outs = [loss, grad_x]
    for group in (grads, delta, new_m, new_v):
        outs += [group[n].reshape(given[n].shape) for n in WEIGHTS]
    return tuple(outs)
```

```python
import jax
import jax.numpy as jnp
from jax import lax
from jax.experimental import pallas as pl
from jax.experimental.pallas import tpu as pltpu

F32 = jnp.float32
BF16 = jnp.bfloat16
MESH = pl.DeviceIdType.MESH
ANY = pl.BlockSpec(memory_space=pl.ANY)

CHUNK = 64
CHUNK_SHIFT = 6
N_HEADS = 8
QK_NOPE = 64
QK_ROPE = 32
QK_HEAD = QK_NOPE + QK_ROPE
V_HEAD = 64
Q_RANK = 256
KV_RANK = 128
MLA_WIDTH = N_HEADS * V_HEAD
CONV_CH = 512
CONV_W = 31
ROPE_THETA = 10000.0
EPS = 1e-6
LANES = 128
SUBLANES = 8
HW = N_HEADS * LANES
OFF_KV = Q_RANK + KV_RANK
OFF_KR = OFF_KV + QK_ROPE
OFF_GLU = OFF_KR + 2 * CONV_CH
KR_LANE = QK_NOPE
MLA_IN = Q_RANK + KV_RANK + LANES
HALO = 32
N_MOD = 6

ADAM_LR = 0.001
ADAM_B1 = 0.9
ADAM_B2 = 0.999
ADAM_EPS = 1e-08
ADAM_WD = 0.01
ADAM_STEP = 10

VMEM_LIMIT = 56 * 1024 * 1024
BQ = 256


def _layout(d):
    p_glu = 2 * d
    p_q = p_glu + 2 * CONV_CH
    return p_glu, p_q, p_q + MLA_IN


def _params(*sem):
    return pltpu.CompilerParams(dimension_semantics=sem, vmem_limit_bytes=VMEM_LIMIT)


def _dot(a, b):
    return jnp.dot(a, b, preferred_element_type=F32)


def _dot_tn(a, b):
    return lax.dot_general(a, b, (((0,), (0,)), ((), ())), preferred_element_type=F32)


def _dot_nt(a, b):
    return lax.dot_general(a, b, (((1,), (1,)), ((), ())), preferred_element_type=F32)


def _acc(ref, val, first):
    @pl.when(first)
    def _():
        ref[...] = val

    @pl.when(jnp.logical_not(first))
    def _():
        ref[...] += val


def _rms(x):
    r = lax.rsqrt(jnp.mean(x * x, axis=-1, keepdims=True) + EPS)
    return x * r, r


def _rms_bwd(n, r, dn):
    return r * (dn - n * jnp.mean(dn * n, axis=-1, keepdims=True))


def _head_rms(sl):
    r = lax.rsqrt(jnp.sum(sl * sl, axis=-1, keepdims=True) * (1.0 / QK_HEAD) + EPS)
    return sl * r, r


def _head_rms_bwd(n, r, dn):
    return r * (dn - n * (jnp.sum(dn * n, axis=-1, keepdims=True) * (1.0 / QK_HEAD)))


def _rope(x, c, s1, s2):
    return x * c + pltpu.roll(x, QK_ROPE // 2, 1) * s1 + pltpu.roll(x, LANES - QK_ROPE // 2, 1) * s2


def _rope_t(dy, c, s1, s2):
    return dy * c + pltpu.roll(dy * s1, LANES - QK_ROPE // 2, 1) + pltpu.roll(dy * s2, QK_ROPE // 2, 1)


def _rope_tables(seq):
    half = QK_ROPE // 2
    inv_freq = ROPE_THETA ** (-jnp.arange(0, QK_ROPE, 2, dtype=F32) / QK_ROPE)
    ang = jnp.arange(seq, dtype=F32)[:, None] * inv_freq[None, :]
    cos, sin = jnp.cos(ang), jnp.sin(ang)
    z = lambda n: jnp.zeros((seq, n), F32)
    tail = LANES - QK_HEAD
    c = jnp.concatenate([jnp.ones((seq, QK_NOPE), F32), cos, cos, jnp.ones((seq, tail), F32)], axis=1)
    s1 = jnp.concatenate([z(QK_NOPE + half), sin, z(tail)], axis=1)
    s2 = jnp.concatenate([z(QK_NOPE), -sin, z(half + tail)], axis=1)
    return c, s1, s2


def _row(tm, w):
    return pl.BlockSpec((tm, w), lambda i: (i, 0))


def _modspec(d, tps):
    return pl.BlockSpec((None, N_MOD, d), lambda i: (i // tps, 0, 0))


def _seqv(w, tps):
    return pl.BlockSpec((None, 1, w), lambda i: (i // tps, 0, 0))


def _full(shape):
    return pl.BlockSpec(shape, lambda i: tuple(0 for _ in shape))


def _sds(shape, dtype):
    return jax.ShapeDtypeStruct(shape, dtype)


CONV_ROWS = 64
CONV_LC = CONV_CH // LANES


def _lane_chunks():
    return [(lc, slice(lc * LANES, (lc + 1) * LANES)) for lc in range(CONV_LC)]


def _fill_shifted(ext_ref, head, body):
    nh = head.shape[0]
    for lc, ls in _lane_chunks():
        ext_ref[0, lc, :nh, :] = head[:, ls]
        ext_ref[0, lc, nh:, :] = body[:, ls]
        rows = ext_ref[0, lc]
        for b in range(1, SUBLANES):
            ext_ref[b, lc] = pltpu.roll(rows, rows.shape[0] - b, 0)


def _shifted_shape(tm):
    return (SUBLANES, CONV_LC, tm + HALO, LANES)


def _conv_chunk(c):
    return c % CONV_LC, pl.multiple_of((c // CONV_LC) * CONV_ROWS, CONV_ROWS)


def _shifted(ext_ref, o, lc, r0):
    a = pl.multiple_of((o // SUBLANES) * SUBLANES + r0, SUBLANES)
    return ext_ref[o % SUBLANES, lc, pl.ds(a, CONV_ROWS), :]


def _by_lane_chunk(a):
    return a.reshape(a.shape[0], CONV_LC, LANES).transpose(1, 0, 2)


def _load_resident(i, pairs):
    @pl.when(i == 0)
    def _():
        for src, dst in pairs:
            pltpu.sync_copy(src, dst)


def _place():
    return lax.axis_index("x"), lax.axis_index("y"), lax.axis_index("c")


def _all_gather8(blocks, name):
    na = len(blocks)

    def body(*refs):
        start, forward, finish = _gather8_phases(refs[:na], refs[na:2 * na], *refs[2 * na:])
        start()
        forward()
        finish()

    outs = pl.pallas_call(
        body, name=name, out_shape=_gather8_shapes(blocks), in_specs=[ANY] * na, out_specs=(ANY,) * na,
        scratch_shapes=_gather8_sems(na),
    )(*blocks)
    return _own_block_placed(outs, blocks)


def _gather8_shapes(blocks):
    return tuple(_sds((8,) + b.shape, b.dtype) for b in blocks)


def _gather8_sems(na):
    return [pltpu.SemaphoreType.DMA((7 * na,)), pltpu.SemaphoreType.DMA((7 * na,))]


def _own_block_placed(outs, blocks):
    ix, iy, ic = _place()
    return tuple(lax.dynamic_update_index_in_dim(o, b, 4 * ix + 2 * iy + ic, 0) for o, b in zip(outs, blocks))


def _gather8_phases(x_refs, out_refs, send_sems, recv_sems):
    na = len(x_refs)
    x, y, c = _place()
    me, sibling = (x, y, c), (x, y, 1 - c)
    chips = [(1 - x, y), (x, 1 - y), (1 - x, 1 - y)]

    def copy(a, k, blk, to, from_input=False):
        dst = out_refs[a].at[4 * blk[0] + 2 * blk[1] + blk[2]]
        return pltpu.make_async_remote_copy(
            src_ref=x_refs[a] if from_input else dst, dst_ref=dst,
            send_sem=send_sems.at[7 * a + k], recv_sem=recv_sems.at[7 * a + k], device_id=to, device_id_type=MESH)

    def first(a):
        return [copy(a, 0, me, sibling, True)] + [copy(a, 1 + j, me, (*chip, c), True) for j, chip in enumerate(chips)]

    def start():
        for a in range(na):
            for cp in first(a):
                cp.start()

    def forward():
        for j, chip in enumerate(chips):
            for a in range(na):
                copy(a, 1 + j, (*chip, c), me).wait_recv()
                copy(a, 4 + j, (*chip, c), sibling).start()

    def finish():
        for a in range(na):
            copy(a, 0, sibling, me).wait_recv()
            for j, chip in enumerate(chips):
                copy(a, 4 + j, (*chip, 1 - c), me).wait_recv()
        for a in range(na):
            for cp in first(a) + [copy(a, 4 + j, (*chip, c), sibling) for j, chip in enumerate(chips)]:
                cp.wait_send()

    return start, forward, finish


def _pair_swap(gs, name):
    na = len(gs)

    def body(*refs):
        start, finish = _swap_phases(refs[:na], refs[na:2 * na], *refs[2 * na:])
        start()
        finish()

    return pl.pallas_call(
        body, name=name, out_shape=_swap_shapes(gs), in_specs=[ANY] * na, out_specs=(ANY,) * na,
        scratch_shapes=_swap_sems(gs),
    )(*gs)


def _swap_shapes(gs):
    return tuple(_sds(g.shape[:1] + g.shape[2:], g.dtype) for g in gs)


def _swap_sems(gs):
    n = sum(g.shape[0] for g in gs)
    return [pltpu.SemaphoreType.DMA((n,)), pltpu.SemaphoreType.DMA((n,))]


def _swap_phases(g_refs, land_refs, send_sems, recv_sems):
    x, y, c = _place()

    def copies():
        cps, k = [], 0
        for g_ref, land_ref in zip(g_refs, land_refs):
            for s in range(g_ref.shape[0]):
                cps.append(pltpu.make_async_remote_copy(
                    src_ref=g_ref.at[s, 1 - c], dst_ref=land_ref.at[s], send_sem=send_sems.at[k],
                    recv_sem=recv_sems.at[k], device_id=(x, y, 1 - c), device_id_type=MESH))
                k += 1
        return cps

    def start():
        for cp in copies():
            cp.start()

    def finish():
        for cp in copies():
            cp.wait()

    return start, finish


def _scatter_shapes(hs):
    return tuple(_sds((3,) + h.shape[1:], h.dtype) for h in hs)


def _scatter_sems(na):
    return [pltpu.SemaphoreType.DMA((3 * na,)), pltpu.SemaphoreType.DMA((3 * na,))]


def _scatter_phases(h_refs, land_refs, send_sems, recv_sems):
    x, y, c = _place()
    chips = [(1 - x, y), (x, 1 - y), (1 - x, 1 - y)]

    def copies():
        return [pltpu.make_async_remote_copy(
            src_ref=h_refs[a].at[2 * tx + ty], dst_ref=land_refs[a].at[j], send_sem=send_sems.at[3 * a + j],
            recv_sem=recv_sems.at[3 * a + j], device_id=(tx, ty, c), device_id_type=MESH)
            for a in range(len(h_refs)) for j, (tx, ty) in enumerate(chips)]

    def start():
        for cp in copies():
            cp.start()

    def finish():
        for cp in copies():
            cp.wait()

    return start, finish


def _pair_gather_and_all_gather8(fs, blocks, name):
    nf, nb = len(fs), len(blocks)

    def body(*refs):
        f_refs = refs[nf + nb:2 * nf + nb]
        b_out = refs[2 * nf + nb:2 * nf + 2 * nb]
        send_sems, recv_sems, g_send, g_recv = refs[2 * nf + 2 * nb:]
        x, y, c = _place()
        start, forward, finish = _gather8_phases(refs[nf:nf + nb], b_out, g_send, g_recv)
        sends = [pltpu.make_async_remote_copy(
            src_ref=f_refs[a].at[c], dst_ref=f_refs[a].at[c], send_sem=send_sems.at[a], recv_sem=recv_sems.at[a],
            device_id=(x, y, 1 - c), device_id_type=MESH) for a in range(nf)]
        recvs = [pltpu.make_async_remote_copy(
            src_ref=f_refs[a].at[c], dst_ref=f_refs[a].at[1 - c], send_sem=send_sems.at[a],
            recv_sem=recv_sems.at[a], device_id=(x, y, 1 - c), device_id_type=MESH) for a in range(nf)]
        start()
        for cp in sends:
            cp.start()
        forward()
        finish()
        for cp in recvs:
            cp.wait_recv()
        for cp in sends:
            cp.wait_send()

    res = pl.pallas_call(
        body, name=name, out_shape=tuple(_sds(f.shape, f.dtype) for f in fs) + _gather8_shapes(blocks),
        in_specs=[ANY] * (nf + nb), out_specs=(ANY,) * (nf + nb), input_output_aliases={a: a for a in range(nf)},
        scratch_shapes=[pltpu.SemaphoreType.DMA((nf,)), pltpu.SemaphoreType.DMA((nf,))] + _gather8_sems(nb),
    )(*fs, *blocks)
    return res[:nf], _own_block_placed(res[nf:], blocks)


def _row_tile(r, n, itemsize=4, budget=1 << 21):
    if r * n * itemsize <= budget:
        return r
    best = None
    for tr in range(16, r, 16):
        if r % tr == 0 and tr * n * itemsize <= budget:
            best = tr
    assert best is not None, (r, n)
    return best


def _add_pair(g, land, cidx, name):
    ns, _, r, n = g.shape
    tr = _row_tile(r, n)

    def body(c_ref, a_ref, b_ref, o_ref, ob_ref):
        s = a_ref[...] + b_ref[...]
        o_ref[...] = s
        ob_ref[...] = s.astype(BF16)

    out = pl.BlockSpec((None, tr, n), lambda s, i, cr: (s, i, 0))
    return pl.pallas_call(
        body, name=name, out_shape=(_sds((ns, r, n), F32), _sds((ns, r, n), BF16)),
        grid_spec=pltpu.PrefetchScalarGridSpec(
            num_scalar_prefetch=1, grid=(ns, r // tr),
            in_specs=[pl.BlockSpec((None, None, tr, n), lambda s, i, cr: (s, cr[0], i, 0)), out],
            out_specs=(out, out)),
        compiler_params=_params("arbitrary", "arbitrary"),
    )(cidx, g, land)


def _add_pair_whole(gs, lands, cidx, name):
    k = len(gs)

    def body(c_ref, *refs):
        for a_ref, b_ref, o_ref, ob_ref in zip(refs[:k], refs[k:2 * k], refs[2 * k:3 * k], refs[3 * k:]):
            s = a_ref[...] + b_ref[...]
            o_ref[...] = s
            ob_ref[...] = s.astype(BF16)

    half = lambda g: pl.BlockSpec((g.shape[0], None) + g.shape[2:], lambda i, cr: (0, cr[0], 0, 0))
    whole = lambda g: pl.BlockSpec(g.shape[:1] + g.shape[2:], lambda i, cr: (0, 0, 0))
    shapes = lambda dt: tuple(_sds(g.shape[:1] + g.shape[2:], dt) for g in gs)
    res = pl.pallas_call(
        body, name=name, out_shape=shapes(F32) + shapes(BF16),
        grid_spec=pltpu.PrefetchScalarGridSpec(
            num_scalar_prefetch=1, grid=(1,),
            in_specs=[half(g) for g in gs] + [whole(g) for g in gs],
            out_specs=tuple(whole(g) for g in gs) * 2),
        compiler_params=_params("arbitrary"),
    )(cidx, *gs, *lands)
    return list(zip(res[:k], res[k:]))


def _add_chips_whole(hs, lands, own_c, name):
    k = len(hs)

    def body(o_idx, *refs):
        for h_ref, l_ref, o_ref in zip(refs[:k], refs[k:2 * k], refs[2 * k:]):
            o_ref[...] = ((h_ref[...] + l_ref[0].astype(F32)) + l_ref[1].astype(F32)) + l_ref[2].astype(F32)

    return pl.pallas_call(
        body, name=name, out_shape=tuple(_sds((2,) + h.shape[1:], F32) for h in hs),
        grid_spec=pltpu.PrefetchScalarGridSpec(
            num_scalar_prefetch=1, grid=(1,),
            in_specs=[pl.BlockSpec((None,) + h.shape[1:], lambda i, o: (o[0], 0, 0)) for h in hs]
            + [pl.BlockSpec(l.shape, lambda i, o: (0, 0, 0)) for l in lands],
            out_specs=tuple(pl.BlockSpec((None,) + h.shape[1:], lambda i, o: (o[1], 0, 0)) for h in hs)),
        compiler_params=_params("arbitrary"),
    )(own_c, *hs, *lands)


def _add_chips(h, land, own_c, name):
    _, r, n = h.shape
    tr = _row_tile(r, n)

    def body(o_idx, h_ref, l_ref, o_ref):
        o_ref[...] = ((h_ref[...] + l_ref[0].astype(F32)) + l_ref[1].astype(F32)) + l_ref[2].astype(F32)

    return pl.pallas_call(
        body, name=name, out_shape=_sds((2, r, n), F32),
        grid_spec=pltpu.PrefetchScalarGridSpec(
            num_scalar_prefetch=1, grid=(r // tr,),
            in_specs=[pl.BlockSpec((None, tr, n), lambda i, o: (o[0], i, 0)),
                      pl.BlockSpec((3, tr, n), lambda i, o: (0, i, 0))],
            out_specs=pl.BlockSpec((None, tr, n), lambda i, o: (o[1], i, 0))),
        compiler_params=_params("arbitrary"),
    )(own_c, h, land)


def _adam_math(w, g, m, v):
    nm = ADAM_B1 * m + (1.0 - ADAM_B1) * g
    nv = ADAM_B2 * v + (1.0 - ADAM_B2) * (g * g)
    m_hat = nm / (1.0 - ADAM_B1 ** ADAM_STEP)
    v_hat = nv / (1.0 - ADAM_B2 ** ADAM_STEP)
    return -ADAM_LR * (m_hat / (jnp.sqrt(v_hat) + ADAM_EPS) + ADAM_WD * w), nm, nv


def _adamw(w, g, m, v, name):
    r, n = w.shape

    def body(w_ref, g_ref, m_ref, v_ref, d_ref, nm_ref, nv_ref):
        d_ref[...], nm_ref[...], nv_ref[...] = _adam_math(w_ref[...], g_ref[...], m_ref[...], v_ref[...])

    if r % 16 == 0:
        tr = _row_tile(r, n)
        steps, spec = r // tr, pl.BlockSpec((tr, n), lambda i: (i, 0))
    else:
        tc = 4 * LANES
        steps, spec = n // tc, pl.BlockSpec((r, tc), lambda j: (0, j))
    return pl.pallas_call(
        body, name=name, out_shape=(_sds((r, n), F32),) * 3, grid=(steps,),
        in_specs=[spec] * 4, out_specs=(spec,) * 3, compiler_params=_params("arbitrary"),
    )(w, g, m, v)


def _adamw_small(ws, gs, ms, vs):
    k = len(ws)

    def body(*refs):
        ins, outs = refs[:4 * k], refs[4 * k:]
        for j in range(k):
            d, nm, nv = _adam_math(ins[j][...], ins[k + j][...], ins[2 * k + j][...], ins[3 * k + j][...])
            outs[j][...] = d
            outs[k + j][...] = nm
            outs[2 * k + j][...] = nv

    shapes = tuple(_sds(w.shape, F32) for w in ws)
    res = pl.pallas_call(body, name="adamw_small", out_shape=shapes * 3,
                         compiler_params=pltpu.CompilerParams(vmem_limit_bytes=VMEM_LIMIT))(*ws, *gs, *ms, *vs)
    return res[:k], res[k:2 * k], res[2 * k:]


def _ada_mod(c_all, w_sh, b_sh):
    b, _ = c_all.shape
    n = w_sh.shape[1]

    def body(c_ref, w_ref, b_ref, o_ref):
        cc = c_ref[...]
        ca = (cc * jax.nn.sigmoid(cc)).astype(BF16)
        o_ref[...] = _dot(ca, w_ref[...].astype(BF16)) + b_ref[...]

    return pl.pallas_call(body, name="ada_mod", out_shape=_sds((b, n), F32),
                          compiler_params=pltpu.CompilerParams(vmem_limit_bytes=VMEM_LIMIT))(c_all, w_sh, b_sh)


def _ada_bwd(c_all, dmod_all, dmod_sh, parts):
    b, d = c_all.shape
    n6 = dmod_all.shape[1]
    n = dmod_sh.shape[1]
    k = len(parts)

    def body(*refs):
        c_ref, da_ref, ds_ref = refs[:3]
        p_refs = refs[3:3 + k]
        dw_ref, db_ref = refs[3 + k:5 + k]
        s_refs = refs[5 + k:]
        cc = c_ref[...]
        ca = (cc * jax.nn.sigmoid(cc)).astype(BF16)
        dw_ref[...] = _dot_tn(ca, ds_ref[...].astype(BF16))
        db_ref[...] = jnp.sum(da_ref[...], axis=0, keepdims=True)
        for p_ref, s_ref in zip(p_refs, s_refs):
            tot = p_ref[0]
            for j in range(1, p_ref.shape[0]):
                tot = tot + p_ref[j]
            s_ref[...] = tot

    return pl.pallas_call(
        body, name="ada_bwd",
        out_shape=(_sds((d, n), F32), _sds((1, n6), F32)) + tuple(_sds(p.shape[1:], F32) for p in parts),
        compiler_params=pltpu.CompilerParams(vmem_limit_bytes=VMEM_LIMIT),
    )(c_all, dmod_all, dmod_sh, *parts)


def _fwd_in(x, g1, mod3, win_p, tm, tps):
    t, d = x.shape
    p_glu, p_q, npad = _layout(d)

    def body(x_ref, g_ref, mod_ref, w_hbm, h_ref, zm_ref, zglu_ref, zgate_ref, u0_ref, w_ref):
        _load_resident(pl.program_id(0), [(w_hbm, w_ref)])
        n, _ = _rms(x_ref[...])
        h = ((n * g_ref[...]) * (1.0 + mod_ref[1:2, :]) + mod_ref[0:1, :]).astype(BF16)
        h_ref[...] = h
        z = _dot(h, w_ref[...])
        zgate_ref[...] = z[:, :p_glu]
        zglu = z[:, p_glu:p_q]
        zglu_ref[...] = zglu
        zm_ref[...] = z[:, p_q:]
        u0_ref[...] = zglu[:, :CONV_CH] * jax.nn.sigmoid(zglu[:, CONV_CH:])

    return pl.pallas_call(
        body, name="fwd_in", grid=(t // tm,),
        out_shape=(_sds((t, d), BF16), _sds((t, MLA_IN), F32), _sds((t, 2 * CONV_CH), F32), _sds((t, 2 * d), F32),
                   _sds((t, CONV_CH), F32)),
        in_specs=[_row(tm, d), _full((1, d)), _modspec(d, tps), ANY],
        out_specs=(_row(tm, d), _row(tm, MLA_IN), _row(tm, 2 * CONV_CH), _row(tm, 2 * d), _row(tm, CONV_CH)),
        scratch_shapes=[pltpu.VMEM(win_p.shape, BF16)],
        compiler_params=_params("arbitrary"),
    )(x, g1, mod3, win_p)


def _mla_prep(zm, gql, gkvl, gq, gk, tabs, wuq_p, wk_p, wv_p, tm, tps):
    t = zm.shape[0]
    c_t, s1_t, s2_t = tabs
    tab = pl.BlockSpec((tm, LANES), lambda i: (i % tps, 0))

    def body(zm_ref, gql_ref, gkvl_ref, gq_ref, gk_ref, c_ref, s1_ref, s2_ref, wuq_ref, wk_ref, wv_ref,
             q_ref, k_ref, v_ref, qln_ref, kvn_ref):
        c, s1, s2 = c_ref[...], s1_ref[...], s2_ref[...]
        nq, _ = _rms(zm_ref[:, :Q_RANK])
        qln = (nq * gql_ref[...]).astype(BF16)
        qln_ref[...] = qln
        qpre = _dot(qln, wuq_ref[...])
        nkv, _ = _rms(zm_ref[:, Q_RANK:OFF_KV])
        kvn = (nkv * gkvl_ref[...]).astype(BF16)
        kvn_ref[...] = kvn
        knope = _dot(kvn, wk_ref[...])
        v_ref[...] = _dot(kvn, wv_ref[...]).astype(BF16)
        zkr_v = zm_ref[:, OFF_KV:]
        kr_roped = _rope(zkr_v * gk_ref[...], c, s1, s2)
        slabs = [slice(hd * LANES, (hd + 1) * LANES) for hd in range(N_HEADS)]
        rq = [_head_rms(qpre[:, sl])[1] for sl in slabs]
        rk = [_head_rms(knope[:, sl] + zkr_v)[1] for sl in slabs]
        for hd, sl in enumerate(slabs):
            q_ref[:, sl] = _rope((qpre[:, sl] * rq[hd]) * gq_ref[...], c, s1, s2).astype(BF16)
            k_ref[:, sl] = (rk[hd] * (knope[:, sl] * gk_ref[...] + kr_roped)).astype(BF16)

    return pl.pallas_call(
        body, name="mla_prep", grid=(t // tm,),
        out_shape=(_sds((t, HW), BF16),) * 3 + (_sds((t, Q_RANK), BF16), _sds((t, KV_RANK), BF16)),
        in_specs=[_row(tm, MLA_IN), _full((1, Q_RANK)), _full((1, KV_RANK)),
                  _full((1, LANES)), _full((1, LANES)), tab, tab, tab,
                  _full(wuq_p.shape), _full(wk_p.shape), _full(wv_p.shape)],
        out_specs=(_row(tm, HW),) * 3 + (_row(tm, Q_RANK), _row(tm, KV_RANK)),
        compiler_params=_params("arbitrary"),
    )(zm, gql, gkvl, gq, gk, c_t, s1_t, s2_t, wuq_p, wk_p, wv_p)


AHEAD = 2
ROW_BAND = 256
SM_SCALE = QK_HEAD ** -0.5
EXP2_SCALE = SM_SCALE * 1.4426950408889634


def _diag_mask():
    rc = jnp.right_shift(lax.broadcasted_iota(jnp.int32, (BQ, 1), 0), CHUNK_SHIFT)
    cc = jnp.right_shift(lax.broadcasted_iota(jnp.int32, (1, BQ), 1), CHUNK_SHIFT)
    return rc >= cc


def _scores(q_i, k_ref, lo, e):
    return (_dot_nt(q_i, k_ref[:lo, :]) if lo else None), _dot_nt(q_i, k_ref[lo:e, :])


def _softmax_parts(scores, mask):
    sp, sd = scores
    sd = jnp.where(mask, sd, jnp.finfo(F32).min)
    m = jnp.max(sd, axis=-1, keepdims=True)
    if sp is not None:
        m = jnp.maximum(m, jnp.max(sp, axis=-1, keepdims=True))
    pd = jnp.exp2((sd - m) * EXP2_SCALE)
    l = jnp.sum(pd, axis=-1, keepdims=True)
    pp = None
    if sp is not None:
        pp = jnp.exp2((sp - m) * EXP2_SCALE)
        l = l + jnp.sum(pp, axis=-1, keepdims=True)
    return pp, pd, l


def _attn_fwd(q, k, v, nseq, seq, gather=()):
    t = q.shape[0]
    na = len(gather)
    blk = pl.BlockSpec((seq, LANES), lambda b, h: (b, h))
    n_steps = nseq * N_HEADS

    def body(q_ref, k_ref, v_ref, *rest):
        o_ref = rest[na]
        if na:
            start, forward, finish = _gather8_phases(rest[:na], rest[na + 1:2 * na + 1], *rest[2 * na + 1:])
            step = pl.program_id(0) * N_HEADS + pl.program_id(1)
            pl.when(step == 0)(start)
            pl.when(step == (7 * n_steps) // 8)(forward)
        mask = _diag_mask()
        nb = seq // BQ
        block_scores = lambda j: _scores(q_ref[j * BQ:(j + 1) * BQ, :], k_ref, j * BQ, (j + 1) * BQ)
        ahead = [block_scores(j) for j in range(min(AHEAD, nb))]
        for i in range(nb):
            lo, e = i * BQ, (i + 1) * BQ
            cur = ahead.pop(0)
            if i + AHEAD < nb:
                ahead.append(block_scores(i + AHEAD))
            pp, pd, l = _softmax_parts(cur, mask)
            o = _dot(pd.astype(BF16), v_ref[lo:e, :])
            if lo:
                o = o + _dot(pp.astype(BF16), v_ref[:lo, :])
            o_ref[lo:e, :] = (o * (1.0 / l)).astype(BF16)
        if na:
            pl.when(step == n_steps - 1)(finish)

    res = pl.pallas_call(
        body, name="attn_fwd", grid=(nseq, N_HEADS), out_shape=(_sds((t, HW), BF16),) + _gather8_shapes(gather),
        in_specs=[blk, blk, blk] + [ANY] * na, out_specs=(blk,) + (ANY,) * na,
        scratch_shapes=_gather8_sems(na) if na else [],
        compiler_params=_params("arbitrary", "arbitrary"),
    )(q, k, v, *gather)
    return res[0], (_own_block_placed(res[1:], gather) if na else ())


def _fwd_mix(attn, u0, zgate, x, mod3, wo_p, cw, cb, lng, lnb, wpw, wout, tm, tps):
    t, d = x.shape
    hpt = tm // HALO
    cwc, cbc = _by_lane_chunk(cw), _by_lane_chunk(cb)

    def body(a_ref, u_ref, uh_ref, zg_ref, x_ref, mod_ref, wo_ref, cw_ref, cb_ref, lng_ref, lnb_ref, wpw_ref, wout_ref,
             x1_ref, mixed_ref, mpre_ref, ya_ref, yb_ref, u1_ref, u3_ref, ext_ref):
        i = pl.program_id(0)
        ya = _dot(a_ref[...], wo_ref[...])
        ya_ref[...] = ya
        first = (i % tps) == 0
        _fill_shifted(ext_ref, jnp.where(first, 0.0, uh_ref[...]), u_ref[...])
        for lc, ls in _lane_chunks():
            acc = jnp.broadcast_to(cb_ref[lc], (tm, LANES))
            for kk in range(CONV_W):
                o = HALO - (CONV_W - 1) + kk
                a = (o // SUBLANES) * SUBLANES
                acc = acc + cw_ref[lc, kk:kk + 1, :] * ext_ref[o % SUBLANES, lc, a:a + tm, :]
            u1_ref[:, ls] = acc
        acc = u1_ref[...]
        mu = jnp.mean(acc, axis=-1, keepdims=True)
        xc = acc - mu
        rstd = lax.rsqrt(jnp.mean(xc * xc, axis=-1, keepdims=True) + EPS)
        l = (xc * rstd) * lng_ref[...] + lnb_ref[...]
        u3 = (l * jax.nn.sigmoid(l)).astype(BF16)
        u3_ref[...] = u3
        yb = _dot(u3, wpw_ref[...])
        yb_ref[...] = yb
        zg = zg_ref[...]
        mpre = (jax.nn.sigmoid(zg[:, :d]) * ya + jax.nn.sigmoid(zg[:, d:]) * yb).astype(BF16)
        mpre_ref[...] = mpre
        mixed = _dot(mpre, wout_ref[...])
        mixed_ref[...] = mixed
        x1_ref[...] = x_ref[...] + mod_ref[2:3, :] * mixed

    halo = pl.BlockSpec((HALO, CONV_CH), lambda i: (jnp.maximum(i * hpt - 1, 0), 0))
    return pl.pallas_call(
        body, name="fwd_mix", grid=(t // tm,),
        out_shape=(_sds((t, d), F32), _sds((t, d), F32), _sds((t, d), BF16), _sds((t, d), F32), _sds((t, d), F32),
                   _sds((t, CONV_CH), F32), _sds((t, CONV_CH), BF16)),
        in_specs=[_row(tm, HW), _row(tm, CONV_CH), halo, _row(tm, 2 * d), _row(tm, d), _modspec(d, tps),
                  _full(wo_p.shape), _full(cwc.shape), _full(cbc.shape), _full((1, CONV_CH)), _full((1, CONV_CH)),
                  _full(wpw.shape), _full(wout.shape)],
        out_specs=(_row(tm, d), _row(tm, d), _row(tm, d), _row(tm, d), _row(tm, d), _row(tm, CONV_CH),
                   _row(tm, CONV_CH)),
        scratch_shapes=[pltpu.VMEM(_shifted_shape(tm), F32)],
        compiler_params=_params("arbitrary"),
    )(attn, u0, u0, zgate, x, mod3, wo_p, cwc, cbc, lng, lnb, wpw, wout)


def _shards_into_columns(w_hbm, w_ref):
    ns = w_hbm.shape[2]
    return [(w_hbm.at[s], w_ref.at[:, pl.ds(s * ns, ns)]) for s in range(w_hbm.shape[0])]


def _fwd_ffn(x1, target, g2, mod3, w1, w2, tm, tps):
    t, d = x1.shape
    dff = w1.shape[0] * w1.shape[2]

    def body(x1_ref, tg_ref, g_ref, mod_ref, w1_hbm, w2_hbm,
             h2_ref, a_ref, r_ref, dy_ref, df_ref, dgate_ref, loss_ref, w1_ref, w2_ref):
        i = pl.program_id(0)
        _load_resident(i, _shards_into_columns(w1_hbm, w1_ref) + [(w2_hbm, w2_ref)])
        x1v = x1_ref[...]
        gate2 = mod_ref[5:6, :]
        n, _ = _rms(x1v)
        h2 = ((n * g_ref[...]) * (1.0 + mod_ref[4:5, :]) + mod_ref[3:4, :]).astype(BF16)
        h2_ref[...] = h2
        a = _dot(h2, w1_ref[...])
        a_ref[...] = a
        r = jnp.square(jnp.maximum(a, 0.0)).astype(BF16)
        r_ref[...] = r
        f = _dot(r, w2_ref[...])
        e = (x1v + gate2 * f) - tg_ref[...]
        part = 0.5 * jnp.sum(jnp.mean(e * e, axis=-1, keepdims=True), axis=0, keepdims=True)
        _acc(loss_ref, jnp.broadcast_to(part, loss_ref.shape), i == 0)
        dy = e * (1.0 / d)
        dy_ref[...] = dy
        df_ref[...] = (dy * gate2).astype(BF16)
        _acc(dgate_ref, jnp.sum(dy * f, axis=0, keepdims=True), (i % tps) == 0)

    nseq = t // (tm * tps)
    return pl.pallas_call(
        body, name="fwd_ffn", grid=(t // tm,),
        out_shape=(_sds((t, d), BF16), _sds((t, dff), F32), _sds((t, dff), BF16), _sds((t, d), F32), _sds((t, d), BF16),
                   _sds((nseq, 1, d), F32), _sds((8, LANES), F32)),
        in_specs=[_row(tm, d), _row(tm, d), _full((1, d)), _modspec(d, tps), ANY, ANY],
        out_specs=(_row(tm, d), _row(tm, dff), _row(tm, dff), _row(tm, d), _row(tm, d), _seqv(d, tps),
                   _full((8, LANES))),
        scratch_shapes=[pltpu.VMEM((d, dff), BF16), pltpu.VMEM(w2.shape, BF16)],
        compiler_params=_params("arbitrary"),
    )(x1, target, g2, mod3, w1, w2)


def _bwd_ffn(df, a, x1, dy, mixed, g2, mod3, w2, w1, tm, tps):
    t, d = x1.shape
    dff = a.shape[1]

    def body(df_ref, a_ref, x1_ref, dy_ref, mx_ref, g_ref, mod_ref, w2_hbm, w1_hbm,
             da_ref, dx1_ref, dmixed_ref, dshift_ref, dscale_ref, dgate1_ref, dg2_ref, w2_ref, w1_ref):
        i = pl.program_id(0)
        _load_resident(i, [(w2_hbm, w2_ref)] + _shards_into_columns(w1_hbm, w1_ref))
        first_seq = (i % tps) == 0
        dr = _dot_nt(df_ref[...], w2_ref[...])
        da = (dr * (2.0 * jnp.maximum(a_ref[...], 0.0))).astype(BF16)
        da_ref[...] = da
        dh2 = _dot_nt(da, w1_ref[...])
        n, r = _rms(x1_ref[...])
        g = g_ref[...]
        sc1 = 1.0 + mod_ref[4:5, :]
        _acc(dshift_ref, jnp.sum(dh2, axis=0, keepdims=True), first_seq)
        _acc(dscale_ref, jnp.sum(dh2 * (n * g), axis=0, keepdims=True), first_seq)
        _acc(dg2_ref, jnp.sum((dh2 * sc1) * n, axis=0, keepdims=True), i == 0)
        dx1 = dy_ref[...] + _rms_bwd(n, r, (dh2 * sc1) * g)
        dx1_ref[...] = dx1
        _acc(dgate1_ref, jnp.sum(dx1 * mx_ref[...], axis=0, keepdims=True), first_seq)
        dmixed_ref[...] = (dx1 * mod_ref[2:3, :]).astype(BF16)

    nseq = t // (tm * tps)
    sv = _sds((nseq, 1, d), F32)
    return pl.pallas_call(
        body, name="bwd_ffn", grid=(t // tm,),
        out_shape=(_sds((t, dff), BF16), _sds((t, d), F32), _sds((t, d), BF16), sv, sv, sv, _sds((1, d), F32)),
        in_specs=[_row(tm, d), _row(tm, dff), _row(tm, d), _row(tm, d), _row(tm, d), _full((1, d)), _modspec(d, tps),
                  ANY, ANY],
        out_specs=(_row(tm, dff), _row(tm, d), _row(tm, d), _seqv(d, tps), _seqv(d, tps), _seqv(d, tps),
                   _full((1, d))),
        scratch_shapes=[pltpu.VMEM(w2.shape, BF16), pltpu.VMEM((d, dff), BF16)],
        compiler_params=_params("arbitrary"),
    )(df, a, x1, dy, mixed, g2, mod3, w2, w1)


def _bwd_mix(dmixed, zgate, ya, yb, u1, lng, lnb, wout, wo_p, wpw, tm, swap=()):
    t, d = ya.shape
    _, _, npad = _layout(d)
    nw = len(swap)
    n_steps = t // tm

    def body(dm_ref, zg_ref, ya_ref, yb_ref, u1_ref, lng_ref, lnb_ref, wout_ref, wo_ref, wpw_ref, *rest):
        dya_ref, dyb_ref, dz_ref, do_ref, du1_ref, dlng_ref, dlnb_ref, dcb_ref = rest[nw:nw + 8]
        i = pl.program_id(0)
        if nw:
            start, finish = _swap_phases(rest[:nw], rest[nw + 8:2 * nw + 8], *rest[2 * nw + 8:])
            pl.when(i == 0)(start)
        nb = max(tm // ROW_BAND, 1)
        bands = [slice(b * (tm // nb), (b + 1) * (tm // nb)) for b in range(nb)]
        col = lambda v: jnp.sum(v, axis=0, keepdims=True)
        dmpre = [_dot_nt(dm_ref[rows, :], wout_ref[...]) for rows in bands]
        dyab = []
        for rows, dmp in zip(bands, dmpre):
            ga = jax.nn.sigmoid(zg_ref[rows, :d])
            gb = jax.nn.sigmoid(zg_ref[rows, d:])
            dya = (dmp * ga).astype(BF16)
            dyb = (dmp * gb).astype(BF16)
            dya_ref[rows, :] = dya
            dyb_ref[rows, :] = dyb
            dz_ref[rows, :d] = ((dmp * ya_ref[rows, :]) * (ga * (1.0 - ga))).astype(BF16)
            dz_ref[rows, d:] = ((dmp * yb_ref[rows, :]) * (gb * (1.0 - gb))).astype(BF16)
            dyab.append((dya, dyb))
        du3s = []
        for rows, (dya, dyb) in zip(bands, dyab):
            do_ref[rows, :] = _dot_nt(dya, wo_ref[...]).astype(BF16)
            du3s.append(_dot_nt(dyb, wpw_ref[...]))
        sums = [jnp.zeros((1, CONV_CH), F32)] * 3
        for rows, du3 in zip(bands, du3s):
            u1 = u1_ref[rows, :]
            mu = jnp.mean(u1, axis=-1, keepdims=True)
            xc = u1 - mu
            rstd = lax.rsqrt(jnp.mean(xc * xc, axis=-1, keepdims=True) + EPS)
            nh = xc * rstd
            l = nh * lng_ref[...] + lnb_ref[...]
            sg = jax.nn.sigmoid(l)
            dl = du3 * (sg * (1.0 + l * (1.0 - sg)))
            dnh = dl * lng_ref[...]
            du1 = rstd * (dnh - jnp.mean(dnh, axis=-1, keepdims=True)
                          - nh * jnp.mean(dnh * nh, axis=-1, keepdims=True))
            du1_ref[rows, :] = du1
            sums = [sums[0] + col(dl * nh), sums[1] + col(dl), sums[2] + col(du1)]
        _acc(dlng_ref, sums[0], i == 0)
        _acc(dlnb_ref, sums[1], i == 0)
        _acc(dcb_ref, sums[2], i == 0)
        if nw:
            pl.when(i == n_steps - 1)(finish)

    cv = _sds((1, CONV_CH), F32)
    res = pl.pallas_call(
        body, name="bwd_mix", grid=(n_steps,),
        out_shape=(_sds((t, d), BF16), _sds((t, d), BF16), _sds((t, npad), BF16), _sds((t, HW), BF16),
                   _sds((t, CONV_CH), F32), cv, cv, cv) + _swap_shapes(swap),
        in_specs=[_row(tm, d), _row(tm, 2 * d), _row(tm, d), _row(tm, d), _row(tm, CONV_CH), _full((1, CONV_CH)),
                  _full((1, CONV_CH)), _full(wout.shape), _full(wo_p.shape), _full(wpw.shape)] + [ANY] * nw,
        out_specs=(_row(tm, d), _row(tm, d), _row(tm, 2 * d), _row(tm, HW), _row(tm, CONV_CH),
                   _full((1, CONV_CH)), _full((1, CONV_CH)), _full((1, CONV_CH))) + (ANY,) * nw,
        scratch_shapes=_swap_sems(swap) if nw else [],
        compiler_params=_params("arbitrary"),
    )(dmixed, zgate, ya, yb, u1, lng, lnb, wout, wo_p, wpw, *swap)
    return res[:8] + (res[8:],)


def _bwd_conv(dz, du1, u0, zglu, cw, tm, tps):
    t = du1.shape[0]
    d = (dz.shape[1] - MLA_IN - 2 * CONV_CH) // 2
    p_glu, _, _ = _layout(d)
    hpt = tm // HALO
    last_blk = t // HALO - 1
    cwc = _by_lane_chunk(cw)

    def body(dz_hbm, du_ref, dun_ref, u_ref, zl_ref, cw_ref, dzl_ref, dcw_ref, dext_ref, uc_ref, dcw8_ref, du0_ref):
        i = pl.program_id(0)
        last = (i % tps) == (tps - 1)
        _fill_shifted(dext_ref, du_ref[...], jnp.where(last, 0.0, dun_ref[...]))
        for lc, ls in _lane_chunks():
            uc_ref[lc] = u_ref[:, ls]

        @pl.when(i == 0)
        def _():
            dcw8_ref[...] = jnp.zeros_like(dcw8_ref)

        groups = CONV_ROWS // SUBLANES

        def conv_chunk(c, carry):
            lc, r0 = _conv_chunk(c)
            u = uc_ref[lc, pl.ds(r0, CONV_ROWS), :]
            du0 = jnp.zeros((CONV_ROWS, LANES), F32)
            for kk in range(CONV_W):
                win = _shifted(dext_ref, CONV_W - 1 - kk, lc, r0)
                prod = u * win
                part = prod[:SUBLANES]
                for g in range(1, groups):
                    part = part + prod[g * SUBLANES:(g + 1) * SUBLANES]
                dcw8_ref[lc, kk] += part
                du0 = du0 + cw_ref[lc, kk:kk + 1, :] * win
            du0_ref[lc, pl.ds(r0, CONV_ROWS), :] = du0
            return carry

        lax.fori_loop(0, CONV_LC * (tm // CONV_ROWS), conv_chunk, 0)

        @pl.when(i == pl.num_programs(0) - 1)
        def _():
            for lc, ls in _lane_chunks():
                dcw_ref[:, ls] = jnp.sum(dcw8_ref[lc], axis=1)

        for lc, ls in _lane_chunks():
            du0 = du0_ref[lc]
            ga = zl_ref[:, ls]
            sb = jax.nn.sigmoid(zl_ref[:, CONV_CH + lc * LANES:CONV_CH + (lc + 1) * LANES])
            dzl_ref[:, ls] = (du0 * sb).astype(BF16)
            dzl_ref[:, CONV_CH + lc * LANES:CONV_CH + (lc + 1) * LANES] = ((du0 * ga) * (sb * (1.0 - sb))).astype(BF16)

    nxt = pl.BlockSpec((HALO, CONV_CH), lambda i: (jnp.minimum((i + 1) * hpt, last_blk), 0))
    glu_blk = p_glu // (2 * CONV_CH)
    return pl.pallas_call(
        body, name="bwd_conv", grid=(t // tm,),
        out_shape=(_sds(dz.shape, BF16), _sds(cw.shape, F32)),
        in_specs=[ANY, _row(tm, CONV_CH), nxt, _row(tm, CONV_CH), _row(tm, 2 * CONV_CH), _full(cwc.shape)],
        out_specs=(pl.BlockSpec((tm, 2 * CONV_CH), lambda i: (i, glu_blk)), _full(cw.shape)),
        scratch_shapes=[pltpu.VMEM(_shifted_shape(tm), F32), pltpu.VMEM((CONV_LC, tm, LANES), F32),
                        pltpu.VMEM((CONV_LC, HALO, SUBLANES, LANES), F32), pltpu.VMEM((CONV_LC, tm, LANES), F32)],
        input_output_aliases={0: 0},
        compiler_params=_params("arbitrary"),
    )(dz, du1, du1, u0, zglu, cwc)


def _attn_bwd(q, k, v, do, nseq, seq, scatter=()):
    t = q.shape[0]
    ns = len(scatter)
    blk = pl.BlockSpec((seq, LANES), lambda b, h: (b, h))
    n_steps = nseq * N_HEADS

    def body(q_ref, k_ref, v_ref, do_ref, *rest):
        dq_ref, dk_ref, dv_ref = rest[ns:ns + 3]
        dka_ref, dva_ref = rest[2 * ns + 3:2 * ns + 5]
        if ns:
            start, finish = _scatter_phases(rest[:ns], rest[ns + 3:2 * ns + 3], *rest[2 * ns + 5:])
            step = pl.program_id(0) * N_HEADS + pl.program_id(1)
            pl.when(step == 0)(start)
        dka_ref[...] = jnp.zeros_like(dka_ref)
        dva_ref[...] = jnp.zeros_like(dva_ref)
        mask = _diag_mask()
        nb = seq // BQ
        block = lambda j: (_scores(q_ref[j * BQ:(j + 1) * BQ, :], k_ref, j * BQ, (j + 1) * BQ),
                           _scores(do_ref[j * BQ:(j + 1) * BQ, :], v_ref, j * BQ, (j + 1) * BQ))
        ahead = [block(j) for j in range(min(AHEAD, nb))]

        def second_stage(lo, e, dsd, dsp, pdb, ppb):
            q_i = q_ref[lo:e, :]
            do_i = do_ref[lo:e, :]
            dq = _dot(dsd, k_ref[lo:e, :])
            dka_ref[lo:e, :] += _dot_tn(dsd, q_i)
            dva_ref[lo:e, :] += _dot_tn(pdb, do_i)
            if lo:
                dq = dq + _dot(dsp, k_ref[:lo, :])
                dka_ref[:lo, :] += _dot_tn(dsp, q_i)
                dva_ref[:lo, :] += _dot_tn(ppb, do_i)
            dq_ref[lo:e, :] = dq * SM_SCALE

        held = None
        for i in range(nb):
            lo, e = i * BQ, (i + 1) * BQ
            scores, (dpp, dpd) = ahead.pop(0)
            if i + AHEAD < nb:
                ahead.append(block(i + AHEAD))
            pp, pd, l = _softmax_parts(scores, mask)
            inv = 1.0 / l
            pd = pd * inv
            delta = jnp.sum(pd * dpd, axis=-1, keepdims=True)
            if lo:
                pp = pp * inv
                delta = delta + jnp.sum(pp * dpp, axis=-1, keepdims=True)
            dsd = (pd * (dpd - delta)).astype(BF16)
            dsp = (pp * (dpp - delta)).astype(BF16) if lo else None
            if held is not None:
                second_stage(*held)
            held = (lo, e, dsd, dsp, pd.astype(BF16), pp.astype(BF16) if lo else None)
        second_stage(*held)
        dk_ref[...] = dka_ref[...] * SM_SCALE
        dv_ref[...] = dva_ref[...].astype(BF16)
        if ns:
            pl.when(step == n_steps - 1)(finish)

    res = pl.pallas_call(
        body, name="attn_bwd", grid=(nseq, N_HEADS),
        out_shape=(_sds((t, HW), F32), _sds((t, HW), F32), _sds((t, HW), BF16)) + _scatter_shapes(scatter),
        in_specs=[blk] * 4 + [ANY] * ns, out_specs=(blk,) * 3 + (ANY,) * ns,
        scratch_shapes=[pltpu.VMEM((seq, LANES), F32), pltpu.VMEM((seq, LANES), F32)]
        + (_scatter_sems(ns) if ns else []),
        compiler_params=_params("arbitrary", "arbitrary"),
    )(q, k, v, do, *scatter)
    return res[0], res[1], res[2], res[3:]


def _mla_bwd(dz, dq, dk, dv, zm, gql, gkvl, gq, gk, tabs, wuq_p, wk_p, wv_p, tm, tps):
    t = zm.shape[0]
    d = (dz.shape[1] - MLA_IN - 2 * CONV_CH) // 2
    _, p_q, _ = _layout(d)
    c_t, s1_t, s2_t = tabs
    tab = pl.BlockSpec((tm, LANES), lambda i: (i % tps, 0))

    def body(dz_hbm, dq_ref, dk_ref, dv_ref, zm_ref, gql_ref, gkvl_ref, gq_ref, gk_ref, c_ref, s1_ref, s2_ref,
             wuq_ref, wk_ref, wv_ref,
             dzm_ref, dqpre_ref, dkh_ref, dgq_ref, dgk_ref, dgql_ref, dgkvl_ref):
        i = pl.program_id(0)
        c, s1, s2 = c_ref[...], s1_ref[...], s2_ref[...]
        nq, rq = _rms(zm_ref[:, :Q_RANK])
        qpre = _dot((nq * gql_ref[...]).astype(BF16), wuq_ref[...])
        nkv, rkv = _rms(zm_ref[:, Q_RANK:OFF_KV])
        knope = _dot((nkv * gkvl_ref[...]).astype(BF16), wk_ref[...])
        zkr_v = zm_ref[:, OFF_KV:]
        gk = gk_ref[...]
        kr_roped = _rope(zkr_v * gk, c, s1, s2)
        dgq = jnp.zeros((1, LANES), F32)
        dgk = jnp.zeros((1, LANES), F32)
        dzkr = jnp.zeros((tm, LANES), F32)
        dt_sum = jnp.zeros((tm, LANES), F32)
        slabs = [slice(hd * LANES, (hd + 1) * LANES) for hd in range(N_HEADS)]
        gq = gq_ref[...]
        rqh = [_head_rms(qpre[:, sl])[1] for sl in slabs]
        rkh = [_head_rms(knope[:, sl] + zkr_v)[1] for sl in slabs]
        dyr = [_rope_t(dq_ref[:, sl], c, s1, s2) for sl in slabs]
        nqh = [qpre[:, sl] * rqh[hd] for hd, sl in enumerate(slabs)]
        sq = [jnp.sum((dyr[hd] * gq) * nqh[hd], axis=-1, keepdims=True) for hd in range(N_HEADS)]
        dr = [jnp.sum(dk_ref[:, sl] * (knope[:, sl] * gk + kr_roped), axis=-1, keepdims=True) for sl in slabs]
        for hd, sl in enumerate(slabs):
            dgq = dgq + jnp.sum(dyr[hd] * nqh[hd], axis=0, keepdims=True)
            dqpre_ref[:, sl] = (rqh[hd] * (dyr[hd] * gq - nqh[hd] * (sq[hd] * (1.0 / QK_HEAD)))).astype(BF16)
            kn = knope[:, sl]
            r = rkh[hd]
            dt = dk_ref[:, sl] * r
            via_r = (dr[hd] * (r * r * r) * (-1.0 / QK_HEAD)) * (kn + zkr_v)
            dgk = dgk + jnp.sum(dt * kn, axis=0, keepdims=True)
            dt_sum = dt_sum + dt
            dzkr = dzkr + via_r
            dkh_ref[:, sl] = (dt * gk + via_r).astype(BF16)
        de = _rope_t(dt_sum, c, s1, s2)
        dzkr = dzkr + de * gk
        dgk = dgk + jnp.sum(de * zkr_v, axis=0, keepdims=True)
        _acc(dgq_ref, dgq[:, :QK_HEAD], i == 0)
        _acc(dgk_ref, dgk[:, :QK_HEAD], i == 0)
        dzm_ref[:, OFF_KV:] = dzkr.astype(BF16)
        dqln = _dot_nt(dqpre_ref[...], wuq_ref[...])
        _acc(dgql_ref, jnp.sum(dqln * nq, axis=0, keepdims=True), i == 0)
        dzm_ref[:, :Q_RANK] = _rms_bwd(nq, rq, dqln * gql_ref[...]).astype(BF16)
        dkvn = _dot_nt(dkh_ref[...], wk_ref[...]) + _dot_nt(dv_ref[...], wv_ref[...])
        _acc(dgkvl_ref, jnp.sum(dkvn * nkv, axis=0, keepdims=True), i == 0)
        dzm_ref[:, Q_RANK:OFF_KV] = _rms_bwd(nkv, rkv, dkvn * gkvl_ref[...]).astype(BF16)

    return pl.pallas_call(
        body, name="mla_bwd", grid=(t // tm,),
        out_shape=(_sds(dz.shape, BF16), _sds((t, HW), BF16), _sds((t, HW), BF16), _sds((1, QK_HEAD), F32),
                   _sds((1, QK_HEAD), F32), _sds((1, Q_RANK), F32), _sds((1, KV_RANK), F32)),
        in_specs=[ANY, _row(tm, HW), _row(tm, HW), _row(tm, HW), _row(tm, MLA_IN),
                  _full((1, Q_RANK)), _full((1, KV_RANK)), _full((1, LANES)), _full((1, LANES)), tab, tab, tab,
                  _full(wuq_p.shape), _full(wk_p.shape), _full(wv_p.shape)],
        out_specs=(pl.BlockSpec((tm, MLA_IN), lambda i: (i, p_q // MLA_IN)), _row(tm, HW), _row(tm, HW),
                   _full((1, QK_HEAD)), _full((1, QK_HEAD)), _full((1, Q_RANK)), _full((1, KV_RANK))),
        input_output_aliases={0: 0},
        compiler_params=_params("arbitrary"),
    )(dz, dq, dk, dv, zm, gql, gkvl, gq, gk, c_t, s1_t, s2_t, wuq_p, wk_p, wv_p)


def _bwd_in(dz, x, dx1, g1, mod3, win_p, tm, tps, scatter=()):
    t, d = x.shape
    npad = dz.shape[1]

    ns = len(scatter)
    n_steps = t // tm

    def body(dz_ref, x_ref, dx1_ref, g_ref, mod_ref, wt_hbm, *rest):
        gx_ref, dshift_ref, dscale_ref, dg1_ref = rest[ns:ns + 4]
        wt_ref = rest[2 * ns + 4]
        i = pl.program_id(0)
        if ns:
            start, finish = _scatter_phases(rest[:ns], rest[ns + 4:2 * ns + 4], *rest[2 * ns + 5:])
            pl.when(i == 0)(start)
        _load_resident(i, [(wt_hbm, wt_ref)])
        first_seq = (i % tps) == 0
        g = g_ref[...]
        sc1 = 1.0 + mod_ref[1:2, :]
        nb = max(tm // ROW_BAND, 1)
        bands = [slice(b * (tm // nb), (b + 1) * (tm // nb)) for b in range(nb)]
        dhs = [_dot_nt(dz_ref[rows, :], wt_ref[...]) for rows in bands]
        sums = [jnp.zeros((1, d), F32)] * 3
        col = lambda v: jnp.sum(v, axis=0, keepdims=True)
        for rows, dh in zip(bands, dhs):
            n, r = _rms(x_ref[rows, :])
            sums = [sums[0] + col(dh), sums[1] + col(dh * (n * g)), sums[2] + col((dh * sc1) * n)]
            gx_ref[rows, :] = dx1_ref[rows, :] + _rms_bwd(n, r, (dh * sc1) * g)
        _acc(dshift_ref, sums[0], first_seq)
        _acc(dscale_ref, sums[1], first_seq)
        _acc(dg1_ref, sums[2], i == 0)
        if ns:
            pl.when(i == n_steps - 1)(finish)

    nseq = t // (tm * tps)
    sv = _sds((nseq, 1, d), F32)
    res = pl.pallas_call(
        body, name="bwd_in", grid=(n_steps,),
        out_shape=(_sds((t, d), F32), sv, sv, _sds((1, d), F32)) + _scatter_shapes(scatter),
        in_specs=[_row(tm, npad), _row(tm, d), _row(tm, d), _full((1, d)), _modspec(d, tps), ANY] + [ANY] * ns,
        out_specs=(_row(tm, d), _seqv(d, tps), _seqv(d, tps), _full((1, d))) + (ANY,) * ns,
        scratch_shapes=[pltpu.VMEM(win_p.shape, BF16)] + (_scatter_sems(ns) if ns else []),
        compiler_params=_params("arbitrary"),
    )(dz, x, dx1, g1, mod3, win_p, *scatter)
    return res[0], res[1], res[2], res[3], res[4:]


def _tile_of(n, choices):
    for c in choices:
        if n % c == 0:
            return c
    return n


def _tn_matmul(a, b, name, col_shards=0):
    t, k = a.shape
    n = b.shape[1]
    tk = _tile_of(k, (1024, 512, 256, 128))
    tn = n // col_shards if col_shards else _tile_of(n, (1024, 896, 768, 512, 384, 256, 128))
    tt = _tile_of(t, (4096, 2048, 1024, 512, 256))

    def body(a_ref, b_ref, o_ref):
        _acc(o_ref, _dot_tn(a_ref[...], b_ref[...]), pl.program_id(2) == 0)

    if col_shards:
        out_shape, out_spec = _sds((col_shards, k, tn), F32), pl.BlockSpec((None, tk, tn), lambda i, j, s: (j, i, 0))
    else:
        out_shape, out_spec = _sds((k, n), F32), pl.BlockSpec((tk, tn), lambda i, j, s: (i, j))
    return pl.pallas_call(
        body, name=name, grid=(k // tk, n // tn, t // tt), out_shape=out_shape,
        in_specs=[pl.BlockSpec((tt, tk), lambda i, j, s: (s, i)), pl.BlockSpec((tt, tn), lambda i, j, s: (s, j))],
        out_specs=out_spec, compiler_params=_params("arbitrary", "arbitrary", "arbitrary"),
    )(a, b)


N_SHARD = 4
COL_SHARDED = ("w_in", "w_uq", "w_ukv", "w_o_mla", "w_pw_out", "w_ff1")
ROW_SHARDED = ("w_out", "w_ff2")
BIG = ("w_in", "w_uq", "w_ukv", "w_o_mla", "w_pw_out", "w_out", "w_ff1", "w_ff2")
SMALL = ("norm1_g", "q_latent_g", "kv_latent_g", "qk_norm_q_g", "qk_norm_k_g", "conv_b", "conv_ln_g", "conv_ln_b",
         "norm2_g")
WEIGHTS = ("w_ada", "b_ada", "norm1_g", "w_in", "q_latent_g", "w_uq", "kv_latent_g", "w_ukv", "qk_norm_q_g",
           "qk_norm_k_g", "w_o_mla", "conv_w", "conv_b", "conv_ln_g", "conv_ln_b", "w_pw_out", "w_out", "norm2_g",
           "w_ff1", "w_ff2")


def _pad_heads(w, width):
    k = w.shape[0]
    w3 = w.reshape(k, N_HEADS, width)
    return jnp.pad(w3, ((0, 0), (0, 0), (0, LANES - width))).reshape(k, HW)


def _unpad_heads(g, width):
    k = g.shape[0]
    return g.reshape(k, N_HEADS, LANES)[:, :, :width].reshape(k, N_HEADS * width)


def _win_segments(d):
    return [(OFF_GLU, OFF_GLU + 2 * d), (OFF_KR, OFF_GLU), (0, OFF_KV), KR_LANE, (OFF_KV, OFF_KR),
            LANES - KR_LANE - QK_ROPE]


def _pad_win(g4):
    _, d, ws = g4.shape
    parts = []
    for seg in _win_segments(d):
        if isinstance(seg, int):
            parts.append(jnp.zeros((d, seg), g4.dtype))
            continue
        a, b = seg
        while a < b:
            s = a // ws
            e = min(b, (s + 1) * ws)
            parts.append(g4[s, :, a - s * ws:e - s * ws])
            a = e
    return jnp.concatenate(parts, axis=1)


def _unpad_win(gp):
    d = gp.shape[0]
    ws = (OFF_GLU + 2 * d) // N_SHARD
    pieces, p = [], 0
    for seg in _win_segments(d):
        if isinstance(seg, int):
            p += seg
        else:
            pieces.append((seg[0], seg[1], p))
            p += seg[1] - seg[0]
    shards = []
    for s in range(N_SHARD):
        lo, hi = s * ws, (s + 1) * ws
        cols = [gp[:, p0 + max(a, lo) - a:p0 + min(b, hi) - a] for a, b, p0 in sorted(pieces) if max(a, lo) < min(b, hi)]
        shards.append(jnp.concatenate(cols, axis=1))
    return jnp.stack(shards)


def _col_shards(g):
    k, n = g.shape
    return g.reshape(k, N_SHARD, n // N_SHARD).transpose(1, 0, 2)


def _from_shards(g, name):
    ns, ks, nn = g.shape
    if name in ROW_SHARDED:
        return g.reshape(ns * ks, nn)
    return g.transpose(1, 0, 2).reshape(ks, ns * nn)


BY_SHARD = ("w_in", "w_ff1")
EARLY = ("w_in", "w_uq", "w_ukv")
LATE = ("w_o_mla", "w_pw_out", "w_out", "w_ff1", "w_ff2")


def _assemble(names, gathered):
    by_shard = {n: g.reshape((N_SHARD, 2 * g.shape[1]) + g.shape[2:]) for n, g in zip(names, gathered)}
    return {n: g if n in BY_SHARD else _from_shards(g, n) for n, g in by_shard.items()}


LARGE = ("w_in", "w_ff1", "w_ff2")
GROUP_A = ("w_out", "w_ff1", "w_ff2")
GROUP_B = ("w_in", "w_uq", "w_ukv", "w_o_mla", "w_pw_out")


def _pair_halves(g):
    return g.reshape(N_SHARD, 2, g.shape[1] // 2, g.shape[2])


def _pair_sums(names, halves, from_sibling):
    if not halves:
        return []
    cidx = lax.axis_index("c").reshape(1).astype(jnp.int32)
    out = {n: _add_pair(g, l, cidx, "pair_sum_" + n)
           for n, g, l in zip(names, halves, from_sibling) if n in LARGE}
    small = [j for j, n in enumerate(names) if n not in LARGE]
    if small:
        res = _add_pair_whole([halves[j] for j in small], [from_sibling[j] for j in small], cidx,
                              "pair_sum_small_" + names[small[0]])
        out.update({names[j]: r for j, r in zip(small, res)})
    return [out[n] for n in names]


def _local_step(x, target, mod, sp, w, late=None, tm=256):
    comm = late is not None
    w = dict(w)
    nseq, seq, d = x.shape
    t = nseq * seq
    tps = seq // tm
    xf = x.reshape(t, d)
    tg = target.reshape(t, d)
    mod3 = mod.reshape(nseq, N_MOD, d)

    win_p = _pad_win(w["w_in"])
    wuq_p = _pad_heads(w["w_uq"], QK_HEAD)
    wkv3 = w["w_ukv"].reshape(KV_RANK, N_HEADS, QK_NOPE + V_HEAD)
    wk_p = _pad_heads(wkv3[:, :, :QK_NOPE].reshape(KV_RANK, -1), QK_NOPE)
    wv_p = _pad_heads(wkv3[:, :, QK_NOPE:].reshape(KV_RANK, -1), V_HEAD)
    cw = jnp.pad(w["conv_w"], ((0, HALO - CONV_W), (0, 0)))
    pad_g = lambda g: jnp.pad(g, ((0, 0), (0, LANES - QK_HEAD)))
    gq, gk = pad_g(sp["qk_norm_q_g"]), pad_g(sp["qk_norm_k_g"])
    tabs = _rope_tables(seq)

    tm_in, tps_in = (2 * tm, tps // 2) if tps % 2 == 0 else (tm, tps)
    h, zm, zglu, zgate, u0 = _fwd_in(xf, sp["norm1_g"], mod3, win_p, tm_in, tps_in)
    q, k, v, qln, kvn = _mla_prep(zm, sp["q_latent_g"], sp["kv_latent_g"], gq, gk, tabs, wuq_p, wk_p, wv_p, tm_in,
                                  tps_in)
    attn, gathered = _attn_fwd(q, k, v, nseq, seq, tuple(late) if comm else ())
    if comm:
        w.update(_assemble(LATE, gathered))
    wo_p = jnp.pad(w["w_o_mla"].reshape(N_HEADS, V_HEAD, d), ((0, 0), (0, LANES - V_HEAD), (0, 0))).reshape(HW, d)
    x1, mixed, mpre, ya, yb, u1, u3 = _fwd_mix(attn, u0, zgate, xf, mod3, wo_p, cw, sp["conv_b"], sp["conv_ln_g"],
                                               sp["conv_ln_b"], w["w_pw_out"], w["w_out"], tm, tps)
    h2, a, r, dy, df, dgate2, loss_acc = _fwd_ffn(x1, tg, sp["norm2_g"], mod3, w["w_ff1"], w["w_ff2"], tm, tps)
    da, dx1, dmixed, dshift2, dscale2, dgate1, dg2 = _bwd_ffn(df, a, x1, dy, mixed, sp["norm2_g"], mod3,
                                                              w["w_ff2"], w["w_ff1"], tm, tps)
    gw = {
        "w_out": _tn_matmul(mpre, dmixed, "dw_out").reshape(N_SHARD, d // N_SHARD, d),
        "w_ff1": _tn_matmul(h2, da, "dw_ff1", N_SHARD),
        "w_ff2": _tn_matmul(r, df, "dw_ff2").reshape(N_SHARD, -1, d),
    }
    halves_a = [_pair_halves(gw[n]) for n in GROUP_A] if comm else []
    dya, dyb, dz, do, du1, dlng, dlnb, dcb, from_sibling = _bwd_mix(
        dmixed, zgate, ya, yb, u1, sp["conv_ln_g"], sp["conv_ln_b"], w["w_out"], wo_p, w["w_pw_out"], tm_in, tuple(halves_a))
    pair_a = _pair_sums(GROUP_A, halves_a, from_sibling)
    dz, dcw = _bwd_conv(dz, du1, u0, zglu, cw, tm_in, tps_in)
    gw["conv_w"] = dcw
    dq, dk, dv, land_a = _attn_bwd(q, k, v, do, nseq, seq, tuple(p[1] for p in pair_a))
    dz, dqpre, dkh, dgq, dgk, dgql, dgkvl = _mla_bwd(dz, dq, dk, dv, zm, sp["q_latent_g"], sp["kv_latent_g"], gq, gk,
                                                      tabs, wuq_p, wk_p, wv_p, tm_in, tps_in)
    dwk_p = _tn_matmul(kvn, dkh, "dw_uk")
    dwv_p = _tn_matmul(kvn, dv, "dw_uv")
    dwkv = jnp.concatenate([dwk_p.reshape(KV_RANK, N_HEADS, LANES)[:, :, :QK_NOPE],
                            dwv_p.reshape(KV_RANK, N_HEADS, LANES)[:, :, :V_HEAD]], axis=2).reshape(KV_RANK, -1)
    dwo = _tn_matmul(attn, dya, "dw_o").reshape(N_HEADS, LANES, d)[:, :V_HEAD].reshape(MLA_WIDTH, d)
    gw["w_in"] = _unpad_win(_tn_matmul(h, dz, "dw_in"))
    gw["w_uq"] = _col_shards(_unpad_heads(_tn_matmul(qln, dqpre, "dw_uq"), QK_HEAD))
    gw["w_ukv"] = _col_shards(dwkv)
    gw["w_o_mla"] = _col_shards(dwo)
    gw["w_pw_out"] = _tn_matmul(u3, dyb, "dw_pw", N_SHARD)
    pair_b = []
    if comm:
        halves_b = [_pair_halves(gw[n]) for n in GROUP_B]
        pair_b = _pair_sums(GROUP_B, halves_b, _pair_swap(halves_b, "grad_pair_swap"))
    gx, dshift1, dscale1, dg1, land_b = _bwd_in(dz, xf, dx1, sp["norm1_g"], mod3, win_p, tm_in, tps_in,
                                                tuple(p[1] for p in pair_b))
    if comm:
        for n, p, l in zip(GROUP_A + GROUP_B, pair_a + pair_b, land_a + land_b):
            gw[n] = (p[0], l)
    gs = {
        "norm1_g": dg1, "q_latent_g": dgql, "kv_latent_g": dgkvl, "qk_norm_q_g": dgq, "qk_norm_k_g": dgk,
        "conv_b": dcb, "conv_ln_g": dlng, "conv_ln_b": dlnb, "norm2_g": dg2,
    }
    dmod = jnp.concatenate([dshift1, dscale1, dgate1, dshift2, dscale2, dgate2], axis=2).reshape(nseq, N_MOD * d)
    return loss_acc, gx.reshape(nseq, seq, d), dmod, gw, gs


def kernel(x, c, w_ada, b_ada, norm1_g, w_in, q_latent_g, w_uq, kv_latent_g, w_ukv, qk_norm_q_g, qk_norm_k_g, w_o_mla, conv_w, conv_b, conv_ln_g, conv_ln_b, w_pw_out, w_out, norm2_g, w_ff1, w_ff2, loss_target, m_w_ada, m_b_ada, m_norm1_g, m_w_in, m_q_latent_g, m_w_uq, m_kv_latent_g, m_w_ukv, m_qk_norm_q_g, m_qk_norm_k_g, m_w_o_mla, m_conv_w, m_conv_b, m_conv_ln_g, m_conv_ln_b, m_w_pw_out, m_w_out, m_norm2_g, m_w_ff1, m_w_ff2, v_w_ada, v_b_ada, v_norm1_g, v_w_in, v_q_latent_g, v_w_uq, v_kv_latent_g, v_w_ukv, v_qk_norm_q_g, v_qk_norm_k_g, v_w_o_mla, v_conv_w, v_conv_b, v_conv_ln_g, v_conv_ln_b, v_w_pw_out, v_w_out, v_norm2_g, v_w_ff1, v_w_ff2):
    given = dict(locals())
    wts = {n: given[n][0] for n in WEIGHTS}
    mom = {n: given["m_" + n][0] for n in WEIGHTS}
    var = {n: given["v_" + n][0] for n in WEIGHTS}
    vec = lambda a: a.reshape(1, -1)
    nseq, seq, d = x.shape
    ix, iy, ic = _place()
    shard = 2 * ix + iy

    half = lambda n: lax.dynamic_slice_in_dim(wts[n].astype(BF16), ic * (wts[n].shape[0] // 2), wts[n].shape[0] // 2,
                                              axis=0)
    gathered = _all_gather8([half(n) for n in EARLY] + [wts["conv_w"], c], "gather_weights")
    full = _assemble(EARLY, gathered)
    full["conv_w"] = _from_shards(gathered[-2][0::2], "conv_w")
    c_all = gathered[-1].reshape(8 * nseq, d)

    n_ada = wts["w_ada"].shape[1]
    b_sh = lax.dynamic_slice_in_dim(vec(wts["b_ada"]), shard * n_ada, n_ada, axis=1)
    mod_sh = _ada_mod(c_all, wts["w_ada"], b_sh)
    hb = 4 * nseq
    mod_blk = lax.dynamic_slice_in_dim(mod_sh, ic * hb, hb, axis=0)
    (mod_all,) = _all_gather8([mod_blk], "gather_mod")
    mod_mine = lax.dynamic_slice_in_dim(mod_all, (2 * iy + ic) * nseq, nseq, axis=1)
    mod = jnp.concatenate([lax.dynamic_index_in_dim(mod_mine, 2 * s + ix, axis=0, keepdims=False)
                           for s in range(N_SHARD)], axis=1)

    sp = {n: vec(wts[n]) for n in SMALL}
    loss_part, grad_x, dmod, gw, gs = _local_step(x, loss_target, mod, sp, full, [half(n) for n in LATE])

    own_c = jnp.stack([shard, ic]).astype(jnp.int32)
    mine_sum = {n: _add_chips(gw[n][0], gw[n][1], own_c, "chip_sum_" + n) for n in LARGE}
    few = tuple(n for n in BIG if n not in LARGE)
    mine_sum.update(zip(few, _add_chips_whole([gw[n][0] for n in few], [gw[n][1] for n in few], own_c, "chip_sum_small")))
    summed, parts = _pair_gather_and_all_gather8(
        [mine_sum[n] for n in BIG], [dmod, gw["conv_w"], loss_part] + [gs[n] for n in SMALL], "tail_exchange")

    dmod_all = parts[0].reshape(8 * nseq, N_MOD * d)
    dmod_sh = lax.dynamic_slice_in_dim(dmod_all, shard * n_ada, n_ada, axis=1)
    res = _ada_bwd(c_all, dmod_all, dmod_sh, parts[1:])
    grads = {"w_ada": res[0], "b_ada": res[1]}
    n_cw = wts["conv_w"].shape[1]
    grads["conv_w"] = lax.dynamic_slice_in_dim(res[2], shard * n_cw, n_cw, axis=1)[:CONV_W]
    loss = res[3][0, 0]
    for n, g in zip(SMALL, res[4:]):
        grads[n] = g
    for n, g in zip(BIG, summed):
        grads[n] = g.reshape(wts[n].shape)

    delta, new_m, new_v = {}, {}, {}
    for n in LARGE + ("w_ada",):
        if n == "w_in":
            res = _adamw(wts[n].T, grads[n].T, mom[n].T, var[n].T, "adamw_" + n)
            delta[n], new_m[n], new_v[n] = (a.T for a in res)
        else:
            delta[n], new_m[n], new_v[n] = _adamw(wts[n], grads[n], mom[n], var[n], "adamw_" + n)
    rest = ("b_ada", "conv_w") + SMALL + few
    as2d = lambda a: a if a.ndim == 2 else vec(a)
    res = _adamw_small(*[[as2d(t[n]) for n in rest] for t in (wts, grads, mom, var)])
    for dst, arrs in zip((delta, new_m, new_v), res):
        for n, a in zip(rest, arrs):
            dst[n] = a

    outs = [loss, grad_x]
    for group in (grads, delta, new_m, new_v):
        outs += [group[n].reshape(given[n].shape) for n in WEIGHTS]
    return tuple(outs)
```

```python
import jax
import jax.numpy as jnp
from jax import lax
from jax.experimental import pallas as pl
from jax.experimental.pallas import tpu as pltpu

F32 = jnp.float32
BF16 = jnp.bfloat16
MESH = pl.DeviceIdType.MESH
ANY = pl.BlockSpec(memory_space=pl.ANY)

CHUNK = 64
CHUNK_SHIFT = 6
N_HEADS = 8
QK_NOPE = 64
QK_ROPE = 32
QK_HEAD = QK_NOPE + QK_ROPE
V_HEAD = 64
Q_RANK = 256
KV_RANK = 128
MLA_WIDTH = N_HEADS * V_HEAD
CONV_CH = 512
CONV_W = 31
ROPE_THETA = 10000.0
EPS = 1e-6
LANES = 128
SUBLANES = 8
HW = N_HEADS * LANES
OFF_KV = Q_RANK + KV_RANK
OFF_KR = OFF_KV + QK_ROPE
OFF_GLU = OFF_KR + 2 * CONV_CH
KR_LANE = QK_NOPE
MLA_IN = Q_RANK + KV_RANK + LANES
HALO = 32
N_MOD = 6

ADAM_LR = 0.001
ADAM_B1 = 0.9
ADAM_B2 = 0.999
ADAM_EPS = 1e-08
ADAM_WD = 0.01
ADAM_STEP = 10

VMEM_LIMIT = 56 * 1024 * 1024
BQ = 256


def _layout(d):
    p_glu = 2 * d
    p_q = p_glu + 2 * CONV_CH
    return p_glu, p_q, p_q + MLA_IN


def _params(*sem):
    return pltpu.CompilerParams(dimension_semantics=sem, vmem_limit_bytes=VMEM_LIMIT)


def _dot(a, b):
    return jnp.dot(a, b, preferred_element_type=F32)


def _dot_tn(a, b):
    return lax.dot_general(a, b, (((0,), (0,)), ((), ())), preferred_element_type=F32)


def _dot_nt(a, b):
    return lax.dot_general(a, b, (((1,), (1,)), ((), ())), preferred_element_type=F32)


def _acc(ref, val, first):
    @pl.when(first)
    def _():
        ref[...] = val

    @pl.when(jnp.logical_not(first))
    def _():
        ref[...] += val


def _rms(x):
    r = lax.rsqrt(jnp.mean(x * x, axis=-1, keepdims=True) + EPS)
    return x * r, r


def _rms_bwd(n, r, dn):
    return r * (dn - n * jnp.mean(dn * n, axis=-1, keepdims=True))


def _head_rms(sl):
    r = lax.rsqrt(jnp.sum(sl * sl, axis=-1, keepdims=True) * (1.0 / QK_HEAD) + EPS)
    return sl * r, r


def _head_rms_bwd(n, r, dn):
    return r * (dn - n * (jnp.sum(dn * n, axis=-1, keepdims=True) * (1.0 / QK_HEAD)))


def _rope(x, c, s1, s2):
    return x * c + pltpu.roll(x, QK_ROPE // 2, 1) * s1 + pltpu.roll(x, LANES - QK_ROPE // 2, 1) * s2


def _rope_t(dy, c, s1, s2):
    return dy * c + pltpu.roll(dy * s1, LANES - QK_ROPE // 2, 1) + pltpu.roll(dy * s2, QK_ROPE // 2, 1)


def _rope_tables(seq):
    half = QK_ROPE // 2
    inv_freq = ROPE_THETA ** (-jnp.arange(0, QK_ROPE, 2, dtype=F32) / QK_ROPE)
    ang = jnp.arange(seq, dtype=F32)[:, None] * inv_freq[None, :]
    cos, sin = jnp.cos(ang), jnp.sin(ang)
    z = lambda n: jnp.zeros((seq, n), F32)
    tail = LANES - QK_HEAD
    c = jnp.concatenate([jnp.ones((seq, QK_NOPE), F32), cos, cos, jnp.ones((seq, tail), F32)], axis=1)
    s1 = jnp.concatenate([z(QK_NOPE + half), sin, z(tail)], axis=1)
    s2 = jnp.concatenate([z(QK_NOPE), -sin, z(half + tail)], axis=1)
    return c, s1, s2


def _row(tm, w):
    return pl.BlockSpec((tm, w), lambda i: (i, 0))


def _modspec(d, tps):
    return pl.BlockSpec((None, N_MOD, d), lambda i: (i // tps, 0, 0))


def _seqv(w, tps):
    return pl.BlockSpec((None, 1, w), lambda i: (i // tps, 0, 0))


def _full(shape):
    return pl.BlockSpec(shape, lambda i: tuple(0 for _ in shape))


def _sds(shape, dtype):
    return jax.ShapeDtypeStruct(shape, dtype)


CONV_ROWS = 64
CONV_LC = CONV_CH // LANES


def _lane_chunks():
    return [(lc, slice(lc * LANES, (lc + 1) * LANES)) for lc in range(CONV_LC)]


def _fill_shifted(ext_ref, head, body):
    nh = head.shape[0]
    for lc, ls in _lane_chunks():
        ext_ref[0, lc, :nh, :] = head[:, ls]
        ext_ref[0, lc, nh:, :] = body[:, ls]
        rows = ext_ref[0, lc]
        for b in range(1, SUBLANES):
            ext_ref[b, lc] = pltpu.roll(rows, rows.shape[0] - b, 0)


def _shifted_shape(tm):
    return (SUBLANES, CONV_LC, tm + HALO, LANES)


def _conv_chunk(c):
    return c % CONV_LC, pl.multiple_of((c // CONV_LC) * CONV_ROWS, CONV_ROWS)


def _shifted(ext_ref, o, lc, r0):
    a = pl.multiple_of((o // SUBLANES) * SUBLANES + r0, SUBLANES)
    return ext_ref[o % SUBLANES, lc, pl.ds(a, CONV_ROWS), :]


def _by_lane_chunk(a):
    return a.reshape(a.shape[0], CONV_LC, LANES).transpose(1, 0, 2)


def _load_resident(i, pairs):
    @pl.when(i == 0)
    def _():
        for src, dst in pairs:
            pltpu.sync_copy(src, dst)


def _place():
    return lax.axis_index("x"), lax.axis_index("y"), lax.axis_index("c")


def _all_gather8(blocks, name):
    na = len(blocks)

    def body(*refs):
        start, forward, finish = _gather8_phases(refs[:na], refs[na:2 * na], *refs[2 * na:])
        start()
        forward()
        finish()

    outs = pl.pallas_call(
        body, name=name, out_shape=_gather8_shapes(blocks), in_specs=[ANY] * na, out_specs=(ANY,) * na,
        scratch_shapes=_gather8_sems(na),
    )(*blocks)
    return _own_block_placed(outs, blocks)


def _gather8_shapes(blocks):
    return tuple(_sds((8,) + b.shape, b.dtype) for b in blocks)


def _gather8_sems(na):
    return [pltpu.SemaphoreType.DMA((7 * na,)), pltpu.SemaphoreType.DMA((7 * na,))]


def _own_block_placed(outs, blocks):
    ix, iy, ic = _place()
    return tuple(lax.dynamic_update_index_in_dim(o, b, 4 * ix + 2 * iy + ic, 0) for o, b in zip(outs, blocks))


def _gather8_phases(x_refs, out_refs, send_sems, recv_sems):
    na = len(x_refs)
    x, y, c = _place()
    me, sibling = (x, y, c), (x, y, 1 - c)
    chips = [(1 - x, y), (x, 1 - y), (1 - x, 1 - y)]

    def copy(a, k, blk, to, from_input=False):
        dst = out_refs[a].at[4 * blk[0] + 2 * blk[1] + blk[2]]
        return pltpu.make_async_remote_copy(
            src_ref=x_refs[a] if from_input else dst, dst_ref=dst,
            send_sem=send_sems.at[7 * a + k], recv_sem=recv_sems.at[7 * a + k], device_id=to, device_id_type=MESH)

    def first(a):
        return [copy(a, 0, me, sibling, True)] + [copy(a, 1 + j, me, (*chip, c), True) for j, chip in enumerate(chips)]

    def start():
        for a in range(na):
            for cp in first(a):
                cp.start()

    def forward():
        for j, chip in enumerate(chips):
            for a in range(na):
                copy(a, 1 + j, (*chip, c), me).wait_recv()
                copy(a, 4 + j, (*chip, c), sibling).start()

    def finish():
        for a in range(na):
            copy(a, 0, sibling, me).wait_recv()
            for j, chip in enumerate(chips):
                copy(a, 4 + j, (*chip, 1 - c), me).wait_recv()
        for a in range(na):
            for cp in first(a) + [copy(a, 4 + j, (*chip, c), sibling) for j, chip in enumerate(chips)]:
                cp.wait_send()

    return start, forward, finish


def _pair_swap(gs, name):
    na = len(gs)

    def body(*refs):
        start, finish = _swap_phases(refs[:na], refs[na:2 * na], *refs[2 * na:])
        start()
        finish()

    return pl.pallas_call(
        body, name=name, out_shape=_swap_shapes(gs), in_specs=[ANY] * na, out_specs=(ANY,) * na,
        scratch_shapes=_swap_sems(gs),
    )(*gs)


def _swap_shapes(gs):
    return tuple(_sds(g.shape[:1] + g.shape[2:], g.dtype) for g in gs)


def _swap_sems(gs):
    n = sum(g.shape[0] for g in gs)
    return [pltpu.SemaphoreType.DMA((n,)), pltpu.SemaphoreType.DMA((n,))]


def _swap_phases(g_refs, land_refs, send_sems, recv_sems):
    x, y, c = _place()

    def copies():
        cps, k = [], 0
        for g_ref, land_ref in zip(g_refs, land_refs):
            for s in range(g_ref.shape[0]):
                cps.append(pltpu.make_async_remote_copy(
                    src_ref=g_ref.at[s, 1 - c], dst_ref=land_ref.at[s], send_sem=send_sems.at[k],
                    recv_sem=recv_sems.at[k], device_id=(x, y, 1 - c), device_id_type=MESH))
                k += 1
        return cps

    def start():
        for cp in copies():
            cp.start()

    def finish():
        for cp in copies():
            cp.wait()

    return start, finish


def _scatter_shapes(hs):
    return tuple(_sds((3,) + h.shape[1:], h.dtype) for h in hs)


def _scatter_sems(na):
    return [pltpu.SemaphoreType.DMA((3 * na,)), pltpu.SemaphoreType.DMA((3 * na,))]


def _scatter_phases(h_refs, land_refs, send_sems, recv_sems):
    x, y, c = _place()
    chips = [(1 - x, y), (x, 1 - y), (1 - x, 1 - y)]

    def copies():
        return [pltpu.make_async_remote_copy(
            src_ref=h_refs[a].at[2 * tx + ty], dst_ref=land_refs[a].at[j], send_sem=send_sems.at[3 * a + j],
            recv_sem=recv_sems.at[3 * a + j], device_id=(tx, ty, c), device_id_type=MESH)
            for a in range(len(h_refs)) for j, (tx, ty) in enumerate(chips)]

    def start():
        for cp in copies():
            cp.start()

    def finish():
        for cp in copies():
            cp.wait()

    return start, finish


def _pair_gather_and_all_gather8(fs, blocks, name):
    nf, nb = len(fs), len(blocks)

    def body(*refs):
        f_refs = refs[nf + nb:2 * nf + nb]
        b_out = refs[2 * nf + nb:2 * nf + 2 * nb]
        send_sems, recv_sems, g_send, g_recv = refs[2 * nf + 2 * nb:]
        x, y, c = _place()
        start, forward, finish = _gather8_phases(refs[nf:nf + nb], b_out, g_send, g_recv)
        sends = [pltpu.make_async_remote_copy(
            src_ref=f_refs[a].at[c], dst_ref=f_refs[a].at[c], send_sem=send_sems.at[a], recv_sem=recv_sems.at[a],
            device_id=(x, y, 1 - c), device_id_type=MESH) for a in range(nf)]
        recvs = [pltpu.make_async_remote_copy(
            src_ref=f_refs[a].at[c], dst_ref=f_refs[a].at[1 - c], send_sem=send_sems.at[a],
            recv_sem=recv_sems.at[a], device_id=(x, y, 1 - c), device_id_type=MESH) for a in range(nf)]
        start()
        for cp in sends:
            cp.start()
        forward()
        finish()
        for cp in recvs:
            cp.wait_recv()
        for cp in sends:
            cp.wait_send()

    res = pl.pallas_call(
        body, name=name, out_shape=tuple(_sds(f.shape, f.dtype) for f in fs) + _gather8_shapes(blocks),
        in_specs=[ANY] * (nf + nb), out_specs=(ANY,) * (nf + nb), input_output_aliases={a: a for a in range(nf)},
        scratch_shapes=[pltpu.SemaphoreType.DMA((nf,)), pltpu.SemaphoreType.DMA((nf,))] + _gather8_sems(nb),
    )(*fs, *blocks)
    return res[:nf], _own_block_placed(res[nf:], blocks)


def _row_tile(r, n, itemsize=4, budget=1 << 21):
    if r * n * itemsize <= budget:
        return r
    best = None
    for tr in range(16, r, 16):
        if r % tr == 0 and tr * n * itemsize <= budget:
            best = tr
    assert best is not None, (r, n)
    return best


def _add_pair(g, land, cidx, name):
    ns, _, r, n = g.shape
    tr = _row_tile(r, n)

    def body(c_ref, a_ref, b_ref, o_ref, ob_ref):
        s = a_ref[...] + b_ref[...]
        o_ref[...] = s
        ob_ref[...] = s.astype(BF16)

    out = pl.BlockSpec((None, tr, n), lambda s, i, cr: (s, i, 0))
    return pl.pallas_call(
        body, name=name, out_shape=(_sds((ns, r, n), F32), _sds((ns, r, n), BF16)),
        grid_spec=pltpu.PrefetchScalarGridSpec(
            num_scalar_prefetch=1, grid=(ns, r // tr),
            in_specs=[pl.BlockSpec((None, None, tr, n), lambda s, i, cr: (s, cr[0], i, 0)), out],
            out_specs=(out, out)),
        compiler_params=_params("arbitrary", "arbitrary"),
    )(cidx, g, land)


def _add_pair_whole(gs, lands, cidx, name):
    k = len(gs)

    def body(c_ref, *refs):
        for a_ref, b_ref, o_ref, ob_ref in zip(refs[:k], refs[k:2 * k], refs[2 * k:3 * k], refs[3 * k:]):
            s = a_ref[...] + b_ref[...]
            o_ref[...] = s
            ob_ref[...] = s.astype(BF16)

    half = lambda g: pl.BlockSpec((g.shape[0], None) + g.shape[2:], lambda i, cr: (0, cr[0], 0, 0))
    whole = lambda g: pl.BlockSpec(g.shape[:1] + g.shape[2:], lambda i, cr: (0, 0, 0))
    shapes = lambda dt: tuple(_sds(g.shape[:1] + g.shape[2:], dt) for g in gs)
    res = pl.pallas_call(
        body, name=name, out_shape=shapes(F32) + shapes(BF16),
        grid_spec=pltpu.PrefetchScalarGridSpec(
            num_scalar_prefetch=1, grid=(1,),
            in_specs=[half(g) for g in gs] + [whole(g) for g in gs],
            out_specs=tuple(whole(g) for g in gs) * 2),
        compiler_params=_params("arbitrary"),
    )(cidx, *gs, *lands)
    return list(zip(res[:k], res[k:]))


def _add_chips_whole(hs, lands, own_c, name):
    k = len(hs)

    def body(o_idx, *refs):
        for h_ref, l_ref, o_ref in zip(refs[:k], refs[k:2 * k], refs[2 * k:]):
            o_ref[...] = ((h_ref[...] + l_ref[0].astype(F32)) + l_ref[1].astype(F32)) + l_ref[2].astype(F32)

    return pl.pallas_call(
        body, name=name, out_shape=tuple(_sds((2,) + h.shape[1:], F32) for h in hs),
        grid_spec=pltpu.PrefetchScalarGridSpec(
            num_scalar_prefetch=1, grid=(1,),
            in_specs=[pl.BlockSpec((None,) + h.shape[1:], lambda i, o: (o[0], 0, 0)) for h in hs]
            + [pl.BlockSpec(l.shape, lambda i, o: (0, 0, 0)) for l in lands],
            out_specs=tuple(pl.BlockSpec((None,) + h.shape[1:], lambda i, o: (o[1], 0, 0)) for h in hs)),
        compiler_params=_params("arbitrary"),
    )(own_c, *hs, *lands)


def _add_chips(h, land, own_c, name):
    _, r, n = h.shape
    tr = _row_tile(r, n)

    def body(o_idx, h_ref, l_ref, o_ref):
        o_ref[...] = ((h_ref[...] + l_ref[0].astype(F32)) + l_ref[1].astype(F32)) + l_ref[2].astype(F32)

    return pl.pallas_call(
        body, name=name, out_shape=_sds((2, r, n), F32),
        grid_spec=pltpu.PrefetchScalarGridSpec(
            num_scalar_prefetch=1, grid=(r // tr,),
            in_specs=[pl.BlockSpec((None, tr, n), lambda i, o: (o[0], i, 0)),
                      pl.BlockSpec((3, tr, n), lambda i, o: (0, i, 0))],
            out_specs=pl.BlockSpec((None, tr, n), lambda i, o: (o[1], i, 0))),
        compiler_params=_params("arbitrary"),
    )(own_c, h, land)


def _adam_math(w, g, m, v):
    nm = ADAM_B1 * m + (1.0 - ADAM_B1) * g
    nv = ADAM_B2 * v + (1.0 - ADAM_B2) * (g * g)
    m_hat = nm / (1.0 - ADAM_B1 ** ADAM_STEP)
    v_hat = nv / (1.0 - ADAM_B2 ** ADAM_STEP)
    return -ADAM_LR * (m_hat / (jnp.sqrt(v_hat) + ADAM_EPS) + ADAM_WD * w), nm, nv


def _adamw(w, g, m, v, name):
    r, n = w.shape

    def body(w_ref, g_ref, m_ref, v_ref, d_ref, nm_ref, nv_ref):
        d_ref[...], nm_ref[...], nv_ref[...] = _adam_math(w_ref[...], g_ref[...], m_ref[...], v_ref[...])

    if r % 16 == 0:
        tr = _row_tile(r, n)
        steps, spec = r // tr, pl.BlockSpec((tr, n), lambda i: (i, 0))
    else:
        tc = 4 * LANES
        steps, spec = n // tc, pl.BlockSpec((r, tc), lambda j: (0, j))
    return pl.pallas_call(
        body, name=name, out_shape=(_sds((r, n), F32),) * 3, grid=(steps,),
        in_specs=[spec] * 4, out_specs=(spec,) * 3, compiler_params=_params("arbitrary"),
    )(w, g, m, v)


def _adamw_small(ws, gs, ms, vs):
    k = len(ws)

    def body(*refs):
        ins, outs = refs[:4 * k], refs[4 * k:]
        for j in range(k):
            d, nm, nv = _adam_math(ins[j][...], ins[k + j][...], ins[2 * k + j][...], ins[3 * k + j][...])
            outs[j][...] = d
            outs[k + j][...] = nm
            outs[2 * k + j][...] = nv

    shapes = tuple(_sds(w.shape, F32) for w in ws)
    res = pl.pallas_call(body, name="adamw_small", out_shape=shapes * 3,
                         compiler_params=pltpu.CompilerParams(vmem_limit_bytes=VMEM_LIMIT))(*ws, *gs, *ms, *vs)
    return res[:k], res[k:2 * k], res[2 * k:]


def _ada_mod(c_all, w_sh, b_sh):
    b, _ = c_all.shape
    n = w_sh.shape[1]

    def body(c_ref, w_ref, b_ref, o_ref):
        cc = c_ref[...]
        ca = (cc * jax.nn.sigmoid(cc)).astype(BF16)
        o_ref[...] = _dot(ca, w_ref[...].astype(BF16)) + b_ref[...]

    return pl.pallas_call(body, name="ada_mod", out_shape=_sds((b, n), F32),
                          compiler_params=pltpu.CompilerParams(vmem_limit_bytes=VMEM_LIMIT))(c_all, w_sh, b_sh)


def _ada_bwd(c_all, dmod_all, dmod_sh, parts):
    b, d = c_all.shape
    n6 = dmod_all.shape[1]
    n = dmod_sh.shape[1]
    k = len(parts)

    def body(*refs):
        c_ref, da_ref, ds_ref = refs[:3]
        p_refs = refs[3:3 + k]
        dw_ref, db_ref = refs[3 + k:5 + k]
        s_refs = refs[5 + k:]
        cc = c_ref[...]
        ca = (cc * jax.nn.sigmoid(cc)).astype(BF16)
        dw_ref[...] = _dot_tn(ca, ds_ref[...].astype(BF16))
        db_ref[...] = jnp.sum(da_ref[...], axis=0, keepdims=True)
        for p_ref, s_ref in zip(p_refs, s_refs):
            tot = p_ref[0]
            for j in range(1, p_ref.shape[0]):
                tot = tot + p_ref[j]
            s_ref[...] = tot

    return pl.pallas_call(
        body, name="ada_bwd",
        out_shape=(_sds((d, n), F32), _sds((1, n6), F32)) + tuple(_sds(p.shape[1:], F32) for p in parts),
        compiler_params=pltpu.CompilerParams(vmem_limit_bytes=VMEM_LIMIT),
    )(c_all, dmod_all, dmod_sh, *parts)


def _fwd_in(x, g1, mod3, win_p, tm, tps):
    t, d = x.shape
    p_glu, p_q, npad = _layout(d)

    def body(x_ref, g_ref, mod_ref, w_hbm, h_ref, zm_ref, zglu_ref, zgate_ref, u0_ref, w_ref):
        _load_resident(pl.program_id(0), [(w_hbm, w_ref)])
        n, _ = _rms(x_ref[...])
        h = ((n * g_ref[...]) * (1.0 + mod_ref[1:2, :]) + mod_ref[0:1, :]).astype(BF16)
        h_ref[...] = h
        z = _dot(h, w_ref[...])
        zgate_ref[...] = z[:, :p_glu]
        zglu = z[:, p_glu:p_q]
        zglu_ref[...] = zglu
        zm_ref[...] = z[:, p_q:]
        u0_ref[...] = zglu[:, :CONV_CH] * jax.nn.sigmoid(zglu[:, CONV_CH:])

    return pl.pallas_call(
        body, name="fwd_in", grid=(t // tm,),
        out_shape=(_sds((t, d), BF16), _sds((t, MLA_IN), F32), _sds((t, 2 * CONV_CH), F32), _sds((t, 2 * d), F32),
                   _sds((t, CONV_CH), F32)),
        in_specs=[_row(tm, d), _full((1, d)), _modspec(d, tps), ANY],
        out_specs=(_row(tm, d), _row(tm, MLA_IN), _row(tm, 2 * CONV_CH), _row(tm, 2 * d), _row(tm, CONV_CH)),
        scratch_shapes=[pltpu.VMEM(win_p.shape, BF16)],
        compiler_params=_params("arbitrary"),
    )(x, g1, mod3, win_p)


def _mla_prep(zm, gql, gkvl, gq, gk, tabs, wuq_p, wk_p, wv_p, tm, tps):
    t = zm.shape[0]
    c_t, s1_t, s2_t = tabs
    tab = pl.BlockSpec((tm, LANES), lambda i: (i % tps, 0))

    def body(zm_ref, gql_ref, gkvl_ref, gq_ref, gk_ref, c_ref, s1_ref, s2_ref, wuq_ref, wk_ref, wv_ref,
             q_ref, k_ref, v_ref, qln_ref, kvn_ref):
        c, s1, s2 = c_ref[...], s1_ref[...], s2_ref[...]
        nq, _ = _rms(zm_ref[:, :Q_RANK])
        qln = (nq * gql_ref[...]).astype(BF16)
        qln_ref[...] = qln
        qpre = _dot(qln, wuq_ref[...])
        nkv, _ = _rms(zm_ref[:, Q_RANK:OFF_KV])
        kvn = (nkv * gkvl_ref[...]).astype(BF16)
        kvn_ref[...] = kvn
        knope = _dot(kvn, wk_ref[...])
        v_ref[...] = _dot(kvn, wv_ref[...]).astype(BF16)
        zkr_v = zm_ref[:, OFF_KV:]
        kr_roped = _rope(zkr_v * gk_ref[...], c, s1, s2)
        slabs = [slice(hd * LANES, (hd + 1) * LANES) for hd in range(N_HEADS)]
        rq = [_head_rms(qpre[:, sl])[1] for sl in slabs]
        rk = [_head_rms(knope[:, sl] + zkr_v)[1] for sl in slabs]
        for hd, sl in enumerate(slabs):
            q_ref[:, sl] = _rope((qpre[:, sl] * rq[hd]) * gq_ref[...], c, s1, s2).astype(BF16)
            k_ref[:, sl] = (rk[hd] * (knope[:, sl] * gk_ref[...] + kr_roped)).astype(BF16)

    return pl.pallas_call(
        body, name="mla_prep", grid=(t // tm,),
        out_shape=(_sds((t, HW), BF16),) * 3 + (_sds((t, Q_RANK), BF16), _sds((t, KV_RANK), BF16)),
        in_specs=[_row(tm, MLA_IN), _full((1, Q_RANK)), _full((1, KV_RANK)),
                  _full((1, LANES)), _full((1, LANES)), tab, tab, tab,
                  _full(wuq_p.shape), _full(wk_p.shape), _full(wv_p.shape)],
        out_specs=(_row(tm, HW),) * 3 + (_row(tm, Q_RANK), _row(tm, KV_RANK)),
        compiler_params=_params("arbitrary"),
    )(zm, gql, gkvl, gq, gk, c_t, s1_t, s2_t, wuq_p, wk_p, wv_p)


AHEAD = 2
ROW_BAND = 256
SM_SCALE = QK_HEAD ** -0.5
EXP2_SCALE = SM_SCALE * 1.4426950408889634


def _diag_mask():
    rc = jnp.right_shift(lax.broadcasted_iota(jnp.int32, (BQ, 1), 0), CHUNK_SHIFT)
    cc = jnp.right_shift(lax.broadcasted_iota(jnp.int32, (1, BQ), 1), CHUNK_SHIFT)
    return rc >= cc


def _scores(q_i, k_ref, lo, e):
    return (_dot_nt(q_i, k_ref[:lo, :]) if lo else None), _dot_nt(q_i, k_ref[lo:e, :])


def _softmax_parts(scores, mask):
    sp, sd = scores
    sd = jnp.where(mask, sd, jnp.finfo(F32).min)
    m = jnp.max(sd, axis=-1, keepdims=True)
    if sp is not None:
        m = jnp.maximum(m, jnp.max(sp, axis=-1, keepdims=True))
    pd = jnp.exp2((sd - m) * EXP2_SCALE)
    l = jnp.sum(pd, axis=-1, keepdims=True)
    pp = None
    if sp is not None:
        pp = jnp.exp2((sp - m) * EXP2_SCALE)
        l = l + jnp.sum(pp, axis=-1, keepdims=True)
    return pp, pd, l


def _attn_fwd(q, k, v, nseq, seq, gather=()):
    t = q.shape[0]
    na = len(gather)
    blk = pl.BlockSpec((seq, LANES), lambda b, h: (b, h))
    n_steps = nseq * N_HEADS

    def body(q_ref, k_ref, v_ref, *rest):
        o_ref = rest[na]
        if na:
            start, forward, finish = _gather8_phases(rest[:na], rest[na + 1:2 * na + 1], *rest[2 * na + 1:])
            step = pl.program_id(0) * N_HEADS + pl.program_id(1)
            pl.when(step == 0)(start)
            pl.when(step == (7 * n_steps) // 8)(forward)
        mask = _diag_mask()
        nb = seq // BQ
        block_scores = lambda j: _scores(q_ref[j * BQ:(j + 1) * BQ, :], k_ref, j * BQ, (j + 1) * BQ)
        ahead = [block_scores(j) for j in range(min(AHEAD, nb))]
        for i in range(nb):
            lo, e = i * BQ, (i + 1) * BQ
            cur = ahead.pop(0)
            if i + AHEAD < nb:
                ahead.append(block_scores(i + AHEAD))
            pp, pd, l = _softmax_parts(cur, mask)
            o = _dot(pd.astype(BF16), v_ref[lo:e, :])
            if lo:
                o = o + _dot(pp.astype(BF16), v_ref[:lo, :])
            o_ref[lo:e, :] = (o * (1.0 / l)).astype(BF16)
        if na:
            pl.when(step == n_steps - 1)(finish)

    res = pl.pallas_call(
        body, name="attn_fwd", grid=(nseq, N_HEADS), out_shape=(_sds((t, HW), BF16),) + _gather8_shapes(gather),
        in_specs=[blk, blk, blk] + [ANY] * na, out_specs=(blk,) + (ANY,) * na,
        scratch_shapes=_gather8_sems(na) if na else [],
        compiler_params=_params("arbitrary", "arbitrary"),
    )(q, k, v, *gather)
    return res[0], (_own_block_placed(res[1:], gather) if na else ())


def _fwd_mix(attn, u0, zgate, x, mod3, wo_p, cw, cb, lng, lnb, wpw, wout, tm, tps):
    t, d = x.shape
    hpt = tm // HALO
    cwc, cbc = _by_lane_chunk(cw), _by_lane_chunk(cb)

    def body(a_ref, u_ref, uh_ref, zg_ref, x_ref, mod_ref, wo_ref, cw_ref, cb_ref, lng_ref, lnb_ref, wpw_ref, wout_ref,
             x1_ref, mixed_ref, mpre_ref, ya_ref, yb_ref, u1_ref, u3_ref, ext_ref):
        i = pl.program_id(0)
        first = (i % tps) == 0
        _fill_shifted(ext_ref, jnp.where(first, 0.0, uh_ref[...]), u_ref[...])
        nb = max(tm // ROW_BAND, 1)
        bw = tm // nb
        bands = [slice(b * bw, (b + 1) * bw) for b in range(nb)]
        yas = [_dot(a_ref[rows, :], wo_ref[...]) for rows in bands]
        u3s = []
        for b, rows in enumerate(bands):
            ya_ref[rows, :] = yas[b]
            for lc, ls in _lane_chunks():
                acc = jnp.broadcast_to(cb_ref[lc], (bw, LANES))
                for kk in range(CONV_W):
                    o = HALO - (CONV_W - 1) + kk
                    a = (o // SUBLANES) * SUBLANES + b * bw
                    acc = acc + cw_ref[lc, kk:kk + 1, :] * ext_ref[o % SUBLANES, lc, a:a + bw, :]
                u1_ref[rows, ls] = acc
            acc = u1_ref[rows, :]
            mu = jnp.mean(acc, axis=-1, keepdims=True)
            xc = acc - mu
            rstd = lax.rsqrt(jnp.mean(xc * xc, axis=-1, keepdims=True) + EPS)
            l = (xc * rstd) * lng_ref[...] + lnb_ref[...]
            u3 = (l * jax.nn.sigmoid(l)).astype(BF16)
            u3_ref[rows, :] = u3
            u3s.append(u3)
        ybs = [_dot(u3, wpw_ref[...]) for u3 in u3s]
        mpres = []
        for b, rows in enumerate(bands):
            yb_ref[rows, :] = ybs[b]
            mpre = (jax.nn.sigmoid(zg_ref[rows, :d]) * yas[b] + jax.nn.sigmoid(zg_ref[rows, d:]) * ybs[b]).astype(BF16)
            mpre_ref[rows, :] = mpre
            mpres.append(mpre)
        for rows, mpre in zip(bands, mpres):
            mixed = _dot(mpre, wout_ref[...])
            mixed_ref[rows, :] = mixed
            x1_ref[rows, :] = x_ref[rows, :] + mod_ref[2:3, :] * mixed

    halo = pl.BlockSpec((HALO, CONV_CH), lambda i: (jnp.maximum(i * hpt - 1, 0), 0))
    return pl.pallas_call(
        body, name="fwd_mix", grid=(t // tm,),
        out_shape=(_sds((t, d), F32), _sds((t, d), F32), _sds((t, d), BF16), _sds((t, d), F32), _sds((t, d), F32),
                   _sds((t, CONV_CH), F32), _sds((t, CONV_CH), BF16)),
        in_specs=[_row(tm, HW), _row(tm, CONV_CH), halo, _row(tm, 2 * d), _row(tm, d), _modspec(d, tps),
                  _full(wo_p.shape), _full(cwc.shape), _full(cbc.shape), _full((1, CONV_CH)), _full((1, CONV_CH)),
                  _full(wpw.shape), _full(wout.shape)],
        out_specs=(_row(tm, d), _row(tm, d), _row(tm, d), _row(tm, d), _row(tm, d), _row(tm, CONV_CH),
                   _row(tm, CONV_CH)),
        scratch_shapes=[pltpu.VMEM(_shifted_shape(tm), F32)],
        compiler_params=_params("arbitrary"),
    )(attn, u0, u0, zgate, x, mod3, wo_p, cwc, cbc, lng, lnb, wpw, wout)


def _shards_into_columns(w_hbm, w_ref):
    ns = w_hbm.shape[2]
    return [(w_hbm.at[s], w_ref.at[:, pl.ds(s * ns, ns)]) for s in range(w_hbm.shape[0])]


def _fwd_ffn(x1, target, g2, mod3, w1, w2, tm, tps):
    t, d = x1.shape
    dff = w1.shape[0] * w1.shape[2]

    def body(x1_ref, tg_ref, g_ref, mod_ref, w1_hbm, w2_hbm,
             h2_ref, a_ref, r_ref, dy_ref, df_ref, dgate_ref, loss_ref, w1_ref, w2_ref):
        i = pl.program_id(0)
        _load_resident(i, _shards_into_columns(w1_hbm, w1_ref) + [(w2_hbm, w2_ref)])
        x1v = x1_ref[...]
        gate2 = mod_ref[5:6, :]
        n, _ = _rms(x1v)
        h2 = ((n * g_ref[...]) * (1.0 + mod_ref[4:5, :]) + mod_ref[3:4, :]).astype(BF16)
        h2_ref[...] = h2
        a = _dot(h2, w1_ref[...])
        a_ref[...] = a
        r = jnp.square(jnp.maximum(a, 0.0)).astype(BF16)
        r_ref[...] = r
        f = _dot(r, w2_ref[...])
        e = (x1v + gate2 * f) - tg_ref[...]
        part = 0.5 * jnp.sum(jnp.mean(e * e, axis=-1, keepdims=True), axis=0, keepdims=True)
        _acc(loss_ref, jnp.broadcast_to(part, loss_ref.shape), i == 0)
        dy = e * (1.0 / d)
        dy_ref[...] = dy
        df_ref[...] = (dy * gate2).astype(BF16)
        _acc(dgate_ref, jnp.sum(dy * f, axis=0, keepdims=True), (i % tps) == 0)

    nseq = t // (tm * tps)
    return pl.pallas_call(
        body, name="fwd_ffn", grid=(t // tm,),
        out_shape=(_sds((t, d), BF16), _sds((t, dff), F32), _sds((t, dff), BF16), _sds((t, d), F32), _sds((t, d), BF16),
                   _sds((nseq, 1, d), F32), _sds((8, LANES), F32)),
        in_specs=[_row(tm, d), _row(tm, d), _full((1, d)), _modspec(d, tps), ANY, ANY],
        out_specs=(_row(tm, d), _row(tm, dff), _row(tm, dff), _row(tm, d), _row(tm, d), _seqv(d, tps),
                   _full((8, LANES))),
        scratch_shapes=[pltpu.VMEM((d, dff), BF16), pltpu.VMEM(w2.shape, BF16)],
        compiler_params=_params("arbitrary"),
    )(x1, target, g2, mod3, w1, w2)


def _bwd_ffn(df, a, x1, dy, mixed, g2, mod3, w2, w1, tm, tps):
    t, d = x1.shape
    dff = a.shape[1]

    def body(df_ref, a_ref, x1_ref, dy_ref, mx_ref, g_ref, mod_ref, w2_hbm, w1_hbm,
             da_ref, dx1_ref, dmixed_ref, dshift_ref, dscale_ref, dgate1_ref, dg2_ref, w2_ref, w1_ref):
        i = pl.program_id(0)
        _load_resident(i, [(w2_hbm, w2_ref)] + _shards_into_columns(w1_hbm, w1_ref))
        first_seq = (i % tps) == 0
        dr = _dot_nt(df_ref[...], w2_ref[...])
        da = (dr * (2.0 * jnp.maximum(a_ref[...], 0.0))).astype(BF16)
        da_ref[...] = da
        dh2 = _dot_nt(da, w1_ref[...])
        n, r = _rms(x1_ref[...])
        g = g_ref[...]
        sc1 = 1.0 + mod_ref[4:5, :]
        _acc(dshift_ref, jnp.sum(dh2, axis=0, keepdims=True), first_seq)
        _acc(dscale_ref, jnp.sum(dh2 * (n * g), axis=0, keepdims=True), first_seq)
        _acc(dg2_ref, jnp.sum((dh2 * sc1) * n, axis=0, keepdims=True), i == 0)
        dx1 = dy_ref[...] + _rms_bwd(n, r, (dh2 * sc1) * g)
        dx1_ref[...] = dx1
        _acc(dgate1_ref, jnp.sum(dx1 * mx_ref[...], axis=0, keepdims=True), first_seq)
        dmixed_ref[...] = (dx1 * mod_ref[2:3, :]).astype(BF16)

    nseq = t // (tm * tps)
    sv = _sds((nseq, 1, d), F32)
    return pl.pallas_call(
        body, name="bwd_ffn", grid=(t // tm,),
        out_shape=(_sds((t, dff), BF16), _sds((t, d), F32), _sds((t, d), BF16), sv, sv, sv, _sds((1, d), F32)),
        in_specs=[_row(tm, d), _row(tm, dff), _row(tm, d), _row(tm, d), _row(tm, d), _full((1, d)), _modspec(d, tps),
                  ANY, ANY],
        out_specs=(_row(tm, dff), _row(tm, d), _row(tm, d), _seqv(d, tps), _seqv(d, tps), _seqv(d, tps),
                   _full((1, d))),
        scratch_shapes=[pltpu.VMEM(w2.shape, BF16), pltpu.VMEM((d, dff), BF16)],
        compiler_params=_params("arbitrary"),
    )(df, a, x1, dy, mixed, g2, mod3, w2, w1)


def _bwd_mix(dmixed, zgate, ya, yb, u1, lng, lnb, wout, wo_p, wpw, tm, swap=()):
    t, d = ya.shape
    _, _, npad = _layout(d)
    nw = len(swap)
    n_steps = t // tm

    def body(dm_ref, zg_ref, ya_ref, yb_ref, u1_ref, lng_ref, lnb_ref, wout_ref, wo_ref, wpw_ref, *rest):
        dya_ref, dyb_ref, dz_ref, do_ref, du1_ref, dlng_ref, dlnb_ref, dcb_ref = rest[nw:nw + 8]
        i = pl.program_id(0)
        if nw:
            start, finish = _swap_phases(rest[:nw], rest[nw + 8:2 * nw + 8], *rest[2 * nw + 8:])
            pl.when(i == 0)(start)
        nb = max(tm // ROW_BAND, 1)
        bands = [slice(b * (tm // nb), (b + 1) * (tm // nb)) for b in range(nb)]
        col = lambda v: jnp.sum(v, axis=0, keepdims=True)
        dmpre = [_dot_nt(dm_ref[rows, :], wout_ref[...]) for rows in bands]
        dyab = []
        for rows, dmp in zip(bands, dmpre):
            ga = jax.nn.sigmoid(zg_ref[rows, :d])
            gb = jax.nn.sigmoid(zg_ref[rows, d:])
            dya = (dmp * ga).astype(BF16)
            dyb = (dmp * gb).astype(BF16)
            dya_ref[rows, :] = dya
            dyb_ref[rows, :] = dyb
            dz_ref[rows, :d] = ((dmp * ya_ref[rows, :]) * (ga * (1.0 - ga))).astype(BF16)
            dz_ref[rows, d:] = ((dmp * yb_ref[rows, :]) * (gb * (1.0 - gb))).astype(BF16)
            dyab.append((dya, dyb))
        du3s = []
        for rows, (dya, dyb) in zip(bands, dyab):
            do_ref[rows, :] = _dot_nt(dya, wo_ref[...]).astype(BF16)
            du3s.append(_dot_nt(dyb, wpw_ref[...]))
        sums = [jnp.zeros((1, CONV_CH), F32)] * 3
        for rows, du3 in zip(bands, du3s):
            u1 = u1_ref[rows, :]
            mu = jnp.mean(u1, axis=-1, keepdims=True)
            xc = u1 - mu
            rstd = lax.rsqrt(jnp.mean(xc * xc, axis=-1, keepdims=True) + EPS)
            nh = xc * rstd
            l = nh * lng_ref[...] + lnb_ref[...]
            sg = jax.nn.sigmoid(l)
            dl = du3 * (sg * (1.0 + l * (1.0 - sg)))
            dnh = dl * lng_ref[...]
            du1 = rstd * (dnh - jnp.mean(dnh, axis=-1, keepdims=True)
                          - nh * jnp.mean(dnh * nh, axis=-1, keepdims=True))
            du1_ref[rows, :] = du1
            sums = [sums[0] + col(dl * nh), sums[1] + col(dl), sums[2] + col(du1)]
        _acc(dlng_ref, sums[0], i == 0)
        _acc(dlnb_ref, sums[1], i == 0)
        _acc(dcb_ref, sums[2], i == 0)
        if nw:
            pl.when(i == n_steps - 1)(finish)

    cv = _sds((1, CONV_CH), F32)
    res = pl.pallas_call(
        body, name="bwd_mix", grid=(n_steps,),
        out_shape=(_sds((t, d), BF16), _sds((t, d), BF16), _sds((t, npad), BF16), _sds((t, HW), BF16),
                   _sds((t, CONV_CH), F32), cv, cv, cv) + _swap_shapes(swap),
        in_specs=[_row(tm, d), _row(tm, 2 * d), _row(tm, d), _row(tm, d), _row(tm, CONV_CH), _full((1, CONV_CH)),
                  _full((1, CONV_CH)), _full(wout.shape), _full(wo_p.shape), _full(wpw.shape)] + [ANY] * nw,
        out_specs=(_row(tm, d), _row(tm, d), _row(tm, 2 * d), _row(tm, HW), _row(tm, CONV_CH),
                   _full((1, CONV_CH)), _full((1, CONV_CH)), _full((1, CONV_CH))) + (ANY,) * nw,
        scratch_shapes=_swap_sems(swap) if nw else [],
        compiler_params=_params("arbitrary"),
    )(dmixed, zgate, ya, yb, u1, lng, lnb, wout, wo_p, wpw, *swap)
    return res[:8] + (res[8:],)


def _bwd_conv(dz, du1, u0, zglu, cw, tm, tps):
    t = du1.shape[0]
    d = (dz.shape[1] - MLA_IN - 2 * CONV_CH) // 2
    p_glu, _, _ = _layout(d)
    hpt = tm // HALO
    last_blk = t // HALO - 1
    cwc = _by_lane_chunk(cw)

    def body(dz_hbm, du_ref, dun_ref, u_ref, zl_ref, cw_ref, dzl_ref, dcw_ref, dext_ref, uc_ref, dcw8_ref, du0_ref):
        i = pl.program_id(0)
        last = (i % tps) == (tps - 1)
        _fill_shifted(dext_ref, du_ref[...], jnp.where(last, 0.0, dun_ref[...]))
        for lc, ls in _lane_chunks():
            uc_ref[lc] = u_ref[:, ls]

        @pl.when(i == 0)
        def _():
            dcw8_ref[...] = jnp.zeros_like(dcw8_ref)

        groups = CONV_ROWS // SUBLANES

        def conv_chunk(c, carry):
            lc, r0 = _conv_chunk(c)
            u = uc_ref[lc, pl.ds(r0, CONV_ROWS), :]
            du0 = jnp.zeros((CONV_ROWS, LANES), F32)
            for kk in range(CONV_W):
                win = _shifted(dext_ref, CONV_W - 1 - kk, lc, r0)
                prod = u * win
                part = prod[:SUBLANES]
                for g in range(1, groups):
                    part = part + prod[g * SUBLANES:(g + 1) * SUBLANES]
                dcw8_ref[lc, kk] += part
                du0 = du0 + cw_ref[lc, kk:kk + 1, :] * win
            du0_ref[lc, pl.ds(r0, CONV_ROWS), :] = du0
            return carry

        lax.fori_loop(0, CONV_LC * (tm // CONV_ROWS), conv_chunk, 0)

        @pl.when(i == pl.num_programs(0) - 1)
        def _():
            for lc, ls in _lane_chunks():
                dcw_ref[:, ls] = jnp.sum(dcw8_ref[lc], axis=1)

        for lc, ls in _lane_chunks():
            du0 = du0_ref[lc]
            ga = zl_ref[:, ls]
            sb = jax.nn.sigmoid(zl_ref[:, CONV_CH + lc * LANES:CONV_CH + (lc + 1) * LANES])
            dzl_ref[:, ls] = (du0 * sb).astype(BF16)
            dzl_ref[:, CONV_CH + lc * LANES:CONV_CH + (lc + 1) * LANES] = ((du0 * ga) * (sb * (1.0 - sb))).astype(BF16)

    nxt = pl.BlockSpec((HALO, CONV_CH), lambda i: (jnp.minimum((i + 1) * hpt, last_blk), 0))
    glu_blk = p_glu // (2 * CONV_CH)
    return pl.pallas_call(
        body, name="bwd_conv", grid=(t // tm,),
        out_shape=(_sds(dz.shape, BF16), _sds(cw.shape, F32)),
        in_specs=[ANY, _row(tm, CONV_CH), nxt, _row(tm, CONV_CH), _row(tm, 2 * CONV_CH), _full(cwc.shape)],
        out_specs=(pl.BlockSpec((tm, 2 * CONV_CH), lambda i: (i, glu_blk)), _full(cw.shape)),
        scratch_shapes=[pltpu.VMEM(_shifted_shape(tm), F32), pltpu.VMEM((CONV_LC, tm, LANES), F32),
                        pltpu.VMEM((CONV_LC, HALO, SUBLANES, LANES), F32), pltpu.VMEM((CONV_LC, tm, LANES), F32)],
        input_output_aliases={0: 0},
        compiler_params=_params("arbitrary"),
    )(dz, du1, du1, u0, zglu, cwc)


def _attn_bwd(q, k, v, do, nseq, seq, scatter=()):
    t = q.shape[0]
    ns = len(scatter)
    blk = pl.BlockSpec((seq, LANES), lambda b, h: (b, h))
    n_steps = nseq * N_HEADS

    def body(q_ref, k_ref, v_ref, do_ref, *rest):
        dq_ref, dk_ref, dv_ref = rest[ns:ns + 3]
        dka_ref, dva_ref = rest[2 * ns + 3:2 * ns + 5]
        if ns:
            start, finish = _scatter_phases(rest[:ns], rest[ns + 3:2 * ns + 3], *rest[2 * ns + 5:])
            step = pl.program_id(0) * N_HEADS + pl.program_id(1)
            pl.when(step == 0)(start)
        dka_ref[...] = jnp.zeros_like(dka_ref)
        dva_ref[...] = jnp.zeros_like(dva_ref)
        mask = _diag_mask()
        nb = seq // BQ
        block = lambda j: (_scores(q_ref[j * BQ:(j + 1) * BQ, :], k_ref, j * BQ, (j + 1) * BQ),
                           _scores(do_ref[j * BQ:(j + 1) * BQ, :], v_ref, j * BQ, (j + 1) * BQ))
        ahead = [block(j) for j in range(min(AHEAD, nb))]

        def second_stage(lo, e, dsd, dsp, pdb, ppb):
            q_i = q_ref[lo:e, :]
            do_i = do_ref[lo:e, :]
            dq = _dot(dsd, k_ref[lo:e, :])
            dka_ref[lo:e, :] += _dot_tn(dsd, q_i)
            dva_ref[lo:e, :] += _dot_tn(pdb, do_i)
            if lo:
                dq = dq + _dot(dsp, k_ref[:lo, :])
                dka_ref[:lo, :] += _dot_tn(dsp, q_i)
                dva_ref[:lo, :] += _dot_tn(ppb, do_i)
            dq_ref[lo:e, :] = dq * SM_SCALE

        held = None
        for i in range(nb):
            lo, e = i * BQ, (i + 1) * BQ
            scores, (dpp, dpd) = ahead.pop(0)
            if i + AHEAD < nb:
                ahead.append(block(i + AHEAD))
            pp, pd, l = _softmax_parts(scores, mask)
            inv = 1.0 / l
            pd = pd * inv
            delta = jnp.sum(pd * dpd, axis=-1, keepdims=True)
            if lo:
                pp = pp * inv
                delta = delta + jnp.sum(pp * dpp, axis=-1, keepdims=True)
            dsd = (pd * (dpd - delta)).astype(BF16)
            dsp = (pp * (dpp - delta)).astype(BF16) if lo else None
            if held is not None:
                second_stage(*held)
            held = (lo, e, dsd, dsp, pd.astype(BF16), pp.astype(BF16) if lo else None)
        second_stage(*held)
        dk_ref[...] = dka_ref[...] * SM_SCALE
        dv_ref[...] = dva_ref[...].astype(BF16)
        if ns:
            pl.when(step == n_steps - 1)(finish)

    res = pl.pallas_call(
        body, name="attn_bwd", grid=(nseq, N_HEADS),
        out_shape=(_sds((t, HW), F32), _sds((t, HW), F32), _sds((t, HW), BF16)) + _scatter_shapes(scatter),
        in_specs=[blk] * 4 + [ANY] * ns, out_specs=(blk,) * 3 + (ANY,) * ns,
        scratch_shapes=[pltpu.VMEM((seq, LANES), F32), pltpu.VMEM((seq, LANES), F32)]
        + (_scatter_sems(ns) if ns else []),
        compiler_params=_params("arbitrary", "arbitrary"),
    )(q, k, v, do, *scatter)
    return res[0], res[1], res[2], res[3:]


def _mla_bwd(dz, dq, dk, dv, zm, gql, gkvl, gq, gk, tabs, wuq_p, wk_p, wv_p, tm, tps):
    t = zm.shape[0]
    d = (dz.shape[1] - MLA_IN - 2 * CONV_CH) // 2
    _, p_q, _ = _layout(d)
    c_t, s1_t, s2_t = tabs
    tab = pl.BlockSpec((tm, LANES), lambda i: (i % tps, 0))

    def body(dz_hbm, dq_ref, dk_ref, dv_ref, zm_ref, gql_ref, gkvl_ref, gq_ref, gk_ref, c_ref, s1_ref, s2_ref,
             wuq_ref, wk_ref, wv_ref,
             dzm_ref, dqpre_ref, dkh_ref, dgq_ref, dgk_ref, dgql_ref, dgkvl_ref):
        i = pl.program_id(0)
        c, s1, s2 = c_ref[...], s1_ref[...], s2_ref[...]
        nq, rq = _rms(zm_ref[:, :Q_RANK])
        qpre = _dot((nq * gql_ref[...]).astype(BF16), wuq_ref[...])
        nkv, rkv = _rms(zm_ref[:, Q_RANK:OFF_KV])
        knope = _dot((nkv * gkvl_ref[...]).astype(BF16), wk_ref[...])
        zkr_v = zm_ref[:, OFF_KV:]
        gk = gk_ref[...]
        kr_roped = _rope(zkr_v * gk, c, s1, s2)
        dgq = jnp.zeros((1, LANES), F32)
        dgk = jnp.zeros((1, LANES), F32)
        dzkr = jnp.zeros((tm, LANES), F32)
        dt_sum = jnp.zeros((tm, LANES), F32)
        slabs = [slice(hd * LANES, (hd + 1) * LANES) for hd in range(N_HEADS)]
        gq = gq_ref[...]
        rqh = [_head_rms(qpre[:, sl])[1] for sl in slabs]
        rkh = [_head_rms(knope[:, sl] + zkr_v)[1] for sl in slabs]
        dyr = [_rope_t(dq_ref[:, sl], c, s1, s2) for sl in slabs]
        nqh = [qpre[:, sl] * rqh[hd] for hd, sl in enumerate(slabs)]
        sq = [jnp.sum((dyr[hd] * gq) * nqh[hd], axis=-1, keepdims=True) for hd in range(N_HEADS)]
        dr = [jnp.sum(dk_ref[:, sl] * (knope[:, sl] * gk + kr_roped), axis=-1, keepdims=True) for sl in slabs]
        for hd, sl in enumerate(slabs):
            dgq = dgq + jnp.sum(dyr[hd] * nqh[hd], axis=0, keepdims=True)
            dqpre_ref[:, sl] = (rqh[hd] * (dyr[hd] * gq - nqh[hd] * (sq[hd] * (1.0 / QK_HEAD)))).astype(BF16)
            kn = knope[:, sl]
            r = rkh[hd]
            dt = dk_ref[:, sl] * r
            via_r = (dr[hd] * (r * r * r) * (-1.0 / QK_HEAD)) * (kn + zkr_v)
            dgk = dgk + jnp.sum(dt * kn, axis=0, keepdims=True)
            dt_sum = dt_sum + dt
            dzkr = dzkr + via_r
            dkh_ref[:, sl] = (dt * gk + via_r).astype(BF16)
        de = _rope_t(dt_sum, c, s1, s2)
        dzkr = dzkr + de * gk
        dgk = dgk + jnp.sum(de * zkr_v, axis=0, keepdims=True)
        _acc(dgq_ref, dgq[:, :QK_HEAD], i == 0)
        _acc(dgk_ref, dgk[:, :QK_HEAD], i == 0)
        dzm_ref[:, OFF_KV:] = dzkr.astype(BF16)
        dqln = _dot_nt(dqpre_ref[...], wuq_ref[...])
        _acc(dgql_ref, jnp.sum(dqln * nq, axis=0, keepdims=True), i == 0)
        dzm_ref[:, :Q_RANK] = _rms_bwd(nq, rq, dqln * gql_ref[...]).astype(BF16)
        dkvn = _dot_nt(dkh_ref[...], wk_ref[...]) + _dot_nt(dv_ref[...], wv_ref[...])
        _acc(dgkvl_ref, jnp.sum(dkvn * nkv, axis=0, keepdims=True), i == 0)
        dzm_ref[:, Q_RANK:OFF_KV] = _rms_bwd(nkv, rkv, dkvn * gkvl_ref[...]).astype(BF16)

    return pl.pallas_call(
        body, name="mla_bwd", grid=(t // tm,),
        out_shape=(_sds(dz.shape, BF16), _sds((t, HW), BF16), _sds((t, HW), BF16), _sds((1, QK_HEAD), F32),
                   _sds((1, QK_HEAD), F32), _sds((1, Q_RANK), F32), _sds((1, KV_RANK), F32)),
        in_specs=[ANY, _row(tm, HW), _row(tm, HW), _row(tm, HW), _row(tm, MLA_IN),
                  _full((1, Q_RANK)), _full((1, KV_RANK)), _full((1, LANES)), _full((1, LANES)), tab, tab, tab,
                  _full(wuq_p.shape), _full(wk_p.shape), _full(wv_p.shape)],
        out_specs=(pl.BlockSpec((tm, MLA_IN), lambda i: (i, p_q // MLA_IN)), _row(tm, HW), _row(tm, HW),
                   _full((1, QK_HEAD)), _full((1, QK_HEAD)), _full((1, Q_RANK)), _full((1, KV_RANK))),
        input_output_aliases={0: 0},
        compiler_params=_params("arbitrary"),
    )(dz, dq, dk, dv, zm, gql, gkvl, gq, gk, c_t, s1_t, s2_t, wuq_p, wk_p, wv_p)


def _bwd_in(dz, x, dx1, g1, mod3, win_p, tm, tps, scatter=()):
    t, d = x.shape
    npad = dz.shape[1]

    ns = len(scatter)
    n_steps = t // tm

    def body(dz_ref, x_ref, dx1_ref, g_ref, mod_ref, wt_hbm, *rest):
        gx_ref, dshift_ref, dscale_ref, dg1_ref = rest[ns:ns + 4]
        wt_ref = rest[2 * ns + 4]
        i = pl.program_id(0)
        if ns:
            start, finish = _scatter_phases(rest[:ns], rest[ns + 4:2 * ns + 4], *rest[2 * ns + 5:])
            pl.when(i == 0)(start)
        _load_resident(i, [(wt_hbm, wt_ref)])
        first_seq = (i % tps) == 0
        g = g_ref[...]
        sc1 = 1.0 + mod_ref[1:2, :]
        nb = max(tm // ROW_BAND, 1)
        bands = [slice(b * (tm // nb), (b + 1) * (tm // nb)) for b in range(nb)]
        dhs = [_dot_nt(dz_ref[rows, :], wt_ref[...]) for rows in bands]
        sums = [jnp.zeros((1, d), F32)] * 3
        col = lambda v: jnp.sum(v, axis=0, keepdims=True)
        for rows, dh in zip(bands, dhs):
            n, r = _rms(x_ref[rows, :])
            sums = [sums[0] + col(dh), sums[1] + col(dh * (n * g)), sums[2] + col((dh * sc1) * n)]
            gx_ref[rows, :] = dx1_ref[rows, :] + _rms_bwd(n, r, (dh * sc1) * g)
        _acc(dshift_ref, sums[0], first_seq)
        _acc(dscale_ref, sums[1], first_seq)
        _acc(dg1_ref, sums[2], i == 0)
        if ns:
            pl.when(i == n_steps - 1)(finish)

    nseq = t // (tm * tps)
    sv = _sds((nseq, 1, d), F32)
    res = pl.pallas_call(
        body, name="bwd_in", grid=(n_steps,),
        out_shape=(_sds((t, d), F32), sv, sv, _sds((1, d), F32)) + _scatter_shapes(scatter),
        in_specs=[_row(tm, npad), _row(tm, d), _row(tm, d), _full((1, d)), _modspec(d, tps), ANY] + [ANY] * ns,
        out_specs=(_row(tm, d), _seqv(d, tps), _seqv(d, tps), _full((1, d))) + (ANY,) * ns,
        scratch_shapes=[pltpu.VMEM(win_p.shape, BF16)] + (_scatter_sems(ns) if ns else []),
        compiler_params=_params("arbitrary"),
    )(dz, x, dx1, g1, mod3, win_p, *scatter)
    return res[0], res[1], res[2], res[3], res[4:]


def _tile_of(n, choices):
    for c in choices:
        if n % c == 0:
            return c
    return n


def _tn_matmul(a, b, name, col_shards=0):
    t, k = a.shape
    n = b.shape[1]
    tk = _tile_of(k, (1024, 512, 256, 128))
    tn = n // col_shards if col_shards else _tile_of(n, (1024, 896, 768, 512, 384, 256, 128))
    tt = _tile_of(t, (4096, 2048, 1024, 512, 256))

    def body(a_ref, b_ref, o_ref):
        _acc(o_ref, _dot_tn(a_ref[...], b_ref[...]), pl.program_id(2) == 0)

    if col_shards:
        out_shape, out_spec = _sds((col_shards, k, tn), F32), pl.BlockSpec((None, tk, tn), lambda i, j, s: (j, i, 0))
    else:
        out_shape, out_spec = _sds((k, n), F32), pl.BlockSpec((tk, tn), lambda i, j, s: (i, j))
    return pl.pallas_call(
        body, name=name, grid=(k // tk, n // tn, t // tt), out_shape=out_shape,
        in_specs=[pl.BlockSpec((tt, tk), lambda i, j, s: (s, i)), pl.BlockSpec((tt, tn), lambda i, j, s: (s, j))],
        out_specs=out_spec, compiler_params=_params("arbitrary", "arbitrary", "arbitrary"),
    )(a, b)


N_SHARD = 4
COL_SHARDED = ("w_in", "w_uq", "w_ukv", "w_o_mla", "w_pw_out", "w_ff1")
ROW_SHARDED = ("w_out", "w_ff2")
BIG = ("w_in", "w_uq", "w_ukv", "w_o_mla", "w_pw_out", "w_out", "w_ff1", "w_ff2")
SMALL = ("norm1_g", "q_latent_g", "kv_latent_g", "qk_norm_q_g", "qk_norm_k_g", "conv_b", "conv_ln_g", "conv_ln_b",
         "norm2_g")
WEIGHTS = ("w_ada", "b_ada", "norm1_g", "w_in", "q_latent_g", "w_uq", "kv_latent_g", "w_ukv", "qk_norm_q_g",
           "qk_norm_k_g", "w_o_mla", "conv_w", "conv_b", "conv_ln_g", "conv_ln_b", "w_pw_out", "w_out", "norm2_g",
           "w_ff1", "w_ff2")


def _pad_heads(w, width):
    k = w.shape[0]
    w3 = w.reshape(k, N_HEADS, width)
    return jnp.pad(w3, ((0, 0), (0, 0), (0, LANES - width))).reshape(k, HW)


def _unpad_heads(g, width):
    k = g.shape[0]
    return g.reshape(k, N_HEADS, LANES)[:, :, :width].reshape(k, N_HEADS * width)


def _win_segments(d):
    return [(OFF_GLU, OFF_GLU + 2 * d), (OFF_KR, OFF_GLU), (0, OFF_KV), KR_LANE, (OFF_KV, OFF_KR),
            LANES - KR_LANE - QK_ROPE]


def _pad_win(g4):
    _, d, ws = g4.shape
    parts = []
    for seg in _win_segments(d):
        if isinstance(seg, int):
            parts.append(jnp.zeros((d, seg), g4.dtype))
            continue
        a, b = seg
        while a < b:
            s = a // ws
            e = min(b, (s + 1) * ws)
            parts.append(g4[s, :, a - s * ws:e - s * ws])
            a = e
    return jnp.concatenate(parts, axis=1)


def _unpad_win(gp):
    d = gp.shape[0]
    ws = (OFF_GLU + 2 * d) // N_SHARD
    pieces, p = [], 0
    for seg in _win_segments(d):
        if isinstance(seg, int):
            p += seg
        else:
            pieces.append((seg[0], seg[1], p))
            p += seg[1] - seg[0]
    shards = []
    for s in range(N_SHARD):
        lo, hi = s * ws, (s + 1) * ws
        cols = [gp[:, p0 + max(a, lo) - a:p0 + min(b, hi) - a] for a, b, p0 in sorted(pieces) if max(a, lo) < min(b, hi)]
        shards.append(jnp.concatenate(cols, axis=1))
    return jnp.stack(shards)


def _col_shards(g):
    k, n = g.shape
    return g.reshape(k, N_SHARD, n // N_SHARD).transpose(1, 0, 2)


def _from_shards(g, name):
    ns, ks, nn = g.shape
    if name in ROW_SHARDED:
        return g.reshape(ns * ks, nn)
    return g.transpose(1, 0, 2).reshape(ks, ns * nn)


BY_SHARD = ("w_in", "w_ff1")
EARLY = ("w_in", "w_uq", "w_ukv")
LATE = ("w_o_mla", "w_pw_out", "w_out", "w_ff1", "w_ff2")


def _assemble(names, gathered):
    by_shard = {n: g.reshape((N_SHARD, 2 * g.shape[1]) + g.shape[2:]) for n, g in zip(names, gathered)}
    return {n: g if n in BY_SHARD else _from_shards(g, n) for n, g in by_shard.items()}


LARGE = ("w_in", "w_ff1", "w_ff2")
GROUP_A = ("w_out", "w_ff1", "w_ff2")
GROUP_B = ("w_in", "w_uq", "w_ukv", "w_o_mla", "w_pw_out")


def _pair_halves(g):
    return g.reshape(N_SHARD, 2, g.shape[1] // 2, g.shape[2])


def _pair_sums(names, halves, from_sibling):
    if not halves:
        return []
    cidx = lax.axis_index("c").reshape(1).astype(jnp.int32)
    out = {n: _add_pair(g, l, cidx, "pair_sum_" + n)
           for n, g, l in zip(names, halves, from_sibling) if n in LARGE}
    small = [j for j, n in enumerate(names) if n not in LARGE]
    if small:
        res = _add_pair_whole([halves[j] for j in small], [from_sibling[j] for j in small], cidx,
                              "pair_sum_small_" + names[small[0]])
        out.update({names[j]: r for j, r in zip(small, res)})
    return [out[n] for n in names]


def _local_step(x, target, mod, sp, w, late=None, tm=256):
    comm = late is not None
    w = dict(w)
    nseq, seq, d = x.shape
    t = nseq * seq
    tps = seq // tm
    xf = x.reshape(t, d)
    tg = target.reshape(t, d)
    mod3 = mod.reshape(nseq, N_MOD, d)

    win_p = _pad_win(w["w_in"])
    wuq_p = _pad_heads(w["w_uq"], QK_HEAD)
    wkv3 = w["w_ukv"].reshape(KV_RANK, N_HEADS, QK_NOPE + V_HEAD)
    wk_p = _pad_heads(wkv3[:, :, :QK_NOPE].reshape(KV_RANK, -1), QK_NOPE)
    wv_p = _pad_heads(wkv3[:, :, QK_NOPE:].reshape(KV_RANK, -1), V_HEAD)
    cw = jnp.pad(w["conv_w"], ((0, HALO - CONV_W), (0, 0)))
    pad_g = lambda g: jnp.pad(g, ((0, 0), (0, LANES - QK_HEAD)))
    gq, gk = pad_g(sp["qk_norm_q_g"]), pad_g(sp["qk_norm_k_g"])
    tabs = _rope_tables(seq)

    tm_in, tps_in = (2 * tm, tps // 2) if tps % 2 == 0 else (tm, tps)
    h, zm, zglu, zgate, u0 = _fwd_in(xf, sp["norm1_g"], mod3, win_p, tm_in, tps_in)
    q, k, v, qln, kvn = _mla_prep(zm, sp["q_latent_g"], sp["kv_latent_g"], gq, gk, tabs, wuq_p, wk_p, wv_p, tm_in,
                                  tps_in)
    attn, gathered = _attn_fwd(q, k, v, nseq, seq, tuple(late) if comm else ())
    if comm:
        w.update(_assemble(LATE, gathered))
    wo_p = jnp.pad(w["w_o_mla"].reshape(N_HEADS, V_HEAD, d), ((0, 0), (0, LANES - V_HEAD), (0, 0))).reshape(HW, d)
    x1, mixed, mpre, ya, yb, u1, u3 = _fwd_mix(attn, u0, zgate, xf, mod3, wo_p, cw, sp["conv_b"], sp["conv_ln_g"],
                                               sp["conv_ln_b"], w["w_pw_out"], w["w_out"], tm_in, tps_in)
    h2, a, r, dy, df, dgate2, loss_acc = _fwd_ffn(x1, tg, sp["norm2_g"], mod3, w["w_ff1"], w["w_ff2"], tm, tps)
    da, dx1, dmixed, dshift2, dscale2, dgate1, dg2 = _bwd_ffn(df, a, x1, dy, mixed, sp["norm2_g"], mod3,
                                                              w["w_ff2"], w["w_ff1"], tm, tps)
    gw = {
        "w_out": _tn_matmul(mpre, dmixed, "dw_out").reshape(N_SHARD, d // N_SHARD, d),
        "w_ff1": _tn_matmul(h2, da, "dw_ff1", N_SHARD),
        "w_ff2": _tn_matmul(r, df, "dw_ff2").reshape(N_SHARD, -1, d),
    }
    halves_a = [_pair_halves(gw[n]) for n in GROUP_A] if comm else []
    dya, dyb, dz, do, du1, dlng, dlnb, dcb, from_sibling = _bwd_mix(
        dmixed, zgate, ya, yb, u1, sp["conv_ln_g"], sp["conv_ln_b"], w["w_out"], wo_p, w["w_pw_out"], tm_in, tuple(halves_a))
    pair_a = _pair_sums(GROUP_A, halves_a, from_sibling)
    dz, dcw = _bwd_conv(dz, du1, u0, zglu, cw, tm_in, tps_in)
    gw["conv_w"] = dcw
    dq, dk, dv, land_a = _attn_bwd(q, k, v, do, nseq, seq, tuple(p[1] for p in pair_a))
    dz, dqpre, dkh, dgq, dgk, dgql, dgkvl = _mla_bwd(dz, dq, dk, dv, zm, sp["q_latent_g"], sp["kv_latent_g"], gq, gk,
                                                      tabs, wuq_p, wk_p, wv_p, tm_in, tps_in)
    dwk_p = _tn_matmul(kvn, dkh, "dw_uk")
    dwv_p = _tn_matmul(kvn, dv, "dw_uv")
    dwkv = jnp.concatenate([dwk_p.reshape(KV_RANK, N_HEADS, LANES)[:, :, :QK_NOPE],
                            dwv_p.reshape(KV_RANK, N_HEADS, LANES)[:, :, :V_HEAD]], axis=2).reshape(KV_RANK, -1)
    dwo = _tn_matmul(attn, dya, "dw_o").reshape(N_HEADS, LANES, d)[:, :V_HEAD].reshape(MLA_WIDTH, d)
    gw["w_in"] = _unpad_win(_tn_matmul(h, dz, "dw_in"))
    gw["w_uq"] = _col_shards(_unpad_heads(_tn_matmul(qln, dqpre, "dw_uq"), QK_HEAD))
    gw["w_ukv"] = _col_shards(dwkv)
    gw["w_o_mla"] = _col_shards(dwo)
    gw["w_pw_out"] = _tn_matmul(u3, dyb, "dw_pw", N_SHARD)
    pair_b = []
    if comm:
        halves_b = [_pair_halves(gw[n]) for n in GROUP_B]
        pair_b = _pair_sums(GROUP_B, halves_b, _pair_swap(halves_b, "grad_pair_swap"))
    gx, dshift1, dscale1, dg1, land_b = _bwd_in(dz, xf, dx1, sp["norm1_g"], mod3, win_p, tm_in, tps_in,
                                                tuple(p[1] for p in pair_b))
    if comm:
        for n, p, l in zip(GROUP_A + GROUP_B, pair_a + pair_b, land_a + land_b):
            gw[n] = (p[0], l)
    gs = {
        "norm1_g": dg1, "q_latent_g": dgql, "kv_latent_g": dgkvl, "qk_norm_q_g": dgq, "qk_norm_k_g": dgk,
        "conv_b": dcb, "conv_ln_g": dlng, "conv_ln_b": dlnb, "norm2_g": dg2,
    }
    dmod = jnp.concatenate([dshift1, dscale1, dgate1, dshift2, dscale2, dgate2], axis=2).reshape(nseq, N_MOD * d)
    return loss_acc, gx.reshape(nseq, seq, d), dmod, gw, gs


def kernel(x, c, w_ada, b_ada, norm1_g, w_in, q_latent_g, w_uq, kv_latent_g, w_ukv, qk_norm_q_g, qk_norm_k_g, w_o_mla, conv_w, conv_b, conv_ln_g, conv_ln_b, w_pw_out, w_out, norm2_g, w_ff1, w_ff2, loss_target, m_w_ada, m_b_ada, m_norm1_g, m_w_in, m_q_latent_g, m_w_uq, m_kv_latent_g, m_w_ukv, m_qk_norm_q_g, m_qk_norm_k_g, m_w_o_mla, m_conv_w, m_conv_b, m_conv_ln_g, m_conv_ln_b, m_w_pw_out, m_w_out, m_norm2_g, m_w_ff1, m_w_ff2, v_w_ada, v_b_ada, v_norm1_g, v_w_in, v_q_latent_g, v_w_uq, v_kv_latent_g, v_w_ukv, v_qk_norm_q_g, v_qk_norm_k_g, v_w_o_mla, v_conv_w, v_conv_b, v_conv_ln_g, v_conv_ln_b, v_w_pw_out, v_w_out, v_norm2_g, v_w_ff1, v_w_ff2):
    given = dict(locals())
    wts = {n: given[n][0] for n in WEIGHTS}
    mom = {n: given["m_" + n][0] for n in WEIGHTS}
    var = {n: given["v_" + n][0] for n in WEIGHTS}
    vec = lambda a: a.reshape(1, -1)
    nseq, seq, d = x.shape
    ix, iy, ic = _place()
    shard = 2 * ix + iy

    half = lambda n: lax.dynamic_slice_in_dim(wts[n].astype(BF16), ic * (wts[n].shape[0] // 2), wts[n].shape[0] // 2,
                                              axis=0)
    gathered = _all_gather8([half(n) for n in EARLY] + [wts["conv_w"], c], "gather_weights")
    full = _assemble(EARLY, gathered)
    full["conv_w"] = _from_shards(gathered[-2][0::2], "conv_w")
    c_all = gathered[-1].reshape(8 * nseq, d)

    n_ada = wts["w_ada"].shape[1]
    b_sh = lax.dynamic_slice_in_dim(vec(wts["b_ada"]), shard * n_ada, n_ada, axis=1)
    mod_sh = _ada_mod(c_all, wts["w_ada"], b_sh)
    hb = 4 * nseq
    mod_blk = lax.dynamic_slice_in_dim(mod_sh, ic * hb, hb, axis=0)
    (mod_all,) = _all_gather8([mod_blk], "gather_mod")
    mod_mine = lax.dynamic_slice_in_dim(mod_all, (2 * iy + ic) * nseq, nseq, axis=1)
    mod = jnp.concatenate([lax.dynamic_index_in_dim(mod_mine, 2 * s + ix, axis=0, keepdims=False)
                           for s in range(N_SHARD)], axis=1)

    sp = {n: vec(wts[n]) for n in SMALL}
    loss_part, grad_x, dmod, gw, gs = _local_step(x, loss_target, mod, sp, full, [half(n) for n in LATE])

    own_c = jnp.stack([shard, ic]).astype(jnp.int32)
    mine_sum = {n: _add_chips(gw[n][0], gw[n][1], own_c, "chip_sum_" + n) for n in LARGE}
    few = tuple(n for n in BIG if n not in LARGE)
    mine_sum.update(zip(few, _add_chips_whole([gw[n][0] for n in few], [gw[n][1] for n in few], own_c, "chip_sum_small")))
    summed, parts = _pair_gather_and_all_gather8(
        [mine_sum[n] for n in BIG], [dmod, gw["conv_w"], loss_part] + [gs[n] for n in SMALL], "tail_exchange")

    dmod_all = parts[0].reshape(8 * nseq, N_MOD * d)
    dmod_sh = lax.dynamic_slice_in_dim(dmod_all, shard * n_ada, n_ada, axis=1)
    res = _ada_bwd(c_all, dmod_all, dmod_sh, parts[1:])
    grads = {"w_ada": res[0], "b_ada": res[1]}
    n_cw = wts["conv_w"].shape[1]
    grads["conv_w"] = lax.dynamic_slice_in_dim(res[2], shard * n_cw, n_cw, axis=1)[:CONV_W]
    loss = res[3][0, 0]
    for n, g in zip(SMALL, res[4:]):
        grads[n] = g
    for n, g in zip(BIG, summed):
        grads[n] = g.reshape(wts[n].shape)

    delta, new_m, new_v = {}, {}, {}
    for n in LARGE + ("w_ada",):
        if n == "w_in":
            res = _adamw(wts[n].T, grads[n].T, mom[n].T, var[n].T, "adamw_" + n)
            delta[n], new_m[n], new_v[n] = (a.T for a in res)
        else:
            delta[n], new_m[n], new_v[n] = _adamw(wts[n], grads[n], mom[n], var[n], "adamw_" + n)
    rest = ("b_ada", "conv_w") + SMALL + few
    as2d = lambda a: a if a.ndim == 2 else vec(a)
    res = _adamw_small(*[[as2d(t[n]) for n in rest] for t in (wts, grads, mom, var)])
    for dst, arrs in zip((delta, new_m, new_v), res):
        for n, a in zip(rest, arrs):
            dst[n] = a

    outs = [loss, grad_x]
    for group in (grads, delta, new_m, new_v):
        outs += [group[n].reshape(given[n].shape) for n in WEIGHTS]
    return tuple(outs)
```

```python
import jax
import jax.numpy as jnp
from jax import lax
from jax.experimental import pallas as pl
from jax.experimental.pallas import tpu as pltpu

F32 = jnp.float32
BF16 = jnp.bfloat16
MESH = pl.DeviceIdType.MESH
ANY = pl.BlockSpec(memory_space=pl.ANY)

CHUNK = 64
CHUNK_SHIFT = 6
N_HEADS = 8
QK_NOPE = 64
QK_ROPE = 32
QK_HEAD = QK_NOPE + QK_ROPE
V_HEAD = 64
Q_RANK = 256
KV_RANK = 128
MLA_WIDTH = N_HEADS * V_HEAD
CONV_CH = 512
CONV_W = 31
ROPE_THETA = 10000.0
EPS = 1e-6
LANES = 128
SUBLANES = 8
HW = N_HEADS * LANES
OFF_KV = Q_RANK + KV_RANK
OFF_KR = OFF_KV + QK_ROPE
OFF_GLU = OFF_KR + 2 * CONV_CH
KR_LANE = QK_NOPE
MLA_IN = Q_RANK + KV_RANK + LANES
HALO = 32
N_MOD = 6

ADAM_LR = 0.001
ADAM_B1 = 0.9
ADAM_B2 = 0.999
ADAM_EPS = 1e-08
ADAM_WD = 0.01
ADAM_STEP = 10

VMEM_LIMIT = 56 * 1024 * 1024
BQ = 256


def _layout(d):
    p_glu = 2 * d
    p_q = p_glu + 2 * CONV_CH
    return p_glu, p_q, p_q + MLA_IN


def _params(*sem):
    return pltpu.CompilerParams(dimension_semantics=sem, vmem_limit_bytes=VMEM_LIMIT)


def _dot(a, b):
    return jnp.dot(a, b, preferred_element_type=F32)


def _dot_tn(a, b):
    return lax.dot_general(a, b, (((0,), (0,)), ((), ())), preferred_element_type=F32)


def _dot_nt(a, b):
    return lax.dot_general(a, b, (((1,), (1,)), ((), ())), preferred_element_type=F32)


def _acc(ref, val, first):
    @pl.when(first)
    def _():
        ref[...] = val

    @pl.when(jnp.logical_not(first))
    def _():
        ref[...] += val


def _rms(x):
    r = lax.rsqrt(jnp.mean(x * x, axis=-1, keepdims=True) + EPS)
    return x * r, r


def _rms_bwd(n, r, dn):
    return r * (dn - n * jnp.mean(dn * n, axis=-1, keepdims=True))


def _head_rms(sl):
    r = lax.rsqrt(jnp.sum(sl * sl, axis=-1, keepdims=True) * (1.0 / QK_HEAD) + EPS)
    return sl * r, r


def _head_rms_bwd(n, r, dn):
    return r * (dn - n * (jnp.sum(dn * n, axis=-1, keepdims=True) * (1.0 / QK_HEAD)))


def _rope(x, c, s1, s2):
    return x * c + pltpu.roll(x, QK_ROPE // 2, 1) * s1 + pltpu.roll(x, LANES - QK_ROPE // 2, 1) * s2


def _rope_t(dy, c, s1, s2):
    return dy * c + pltpu.roll(dy * s1, LANES - QK_ROPE // 2, 1) + pltpu.roll(dy * s2, QK_ROPE // 2, 1)


def _rope_tables(seq):
    half = QK_ROPE // 2
    inv_freq = ROPE_THETA ** (-jnp.arange(0, QK_ROPE, 2, dtype=F32) / QK_ROPE)
    ang = jnp.arange(seq, dtype=F32)[:, None] * inv_freq[None, :]
    cos, sin = jnp.cos(ang), jnp.sin(ang)
    z = lambda n: jnp.zeros((seq, n), F32)
    tail = LANES - QK_HEAD
    c = jnp.concatenate([jnp.ones((seq, QK_NOPE), F32), cos, cos, jnp.ones((seq, tail), F32)], axis=1)
    s1 = jnp.concatenate([z(QK_NOPE + half), sin, z(tail)], axis=1)
    s2 = jnp.concatenate([z(QK_NOPE), -sin, z(half + tail)], axis=1)
    return c, s1, s2


def _row(tm, w):
    return pl.BlockSpec((tm, w), lambda i: (i, 0))


def _modspec(d, tps):
    return pl.BlockSpec((None, N_MOD, d), lambda i: (i // tps, 0, 0))


def _seqv(w, tps):
    return pl.BlockSpec((None, 1, w), lambda i: (i // tps, 0, 0))


def _full(shape):
    return pl.BlockSpec(shape, lambda i: tuple(0 for _ in shape))


def _sds(shape, dtype):
    return jax.ShapeDtypeStruct(shape, dtype)


CONV_ROWS = 64
CONV_LC = CONV_CH // LANES


def _lane_chunks():
    return [(lc, slice(lc * LANES, (lc + 1) * LANES)) for lc in range(CONV_LC)]


def _fill_shifted(ext_ref, head, body):
    nh = head.shape[0]
    for lc, ls in _lane_chunks():
        ext_ref[0, lc, :nh, :] = head[:, ls]
        ext_ref[0, lc, nh:, :] = body[:, ls]
        rows = ext_ref[0, lc]
        for b in range(1, SUBLANES):
            ext_ref[b, lc] = pltpu.roll(rows, rows.shape[0] - b, 0)


def _shifted_shape(tm):
    return (SUBLANES, CONV_LC, tm + HALO, LANES)


def _conv_chunk(c):
    return c % CONV_LC, pl.multiple_of((c // CONV_LC) * CONV_ROWS, CONV_ROWS)


def _shifted(ext_ref, o, lc, r0):
    a = pl.multiple_of((o // SUBLANES) * SUBLANES + r0, SUBLANES)
    return ext_ref[o % SUBLANES, lc, pl.ds(a, CONV_ROWS), :]


def _by_lane_chunk(a):
    return a.reshape(a.shape[0], CONV_LC, LANES).transpose(1, 0, 2)


def _load_resident(i, pairs):
    @pl.when(i == 0)
    def _():
        for src, dst in pairs:
            pltpu.sync_copy(src, dst)


def _place():
    return lax.axis_index("x"), lax.axis_index("y"), lax.axis_index("c")


def _all_gather8(blocks, name):
    na = len(blocks)

    def body(*refs):
        start, forward, finish = _gather8_phases(refs[:na], refs[na:2 * na], *refs[2 * na:])
        start()
        forward()
        finish()

    outs = pl.pallas_call(
        body, name=name, out_shape=_gather8_shapes(blocks), in_specs=[ANY] * na, out_specs=(ANY,) * na,
        scratch_shapes=_gather8_sems(na),
    )(*blocks)
    return _own_block_placed(outs, blocks)


def _gather8_shapes(blocks):
    return tuple(_sds((8,) + b.shape, b.dtype) for b in blocks)


def _gather8_sems(na):
    return [pltpu.SemaphoreType.DMA((7 * na,)), pltpu.SemaphoreType.DMA((7 * na,))]


def _own_block_placed(outs, blocks):
    ix, iy, ic = _place()
    return tuple(lax.dynamic_update_index_in_dim(o, b, 4 * ix + 2 * iy + ic, 0) for o, b in zip(outs, blocks))


def _gather8_phases(x_refs, out_refs, send_sems, recv_sems):
    na = len(x_refs)
    x, y, c = _place()
    me, sibling = (x, y, c), (x, y, 1 - c)
    chips = [(1 - x, y), (x, 1 - y), (1 - x, 1 - y)]

    def copy(a, k, blk, to, from_input=False):
        dst = out_refs[a].at[4 * blk[0] + 2 * blk[1] + blk[2]]
        return pltpu.make_async_remote_copy(
            src_ref=x_refs[a] if from_input else dst, dst_ref=dst,
            send_sem=send_sems.at[7 * a + k], recv_sem=recv_sems.at[7 * a + k], device_id=to, device_id_type=MESH)

    def first(a):
        return [copy(a, 0, me, sibling, True)] + [copy(a, 1 + j, me, (*chip, c), True) for j, chip in enumerate(chips)]

    def start():
        for a in range(na):
            for cp in first(a):
                cp.start()

    def forward():
        for j, chip in enumerate(chips):
            for a in range(na):
                copy(a, 1 + j, (*chip, c), me).wait_recv()
                copy(a, 4 + j, (*chip, c), sibling).start()

    def finish():
        for a in range(na):
            copy(a, 0, sibling, me).wait_recv()
            for j, chip in enumerate(chips):
                copy(a, 4 + j, (*chip, 1 - c), me).wait_recv()
        for a in range(na):
            for cp in first(a) + [copy(a, 4 + j, (*chip, c), sibling) for j, chip in enumerate(chips)]:
                cp.wait_send()

    return start, forward, finish


def _pair_swap(gs, name):
    na = len(gs)

    def body(*refs):
        start, finish = _swap_phases(refs[:na], refs[na:2 * na], *refs[2 * na:])
        start()
        finish()

    return pl.pallas_call(
        body, name=name, out_shape=_swap_shapes(gs), in_specs=[ANY] * na, out_specs=(ANY,) * na,
        scratch_shapes=_swap_sems(gs),
    )(*gs)


def _swap_shapes(gs):
    return tuple(_sds(g.shape[:1] + g.shape[2:], g.dtype) for g in gs)


def _swap_sems(gs):
    n = sum(g.shape[0] for g in gs)
    return [pltpu.SemaphoreType.DMA((n,)), pltpu.SemaphoreType.DMA((n,))]


def _swap_phases(g_refs, land_refs, send_sems, recv_sems):
    x, y, c = _place()

    def copies():
        cps, k = [], 0
        for g_ref, land_ref in zip(g_refs, land_refs):
            for s in range(g_ref.shape[0]):
                cps.append(pltpu.make_async_remote_copy(
                    src_ref=g_ref.at[s, 1 - c], dst_ref=land_ref.at[s], send_sem=send_sems.at[k],
                    recv_sem=recv_sems.at[k], device_id=(x, y, 1 - c), device_id_type=MESH))
                k += 1
        return cps

    def start():
        for cp in copies():
            cp.start()

    def finish():
        for cp in copies():
            cp.wait()

    return start, finish


def _scatter_shapes(hs):
    return tuple(_sds((3,) + h.shape[1:], h.dtype) for h in hs)


def _scatter_sems(na):
    return [pltpu.SemaphoreType.DMA((3 * na,)), pltpu.SemaphoreType.DMA((3 * na,))]


def _scatter_phases(h_refs, land_refs, send_sems, recv_sems):
    x, y, c = _place()
    chips = [(1 - x, y), (x, 1 - y), (1 - x, 1 - y)]

    def copies():
        return [pltpu.make_async_remote_copy(
            src_ref=h_refs[a].at[2 * tx + ty], dst_ref=land_refs[a].at[j], send_sem=send_sems.at[3 * a + j],
            recv_sem=recv_sems.at[3 * a + j], device_id=(tx, ty, c), device_id_type=MESH)
            for a in range(len(h_refs)) for j, (tx, ty) in enumerate(chips)]

    def start():
        for cp in copies():
            cp.start()

    def finish():
        for cp in copies():
            cp.wait()

    return start, finish


def _pair_gather_and_all_gather8(fs, blocks, name):
    nf, nb = len(fs), len(blocks)

    def body(*refs):
        f_refs = refs[nf + nb:2 * nf + nb]
        b_out = refs[2 * nf + nb:2 * nf + 2 * nb]
        send_sems, recv_sems, g_send, g_recv = refs[2 * nf + 2 * nb:]
        x, y, c = _place()
        start, forward, finish = _gather8_phases(refs[nf:nf + nb], b_out, g_send, g_recv)
        sends = [pltpu.make_async_remote_copy(
            src_ref=f_refs[a].at[c], dst_ref=f_refs[a].at[c], send_sem=send_sems.at[a], recv_sem=recv_sems.at[a],
            device_id=(x, y, 1 - c), device_id_type=MESH) for a in range(nf)]
        recvs = [pltpu.make_async_remote_copy(
            src_ref=f_refs[a].at[c], dst_ref=f_refs[a].at[1 - c], send_sem=send_sems.at[a],
            recv_sem=recv_sems.at[a], device_id=(x, y, 1 - c), device_id_type=MESH) for a in range(nf)]
        start()
        for cp in sends:
            cp.start()
        forward()
        finish()
        for cp in recvs:
            cp.wait_recv()
        for cp in sends:
            cp.wait_send()

    res = pl.pallas_call(
        body, name=name, out_shape=tuple(_sds(f.shape, f.dtype) for f in fs) + _gather8_shapes(blocks),
        in_specs=[ANY] * (nf + nb), out_specs=(ANY,) * (nf + nb), input_output_aliases={a: a for a in range(nf)},
        scratch_shapes=[pltpu.SemaphoreType.DMA((nf,)), pltpu.SemaphoreType.DMA((nf,))] + _gather8_sems(nb),
    )(*fs, *blocks)
    return res[:nf], _own_block_placed(res[nf:], blocks)


def _row_tile(r, n, itemsize=4, budget=1 << 21):
    if r * n * itemsize <= budget:
        return r
    best = None
    for tr in range(16, r, 16):
        if r % tr == 0 and tr * n * itemsize <= budget:
            best = tr
    assert best is not None, (r, n)
    return best


def _add_pair(g, land, c_own, name):
    ns, _, r, n = g.shape
    tr = _row_tile(r, n)

    def body(co_ref, a_ref, b_ref, o_ref, ob_ref):
        s = a_ref[...] + b_ref[...]
        ob_ref[...] = s.astype(BF16)

        @pl.when(pl.program_id(1) == co_ref[1])
        def _():
            o_ref[...] = s

    per_shard = pl.BlockSpec((None, tr, n), lambda i, s, co: (s, i, 0))
    return pl.pallas_call(
        body, name=name, out_shape=(_sds((r, n), F32), _sds((ns, r, n), BF16)),
        grid_spec=pltpu.PrefetchScalarGridSpec(
            num_scalar_prefetch=1, grid=(r // tr, ns),
            in_specs=[pl.BlockSpec((None, None, tr, n), lambda i, s, co: (s, co[0], i, 0)), per_shard],
            out_specs=(pl.BlockSpec((tr, n), lambda i, s, co: (i, 0)), per_shard)),
        compiler_params=_params("arbitrary", "arbitrary"),
    )(c_own, g, land)


def _add_pair_whole(gs, lands, cidx, name):
    k = len(gs)

    def body(c_ref, *refs):
        for a_ref, b_ref, o_ref, ob_ref in zip(refs[:k], refs[k:2 * k], refs[2 * k:3 * k], refs[3 * k:]):
            s = a_ref[...] + b_ref[...]
            o_ref[...] = s
            ob_ref[...] = s.astype(BF16)

    half = lambda g: pl.BlockSpec((g.shape[0], None) + g.shape[2:], lambda i, cr: (0, cr[0], 0, 0))
    whole = lambda g: pl.BlockSpec(g.shape[:1] + g.shape[2:], lambda i, cr: (0, 0, 0))
    shapes = lambda dt: tuple(_sds(g.shape[:1] + g.shape[2:], dt) for g in gs)
    res = pl.pallas_call(
        body, name=name, out_shape=shapes(F32) + shapes(BF16),
        grid_spec=pltpu.PrefetchScalarGridSpec(
            num_scalar_prefetch=1, grid=(1,),
            in_specs=[half(g) for g in gs] + [whole(g) for g in gs],
            out_specs=tuple(whole(g) for g in gs) * 2),
        compiler_params=_params("arbitrary"),
    )(cidx, *gs, *lands)
    return list(zip(res[:k], res[k:]))


def _add_chips_whole(hs, lands, own_c, name):
    k = len(hs)

    def body(o_idx, *refs):
        for h_ref, l_ref, o_ref in zip(refs[:k], refs[k:2 * k], refs[2 * k:]):
            o_ref[...] = ((h_ref[...] + l_ref[0].astype(F32)) + l_ref[1].astype(F32)) + l_ref[2].astype(F32)

    return pl.pallas_call(
        body, name=name, out_shape=tuple(_sds((2,) + h.shape[1:], F32) for h in hs),
        grid_spec=pltpu.PrefetchScalarGridSpec(
            num_scalar_prefetch=1, grid=(1,),
            in_specs=[pl.BlockSpec((None,) + h.shape[1:], lambda i, o: (o[0], 0, 0)) for h in hs]
            + [pl.BlockSpec(l.shape, lambda i, o: (0, 0, 0)) for l in lands],
            out_specs=tuple(pl.BlockSpec((None,) + h.shape[1:], lambda i, o: (o[1], 0, 0)) for h in hs)),
        compiler_params=_params("arbitrary"),
    )(own_c, *hs, *lands)


def _add_chips(h, land, own_c, name):
    r, n = h.shape
    tr = _row_tile(r, n)

    def body(o_idx, h_ref, l_ref, o_ref):
        o_ref[...] = ((h_ref[...] + l_ref[0].astype(F32)) + l_ref[1].astype(F32)) + l_ref[2].astype(F32)

    return pl.pallas_call(
        body, name=name, out_shape=_sds((2, r, n), F32),
        grid_spec=pltpu.PrefetchScalarGridSpec(
            num_scalar_prefetch=1, grid=(r // tr,),
            in_specs=[pl.BlockSpec((tr, n), lambda i, o: (i, 0)),
                      pl.BlockSpec((3, tr, n), lambda i, o: (0, i, 0))],
            out_specs=pl.BlockSpec((None, tr, n), lambda i, o: (o[1], i, 0))),
        compiler_params=_params("arbitrary"),
    )(own_c, h, land)


def _adam_math(w, g, m, v):
    nm = ADAM_B1 * m + (1.0 - ADAM_B1) * g
    nv = ADAM_B2 * v + (1.0 - ADAM_B2) * (g * g)
    m_hat = nm / (1.0 - ADAM_B1 ** ADAM_STEP)
    v_hat = nv / (1.0 - ADAM_B2 ** ADAM_STEP)
    return -ADAM_LR * (m_hat / (jnp.sqrt(v_hat) + ADAM_EPS) + ADAM_WD * w), nm, nv


def _adamw(w, g, m, v, name):
    r, n = w.shape

    def body(w_ref, g_ref, m_ref, v_ref, d_ref, nm_ref, nv_ref):
        d_ref[...], nm_ref[...], nv_ref[...] = _adam_math(w_ref[...], g_ref[...], m_ref[...], v_ref[...])

    if r % 16 == 0:
        tr = _row_tile(r, n)
        steps, spec = r // tr, pl.BlockSpec((tr, n), lambda i: (i, 0))
    else:
        tc = 4 * LANES
        steps, spec = n // tc, pl.BlockSpec((r, tc), lambda j: (0, j))
    return pl.pallas_call(
        body, name=name, out_shape=(_sds((r, n), F32),) * 3, grid=(steps,),
        in_specs=[spec] * 4, out_specs=(spec,) * 3, compiler_params=_params("arbitrary"),
    )(w, g, m, v)


def _adamw_small(ws, gs, ms, vs):
    k = len(ws)

    def body(*refs):
        ins, outs = refs[:4 * k], refs[4 * k:]
        for j in range(k):
            d, nm, nv = _adam_math(ins[j][...], ins[k + j][...], ins[2 * k + j][...], ins[3 * k + j][...])
            outs[j][...] = d
            outs[k + j][...] = nm
            outs[2 * k + j][...] = nv

    shapes = tuple(_sds(w.shape, F32) for w in ws)
    res = pl.pallas_call(body, name="adamw_small", out_shape=shapes * 3,
                         compiler_params=pltpu.CompilerParams(vmem_limit_bytes=VMEM_LIMIT))(*ws, *gs, *ms, *vs)
    return res[:k], res[k:2 * k], res[2 * k:]


def _ada_mod(c_all, w_sh, b_sh):
    b, _ = c_all.shape
    n = w_sh.shape[1]

    def body(c_ref, w_ref, b_ref, o_ref):
        cc = c_ref[...]
        ca = (cc * jax.nn.sigmoid(cc)).astype(BF16)
        o_ref[...] = _dot(ca, w_ref[...].astype(BF16)) + b_ref[...]

    return pl.pallas_call(body, name="ada_mod", out_shape=_sds((b, n), F32),
                          compiler_params=pltpu.CompilerParams(vmem_limit_bytes=VMEM_LIMIT))(c_all, w_sh, b_sh)


def _ada_bwd(c_all, dmod_all, dmod_sh, parts):
    b, d = c_all.shape
    n6 = dmod_all.shape[1]
    n = dmod_sh.shape[1]
    k = len(parts)

    def body(*refs):
        c_ref, da_ref, ds_ref = refs[:3]
        p_refs = refs[3:3 + k]
        dw_ref, db_ref = refs[3 + k:5 + k]
        s_refs = refs[5 + k:]
        cc = c_ref[...]
        ca = (cc * jax.nn.sigmoid(cc)).astype(BF16)
        dw_ref[...] = _dot_tn(ca, ds_ref[...].astype(BF16))
        db_ref[...] = jnp.sum(da_ref[...], axis=0, keepdims=True)
        for p_ref, s_ref in zip(p_refs, s_refs):
            tot = p_ref[0]
            for j in range(1, p_ref.shape[0]):
                tot = tot + p_ref[j]
            s_ref[...] = tot

    return pl.pallas_call(
        body, name="ada_bwd",
        out_shape=(_sds((d, n), F32), _sds((1, n6), F32)) + tuple(_sds(p.shape[1:], F32) for p in parts),
        compiler_params=pltpu.CompilerParams(vmem_limit_bytes=VMEM_LIMIT),
    )(c_all, dmod_all, dmod_sh, *parts)


def _fwd_in(x, g1, mod3, win_p, tm, tps):
    t, d = x.shape
    p_glu, p_q, npad = _layout(d)

    def body(x_ref, g_ref, mod_ref, w_hbm, h_ref, zm_ref, zglu_ref, zgate_ref, u0_ref, w_ref):
        _load_resident(pl.program_id(0), [(w_hbm, w_ref)])
        n, _ = _rms(x_ref[...])
        h = ((n * g_ref[...]) * (1.0 + mod_ref[1:2, :]) + mod_ref[0:1, :]).astype(BF16)
        h_ref[...] = h
        z = _dot(h, w_ref[...])
        zgate_ref[...] = z[:, :p_glu]
        zglu = z[:, p_glu:p_q]
        zglu_ref[...] = zglu
        zm_ref[...] = z[:, p_q:]
        u0_ref[...] = zglu[:, :CONV_CH] * jax.nn.sigmoid(zglu[:, CONV_CH:])

    return pl.pallas_call(
        body, name="fwd_in", grid=(t // tm,),
        out_shape=(_sds((t, d), BF16), _sds((t, MLA_IN), F32), _sds((t, 2 * CONV_CH), F32), _sds((t, 2 * d), F32),
                   _sds((t, CONV_CH), F32)),
        in_specs=[_row(tm, d), _full((1, d)), _modspec(d, tps), ANY],
        out_specs=(_row(tm, d), _row(tm, MLA_IN), _row(tm, 2 * CONV_CH), _row(tm, 2 * d), _row(tm, CONV_CH)),
        scratch_shapes=[pltpu.VMEM(win_p.shape, BF16)],
        compiler_params=_params("arbitrary"),
    )(x, g1, mod3, win_p)


def _mla_prep(zm, gql, gkvl, gq, gk, tabs, wuq_p, wk_p, wv_p, tm, tps):
    t = zm.shape[0]
    c_t, s1_t, s2_t = tabs
    tab = pl.BlockSpec((tm, LANES), lambda i: (i % tps, 0))

    def body(zm_ref, gql_ref, gkvl_ref, gq_ref, gk_ref, c_ref, s1_ref, s2_ref, wuq_ref, wk_ref, wv_ref,
             q_ref, k_ref, v_ref, qln_ref, kvn_ref):
        c, s1, s2 = c_ref[...], s1_ref[...], s2_ref[...]
        nq, _ = _rms(zm_ref[:, :Q_RANK])
        qln = (nq * gql_ref[...]).astype(BF16)
        qln_ref[...] = qln
        qpre = _dot(qln, wuq_ref[...])
        nkv, _ = _rms(zm_ref[:, Q_RANK:OFF_KV])
        kvn = (nkv * gkvl_ref[...]).astype(BF16)
        kvn_ref[...] = kvn
        knope = _dot(kvn, wk_ref[...])
        v_ref[...] = _dot(kvn, wv_ref[...]).astype(BF16)
        zkr_v = zm_ref[:, OFF_KV:]
        kr_roped = _rope(zkr_v * gk_ref[...], c, s1, s2)
        slabs = [slice(hd * LANES, (hd + 1) * LANES) for hd in range(N_HEADS)]
        rq = [_head_rms(qpre[:, sl])[1] for sl in slabs]
        rk = [_head_rms(knope[:, sl] + zkr_v)[1] for sl in slabs]
        for hd, sl in enumerate(slabs):
            q_ref[:, sl] = _rope((qpre[:, sl] * rq[hd]) * gq_ref[...], c, s1, s2).astype(BF16)
            k_ref[:, sl] = (rk[hd] * (knope[:, sl] * gk_ref[...] + kr_roped)).astype(BF16)

    return pl.pallas_call(
        body, name="mla_prep", grid=(t // tm,),
        out_shape=(_sds((t, HW), BF16),) * 3 + (_sds((t, Q_RANK), BF16), _sds((t, KV_RANK), BF16)),
        in_specs=[_row(tm, MLA_IN), _full((1, Q_RANK)), _full((1, KV_RANK)),
                  _full((1, LANES)), _full((1, LANES)), tab, tab, tab,
                  _full(wuq_p.shape), _full(wk_p.shape), _full(wv_p.shape)],
        out_specs=(_row(tm, HW),) * 3 + (_row(tm, Q_RANK), _row(tm, KV_RANK)),
        compiler_params=_params("arbitrary"),
    )(zm, gql, gkvl, gq, gk, c_t, s1_t, s2_t, wuq_p, wk_p, wv_p)


AHEAD = 2
ROW_BAND = 256
SM_SCALE = QK_HEAD ** -0.5
EXP2_SCALE = SM_SCALE * 1.4426950408889634


def _diag_mask():
    rc = jnp.right_shift(lax.broadcasted_iota(jnp.int32, (BQ, 1), 0), CHUNK_SHIFT)
    cc = jnp.right_shift(lax.broadcasted_iota(jnp.int32, (1, BQ), 1), CHUNK_SHIFT)
    return rc >= cc


def _scores(q_i, k_ref, lo, e):
    return (_dot_nt(q_i, k_ref[:lo, :]) if lo else None), _dot_nt(q_i, k_ref[lo:e, :])


def _softmax_parts(scores, mask):
    sp, sd = scores
    sd = jnp.where(mask, sd, jnp.finfo(F32).min)
    m = jnp.max(sd, axis=-1, keepdims=True)
    if sp is not None:
        m = jnp.maximum(m, jnp.max(sp, axis=-1, keepdims=True))
    pd = jnp.exp2((sd - m) * EXP2_SCALE)
    l = jnp.sum(pd, axis=-1, keepdims=True)
    pp = None
    if sp is not None:
        pp = jnp.exp2((sp - m) * EXP2_SCALE)
        l = l + jnp.sum(pp, axis=-1, keepdims=True)
    return pp, pd, l


def _attn_fwd(q, k, v, nseq, seq, gather=()):
    t = q.shape[0]
    na = len(gather)
    blk = pl.BlockSpec((seq, LANES), lambda b, h: (b, h))
    n_steps = nseq * N_HEADS

    def body(q_ref, k_ref, v_ref, *rest):
        o_ref = rest[na]
        if na:
            start, forward, finish = _gather8_phases(rest[:na], rest[na + 1:2 * na + 1], *rest[2 * na + 1:])
            step = pl.program_id(0) * N_HEADS + pl.program_id(1)
            pl.when(step == 0)(start)
            pl.when(step == (7 * n_steps) // 8)(forward)
        mask = _diag_mask()
        nb = seq // BQ
        block_scores = lambda j: _scores(q_ref[j * BQ:(j + 1) * BQ, :], k_ref, j * BQ, (j + 1) * BQ)
        ahead = [block_scores(j) for j in range(min(AHEAD, nb))]
        for i in range(nb):
            lo, e = i * BQ, (i + 1) * BQ
            cur = ahead.pop(0)
            if i + AHEAD < nb:
                ahead.append(block_scores(i + AHEAD))
            pp, pd, l = _softmax_parts(cur, mask)
            o = _dot(pd.astype(BF16), v_ref[lo:e, :])
            if lo:
                o = o + _dot(pp.astype(BF16), v_ref[:lo, :])
            o_ref[lo:e, :] = (o * (1.0 / l)).astype(BF16)
        if na:
            pl.when(step == n_steps - 1)(finish)

    res = pl.pallas_call(
        body, name="attn_fwd", grid=(nseq, N_HEADS), out_shape=(_sds((t, HW), BF16),) + _gather8_shapes(gather),
        in_specs=[blk, blk, blk] + [ANY] * na, out_specs=(blk,) + (ANY,) * na,
        scratch_shapes=_gather8_sems(na) if na else [],
        compiler_params=_params("arbitrary", "arbitrary"),
    )(q, k, v, *gather)
    return res[0], (_own_block_placed(res[1:], gather) if na else ())


def _fwd_mix(attn, u0, zgate, x, mod3, wo_p, cw, cb, lng, lnb, wpw, wout, tm, tps):
    t, d = x.shape
    hpt = tm // HALO
    cwc, cbc = _by_lane_chunk(cw), _by_lane_chunk(cb)

    def body(a_ref, u_ref, uh_ref, zg_ref, x_ref, mod_ref, wo_ref, cw_ref, cb_ref, lng_ref, lnb_ref, wpw_ref, wout_ref,
             x1_ref, mixed_ref, mpre_ref, ya_ref, yb_ref, u1_ref, u3_ref, ext_ref):
        i = pl.program_id(0)
        first = (i % tps) == 0
        _fill_shifted(ext_ref, jnp.where(first, 0.0, uh_ref[...]), u_ref[...])
        nb = max(tm // ROW_BAND, 1)
        bw = tm // nb
        bands = [slice(b * bw, (b + 1) * bw) for b in range(nb)]
        yas = [_dot(a_ref[rows, :], wo_ref[...]) for rows in bands]
        u3s = []
        for b, rows in enumerate(bands):
            ya_ref[rows, :] = yas[b]
            for lc, ls in _lane_chunks():
                acc = jnp.broadcast_to(cb_ref[lc], (bw, LANES))
                for kk in range(CONV_W):
                    o = HALO - (CONV_W - 1) + kk
                    a = (o // SUBLANES) * SUBLANES + b * bw
                    acc = acc + cw_ref[lc, kk:kk + 1, :] * ext_ref[o % SUBLANES, lc, a:a + bw, :]
                u1_ref[rows, ls] = acc
            acc = u1_ref[rows, :]
            mu = jnp.mean(acc, axis=-1, keepdims=True)
            xc = acc - mu
            rstd = lax.rsqrt(jnp.mean(xc * xc, axis=-1, keepdims=True) + EPS)
            l = (xc * rstd) * lng_ref[...] + lnb_ref[...]
            u3 = (l * jax.nn.sigmoid(l)).astype(BF16)
            u3_ref[rows, :] = u3
            u3s.append(u3)
        ybs = [_dot(u3, wpw_ref[...]) for u3 in u3s]
        mpres = []
        for b, rows in enumerate(bands):
            yb_ref[rows, :] = ybs[b]
            mpre = (jax.nn.sigmoid(zg_ref[rows, :d]) * yas[b] + jax.nn.sigmoid(zg_ref[rows, d:]) * ybs[b]).astype(BF16)
            mpre_ref[rows, :] = mpre
            mpres.append(mpre)
        for rows, mpre in zip(bands, mpres):
            mixed = _dot(mpre, wout_ref[...])
            mixed_ref[rows, :] = mixed
            x1_ref[rows, :] = x_ref[rows, :] + mod_ref[2:3, :] * mixed

    halo = pl.BlockSpec((HALO, CONV_CH), lambda i: (jnp.maximum(i * hpt - 1, 0), 0))
    return pl.pallas_call(
        body, name="fwd_mix", grid=(t // tm,),
        out_shape=(_sds((t, d), F32), _sds((t, d), F32), _sds((t, d), BF16), _sds((t, d), F32), _sds((t, d), F32),
                   _sds((t, CONV_CH), F32), _sds((t, CONV_CH), BF16)),
        in_specs=[_row(tm, HW), _row(tm, CONV_CH), halo, _row(tm, 2 * d), _row(tm, d), _modspec(d, tps),
                  _full(wo_p.shape), _full(cwc.shape), _full(cbc.shape), _full((1, CONV_CH)), _full((1, CONV_CH)),
                  _full(wpw.shape), _full(wout.shape)],
        out_specs=(_row(tm, d), _row(tm, d), _row(tm, d), _row(tm, d), _row(tm, d), _row(tm, CONV_CH),
                   _row(tm, CONV_CH)),
        scratch_shapes=[pltpu.VMEM(_shifted_shape(tm), F32)],
        compiler_params=_params("arbitrary"),
    )(attn, u0, u0, zgate, x, mod3, wo_p, cwc, cbc, lng, lnb, wpw, wout)


def _shards_into_columns(w_hbm, w_ref):
    ns = w_hbm.shape[2]
    return [(w_hbm.at[s], w_ref.at[:, pl.ds(s * ns, ns)]) for s in range(w_hbm.shape[0])]


def _fwd_ffn(x1, target, g2, mod3, w1, w2, tm, tps):
    t, d = x1.shape
    dff = w1.shape[0] * w1.shape[2]

    def body(x1_ref, tg_ref, g_ref, mod_ref, w1_hbm, w2_hbm,
             h2_ref, a_ref, r_ref, dy_ref, df_ref, dgate_ref, loss_ref, w1_ref, w2_ref):
        i = pl.program_id(0)
        _load_resident(i, _shards_into_columns(w1_hbm, w1_ref) + [(w2_hbm, w2_ref)])
        x1v = x1_ref[...]
        gate2 = mod_ref[5:6, :]
        n, _ = _rms(x1v)
        h2 = ((n * g_ref[...]) * (1.0 + mod_ref[4:5, :]) + mod_ref[3:4, :]).astype(BF16)
        h2_ref[...] = h2
        a = _dot(h2, w1_ref[...])
        a_ref[...] = a
        r = jnp.square(jnp.maximum(a, 0.0)).astype(BF16)
        r_ref[...] = r
        f = _dot(r, w2_ref[...])
        e = (x1v + gate2 * f) - tg_ref[...]
        part = 0.5 * jnp.sum(jnp.mean(e * e, axis=-1, keepdims=True), axis=0, keepdims=True)
        _acc(loss_ref, jnp.broadcast_to(part, loss_ref.shape), i == 0)
        dy = e * (1.0 / d)
        dy_ref[...] = dy
        df_ref[...] = (dy * gate2).astype(BF16)
        _acc(dgate_ref, jnp.sum(dy * f, axis=0, keepdims=True), (i % tps) == 0)

    nseq = t // (tm * tps)
    return pl.pallas_call(
        body, name="fwd_ffn", grid=(t // tm,),
        out_shape=(_sds((t, d), BF16), _sds((t, dff), F32), _sds((t, dff), BF16), _sds((t, d), F32), _sds((t, d), BF16),
                   _sds((nseq, 1, d), F32), _sds((8, LANES), F32)),
        in_specs=[_row(tm, d), _row(tm, d), _full((1, d)), _modspec(d, tps), ANY, ANY],
        out_specs=(_row(tm, d), _row(tm, dff), _row(tm, dff), _row(tm, d), _row(tm, d), _seqv(d, tps),
                   _full((8, LANES))),
        scratch_shapes=[pltpu.VMEM((d, dff), BF16), pltpu.VMEM(w2.shape, BF16)],
        compiler_params=_params("arbitrary"),
    )(x1, target, g2, mod3, w1, w2)


def _bwd_ffn(df, a, x1, dy, mixed, g2, mod3, w2, w1, tm, tps):
    t, d = x1.shape
    dff = a.shape[1]

    def body(df_ref, a_ref, x1_ref, dy_ref, mx_ref, g_ref, mod_ref, w2_hbm, w1_hbm,
             da_ref, dx1_ref, dmixed_ref, dshift_ref, dscale_ref, dgate1_ref, dg2_ref, w2_ref, w1_ref):
        i = pl.program_id(0)
        _load_resident(i, [(w2_hbm, w2_ref)] + _shards_into_columns(w1_hbm, w1_ref))
        first_seq = (i % tps) == 0
        dr = _dot_nt(df_ref[...], w2_ref[...])
        da = (dr * (2.0 * jnp.maximum(a_ref[...], 0.0))).astype(BF16)
        da_ref[...] = da
        dh2 = _dot_nt(da, w1_ref[...])
        n, r = _rms(x1_ref[...])
        g = g_ref[...]
        sc1 = 1.0 + mod_ref[4:5, :]
        _acc(dshift_ref, jnp.sum(dh2, axis=0, keepdims=True), first_seq)
        _acc(dscale_ref, jnp.sum(dh2 * (n * g), axis=0, keepdims=True), first_seq)
        _acc(dg2_ref, jnp.sum((dh2 * sc1) * n, axis=0, keepdims=True), i == 0)
        dx1 = dy_ref[...] + _rms_bwd(n, r, (dh2 * sc1) * g)
        dx1_ref[...] = dx1
        _acc(dgate1_ref, jnp.sum(dx1 * mx_ref[...], axis=0, keepdims=True), first_seq)
        dmixed_ref[...] = (dx1 * mod_ref[2:3, :]).astype(BF16)

    nseq = t // (tm * tps)
    sv = _sds((nseq, 1, d), F32)
    return pl.pallas_call(
        body, name="bwd_ffn", grid=(t // tm,),
        out_shape=(_sds((t, dff), BF16), _sds((t, d), F32), _sds((t, d), BF16), sv, sv, sv, _sds((1, d), F32)),
        in_specs=[_row(tm, d), _row(tm, dff), _row(tm, d), _row(tm, d), _row(tm, d), _full((1, d)), _modspec(d, tps),
                  ANY, ANY],
        out_specs=(_row(tm, dff), _row(tm, d), _row(tm, d), _seqv(d, tps), _seqv(d, tps), _seqv(d, tps),
                   _full((1, d))),
        scratch_shapes=[pltpu.VMEM(w2.shape, BF16), pltpu.VMEM((d, dff), BF16)],
        compiler_params=_params("arbitrary"),
    )(df, a, x1, dy, mixed, g2, mod3, w2, w1)


def _bwd_mix(dmixed, zgate, ya, yb, u1, lng, lnb, wout, wo_p, wpw, tm, swap=()):
    t, d = ya.shape
    _, _, npad = _layout(d)
    nw = len(swap)
    n_steps = t // tm

    def body(dm_ref, zg_ref, ya_ref, yb_ref, u1_ref, lng_ref, lnb_ref, wout_ref, wo_ref, wpw_ref, *rest):
        dya_ref, dyb_ref, dz_ref, do_ref, du1_ref, dlng_ref, dlnb_ref, dcb_ref = rest[nw:nw + 8]
        i = pl.program_id(0)
        if nw:
            start, finish = _swap_phases(rest[:nw], rest[nw + 8:2 * nw + 8], *rest[2 * nw + 8:])
            pl.when(i == 0)(start)
        nb = max(tm // ROW_BAND, 1)
        bands = [slice(b * (tm // nb), (b + 1) * (tm // nb)) for b in range(nb)]
        col = lambda v: jnp.sum(v, axis=0, keepdims=True)
        dmpre = [_dot_nt(dm_ref[rows, :], wout_ref[...]) for rows in bands]
        dyab = []
        for rows, dmp in zip(bands, dmpre):
            ga = jax.nn.sigmoid(zg_ref[rows, :d])
            gb = jax.nn.sigmoid(zg_ref[rows, d:])
            dya = (dmp * ga).astype(BF16)
            dyb = (dmp * gb).astype(BF16)
            dya_ref[rows, :] = dya
            dyb_ref[rows, :] = dyb
            dz_ref[rows, :d] = ((dmp * ya_ref[rows, :]) * (ga * (1.0 - ga))).astype(BF16)
            dz_ref[rows, d:] = ((dmp * yb_ref[rows, :]) * (gb * (1.0 - gb))).astype(BF16)
            dyab.append((dya, dyb))
        du3s = []
        for rows, (dya, dyb) in zip(bands, dyab):
            do_ref[rows, :] = _dot_nt(dya, wo_ref[...]).astype(BF16)
            du3s.append(_dot_nt(dyb, wpw_ref[...]))
        sums = [jnp.zeros((1, CONV_CH), F32)] * 3
        for rows, du3 in zip(bands, du3s):
            u1 = u1_ref[rows, :]
            mu = jnp.mean(u1, axis=-1, keepdims=True)
            xc = u1 - mu
            rstd = lax.rsqrt(jnp.mean(xc * xc, axis=-1, keepdims=True) + EPS)
            nh = xc * rstd
            l = nh * lng_ref[...] + lnb_ref[...]
            sg = jax.nn.sigmoid(l)
            dl = du3 * (sg * (1.0 + l * (1.0 - sg)))
            dnh = dl * lng_ref[...]
            du1 = rstd * (dnh - jnp.mean(dnh, axis=-1, keepdims=True)
                          - nh * jnp.mean(dnh * nh, axis=-1, keepdims=True))
            du1_ref[rows, :] = du1
            sums = [sums[0] + col(dl * nh), sums[1] + col(dl), sums[2] + col(du1)]
        _acc(dlng_ref, sums[0], i == 0)
        _acc(dlnb_ref, sums[1], i == 0)
        _acc(dcb_ref, sums[2], i == 0)
        if nw:
            pl.when(i == n_steps - 1)(finish)

    cv = _sds((1, CONV_CH), F32)
    res = pl.pallas_call(
        body, name="bwd_mix", grid=(n_steps,),
        out_shape=(_sds((t, d), BF16), _sds((t, d), BF16), _sds((t, npad), BF16), _sds((t, HW), BF16),
                   _sds((t, CONV_CH), F32), cv, cv, cv) + _swap_shapes(swap),
        in_specs=[_row(tm, d), _row(tm, 2 * d), _row(tm, d), _row(tm, d), _row(tm, CONV_CH), _full((1, CONV_CH)),
                  _full((1, CONV_CH)), _full(wout.shape), _full(wo_p.shape), _full(wpw.shape)] + [ANY] * nw,
        out_specs=(_row(tm, d), _row(tm, d), _row(tm, 2 * d), _row(tm, HW), _row(tm, CONV_CH),
                   _full((1, CONV_CH)), _full((1, CONV_CH)), _full((1, CONV_CH))) + (ANY,) * nw,
        scratch_shapes=_swap_sems(swap) if nw else [],
        compiler_params=_params("arbitrary"),
    )(dmixed, zgate, ya, yb, u1, lng, lnb, wout, wo_p, wpw, *swap)
    return res[:8] + (res[8:],)


def _bwd_conv(dz, du1, u0, zglu, cw, tm, tps):
    t = du1.shape[0]
    d = (dz.shape[1] - MLA_IN - 2 * CONV_CH) // 2
    p_glu, _, _ = _layout(d)
    hpt = tm // HALO
    last_blk = t // HALO - 1
    cwc = _by_lane_chunk(cw)

    def body(dz_hbm, du_ref, dun_ref, u_ref, zl_ref, cw_ref, dzl_ref, dcw_ref, dext_ref, uc_ref, dcw8_ref, du0_ref):
        i = pl.program_id(0)
        last = (i % tps) == (tps - 1)
        _fill_shifted(dext_ref, du_ref[...], jnp.where(last, 0.0, dun_ref[...]))
        for lc, ls in _lane_chunks():
            uc_ref[lc] = u_ref[:, ls]

        @pl.when(i == 0)
        def _():
            dcw8_ref[...] = jnp.zeros_like(dcw8_ref)

        groups = CONV_ROWS // SUBLANES

        def conv_chunk(c, carry):
            lc, r0 = _conv_chunk(c)
            u = uc_ref[lc, pl.ds(r0, CONV_ROWS), :]
            du0 = jnp.zeros((CONV_ROWS, LANES), F32)
            for kk in range(CONV_W):
                win = _shifted(dext_ref, CONV_W - 1 - kk, lc, r0)
                prod = u * win
                part = prod[:SUBLANES]
                for g in range(1, groups):
                    part = part + prod[g * SUBLANES:(g + 1) * SUBLANES]
                dcw8_ref[lc, kk] += part
                du0 = du0 + cw_ref[lc, kk:kk + 1, :] * win
            du0_ref[lc, pl.ds(r0, CONV_ROWS), :] = du0
            return carry

        lax.fori_loop(0, CONV_LC * (tm // CONV_ROWS), conv_chunk, 0)

        @pl.when(i == pl.num_programs(0) - 1)
        def _():
            for lc, ls in _lane_chunks():
                dcw_ref[:, ls] = jnp.sum(dcw8_ref[lc], axis=1)

        for lc, ls in _lane_chunks():
            du0 = du0_ref[lc]
            ga = zl_ref[:, ls]
            sb = jax.nn.sigmoid(zl_ref[:, CONV_CH + lc * LANES:CONV_CH + (lc + 1) * LANES])
            dzl_ref[:, ls] = (du0 * sb).astype(BF16)
            dzl_ref[:, CONV_CH + lc * LANES:CONV_CH + (lc + 1) * LANES] = ((du0 * ga) * (sb * (1.0 - sb))).astype(BF16)

    nxt = pl.BlockSpec((HALO, CONV_CH), lambda i: (jnp.minimum((i + 1) * hpt, last_blk), 0))
    glu_blk = p_glu // (2 * CONV_CH)
    return pl.pallas_call(
        body, name="bwd_conv", grid=(t // tm,),
        out_shape=(_sds(dz.shape, BF16), _sds(cw.shape, F32)),
        in_specs=[ANY, _row(tm, CONV_CH), nxt, _row(tm, CONV_CH), _row(tm, 2 * CONV_CH), _full(cwc.shape)],
        out_specs=(pl.BlockSpec((tm, 2 * CONV_CH), lambda i: (i, glu_blk)), _full(cw.shape)),
        scratch_shapes=[pltpu.VMEM(_shifted_shape(tm), F32), pltpu.VMEM((CONV_LC, tm, LANES), F32),
                        pltpu.VMEM((CONV_LC, HALO, SUBLANES, LANES), F32), pltpu.VMEM((CONV_LC, tm, LANES), F32)],
        input_output_aliases={0: 0},
        compiler_params=_params("arbitrary"),
    )(dz, du1, du1, u0, zglu, cwc)


def _attn_bwd(q, k, v, do, nseq, seq, scatter=()):
    t = q.shape[0]
    ns = len(scatter)
    blk = pl.BlockSpec((seq, LANES), lambda b, h: (b, h))
    n_steps = nseq * N_HEADS

    def body(q_ref, k_ref, v_ref, do_ref, *rest):
        dq_ref, dk_ref, dv_ref = rest[ns:ns + 3]
        dka_ref, dva_ref = rest[2 * ns + 3:2 * ns + 5]
        if ns:
            start, finish = _scatter_phases(rest[:ns], rest[ns + 3:2 * ns + 3], *rest[2 * ns + 5:])
            step = pl.program_id(0) * N_HEADS + pl.program_id(1)
            pl.when(step == 0)(start)
        dka_ref[...] = jnp.zeros_like(dka_ref)
        dva_ref[...] = jnp.zeros_like(dva_ref)
        mask = _diag_mask()
        nb = seq // BQ
        block = lambda j: (_scores(q_ref[j * BQ:(j + 1) * BQ, :], k_ref, j * BQ, (j + 1) * BQ),
                           _scores(do_ref[j * BQ:(j + 1) * BQ, :], v_ref, j * BQ, (j + 1) * BQ))
        ahead = [block(j) for j in range(min(AHEAD, nb))]

        def second_stage(lo, e, dsd, dsp, pdb, ppb):
            q_i = q_ref[lo:e, :]
            do_i = do_ref[lo:e, :]
            dq = _dot(dsd, k_ref[lo:e, :])
            dka_ref[lo:e, :] += _dot_tn(dsd, q_i)
            dva_ref[lo:e, :] += _dot_tn(pdb, do_i)
            if lo:
                dq = dq + _dot(dsp, k_ref[:lo, :])
                dka_ref[:lo, :] += _dot_tn(dsp, q_i)
                dva_ref[:lo, :] += _dot_tn(ppb, do_i)
            dq_ref[lo:e, :] = dq * SM_SCALE

        held = None
        for i in range(nb):
            lo, e = i * BQ, (i + 1) * BQ
            scores, (dpp, dpd) = ahead.pop(0)
            if i + AHEAD < nb:
                ahead.append(block(i + AHEAD))
            pp, pd, l = _softmax_parts(scores, mask)
            inv = 1.0 / l
            pd = pd * inv
            delta = jnp.sum(pd * dpd, axis=-1, keepdims=True)
            if lo:
                pp = pp * inv
                delta = delta + jnp.sum(pp * dpp, axis=-1, keepdims=True)
            dsd = (pd * (dpd - delta)).astype(BF16)
            dsp = (pp * (dpp - delta)).astype(BF16) if lo else None
            if held is not None:
                second_stage(*held)
            held = (lo, e, dsd, dsp, pd.astype(BF16), pp.astype(BF16) if lo else None)
        second_stage(*held)
        dk_ref[...] = dka_ref[...] * SM_SCALE
        dv_ref[...] = dva_ref[...].astype(BF16)
        if ns:
            pl.when(step == n_steps - 1)(finish)

    res = pl.pallas_call(
        body, name="attn_bwd", grid=(nseq, N_HEADS),
        out_shape=(_sds((t, HW), F32), _sds((t, HW), F32), _sds((t, HW), BF16)) + _scatter_shapes(scatter),
        in_specs=[blk] * 4 + [ANY] * ns, out_specs=(blk,) * 3 + (ANY,) * ns,
        scratch_shapes=[pltpu.VMEM((seq, LANES), F32), pltpu.VMEM((seq, LANES), F32)]
        + (_scatter_sems(ns) if ns else []),
        compiler_params=_params("arbitrary", "arbitrary"),
    )(q, k, v, do, *scatter)
    return res[0], res[1], res[2], res[3:]


def _mla_bwd(dz, dq, dk, dv, zm, gql, gkvl, gq, gk, tabs, wuq_p, wk_p, wv_p, tm, tps):
    t = zm.shape[0]
    d = (dz.shape[1] - MLA_IN - 2 * CONV_CH) // 2
    _, p_q, _ = _layout(d)
    c_t, s1_t, s2_t = tabs
    tab = pl.BlockSpec((tm, LANES), lambda i: (i % tps, 0))

    def body(dz_hbm, dq_ref, dk_ref, dv_ref, zm_ref, gql_ref, gkvl_ref, gq_ref, gk_ref, c_ref, s1_ref, s2_ref,
             wuq_ref, wk_ref, wv_ref,
             dzm_ref, dqpre_ref, dkh_ref, dgq_ref, dgk_ref, dgql_ref, dgkvl_ref):
        i = pl.program_id(0)
        c, s1, s2 = c_ref[...], s1_ref[...], s2_ref[...]
        nq, rq = _rms(zm_ref[:, :Q_RANK])
        qpre = _dot((nq * gql_ref[...]).astype(BF16), wuq_ref[...])
        nkv, rkv = _rms(zm_ref[:, Q_RANK:OFF_KV])
        knope = _dot((nkv * gkvl_ref[...]).astype(BF16), wk_ref[...])
        zkr_v = zm_ref[:, OFF_KV:]
        gk = gk_ref[...]
        kr_roped = _rope(zkr_v * gk, c, s1, s2)
        dgq = jnp.zeros((1, LANES), F32)
        dgk = jnp.zeros((1, LANES), F32)
        dzkr = jnp.zeros((tm, LANES), F32)
        dt_sum = jnp.zeros((tm, LANES), F32)
        slabs = [slice(hd * LANES, (hd + 1) * LANES) for hd in range(N_HEADS)]
        gq = gq_ref[...]
        rqh = [_head_rms(qpre[:, sl])[1] for sl in slabs]
        rkh = [_head_rms(knope[:, sl] + zkr_v)[1] for sl in slabs]
        dyr = [_rope_t(dq_ref[:, sl], c, s1, s2) for sl in slabs]
        nqh = [qpre[:, sl] * rqh[hd] for hd, sl in enumerate(slabs)]
        sq = [jnp.sum((dyr[hd] * gq) * nqh[hd], axis=-1, keepdims=True) for hd in range(N_HEADS)]
        dr = [jnp.sum(dk_ref[:, sl] * (knope[:, sl] * gk + kr_roped), axis=-1, keepdims=True) for sl in slabs]
        for hd, sl in enumerate(slabs):
            dgq = dgq + jnp.sum(dyr[hd] * nqh[hd], axis=0, keepdims=True)
            dqpre_ref[:, sl] = (rqh[hd] * (dyr[hd] * gq - nqh[hd] * (sq[hd] * (1.0 / QK_HEAD)))).astype(BF16)
            kn = knope[:, sl]
            r = rkh[hd]
            dt = dk_ref[:, sl] * r
            via_r = (dr[hd] * (r * r * r) * (-1.0 / QK_HEAD)) * (kn + zkr_v)
            dgk = dgk + jnp.sum(dt * kn, axis=0, keepdims=True)
            dt_sum = dt_sum + dt
            dzkr = dzkr + via_r
            dkh_ref[:, sl] = (dt * gk + via_r).astype(BF16)
        de = _rope_t(dt_sum, c, s1, s2)
        dzkr = dzkr + de * gk
        dgk = dgk + jnp.sum(de * zkr_v, axis=0, keepdims=True)
        _acc(dgq_ref, dgq[:, :QK_HEAD], i == 0)
        _acc(dgk_ref, dgk[:, :QK_HEAD], i == 0)
        dzm_ref[:, OFF_KV:] = dzkr.astype(BF16)
        dqln = _dot_nt(dqpre_ref[...], wuq_ref[...])
        _acc(dgql_ref, jnp.sum(dqln * nq, axis=0, keepdims=True), i == 0)
        dzm_ref[:, :Q_RANK] = _rms_bwd(nq, rq, dqln * gql_ref[...]).astype(BF16)
        dkvn = _dot_nt(dkh_ref[...], wk_ref[...]) + _dot_nt(dv_ref[...], wv_ref[...])
        _acc(dgkvl_ref, jnp.sum(dkvn * nkv, axis=0, keepdims=True), i == 0)
        dzm_ref[:, Q_RANK:OFF_KV] = _rms_bwd(nkv, rkv, dkvn * gkvl_ref[...]).astype(BF16)

    return pl.pallas_call(
        body, name="mla_bwd", grid=(t // tm,),
        out_shape=(_sds(dz.shape, BF16), _sds((t, HW), BF16), _sds((t, HW), BF16), _sds((1, QK_HEAD), F32),
                   _sds((1, QK_HEAD), F32), _sds((1, Q_RANK), F32), _sds((1, KV_RANK), F32)),
        in_specs=[ANY, _row(tm, HW), _row(tm, HW), _row(tm, HW), _row(tm, MLA_IN),
                  _full((1, Q_RANK)), _full((1, KV_RANK)), _full((1, LANES)), _full((1, LANES)), tab, tab, tab,
                  _full(wuq_p.shape), _full(wk_p.shape), _full(wv_p.shape)],
        out_specs=(pl.BlockSpec((tm, MLA_IN), lambda i: (i, p_q // MLA_IN)), _row(tm, HW), _row(tm, HW),
                   _full((1, QK_HEAD)), _full((1, QK_HEAD)), _full((1, Q_RANK)), _full((1, KV_RANK))),
        input_output_aliases={0: 0},
        compiler_params=_params("arbitrary"),
    )(dz, dq, dk, dv, zm, gql, gkvl, gq, gk, c_t, s1_t, s2_t, wuq_p, wk_p, wv_p)


def _bwd_in(dz, x, dx1, g1, mod3, win_p, tm, tps, scatter=()):
    t, d = x.shape
    npad = dz.shape[1]

    ns = len(scatter)
    n_steps = t // tm

    def body(dz_ref, x_ref, dx1_ref, g_ref, mod_ref, wt_hbm, *rest):
        gx_ref, dshift_ref, dscale_ref, dg1_ref = rest[ns:ns + 4]
        wt_ref = rest[2 * ns + 4]
        i = pl.program_id(0)
        if ns:
            start, finish = _scatter_phases(rest[:ns], rest[ns + 4:2 * ns + 4], *rest[2 * ns + 5:])
            pl.when(i == 0)(start)
        _load_resident(i, [(wt_hbm, wt_ref)])
        first_seq = (i % tps) == 0
        g = g_ref[...]
        sc1 = 1.0 + mod_ref[1:2, :]
        nb = max(tm // ROW_BAND, 1)
        bands = [slice(b * (tm // nb), (b + 1) * (tm // nb)) for b in range(nb)]
        dhs = [_dot_nt(dz_ref[rows, :], wt_ref[...]) for rows in bands]
        sums = [jnp.zeros((1, d), F32)] * 3
        col = lambda v: jnp.sum(v, axis=0, keepdims=True)
        for rows, dh in zip(bands, dhs):
            n, r = _rms(x_ref[rows, :])
            sums = [sums[0] + col(dh), sums[1] + col(dh * (n * g)), sums[2] + col((dh * sc1) * n)]
            gx_ref[rows, :] = dx1_ref[rows, :] + _rms_bwd(n, r, (dh * sc1) * g)
        _acc(dshift_ref, sums[0], first_seq)
        _acc(dscale_ref, sums[1], first_seq)
        _acc(dg1_ref, sums[2], i == 0)
        if ns:
            pl.when(i == n_steps - 1)(finish)

    nseq = t // (tm * tps)
    sv = _sds((nseq, 1, d), F32)
    res = pl.pallas_call(
        body, name="bwd_in", grid=(n_steps,),
        out_shape=(_sds((t, d), F32), sv, sv, _sds((1, d), F32)) + _scatter_shapes(scatter),
        in_specs=[_row(tm, npad), _row(tm, d), _row(tm, d), _full((1, d)), _modspec(d, tps), ANY] + [ANY] * ns,
        out_specs=(_row(tm, d), _seqv(d, tps), _seqv(d, tps), _full((1, d))) + (ANY,) * ns,
        scratch_shapes=[pltpu.VMEM(win_p.shape, BF16)] + (_scatter_sems(ns) if ns else []),
        compiler_params=_params("arbitrary"),
    )(dz, x, dx1, g1, mod3, win_p, *scatter)
    return res[0], res[1], res[2], res[3], res[4:]


def _tile_of(n, choices):
    for c in choices:
        if n % c == 0:
            return c
    return n


def _tn_matmul(a, b, name, col_shards=0):
    t, k = a.shape
    n = b.shape[1]
    tk = _tile_of(k, (1024, 512, 256, 128))
    tn = n // col_shards if col_shards else _tile_of(n, (1024, 896, 768, 512, 384, 256, 128))
    tt = _tile_of(t, (4096, 2048, 1024, 512, 256))

    def body(a_ref, b_ref, o_ref):
        _acc(o_ref, _dot_tn(a_ref[...], b_ref[...]), pl.program_id(2) == 0)

    if col_shards:
        out_shape, out_spec = _sds((col_shards, k, tn), F32), pl.BlockSpec((None, tk, tn), lambda i, j, s: (j, i, 0))
    else:
        out_shape, out_spec = _sds((k, n), F32), pl.BlockSpec((tk, tn), lambda i, j, s: (i, j))
    return pl.pallas_call(
        body, name=name, grid=(k // tk, n // tn, t // tt), out_shape=out_shape,
        in_specs=[pl.BlockSpec((tt, tk), lambda i, j, s: (s, i)), pl.BlockSpec((tt, tn), lambda i, j, s: (s, j))],
        out_specs=out_spec, compiler_params=_params("arbitrary", "arbitrary", "arbitrary"),
    )(a, b)


N_SHARD = 4
COL_SHARDED = ("w_in", "w_uq", "w_ukv", "w_o_mla", "w_pw_out", "w_ff1")
ROW_SHARDED = ("w_out", "w_ff2")
BIG = ("w_in", "w_uq", "w_ukv", "w_o_mla", "w_pw_out", "w_out", "w_ff1", "w_ff2")
SMALL = ("norm1_g", "q_latent_g", "kv_latent_g", "qk_norm_q_g", "qk_norm_k_g", "conv_b", "conv_ln_g", "conv_ln_b",
         "norm2_g")
WEIGHTS = ("w_ada", "b_ada", "norm1_g", "w_in", "q_latent_g", "w_uq", "kv_latent_g", "w_ukv", "qk_norm_q_g",
           "qk_norm_k_g", "w_o_mla", "conv_w", "conv_b", "conv_ln_g", "conv_ln_b", "w_pw_out", "w_out", "norm2_g",
           "w_ff1", "w_ff2")


def _pad_heads(w, width):
    k = w.shape[0]
    w3 = w.reshape(k, N_HEADS, width)
    return jnp.pad(w3, ((0, 0), (0, 0), (0, LANES - width))).reshape(k, HW)


def _unpad_heads(g, width):
    k = g.shape[0]
    return g.reshape(k, N_HEADS, LANES)[:, :, :width].reshape(k, N_HEADS * width)


def _win_segments(d):
    return [(OFF_GLU, OFF_GLU + 2 * d), (OFF_KR, OFF_GLU), (0, OFF_KV), KR_LANE, (OFF_KV, OFF_KR),
            LANES - KR_LANE - QK_ROPE]


def _pad_win(g4):
    _, d, ws = g4.shape
    parts = []
    for seg in _win_segments(d):
        if isinstance(seg, int):
            parts.append(jnp.zeros((d, seg), g4.dtype))
            continue
        a, b = seg
        while a < b:
            s = a // ws
            e = min(b, (s + 1) * ws)
            parts.append(g4[s, :, a - s * ws:e - s * ws])
            a = e
    return jnp.concatenate(parts, axis=1)


def _unpad_win(gp):
    d = gp.shape[0]
    ws = (OFF_GLU + 2 * d) // N_SHARD
    pieces, p = [], 0
    for seg in _win_segments(d):
        if isinstance(seg, int):
            p += seg
        else:
            pieces.append((seg[0], seg[1], p))
            p += seg[1] - seg[0]
    shards = []
    for s in range(N_SHARD):
        lo, hi = s * ws, (s + 1) * ws
        cols = [gp[:, p0 + max(a, lo) - a:p0 + min(b, hi) - a] for a, b, p0 in sorted(pieces) if max(a, lo) < min(b, hi)]
        shards.append(jnp.concatenate(cols, axis=1))
    return jnp.stack(shards)


def _col_shards(g):
    k, n = g.shape
    return g.reshape(k, N_SHARD, n // N_SHARD).transpose(1, 0, 2)


def _from_shards(g, name):
    ns, ks, nn = g.shape
    if name in ROW_SHARDED:
        return g.reshape(ns * ks, nn)
    return g.transpose(1, 0, 2).reshape(ks, ns * nn)


BY_SHARD = ("w_in", "w_ff1")
EARLY = ("w_in", "w_uq", "w_ukv")
LATE = ("w_o_mla", "w_pw_out", "w_out", "w_ff1", "w_ff2")


def _assemble(names, gathered):
    by_shard = {n: g.reshape((N_SHARD, 2 * g.shape[1]) + g.shape[2:]) for n, g in zip(names, gathered)}
    return {n: g if n in BY_SHARD else _from_shards(g, n) for n, g in by_shard.items()}


LARGE = ("w_in", "w_ff1", "w_ff2")
GROUP_A = ("w_out", "w_ff1", "w_ff2")
GROUP_B = ("w_in", "w_uq", "w_ukv", "w_o_mla", "w_pw_out")


def _pair_halves(g):
    return g.reshape(N_SHARD, 2, g.shape[1] // 2, g.shape[2])


def _pair_sums(names, halves, from_sibling):
    if not halves:
        return []
    ix, iy, ic = _place()
    cidx = ic.reshape(1).astype(jnp.int32)
    c_own = jnp.stack([ic, 2 * ix + iy]).astype(jnp.int32)
    out = {n: _add_pair(g, l, c_own, "pair_sum_" + n)
           for n, g, l in zip(names, halves, from_sibling) if n in LARGE}
    small = [j for j, n in enumerate(names) if n not in LARGE]
    if small:
        res = _add_pair_whole([halves[j] for j in small], [from_sibling[j] for j in small], cidx,
                              "pair_sum_small_" + names[small[0]])
        out.update({names[j]: r for j, r in zip(small, res)})
    return [out[n] for n in names]


def _local_step(x, target, mod, sp, w, late=None, tm=256):
    comm = late is not None
    w = dict(w)
    nseq, seq, d = x.shape
    t = nseq * seq
    tps = seq // tm
    xf = x.reshape(t, d)
    tg = target.reshape(t, d)
    mod3 = mod.reshape(nseq, N_MOD, d)

    win_p = _pad_win(w["w_in"])
    wuq_p = _pad_heads(w["w_uq"], QK_HEAD)
    wkv3 = w["w_ukv"].reshape(KV_RANK, N_HEADS, QK_NOPE + V_HEAD)
    wk_p = _pad_heads(wkv3[:, :, :QK_NOPE].reshape(KV_RANK, -1), QK_NOPE)
    wv_p = _pad_heads(wkv3[:, :, QK_NOPE:].reshape(KV_RANK, -1), V_HEAD)
    cw = jnp.pad(w["conv_w"], ((0, HALO - CONV_W), (0, 0)))
    pad_g = lambda g: jnp.pad(g, ((0, 0), (0, LANES - QK_HEAD)))
    gq, gk = pad_g(sp["qk_norm_q_g"]), pad_g(sp["qk_norm_k_g"])
    tabs = _rope_tables(seq)

    tm_in, tps_in = (2 * tm, tps // 2) if tps % 2 == 0 else (tm, tps)
    h, zm, zglu, zgate, u0 = _fwd_in(xf, sp["norm1_g"], mod3, win_p, tm_in, tps_in)
    q, k, v, qln, kvn = _mla_prep(zm, sp["q_latent_g"], sp["kv_latent_g"], gq, gk, tabs, wuq_p, wk_p, wv_p, tm_in,
                                  tps_in)
    attn, gathered = _attn_fwd(q, k, v, nseq, seq, tuple(late) if comm else ())
    if comm:
        w.update(_assemble(LATE, gathered))
    wo_p = jnp.pad(w["w_o_mla"].reshape(N_HEADS, V_HEAD, d), ((0, 0), (0, LANES - V_HEAD), (0, 0))).reshape(HW, d)
    x1, mixed, mpre, ya, yb, u1, u3 = _fwd_mix(attn, u0, zgate, xf, mod3, wo_p, cw, sp["conv_b"], sp["conv_ln_g"],
                                               sp["conv_ln_b"], w["w_pw_out"], w["w_out"], tm_in, tps_in)
    h2, a, r, dy, df, dgate2, loss_acc = _fwd_ffn(x1, tg, sp["norm2_g"], mod3, w["w_ff1"], w["w_ff2"], tm, tps)
    da, dx1, dmixed, dshift2, dscale2, dgate1, dg2 = _bwd_ffn(df, a, x1, dy, mixed, sp["norm2_g"], mod3,
                                                              w["w_ff2"], w["w_ff1"], tm, tps)
    gw = {
        "w_out": _tn_matmul(mpre, dmixed, "dw_out").reshape(N_SHARD, d // N_SHARD, d),
        "w_ff1": _tn_matmul(h2, da, "dw_ff1", N_SHARD),
        "w_ff2": _tn_matmul(r, df, "dw_ff2").reshape(N_SHARD, -1, d),
    }
    halves_a = [_pair_halves(gw[n]) for n in GROUP_A] if comm else []
    dya, dyb, dz, do, du1, dlng, dlnb, dcb, from_sibling = _bwd_mix(
        dmixed, zgate, ya, yb, u1, sp["conv_ln_g"], sp["conv_ln_b"], w["w_out"], wo_p, w["w_pw_out"], tm_in, tuple(halves_a))
    pair_a = _pair_sums(GROUP_A, halves_a, from_sibling)
    dz, dcw = _bwd_conv(dz, du1, u0, zglu, cw, tm_in, tps_in)
    gw["conv_w"] = dcw
    dq, dk, dv, land_a = _attn_bwd(q, k, v, do, nseq, seq, tuple(p[1] for p in pair_a))
    dz, dqpre, dkh, dgq, dgk, dgql, dgkvl = _mla_bwd(dz, dq, dk, dv, zm, sp["q_latent_g"], sp["kv_latent_g"], gq, gk,
                                                      tabs, wuq_p, wk_p, wv_p, tm_in, tps_in)
    dwk_p = _tn_matmul(kvn, dkh, "dw_uk")
    dwv_p = _tn_matmul(kvn, dv, "dw_uv")
    dwkv = jnp.concatenate([dwk_p.reshape(KV_RANK, N_HEADS, LANES)[:, :, :QK_NOPE],
                            dwv_p.reshape(KV_RANK, N_HEADS, LANES)[:, :, :V_HEAD]], axis=2).reshape(KV_RANK, -1)
    dwo = _tn_matmul(attn, dya, "dw_o").reshape(N_HEADS, LANES, d)[:, :V_HEAD].reshape(MLA_WIDTH, d)
    gw["w_in"] = _unpad_win(_tn_matmul(h, dz, "dw_in"))
    gw["w_uq"] = _col_shards(_unpad_heads(_tn_matmul(qln, dqpre, "dw_uq"), QK_HEAD))
    gw["w_ukv"] = _col_shards(dwkv)
    gw["w_o_mla"] = _col_shards(dwo)
    gw["w_pw_out"] = _tn_matmul(u3, dyb, "dw_pw", N_SHARD)
    pair_b = []
    if comm:
        halves_b = [_pair_halves(gw[n]) for n in GROUP_B]
        pair_b = _pair_sums(GROUP_B, halves_b, _pair_swap(halves_b, "grad_pair_swap"))
    gx, dshift1, dscale1, dg1, land_b = _bwd_in(dz, xf, dx1, sp["norm1_g"], mod3, win_p, tm_in, tps_in,
                                                tuple(p[1] for p in pair_b))
    if comm:
        for n, p, l in zip(GROUP_A + GROUP_B, pair_a + pair_b, land_a + land_b):
            gw[n] = (p[0], l)
    gs = {
        "norm1_g": dg1, "q_latent_g": dgql, "kv_latent_g": dgkvl, "qk_norm_q_g": dgq, "qk_norm_k_g": dgk,
        "conv_b": dcb, "conv_ln_g": dlng, "conv_ln_b": dlnb, "norm2_g": dg2,
    }
    dmod = jnp.concatenate([dshift1, dscale1, dgate1, dshift2, dscale2, dgate2], axis=2).reshape(nseq, N_MOD * d)
    return loss_acc, gx.reshape(nseq, seq, d), dmod, gw, gs


def kernel(x, c, w_ada, b_ada, norm1_g, w_in, q_latent_g, w_uq, kv_latent_g, w_ukv, qk_norm_q_g, qk_norm_k_g, w_o_mla, conv_w, conv_b, conv_ln_g, conv_ln_b, w_pw_out, w_out, norm2_g, w_ff1, w_ff2, loss_target, m_w_ada, m_b_ada, m_norm1_g, m_w_in, m_q_latent_g, m_w_uq, m_kv_latent_g, m_w_ukv, m_qk_norm_q_g, m_qk_norm_k_g, m_w_o_mla, m_conv_w, m_conv_b, m_conv_ln_g, m_conv_ln_b, m_w_pw_out, m_w_out, m_norm2_g, m_w_ff1, m_w_ff2, v_w_ada, v_b_ada, v_norm1_g, v_w_in, v_q_latent_g, v_w_uq, v_kv_latent_g, v_w_ukv, v_qk_norm_q_g, v_qk_norm_k_g, v_w_o_mla, v_conv_w, v_conv_b, v_conv_ln_g, v_conv_ln_b, v_w_pw_out, v_w_out, v_norm2_g, v_w_ff1, v_w_ff2):
    given = dict(locals())
    wts = {n: given[n][0] for n in WEIGHTS}
    mom = {n: given["m_" + n][0] for n in WEIGHTS}
    var = {n: given["v_" + n][0] for n in WEIGHTS}
    vec = lambda a: a.reshape(1, -1)
    nseq, seq, d = x.shape
    ix, iy, ic = _place()
    shard = 2 * ix + iy

    half = lambda n: lax.dynamic_slice_in_dim(wts[n].astype(BF16), ic * (wts[n].shape[0] // 2), wts[n].shape[0] // 2,
                                              axis=0)
    gathered = _all_gather8([half(n) for n in EARLY] + [wts["conv_w"], c], "gather_weights")
    full = _assemble(EARLY, gathered)
    full["conv_w"] = _from_shards(gathered[-2][0::2], "conv_w")
    c_all = gathered[-1].reshape(8 * nseq, d)

    n_ada = wts["w_ada"].shape[1]
    b_sh = lax.dynamic_slice_in_dim(vec(wts["b_ada"]), shard * n_ada, n_ada, axis=1)
    mod_sh = _ada_mod(c_all, wts["w_ada"], b_sh)
    hb = 4 * nseq
    mod_blk = lax.dynamic_slice_in_dim(mod_sh, ic * hb, hb, axis=0)
    (mod_all,) = _all_gather8([mod_blk], "gather_mod")
    mod_mine = lax.dynamic_slice_in_dim(mod_all, (2 * iy + ic) * nseq, nseq, axis=1)
    mod = jnp.concatenate([lax.dynamic_index_in_dim(mod_mine, 2 * s + ix, axis=0, keepdims=False)
                           for s in range(N_SHARD)], axis=1)

    sp = {n: vec(wts[n]) for n in SMALL}
    loss_part, grad_x, dmod, gw, gs = _local_step(x, loss_target, mod, sp, full, [half(n) for n in LATE])

    own_c = jnp.stack([shard, ic]).astype(jnp.int32)
    mine_sum = {n: _add_chips(gw[n][0], gw[n][1], own_c, "chip_sum_" + n) for n in LARGE}
    few = tuple(n for n in BIG if n not in LARGE)
    mine_sum.update(zip(few, _add_chips_whole([gw[n][0] for n in few], [gw[n][1] for n in few], own_c, "chip_sum_small")))
    summed, parts = _pair_gather_and_all_gather8(
        [mine_sum[n] for n in BIG], [dmod, gw["conv_w"], loss_part] + [gs[n] for n in SMALL], "tail_exchange")

    dmod_all = parts[0].reshape(8 * nseq, N_MOD * d)
    dmod_sh = lax.dynamic_slice_in_dim(dmod_all, shard * n_ada, n_ada, axis=1)
    res = _ada_bwd(c_all, dmod_all, dmod_sh, parts[1:])
    grads = {"w_ada": res[0], "b_ada": res[1]}
    n_cw = wts["conv_w"].shape[1]
    grads["conv_w"] = lax.dynamic_slice_in_dim(res[2], shard * n_cw, n_cw, axis=1)[:CONV_W]
    loss = res[3][0, 0]
    for n, g in zip(SMALL, res[4:]):
        grads[n] = g
    for n, g in zip(BIG, summed):
        grads[n] = g.reshape(wts[n].shape)

    delta, new_m, new_v = {}, {}, {}
    for n in LARGE + ("w_ada",):
        if n == "w_in":
            res = _adamw(wts[n].T, grads[n].T, mom[n].T, var[n].T, "adamw_" + n)
            delta[n], new_m[n], new_v[n] = (a.T for a in res)
        else:
            delta[n], new_m[n], new_v[n] = _adamw(wts[n], grads[n], mom[n], var[n], "adamw_" + n)
    rest = ("b_ada", "conv_w") + SMALL + few
    as2d = lambda a: a if a.ndim == 2 else vec(a)
    res = _adamw_small(*[[as2d(t[n]) for n in rest] for t in (wts, grads, mom, var)])
    for dst, arrs in zip((delta, new_m, new_v), res):
        for n, a in zip(rest, arrs):
            dst[n] = a

    outs = [loss, grad_x]
    for group in (grads, delta, new_m, new_v):
        outs += [group[n].reshape(given[n].shape) for n in WEIGHTS]
    return tuple(outs)
```

```python
import jax
import jax.numpy as jnp
from jax import lax
from jax.experimental import pallas as pl
from jax.experimental.pallas import tpu as pltpu

F32 = jnp.float32
BF16 = jnp.bfloat16
MESH = pl.DeviceIdType.MESH
ANY = pl.BlockSpec(memory_space=pl.ANY)

CHUNK = 64
CHUNK_SHIFT = 6
N_HEADS = 8
QK_NOPE = 64
QK_ROPE = 32
QK_HEAD = QK_NOPE + QK_ROPE
V_HEAD = 64
Q_RANK = 256
KV_RANK = 128
MLA_WIDTH = N_HEADS * V_HEAD
CONV_CH = 512
CONV_W = 31
ROPE_THETA = 10000.0
EPS = 1e-6
LANES = 128
SUBLANES = 8
HW = N_HEADS * LANES
OFF_KV = Q_RANK + KV_RANK
OFF_KR = OFF_KV + QK_ROPE
OFF_GLU = OFF_KR + 2 * CONV_CH
KR_LANE = QK_NOPE
MLA_IN = Q_RANK + KV_RANK + LANES
HALO = 32
N_MOD = 6

ADAM_LR = 0.001
ADAM_B1 = 0.9
ADAM_B2 = 0.999
ADAM_EPS = 1e-08
ADAM_WD = 0.01
ADAM_STEP = 10

VMEM_LIMIT = 56 * 1024 * 1024
BQ = 256


def _layout(d):
    p_glu = 2 * d
    p_q = p_glu + 2 * CONV_CH
    return p_glu, p_q, p_q + MLA_IN


def _params(*sem):
    return pltpu.CompilerParams(dimension_semantics=sem, vmem_limit_bytes=VMEM_LIMIT)


def _dot(a, b):
    return jnp.dot(a, b, preferred_element_type=F32)


def _dot_tn(a, b):
    return lax.dot_general(a, b, (((0,), (0,)), ((), ())), preferred_element_type=F32)


def _dot_nt(a, b):
    return lax.dot_general(a, b, (((1,), (1,)), ((), ())), preferred_element_type=F32)


def _acc(ref, val, first):
    @pl.when(first)
    def _():
        ref[...] = val

    @pl.when(jnp.logical_not(first))
    def _():
        ref[...] += val


def _rms(x):
    r = lax.rsqrt(jnp.mean(x * x, axis=-1, keepdims=True) + EPS)
    return x * r, r


def _rms_bwd(n, r, dn):
    return r * (dn - n * jnp.mean(dn * n, axis=-1, keepdims=True))


def _head_rms(sl):
    r = lax.rsqrt(jnp.sum(sl * sl, axis=-1, keepdims=True) * (1.0 / QK_HEAD) + EPS)
    return sl * r, r


def _head_rms_bwd(n, r, dn):
    return r * (dn - n * (jnp.sum(dn * n, axis=-1, keepdims=True) * (1.0 / QK_HEAD)))


def _rope(x, c, s1, s2):
    return x * c + pltpu.roll(x, QK_ROPE // 2, 1) * s1 + pltpu.roll(x, LANES - QK_ROPE // 2, 1) * s2


def _rope_t(dy, c, s1, s2):
    return dy * c + pltpu.roll(dy * s1, LANES - QK_ROPE // 2, 1) + pltpu.roll(dy * s2, QK_ROPE // 2, 1)


def _rope_tables(seq):
    half = QK_ROPE // 2
    inv_freq = ROPE_THETA ** (-jnp.arange(0, QK_ROPE, 2, dtype=F32) / QK_ROPE)
    ang = jnp.arange(seq, dtype=F32)[:, None] * inv_freq[None, :]
    cos, sin = jnp.cos(ang), jnp.sin(ang)
    z = lambda n: jnp.zeros((seq, n), F32)
    tail = LANES - QK_HEAD
    c = jnp.concatenate([jnp.ones((seq, QK_NOPE), F32), cos, cos, jnp.ones((seq, tail), F32)], axis=1)
    s1 = jnp.concatenate([z(QK_NOPE + half), sin, z(tail)], axis=1)
    s2 = jnp.concatenate([z(QK_NOPE), -sin, z(half + tail)], axis=1)
    return c, s1, s2


def _row(tm, w):
    return pl.BlockSpec((tm, w), lambda i: (i, 0))


def _modspec(d, tps):
    return pl.BlockSpec((None, N_MOD, d), lambda i: (i // tps, 0, 0))


def _seqv(w, tps):
    return pl.BlockSpec((None, 1, w), lambda i: (i // tps, 0, 0))


def _full(shape):
    return pl.BlockSpec(shape, lambda i: tuple(0 for _ in shape))


def _sds(shape, dtype):
    return jax.ShapeDtypeStruct(shape, dtype)


CONV_ROWS = 64
CONV_LC = CONV_CH // LANES


def _lane_chunks():
    return [(lc, slice(lc * LANES, (lc + 1) * LANES)) for lc in range(CONV_LC)]


def _fill_shifted(ext_ref, head, body):
    nh = head.shape[0]
    for lc, ls in _lane_chunks():
        ext_ref[0, lc, :nh, :] = head[:, ls]
        ext_ref[0, lc, nh:, :] = body[:, ls]
        rows = ext_ref[0, lc]
        for b in range(1, SUBLANES):
            ext_ref[b, lc] = pltpu.roll(rows, rows.shape[0] - b, 0)


def _shifted_shape(tm):
    return (SUBLANES, CONV_LC, tm + HALO, LANES)


def _conv_chunk(c):
    return c % CONV_LC, pl.multiple_of((c // CONV_LC) * CONV_ROWS, CONV_ROWS)


def _shifted(ext_ref, o, lc, r0):
    a = pl.multiple_of((o // SUBLANES) * SUBLANES + r0, SUBLANES)
    return ext_ref[o % SUBLANES, lc, pl.ds(a, CONV_ROWS), :]


def _by_lane_chunk(a):
    return a.reshape(a.shape[0], CONV_LC, LANES).transpose(1, 0, 2)


def _load_resident(i, pairs):
    @pl.when(i == 0)
    def _():
        for src, dst in pairs:
            pltpu.sync_copy(src, dst)


def _place():
    return lax.axis_index("x"), lax.axis_index("y"), lax.axis_index("c")


def _all_gather8(blocks, name):
    na = len(blocks)

    def body(*refs):
        start, forward, finish = _gather8_phases(refs[:na], refs[na:2 * na], *refs[2 * na:])
        start()
        forward()
        finish()

    outs = pl.pallas_call(
        body, name=name, out_shape=_gather8_shapes(blocks), in_specs=[ANY] * na, out_specs=(ANY,) * na,
        scratch_shapes=_gather8_sems(na),
    )(*blocks)
    return _own_block_placed(outs, blocks)


def _gather8_shapes(blocks):
    return tuple(_sds((8,) + b.shape, b.dtype) for b in blocks)


def _gather8_sems(na):
    return [pltpu.SemaphoreType.DMA((7 * na,)), pltpu.SemaphoreType.DMA((7 * na,))]


def _own_block_placed(outs, blocks):
    ix, iy, ic = _place()
    return tuple(lax.dynamic_update_index_in_dim(o, b, 4 * ix + 2 * iy + ic, 0) for o, b in zip(outs, blocks))


def _gather8_phases(x_refs, out_refs, send_sems, recv_sems):
    na = len(x_refs)
    x, y, c = _place()
    me, sibling = (x, y, c), (x, y, 1 - c)
    chips = [(1 - x, y), (x, 1 - y), (1 - x, 1 - y)]

    def copy(a, k, blk, to, from_input=False):
        dst = out_refs[a].at[4 * blk[0] + 2 * blk[1] + blk[2]]
        return pltpu.make_async_remote_copy(
            src_ref=x_refs[a] if from_input else dst, dst_ref=dst,
            send_sem=send_sems.at[7 * a + k], recv_sem=recv_sems.at[7 * a + k], device_id=to, device_id_type=MESH)

    def first(a):
        return [copy(a, 0, me, sibling, True)] + [copy(a, 1 + j, me, (*chip, c), True) for j, chip in enumerate(chips)]

    def start():
        for a in range(na):
            for cp in first(a):
                cp.start()

    def forward():
        for j, chip in enumerate(chips):
            for a in range(na):
                copy(a, 1 + j, (*chip, c), me).wait_recv()
                copy(a, 4 + j, (*chip, c), sibling).start()

    def finish():
        for a in range(na):
            copy(a, 0, sibling, me).wait_recv()
            for j, chip in enumerate(chips):
                copy(a, 4 + j, (*chip, 1 - c), me).wait_recv()
        for a in range(na):
            for cp in first(a) + [copy(a, 4 + j, (*chip, c), sibling) for j, chip in enumerate(chips)]:
                cp.wait_send()

    return start, forward, finish


def _pair_swap(gs, name):
    na = len(gs)

    def body(*refs):
        start, finish = _swap_phases(refs[:na], refs[na:2 * na], *refs[2 * na:])
        start()
        finish()

    return pl.pallas_call(
        body, name=name, out_shape=_swap_shapes(gs), in_specs=[ANY] * na, out_specs=(ANY,) * na,
        scratch_shapes=_swap_sems(gs),
    )(*gs)


def _swap_shapes(gs):
    return tuple(_sds(g.shape[:1] + g.shape[2:], g.dtype) for g in gs)


def _swap_sems(gs):
    n = sum(g.shape[0] for g in gs)
    return [pltpu.SemaphoreType.DMA((n,)), pltpu.SemaphoreType.DMA((n,))]


def _swap_phases(g_refs, land_refs, send_sems, recv_sems):
    x, y, c = _place()

    def copies():
        cps, k = [], 0
        for g_ref, land_ref in zip(g_refs, land_refs):
            for s in range(g_ref.shape[0]):
                cps.append(pltpu.make_async_remote_copy(
                    src_ref=g_ref.at[s, 1 - c], dst_ref=land_ref.at[s], send_sem=send_sems.at[k],
                    recv_sem=recv_sems.at[k], device_id=(x, y, 1 - c), device_id_type=MESH))
                k += 1
        return cps

    def start():
        for cp in copies():
            cp.start()

    def finish():
        for cp in copies():
            cp.wait()

    return start, finish


def _scatter_shapes(hs):
    return tuple(_sds((3,) + h.shape[1:], h.dtype) for h in hs)


def _scatter_sems(na):
    return [pltpu.SemaphoreType.DMA((3 * na,)), pltpu.SemaphoreType.DMA((3 * na,))]


def _scatter_phases(h_refs, land_refs, send_sems, recv_sems):
    x, y, c = _place()
    chips = [(1 - x, y), (x, 1 - y), (1 - x, 1 - y)]

    def copies():
        return [pltpu.make_async_remote_copy(
            src_ref=h_refs[a].at[2 * tx + ty], dst_ref=land_refs[a].at[j], send_sem=send_sems.at[3 * a + j],
            recv_sem=recv_sems.at[3 * a + j], device_id=(tx, ty, c), device_id_type=MESH)
            for a in range(len(h_refs)) for j, (tx, ty) in enumerate(chips)]

    def start():
        for cp in copies():
            cp.start()

    def finish():
        for cp in copies():
            cp.wait()

    return start, finish


def _pair_gather_and_all_gather8(fs, blocks, name):
    nf, nb = len(fs), len(blocks)

    def body(*refs):
        f_refs = refs[nf + nb:2 * nf + nb]
        b_out = refs[2 * nf + nb:2 * nf + 2 * nb]
        send_sems, recv_sems, g_send, g_recv = refs[2 * nf + 2 * nb:]
        x, y, c = _place()
        start, forward, finish = _gather8_phases(refs[nf:nf + nb], b_out, g_send, g_recv)
        sends = [pltpu.make_async_remote_copy(
            src_ref=f_refs[a].at[c], dst_ref=f_refs[a].at[c], send_sem=send_sems.at[a], recv_sem=recv_sems.at[a],
            device_id=(x, y, 1 - c), device_id_type=MESH) for a in range(nf)]
        recvs = [pltpu.make_async_remote_copy(
            src_ref=f_refs[a].at[c], dst_ref=f_refs[a].at[1 - c], send_sem=send_sems.at[a],
            recv_sem=recv_sems.at[a], device_id=(x, y, 1 - c), device_id_type=MESH) for a in range(nf)]
        start()
        for cp in sends:
            cp.start()
        forward()
        finish()
        for cp in recvs:
            cp.wait_recv()
        for cp in sends:
            cp.wait_send()

    res = pl.pallas_call(
        body, name=name, out_shape=tuple(_sds(f.shape, f.dtype) for f in fs) + _gather8_shapes(blocks),
        in_specs=[ANY] * (nf + nb), out_specs=(ANY,) * (nf + nb), input_output_aliases={a: a for a in range(nf)},
        scratch_shapes=[pltpu.SemaphoreType.DMA((nf,)), pltpu.SemaphoreType.DMA((nf,))] + _gather8_sems(nb),
    )(*fs, *blocks)
    return res[:nf], _own_block_placed(res[nf:], blocks)


def _row_tile(r, n, itemsize=4, budget=1 << 21):
    if r * n * itemsize <= budget:
        return r
    best = None
    for tr in range(16, r, 16):
        if r % tr == 0 and tr * n * itemsize <= budget:
            best = tr
    assert best is not None, (r, n)
    return best


def _add_pair(g, land, c_own, name):
    ns, _, r, n = g.shape
    tr = _row_tile(r, n)

    def body(co_ref, a_ref, b_ref, o_ref, ob_ref):
        s = a_ref[...] + b_ref[...].astype(F32)
        ob_ref[...] = s.astype(BF16)

        @pl.when(pl.program_id(1) == co_ref[1])
        def _():
            o_ref[...] = s

    per_shard = pl.BlockSpec((None, tr, n), lambda i, s, co: (s, i, 0))
    return pl.pallas_call(
        body, name=name, out_shape=(_sds((r, n), F32), _sds((ns, r, n), BF16)),
        grid_spec=pltpu.PrefetchScalarGridSpec(
            num_scalar_prefetch=1, grid=(r // tr, ns),
            in_specs=[pl.BlockSpec((None, None, tr, n), lambda i, s, co: (s, co[0], i, 0)), per_shard],
            out_specs=(pl.BlockSpec((tr, n), lambda i, s, co: (i, 0)), per_shard)),
        compiler_params=_params("arbitrary", "arbitrary"),
    )(c_own, g, land)


def _add_pair_whole(gs, lands, cidx, name):
    k = len(gs)

    def body(c_ref, *refs):
        for a_ref, b_ref, o_ref, ob_ref in zip(refs[:k], refs[k:2 * k], refs[2 * k:3 * k], refs[3 * k:]):
            s = a_ref[...] + b_ref[...].astype(F32)
            o_ref[...] = s
            ob_ref[...] = s.astype(BF16)

    half = lambda g: pl.BlockSpec((g.shape[0], None) + g.shape[2:], lambda i, cr: (0, cr[0], 0, 0))
    whole = lambda g: pl.BlockSpec(g.shape[:1] + g.shape[2:], lambda i, cr: (0, 0, 0))
    shapes = lambda dt: tuple(_sds(g.shape[:1] + g.shape[2:], dt) for g in gs)
    res = pl.pallas_call(
        body, name=name, out_shape=shapes(F32) + shapes(BF16),
        grid_spec=pltpu.PrefetchScalarGridSpec(
            num_scalar_prefetch=1, grid=(1,),
            in_specs=[half(g) for g in gs] + [whole(g) for g in gs],
            out_specs=tuple(whole(g) for g in gs) * 2),
        compiler_params=_params("arbitrary"),
    )(cidx, *gs, *lands)
    return list(zip(res[:k], res[k:]))


def _add_chips_whole(hs, lands, own_c, name):
    k = len(hs)

    def body(o_idx, *refs):
        for h_ref, l_ref, o_ref in zip(refs[:k], refs[k:2 * k], refs[2 * k:]):
            o_ref[...] = ((h_ref[...] + l_ref[0].astype(F32)) + l_ref[1].astype(F32)) + l_ref[2].astype(F32)

    return pl.pallas_call(
        body, name=name, out_shape=tuple(_sds((2,) + h.shape[1:], F32) for h in hs),
        grid_spec=pltpu.PrefetchScalarGridSpec(
            num_scalar_prefetch=1, grid=(1,),
            in_specs=[pl.BlockSpec((None,) + h.shape[1:], lambda i, o: (o[0], 0, 0)) for h in hs]
            + [pl.BlockSpec(l.shape, lambda i, o: (0, 0, 0)) for l in lands],
            out_specs=tuple(pl.BlockSpec((None,) + h.shape[1:], lambda i, o: (o[1], 0, 0)) for h in hs)),
        compiler_params=_params("arbitrary"),
    )(own_c, *hs, *lands)


def _add_chips(h, land, own_c, name):
    r, n = h.shape
    tr = _row_tile(r, n)

    def body(o_idx, h_ref, l_ref, o_ref):
        o_ref[...] = ((h_ref[...] + l_ref[0].astype(F32)) + l_ref[1].astype(F32)) + l_ref[2].astype(F32)

    return pl.pallas_call(
        body, name=name, out_shape=_sds((2, r, n), F32),
        grid_spec=pltpu.PrefetchScalarGridSpec(
            num_scalar_prefetch=1, grid=(r // tr,),
            in_specs=[pl.BlockSpec((tr, n), lambda i, o: (i, 0)),
                      pl.BlockSpec((3, tr, n), lambda i, o: (0, i, 0))],
            out_specs=pl.BlockSpec((None, tr, n), lambda i, o: (o[1], i, 0))),
        compiler_params=_params("arbitrary"),
    )(own_c, h, land)


def _adam_math(w, g, m, v):
    nm = ADAM_B1 * m + (1.0 - ADAM_B1) * g
    nv = ADAM_B2 * v + (1.0 - ADAM_B2) * (g * g)
    m_hat = nm / (1.0 - ADAM_B1 ** ADAM_STEP)
    v_hat = nv / (1.0 - ADAM_B2 ** ADAM_STEP)
    return -ADAM_LR * (m_hat / (jnp.sqrt(v_hat) + ADAM_EPS) + ADAM_WD * w), nm, nv


def _adamw(w, g, m, v, name):
    r, n = w.shape

    def body(w_ref, g_ref, m_ref, v_ref, d_ref, nm_ref, nv_ref):
        d_ref[...], nm_ref[...], nv_ref[...] = _adam_math(w_ref[...], g_ref[...], m_ref[...], v_ref[...])

    if r % 16 == 0:
        tr = _row_tile(r, n)
        steps, spec = r // tr, pl.BlockSpec((tr, n), lambda i: (i, 0))
    else:
        tc = 4 * LANES
        steps, spec = n // tc, pl.BlockSpec((r, tc), lambda j: (0, j))
    return pl.pallas_call(
        body, name=name, out_shape=(_sds((r, n), F32),) * 3, grid=(steps,),
        in_specs=[spec] * 4, out_specs=(spec,) * 3, compiler_params=_params("arbitrary"),
    )(w, g, m, v)


def _adamw_small(ws, gs, ms, vs):
    k = len(ws)

    def body(*refs):
        ins, outs = refs[:4 * k], refs[4 * k:]
        for j in range(k):
            d, nm, nv = _adam_math(ins[j][...], ins[k + j][...], ins[2 * k + j][...], ins[3 * k + j][...])
            outs[j][...] = d
            outs[k + j][...] = nm
            outs[2 * k + j][...] = nv

    shapes = tuple(_sds(w.shape, F32) for w in ws)
    res = pl.pallas_call(body, name="adamw_small", out_shape=shapes * 3,
                         compiler_params=pltpu.CompilerParams(vmem_limit_bytes=VMEM_LIMIT))(*ws, *gs, *ms, *vs)
    return res[:k], res[k:2 * k], res[2 * k:]


def _ada_mod(c_all, w_sh, b_sh):
    b, _ = c_all.shape
    n = w_sh.shape[1]

    def body(c_ref, w_ref, b_ref, o_ref):
        cc = c_ref[...]
        ca = (cc * jax.nn.sigmoid(cc)).astype(BF16)
        o_ref[...] = _dot(ca, w_ref[...].astype(BF16)) + b_ref[...]

    return pl.pallas_call(body, name="ada_mod", out_shape=_sds((b, n), F32),
                          compiler_params=pltpu.CompilerParams(vmem_limit_bytes=VMEM_LIMIT))(c_all, w_sh, b_sh)


def _ada_bwd(c_all, dmod_all, dmod_sh, parts):
    b, d = c_all.shape
    n6 = dmod_all.shape[1]
    n = dmod_sh.shape[1]
    k = len(parts)

    def body(*refs):
        c_ref, da_ref, ds_ref = refs[:3]
        p_refs = refs[3:3 + k]
        dw_ref, db_ref = refs[3 + k:5 + k]
        s_refs = refs[5 + k:]
        cc = c_ref[...]
        ca = (cc * jax.nn.sigmoid(cc)).astype(BF16)
        dw_ref[...] = _dot_tn(ca, ds_ref[...].astype(BF16))
        db_ref[...] = jnp.sum(da_ref[...], axis=0, keepdims=True)
        for p_ref, s_ref in zip(p_refs, s_refs):
            tot = p_ref[0]
            for j in range(1, p_ref.shape[0]):
                tot = tot + p_ref[j]
            s_ref[...] = tot

    return pl.pallas_call(
        body, name="ada_bwd",
        out_shape=(_sds((d, n), F32), _sds((1, n6), F32)) + tuple(_sds(p.shape[1:], F32) for p in parts),
        compiler_params=pltpu.CompilerParams(vmem_limit_bytes=VMEM_LIMIT),
    )(c_all, dmod_all, dmod_sh, *parts)


def _fwd_in(x, g1, mod3, win_p, tm, tps):
    t, d = x.shape
    p_glu, p_q, npad = _layout(d)

    def body(x_ref, g_ref, mod_ref, w_hbm, h_ref, zm_ref, zglu_ref, zgate_ref, u0_ref, w_ref):
        _load_resident(pl.program_id(0), [(w_hbm, w_ref)])
        n, _ = _rms(x_ref[...])
        h = ((n * g_ref[...]) * (1.0 + mod_ref[1:2, :]) + mod_ref[0:1, :]).astype(BF16)
        h_ref[...] = h
        z = _dot(h, w_ref[...])
        zgate_ref[...] = z[:, :p_glu]
        zglu = z[:, p_glu:p_q]
        zglu_ref[...] = zglu
        zm_ref[...] = z[:, p_q:]
        u0_ref[...] = zglu[:, :CONV_CH] * jax.nn.sigmoid(zglu[:, CONV_CH:])

    return pl.pallas_call(
        body, name="fwd_in", grid=(t // tm,),
        out_shape=(_sds((t, d), BF16), _sds((t, MLA_IN), F32), _sds((t, 2 * CONV_CH), F32), _sds((t, 2 * d), F32),
                   _sds((t, CONV_CH), F32)),
        in_specs=[_row(tm, d), _full((1, d)), _modspec(d, tps), ANY],
        out_specs=(_row(tm, d), _row(tm, MLA_IN), _row(tm, 2 * CONV_CH), _row(tm, 2 * d), _row(tm, CONV_CH)),
        scratch_shapes=[pltpu.VMEM(win_p.shape, BF16)],
        compiler_params=_params("arbitrary"),
    )(x, g1, mod3, win_p)


def _mla_prep(zm, gql, gkvl, gq, gk, tabs, wuq_p, wk_p, wv_p, tm, tps):
    t = zm.shape[0]
    c_t, s1_t, s2_t = tabs
    tab = pl.BlockSpec((tm, LANES), lambda i: (i % tps, 0))

    def body(zm_ref, gql_ref, gkvl_ref, gq_ref, gk_ref, c_ref, s1_ref, s2_ref, wuq_ref, wk_ref, wv_ref,
             q_ref, k_ref, v_ref, qln_ref, kvn_ref):
        c, s1, s2 = c_ref[...], s1_ref[...], s2_ref[...]
        nq, _ = _rms(zm_ref[:, :Q_RANK])
        qln = (nq * gql_ref[...]).astype(BF16)
        qln_ref[...] = qln
        qpre = _dot(qln, wuq_ref[...])
        nkv, _ = _rms(zm_ref[:, Q_RANK:OFF_KV])
        kvn = (nkv * gkvl_ref[...]).astype(BF16)
        kvn_ref[...] = kvn
        knope = _dot(kvn, wk_ref[...])
        v_ref[...] = _dot(kvn, wv_ref[...]).astype(BF16)
        zkr_v = zm_ref[:, OFF_KV:]
        kr_roped = _rope(zkr_v * gk_ref[...], c, s1, s2)
        slabs = [slice(hd * LANES, (hd + 1) * LANES) for hd in range(N_HEADS)]
        rq = [_head_rms(qpre[:, sl])[1] for sl in slabs]
        rk = [_head_rms(knope[:, sl] + zkr_v)[1] for sl in slabs]
        for hd, sl in enumerate(slabs):
            q_ref[:, sl] = _rope((qpre[:, sl] * rq[hd]) * gq_ref[...], c, s1, s2).astype(BF16)
            k_ref[:, sl] = (rk[hd] * (knope[:, sl] * gk_ref[...] + kr_roped)).astype(BF16)

    return pl.pallas_call(
        body, name="mla_prep", grid=(t // tm,),
        out_shape=(_sds((t, HW), BF16),) * 3 + (_sds((t, Q_RANK), BF16), _sds((t, KV_RANK), BF16)),
        in_specs=[_row(tm, MLA_IN), _full((1, Q_RANK)), _full((1, KV_RANK)),
                  _full((1, LANES)), _full((1, LANES)), tab, tab, tab,
                  _full(wuq_p.shape), _full(wk_p.shape), _full(wv_p.shape)],
        out_specs=(_row(tm, HW),) * 3 + (_row(tm, Q_RANK), _row(tm, KV_RANK)),
        compiler_params=_params("arbitrary"),
    )(zm, gql, gkvl, gq, gk, c_t, s1_t, s2_t, wuq_p, wk_p, wv_p)


AHEAD = 2
ROW_BAND = 256
SM_SCALE = QK_HEAD ** -0.5
EXP2_SCALE = SM_SCALE * 1.4426950408889634


def _diag_mask():
    rc = jnp.right_shift(lax.broadcasted_iota(jnp.int32, (BQ, 1), 0), CHUNK_SHIFT)
    cc = jnp.right_shift(lax.broadcasted_iota(jnp.int32, (1, BQ), 1), CHUNK_SHIFT)
    return rc >= cc


def _scores(q_i, k_ref, lo, e):
    return (_dot_nt(q_i, k_ref[:lo, :]) if lo else None), _dot_nt(q_i, k_ref[lo:e, :])


def _softmax_parts(scores, mask):
    sp, sd = scores
    sd = jnp.where(mask, sd, jnp.finfo(F32).min)
    m = jnp.max(sd, axis=-1, keepdims=True)
    if sp is not None:
        m = jnp.maximum(m, jnp.max(sp, axis=-1, keepdims=True))
    pd = jnp.exp2((sd - m) * EXP2_SCALE)
    l = jnp.sum(pd, axis=-1, keepdims=True)
    pp = None
    if sp is not None:
        pp = jnp.exp2((sp - m) * EXP2_SCALE)
        l = l + jnp.sum(pp, axis=-1, keepdims=True)
    return pp, pd, l


def _attn_fwd(q, k, v, nseq, seq, gather=()):
    t = q.shape[0]
    na = len(gather)
    blk = pl.BlockSpec((seq, LANES), lambda b, h: (b, h))
    n_steps = nseq * N_HEADS

    def body(q_ref, k_ref, v_ref, *rest):
        o_ref = rest[na]
        if na:
            start, forward, finish = _gather8_phases(rest[:na], rest[na + 1:2 * na + 1], *rest[2 * na + 1:])
            step = pl.program_id(0) * N_HEADS + pl.program_id(1)
            pl.when(step == 0)(start)
            pl.when(step == (7 * n_steps) // 8)(forward)
        mask = _diag_mask()
        nb = seq // BQ
        block_scores = lambda j: _scores(q_ref[j * BQ:(j + 1) * BQ, :], k_ref, j * BQ, (j + 1) * BQ)
        ahead = [block_scores(j) for j in range(min(AHEAD, nb))]
        for i in range(nb):
            lo, e = i * BQ, (i + 1) * BQ
            cur = ahead.pop(0)
            if i + AHEAD < nb:
                ahead.append(block_scores(i + AHEAD))
            pp, pd, l = _softmax_parts(cur, mask)
            o = _dot(pd.astype(BF16), v_ref[lo:e, :])
            if lo:
                o = o + _dot(pp.astype(BF16), v_ref[:lo, :])
            o_ref[lo:e, :] = (o * (1.0 / l)).astype(BF16)
        if na:
            pl.when(step == n_steps - 1)(finish)

    res = pl.pallas_call(
        body, name="attn_fwd", grid=(nseq, N_HEADS), out_shape=(_sds((t, HW), BF16),) + _gather8_shapes(gather),
        in_specs=[blk, blk, blk] + [ANY] * na, out_specs=(blk,) + (ANY,) * na,
        scratch_shapes=_gather8_sems(na) if na else [],
        compiler_params=_params("arbitrary", "arbitrary"),
    )(q, k, v, *gather)
    return res[0], (_own_block_placed(res[1:], gather) if na else ())


def _fwd_mix(attn, u0, zgate, x, mod3, wo_p, cw, cb, lng, lnb, wpw, wout, tm, tps):
    t, d = x.shape
    hpt = tm // HALO
    cwc, cbc = _by_lane_chunk(cw), _by_lane_chunk(cb)

    def body(a_ref, u_ref, uh_ref, zg_ref, x_ref, mod_ref, wo_ref, cw_ref, cb_ref, lng_ref, lnb_ref, wpw_ref, wout_ref,
             x1_ref, mixed_ref, mpre_ref, ya_ref, yb_ref, u1_ref, u3_ref, ext_ref):
        i = pl.program_id(0)
        first = (i % tps) == 0
        _fill_shifted(ext_ref, jnp.where(first, 0.0, uh_ref[...]), u_ref[...])
        nb = max(tm // ROW_BAND, 1)
        bw = tm // nb
        bands = [slice(b * bw, (b + 1) * bw) for b in range(nb)]
        yas = [_dot(a_ref[rows, :], wo_ref[...]) for rows in bands]
        u3s = []
        for b, rows in enumerate(bands):
            ya_ref[rows, :] = yas[b]
            for lc, ls in _lane_chunks():
                acc = jnp.broadcast_to(cb_ref[lc], (bw, LANES))
                for kk in range(CONV_W):
                    o = HALO - (CONV_W - 1) + kk
                    a = (o // SUBLANES) * SUBLANES + b * bw
                    acc = acc + cw_ref[lc, kk:kk + 1, :] * ext_ref[o % SUBLANES, lc, a:a + bw, :]
                u1_ref[rows, ls] = acc
            acc = u1_ref[rows, :]
            mu = jnp.mean(acc, axis=-1, keepdims=True)
            xc = acc - mu
            rstd = lax.rsqrt(jnp.mean(xc * xc, axis=-1, keepdims=True) + EPS)
            l = (xc * rstd) * lng_ref[...] + lnb_ref[...]
            u3 = (l * jax.nn.sigmoid(l)).astype(BF16)
            u3_ref[rows, :] = u3
            u3s.append(u3)
        ybs = [_dot(u3, wpw_ref[...]) for u3 in u3s]
        mpres = []
        for b, rows in enumerate(bands):
            yb_ref[rows, :] = ybs[b]
            mpre = (jax.nn.sigmoid(zg_ref[rows, :d]) * yas[b] + jax.nn.sigmoid(zg_ref[rows, d:]) * ybs[b]).astype(BF16)
            mpre_ref[rows, :] = mpre
            mpres.append(mpre)
        for rows, mpre in zip(bands, mpres):
            mixed = _dot(mpre, wout_ref[...])
            mixed_ref[rows, :] = mixed
            x1_ref[rows, :] = x_ref[rows, :] + mod_ref[2:3, :] * mixed

    halo = pl.BlockSpec((HALO, CONV_CH), lambda i: (jnp.maximum(i * hpt - 1, 0), 0))
    return pl.pallas_call(
        body, name="fwd_mix", grid=(t // tm,),
        out_shape=(_sds((t, d), F32), _sds((t, d), F32), _sds((t, d), BF16), _sds((t, d), F32), _sds((t, d), F32),
                   _sds((t, CONV_CH), F32), _sds((t, CONV_CH), BF16)),
        in_specs=[_row(tm, HW), _row(tm, CONV_CH), halo, _row(tm, 2 * d), _row(tm, d), _modspec(d, tps),
                  _full(wo_p.shape), _full(cwc.shape), _full(cbc.shape), _full((1, CONV_CH)), _full((1, CONV_CH)),
                  _full(wpw.shape), _full(wout.shape)],
        out_specs=(_row(tm, d), _row(tm, d), _row(tm, d), _row(tm, d), _row(tm, d), _row(tm, CONV_CH),
                   _row(tm, CONV_CH)),
        scratch_shapes=[pltpu.VMEM(_shifted_shape(tm), F32)],
        compiler_params=_params("arbitrary"),
    )(attn, u0, u0, zgate, x, mod3, wo_p, cwc, cbc, lng, lnb, wpw, wout)


def _shards_into_columns(w_hbm, w_ref):
    ns = w_hbm.shape[2]
    return [(w_hbm.at[s], w_ref.at[:, pl.ds(s * ns, ns)]) for s in range(w_hbm.shape[0])]


def _fwd_ffn(x1, target, g2, mod3, w1, w2, tm, tps):
    t, d = x1.shape
    dff = w1.shape[0] * w1.shape[2]

    def body(x1_ref, tg_ref, g_ref, mod_ref, w1_hbm, w2_hbm,
             h2_ref, a_ref, r_ref, dy_ref, df_ref, dgate_ref, loss_ref, w1_ref, w2_ref):
        i = pl.program_id(0)
        _load_resident(i, _shards_into_columns(w1_hbm, w1_ref) + [(w2_hbm, w2_ref)])
        x1v = x1_ref[...]
        gate2 = mod_ref[5:6, :]
        n, _ = _rms(x1v)
        h2 = ((n * g_ref[...]) * (1.0 + mod_ref[4:5, :]) + mod_ref[3:4, :]).astype(BF16)
        h2_ref[...] = h2
        a = _dot(h2, w1_ref[...])
        a_ref[...] = a
        r = jnp.square(jnp.maximum(a, 0.0)).astype(BF16)
        r_ref[...] = r
        f = _dot(r, w2_ref[...])
        e = (x1v + gate2 * f) - tg_ref[...]
        part = 0.5 * jnp.sum(jnp.mean(e * e, axis=-1, keepdims=True), axis=0, keepdims=True)
        _acc(loss_ref, jnp.broadcast_to(part, loss_ref.shape), i == 0)
        dy = e * (1.0 / d)
        dy_ref[...] = dy
        df_ref[...] = (dy * gate2).astype(BF16)
        _acc(dgate_ref, jnp.sum(dy * f, axis=0, keepdims=True), (i % tps) == 0)

    nseq = t // (tm * tps)
    return pl.pallas_call(
        body, name="fwd_ffn", grid=(t // tm,),
        out_shape=(_sds((t, d), BF16), _sds((t, dff), F32), _sds((t, dff), BF16), _sds((t, d), F32), _sds((t, d), BF16),
                   _sds((nseq, 1, d), F32), _sds((8, LANES), F32)),
        in_specs=[_row(tm, d), _row(tm, d), _full((1, d)), _modspec(d, tps), ANY, ANY],
        out_specs=(_row(tm, d), _row(tm, dff), _row(tm, dff), _row(tm, d), _row(tm, d), _seqv(d, tps),
                   _full((8, LANES))),
        scratch_shapes=[pltpu.VMEM((d, dff), BF16), pltpu.VMEM(w2.shape, BF16)],
        compiler_params=_params("arbitrary"),
    )(x1, target, g2, mod3, w1, w2)


def _bwd_ffn(df, a, x1, dy, mixed, g2, mod3, w2, w1, tm, tps):
    t, d = x1.shape
    dff = a.shape[1]

    def body(df_ref, a_ref, x1_ref, dy_ref, mx_ref, g_ref, mod_ref, w2_hbm, w1_hbm,
             da_ref, dx1_ref, dmixed_ref, dshift_ref, dscale_ref, dgate1_ref, dg2_ref, w2_ref, w1_ref):
        i = pl.program_id(0)
        _load_resident(i, [(w2_hbm, w2_ref)] + _shards_into_columns(w1_hbm, w1_ref))
        first_seq = (i % tps) == 0
        dr = _dot_nt(df_ref[...], w2_ref[...])
        da = (dr * (2.0 * jnp.maximum(a_ref[...], 0.0))).astype(BF16)
        da_ref[...] = da
        dh2 = _dot_nt(da, w1_ref[...])
        n, r = _rms(x1_ref[...])
        g = g_ref[...]
        sc1 = 1.0 + mod_ref[4:5, :]
        _acc(dshift_ref, jnp.sum(dh2, axis=0, keepdims=True), first_seq)
        _acc(dscale_ref, jnp.sum(dh2 * (n * g), axis=0, keepdims=True), first_seq)
        _acc(dg2_ref, jnp.sum((dh2 * sc1) * n, axis=0, keepdims=True), i == 0)
        dx1 = dy_ref[...] + _rms_bwd(n, r, (dh2 * sc1) * g)
        dx1_ref[...] = dx1
        _acc(dgate1_ref, jnp.sum(dx1 * mx_ref[...], axis=0, keepdims=True), first_seq)
        dmixed_ref[...] = (dx1 * mod_ref[2:3, :]).astype(BF16)

    nseq = t // (tm * tps)
    sv = _sds((nseq, 1, d), F32)
    return pl.pallas_call(
        body, name="bwd_ffn", grid=(t // tm,),
        out_shape=(_sds((t, dff), BF16), _sds((t, d), F32), _sds((t, d), BF16), sv, sv, sv, _sds((1, d), F32)),
        in_specs=[_row(tm, d), _row(tm, dff), _row(tm, d), _row(tm, d), _row(tm, d), _full((1, d)), _modspec(d, tps),
                  ANY, ANY],
        out_specs=(_row(tm, dff), _row(tm, d), _row(tm, d), _seqv(d, tps), _seqv(d, tps), _seqv(d, tps),
                   _full((1, d))),
        scratch_shapes=[pltpu.VMEM(w2.shape, BF16), pltpu.VMEM((d, dff), BF16)],
        compiler_params=_params("arbitrary"),
    )(df, a, x1, dy, mixed, g2, mod3, w2, w1)


def _bwd_mix(dmixed, zgate, ya, yb, u1, lng, lnb, wout, wo_p, wpw, tm, swap=()):
    t, d = ya.shape
    _, _, npad = _layout(d)
    nw = len(swap)
    n_steps = t // tm

    def body(dm_ref, zg_ref, ya_ref, yb_ref, u1_ref, lng_ref, lnb_ref, wout_ref, wo_ref, wpw_ref, *rest):
        dya_ref, dyb_ref, dz_ref, do_ref, du1_ref, dlng_ref, dlnb_ref, dcb_ref = rest[nw:nw + 8]
        i = pl.program_id(0)
        if nw:
            start, finish = _swap_phases(rest[:nw], rest[nw + 8:2 * nw + 8], *rest[2 * nw + 8:])
            pl.when(i == 0)(start)
        nb = max(tm // ROW_BAND, 1)
        bands = [slice(b * (tm // nb), (b + 1) * (tm // nb)) for b in range(nb)]
        col = lambda v: jnp.sum(v, axis=0, keepdims=True)
        dmpre = [_dot_nt(dm_ref[rows, :], wout_ref[...]) for rows in bands]
        dyab = []
        for rows, dmp in zip(bands, dmpre):
            ga = jax.nn.sigmoid(zg_ref[rows, :d])
            gb = jax.nn.sigmoid(zg_ref[rows, d:])
            dya = (dmp * ga).astype(BF16)
            dyb = (dmp * gb).astype(BF16)
            dya_ref[rows, :] = dya
            dyb_ref[rows, :] = dyb
            dz_ref[rows, :d] = ((dmp * ya_ref[rows, :]) * (ga * (1.0 - ga))).astype(BF16)
            dz_ref[rows, d:] = ((dmp * yb_ref[rows, :]) * (gb * (1.0 - gb))).astype(BF16)
            dyab.append((dya, dyb))
        du3s = []
        for rows, (dya, dyb) in zip(bands, dyab):
            do_ref[rows, :] = _dot_nt(dya, wo_ref[...]).astype(BF16)
            du3s.append(_dot_nt(dyb, wpw_ref[...]))
        sums = [jnp.zeros((1, CONV_CH), F32)] * 3
        for rows, du3 in zip(bands, du3s):
            u1 = u1_ref[rows, :]
            mu = jnp.mean(u1, axis=-1, keepdims=True)
            xc = u1 - mu
            rstd = lax.rsqrt(jnp.mean(xc * xc, axis=-1, keepdims=True) + EPS)
            nh = xc * rstd
            l = nh * lng_ref[...] + lnb_ref[...]
            sg = jax.nn.sigmoid(l)
            dl = du3 * (sg * (1.0 + l * (1.0 - sg)))
            dnh = dl * lng_ref[...]
            du1 = rstd * (dnh - jnp.mean(dnh, axis=-1, keepdims=True)
                          - nh * jnp.mean(dnh * nh, axis=-1, keepdims=True))
            du1_ref[rows, :] = du1
            sums = [sums[0] + col(dl * nh), sums[1] + col(dl), sums[2] + col(du1)]
        _acc(dlng_ref, sums[0], i == 0)
        _acc(dlnb_ref, sums[1], i == 0)
        _acc(dcb_ref, sums[2], i == 0)
        if nw:
            pl.when(i == n_steps - 1)(finish)

    cv = _sds((1, CONV_CH), F32)
    res = pl.pallas_call(
        body, name="bwd_mix", grid=(n_steps,),
        out_shape=(_sds((t, d), BF16), _sds((t, d), BF16), _sds((t, npad), BF16), _sds((t, HW), BF16),
                   _sds((t, CONV_CH), F32), cv, cv, cv) + _swap_shapes(swap),
        in_specs=[_row(tm, d), _row(tm, 2 * d), _row(tm, d), _row(tm, d), _row(tm, CONV_CH), _full((1, CONV_CH)),
                  _full((1, CONV_CH)), _full(wout.shape), _full(wo_p.shape), _full(wpw.shape)] + [ANY] * nw,
        out_specs=(_row(tm, d), _row(tm, d), _row(tm, 2 * d), _row(tm, HW), _row(tm, CONV_CH),
                   _full((1, CONV_CH)), _full((1, CONV_CH)), _full((1, CONV_CH))) + (ANY,) * nw,
        scratch_shapes=_swap_sems(swap) if nw else [],
        compiler_params=_params("arbitrary"),
    )(dmixed, zgate, ya, yb, u1, lng, lnb, wout, wo_p, wpw, *swap)
    return res[:8] + (res[8:],)


def _bwd_conv(dz, du1, u0, zglu, cw, tm, tps):
    t = du1.shape[0]
    d = (dz.shape[1] - MLA_IN - 2 * CONV_CH) // 2
    p_glu, _, _ = _layout(d)
    hpt = tm // HALO
    last_blk = t // HALO - 1
    cwc = _by_lane_chunk(cw)

    def body(dz_hbm, du_ref, dun_ref, u_ref, zl_ref, cw_ref, dzl_ref, dcw_ref, dext_ref, uc_ref, dcw8_ref, du0_ref):
        i = pl.program_id(0)
        last = (i % tps) == (tps - 1)
        _fill_shifted(dext_ref, du_ref[...], jnp.where(last, 0.0, dun_ref[...]))
        for lc, ls in _lane_chunks():
            uc_ref[lc] = u_ref[:, ls]

        @pl.when(i == 0)
        def _():
            dcw8_ref[...] = jnp.zeros_like(dcw8_ref)

        groups = CONV_ROWS // SUBLANES

        def conv_chunk(c, carry):
            lc, r0 = _conv_chunk(c)
            u = uc_ref[lc, pl.ds(r0, CONV_ROWS), :]
            du0 = jnp.zeros((CONV_ROWS, LANES), F32)
            for kk in range(CONV_W):
                win = _shifted(dext_ref, CONV_W - 1 - kk, lc, r0)
                prod = u * win
                part = prod[:SUBLANES]
                for g in range(1, groups):
                    part = part + prod[g * SUBLANES:(g + 1) * SUBLANES]
                dcw8_ref[lc, kk] += part
                du0 = du0 + cw_ref[lc, kk:kk + 1, :] * win
            du0_ref[lc, pl.ds(r0, CONV_ROWS), :] = du0
            return carry

        lax.fori_loop(0, CONV_LC * (tm // CONV_ROWS), conv_chunk, 0)

        @pl.when(i == pl.num_programs(0) - 1)
        def _():
            for lc, ls in _lane_chunks():
                dcw_ref[:, ls] = jnp.sum(dcw8_ref[lc], axis=1)

        for lc, ls in _lane_chunks():
            du0 = du0_ref[lc]
            ga = zl_ref[:, ls]
            sb = jax.nn.sigmoid(zl_ref[:, CONV_CH + lc * LANES:CONV_CH + (lc + 1) * LANES])
            dzl_ref[:, ls] = (du0 * sb).astype(BF16)
            dzl_ref[:, CONV_CH + lc * LANES:CONV_CH + (lc + 1) * LANES] = ((du0 * ga) * (sb * (1.0 - sb))).astype(BF16)

    nxt = pl.BlockSpec((HALO, CONV_CH), lambda i: (jnp.minimum((i + 1) * hpt, last_blk), 0))
    glu_blk = p_glu // (2 * CONV_CH)
    return pl.pallas_call(
        body, name="bwd_conv", grid=(t // tm,),
        out_shape=(_sds(dz.shape, BF16), _sds(cw.shape, F32)),
        in_specs=[ANY, _row(tm, CONV_CH), nxt, _row(tm, CONV_CH), _row(tm, 2 * CONV_CH), _full(cwc.shape)],
        out_specs=(pl.BlockSpec((tm, 2 * CONV_CH), lambda i: (i, glu_blk)), _full(cw.shape)),
        scratch_shapes=[pltpu.VMEM(_shifted_shape(tm), F32), pltpu.VMEM((CONV_LC, tm, LANES), F32),
                        pltpu.VMEM((CONV_LC, HALO, SUBLANES, LANES), F32), pltpu.VMEM((CONV_LC, tm, LANES), F32)],
        input_output_aliases={0: 0},
        compiler_params=_params("arbitrary"),
    )(dz, du1, du1, u0, zglu, cwc)


def _attn_bwd(q, k, v, do, nseq, seq, scatter=()):
    t = q.shape[0]
    ns = len(scatter)
    blk = pl.BlockSpec((seq, LANES), lambda b, h: (b, h))
    n_steps = nseq * N_HEADS

    def body(q_ref, k_ref, v_ref, do_ref, *rest):
        dq_ref, dk_ref, dv_ref = rest[ns:ns + 3]
        dka_ref, dva_ref = rest[2 * ns + 3:2 * ns + 5]
        if ns:
            start, finish = _scatter_phases(rest[:ns], rest[ns + 3:2 * ns + 3], *rest[2 * ns + 5:])
            step = pl.program_id(0) * N_HEADS + pl.program_id(1)
            pl.when(step == 0)(start)
        dka_ref[...] = jnp.zeros_like(dka_ref)
        dva_ref[...] = jnp.zeros_like(dva_ref)
        mask = _diag_mask()
        nb = seq // BQ
        block = lambda j: (_scores(q_ref[j * BQ:(j + 1) * BQ, :], k_ref, j * BQ, (j + 1) * BQ),
                           _scores(do_ref[j * BQ:(j + 1) * BQ, :], v_ref, j * BQ, (j + 1) * BQ))
        ahead = [block(j) for j in range(min(AHEAD, nb))]

        def second_stage(lo, e, dsd, dsp, pdb, ppb):
            q_i = q_ref[lo:e, :]
            do_i = do_ref[lo:e, :]
            dq = _dot(dsd, k_ref[lo:e, :])
            dka_ref[lo:e, :] += _dot_tn(dsd, q_i)
            dva_ref[lo:e, :] += _dot_tn(pdb, do_i)
            if lo:
                dq = dq + _dot(dsp, k_ref[:lo, :])
                dka_ref[:lo, :] += _dot_tn(dsp, q_i)
                dva_ref[:lo, :] += _dot_tn(ppb, do_i)
            dq_ref[lo:e, :] = dq * SM_SCALE

        held = None
        for i in range(nb):
            lo, e = i * BQ, (i + 1) * BQ
            scores, (dpp, dpd) = ahead.pop(0)
            if i + AHEAD < nb:
                ahead.append(block(i + AHEAD))
            pp, pd, l = _softmax_parts(scores, mask)
            inv = 1.0 / l
            pd = pd * inv
            delta = jnp.sum(pd * dpd, axis=-1, keepdims=True)
            if lo:
                pp = pp * inv
                delta = delta + jnp.sum(pp * dpp, axis=-1, keepdims=True)
            dsd = (pd * (dpd - delta)).astype(BF16)
            dsp = (pp * (dpp - delta)).astype(BF16) if lo else None
            if held is not None:
                second_stage(*held)
            held = (lo, e, dsd, dsp, pd.astype(BF16), pp.astype(BF16) if lo else None)
        second_stage(*held)
        dk_ref[...] = dka_ref[...] * SM_SCALE
        dv_ref[...] = dva_ref[...].astype(BF16)
        if ns:
            pl.when(step == n_steps - 1)(finish)

    res = pl.pallas_call(
        body, name="attn_bwd", grid=(nseq, N_HEADS),
        out_shape=(_sds((t, HW), F32), _sds((t, HW), F32), _sds((t, HW), BF16)) + _scatter_shapes(scatter),
        in_specs=[blk] * 4 + [ANY] * ns, out_specs=(blk,) * 3 + (ANY,) * ns,
        scratch_shapes=[pltpu.VMEM((seq, LANES), F32), pltpu.VMEM((seq, LANES), F32)]
        + (_scatter_sems(ns) if ns else []),
        compiler_params=_params("arbitrary", "arbitrary"),
    )(q, k, v, do, *scatter)
    return res[0], res[1], res[2], res[3:]


def _mla_bwd(dz, dq, dk, dv, zm, gql, gkvl, gq, gk, tabs, wuq_p, wk_p, wv_p, tm, tps):
    t = zm.shape[0]
    d = (dz.shape[1] - MLA_IN - 2 * CONV_CH) // 2
    _, p_q, _ = _layout(d)
    c_t, s1_t, s2_t = tabs
    tab = pl.BlockSpec((tm, LANES), lambda i: (i % tps, 0))

    def body(dz_hbm, dq_ref, dk_ref, dv_ref, zm_ref, gql_ref, gkvl_ref, gq_ref, gk_ref, c_ref, s1_ref, s2_ref,
             wuq_ref, wk_ref, wv_ref,
             dzm_ref, dqpre_ref, dkh_ref, dgq_ref, dgk_ref, dgql_ref, dgkvl_ref):
        i = pl.program_id(0)
        c, s1, s2 = c_ref[...], s1_ref[...], s2_ref[...]
        nq, rq = _rms(zm_ref[:, :Q_RANK])
        qpre = _dot((nq * gql_ref[...]).astype(BF16), wuq_ref[...])
        nkv, rkv = _rms(zm_ref[:, Q_RANK:OFF_KV])
        knope = _dot((nkv * gkvl_ref[...]).astype(BF16), wk_ref[...])
        zkr_v = zm_ref[:, OFF_KV:]
        gk = gk_ref[...]
        kr_roped = _rope(zkr_v * gk, c, s1, s2)
        dgq = jnp.zeros((1, LANES), F32)
        dgk = jnp.zeros((1, LANES), F32)
        dzkr = jnp.zeros((tm, LANES), F32)
        dt_sum = jnp.zeros((tm, LANES), F32)
        slabs = [slice(hd * LANES, (hd + 1) * LANES) for hd in range(N_HEADS)]
        gq = gq_ref[...]
        rqh = [_head_rms(qpre[:, sl])[1] for sl in slabs]
        rkh = [_head_rms(knope[:, sl] + zkr_v)[1] for sl in slabs]
        dyr = [_rope_t(dq_ref[:, sl], c, s1, s2) for sl in slabs]
        nqh = [qpre[:, sl] * rqh[hd] for hd, sl in enumerate(slabs)]
        sq = [jnp.sum((dyr[hd] * gq) * nqh[hd], axis=-1, keepdims=True) for hd in range(N_HEADS)]
        dr = [jnp.sum(dk_ref[:, sl] * (knope[:, sl] * gk + kr_roped), axis=-1, keepdims=True) for sl in slabs]
        for hd, sl in enumerate(slabs):
            dgq = dgq + jnp.sum(dyr[hd] * nqh[hd], axis=0, keepdims=True)
            dqpre_ref[:, sl] = (rqh[hd] * (dyr[hd] * gq - nqh[hd] * (sq[hd] * (1.0 / QK_HEAD)))).astype(BF16)
            kn = knope[:, sl]
            r = rkh[hd]
            dt = dk_ref[:, sl] * r
            via_r = (dr[hd] * (r * r * r) * (-1.0 / QK_HEAD)) * (kn + zkr_v)
            dgk = dgk + jnp.sum(dt * kn, axis=0, keepdims=True)
            dt_sum = dt_sum + dt
            dzkr = dzkr + via_r
            dkh_ref[:, sl] = (dt * gk + via_r).astype(BF16)
        de = _rope_t(dt_sum, c, s1, s2)
        dzkr = dzkr + de * gk
        dgk = dgk + jnp.sum(de * zkr_v, axis=0, keepdims=True)
        _acc(dgq_ref, dgq[:, :QK_HEAD], i == 0)
        _acc(dgk_ref, dgk[:, :QK_HEAD], i == 0)
        dzm_ref[:, OFF_KV:] = dzkr.astype(BF16)
        dqln = _dot_nt(dqpre_ref[...], wuq_ref[...])
        _acc(dgql_ref, jnp.sum(dqln * nq, axis=0, keepdims=True), i == 0)
        dzm_ref[:, :Q_RANK] = _rms_bwd(nq, rq, dqln * gql_ref[...]).astype(BF16)
        dkvn = _dot_nt(dkh_ref[...], wk_ref[...]) + _dot_nt(dv_ref[...], wv_ref[...])
        _acc(dgkvl_ref, jnp.sum(dkvn * nkv, axis=0, keepdims=True), i == 0)
        dzm_ref[:, Q_RANK:OFF_KV] = _rms_bwd(nkv, rkv, dkvn * gkvl_ref[...]).astype(BF16)

    return pl.pallas_call(
        body, name="mla_bwd", grid=(t // tm,),
        out_shape=(_sds(dz.shape, BF16), _sds((t, HW), BF16), _sds((t, HW), BF16), _sds((1, QK_HEAD), F32),
                   _sds((1, QK_HEAD), F32), _sds((1, Q_RANK), F32), _sds((1, KV_RANK), F32)),
        in_specs=[ANY, _row(tm, HW), _row(tm, HW), _row(tm, HW), _row(tm, MLA_IN),
                  _full((1, Q_RANK)), _full((1, KV_RANK)), _full((1, LANES)), _full((1, LANES)), tab, tab, tab,
                  _full(wuq_p.shape), _full(wk_p.shape), _full(wv_p.shape)],
        out_specs=(pl.BlockSpec((tm, MLA_IN), lambda i: (i, p_q // MLA_IN)), _row(tm, HW), _row(tm, HW),
                   _full((1, QK_HEAD)), _full((1, QK_HEAD)), _full((1, Q_RANK)), _full((1, KV_RANK))),
        input_output_aliases={0: 0},
        compiler_params=_params("arbitrary"),
    )(dz, dq, dk, dv, zm, gql, gkvl, gq, gk, c_t, s1_t, s2_t, wuq_p, wk_p, wv_p)


def _bwd_in(dz, x, dx1, g1, mod3, win_p, tm, tps, scatter=()):
    t, d = x.shape
    npad = dz.shape[1]

    ns = len(scatter)
    n_steps = t // tm

    def body(dz_ref, x_ref, dx1_ref, g_ref, mod_ref, wt_hbm, *rest):
        gx_ref, dshift_ref, dscale_ref, dg1_ref = rest[ns:ns + 4]
        wt_ref = rest[2 * ns + 4]
        i = pl.program_id(0)
        if ns:
            start, finish = _scatter_phases(rest[:ns], rest[ns + 4:2 * ns + 4], *rest[2 * ns + 5:])
            pl.when(i == 0)(start)
        _load_resident(i, [(wt_hbm, wt_ref)])
        first_seq = (i % tps) == 0
        g = g_ref[...]
        sc1 = 1.0 + mod_ref[1:2, :]
        nb = max(tm // ROW_BAND, 1)
        bands = [slice(b * (tm // nb), (b + 1) * (tm // nb)) for b in range(nb)]
        dhs = [_dot_nt(dz_ref[rows, :], wt_ref[...]) for rows in bands]
        sums = [jnp.zeros((1, d), F32)] * 3
        col = lambda v: jnp.sum(v, axis=0, keepdims=True)
        for rows, dh in zip(bands, dhs):
            n, r = _rms(x_ref[rows, :])
            sums = [sums[0] + col(dh), sums[1] + col(dh * (n * g)), sums[2] + col((dh * sc1) * n)]
            gx_ref[rows, :] = dx1_ref[rows, :] + _rms_bwd(n, r, (dh * sc1) * g)
        _acc(dshift_ref, sums[0], first_seq)
        _acc(dscale_ref, sums[1], first_seq)
        _acc(dg1_ref, sums[2], i == 0)
        if ns:
            pl.when(i == n_steps - 1)(finish)

    nseq = t // (tm * tps)
    sv = _sds((nseq, 1, d), F32)
    res = pl.pallas_call(
        body, name="bwd_in", grid=(n_steps,),
        out_shape=(_sds((t, d), F32), sv, sv, _sds((1, d), F32)) + _scatter_shapes(scatter),
        in_specs=[_row(tm, npad), _row(tm, d), _row(tm, d), _full((1, d)), _modspec(d, tps), ANY] + [ANY] * ns,
        out_specs=(_row(tm, d), _seqv(d, tps), _seqv(d, tps), _full((1, d))) + (ANY,) * ns,
        scratch_shapes=[pltpu.VMEM(win_p.shape, BF16)] + (_scatter_sems(ns) if ns else []),
        compiler_params=_params("arbitrary"),
    )(dz, x, dx1, g1, mod3, win_p, *scatter)
    return res[0], res[1], res[2], res[3], res[4:]


def _tile_of(n, choices):
    for c in choices:
        if n % c == 0:
            return c
    return n


def _tn_matmul(a, b, name, col_shards=0):
    t, k = a.shape
    n = b.shape[1]
    tk = _tile_of(k, (1024, 512, 256, 128))
    tn = n // col_shards if col_shards else _tile_of(n, (1024, 896, 768, 512, 384, 256, 128))
    tt = _tile_of(t, (4096, 2048, 1024, 512, 256))

    def body(a_ref, b_ref, o_ref):
        _acc(o_ref, _dot_tn(a_ref[...], b_ref[...]), pl.program_id(2) == 0)

    if col_shards:
        out_shape, out_spec = _sds((col_shards, k, tn), F32), pl.BlockSpec((None, tk, tn), lambda i, j, s: (j, i, 0))
    else:
        out_shape, out_spec = _sds((k, n), F32), pl.BlockSpec((tk, tn), lambda i, j, s: (i, j))
    return pl.pallas_call(
        body, name=name, grid=(k // tk, n // tn, t // tt), out_shape=out_shape,
        in_specs=[pl.BlockSpec((tt, tk), lambda i, j, s: (s, i)), pl.BlockSpec((tt, tn), lambda i, j, s: (s, j))],
        out_specs=out_spec, compiler_params=_params("arbitrary", "arbitrary", "arbitrary"),
    )(a, b)


N_SHARD = 4
COL_SHARDED = ("w_in", "w_uq", "w_ukv", "w_o_mla", "w_pw_out", "w_ff1")
ROW_SHARDED = ("w_out", "w_ff2")
BIG = ("w_in", "w_uq", "w_ukv", "w_o_mla", "w_pw_out", "w_out", "w_ff1", "w_ff2")
SMALL = ("norm1_g", "q_latent_g", "kv_latent_g", "qk_norm_q_g", "qk_norm_k_g", "conv_b", "conv_ln_g", "conv_ln_b",
         "norm2_g")
WEIGHTS = ("w_ada", "b_ada", "norm1_g", "w_in", "q_latent_g", "w_uq", "kv_latent_g", "w_ukv", "qk_norm_q_g",
           "qk_norm_k_g", "w_o_mla", "conv_w", "conv_b", "conv_ln_g", "conv_ln_b", "w_pw_out", "w_out", "norm2_g",
           "w_ff1", "w_ff2")


def _pad_heads(w, width):
    k = w.shape[0]
    w3 = w.reshape(k, N_HEADS, width)
    return jnp.pad(w3, ((0, 0), (0, 0), (0, LANES - width))).reshape(k, HW)


def _unpad_heads(g, width):
    k = g.shape[0]
    return g.reshape(k, N_HEADS, LANES)[:, :, :width].reshape(k, N_HEADS * width)


def _win_segments(d):
    return [(OFF_GLU, OFF_GLU + 2 * d), (OFF_KR, OFF_GLU), (0, OFF_KV), KR_LANE, (OFF_KV, OFF_KR),
            LANES - KR_LANE - QK_ROPE]


def _pad_win(g4):
    _, d, ws = g4.shape
    parts = []
    for seg in _win_segments(d):
        if isinstance(seg, int):
            parts.append(jnp.zeros((d, seg), g4.dtype))
            continue
        a, b = seg
        while a < b:
            s = a // ws
            e = min(b, (s + 1) * ws)
            parts.append(g4[s, :, a - s * ws:e - s * ws])
            a = e
    return jnp.concatenate(parts, axis=1)


def _unpad_win(gp):
    d = gp.shape[0]
    ws = (OFF_GLU + 2 * d) // N_SHARD
    pieces, p = [], 0
    for seg in _win_segments(d):
        if isinstance(seg, int):
            p += seg
        else:
            pieces.append((seg[0], seg[1], p))
            p += seg[1] - seg[0]
    shards = []
    for s in range(N_SHARD):
        lo, hi = s * ws, (s + 1) * ws
        cols = [gp[:, p0 + max(a, lo) - a:p0 + min(b, hi) - a] for a, b, p0 in sorted(pieces) if max(a, lo) < min(b, hi)]
        shards.append(jnp.concatenate(cols, axis=1))
    return jnp.stack(shards)


def _col_shards(g):
    k, n = g.shape
    return g.reshape(k, N_SHARD, n // N_SHARD).transpose(1, 0, 2)


def _from_shards(g, name):
    ns, ks, nn = g.shape
    if name in ROW_SHARDED:
        return g.reshape(ns * ks, nn)
    return g.transpose(1, 0, 2).reshape(ks, ns * nn)


BY_SHARD = ("w_in", "w_ff1")
EARLY = ("w_in", "w_uq", "w_ukv")
LATE = ("w_o_mla", "w_pw_out", "w_out", "w_ff1", "w_ff2")


def _assemble(names, gathered):
    by_shard = {n: g.reshape((N_SHARD, 2 * g.shape[1]) + g.shape[2:]) for n, g in zip(names, gathered)}
    return {n: g if n in BY_SHARD else _from_shards(g, n) for n, g in by_shard.items()}


LARGE = ("w_in", "w_ff1", "w_ff2")
GROUP_A = ("w_out", "w_ff1", "w_ff2")
GROUP_B = ("w_in", "w_uq", "w_ukv", "w_o_mla", "w_pw_out")


def _pair_halves(g):
    return g.reshape(N_SHARD, 2, g.shape[1] // 2, g.shape[2])


def _pair_sums(names, halves, from_sibling):
    if not halves:
        return []
    ix, iy, ic = _place()
    cidx = ic.reshape(1).astype(jnp.int32)
    c_own = jnp.stack([ic, 2 * ix + iy]).astype(jnp.int32)
    out = {n: _add_pair(g, l, c_own, "pair_sum_" + n)
           for n, g, l in zip(names, halves, from_sibling) if n in LARGE}
    small = [j for j, n in enumerate(names) if n not in LARGE]
    if small:
        res = _add_pair_whole([halves[j] for j in small], [from_sibling[j] for j in small], cidx,
                              "pair_sum_small_" + names[small[0]])
        out.update({names[j]: r for j, r in zip(small, res)})
    return [out[n] for n in names]


def _local_step(x, target, mod, sp, w, late=None, tm=256):
    comm = late is not None
    w = dict(w)
    nseq, seq, d = x.shape
    t = nseq * seq
    tps = seq // tm
    xf = x.reshape(t, d)
    tg = target.reshape(t, d)
    mod3 = mod.reshape(nseq, N_MOD, d)

    win_p = _pad_win(w["w_in"])
    wuq_p = _pad_heads(w["w_uq"], QK_HEAD)
    wkv3 = w["w_ukv"].reshape(KV_RANK, N_HEADS, QK_NOPE + V_HEAD)
    wk_p = _pad_heads(wkv3[:, :, :QK_NOPE].reshape(KV_RANK, -1), QK_NOPE)
    wv_p = _pad_heads(wkv3[:, :, QK_NOPE:].reshape(KV_RANK, -1), V_HEAD)
    cw = jnp.pad(w["conv_w"], ((0, HALO - CONV_W), (0, 0)))
    pad_g = lambda g: jnp.pad(g, ((0, 0), (0, LANES - QK_HEAD)))
    gq, gk = pad_g(sp["qk_norm_q_g"]), pad_g(sp["qk_norm_k_g"])
    tabs = _rope_tables(seq)

    tm_in, tps_in = (2 * tm, tps // 2) if tps % 2 == 0 else (tm, tps)
    h, zm, zglu, zgate, u0 = _fwd_in(xf, sp["norm1_g"], mod3, win_p, tm_in, tps_in)
    q, k, v, qln, kvn = _mla_prep(zm, sp["q_latent_g"], sp["kv_latent_g"], gq, gk, tabs, wuq_p, wk_p, wv_p, tm_in,
                                  tps_in)
    attn, gathered = _attn_fwd(q, k, v, nseq, seq, tuple(late) if comm else ())
    if comm:
        w.update(_assemble(LATE, gathered))
    wo_p = jnp.pad(w["w_o_mla"].reshape(N_HEADS, V_HEAD, d), ((0, 0), (0, LANES - V_HEAD), (0, 0))).reshape(HW, d)
    x1, mixed, mpre, ya, yb, u1, u3 = _fwd_mix(attn, u0, zgate, xf, mod3, wo_p, cw, sp["conv_b"], sp["conv_ln_g"],
                                               sp["conv_ln_b"], w["w_pw_out"], w["w_out"], tm_in, tps_in)
    h2, a, r, dy, df, dgate2, loss_acc = _fwd_ffn(x1, tg, sp["norm2_g"], mod3, w["w_ff1"], w["w_ff2"], tm, tps)
    da, dx1, dmixed, dshift2, dscale2, dgate1, dg2 = _bwd_ffn(df, a, x1, dy, mixed, sp["norm2_g"], mod3,
                                                              w["w_ff2"], w["w_ff1"], tm, tps)
    gw = {
        "w_out": _tn_matmul(mpre, dmixed, "dw_out").reshape(N_SHARD, d // N_SHARD, d),
        "w_ff1": _tn_matmul(h2, da, "dw_ff1", N_SHARD),
        "w_ff2": _tn_matmul(r, df, "dw_ff2").reshape(N_SHARD, -1, d),
    }
    halves_a = [_pair_halves(gw[n]) for n in GROUP_A] if comm else []
    dya, dyb, dz, do, du1, dlng, dlnb, dcb, from_sibling = _bwd_mix(
        dmixed, zgate, ya, yb, u1, sp["conv_ln_g"], sp["conv_ln_b"], w["w_out"], wo_p, w["w_pw_out"], tm_in, tuple(halves_a))
    pair_a = _pair_sums(GROUP_A, halves_a, from_sibling)
    dz, dcw = _bwd_conv(dz, du1, u0, zglu, cw, tm_in, tps_in)
    gw["conv_w"] = dcw
    dq, dk, dv, land_a = _attn_bwd(q, k, v, do, nseq, seq, tuple(p[1] for p in pair_a))
    dz, dqpre, dkh, dgq, dgk, dgql, dgkvl = _mla_bwd(dz, dq, dk, dv, zm, sp["q_latent_g"], sp["kv_latent_g"], gq, gk,
                                                      tabs, wuq_p, wk_p, wv_p, tm_in, tps_in)
    dwk_p = _tn_matmul(kvn, dkh, "dw_uk")
    dwv_p = _tn_matmul(kvn, dv, "dw_uv")
    dwkv = jnp.concatenate([dwk_p.reshape(KV_RANK, N_HEADS, LANES)[:, :, :QK_NOPE],
                            dwv_p.reshape(KV_RANK, N_HEADS, LANES)[:, :, :V_HEAD]], axis=2).reshape(KV_RANK, -1)
    dwo = _tn_matmul(attn, dya, "dw_o").reshape(N_HEADS, LANES, d)[:, :V_HEAD].reshape(MLA_WIDTH, d)
    gw["w_in"] = _unpad_win(_tn_matmul(h, dz, "dw_in"))
    gw["w_uq"] = _col_shards(_unpad_heads(_tn_matmul(qln, dqpre, "dw_uq"), QK_HEAD))
    gw["w_ukv"] = _col_shards(dwkv)
    gw["w_o_mla"] = _col_shards(dwo)
    gw["w_pw_out"] = _tn_matmul(u3, dyb, "dw_pw", N_SHARD)
    pair_b = []
    if comm:
        halves_b = [_pair_halves(gw[n]) for n in GROUP_B]
        pair_b = _pair_sums(GROUP_B, halves_b, _pair_swap([h.astype(BF16) for h in halves_b], "grad_pair_swap"))
    gx, dshift1, dscale1, dg1, land_b = _bwd_in(dz, xf, dx1, sp["norm1_g"], mod3, win_p, tm_in, tps_in,
                                                tuple(p[1] for p in pair_b))
    if comm:
        for n, p, l in zip(GROUP_A + GROUP_B, pair_a + pair_b, land_a + land_b):
            gw[n] = (p[0], l)
    gs = {
        "norm1_g": dg1, "q_latent_g": dgql, "kv_latent_g": dgkvl, "qk_norm_q_g": dgq, "qk_norm_k_g": dgk,
        "conv_b": dcb, "conv_ln_g": dlng, "conv_ln_b": dlnb, "norm2_g": dg2,
    }
    dmod = jnp.concatenate([dshift1, dscale1, dgate1, dshift2, dscale2, dgate2], axis=2).reshape(nseq, N_MOD * d)
    return loss_acc, gx.reshape(nseq, seq, d), dmod, gw, gs


def kernel(x, c, w_ada, b_ada, norm1_g, w_in, q_latent_g, w_uq, kv_latent_g, w_ukv, qk_norm_q_g, qk_norm_k_g, w_o_mla, conv_w, conv_b, conv_ln_g, conv_ln_b, w_pw_out, w_out, norm2_g, w_ff1, w_ff2, loss_target, m_w_ada, m_b_ada, m_norm1_g, m_w_in, m_q_latent_g, m_w_uq, m_kv_latent_g, m_w_ukv, m_qk_norm_q_g, m_qk_norm_k_g, m_w_o_mla, m_conv_w, m_conv_b, m_conv_ln_g, m_conv_ln_b, m_w_pw_out, m_w_out, m_norm2_g, m_w_ff1, m_w_ff2, v_w_ada, v_b_ada, v_norm1_g, v_w_in, v_q_latent_g, v_w_uq, v_kv_latent_g, v_w_ukv, v_qk_norm_q_g, v_qk_norm_k_g, v_w_o_mla, v_conv_w, v_conv_b, v_conv_ln_g, v_conv_ln_b, v_w_pw_out, v_w_out, v_norm2_g, v_w_ff1, v_w_ff2):
    given = dict(locals())
    wts = {n: given[n][0] for n in WEIGHTS}
    mom = {n: given["m_" + n][0] for n in WEIGHTS}
    var = {n: given["v_" + n][0] for n in WEIGHTS}
    vec = lambda a: a.reshape(1, -1)
    nseq, seq, d = x.shape
    ix, iy, ic = _place()
    shard = 2 * ix + iy

    half = lambda n: lax.dynamic_slice_in_dim(wts[n].astype(BF16), ic * (wts[n].shape[0] // 2), wts[n].shape[0] // 2,
                                              axis=0)
    gathered = _all_gather8([half(n) for n in EARLY] + [wts["conv_w"], c], "gather_weights")
    full = _assemble(EARLY, gathered)
    full["conv_w"] = _from_shards(gathered[-2][0::2], "conv_w")
    c_all = gathered[-1].reshape(8 * nseq, d)

    n_ada = wts["w_ada"].shape[1]
    b_sh = lax.dynamic_slice_in_dim(vec(wts["b_ada"]), shard * n_ada, n_ada, axis=1)
    mod_sh = _ada_mod(c_all, wts["w_ada"], b_sh)
    hb = 4 * nseq
    mod_blk = lax.dynamic_slice_in_dim(mod_sh, ic * hb, hb, axis=0)
    (mod_all,) = _all_gather8([mod_blk], "gather_mod")
    mod_mine = lax.dynamic_slice_in_dim(mod_all, (2 * iy + ic) * nseq, nseq, axis=1)
    mod = jnp.concatenate([lax.dynamic_index_in_dim(mod_mine, 2 * s + ix, axis=0, keepdims=False)
                           for s in range(N_SHARD)], axis=1)

    sp = {n: vec(wts[n]) for n in SMALL}
    loss_part, grad_x, dmod, gw, gs = _local_step(x, loss_target, mod, sp, full, [half(n) for n in LATE])

    own_c = jnp.stack([shard, ic]).astype(jnp.int32)
    mine_sum = {n: _add_chips(gw[n][0], gw[n][1], own_c, "chip_sum_" + n) for n in LARGE}
    few = tuple(n for n in BIG if n not in LARGE)
    mine_sum.update(zip(few, _add_chips_whole([gw[n][0] for n in few], [gw[n][1] for n in few], own_c, "chip_sum_small")))
    summed, parts = _pair_gather_and_all_gather8(
        [mine_sum[n] for n in BIG], [dmod, gw["conv_w"], loss_part] + [gs[n] for n in SMALL], "tail_exchange")

    dmod_all = parts[0].reshape(8 * nseq, N_MOD * d)
    dmod_sh = lax.dynamic_slice_in_dim(dmod_all, shard * n_ada, n_ada, axis=1)
    res = _ada_bwd(c_all, dmod_all, dmod_sh, parts[1:])
    grads = {"w_ada": res[0], "b_ada": res[1]}
    n_cw = wts["conv_w"].shape[1]
    grads["conv_w"] = lax.dynamic_slice_in_dim(res[2], shard * n_cw, n_cw, axis=1)[:CONV_W]
    loss = res[3][0, 0]
    for n, g in zip(SMALL, res[4:]):
        grads[n] = g
    for n, g in zip(BIG, summed):
        grads[n] = g.reshape(wts[n].shape)

    delta, new_m, new_v = {}, {}, {}
    for n in LARGE + ("w_ada",):
        if n == "w_in":
            res = _adamw(wts[n].T, grads[n].T, mom[n].T, var[n].T, "adamw_" + n)
            delta[n], new_m[n], new_v[n] = (a.T for a in res)
        else:
            delta[n], new_m[n], new_v[n] = _adamw(wts[n], grads[n], mom[n], var[n], "adamw_" + n)
    rest = ("b_ada", "conv_w") + SMALL + few
    as2d = lambda a: a if a.ndim == 2 else vec(a)
    res = _adamw_small(*[[as2d(t[n]) for n in rest] for t in (wts, grads, mom, var)])
    for dst, arrs in zip((delta, new_m, new_v), res):
        for n, a in zip(rest, arrs):
            dst[n] = a

    outs = [loss, grad_x]
    for group in (grads, delta, new_m, new_v):
        outs += [group[n].reshape(given[n].shape) for n in WEIGHTS]
    return tuple(outs)
```

```python
import jax
import jax.numpy as jnp
from jax import lax
from jax.experimental import pallas as pl
from jax.experimental.pallas import tpu as pltpu

F32 = jnp.float32
BF16 = jnp.bfloat16
MESH = pl.DeviceIdType.MESH
ANY = pl.BlockSpec(memory_space=pl.ANY)

CHUNK = 64
CHUNK_SHIFT = 6
N_HEADS = 8
QK_NOPE = 64
QK_ROPE = 32
QK_HEAD = QK_NOPE + QK_ROPE
V_HEAD = 64
Q_RANK = 256
KV_RANK = 128
MLA_WIDTH = N_HEADS * V_HEAD
CONV_CH = 512
CONV_W = 31
ROPE_THETA = 10000.0
EPS = 1e-6
LANES = 128
SUBLANES = 8
HW = N_HEADS * LANES
OFF_KV = Q_RANK + KV_RANK
OFF_KR = OFF_KV + QK_ROPE
OFF_GLU = OFF_KR + 2 * CONV_CH
KR_LANE = QK_NOPE
MLA_IN = Q_RANK + KV_RANK + LANES
HALO = 32
N_MOD = 6

ADAM_LR = 0.001
ADAM_B1 = 0.9
ADAM_B2 = 0.999
ADAM_EPS = 1e-08
ADAM_WD = 0.01
ADAM_STEP = 10

VMEM_LIMIT = 56 * 1024 * 1024
BQ = 256


def _layout(d):
    p_glu = 2 * d
    p_q = p_glu + 2 * CONV_CH
    return p_glu, p_q, p_q + MLA_IN


def _params(*sem):
    return pltpu.CompilerParams(dimension_semantics=sem, vmem_limit_bytes=VMEM_LIMIT)


def _dot(a, b):
    return jnp.dot(a, b, preferred_element_type=F32)


def _dot_tn(a, b):
    return lax.dot_general(a, b, (((0,), (0,)), ((), ())), preferred_element_type=F32)


def _dot_nt(a, b):
    return lax.dot_general(a, b, (((1,), (1,)), ((), ())), preferred_element_type=F32)


def _acc(ref, val, first):
    @pl.when(first)
    def _():
        ref[...] = val

    @pl.when(jnp.logical_not(first))
    def _():
        ref[...] += val


def _rms(x):
    r = lax.rsqrt(jnp.mean(x * x, axis=-1, keepdims=True) + EPS)
    return x * r, r


def _rms_bwd(n, r, dn):
    return r * (dn - n * jnp.mean(dn * n, axis=-1, keepdims=True))


def _head_rms(sl):
    r = lax.rsqrt(jnp.sum(sl * sl, axis=-1, keepdims=True) * (1.0 / QK_HEAD) + EPS)
    return sl * r, r


def _head_rms_bwd(n, r, dn):
    return r * (dn - n * (jnp.sum(dn * n, axis=-1, keepdims=True) * (1.0 / QK_HEAD)))


def _rope(x, c, s1, s2):
    return x * c + pltpu.roll(x, QK_ROPE // 2, 1) * s1 + pltpu.roll(x, LANES - QK_ROPE // 2, 1) * s2


def _rope_t(dy, c, s1, s2):
    return dy * c + pltpu.roll(dy * s1, LANES - QK_ROPE // 2, 1) + pltpu.roll(dy * s2, QK_ROPE // 2, 1)


def _rope_tables(seq):
    half = QK_ROPE // 2
    inv_freq = ROPE_THETA ** (-jnp.arange(0, QK_ROPE, 2, dtype=F32) / QK_ROPE)
    ang = jnp.arange(seq, dtype=F32)[:, None] * inv_freq[None, :]
    cos, sin = jnp.cos(ang), jnp.sin(ang)
    z = lambda n: jnp.zeros((seq, n), F32)
    tail = LANES - QK_HEAD
    c = jnp.concatenate([jnp.ones((seq, QK_NOPE), F32), cos, cos, jnp.ones((seq, tail), F32)], axis=1)
    s1 = jnp.concatenate([z(QK_NOPE + half), sin, z(tail)], axis=1)
    s2 = jnp.concatenate([z(QK_NOPE), -sin, z(half + tail)], axis=1)
    return c, s1, s2


def _row(tm, w):
    return pl.BlockSpec((tm, w), lambda i: (i, 0))


def _modspec(d, tps):
    return pl.BlockSpec((None, N_MOD, d), lambda i: (i // tps, 0, 0))


def _seqv(w, tps):
    return pl.BlockSpec((None, 1, w), lambda i: (i // tps, 0, 0))


def _full(shape):
    return pl.BlockSpec(shape, lambda i: tuple(0 for _ in shape))


def _sds(shape, dtype):
    return jax.ShapeDtypeStruct(shape, dtype)


CONV_ROWS = 64
CONV_LC = CONV_CH // LANES


def _lane_chunks():
    return [(lc, slice(lc * LANES, (lc + 1) * LANES)) for lc in range(CONV_LC)]


def _fill_shifted(ext_ref, head, body):
    nh = head.shape[0]
    for lc, ls in _lane_chunks():
        ext_ref[0, lc, :nh, :] = head[:, ls]
        ext_ref[0, lc, nh:, :] = body[:, ls]
        rows = ext_ref[0, lc]
        for b in range(1, SUBLANES):
            ext_ref[b, lc] = pltpu.roll(rows, rows.shape[0] - b, 0)


def _shifted_shape(tm):
    return (SUBLANES, CONV_LC, tm + HALO, LANES)


def _conv_chunk(c):
    return c % CONV_LC, pl.multiple_of((c // CONV_LC) * CONV_ROWS, CONV_ROWS)


def _shifted(ext_ref, o, lc, r0):
    a = pl.multiple_of((o // SUBLANES) * SUBLANES + r0, SUBLANES)
    return ext_ref[o % SUBLANES, lc, pl.ds(a, CONV_ROWS), :]


def _by_lane_chunk(a):
    return a.reshape(a.shape[0], CONV_LC, LANES).transpose(1, 0, 2)


def _load_resident(i, pairs):
    @pl.when(i == 0)
    def _():
        for src, dst in pairs:
            pltpu.sync_copy(src, dst)


def _place():
    return lax.axis_index("x"), lax.axis_index("y"), lax.axis_index("c")


def _all_gather8(blocks, name):
    na = len(blocks)

    def body(*refs):
        start, forward, finish = _gather8_phases(refs[:na], refs[na:2 * na], *refs[2 * na:])
        start()
        forward()
        finish()

    outs = pl.pallas_call(
        body, name=name, out_shape=_gather8_shapes(blocks), in_specs=[ANY] * na, out_specs=(ANY,) * na,
        scratch_shapes=_gather8_sems(na),
    )(*blocks)
    return _own_block_placed(outs, blocks)


def _gather8_shapes(blocks):
    return tuple(_sds((8,) + b.shape, b.dtype) for b in blocks)


def _gather8_sems(na):
    return [pltpu.SemaphoreType.DMA((7 * na,)), pltpu.SemaphoreType.DMA((7 * na,))]


def _own_block_placed(outs, blocks):
    ix, iy, ic = _place()
    return tuple(lax.dynamic_update_index_in_dim(o, b, 4 * ix + 2 * iy + ic, 0) for o, b in zip(outs, blocks))


def _gather8_phases(x_refs, out_refs, send_sems, recv_sems):
    na = len(x_refs)
    x, y, c = _place()
    me, sibling = (x, y, c), (x, y, 1 - c)
    chips = [(1 - x, y), (x, 1 - y), (1 - x, 1 - y)]

    def copy(a, k, blk, to, from_input=False):
        dst = out_refs[a].at[4 * blk[0] + 2 * blk[1] + blk[2]]
        return pltpu.make_async_remote_copy(
            src_ref=x_refs[a] if from_input else dst, dst_ref=dst,
            send_sem=send_sems.at[7 * a + k], recv_sem=recv_sems.at[7 * a + k], device_id=to, device_id_type=MESH)

    def first(a):
        return [copy(a, 0, me, sibling, True)] + [copy(a, 1 + j, me, (*chip, c), True) for j, chip in enumerate(chips)]

    def start():
        for a in range(na):
            for cp in first(a):
                cp.start()

    def forward():
        for j, chip in enumerate(chips):
            for a in range(na):
                copy(a, 1 + j, (*chip, c), me).wait_recv()
                copy(a, 4 + j, (*chip, c), sibling).start()

    def finish():
        for a in range(na):
            copy(a, 0, sibling, me).wait_recv()
            for j, chip in enumerate(chips):
                copy(a, 4 + j, (*chip, 1 - c), me).wait_recv()
        for a in range(na):
            for cp in first(a) + [copy(a, 4 + j, (*chip, c), sibling) for j, chip in enumerate(chips)]:
                cp.wait_send()

    return start, forward, finish


def _pair_swap(gs, name):
    na = len(gs)

    def body(*refs):
        start, finish = _swap_phases(refs[:na], refs[na:2 * na], *refs[2 * na:])
        start()
        finish()

    return pl.pallas_call(
        body, name=name, out_shape=_swap_shapes(gs), in_specs=[ANY] * na, out_specs=(ANY,) * na,
        scratch_shapes=_swap_sems(gs),
    )(*gs)


def _swap_shapes(gs):
    return tuple(_sds(g.shape[:1] + g.shape[2:], g.dtype) for g in gs)


def _swap_sems(gs):
    n = sum(g.shape[0] for g in gs)
    return [pltpu.SemaphoreType.DMA((n,)), pltpu.SemaphoreType.DMA((n,))]


def _swap_phases(g_refs, land_refs, send_sems, recv_sems):
    x, y, c = _place()

    def copies():
        cps, k = [], 0
        for g_ref, land_ref in zip(g_refs, land_refs):
            for s in range(g_ref.shape[0]):
                cps.append(pltpu.make_async_remote_copy(
                    src_ref=g_ref.at[s, 1 - c], dst_ref=land_ref.at[s], send_sem=send_sems.at[k],
                    recv_sem=recv_sems.at[k], device_id=(x, y, 1 - c), device_id_type=MESH))
                k += 1
        return cps

    def start():
        for cp in copies():
            cp.start()

    def finish():
        for cp in copies():
            cp.wait()

    return start, finish


def _scatter_shapes(hs):
    return tuple(_sds((3,) + h.shape[1:], h.dtype) for h in hs)


def _scatter_sems(na):
    return [pltpu.SemaphoreType.DMA((3 * na,)), pltpu.SemaphoreType.DMA((3 * na,))]


def _scatter_phases(h_refs, land_refs, send_sems, recv_sems):
    x, y, c = _place()
    chips = [(1 - x, y), (x, 1 - y), (1 - x, 1 - y)]

    def copies():
        return [pltpu.make_async_remote_copy(
            src_ref=h_refs[a].at[2 * tx + ty], dst_ref=land_refs[a].at[j], send_sem=send_sems.at[3 * a + j],
            recv_sem=recv_sems.at[3 * a + j], device_id=(tx, ty, c), device_id_type=MESH)
            for a in range(len(h_refs)) for j, (tx, ty) in enumerate(chips)]

    def start():
        for cp in copies():
            cp.start()

    def finish():
        for cp in copies():
            cp.wait()

    return start, finish


def _pair_gather_and_all_gather8(fs, blocks, name):
    nf, nb = len(fs), len(blocks)

    def body(*refs):
        f_refs = refs[nf + nb:2 * nf + nb]
        b_out = refs[2 * nf + nb:2 * nf + 2 * nb]
        send_sems, recv_sems, g_send, g_recv = refs[2 * nf + 2 * nb:]
        x, y, c = _place()
        start, forward, finish = _gather8_phases(refs[nf:nf + nb], b_out, g_send, g_recv)
        sends = [pltpu.make_async_remote_copy(
            src_ref=f_refs[a].at[c], dst_ref=f_refs[a].at[c], send_sem=send_sems.at[a], recv_sem=recv_sems.at[a],
            device_id=(x, y, 1 - c), device_id_type=MESH) for a in range(nf)]
        recvs = [pltpu.make_async_remote_copy(
            src_ref=f_refs[a].at[c], dst_ref=f_refs[a].at[1 - c], send_sem=send_sems.at[a],
            recv_sem=recv_sems.at[a], device_id=(x, y, 1 - c), device_id_type=MESH) for a in range(nf)]
        start()
        for cp in sends:
            cp.start()
        forward()
        finish()
        for cp in recvs:
            cp.wait_recv()
        for cp in sends:
            cp.wait_send()

    res = pl.pallas_call(
        body, name=name, out_shape=tuple(_sds(f.shape, f.dtype) for f in fs) + _gather8_shapes(blocks),
        in_specs=[ANY] * (nf + nb), out_specs=(ANY,) * (nf + nb), input_output_aliases={a: a for a in range(nf)},
        scratch_shapes=[pltpu.SemaphoreType.DMA((nf,)), pltpu.SemaphoreType.DMA((nf,))] + _gather8_sems(nb),
    )(*fs, *blocks)
    return res[:nf], _own_block_placed(res[nf:], blocks)


def _row_tile(r, n, itemsize=4, budget=1 << 21):
    if r * n * itemsize <= budget:
        return r
    best = None
    for tr in range(16, r, 16):
        if r % tr == 0 and tr * n * itemsize <= budget:
            best = tr
    assert best is not None, (r, n)
    return best


def _add_pair(g, land, c_own, name):
    ns, _, r, n = g.shape
    tr = _row_tile(r, n)

    def body(co_ref, a_ref, b_ref, o_ref, ob_ref):
        s = a_ref[...] + b_ref[...]
        ob_ref[...] = s.astype(BF16)

        @pl.when(pl.program_id(1) == co_ref[1])
        def _():
            o_ref[...] = s

    per_shard = pl.BlockSpec((None, tr, n), lambda i, s, co: (s, i, 0))
    return pl.pallas_call(
        body, name=name, out_shape=(_sds((r, n), F32), _sds((ns, r, n), BF16)),
        grid_spec=pltpu.PrefetchScalarGridSpec(
            num_scalar_prefetch=1, grid=(r // tr, ns),
            in_specs=[pl.BlockSpec((None, None, tr, n), lambda i, s, co: (s, co[0], i, 0)), per_shard],
            out_specs=(pl.BlockSpec((tr, n), lambda i, s, co: (i, 0)), per_shard)),
        compiler_params=_params("arbitrary", "arbitrary"),
    )(c_own, g, land)


def _add_pair_whole(gs, lands, cidx, name):
    k = len(gs)

    def body(c_ref, *refs):
        for a_ref, b_ref, o_ref, ob_ref in zip(refs[:k], refs[k:2 * k], refs[2 * k:3 * k], refs[3 * k:]):
            s = a_ref[...] + b_ref[...]
            o_ref[...] = s
            ob_ref[...] = s.astype(BF16)

    half = lambda g: pl.BlockSpec((g.shape[0], None) + g.shape[2:], lambda i, cr: (0, cr[0], 0, 0))
    whole = lambda g: pl.BlockSpec(g.shape[:1] + g.shape[2:], lambda i, cr: (0, 0, 0))
    shapes = lambda dt: tuple(_sds(g.shape[:1] + g.shape[2:], dt) for g in gs)
    res = pl.pallas_call(
        body, name=name, out_shape=shapes(F32) + shapes(BF16),
        grid_spec=pltpu.PrefetchScalarGridSpec(
            num_scalar_prefetch=1, grid=(1,),
            in_specs=[half(g) for g in gs] + [whole(g) for g in gs],
            out_specs=tuple(whole(g) for g in gs) * 2),
        compiler_params=_params("arbitrary"),
    )(cidx, *gs, *lands)
    return list(zip(res[:k], res[k:]))


def _add_chips_whole(hs, lands, own_c, name):
    k = len(hs)

    def body(o_idx, *refs):
        for h_ref, l_ref, o_ref in zip(refs[:k], refs[k:2 * k], refs[2 * k:]):
            o_ref[...] = ((h_ref[...] + l_ref[0].astype(F32)) + l_ref[1].astype(F32)) + l_ref[2].astype(F32)

    return pl.pallas_call(
        body, name=name, out_shape=tuple(_sds((2,) + h.shape[1:], F32) for h in hs),
        grid_spec=pltpu.PrefetchScalarGridSpec(
            num_scalar_prefetch=1, grid=(1,),
            in_specs=[pl.BlockSpec((None,) + h.shape[1:], lambda i, o: (o[0], 0, 0)) for h in hs]
            + [pl.BlockSpec(l.shape, lambda i, o: (0, 0, 0)) for l in lands],
            out_specs=tuple(pl.BlockSpec((None,) + h.shape[1:], lambda i, o: (o[1], 0, 0)) for h in hs)),
        compiler_params=_params("arbitrary"),
    )(own_c, *hs, *lands)


def _add_chips(h, land, own_c, name):
    r, n = h.shape
    tr = _row_tile(r, n)

    def body(o_idx, h_ref, l_ref, o_ref):
        o_ref[...] = ((h_ref[...] + l_ref[0].astype(F32)) + l_ref[1].astype(F32)) + l_ref[2].astype(F32)

    return pl.pallas_call(
        body, name=name, out_shape=_sds((2, r, n), F32),
        grid_spec=pltpu.PrefetchScalarGridSpec(
            num_scalar_prefetch=1, grid=(r // tr,),
            in_specs=[pl.BlockSpec((tr, n), lambda i, o: (i, 0)),
                      pl.BlockSpec((3, tr, n), lambda i, o: (0, i, 0))],
            out_specs=pl.BlockSpec((None, tr, n), lambda i, o: (o[1], i, 0))),
        compiler_params=_params("arbitrary"),
    )(own_c, h, land)


def _adam_math(w, g, m, v):
    nm = ADAM_B1 * m + (1.0 - ADAM_B1) * g
    nv = ADAM_B2 * v + (1.0 - ADAM_B2) * (g * g)
    m_hat = nm / (1.0 - ADAM_B1 ** ADAM_STEP)
    v_hat = nv / (1.0 - ADAM_B2 ** ADAM_STEP)
    return -ADAM_LR * (m_hat / (jnp.sqrt(v_hat) + ADAM_EPS) + ADAM_WD * w), nm, nv


def _adamw(w, g, m, v, name):
    r, n = w.shape

    def body(w_ref, g_ref, m_ref, v_ref, d_ref, nm_ref, nv_ref):
        d_ref[...], nm_ref[...], nv_ref[...] = _adam_math(w_ref[...], g_ref[...], m_ref[...], v_ref[...])

    if r % 16 == 0:
        tr = _row_tile(r, n)
        steps, spec = r // tr, pl.BlockSpec((tr, n), lambda i: (i, 0))
    else:
        tc = 4 * LANES
        steps, spec = n // tc, pl.BlockSpec((r, tc), lambda j: (0, j))
    return pl.pallas_call(
        body, name=name, out_shape=(_sds((r, n), F32),) * 3, grid=(steps,),
        in_specs=[spec] * 4, out_specs=(spec,) * 3, compiler_params=_params("arbitrary"),
    )(w, g, m, v)


def _adamw_small(ws, gs, ms, vs):
    k = len(ws)

    def body(*refs):
        ins, outs = refs[:4 * k], refs[4 * k:]
        for j in range(k):
            d, nm, nv = _adam_math(ins[j][...], ins[k + j][...], ins[2 * k + j][...], ins[3 * k + j][...])
            outs[j][...] = d
            outs[k + j][...] = nm
            outs[2 * k + j][...] = nv

    shapes = tuple(_sds(w.shape, F32) for w in ws)
    res = pl.pallas_call(body, name="adamw_small", out_shape=shapes * 3,
                         compiler_params=pltpu.CompilerParams(vmem_limit_bytes=VMEM_LIMIT))(*ws, *gs, *ms, *vs)
    return res[:k], res[k:2 * k], res[2 * k:]


def _ada_mod(c_all, w_sh, b_sh):
    b, _ = c_all.shape
    n = w_sh.shape[1]

    def body(c_ref, w_ref, b_ref, o_ref):
        cc = c_ref[...]
        ca = (cc * jax.nn.sigmoid(cc)).astype(BF16)
        o_ref[...] = _dot(ca, w_ref[...].astype(BF16)) + b_ref[...]

    return pl.pallas_call(body, name="ada_mod", out_shape=_sds((b, n), F32),
                          compiler_params=pltpu.CompilerParams(vmem_limit_bytes=VMEM_LIMIT))(c_all, w_sh, b_sh)


def _ada_bwd(c_all, dmod_all, dmod_sh, parts):
    b, d = c_all.shape
    n6 = dmod_all.shape[1]
    n = dmod_sh.shape[1]
    k = len(parts)

    def body(*refs):
        c_ref, da_ref, ds_ref = refs[:3]
        p_refs = refs[3:3 + k]
        dw_ref, db_ref = refs[3 + k:5 + k]
        s_refs = refs[5 + k:]
        cc = c_ref[...]
        ca = (cc * jax.nn.sigmoid(cc)).astype(BF16)
        dw_ref[...] = _dot_tn(ca, ds_ref[...].astype(BF16))
        db_ref[...] = jnp.sum(da_ref[...], axis=0, keepdims=True)
        for p_ref, s_ref in zip(p_refs, s_refs):
            tot = p_ref[0]
            for j in range(1, p_ref.shape[0]):
                tot = tot + p_ref[j]
            s_ref[...] = tot

    return pl.pallas_call(
        body, name="ada_bwd",
        out_shape=(_sds((d, n), F32), _sds((1, n6), F32)) + tuple(_sds(p.shape[1:], F32) for p in parts),
        compiler_params=pltpu.CompilerParams(vmem_limit_bytes=VMEM_LIMIT),
    )(c_all, dmod_all, dmod_sh, *parts)


def _fwd_in(x, g1, mod3, win_p, tm, tps):
    t, d = x.shape
    p_glu, p_q, npad = _layout(d)

    def body(x_ref, g_ref, mod_ref, w_hbm, h_ref, zm_ref, zglu_ref, zgate_ref, u0_ref, w_ref):
        _load_resident(pl.program_id(0), [(w_hbm, w_ref)])
        n, _ = _rms(x_ref[...])
        h = ((n * g_ref[...]) * (1.0 + mod_ref[1:2, :]) + mod_ref[0:1, :]).astype(BF16)
        h_ref[...] = h
        z = _dot(h, w_ref[...])
        zgate_ref[...] = z[:, :p_glu]
        zglu = z[:, p_glu:p_q]
        zglu_ref[...] = zglu
        zm_ref[...] = z[:, p_q:]
        u0_ref[...] = zglu[:, :CONV_CH] * jax.nn.sigmoid(zglu[:, CONV_CH:])

    return pl.pallas_call(
        body, name="fwd_in", grid=(t // tm,),
        out_shape=(_sds((t, d), BF16), _sds((t, MLA_IN), F32), _sds((t, 2 * CONV_CH), F32), _sds((t, 2 * d), F32),
                   _sds((t, CONV_CH), F32)),
        in_specs=[_row(tm, d), _full((1, d)), _modspec(d, tps), ANY],
        out_specs=(_row(tm, d), _row(tm, MLA_IN), _row(tm, 2 * CONV_CH), _row(tm, 2 * d), _row(tm, CONV_CH)),
        scratch_shapes=[pltpu.VMEM(win_p.shape, BF16)],
        compiler_params=_params("arbitrary"),
    )(x, g1, mod3, win_p)


def _mla_prep(zm, gql, gkvl, gq, gk, tabs, wuq_p, wk_p, wv_p, tm, tps):
    t = zm.shape[0]
    c_t, s1_t, s2_t = tabs
    tab = pl.BlockSpec((tm, LANES), lambda i: (i % tps, 0))

    def body(zm_ref, gql_ref, gkvl_ref, gq_ref, gk_ref, c_ref, s1_ref, s2_ref, wuq_ref, wk_ref, wv_ref,
             q_ref, k_ref, v_ref, qln_ref, kvn_ref):
        c, s1, s2 = c_ref[...], s1_ref[...], s2_ref[...]
        nq, _ = _rms(zm_ref[:, :Q_RANK])
        qln = (nq * gql_ref[...]).astype(BF16)
        qln_ref[...] = qln
        qpre = _dot(qln, wuq_ref[...])
        nkv, _ = _rms(zm_ref[:, Q_RANK:OFF_KV])
        kvn = (nkv * gkvl_ref[...]).astype(BF16)
        kvn_ref[...] = kvn
        knope = _dot(kvn, wk_ref[...])
        v_ref[...] = _dot(kvn, wv_ref[...]).astype(BF16)
        zkr_v = zm_ref[:, OFF_KV:]
        kr_roped = _rope(zkr_v * gk_ref[...], c, s1, s2)
        slabs = [slice(hd * LANES, (hd + 1) * LANES) for hd in range(N_HEADS)]
        rq = [_head_rms(qpre[:, sl])[1] for sl in slabs]
        rk = [_head_rms(knope[:, sl] + zkr_v)[1] for sl in slabs]
        for hd, sl in enumerate(slabs):
            q_ref[:, sl] = _rope((qpre[:, sl] * rq[hd]) * gq_ref[...], c, s1, s2).astype(BF16)
            k_ref[:, sl] = (rk[hd] * (knope[:, sl] * gk_ref[...] + kr_roped)).astype(BF16)

    return pl.pallas_call(
        body, name="mla_prep", grid=(t // tm,),
        out_shape=(_sds((t, HW), BF16),) * 3 + (_sds((t, Q_RANK), BF16), _sds((t, KV_RANK), BF16)),
        in_specs=[_row(tm, MLA_IN), _full((1, Q_RANK)), _full((1, KV_RANK)),
                  _full((1, LANES)), _full((1, LANES)), tab, tab, tab,
                  _full(wuq_p.shape), _full(wk_p.shape), _full(wv_p.shape)],
        out_specs=(_row(tm, HW),) * 3 + (_row(tm, Q_RANK), _row(tm, KV_RANK)),
        compiler_params=_params("arbitrary"),
    )(zm, gql, gkvl, gq, gk, c_t, s1_t, s2_t, wuq_p, wk_p, wv_p)


AHEAD = 2
ROW_BAND = 256
SM_SCALE = QK_HEAD ** -0.5
EXP2_SCALE = SM_SCALE * 1.4426950408889634


def _diag_mask():
    rc = jnp.right_shift(lax.broadcasted_iota(jnp.int32, (BQ, 1), 0), CHUNK_SHIFT)
    cc = jnp.right_shift(lax.broadcasted_iota(jnp.int32, (1, BQ), 1), CHUNK_SHIFT)
    return rc >= cc


def _scores(q_i, k_ref, lo, e):
    return (_dot_nt(q_i, k_ref[:lo, :]) if lo else None), _dot_nt(q_i, k_ref[lo:e, :])


def _softmax_parts(scores, mask):
    sp, sd = scores
    sd = jnp.where(mask, sd, jnp.finfo(F32).min)
    m = jnp.max(sd, axis=-1, keepdims=True)
    if sp is not None:
        m = jnp.maximum(m, jnp.max(sp, axis=-1, keepdims=True))
    pd = jnp.exp2((sd - m) * EXP2_SCALE)
    l = jnp.sum(pd, axis=-1, keepdims=True)
    pp = None
    if sp is not None:
        pp = jnp.exp2((sp - m) * EXP2_SCALE)
        l = l + jnp.sum(pp, axis=-1, keepdims=True)
    return pp, pd, l


def _attn_fwd(q, k, v, nseq, seq, gather=()):
    t = q.shape[0]
    na = len(gather)
    blk = pl.BlockSpec((seq, LANES), lambda b, h: (b, h))
    n_steps = nseq * N_HEADS

    def body(q_ref, k_ref, v_ref, *rest):
        o_ref = rest[na]
        if na:
            start, forward, finish = _gather8_phases(rest[:na], rest[na + 1:2 * na + 1], *rest[2 * na + 1:])
            step = pl.program_id(0) * N_HEADS + pl.program_id(1)
            pl.when(step == 0)(start)
            pl.when(step == (7 * n_steps) // 8)(forward)
        mask = _diag_mask()
        nb = seq // BQ
        block_scores = lambda j: _scores(q_ref[j * BQ:(j + 1) * BQ, :], k_ref, j * BQ, (j + 1) * BQ)
        ahead = [block_scores(j) for j in range(min(AHEAD, nb))]
        for i in range(nb):
            lo, e = i * BQ, (i + 1) * BQ
            cur = ahead.pop(0)
            if i + AHEAD < nb:
                ahead.append(block_scores(i + AHEAD))
            pp, pd, l = _softmax_parts(cur, mask)
            o = _dot(pd.astype(BF16), v_ref[lo:e, :])
            if lo:
                o = o + _dot(pp.astype(BF16), v_ref[:lo, :])
            o_ref[lo:e, :] = (o * (1.0 / l)).astype(BF16)
        if na:
            pl.when(step == n_steps - 1)(finish)

    res = pl.pallas_call(
        body, name="attn_fwd", grid=(nseq, N_HEADS), out_shape=(_sds((t, HW), BF16),) + _gather8_shapes(gather),
        in_specs=[blk, blk, blk] + [ANY] * na, out_specs=(blk,) + (ANY,) * na,
        scratch_shapes=_gather8_sems(na) if na else [],
        compiler_params=_params("arbitrary", "arbitrary"),
    )(q, k, v, *gather)
    return res[0], (_own_block_placed(res[1:], gather) if na else ())


def _fwd_mix(attn, u0, zgate, x, mod3, wo_p, cw, cb, lng, lnb, wpw, wout, tm, tps):
    t, d = x.shape
    hpt = tm // HALO
    cwc, cbc = _by_lane_chunk(cw), _by_lane_chunk(cb)

    def body(a_ref, u_ref, uh_ref, zg_ref, x_ref, mod_ref, wo_ref, cw_ref, cb_ref, lng_ref, lnb_ref, wpw_ref, wout_ref,
             x1_ref, mixed_ref, mpre_ref, ya_ref, yb_ref, u1_ref, u3_ref, ext_ref):
        i = pl.program_id(0)
        first = (i % tps) == 0
        _fill_shifted(ext_ref, jnp.where(first, 0.0, uh_ref[...]), u_ref[...])
        nb = max(tm // ROW_BAND, 1)
        bw = tm // nb
        bands = [slice(b * bw, (b + 1) * bw) for b in range(nb)]
        yas = [_dot(a_ref[rows, :], wo_ref[...]) for rows in bands]
        u3s = []
        for b, rows in enumerate(bands):
            ya_ref[rows, :] = yas[b]
            for lc, ls in _lane_chunks():
                acc = jnp.broadcast_to(cb_ref[lc], (bw, LANES))
                for kk in range(CONV_W):
                    o = HALO - (CONV_W - 1) + kk
                    a = (o // SUBLANES) * SUBLANES + b * bw
                    acc = acc + cw_ref[lc, kk:kk + 1, :] * ext_ref[o % SUBLANES, lc, a:a + bw, :]
                u1_ref[rows, ls] = acc
            acc = u1_ref[rows, :]
            mu = jnp.mean(acc, axis=-1, keepdims=True)
            xc = acc - mu
            rstd = lax.rsqrt(jnp.mean(xc * xc, axis=-1, keepdims=True) + EPS)
            l = (xc * rstd) * lng_ref[...] + lnb_ref[...]
            u3 = (l * jax.nn.sigmoid(l)).astype(BF16)
            u3_ref[rows, :] = u3
            u3s.append(u3)
        ybs = [_dot(u3, wpw_ref[...]) for u3 in u3s]
        mpres = []
        for b, rows in enumerate(bands):
            yb_ref[rows, :] = ybs[b]
            mpre = (jax.nn.sigmoid(zg_ref[rows, :d]) * yas[b] + jax.nn.sigmoid(zg_ref[rows, d:]) * ybs[b]).astype(BF16)
            mpre_ref[rows, :] = mpre
            mpres.append(mpre)
        for rows, mpre in zip(bands, mpres):
            mixed = _dot(mpre, wout_ref[...])
            mixed_ref[rows, :] = mixed
            x1_ref[rows, :] = x_ref[rows, :] + mod_ref[2:3, :] * mixed

    halo = pl.BlockSpec((HALO, CONV_CH), lambda i: (jnp.maximum(i * hpt - 1, 0), 0))
    return pl.pallas_call(
        body, name="fwd_mix", grid=(t // tm,),
        out_shape=(_sds((t, d), F32), _sds((t, d), F32), _sds((t, d), BF16), _sds((t, d), F32), _sds((t, d), F32),
                   _sds((t, CONV_CH), F32), _sds((t, CONV_CH), BF16)),
        in_specs=[_row(tm, HW), _row(tm, CONV_CH), halo, _row(tm, 2 * d), _row(tm, d), _modspec(d, tps),
                  _full(wo_p.shape), _full(cwc.shape), _full(cbc.shape), _full((1, CONV_CH)), _full((1, CONV_CH)),
                  _full(wpw.shape), _full(wout.shape)],
        out_specs=(_row(tm, d), _row(tm, d), _row(tm, d), _row(tm, d), _row(tm, d), _row(tm, CONV_CH),
                   _row(tm, CONV_CH)),
        scratch_shapes=[pltpu.VMEM(_shifted_shape(tm), F32)],
        compiler_params=_params("arbitrary"),
    )(attn, u0, u0, zgate, x, mod3, wo_p, cwc, cbc, lng, lnb, wpw, wout)


def _shards_into_columns(w_hbm, w_ref):
    ns = w_hbm.shape[2]
    return [(w_hbm.at[s], w_ref.at[:, pl.ds(s * ns, ns)]) for s in range(w_hbm.shape[0])]


def _fwd_ffn(x1, target, g2, mod3, w1, w2, tm, tps):
    t, d = x1.shape
    dff = w1.shape[0] * w1.shape[2]

    def body(x1_ref, tg_ref, g_ref, mod_ref, w1_hbm, w2_hbm,
             h2_ref, a_ref, r_ref, dy_ref, df_ref, dgate_ref, loss_ref, w1_ref, w2_ref):
        i = pl.program_id(0)
        _load_resident(i, _shards_into_columns(w1_hbm, w1_ref) + [(w2_hbm, w2_ref)])
        x1v = x1_ref[...]
        gate2 = mod_ref[5:6, :]
        n, _ = _rms(x1v)
        h2 = ((n * g_ref[...]) * (1.0 + mod_ref[4:5, :]) + mod_ref[3:4, :]).astype(BF16)
        h2_ref[...] = h2
        a = _dot(h2, w1_ref[...])
        a_ref[...] = a
        r = jnp.square(jnp.maximum(a, 0.0)).astype(BF16)
        r_ref[...] = r
        f = _dot(r, w2_ref[...])
        e = (x1v + gate2 * f) - tg_ref[...]
        part = 0.5 * jnp.sum(jnp.mean(e * e, axis=-1, keepdims=True), axis=0, keepdims=True)
        _acc(loss_ref, jnp.broadcast_to(part, loss_ref.shape), i == 0)
        dy = e * (1.0 / d)
        dy_ref[...] = dy
        df_ref[...] = (dy * gate2).astype(BF16)
        _acc(dgate_ref, jnp.sum(dy * f, axis=0, keepdims=True), (i % tps) == 0)

    nseq = t // (tm * tps)
    return pl.pallas_call(
        body, name="fwd_ffn", grid=(t // tm,),
        out_shape=(_sds((t, d), BF16), _sds((t, dff), F32), _sds((t, dff), BF16), _sds((t, d), F32), _sds((t, d), BF16),
                   _sds((nseq, 1, d), F32), _sds((8, LANES), F32)),
        in_specs=[_row(tm, d), _row(tm, d), _full((1, d)), _modspec(d, tps), ANY, ANY],
        out_specs=(_row(tm, d), _row(tm, dff), _row(tm, dff), _row(tm, d), _row(tm, d), _seqv(d, tps),
                   _full((8, LANES))),
        scratch_shapes=[pltpu.VMEM((d, dff), BF16), pltpu.VMEM(w2.shape, BF16)],
        compiler_params=_params("arbitrary"),
    )(x1, target, g2, mod3, w1, w2)


def _bwd_ffn(df, a, x1, dy, mixed, g2, mod3, w2, w1, tm, tps):
    t, d = x1.shape
    dff = a.shape[1]

    def body(df_ref, a_ref, x1_ref, dy_ref, mx_ref, g_ref, mod_ref, w2_hbm, w1_hbm,
             da_ref, dx1_ref, dmixed_ref, dshift_ref, dscale_ref, dgate1_ref, dg2_ref, w2_ref, w1_ref):
        i = pl.program_id(0)
        _load_resident(i, [(w2_hbm, w2_ref)] + _shards_into_columns(w1_hbm, w1_ref))
        first_seq = (i % tps) == 0
        dr = _dot_nt(df_ref[...], w2_ref[...])
        da = (dr * (2.0 * jnp.maximum(a_ref[...], 0.0))).astype(BF16)
        da_ref[...] = da
        dh2 = _dot_nt(da, w1_ref[...])
        n, r = _rms(x1_ref[...])
        g = g_ref[...]
        sc1 = 1.0 + mod_ref[4:5, :]
        _acc(dshift_ref, jnp.sum(dh2, axis=0, keepdims=True), first_seq)
        _acc(dscale_ref, jnp.sum(dh2 * (n * g), axis=0, keepdims=True), first_seq)
        _acc(dg2_ref, jnp.sum((dh2 * sc1) * n, axis=0, keepdims=True), i == 0)
        dx1 = dy_ref[...] + _rms_bwd(n, r, (dh2 * sc1) * g)
        dx1_ref[...] = dx1
        _acc(dgate1_ref, jnp.sum(dx1 * mx_ref[...], axis=0, keepdims=True), first_seq)
        dmixed_ref[...] = (dx1 * mod_ref[2:3, :]).astype(BF16)

    nseq = t // (tm * tps)
    sv = _sds((nseq, 1, d), F32)
    return pl.pallas_call(
        body, name="bwd_ffn", grid=(t // tm,),
        out_shape=(_sds((t, dff), BF16), _sds((t, d), F32), _sds((t, d), BF16), sv, sv, sv, _sds((1, d), F32)),
        in_specs=[_row(tm, d), _row(tm, dff), _row(tm, d), _row(tm, d), _row(tm, d), _full((1, d)), _modspec(d, tps),
                  ANY, ANY],
        out_specs=(_row(tm, dff), _row(tm, d), _row(tm, d), _seqv(d, tps), _seqv(d, tps), _seqv(d, tps),
                   _full((1, d))),
        scratch_shapes=[pltpu.VMEM(w2.shape, BF16), pltpu.VMEM((d, dff), BF16)],
        compiler_params=_params("arbitrary"),
    )(df, a, x1, dy, mixed, g2, mod3, w2, w1)


def _bwd_mix(dmixed, zgate, ya, yb, u1, lng, lnb, wout, wo_p, wpw, tm, swap=()):
    t, d = ya.shape
    _, _, npad = _layout(d)
    nw = len(swap)
    n_steps = t // tm

    def body(dm_ref, zg_ref, ya_ref, yb_ref, u1_ref, lng_ref, lnb_ref, wout_ref, wo_ref, wpw_ref, *rest):
        dya_ref, dyb_ref, dz_ref, do_ref, du1_ref, dlng_ref, dlnb_ref, dcb_ref = rest[nw:nw + 8]
        i = pl.program_id(0)
        if nw:
            start, finish = _swap_phases(rest[:nw], rest[nw + 8:2 * nw + 8], *rest[2 * nw + 8:])
            pl.when(i == 0)(start)
        nb = max(tm // ROW_BAND, 1)
        bands = [slice(b * (tm // nb), (b + 1) * (tm // nb)) for b in range(nb)]
        col = lambda v: jnp.sum(v, axis=0, keepdims=True)
        dmpre = [_dot_nt(dm_ref[rows, :], wout_ref[...]) for rows in bands]
        dyab = []
        for rows, dmp in zip(bands, dmpre):
            ga = jax.nn.sigmoid(zg_ref[rows, :d])
            gb = jax.nn.sigmoid(zg_ref[rows, d:])
            dya = (dmp * ga).astype(BF16)
            dyb = (dmp * gb).astype(BF16)
            dya_ref[rows, :] = dya
            dyb_ref[rows, :] = dyb
            dz_ref[rows, :d] = ((dmp * ya_ref[rows, :]) * (ga * (1.0 - ga))).astype(BF16)
            dz_ref[rows, d:] = ((dmp * yb_ref[rows, :]) * (gb * (1.0 - gb))).astype(BF16)
            dyab.append((dya, dyb))
        du3s = []
        for rows, (dya, dyb) in zip(bands, dyab):
            do_ref[rows, :] = _dot_nt(dya, wo_ref[...]).astype(BF16)
            du3s.append(_dot_nt(dyb, wpw_ref[...]))
        sums = [jnp.zeros((1, CONV_CH), F32)] * 3
        for rows, du3 in zip(bands, du3s):
            u1 = u1_ref[rows, :]
            mu = jnp.mean(u1, axis=-1, keepdims=True)
            xc = u1 - mu
            rstd = lax.rsqrt(jnp.mean(xc * xc, axis=-1, keepdims=True) + EPS)
            nh = xc * rstd
            l = nh * lng_ref[...] + lnb_ref[...]
            sg = jax.nn.sigmoid(l)
            dl = du3 * (sg * (1.0 + l * (1.0 - sg)))
            dnh = dl * lng_ref[...]
            du1 = rstd * (dnh - jnp.mean(dnh, axis=-1, keepdims=True)
                          - nh * jnp.mean(dnh * nh, axis=-1, keepdims=True))
            du1_ref[rows, :] = du1
            sums = [sums[0] + col(dl * nh), sums[1] + col(dl), sums[2] + col(du1)]
        _acc(dlng_ref, sums[0], i == 0)
        _acc(dlnb_ref, sums[1], i == 0)
        _acc(dcb_ref, sums[2], i == 0)
        if nw:
            pl.when(i == n_steps - 1)(finish)

    cv = _sds((1, CONV_CH), F32)
    res = pl.pallas_call(
        body, name="bwd_mix", grid=(n_steps,),
        out_shape=(_sds((t, d), BF16), _sds((t, d), BF16), _sds((t, npad), BF16), _sds((t, HW), BF16),
                   _sds((t, CONV_CH), F32), cv, cv, cv) + _swap_shapes(swap),
        in_specs=[_row(tm, d), _row(tm, 2 * d), _row(tm, d), _row(tm, d), _row(tm, CONV_CH), _full((1, CONV_CH)),
                  _full((1, CONV_CH)), _full(wout.shape), _full(wo_p.shape), _full(wpw.shape)] + [ANY] * nw,
        out_specs=(_row(tm, d), _row(tm, d), _row(tm, 2 * d), _row(tm, HW), _row(tm, CONV_CH),
                   _full((1, CONV_CH)), _full((1, CONV_CH)), _full((1, CONV_CH))) + (ANY,) * nw,
        scratch_shapes=_swap_sems(swap) if nw else [],
        compiler_params=_params("arbitrary"),
    )(dmixed, zgate, ya, yb, u1, lng, lnb, wout, wo_p, wpw, *swap)
    return res[:8] + (res[8:],)


def _bwd_conv(dz, du1, u0, zglu, cw, tm, tps):
    t = du1.shape[0]
    d = (dz.shape[1] - MLA_IN - 2 * CONV_CH) // 2
    p_glu, _, _ = _layout(d)
    hpt = tm // HALO
    last_blk = t // HALO - 1
    cwc = _by_lane_chunk(cw)

    def body(dz_hbm, du_ref, dun_ref, u_ref, zl_ref, cw_ref, dzl_ref, dcw_ref, dext_ref, uc_ref, dcw8_ref, du0_ref):
        i = pl.program_id(0)
        last = (i % tps) == (tps - 1)
        _fill_shifted(dext_ref, du_ref[...], jnp.where(last, 0.0, dun_ref[...]))
        for lc, ls in _lane_chunks():
            uc_ref[lc] = u_ref[:, ls]

        @pl.when(i == 0)
        def _():
            dcw8_ref[...] = jnp.zeros_like(dcw8_ref)

        groups = CONV_ROWS // SUBLANES

        def conv_chunk(c, carry):
            lc, r0 = _conv_chunk(c)
            u = uc_ref[lc, pl.ds(r0, CONV_ROWS), :]
            du0 = jnp.zeros((CONV_ROWS, LANES), F32)
            for kk in range(CONV_W):
                win = _shifted(dext_ref, CONV_W - 1 - kk, lc, r0)
                prod = u * win
                part = prod[:SUBLANES]
                for g in range(1, groups):
                    part = part + prod[g * SUBLANES:(g + 1) * SUBLANES]
                dcw8_ref[lc, kk] += part
                du0 = du0 + cw_ref[lc, kk:kk + 1, :] * win
            du0_ref[lc, pl.ds(r0, CONV_ROWS), :] = du0
            return carry

        lax.fori_loop(0, CONV_LC * (tm // CONV_ROWS), conv_chunk, 0)

        @pl.when(i == pl.num_programs(0) - 1)
        def _():
            for lc, ls in _lane_chunks():
                dcw_ref[:, ls] = jnp.sum(dcw8_ref[lc], axis=1)

        for lc, ls in _lane_chunks():
            du0 = du0_ref[lc]
            ga = zl_ref[:, ls]
            sb = jax.nn.sigmoid(zl_ref[:, CONV_CH + lc * LANES:CONV_CH + (lc + 1) * LANES])
            dzl_ref[:, ls] = (du0 * sb).astype(BF16)
            dzl_ref[:, CONV_CH + lc * LANES:CONV_CH + (lc + 1) * LANES] = ((du0 * ga) * (sb * (1.0 - sb))).astype(BF16)

    nxt = pl.BlockSpec((HALO, CONV_CH), lambda i: (jnp.minimum((i + 1) * hpt, last_blk), 0))
    glu_blk = p_glu // (2 * CONV_CH)
    return pl.pallas_call(
        body, name="bwd_conv", grid=(t // tm,),
        out_shape=(_sds(dz.shape, BF16), _sds(cw.shape, F32)),
        in_specs=[ANY, _row(tm, CONV_CH), nxt, _row(tm, CONV_CH), _row(tm, 2 * CONV_CH), _full(cwc.shape)],
        out_specs=(pl.BlockSpec((tm, 2 * CONV_CH), lambda i: (i, glu_blk)), _full(cw.shape)),
        scratch_shapes=[pltpu.VMEM(_shifted_shape(tm), F32), pltpu.VMEM((CONV_LC, tm, LANES), F32),
                        pltpu.VMEM((CONV_LC, HALO, SUBLANES, LANES), F32), pltpu.VMEM((CONV_LC, tm, LANES), F32)],
        input_output_aliases={0: 0},
        compiler_params=_params("arbitrary"),
    )(dz, du1, du1, u0, zglu, cwc)


def _attn_bwd(q, k, v, do, nseq, seq, scatter=()):
    t = q.shape[0]
    ns = len(scatter)
    blk = pl.BlockSpec((seq, LANES), lambda b, h: (b, h))
    n_steps = nseq * N_HEADS

    def body(q_ref, k_ref, v_ref, do_ref, *rest):
        dq_ref, dk_ref, dv_ref = rest[ns:ns + 3]
        dka_ref, dva_ref = rest[2 * ns + 3:2 * ns + 5]
        if ns:
            start, finish = _scatter_phases(rest[:ns], rest[ns + 3:2 * ns + 3], *rest[2 * ns + 5:])
            step = pl.program_id(0) * N_HEADS + pl.program_id(1)
            pl.when(step == 0)(start)
        dka_ref[...] = jnp.zeros_like(dka_ref)
        dva_ref[...] = jnp.zeros_like(dva_ref)
        mask = _diag_mask()
        nb = seq // BQ
        block = lambda j: (_scores(q_ref[j * BQ:(j + 1) * BQ, :], k_ref, j * BQ, (j + 1) * BQ),
                           _scores(do_ref[j * BQ:(j + 1) * BQ, :], v_ref, j * BQ, (j + 1) * BQ))
        ahead = [block(j) for j in range(min(AHEAD, nb))]

        def second_stage(lo, e, dsd, dsp, pdb, ppb):
            q_i = q_ref[lo:e, :]
            do_i = do_ref[lo:e, :]
            dq = _dot(dsd, k_ref[lo:e, :])
            dka_ref[lo:e, :] += _dot_tn(dsd, q_i)
            dva_ref[lo:e, :] += _dot_tn(pdb, do_i)
            if lo:
                dq = dq + _dot(dsp, k_ref[:lo, :])
                dka_ref[:lo, :] += _dot_tn(dsp, q_i)
                dva_ref[:lo, :] += _dot_tn(ppb, do_i)
            dq_ref[lo:e, :] = dq * SM_SCALE

        held = None
        for i in range(nb):
            lo, e = i * BQ, (i + 1) * BQ
            scores, (dpp, dpd) = ahead.pop(0)
            if i + AHEAD < nb:
                ahead.append(block(i + AHEAD))
            pp, pd, l = _softmax_parts(scores, mask)
            inv = 1.0 / l
            pd = pd * inv
            delta = jnp.sum(pd * dpd, axis=-1, keepdims=True)
            if lo:
                pp = pp * inv
                delta = delta + jnp.sum(pp * dpp, axis=-1, keepdims=True)
            dsd = (pd * (dpd - delta)).astype(BF16)
            dsp = (pp * (dpp - delta)).astype(BF16) if lo else None
            if held is not None:
                second_stage(*held)
            held = (lo, e, dsd, dsp, pd.astype(BF16), pp.astype(BF16) if lo else None)
        second_stage(*held)
        dk_ref[...] = dka_ref[...] * SM_SCALE
        dv_ref[...] = dva_ref[...].astype(BF16)
        if ns:
            pl.when(step == n_steps - 1)(finish)

    res = pl.pallas_call(
        body, name="attn_bwd", grid=(nseq, N_HEADS),
        out_shape=(_sds((t, HW), F32), _sds((t, HW), F32), _sds((t, HW), BF16)) + _scatter_shapes(scatter),
        in_specs=[blk] * 4 + [ANY] * ns, out_specs=(blk,) * 3 + (ANY,) * ns,
        scratch_shapes=[pltpu.VMEM((seq, LANES), F32), pltpu.VMEM((seq, LANES), F32)]
        + (_scatter_sems(ns) if ns else []),
        compiler_params=_params("arbitrary", "arbitrary"),
    )(q, k, v, do, *scatter)
    return res[0], res[1], res[2], res[3:]


def _mla_bwd(dz, dq, dk, dv, zm, gql, gkvl, gq, gk, tabs, wuq_p, wk_p, wv_p, tm, tps):
    t = zm.shape[0]
    d = (dz.shape[1] - MLA_IN - 2 * CONV_CH) // 2
    _, p_q, _ = _layout(d)
    c_t, s1_t, s2_t = tabs
    tab = pl.BlockSpec((tm, LANES), lambda i: (i % tps, 0))

    def body(dz_hbm, dq_ref, dk_ref, dv_ref, zm_ref, gql_ref, gkvl_ref, gq_ref, gk_ref, c_ref, s1_ref, s2_ref,
             wuq_ref, wk_ref, wv_ref,
             dzm_ref, dqpre_ref, dkh_ref, dgq_ref, dgk_ref, dgql_ref, dgkvl_ref, dwq_ref, dwk_ref, dwv_ref):
        i = pl.program_id(0)
        c, s1, s2 = c_ref[...], s1_ref[...], s2_ref[...]
        nq, rq = _rms(zm_ref[:, :Q_RANK])
        qln = (nq * gql_ref[...]).astype(BF16)
        qpre = _dot(qln, wuq_ref[...])
        nkv, rkv = _rms(zm_ref[:, Q_RANK:OFF_KV])
        kvn = (nkv * gkvl_ref[...]).astype(BF16)
        knope = _dot(kvn, wk_ref[...])
        zkr_v = zm_ref[:, OFF_KV:]
        gk = gk_ref[...]
        kr_roped = _rope(zkr_v * gk, c, s1, s2)
        dgq = jnp.zeros((1, LANES), F32)
        dgk = jnp.zeros((1, LANES), F32)
        dzkr = jnp.zeros((tm, LANES), F32)
        dt_sum = jnp.zeros((tm, LANES), F32)
        slabs = [slice(hd * LANES, (hd + 1) * LANES) for hd in range(N_HEADS)]
        gq = gq_ref[...]
        rqh = [_head_rms(qpre[:, sl])[1] for sl in slabs]
        rkh = [_head_rms(knope[:, sl] + zkr_v)[1] for sl in slabs]
        dyr = [_rope_t(dq_ref[:, sl], c, s1, s2) for sl in slabs]
        nqh = [qpre[:, sl] * rqh[hd] for hd, sl in enumerate(slabs)]
        sq = [jnp.sum((dyr[hd] * gq) * nqh[hd], axis=-1, keepdims=True) for hd in range(N_HEADS)]
        dr = [jnp.sum(dk_ref[:, sl] * (knope[:, sl] * gk + kr_roped), axis=-1, keepdims=True) for sl in slabs]
        for hd, sl in enumerate(slabs):
            dgq = dgq + jnp.sum(dyr[hd] * nqh[hd], axis=0, keepdims=True)
            dqpre_ref[:, sl] = (rqh[hd] * (dyr[hd] * gq - nqh[hd] * (sq[hd] * (1.0 / QK_HEAD)))).astype(BF16)
            kn = knope[:, sl]
            r = rkh[hd]
            dt = dk_ref[:, sl] * r
            via_r = (dr[hd] * (r * r * r) * (-1.0 / QK_HEAD)) * (kn + zkr_v)
            dgk = dgk + jnp.sum(dt * kn, axis=0, keepdims=True)
            dt_sum = dt_sum + dt
            dzkr = dzkr + via_r
            dkh_ref[:, sl] = (dt * gk + via_r).astype(BF16)
        de = _rope_t(dt_sum, c, s1, s2)
        dzkr = dzkr + de * gk
        dgk = dgk + jnp.sum(de * zkr_v, axis=0, keepdims=True)
        _acc(dgq_ref, dgq[:, :QK_HEAD], i == 0)
        _acc(dgk_ref, dgk[:, :QK_HEAD], i == 0)
        dzm_ref[:, OFF_KV:] = dzkr.astype(BF16)
        dqln = _dot_nt(dqpre_ref[...], wuq_ref[...])
        _acc(dgql_ref, jnp.sum(dqln * nq, axis=0, keepdims=True), i == 0)
        dzm_ref[:, :Q_RANK] = _rms_bwd(nq, rq, dqln * gql_ref[...]).astype(BF16)
        dkvn = _dot_nt(dkh_ref[...], wk_ref[...]) + _dot_nt(dv_ref[...], wv_ref[...])
        _acc(dgkvl_ref, jnp.sum(dkvn * nkv, axis=0, keepdims=True), i == 0)
        dzm_ref[:, Q_RANK:OFF_KV] = _rms_bwd(nkv, rkv, dkvn * gkvl_ref[...]).astype(BF16)
        _acc(dwq_ref, _dot_tn(qln, dqpre_ref[...]), i == 0)
        _acc(dwk_ref, _dot_tn(kvn, dkh_ref[...]), i == 0)
        _acc(dwv_ref, _dot_tn(kvn, dv_ref[...]), i == 0)

    return pl.pallas_call(
        body, name="mla_bwd", grid=(t // tm,),
        out_shape=(_sds(dz.shape, BF16), _sds((t, HW), BF16), _sds((t, HW), BF16), _sds((1, QK_HEAD), F32),
                   _sds((1, QK_HEAD), F32), _sds((1, Q_RANK), F32), _sds((1, KV_RANK), F32),
                   _sds((Q_RANK, HW), F32), _sds((KV_RANK, HW), F32), _sds((KV_RANK, HW), F32)),
        in_specs=[ANY, _row(tm, HW), _row(tm, HW), _row(tm, HW), _row(tm, MLA_IN),
                  _full((1, Q_RANK)), _full((1, KV_RANK)), _full((1, LANES)), _full((1, LANES)), tab, tab, tab,
                  _full(wuq_p.shape), _full(wk_p.shape), _full(wv_p.shape)],
        out_specs=(pl.BlockSpec((tm, MLA_IN), lambda i: (i, p_q // MLA_IN)), _row(tm, HW), _row(tm, HW),
                   _full((1, QK_HEAD)), _full((1, QK_HEAD)), _full((1, Q_RANK)), _full((1, KV_RANK)),
                   _full((Q_RANK, HW)), _full((KV_RANK, HW)), _full((KV_RANK, HW))),
        input_output_aliases={0: 0},
        compiler_params=_params("arbitrary"),
    )(dz, dq, dk, dv, zm, gql, gkvl, gq, gk, c_t, s1_t, s2_t, wuq_p, wk_p, wv_p)


def _bwd_in(dz, x, dx1, g1, mod3, win_p, tm, tps, scatter=()):
    t, d = x.shape
    npad = dz.shape[1]

    ns = len(scatter)
    n_steps = t // tm

    def body(dz_ref, x_ref, dx1_ref, g_ref, mod_ref, wt_hbm, *rest):
        gx_ref, dshift_ref, dscale_ref, dg1_ref = rest[ns:ns + 4]
        wt_ref = rest[2 * ns + 4]
        i = pl.program_id(0)
        if ns:
            start, finish = _scatter_phases(rest[:ns], rest[ns + 4:2 * ns + 4], *rest[2 * ns + 5:])
            pl.when(i == 0)(start)
        _load_resident(i, [(wt_hbm, wt_ref)])
        first_seq = (i % tps) == 0
        g = g_ref[...]
        sc1 = 1.0 + mod_ref[1:2, :]
        nb = max(tm // ROW_BAND, 1)
        bands = [slice(b * (tm // nb), (b + 1) * (tm // nb)) for b in range(nb)]
        dhs = [_dot_nt(dz_ref[rows, :], wt_ref[...]) for rows in bands]
        sums = [jnp.zeros((1, d), F32)] * 3
        col = lambda v: jnp.sum(v, axis=0, keepdims=True)
        for rows, dh in zip(bands, dhs):
            n, r = _rms(x_ref[rows, :])
            sums = [sums[0] + col(dh), sums[1] + col(dh * (n * g)), sums[2] + col((dh * sc1) * n)]
            gx_ref[rows, :] = dx1_ref[rows, :] + _rms_bwd(n, r, (dh * sc1) * g)
        _acc(dshift_ref, sums[0], first_seq)
        _acc(dscale_ref, sums[1], first_seq)
        _acc(dg1_ref, sums[2], i == 0)
        if ns:
            pl.when(i == n_steps - 1)(finish)

    nseq = t // (tm * tps)
    sv = _sds((nseq, 1, d), F32)
    res = pl.pallas_call(
        body, name="bwd_in", grid=(n_steps,),
        out_shape=(_sds((t, d), F32), sv, sv, _sds((1, d), F32)) + _scatter_shapes(scatter),
        in_specs=[_row(tm, npad), _row(tm, d), _row(tm, d), _full((1, d)), _modspec(d, tps), ANY] + [ANY] * ns,
        out_specs=(_row(tm, d), _seqv(d, tps), _seqv(d, tps), _full((1, d))) + (ANY,) * ns,
        scratch_shapes=[pltpu.VMEM(win_p.shape, BF16)] + (_scatter_sems(ns) if ns else []),
        compiler_params=_params("arbitrary"),
    )(dz, x, dx1, g1, mod3, win_p, *scatter)
    return res[0], res[1], res[2], res[3], res[4:]


def _tile_of(n, choices):
    for c in choices:
        if n % c == 0:
            return c
    return n


def _tn_matmul(a, b, name, col_shards=0):
    t, k = a.shape
    n = b.shape[1]
    tk = _tile_of(k, (1024, 512, 256, 128))
    tn = n // col_shards if col_shards else _tile_of(n, (1024, 896, 768, 512, 384, 256, 128))
    tt = _tile_of(t, (4096, 2048, 1024, 512, 256))

    def body(a_ref, b_ref, o_ref):
        _acc(o_ref, _dot_tn(a_ref[...], b_ref[...]), pl.program_id(2) == 0)

    if col_shards:
        out_shape, out_spec = _sds((col_shards, k, tn), F32), pl.BlockSpec((None, tk, tn), lambda i, j, s: (j, i, 0))
    else:
        out_shape, out_spec = _sds((k, n), F32), pl.BlockSpec((tk, tn), lambda i, j, s: (i, j))
    return pl.pallas_call(
        body, name=name, grid=(k // tk, n // tn, t // tt), out_shape=out_shape,
        in_specs=[pl.BlockSpec((tt, tk), lambda i, j, s: (s, i)), pl.BlockSpec((tt, tn), lambda i, j, s: (s, j))],
        out_specs=out_spec, compiler_params=_params("arbitrary", "arbitrary", "arbitrary"),
    )(a, b)


N_SHARD = 4
COL_SHARDED = ("w_in", "w_uq", "w_ukv", "w_o_mla", "w_pw_out", "w_ff1")
ROW_SHARDED = ("w_out", "w_ff2")
BIG = ("w_in", "w_uq", "w_ukv", "w_o_mla", "w_pw_out", "w_out", "w_ff1", "w_ff2")
SMALL = ("norm1_g", "q_latent_g", "kv_latent_g", "qk_norm_q_g", "qk_norm_k_g", "conv_b", "conv_ln_g", "conv_ln_b",
         "norm2_g")
WEIGHTS = ("w_ada", "b_ada", "norm1_g", "w_in", "q_latent_g", "w_uq", "kv_latent_g", "w_ukv", "qk_norm_q_g",
           "qk_norm_k_g", "w_o_mla", "conv_w", "conv_b", "conv_ln_g", "conv_ln_b", "w_pw_out", "w_out", "norm2_g",
           "w_ff1", "w_ff2")


def _pad_heads(w, width):
    k = w.shape[0]
    w3 = w.reshape(k, N_HEADS, width)
    return jnp.pad(w3, ((0, 0), (0, 0), (0, LANES - width))).reshape(k, HW)


def _unpad_heads(g, width):
    k = g.shape[0]
    return g.reshape(k, N_HEADS, LANES)[:, :, :width].reshape(k, N_HEADS * width)


def _win_segments(d):
    return [(OFF_GLU, OFF_GLU + 2 * d), (OFF_KR, OFF_GLU), (0, OFF_KV), KR_LANE, (OFF_KV, OFF_KR),
            LANES - KR_LANE - QK_ROPE]


def _pad_win(g4):
    _, d, ws = g4.shape
    parts = []
    for seg in _win_segments(d):
        if isinstance(seg, int):
            parts.append(jnp.zeros((d, seg), g4.dtype))
            continue
        a, b = seg
        while a < b:
            s = a // ws
            e = min(b, (s + 1) * ws)
            parts.append(g4[s, :, a - s * ws:e - s * ws])
            a = e
    return jnp.concatenate(parts, axis=1)


def _unpad_win(gp):
    d = gp.shape[0]
    ws = (OFF_GLU + 2 * d) // N_SHARD
    pieces, p = [], 0
    for seg in _win_segments(d):
        if isinstance(seg, int):
            p += seg
        else:
            pieces.append((seg[0], seg[1], p))
            p += seg[1] - seg[0]
    shards = []
    for s in range(N_SHARD):
        lo, hi = s * ws, (s + 1) * ws
        cols = [gp[:, p0 + max(a, lo) - a:p0 + min(b, hi) - a] for a, b, p0 in sorted(pieces) if max(a, lo) < min(b, hi)]
        shards.append(jnp.concatenate(cols, axis=1))
    return jnp.stack(shards)


def _col_shards(g):
    k, n = g.shape
    return g.reshape(k, N_SHARD, n // N_SHARD).transpose(1, 0, 2)


def _from_shards(g, name):
    ns, ks, nn = g.shape
    if name in ROW_SHARDED:
        return g.reshape(ns * ks, nn)
    return g.transpose(1, 0, 2).reshape(ks, ns * nn)


BY_SHARD = ("w_in", "w_ff1")
EARLY = ("w_in", "w_uq", "w_ukv")
LATE = ("w_o_mla", "w_pw_out", "w_out", "w_ff1", "w_ff2")


def _assemble(names, gathered):
    by_shard = {n: g.reshape((N_SHARD, 2 * g.shape[1]) + g.shape[2:]) for n, g in zip(names, gathered)}
    return {n: g if n in BY_SHARD else _from_shards(g, n) for n, g in by_shard.items()}


LARGE = ("w_in", "w_ff1", "w_ff2")
GROUP_A = ("w_out", "w_ff1", "w_ff2")
GROUP_B = ("w_in", "w_uq", "w_ukv", "w_o_mla", "w_pw_out")


def _pair_halves(g):
    return g.reshape(N_SHARD, 2, g.shape[1] // 2, g.shape[2])


def _pair_sums(names, halves, from_sibling):
    if not halves:
        return []
    ix, iy, ic = _place()
    cidx = ic.reshape(1).astype(jnp.int32)
    c_own = jnp.stack([ic, 2 * ix + iy]).astype(jnp.int32)
    out = {n: _add_pair(g, l, c_own, "pair_sum_" + n)
           for n, g, l in zip(names, halves, from_sibling) if n in LARGE}
    small = [j for j, n in enumerate(names) if n not in LARGE]
    if small:
        res = _add_pair_whole([halves[j] for j in small], [from_sibling[j] for j in small], cidx,
                              "pair_sum_small_" + names[small[0]])
        out.update({names[j]: r for j, r in zip(small, res)})
    return [out[n] for n in names]


def _local_step(x, target, mod, sp, w, late=None, tm=256):
    comm = late is not None
    w = dict(w)
    nseq, seq, d = x.shape
    t = nseq * seq
    tps = seq // tm
    xf = x.reshape(t, d)
    tg = target.reshape(t, d)
    mod3 = mod.reshape(nseq, N_MOD, d)

    win_p = _pad_win(w["w_in"])
    wuq_p = _pad_heads(w["w_uq"], QK_HEAD)
    wkv3 = w["w_ukv"].reshape(KV_RANK, N_HEADS, QK_NOPE + V_HEAD)
    wk_p = _pad_heads(wkv3[:, :, :QK_NOPE].reshape(KV_RANK, -1), QK_NOPE)
    wv_p = _pad_heads(wkv3[:, :, QK_NOPE:].reshape(KV_RANK, -1), V_HEAD)
    cw = jnp.pad(w["conv_w"], ((0, HALO - CONV_W), (0, 0)))
    pad_g = lambda g: jnp.pad(g, ((0, 0), (0, LANES - QK_HEAD)))
    gq, gk = pad_g(sp["qk_norm_q_g"]), pad_g(sp["qk_norm_k_g"])
    tabs = _rope_tables(seq)

    tm_in, tps_in = (2 * tm, tps // 2) if tps % 2 == 0 else (tm, tps)
    h, zm, zglu, zgate, u0 = _fwd_in(xf, sp["norm1_g"], mod3, win_p, tm_in, tps_in)
    q, k, v, qln, kvn = _mla_prep(zm, sp["q_latent_g"], sp["kv_latent_g"], gq, gk, tabs, wuq_p, wk_p, wv_p, tm_in,
                                  tps_in)
    attn, gathered = _attn_fwd(q, k, v, nseq, seq, tuple(late) if comm else ())
    if comm:
        w.update(_assemble(LATE, gathered))
    wo_p = jnp.pad(w["w_o_mla"].reshape(N_HEADS, V_HEAD, d), ((0, 0), (0, LANES - V_HEAD), (0, 0))).reshape(HW, d)
    x1, mixed, mpre, ya, yb, u1, u3 = _fwd_mix(attn, u0, zgate, xf, mod3, wo_p, cw, sp["conv_b"], sp["conv_ln_g"],
                                               sp["conv_ln_b"], w["w_pw_out"], w["w_out"], tm_in, tps_in)
    h2, a, r, dy, df, dgate2, loss_acc = _fwd_ffn(x1, tg, sp["norm2_g"], mod3, w["w_ff1"], w["w_ff2"], tm, tps)
    da, dx1, dmixed, dshift2, dscale2, dgate1, dg2 = _bwd_ffn(df, a, x1, dy, mixed, sp["norm2_g"], mod3,
                                                              w["w_ff2"], w["w_ff1"], tm, tps)
    gw = {
        "w_out": _tn_matmul(mpre, dmixed, "dw_out").reshape(N_SHARD, d // N_SHARD, d),
        "w_ff1": _tn_matmul(h2, da, "dw_ff1", N_SHARD),
        "w_ff2": _tn_matmul(r, df, "dw_ff2").reshape(N_SHARD, -1, d),
    }
    halves_a = [_pair_halves(gw[n]) for n in GROUP_A] if comm else []
    dya, dyb, dz, do, du1, dlng, dlnb, dcb, from_sibling = _bwd_mix(
        dmixed, zgate, ya, yb, u1, sp["conv_ln_g"], sp["conv_ln_b"], w["w_out"], wo_p, w["w_pw_out"], tm_in, tuple(halves_a))
    pair_a = _pair_sums(GROUP_A, halves_a, from_sibling)
    dz, dcw = _bwd_conv(dz, du1, u0, zglu, cw, tm_in, tps_in)
    gw["conv_w"] = dcw
    dq, dk, dv, land_a = _attn_bwd(q, k, v, do, nseq, seq, tuple(p[1] for p in pair_a))
    dz, _, _, dgq, dgk, dgql, dgkvl, dwq_p, dwk_p, dwv_p = _mla_bwd(
        dz, dq, dk, dv, zm, sp["q_latent_g"], sp["kv_latent_g"], gq, gk, tabs, wuq_p, wk_p, wv_p, tm_in, tps_in)
    dwkv = jnp.concatenate([dwk_p.reshape(KV_RANK, N_HEADS, LANES)[:, :, :QK_NOPE],
                            dwv_p.reshape(KV_RANK, N_HEADS, LANES)[:, :, :V_HEAD]], axis=2).reshape(KV_RANK, -1)
    dwo = _tn_matmul(attn, dya, "dw_o").reshape(N_HEADS, LANES, d)[:, :V_HEAD].reshape(MLA_WIDTH, d)
    gw["w_in"] = _unpad_win(_tn_matmul(h, dz, "dw_in"))
    gw["w_uq"] = _col_shards(_unpad_heads(dwq_p, QK_HEAD))
    gw["w_ukv"] = _col_shards(dwkv)
    gw["w_o_mla"] = _col_shards(dwo)
    gw["w_pw_out"] = _tn_matmul(u3, dyb, "dw_pw", N_SHARD)
    pair_b = []
    if comm:
        halves_b = [_pair_halves(gw[n]) for n in GROUP_B]
        pair_b = _pair_sums(GROUP_B, halves_b, _pair_swap(halves_b, "grad_pair_swap"))
    gx, dshift1, dscale1, dg1, land_b = _bwd_in(dz, xf, dx1, sp["norm1_g"], mod3, win_p, tm_in, tps_in,
                                                tuple(p[1] for p in pair_b))
    if comm:
        for n, p, l in zip(GROUP_A + GROUP_B, pair_a + pair_b, land_a + land_b):
            gw[n] = (p[0], l)
    gs = {
        "norm1_g": dg1, "q_latent_g": dgql, "kv_latent_g": dgkvl, "qk_norm_q_g": dgq, "qk_norm_k_g": dgk,
        "conv_b": dcb, "conv_ln_g": dlng, "conv_ln_b": dlnb, "norm2_g": dg2,
    }
    dmod = jnp.concatenate([dshift1, dscale1, dgate1, dshift2, dscale2, dgate2], axis=2).reshape(nseq, N_MOD * d)
    return loss_acc, gx.reshape(nseq, seq, d), dmod, gw, gs


def kernel(x, c, w_ada, b_ada, norm1_g, w_in, q_latent_g, w_uq, kv_latent_g, w_ukv, qk_norm_q_g, qk_norm_k_g, w_o_mla, conv_w, conv_b, conv_ln_g, conv_ln_b, w_pw_out, w_out, norm2_g, w_ff1, w_ff2, loss_target, m_w_ada, m_b_ada, m_norm1_g, m_w_in, m_q_latent_g, m_w_uq, m_kv_latent_g, m_w_ukv, m_qk_norm_q_g, m_qk_norm_k_g, m_w_o_mla, m_conv_w, m_conv_b, m_conv_ln_g, m_conv_ln_b, m_w_pw_out, m_w_out, m_norm2_g, m_w_ff1, m_w_ff2, v_w_ada, v_b_ada, v_norm1_g, v_w_in, v_q_latent_g, v_w_uq, v_kv_latent_g, v_w_ukv, v_qk_norm_q_g, v_qk_norm_k_g, v_w_o_mla, v_conv_w, v_conv_b, v_conv_ln_g, v_conv_ln_b, v_w_pw_out, v_w_out, v_norm2_g, v_w_ff1, v_w_ff2):
    given = dict(locals())
    wts = {n: given[n][0] for n in WEIGHTS}
    mom = {n: given["m_" + n][0] for n in WEIGHTS}
    var = {n: given["v_" + n][0] for n in WEIGHTS}
    vec = lambda a: a.reshape(1, -1)
    nseq, seq, d = x.shape
    ix, iy, ic = _place()
    shard = 2 * ix + iy

    half = lambda n: lax.dynamic_slice_in_dim(wts[n].astype(BF16), ic * (wts[n].shape[0] // 2), wts[n].shape[0] // 2,
                                              axis=0)
    gathered = _all_gather8([half(n) for n in EARLY] + [wts["conv_w"], c], "gather_weights")
    full = _assemble(EARLY, gathered)
    full["conv_w"] = _from_shards(gathered[-2][0::2], "conv_w")
    c_all = gathered[-1].reshape(8 * nseq, d)

    n_ada = wts["w_ada"].shape[1]
    b_sh = lax.dynamic_slice_in_dim(vec(wts["b_ada"]), shard * n_ada, n_ada, axis=1)
    mod_sh = _ada_mod(c_all, wts["w_ada"], b_sh)
    hb = 4 * nseq
    mod_blk = lax.dynamic_slice_in_dim(mod_sh, ic * hb, hb, axis=0)
    (mod_all,) = _all_gather8([mod_blk], "gather_mod")
    mod_mine = lax.dynamic_slice_in_dim(mod_all, (2 * iy + ic) * nseq, nseq, axis=1)
    mod = jnp.concatenate([lax.dynamic_index_in_dim(mod_mine, 2 * s + ix, axis=0, keepdims=False)
                           for s in range(N_SHARD)], axis=1)

    sp = {n: vec(wts[n]) for n in SMALL}
    loss_part, grad_x, dmod, gw, gs = _local_step(x, loss_target, mod, sp, full, [half(n) for n in LATE])

    own_c = jnp.stack([shard, ic]).astype(jnp.int32)
    mine_sum = {n: _add_chips(gw[n][0], gw[n][1], own_c, "chip_sum_" + n) for n in LARGE}
    few = tuple(n for n in BIG if n not in LARGE)
    mine_sum.update(zip(few, _add_chips_whole([gw[n][0] for n in few], [gw[n][1] for n in few], own_c, "chip_sum_small")))
    summed, parts = _pair_gather_and_all_gather8(
        [mine_sum[n] for n in BIG], [dmod, gw["conv_w"], loss_part] + [gs[n] for n in SMALL], "tail_exchange")

    dmod_all = parts[0].reshape(8 * nseq, N_MOD * d)
    dmod_sh = lax.dynamic_slice_in_dim(dmod_all, shard * n_ada, n_ada, axis=1)
    res = _ada_bwd(c_all, dmod_all, dmod_sh, parts[1:])
    grads = {"w_ada": res[0], "b_ada": res[1]}
    n_cw = wts["conv_w"].shape[1]
    grads["conv_w"] = lax.dynamic_slice_in_dim(res[2], shard * n_cw, n_cw, axis=1)[:CONV_W]
    loss = res[3][0, 0]
    for n, g in zip(SMALL, res[4:]):
        grads[n] = g
    for n, g in zip(BIG, summed):
        grads[n] = g.reshape(wts[n].shape)

    delta, new_m, new_v = {}, {}, {}
    for n in LARGE + ("w_ada",):
        if n == "w_in":
            res = _adamw(wts[n].T, grads[n].T, mom[n].T, var[n].T, "adamw_" + n)
            delta[n], new_m[n], new_v[n] = (a.T for a in res)
        else:
            delta[n], new_m[n], new_v[n] = _adamw(wts[n], grads[n], mom[n], var[n], "adamw_" + n)
    rest = ("b_ada", "conv_w") + SMALL + few
    as2d = lambda a: a if a.ndim == 2 else vec(a)
    res = _adamw_small(*[[as2d(t[n]) for n in rest] for t in (wts, grads, mom, var)])
    for dst, arrs in zip((delta, new_m, new_v), res):
        for n, a in zip(rest, arrs):
            dst[n] = a

    outs = [loss, grad_x]
    for group in (grads, delta, new_m, new_v):
        outs += [group[n].reshape(given[n].shape) for n in WEIGHTS]
    return tuple(outs)
```
